```python
import math
import jax, jax.numpy as jnp
from jax import lax
import numpy as np

D_MODEL = 1024
BATCH = 2
SEQ = 8192
DEPTH = 1

GRID_W = 64
CTX_LEN = 256
ATTN_WIDTH = 512
HYENA_WIDTH = 512
MIX_WIDTH = ATTN_WIDTH + HYENA_WIDTH
HEAD_DIM = 64
N_Q_HEADS = ATTN_WIDTH // HEAD_DIM
N_KV_HEADS = 2
Q_PER_KV = N_Q_HEADS // N_KV_HEADS
KV_WIDTH = N_KV_HEADS * HEAD_DIM
HYENA_GROUPS = 8
IN_WIDTH = ATTN_WIDTH + 2 * KV_WIDTH + 3 * HYENA_WIDTH
SHORT_CONV = 3
FILTER_EMB = 33
FILTER_ORDER = 64
DECAY_TARGET = 1e-2
FAST_DECAY_PCT = 0.3
SLOW_DECAY_PCT = 1.5
N_EXPERTS = 16
CAPACITY_FACTOR = 2
D_EXPERT = 2752
ROPE_THETA = 10000.0
Q_BLOCK = 128
EPS = 1e-6

kernel_name = "hybrid_hyena_gqa_ec_moe_dit_layer"

F32 = jnp.float32


def rms_norm(x, g):
    xf = x.astype(F32)
    y = xf * lax.rsqrt(jnp.mean(xf * xf, axis=-1, keepdims=True) + EPS)
    return (y * g.astype(F32)).astype(x.dtype)


def modulate(h, shift, scale):
    return h * (1.0 + scale) + shift


def axial_rope_tables(n):
    rows = n // GRID_W
    row_id, col_id = jnp.meshgrid(jnp.arange(rows, dtype=F32), jnp.arange(GRID_W, dtype=F32), indexing="ij")
    pos = jnp.stack([row_id.reshape(-1), col_id.reshape(-1)], axis=-1)
    quarter = HEAD_DIM // 4
    inv_freq = ROPE_THETA ** (-jnp.arange(quarter, dtype=F32) / quarter)
    ang = pos[..., None] * inv_freq
    return jnp.cos(ang), jnp.sin(ang)


def apply_axial_rope(x, cos, sin):
    B, n, H, d = x.shape
    xf = x.astype(F32).reshape(B, n, H, 2, d // 2)
    x1, x2 = xf[..., : d // 4], xf[..., d // 4:]
    c = cos[:, None]
    s = sin[:, None]
    out = jnp.concatenate([x1 * c - x2 * s, x1 * s + x2 * c], axis=-1)
    return out.reshape(B, n, H, d).astype(x.dtype)


def attend(q, k_all, v_all):
    s = jnp.einsum("bqgrd,bkgd->bgrqk", q, k_all, preferred_element_type=F32) * (HEAD_DIM ** -0.5)
    p = jax.nn.softmax(s, axis=-1).astype(v_all.dtype)
    return jnp.einsum("bgrqk,bkgd->bqgrd", p, v_all)


def latent_attention(q, k, v, kc, vc):
    B, S = q.shape[:2]
    k_all = jnp.concatenate([k, kc], axis=1)
    v_all = jnp.concatenate([v, vc], axis=1)
    nb = S // Q_BLOCK
    qb = q.reshape(B, nb, Q_BLOCK, N_KV_HEADS, Q_PER_KV, HEAD_DIM).transpose(1, 0, 2, 3, 4, 5)
    out = lax.map(lambda qblk: attend(qblk, k_all, v_all), qb)
    return out.transpose(1, 0, 2, 3, 4, 5).reshape(B, S, ATTN_WIDTH)


def context_attention(qc, kc, vc):
    B, L = qc.shape[:2]
    q = qc.reshape(B, L, N_KV_HEADS, Q_PER_KV, HEAD_DIM)
    return attend(q, kc, vc).reshape(B, L, ATTN_WIDTH)


def implicit_filter(L, w1, b1, w2, b2, w3, freq):
    t = jnp.linspace(0.0, 1.0, L, dtype=F32)[:, None]
    bands = (FILTER_EMB - 1) // 2
    w = 2.0 * math.pi * jnp.arange(L, dtype=F32) / L
    f = jnp.linspace(1e-4, bands - 1, bands, dtype=F32)
    fw = w[:, None] * f[None, :]
    z = jnp.concatenate([t, jnp.cos(fw), -jnp.sin(fw)], axis=-1)
    fr = freq.astype(F32)
    h = jnp.sin(fr * (z @ w1.astype(F32) + b1.astype(F32)))
    h = jnp.sin(fr * (h @ w2.astype(F32) + b2.astype(F32)))
    h = h @ w3.astype(F32)
    max_decay = math.log(DECAY_TARGET) / FAST_DECAY_PCT
    min_decay = math.log(DECAY_TARGET) / SLOW_DECAY_PCT
    deltas = jnp.tile(jnp.linspace(min_decay, max_decay, HYENA_WIDTH, dtype=F32), 2)
    h = h * jnp.exp(-t * jnp.abs(deltas))
    h_fwd, h_bwd = h[:, :HYENA_WIDTH], h[:, HYENA_WIDTH:]
    kern = jnp.concatenate([h_fwd, jnp.zeros((1, HYENA_WIDTH), F32), h_bwd[1:][::-1]], axis=0)
    return kern / jnp.sum(jnp.abs(kern), axis=0, keepdims=True)


def hyena_mixer(p, kern, conv_w, conv_b, bias):
    L = p.shape[1]
    C3 = p.shape[-1]
    pad = SHORT_CONV // 2
    pc = lax.conv_general_dilated(p, conv_w[:, None, :].astype(p.dtype), window_strides=(1,),
                                  padding=((pad, pad),), dimension_numbers=("NWC", "WIO", "NWC"),
                                  feature_group_count=C3) + conv_b
    x1, x2, v = jnp.split(pc, 3, axis=-1)
    u = (v * x1).astype(F32)
    U = jnp.fft.rfft(u, n=2 * L, axis=1)
    K = jnp.fft.rfft(kern, axis=0)
    y = jnp.fft.irfft(U * K[None], n=2 * L, axis=1)[:, :L]
    y = (y + u * bias.astype(F32)) * x2.astype(F32)
    return y.astype(p.dtype)


def expert_choice_ffn(h, w_router, w_gate, w_up, w_down):
    B, N, D = h.shape
    cap = CAPACITY_FACTOR * N // N_EXPERTS
    logits = jnp.einsum("bnd,de->ben", h, w_router, preferred_element_type=F32)
    aff = jax.nn.softmax(logits, axis=1)
    gates, idx = lax.top_k(aff, cap)
    xg = jax.vmap(lambda hb, ib: hb[ib])(h, idx)
    a = jnp.einsum("becd,edf->becf", xg, w_gate)
    u = jnp.einsum("becd,edf->becf", xg, w_up)
    y = jnp.einsum("becf,efd->becd", jax.nn.silu(a) * u, w_down)
    y = y * gates[..., None].astype(y.dtype)
    return jax.vmap(lambda ib, yb: jnp.zeros((N, D), yb.dtype).at[ib.reshape(-1)].add(yb.reshape(-1, D)))(idx, y)


def setup_inputs(seed: int = 0) -> dict:
    key = jax.random.key(seed)
    ks = jax.random.split(key, 25)
    nrm = jax.random.normal
    C3 = 3 * HYENA_WIDTH
    return {
        "x": nrm(ks[0], (BATCH, SEQ, D_MODEL), F32),
        "c": nrm(ks[1], (BATCH, D_MODEL), F32),
        "ctx": nrm(ks[2], (BATCH, CTX_LEN, D_MODEL), F32),
        "c_ctx": nrm(ks[3], (D_MODEL,), F32),
        "w_mod": nrm(ks[4], (DEPTH, D_MODEL, 6 * D_MODEL), F32) * (0.5 * D_MODEL ** -0.5),
        "b_mod": nrm(ks[5], (DEPTH, 6 * D_MODEL), F32) * 0.02,
        "norm1_g": 1.0 + 0.02 * nrm(ks[6], (DEPTH, D_MODEL), F32),
        "norm2_g": 1.0 + 0.02 * nrm(ks[7], (DEPTH, D_MODEL), F32),
        "w_in": nrm(ks[8], (DEPTH, D_MODEL, IN_WIDTH), F32) * D_MODEL ** -0.5,
        "w_out": nrm(ks[9], (DEPTH, MIX_WIDTH, D_MODEL), F32) * MIX_WIDTH ** -0.5,
        "q_norm_g": 1.0 + 0.02 * nrm(ks[10], (DEPTH, HEAD_DIM), F32),
        "k_norm_g": 1.0 + 0.02 * nrm(ks[11], (DEPTH, HEAD_DIM), F32),
        "conv_w": nrm(ks[12], (DEPTH, SHORT_CONV, C3), F32) * SHORT_CONV ** -0.5,
        "conv_b": nrm(ks[13], (DEPTH, C3), F32) * 0.02,
        "filt_w1": nrm(ks[14], (DEPTH, FILTER_EMB, FILTER_ORDER), F32) * FILTER_EMB ** -0.5,
        "filt_b1": nrm(ks[15], (DEPTH, FILTER_ORDER), F32) * 0.1,
        "filt_w2": nrm(ks[16], (DEPTH, FILTER_ORDER, FILTER_ORDER), F32) * FILTER_ORDER ** -0.5,
        "filt_b2": nrm(ks[17], (DEPTH, FILTER_ORDER), F32) * 0.1,
        "filt_w3": nrm(ks[18], (DEPTH, FILTER_ORDER, 2 * HYENA_WIDTH), F32) * FILTER_ORDER ** -0.5,
        "filt_freq": 1.0 + 0.1 * nrm(ks[19], (DEPTH, FILTER_ORDER), F32),
        "hyena_bias": nrm(ks[20], (DEPTH, HYENA_WIDTH), F32),
        "w_router": nrm(ks[21], (DEPTH, D_MODEL, N_EXPERTS), F32) * D_MODEL ** -0.5,
        "w_gate": nrm(ks[22], (DEPTH, N_EXPERTS, D_MODEL, D_EXPERT), F32) * D_MODEL ** -0.5,
        "w_up": nrm(ks[23], (DEPTH, N_EXPERTS, D_MODEL, D_EXPERT), F32) * D_MODEL ** -0.5,
        "w_down": nrm(ks[24], (DEPTH, N_EXPERTS, D_EXPERT, D_MODEL), F32) * D_EXPERT ** -0.5,
    }


def reference(x, c, ctx, c_ctx, w_mod, b_mod, norm1_g, norm2_g, w_in, w_out, q_norm_g, k_norm_g,
              conv_w, conv_b, filt_w1, filt_b1, filt_w2, filt_b2, filt_w3, filt_freq, hyena_bias,
              w_router, w_gate, w_up, w_down):
    B, S, _ = x.shape
    Lc = ctx.shape[1]
    cos, sin = axial_rope_tables(S)
    silu_c = jax.nn.silu(c)
    silu_cc = jax.nn.silu(c_ctx)
    splits = [ATTN_WIDTH, ATTN_WIDTH + KV_WIDTH, ATTN_WIDTH + 2 * KV_WIDTH]
    for l in range(DEPTH):
        last = l == DEPTH - 1
        mod = silu_c @ w_mod[l] + b_mod[l]
        mod_c = silu_cc @ w_mod[l] + b_mod[l]
        sh1, sc1, g1, sh2, sc2, g2 = jnp.split(mod[:, None, :], 6, axis=-1)
        csh1, csc1, cg1, csh2, csc2, cg2 = jnp.split(mod_c, 6)
        w_q, w_k, w_v, w_h = jnp.split(w_in[l], splits, axis=1)
        hx = modulate(rms_norm(x, norm1_g[l]), sh1, sc1)
        hc = modulate(rms_norm(ctx, norm1_g[l]), csh1, csc1)
        kc = rms_norm((hc @ w_k).reshape(B, Lc, N_KV_HEADS, HEAD_DIM), k_norm_g[l])
        vc = (hc @ w_v).reshape(B, Lc, N_KV_HEADS, HEAD_DIM)
        q = apply_axial_rope(rms_norm((hx @ w_q).reshape(B, S, N_Q_HEADS, HEAD_DIM), q_norm_g[l]), cos, sin)
        k = apply_axial_rope(rms_norm((hx @ w_k).reshape(B, S, N_KV_HEADS, HEAD_DIM), k_norm_g[l]), cos, sin)
        v = (hx @ w_v).reshape(B, S, N_KV_HEADS, HEAD_DIM)
        attn_x = latent_attention(q, k, v, kc, vc)
        kern_x = implicit_filter(S, filt_w1[l], filt_b1[l], filt_w2[l], filt_b2[l], filt_w3[l], filt_freq[l])
        hy_x = hyena_mixer(hx @ w_h, kern_x, conv_w[l], conv_b[l], hyena_bias[l])
        mix_x = jnp.concatenate([attn_x, hy_x], axis=-1) @ w_out[l]
        if not last:
            qc = rms_norm((hc @ w_q).reshape(B, Lc, N_Q_HEADS, HEAD_DIM), q_norm_g[l])
            attn_c = context_attention(qc, kc, vc)
            kern_c = implicit_filter(Lc, filt_w1[l], filt_b1[l], filt_w2[l], filt_b2[l], filt_w3[l], filt_freq[l])
            hy_c = hyena_mixer(hc @ w_h, kern_c, conv_w[l], conv_b[l], hyena_bias[l])
            ctx = ctx + cg1 * (jnp.concatenate([attn_c, hy_c], axis=-1) @ w_out[l])
            hc2 = modulate(rms_norm(ctx, norm2_g[l]), csh2, csc2)
            ctx = ctx + cg2 * expert_choice_ffn(hc2, w_router[l], w_gate[l], w_up[l], w_down[l])
        x = x + g1 * mix_x
        hx2 = modulate(rms_norm(x, norm2_g[l]), sh2, sc2)
        x = x + g2 * expert_choice_ffn(hx2, w_router[l], w_gate[l], w_up[l], w_down[l])
    return x
```

```python
import functools
import math

import numpy as np
import jax
import jax.numpy as jnp
from jax import lax
from jax.experimental import pallas as pl
from jax.experimental.pallas import tpu as pltpu

F32 = jnp.float32
BF16 = jnp.bfloat16
I32 = jnp.int32

D = 1024
SEQ = 8192
CTX = 256
GRID_W = 64
ATT_W = 512
HY_W = 512
HD = 64
NQ = 8
NKV = 2
QPK = NQ // NKV
KV_W = NKV * HD
IN_W = ATT_W + 2 * KV_W + 3 * HY_W
FEMB = 33
FORD = 64
NE = 16
CAP = 2 * SEQ // NE
DEXP = 2752
ROPE_THETA = 10000.0
EPS = 1e-6
DECAY_TARGET = 1e-2
FAST_DECAY_PCT = 0.3
SLOW_DECAY_PCT = 1.5

LANES = 128
SUBLANES = 8
BF16_ROWS = 16
VMEM_BYTES_V7X = 64 * 1024 * 1024
VMEM_LIMIT = VMEM_BYTES_V7X - 8 * 1024 * 1024

FN = 2 * SEQ
FN1 = 128
FN2 = 128

TCH = LANES
NTCH = SEQ // TCH
GW = TCH + SUBLANES
CW = TCH + BF16_ROWS
YROWS = CAP + BF16_ROWS


def _cparams(sem, vmem=None):
    return pltpu.CompilerParams(dimension_semantics=sem, vmem_limit_bytes=vmem or VMEM_LIMIT)


def _split(a):
    hi = a.astype(BF16)
    lo = (a - hi.astype(F32)).astype(BF16)
    return hi, lo


_NN = (((1,), (0,)), ((), ()))
_NT = (((1,), (1,)), ((), ()))
_TN = (((0,), (0,)), ((), ()))


def _dot(a, b, dn=_NN):
    return lax.dot_general(a, b, dn, preferred_element_type=F32)


def _dot3(a, b, dn=_NN):
    ah, al = _split(a)
    bh, bl = _split(b)
    return _dot(ah, bh, dn) + _dot(ah, bl, dn) + _dot(al, bh, dn)


def _dot3c(fh, fl, z):
    zh, zl = _split(z)
    return _dot(fh, zh) + _dot(fh, zl) + _dot(fl, zh)


def _mod_body(c_ref, w_ref, b_ref, o_ref):
    c = c_ref[...]
    s = c * (1.0 / (1.0 + jnp.exp(-c)))
    o_ref[...] = _dot3(s, w_ref[...]) + b_ref[...]


def _modulation(cc, w_mod, b_mod):
    n = w_mod.shape[1]
    return pl.pallas_call(
        _mod_body,
        grid=(n // D,),
        in_specs=[pl.BlockSpec((SUBLANES, D), lambda j: (0, 0)),
                  pl.BlockSpec((D, D), lambda j: (0, j)),
                  pl.BlockSpec((1, D), lambda j: (0, j))],
        out_specs=pl.BlockSpec((SUBLANES, D), lambda j: (0, j)),
        out_shape=jax.ShapeDtypeStruct((SUBLANES, n), F32),
        compiler_params=_cparams(("arbitrary",)),
        name="modulation",
    )(cc, w_mod, b_mod)


def _rms_mod(x, g, sh, sc):
    ms = jnp.mean(x * x, axis=-1, keepdims=True)
    return (x * lax.rsqrt(ms + EPS) * g) * (1.0 + sc) + sh


def _head_norm_rope(t, g, bd, cos, sin):
    sq = t * t
    hi, lo = _split(sq)
    ms = _dot(hi, bd) + _dot(lo, bd)
    tn = t * lax.rsqrt(ms + EPS) * g
    lane = lax.broadcasted_iota(I32, tn.shape, 1)
    sw = jnp.where((lane & 31) < 16, pltpu.roll(tn, LANES - 16, 1), pltpu.roll(tn, 16, 1))
    return tn * cos + sw * sin


def _proj_body(x_ref, g_ref, sh_ref, sc_ref, w_ref, gq_ref, gk_ref, bd_ref, cos_ref, sin_ref,
               q_ref, k_ref, v_ref, p_ref):
    h = _rms_mod(x_ref[0], g_ref[...], sh_ref[0], sc_ref[0])
    proj = _dot(h.astype(BF16), w_ref[...])
    bd = bd_ref[...]
    cos = cos_ref[...]
    sin = sin_ref[...]
    for j in range(ATT_W // LANES):
        sl = slice(j * LANES, (j + 1) * LANES)
        qj = _head_norm_rope(proj[:, sl], gq_ref[...], bd, cos, sin)
        q_ref[0, :, sl] = (qj * (HD ** -0.5)).astype(BF16)
    k_ref[0] = _head_norm_rope(proj[:, ATT_W:ATT_W + KV_W], gk_ref[...], bd, cos, sin).astype(BF16)
    v_ref[0] = proj[:, ATT_W + KV_W:ATT_W + 2 * KV_W].astype(BF16)
    p_ref[0] = proj[:, ATT_W + 2 * KV_W:]


def _in_projection(x, g1, sh, sc, w_in_bf, gq2, gk2, bd, cos_t, sin_t, tm):
    b, s, _ = x.shape
    row = lambda bi, i: (bi, 0, 0)
    tok = lambda bi, i: (bi, i, 0)
    const = lambda bi, i: (0, 0)
    return pl.pallas_call(
        _proj_body,
        grid=(b, s // tm),
        in_specs=[pl.BlockSpec((1, tm, D), tok),
                  pl.BlockSpec((1, D), const),
                  pl.BlockSpec((1, 1, D), row),
                  pl.BlockSpec((1, 1, D), row),
                  pl.BlockSpec((D, IN_W), const),
                  pl.BlockSpec((1, LANES), const),
                  pl.BlockSpec((1, LANES), const),
                  pl.BlockSpec((LANES, LANES), const),
                  pl.BlockSpec((tm, LANES), lambda bi, i: (i, 0)),
                  pl.BlockSpec((tm, LANES), lambda bi, i: (i, 0))],
        out_specs=[pl.BlockSpec((1, tm, ATT_W), tok),
                   pl.BlockSpec((1, tm, KV_W), tok),
                   pl.BlockSpec((1, tm, KV_W), tok),
                   pl.BlockSpec((1, tm, 3 * HY_W), tok)],
        out_shape=[jax.ShapeDtypeStruct((b, s, ATT_W), BF16),
                   jax.ShapeDtypeStruct((b, s, KV_W), BF16),
                   jax.ShapeDtypeStruct((b, s, KV_W), BF16),
                   jax.ShapeDtypeStruct((b, s, 3 * HY_W), F32)],
        compiler_params=_cparams(("parallel", "parallel")),
        name="in_projection",
    )(x, g1, sh, sc, w_in_bf, gq2, gk2, bd, cos_t, sin_t)


ATT_TQ = 256
ATT_TK = 768
SK = SEQ + CTX
ATT_NCH = SK // ATT_TK


def _attn_body(q_ref, k_ref, vt_ref, o_ref):
    for r in range(QPK):
        qr = q_ref[0, :, r * HD:(r + 1) * HD]

        def chunk(c, carry):
            m, l, acc = carry
            s = _dot(k_ref[0, 0, c], qr, _NT)
            m_new = jnp.maximum(m, jnp.max(s, axis=0, keepdims=True))
            alpha = jnp.exp(m - m_new)
            p = jnp.exp(s - m_new)
            l = alpha * l + jnp.sum(p, axis=0, keepdims=True)
            acc = alpha * acc + _dot(vt_ref[0, 0, c], p.astype(BF16))
            return m_new, l, acc

        init = (jnp.full((1, ATT_TQ), -1e30, F32), jnp.zeros((1, ATT_TQ), F32), jnp.zeros((HD, ATT_TQ), F32))
        _, l, acc = lax.fori_loop(0, ATT_NCH, chunk, init)
        o_ref[0, :, r * HD:(r + 1) * HD] = (acc * (1.0 / l)).T.astype(BF16)


def _attention(q, kch, vtch):
    b = q.shape[0]
    return pl.pallas_call(
        _attn_body,
        grid=(b, NKV, SEQ // ATT_TQ),
        in_specs=[pl.BlockSpec((1, ATT_TQ, QPK * HD), lambda bi, g, i: (bi, i, g)),
                  pl.BlockSpec((1, 1, ATT_NCH, ATT_TK, HD), lambda bi, g, i: (bi, g, 0, 0, 0)),
                  pl.BlockSpec((1, 1, ATT_NCH, HD, ATT_TK), lambda bi, g, i: (bi, g, 0, 0, 0))],
        out_specs=pl.BlockSpec((1, ATT_TQ, QPK * HD), lambda bi, g, i: (bi, i, g)),
        out_shape=jax.ShapeDtypeStruct((b, SEQ, ATT_W), BF16),
        compiler_params=_cparams(("parallel", "parallel", "parallel")),
        name="attention",
    )(q, kch, vtch)


SC_TM = 1024


def _sconv_body(m1, a1, n1, m2, a2, n2, m3, a3, n3, w_ref, b_ref, u_ref, x2_ref):
    i = pl.program_id(1)
    last = pl.num_programs(1) - 1
    rows = lax.broadcasted_iota(I32, (SC_TM, HY_W), 0)

    def conv(main, prev, nxt, g):
        x = main[0]
        pr = jnp.where(i > 0, prev[0, SUBLANES - 1:SUBLANES, :], 0.0)
        nx = jnp.where(i < last, nxt[0, 0:1, :], 0.0)
        xm = jnp.where(rows == 0, pr, pltpu.roll(x, 1, 0))
        xp = jnp.where(rows == SC_TM - 1, nx, pltpu.roll(x, SC_TM - 1, 0))
        return (w_ref[g:g + 1, :] * xm + w_ref[3 + g:4 + g, :] * x + w_ref[6 + g:7 + g, :] * xp
                + b_ref[g:g + 1, :])

    x1 = conv(m1, a1, n1, 0)
    x2 = conv(m2, a2, n2, 1)
    v = conv(m3, a3, n3, 2)
    u_ref[0] = v * x1
    x2_ref[0] = x2


def _short_conv(p, cw9, cb3):
    b = p.shape[0]
    nblk8 = SEQ // SUBLANES
    step8 = SC_TM // SUBLANES
    specs = []
    for g in range(3):
        specs += [pl.BlockSpec((1, SC_TM, HY_W), lambda bi, i, g=g: (bi, i, g)),
                  pl.BlockSpec((1, SUBLANES, HY_W), lambda bi, i, g=g: (bi, jnp.maximum(i * step8 - 1, 0), g)),
                  pl.BlockSpec((1, SUBLANES, HY_W), lambda bi, i, g=g: (bi, jnp.minimum((i + 1) * step8, nblk8 - 1), g))]
    specs += [pl.BlockSpec((9, HY_W), lambda bi, i: (0, 0)), pl.BlockSpec((3, HY_W), lambda bi, i: (0, 0))]
    out = pl.BlockSpec((1, SC_TM, HY_W), lambda bi, i: (bi, i, 0))
    return pl.pallas_call(
        _sconv_body,
        grid=(b, SEQ // SC_TM),
        in_specs=specs,
        out_specs=[out, out],
        out_shape=[jax.ShapeDtypeStruct((b, SEQ, HY_W), F32)] * 2,
        compiler_params=_cparams(("parallel", "parallel")),
        name="short_conv",
    )(p, p, p, p, p, p, p, p, p, cw9, cb3)


FILT_TR = 1024


def _filter_body(z_ref, t_ref, msk_ref, w1_ref, b1_ref, w2_ref, b2_ref, w3_ref, fr_ref, dl_ref,
                 k_ref, s_ref):
    fr = fr_ref[...]
    h = jnp.sin(fr * (_dot3(z_ref[...], w1_ref[...]) + b1_ref[...]))
    h = jnp.sin(fr * (_dot3(h, w2_ref[...]) + b2_ref[...]))
    h = _dot3(h, w3_ref[...])
    kern = h * jnp.exp(t_ref[...] * dl_ref[...]) * msk_ref[...]
    k_ref[...] = kern

    @pl.when((pl.program_id(0) == 0) & (pl.program_id(1) == 0))
    def _():
        s_ref[...] = jnp.zeros_like(s_ref)

    s_ref[...] += jnp.sum(jnp.abs(kern), axis=0, keepdims=True)


def _implicit_filter(ztab, ttab, mtab, w1p, b1, w2, b2, w3, freq, negdelta):
    nt = SEQ // FILT_TR
    rowblk = lambda hf, i: (hf * nt + i, 0)
    const = lambda hf, i: (0, 0)
    return pl.pallas_call(
        _filter_body,
        grid=(2, nt),
        in_specs=[pl.BlockSpec((FILT_TR, FORD), rowblk),
                  pl.BlockSpec((FILT_TR, 1), rowblk),
                  pl.BlockSpec((FILT_TR, 1), rowblk),
                  pl.BlockSpec((FORD, FORD), const),
                  pl.BlockSpec((1, FORD), const),
                  pl.BlockSpec((FORD, FORD), const),
                  pl.BlockSpec((1, FORD), const),
                  pl.BlockSpec((FORD, HY_W), lambda hf, i: (0, hf)),
                  pl.BlockSpec((1, FORD), const),
                  pl.BlockSpec((1, HY_W), const)],
        out_specs=[pl.BlockSpec((FILT_TR, HY_W), rowblk),
                   pl.BlockSpec((1, HY_W), const)],
        out_shape=[jax.ShapeDtypeStruct((FN, HY_W), F32), jax.ShapeDtypeStruct((1, HY_W), F32)],
        compiler_params=_cparams(("arbitrary", "arbitrary")),
        name="implicit_filter",
    )(ztab, ttab, mtab, w1p, b1, w2, b2, w3, freq, negdelta)


DFT_LT = 4096


def _dft_lead_body(z_ref, fh_ref, fl_ref, o_ref):
    nb = z_ref.shape[0]
    z = jnp.concatenate([z_ref[j] for j in range(nb)], axis=0) if nb > 1 else z_ref[0]
    o_ref[...] = _dot3c(fh_ref[...], fl_ref[...], z)


def _dft_lead(z, fh, fl):
    nb, rows, lanes = z.shape
    m = fh.shape[0]
    return pl.pallas_call(
        _dft_lead_body,
        grid=(lanes // DFT_LT,),
        in_specs=[pl.BlockSpec((nb, rows, DFT_LT), lambda j: (0, 0, j)),
                  pl.BlockSpec(fh.shape, lambda j: (0, 0)),
                  pl.BlockSpec(fl.shape, lambda j: (0, 0))],
        out_specs=pl.BlockSpec((m, DFT_LT), lambda j: (0, j)),
        out_shape=jax.ShapeDtypeStruct((m, lanes), F32),
        compiler_params=_cparams(("parallel",)),
        name="dft_lead",
    )(z, fh, fl)


DFT_KB = 4


def _twiddle(tr_ref, ti_ref, j):
    reps = HY_W // LANES
    tr = jnp.concatenate([tr_ref[j]] * reps, axis=1)
    ti = jnp.concatenate([ti_ref[j]] * reps, axis=1)
    return tr, ti


def _spectrum_body(a_ref, tr_ref, ti_ref, fh_ref, fl_ref, sc_ref, o_ref):
    for j in range(DFT_KB):
        ar, ai = a_ref[0, j], a_ref[1, j]
        tr, ti = _twiddle(tr_ref, ti_ref, j)
        b = jnp.concatenate([ar * tr - ai * ti, ar * ti + ai * tr], axis=0)
        x = _dot3c(fh_ref[...], fl_ref[...], b) * sc_ref[...]
        o_ref[0, j] = x[:FN2]
        o_ref[1, j] = x[FN2:]


def _conv_mid_body(a_ref, tr_ref, ti_ref, fh_ref, fl_ref, gh_ref, gl_ref, k_ref, o_ref):
    for j in range(DFT_KB):
        ar, ai = a_ref[0, j], a_ref[1, j]
        tr, ti = _twiddle(tr_ref, ti_ref, j)
        b = jnp.concatenate([ar * tr - ai * ti, ar * ti + ai * tr], axis=0)
        x = _dot3c(fh_ref[...], fl_ref[...], b)
        xr, xi = x[:FN2], x[FN2:]
        kr, ki = k_ref[0, j], k_ref[1, j]
        y = jnp.concatenate([xr * kr - xi * ki, xr * ki + xi * kr], axis=0)
        c = _dot3c(gh_ref[...], gl_ref[...], y)
        cr, ci = c[:FN2], c[FN2:]
        o_ref[0, j] = cr * tr + ci * ti
        o_ref[1, j] = ci * tr - cr * ti


def _dft_mid_specs():
    slab = pl.BlockSpec((2, DFT_KB, FN2, HY_W), lambda i: (0, i, 0, 0))
    tw = pl.BlockSpec((DFT_KB, FN2, LANES), lambda i: (i, 0, 0))
    mat = pl.BlockSpec((2 * FN2, 2 * FN2), lambda i: (0, 0))
    return slab, tw, mat


def _filter_spectrum(a4, twr, twi, fh, fl, scale):
    slab, tw, mat = _dft_mid_specs()
    return pl.pallas_call(
        _spectrum_body,
        grid=(FN1 // DFT_KB,),
        in_specs=[slab, tw, tw, mat, mat, pl.BlockSpec((1, HY_W), lambda i: (0, 0))],
        out_specs=slab,
        out_shape=jax.ShapeDtypeStruct((2, FN1, FN2, HY_W), F32),
        compiler_params=_cparams(("parallel",)),
        name="filter_spectrum",
    )(a4, twr, twi, fh, fl, scale)


def _conv_mid(a4, twr, twi, fh, fl, gh, gl, khat):
    slab, tw, mat = _dft_mid_specs()
    return pl.pallas_call(
        _conv_mid_body,
        grid=(FN1 // DFT_KB,),
        in_specs=[slab, tw, tw, mat, mat, mat, mat, slab],
        out_specs=slab,
        out_shape=jax.ShapeDtypeStruct((2, FN1, FN2, HY_W), F32),
        compiler_params=_cparams(("parallel",)),
        name="conv_mid",
    )(a4, twr, twi, fh, fl, gh, gl, khat)


HY_LT = 2048


def _conv_out_body(d_ref, fh_ref, fl_ref, u_ref, x2_ref, bias_ref, o_ref):
    y = _dot3c(fh_ref[...], fl_ref[...], d_ref[...])
    half = SEQ // FN2
    bias = bias_ref[...]
    for b in range(2):
        yb = y[b * half:(b + 1) * half]
        o_ref[b] = ((yb + u_ref[b] * bias) * x2_ref[b]).astype(BF16)


def _conv_out(d2, fh, fl, u3, x23, bias_t):
    half = SEQ // FN2
    lanes = d2.shape[1]
    io = pl.BlockSpec((2, half, HY_LT), lambda j: (0, 0, j))
    return pl.pallas_call(
        _conv_out_body,
        grid=(lanes // HY_LT,),
        in_specs=[pl.BlockSpec((2 * FN1, HY_LT), lambda j: (0, j)),
                  pl.BlockSpec(fh.shape, lambda j: (0, 0)),
                  pl.BlockSpec(fl.shape, lambda j: (0, 0)),
                  io, io,
                  pl.BlockSpec((1, HY_LT), lambda j: (0, 0))],
        out_specs=io,
        out_shape=jax.ShapeDtypeStruct((2, half, lanes), BF16),
        compiler_params=_cparams(("parallel",)),
        name="conv_out",
    )(d2, fh, fl, u3, x23, bias_t)


OP_TM = 512


def _outproj_body(att_ref, hy_ref, x_ref, w_ref, g1_ref, n2_ref, sh_ref, sc_ref, wrh_ref, wrl_ref,
                  x1_ref, h2_ref, lg_ref):
    a = jnp.concatenate([att_ref[0], hy_ref[0]], axis=1)
    x1 = x_ref[0] + g1_ref[0] * _dot(a, w_ref[...])
    x1_ref[0] = x1
    h2 = _rms_mod(x1, n2_ref[...], sh_ref[0], sc_ref[0])
    hh, hl = _split(h2)
    h2_ref[0] = hh
    wrh = wrh_ref[...]
    lg_ref[0] = _dot(wrh, hh, _NT) + _dot(wrh, hl, _NT) + _dot(wrl_ref[...], hh, _NT)


def _out_projection(att, hy, x, w_out_bf, g1r, n2g, sh2, sc2, wrh, wrl):
    b = x.shape[0]
    tok = lambda bi, i: (bi, i, 0)
    row = lambda bi, i: (bi, 0, 0)
    const = lambda bi, i: (0, 0)
    return pl.pallas_call(
        _outproj_body,
        grid=(b, SEQ // OP_TM),
        in_specs=[pl.BlockSpec((1, OP_TM, ATT_W), tok),
                  pl.BlockSpec((1, OP_TM, HY_W), tok),
                  pl.BlockSpec((1, OP_TM, D), tok),
                  pl.BlockSpec((ATT_W + HY_W, D), const),
                  pl.BlockSpec((1, 1, D), row),
                  pl.BlockSpec((1, D), const),
                  pl.BlockSpec((1, 1, D), row),
                  pl.BlockSpec((1, 1, D), row),
                  pl.BlockSpec((NE, D), const),
                  pl.BlockSpec((NE, D), const)],
        out_specs=[pl.BlockSpec((1, OP_TM, D), tok),
                   pl.BlockSpec((1, OP_TM, D), tok),
                   pl.BlockSpec((1, NE, OP_TM), lambda bi, i: (bi, 0, i))],
        out_shape=[jax.ShapeDtypeStruct((b, SEQ, D), F32),
                   jax.ShapeDtypeStruct((b, SEQ, D), BF16),
                   jax.ShapeDtypeStruct((b, NE, SEQ), F32)],
        compiler_params=_cparams(("parallel", "parallel")),
        name="out_projection",
    )(att, hy, x, w_out_bf, g1r, n2g, sh2, sc2, wrh, wrl)


def _routing_body(lg_ref, tri_ref, pos_ref, gate_ref, off_ref, cs_ref):
    lg = lg_ref[0]
    e = jnp.exp(lg - jnp.max(lg, axis=0, keepdims=True))
    aff = e / jnp.sum(e, axis=0, keepdims=True)
    gate_ref[0] = aff
    def count_ge(t):
        return jnp.sum(jnp.where(aff >= t, 1.0, 0.0), axis=1, keepdims=True)

    def bisect(i, thr):
        cand = thr | (jnp.int32(1) << (30 - i))
        return jnp.where(count_ge(pltpu.bitcast(cand, F32)) >= float(CAP), cand, thr)

    thr = lax.fori_loop(0, 31, bisect, jnp.zeros((NE, 1), I32))
    lo = pltpu.bitcast(thr, F32)
    hi = jnp.maximum(pltpu.bitcast(thr + 1, F32), jnp.finfo(F32).tiny)

    def refine(i, c):
        lo, hi = c
        mid = lo + (hi - lo) * 0.5
        ok = count_ge(mid) >= float(CAP)
        return jnp.where(ok, mid, lo), jnp.where(ok, hi, mid)

    lo, hi = lax.fori_loop(0, 32, refine, (lo, hi))
    gt = aff >= hi
    eq = (aff >= lo) & jnp.logical_not(gt)
    need = float(CAP) - jnp.sum(jnp.where(gt, 1.0, 0.0), axis=1, keepdims=True)
    tri = tri_ref[...]

    def excl_cumsum(mask_f, record_offsets):
        carry = jnp.zeros((NE, 1), F32)
        for c in range(NTCH):
            sl = slice(c * TCH, (c + 1) * TCH)
            m = mask_f[:, sl]
            inc = _dot(m.astype(BF16), tri)
            cs_ref[:, sl] = inc - m + carry
            if record_offsets:
                off_ref[0, :, c:c + 1] = carry.astype(I32)
            carry = carry + inc[:, TCH - 1:TCH]
        return cs_ref[...]

    eq_rank = excl_cumsum(jnp.where(eq, 1.0, 0.0), False)
    sel = gt | (eq & (eq_rank < need))
    pos = excl_cumsum(jnp.where(sel, 1.0, 0.0), True)
    pos_ref[0] = jnp.where(sel, pos.astype(I32), -1)


def _routing(logits, tri):
    b = logits.shape[0]
    blk = pl.BlockSpec((1, NE, SEQ), lambda bi: (bi, 0, 0))
    return pl.pallas_call(
        _routing_body,
        grid=(b,),
        in_specs=[blk, pl.BlockSpec((TCH, TCH), lambda bi: (0, 0))],
        out_specs=[blk, blk, pl.BlockSpec((1, NE, NTCH), lambda bi: (bi, 0, 0))],
        out_shape=[jax.ShapeDtypeStruct((b, NE, SEQ), I32),
                   jax.ShapeDtypeStruct((b, NE, SEQ), F32),
                   jax.ShapeDtypeStruct((b, NE, NTCH), I32)],
        scratch_shapes=[pltpu.VMEM((NE, SEQ), F32)],
        compiler_params=_cparams(("parallel",)),
        name="routing",
    )(logits, tri)


def _gather_body(off_ref, h_ref, pos_ref, xg_ref, acc_ref):
    b = pl.program_id(0)
    e = pl.program_id(1)
    acc_ref[...] = jnp.zeros_like(acc_ref)
    crow = lax.broadcasted_iota(I32, (GW, TCH), 0)

    def chunk(c, _):
        off = off_ref[(b * NE + e) * NTCH + c]
        base = pl.multiple_of(jnp.minimum((off >> 3) << 3, CAP - TCH), SUBLANES)
        t0 = pl.multiple_of(c * TCH, TCH)
        rel = pos_ref[0, 0, pl.ds(c, 1), :] - base
        onehot = jnp.where(crow == rel, 1.0, 0.0).astype(BF16)
        acc_ref[pl.ds(base, GW), :] += _dot(onehot, h_ref[0, pl.ds(t0, TCH), :])
        return 0

    lax.fori_loop(0, NTCH, chunk, 0)
    xg_ref[0, 0] = acc_ref[0:CAP, :].astype(BF16)


def _gather(offs_flat, h2, pos4):
    b = h2.shape[0]
    grid_spec = pltpu.PrefetchScalarGridSpec(
        num_scalar_prefetch=1,
        grid=(b, NE),
        in_specs=[pl.BlockSpec((1, SEQ, D), lambda bi, e, off: (bi, 0, 0)),
                  pl.BlockSpec((1, 1, NTCH, TCH), lambda bi, e, off: (bi, e, 0, 0))],
        out_specs=pl.BlockSpec((1, 1, CAP, D), lambda bi, e, off: (bi, e, 0, 0)),
        scratch_shapes=[pltpu.VMEM((CAP + SUBLANES, D), F32)],
    )
    return pl.pallas_call(
        _gather_body,
        grid_spec=grid_spec,
        out_shape=jax.ShapeDtypeStruct((b, NE, CAP, D), BF16),
        compiler_params=_cparams(("parallel", "arbitrary")),
        name="moe_gather",
    )(offs_flat, h2, pos4)


FFN_TM = 256


def _ffn_body(xg_ref, wg_ref, wu_ref, wd_ref, y_ref):
    wg = wg_ref[0]
    wu = wu_ref[0]
    wd = wd_ref[0]
    for mb in range(CAP // FFN_TM):
        rows = slice(mb * FFN_TM, (mb + 1) * FFN_TM)
        xb = xg_ref[0, 0, rows, :]
        a = _dot(xb, wg)
        u = _dot(xb, wu)
        h = (a * (1.0 / (1.0 + jnp.exp(-a))) * u).astype(BF16)
        y_ref[0, 0, rows, :] = _dot(h, wd).astype(BF16)
    y_ref[0, 0, CAP:YROWS, :] = jnp.zeros((YROWS - CAP, D), BF16)


def _expert_ffn(xg, wg_bf, wu_bf, wd_bf):
    b = xg.shape[0]
    return pl.pallas_call(
        _ffn_body,
        grid=(NE, b),
        in_specs=[pl.BlockSpec((1, 1, CAP, D), lambda e, bi: (bi, e, 0, 0)),
                  pl.BlockSpec((1, D, DEXP), lambda e, bi: (e, 0, 0)),
                  pl.BlockSpec((1, D, DEXP), lambda e, bi: (e, 0, 0)),
                  pl.BlockSpec((1, DEXP, D), lambda e, bi: (e, 0, 0))],
        out_specs=pl.BlockSpec((1, 1, YROWS, D), lambda e, bi: (bi, e, 0, 0)),
        out_shape=jax.ShapeDtypeStruct((b, NE, YROWS, D), BF16),
        compiler_params=_cparams(("parallel", "arbitrary")),
        name="expert_ffn",
    )(xg, wg_bf, wu_bf, wd_bf)


def _combine_body(off_ref, y_ref, pos_ref, gate_ref, x1_ref, g2_ref, o_ref):
    b = pl.program_id(0)
    i = pl.program_id(1)
    crow = lax.broadcasted_iota(I32, (CW, TCH), 0)
    acc = jnp.zeros((TCH, D), F32)
    for e in range(NE):
        off = off_ref[(b * NE + e) * NTCH + i]
        base = pl.multiple_of(jnp.minimum((off >> 4) << 4, CAP - TCH), BF16_ROWS)
        rel = pos_ref[0, e:e + 1, :] - base
        w = jnp.where(crow == rel, gate_ref[0, e:e + 1, :], 0.0)
        wh, wl = _split(w)
        yw = y_ref[0, e, pl.ds(base, CW), :]
        acc = acc + _dot(wh, yw, _TN) + _dot(wl, yw, _TN)
    o_ref[0] = x1_ref[0] + g2_ref[0] * acc


def _combine(offs_flat, y, pos, gate, x1, g2r):
    b = x1.shape[0]
    grid_spec = pltpu.PrefetchScalarGridSpec(
        num_scalar_prefetch=1,
        grid=(b, NTCH),
        in_specs=[pl.BlockSpec((1, NE, YROWS, D), lambda bi, i, off: (bi, 0, 0, 0),
                               pipeline_mode=pl.Buffered(1)),
                  pl.BlockSpec((1, NE, TCH), lambda bi, i, off: (bi, 0, i)),
                  pl.BlockSpec((1, NE, TCH), lambda bi, i, off: (bi, 0, i)),
                  pl.BlockSpec((1, TCH, D), lambda bi, i, off: (bi, i, 0)),
                  pl.BlockSpec((1, 1, D), lambda bi, i, off: (bi, 0, 0))],
        out_specs=pl.BlockSpec((1, TCH, D), lambda bi, i, off: (bi, i, 0)),
    )
    return pl.pallas_call(
        _combine_body,
        grid_spec=grid_spec,
        out_shape=jax.ShapeDtypeStruct((b, SEQ, D), F32),
        compiler_params=_cparams(("parallel", "arbitrary")),
        name="moe_combine",
    )(offs_flat, y, pos, gate, x1, g2r)


def _np_split(m):
    hi = np.asarray(m, np.float64).astype(BF16)
    lo = (m - hi.astype(np.float64)).astype(BF16)
    return jnp.asarray(hi), jnp.asarray(lo)


@functools.lru_cache(maxsize=None)
def _dft_tables():
    a = np.arange(FN1, dtype=np.float64)
    ang = 2.0 * np.pi * np.outer(a, a) / FN1
    fr, fi = np.cos(ang), -np.sin(ang)
    half = SEQ // FN2
    lead_u = np.block([[fr[:, :half], -fi[:, :half]], [fi[:, :half], fr[:, :half]]])
    lead_k = np.concatenate([fr, fi], axis=0)
    fwd = np.block([[fr, -fi], [fi, fr]])
    inv = np.block([[fr, fi], [-fi, fr]])
    out = np.block([[fr[:half], fi[:half]], [-fi[:half], fr[:half]]])
    n2 = np.arange(FN2, dtype=np.float64)
    tw = 2.0 * np.pi * np.outer(a, n2) / FN
    twr = np.broadcast_to(np.cos(tw)[:, :, None], (FN1, FN2, LANES)).astype(np.float32)
    twi = np.broadcast_to(-np.sin(tw)[:, :, None], (FN1, FN2, LANES)).astype(np.float32)
    return dict(lead_u=lead_u, lead_k=lead_k, fwd=fwd, inv=inv, out=out, twr=twr, twi=twi)


def _filter_tables():
    L = SEQ
    r = jnp.arange(FN)
    j = jnp.where(r < L, r, FN - r)
    jc = jnp.minimum(j, L - 1)
    t = jnp.linspace(0.0, 1.0, L, dtype=F32)[jc][:, None]
    bands = (FEMB - 1) // 2
    w = (2.0 * math.pi * jnp.arange(L, dtype=F32) / L)[jc]
    f = jnp.linspace(1e-4, bands - 1, bands, dtype=F32)
    fw = w[:, None] * f[None, :]
    z = jnp.concatenate([t, jnp.cos(fw), -jnp.sin(fw), jnp.zeros((FN, FORD - FEMB), F32)], axis=-1)
    mask = jnp.where(r == L, 0.0, 1.0).astype(F32)[:, None]
    max_decay = math.log(DECAY_TARGET) / FAST_DECAY_PCT
    min_decay = math.log(DECAY_TARGET) / SLOW_DECAY_PCT
    deltas = jnp.linspace(min_decay, max_decay, HY_W, dtype=F32)
    return z, t, mask, (-jnp.abs(deltas))[None, :]


def _rope_tables(n):
    rows = n // GRID_W
    row_id, col_id = jnp.meshgrid(jnp.arange(rows, dtype=F32), jnp.arange(GRID_W, dtype=F32), indexing="ij")
    quarter = HD // 4
    inv_freq = ROPE_THETA ** (-jnp.arange(quarter, dtype=F32) / quarter)
    ar = row_id.reshape(-1)[:, None] * inv_freq
    ac = col_id.reshape(-1)[:, None] * inv_freq
    cos = jnp.concatenate([jnp.cos(ar), jnp.cos(ar), jnp.cos(ac), jnp.cos(ac)], axis=-1)
    sin = jnp.concatenate([-jnp.sin(ar), jnp.sin(ar), -jnp.sin(ac), jnp.sin(ac)], axis=-1)
    return jnp.tile(cos, (1, LANES // HD)), jnp.tile(sin, (1, LANES // HD))


def _hyena_long_conv(u, x2c, kern, abs_sum, bias):
    tb = _dft_tables()
    half = SEQ // FN2
    lanes = FN2 * HY_W
    twr, twi = jnp.asarray(tb["twr"]), jnp.asarray(tb["twi"])
    fwd_h, fwd_l = _np_split(tb["fwd"])
    inv_h, inv_l = _np_split(tb["inv"])
    ak = _dft_lead(kern.reshape(1, FN1, lanes), *_np_split(tb["lead_k"]))
    scale = 1.0 / (abs_sum * float(FN))
    khat = _filter_spectrum(ak.reshape(2, FN1, FN2, HY_W), twr, twi, fwd_h, fwd_l, scale)
    u3 = u.reshape(2, half, lanes)
    au = _dft_lead(u3, *_np_split(tb["lead_u"]))
    dd = _conv_mid(au.reshape(2, FN1, FN2, HY_W), twr, twi, fwd_h, fwd_l, inv_h, inv_l, khat)
    bias_t = jnp.tile(bias.reshape(1, HY_W), (1, HY_LT // HY_W))
    hy = _conv_out(dd.reshape(2 * FN1, lanes), *_np_split(tb["out"]), u3, x2c.reshape(2, half, lanes), bias_t)
    return hy.reshape(2, SEQ, HY_W)


def kernel(x, c, ctx, c_ctx, w_mod, b_mod, norm1_g, norm2_g, w_in, w_out, q_norm_g, k_norm_g,
           conv_w, conv_b, filt_w1, filt_b1, filt_w2, filt_b2, filt_w3, filt_freq, hyena_bias,
           w_router, w_gate, w_up, w_down):
    B = x.shape[0]
    assert x.shape == (B, SEQ, D) and B == 2 and ctx.shape == (B, CTX, D) and w_mod.shape[0] == 1
    l = 0

    cc = jnp.concatenate([c, c_ctx[None, :], jnp.zeros((SUBLANES - B - 1, D), F32)], axis=0)
    mod = _modulation(cc, w_mod[l], b_mod[l][None, :])
    sh1, sc1, g1, sh2, sc2, g2 = [mod[:, i * D:(i + 1) * D] for i in range(6)]
    lat = lambda m: m[:B, None, :]
    ctxrow = lambda m: jnp.broadcast_to(m[B:B + 1, None, :], (B, 1, D))

    w_in_bf = w_in[l].astype(BF16)
    gq2 = jnp.tile(q_norm_g[l][None, :], (1, LANES // HD))
    gk2 = jnp.tile(k_norm_g[l][None, :], (1, LANES // HD))
    bd = jnp.asarray(np.kron(np.eye(LANES // HD), np.full((HD, HD), 1.0 / HD)), BF16)
    cos_t, sin_t = _rope_tables(SEQ)
    n1g = norm1_g[l][None, :]

    q, k, v, p = _in_projection(x, n1g, lat(sh1), lat(sc1), w_in_bf, gq2, gk2, bd, cos_t, sin_t, 512)
    _, kc, vc, _ = _in_projection(ctx, n1g, ctxrow(sh1), ctxrow(sc1), w_in_bf, gq2, gk2, bd,
                                  jnp.ones((CTX, LANES), F32), jnp.zeros((CTX, LANES), F32), CTX)

    k_all = jnp.concatenate([k, kc], axis=1).reshape(B, ATT_NCH, ATT_TK, NKV, HD)
    v_all = jnp.concatenate([v, vc], axis=1).reshape(B, ATT_NCH, ATT_TK, NKV, HD)
    kch = k_all.transpose(0, 3, 1, 2, 4)
    vtch = v_all.transpose(0, 3, 1, 4, 2)
    att = _attention(q, kch, vtch)

    cw9 = conv_w[l].reshape(3, 3, HY_W).reshape(9, HY_W)
    cb3 = conv_b[l].reshape(3, HY_W)
    u, x2c = _short_conv(p, cw9, cb3)
    ztab, ttab, mtab, negdelta = _filter_tables()
    w1p = jnp.concatenate([filt_w1[l], jnp.zeros((FORD - FEMB, FORD), F32)], axis=0)
    kern, abs_sum = _implicit_filter(ztab, ttab, mtab, w1p, filt_b1[l][None, :], filt_w2[l],
                                     filt_b2[l][None, :], filt_w3[l], filt_freq[l][None, :], negdelta)
    hy = _hyena_long_conv(u, x2c, kern, abs_sum, hyena_bias[l])

    wrh, wrl = _split(w_router[l].T)
    x1, h2, logits = _out_projection(att, hy, x, w_out[l].astype(BF16), lat(g1), norm2_g[l][None, :],
                                     lat(sh2), lat(sc2), wrh, wrl)

    tri = jnp.asarray(np.triu(np.ones((TCH, TCH))), BF16)
    pos, gate, offs = _routing(logits, tri)
    offs_flat = offs.reshape(-1)
    xg = _gather(offs_flat, h2, pos.reshape(B, NE, NTCH, TCH))
    y = _expert_ffn(xg, w_gate[l].astype(BF16), w_up[l].astype(BF16), w_down[l].astype(BF16))
    return _combine(offs_flat, y, pos, gate, x1, lat(g2))
```

```python
import functools
import math

import numpy as np
import jax
import jax.numpy as jnp
from jax import lax
from jax.experimental import pallas as pl
from jax.experimental.pallas import tpu as pltpu

F32 = jnp.float32
BF16 = jnp.bfloat16
I32 = jnp.int32

D = 1024
SEQ = 8192
CTX = 256
GRID_W = 64
ATT_W = 512
HY_W = 512
HD = 64
NQ = 8
NKV = 2
QPK = NQ // NKV
KV_W = NKV * HD
IN_W = ATT_W + 2 * KV_W + 3 * HY_W
FEMB = 33
FORD = 64
NE = 16
CAP = 2 * SEQ // NE
DEXP = 2752
ROPE_THETA = 10000.0
EPS = 1e-6
DECAY_TARGET = 1e-2
FAST_DECAY_PCT = 0.3
SLOW_DECAY_PCT = 1.5

LANES = 128
SUBLANES = 8
BF16_ROWS = 16
VMEM_BYTES_V7X = 64 * 1024 * 1024
VMEM_LIMIT = VMEM_BYTES_V7X - 8 * 1024 * 1024

FN = 2 * SEQ
FN1 = 128
FN2 = 128

TCH = LANES
NTCH = SEQ // TCH
GW = TCH + SUBLANES
CW = TCH + BF16_ROWS
YROWS = CAP + BF16_ROWS


def _cparams(sem, vmem=None):
    return pltpu.CompilerParams(dimension_semantics=sem, vmem_limit_bytes=vmem or VMEM_LIMIT)


def _split(a):
    hi = a.astype(BF16)
    lo = (a - hi.astype(F32)).astype(BF16)
    return hi, lo


_NN = (((1,), (0,)), ((), ()))
_NT = (((1,), (1,)), ((), ()))
_TN = (((0,), (0,)), ((), ()))


def _dot(a, b, dn=_NN):
    return lax.dot_general(a, b, dn, preferred_element_type=F32)


def _dot3(a, b, dn=_NN):
    ah, al = _split(a)
    bh, bl = _split(b)
    return _dot(ah, bh, dn) + _dot(ah, bl, dn) + _dot(al, bh, dn)


def _dot3c(fh, fl, z):
    zh, zl = _split(z)
    return _dot(fh, zh) + _dot(fh, zl) + _dot(fl, zh)


def _mod_body(c_ref, w_ref, b_ref, o_ref):
    c = c_ref[...]
    s = c * (1.0 / (1.0 + jnp.exp(-c)))
    o_ref[...] = _dot3(s, w_ref[...]) + b_ref[...]


def _modulation(cc, w_mod, b_mod):
    n = w_mod.shape[1]
    return pl.pallas_call(
        _mod_body,
        grid=(n // D,),
        in_specs=[pl.BlockSpec((SUBLANES, D), lambda j: (0, 0)),
                  pl.BlockSpec((D, D), lambda j: (0, j)),
                  pl.BlockSpec((1, D), lambda j: (0, j))],
        out_specs=pl.BlockSpec((SUBLANES, D), lambda j: (0, j)),
        out_shape=jax.ShapeDtypeStruct((SUBLANES, n), F32),
        compiler_params=_cparams(("arbitrary",)),
        name="modulation",
    )(cc, w_mod, b_mod)


Q_SCALE = HD ** -0.5 * math.log2(math.e)


def _rms_mod(x, g, sh, sc):
    ms = jnp.mean(x * x, axis=-1, keepdims=True)
    return (x * lax.rsqrt(ms + EPS) * g) * (1.0 + sc) + sh


def _head_norm_rope(t, g, bd, cos, sin):
    sq = t * t
    hi, lo = _split(sq)
    ms = _dot(hi, bd) + _dot(lo, bd)
    tn = t * lax.rsqrt(ms + EPS) * g
    lane = lax.broadcasted_iota(I32, tn.shape, 1)
    sw = jnp.where((lane & 31) < 16, pltpu.roll(tn, LANES - 16, 1), pltpu.roll(tn, 16, 1))
    return tn * cos + sw * sin


def _proj_body(x_ref, g_ref, sh_ref, sc_ref, w_ref, gq_ref, gk_ref, bd_ref, cos_ref, sin_ref,
               q_ref, k_ref, v_ref, p_ref):
    h = _rms_mod(x_ref[0], g_ref[...], sh_ref[0], sc_ref[0])
    proj = _dot(h.astype(BF16), w_ref[...])
    bd = bd_ref[...]
    cos = cos_ref[...]
    sin = sin_ref[...]
    for j in range(ATT_W // LANES):
        sl = slice(j * LANES, (j + 1) * LANES)
        qj = _head_norm_rope(proj[:, sl], gq_ref[...], bd, cos, sin)
        q_ref[0, :, sl] = (qj * Q_SCALE).astype(BF16)
    k_ref[0] = _head_norm_rope(proj[:, ATT_W:ATT_W + KV_W], gk_ref[...], bd, cos, sin).astype(BF16)
    v_ref[0] = proj[:, ATT_W + KV_W:ATT_W + 2 * KV_W].astype(BF16)
    p_ref[0] = proj[:, ATT_W + 2 * KV_W:]


def _in_projection(x, g1, sh, sc, w_in_bf, gq2, gk2, bd, cos_t, sin_t, tm):
    b, s, _ = x.shape
    row = lambda bi, i: (bi, 0, 0)
    tok = lambda bi, i: (bi, i, 0)
    const = lambda bi, i: (0, 0)
    return pl.pallas_call(
        _proj_body,
        grid=(b, s // tm),
        in_specs=[pl.BlockSpec((1, tm, D), tok),
                  pl.BlockSpec((1, D), const),
                  pl.BlockSpec((1, 1, D), row),
                  pl.BlockSpec((1, 1, D), row),
                  pl.BlockSpec((D, IN_W), const),
                  pl.BlockSpec((1, LANES), const),
                  pl.BlockSpec((1, LANES), const),
                  pl.BlockSpec((LANES, LANES), const),
                  pl.BlockSpec((tm, LANES), lambda bi, i: (i, 0)),
                  pl.BlockSpec((tm, LANES), lambda bi, i: (i, 0))],
        out_specs=[pl.BlockSpec((1, tm, ATT_W), tok),
                   pl.BlockSpec((1, tm, KV_W), tok),
                   pl.BlockSpec((1, tm, KV_W), tok),
                   pl.BlockSpec((1, tm, 3 * HY_W), tok)],
        out_shape=[jax.ShapeDtypeStruct((b, s, ATT_W), BF16),
                   jax.ShapeDtypeStruct((b, s, KV_W), BF16),
                   jax.ShapeDtypeStruct((b, s, KV_W), BF16),
                   jax.ShapeDtypeStruct((b, s, 3 * HY_W), F32)],
        compiler_params=_cparams(("parallel", "parallel")),
        name="in_projection",
    )(x, g1, sh, sc, w_in_bf, gq2, gk2, bd, cos_t, sin_t)


ATT_TQ = 256
ATT_TK = 768
SK = SEQ + CTX
ATT_NCH = SK // ATT_TK


ATT_NQ = QPK * ATT_TQ


def _attn_body(q_ref, k_ref, vt_ref, o_ref, m_ref, l_ref, acc_ref):
    qall = jnp.concatenate([q_ref[0, :, r * HD:(r + 1) * HD] for r in range(QPK)], axis=0)
    m_ref[...] = jnp.full(m_ref.shape, -1e30, F32)
    l_ref[...] = jnp.zeros_like(l_ref)
    acc_ref[...] = jnp.zeros_like(acc_ref)

    def chunk(c, _):
        s = _dot(k_ref[0, 0, c], qall, _NT)
        m_old = m_ref[...]
        m_new = jnp.maximum(m_old, jnp.max(s, axis=0, keepdims=True))
        alpha = jnp.exp2(m_old - m_new)
        p = jnp.exp2(s - m_new)
        l_ref[...] = alpha * l_ref[...] + jnp.sum(p, axis=0, keepdims=True)
        acc_ref[...] = alpha * acc_ref[...] + _dot(vt_ref[0, 0, c], p.astype(BF16))
        m_ref[...] = m_new
        return 0

    lax.fori_loop(0, ATT_NCH, chunk, 0)
    out = acc_ref[...] * (1.0 / l_ref[...])
    for r in range(QPK):
        o_ref[0, :, r * HD:(r + 1) * HD] = out[:, r * ATT_TQ:(r + 1) * ATT_TQ].T.astype(BF16)


def _attention(q, kch, vtch):
    b = q.shape[0]
    return pl.pallas_call(
        _attn_body,
        grid=(b, NKV, SEQ // ATT_TQ),
        in_specs=[pl.BlockSpec((1, ATT_TQ, QPK * HD), lambda bi, g, i: (bi, i, g)),
                  pl.BlockSpec((1, 1, ATT_NCH, ATT_TK, HD), lambda bi, g, i: (bi, g, 0, 0, 0)),
                  pl.BlockSpec((1, 1, ATT_NCH, HD, ATT_TK), lambda bi, g, i: (bi, g, 0, 0, 0))],
        out_specs=pl.BlockSpec((1, ATT_TQ, QPK * HD), lambda bi, g, i: (bi, i, g)),
        out_shape=jax.ShapeDtypeStruct((b, SEQ, ATT_W), BF16),
        scratch_shapes=[pltpu.VMEM((1, ATT_NQ), F32), pltpu.VMEM((1, ATT_NQ), F32),
                        pltpu.VMEM((HD, ATT_NQ), F32)],
        compiler_params=_cparams(("parallel", "parallel", "parallel")),
        name="attention",
    )(q, kch, vtch)


SC_TM = 1024


def _sconv_body(m1, a1, n1, m2, a2, n2, m3, a3, n3, w_ref, b_ref, u_ref, x2_ref):
    i = pl.program_id(1)
    last = pl.num_programs(1) - 1
    rows = lax.broadcasted_iota(I32, (SC_TM, HY_W), 0)

    def conv(main, prev, nxt, g):
        x = main[0]
        pr = jnp.where(i > 0, prev[0, SUBLANES - 1:SUBLANES, :], 0.0)
        nx = jnp.where(i < last, nxt[0, 0:1, :], 0.0)
        xm = jnp.where(rows == 0, pr, pltpu.roll(x, 1, 0))
        xp = jnp.where(rows == SC_TM - 1, nx, pltpu.roll(x, SC_TM - 1, 0))
        return (w_ref[g:g + 1, :] * xm + w_ref[3 + g:4 + g, :] * x + w_ref[6 + g:7 + g, :] * xp
                + b_ref[g:g + 1, :])

    x1 = conv(m1, a1, n1, 0)
    x2 = conv(m2, a2, n2, 1)
    v = conv(m3, a3, n3, 2)
    u_ref[0] = v * x1
    x2_ref[0] = x2


def _short_conv(p, cw9, cb3):
    b = p.shape[0]
    nblk8 = SEQ // SUBLANES
    step8 = SC_TM // SUBLANES
    specs = []
    for g in range(3):
        specs += [pl.BlockSpec((1, SC_TM, HY_W), lambda bi, i, g=g: (bi, i, g)),
                  pl.BlockSpec((1, SUBLANES, HY_W), lambda bi, i, g=g: (bi, jnp.maximum(i * step8 - 1, 0), g)),
                  pl.BlockSpec((1, SUBLANES, HY_W), lambda bi, i, g=g: (bi, jnp.minimum((i + 1) * step8, nblk8 - 1), g))]
    specs += [pl.BlockSpec((9, HY_W), lambda bi, i: (0, 0)), pl.BlockSpec((3, HY_W), lambda bi, i: (0, 0))]
    out = pl.BlockSpec((1, SC_TM, HY_W), lambda bi, i: (bi, i, 0))
    return pl.pallas_call(
        _sconv_body,
        grid=(b, SEQ // SC_TM),
        in_specs=specs,
        out_specs=[out, out],
        out_shape=[jax.ShapeDtypeStruct((b, SEQ, HY_W), F32)] * 2,
        compiler_params=_cparams(("parallel", "parallel")),
        name="short_conv",
    )(p, p, p, p, p, p, p, p, p, cw9, cb3)


FILT_TR = 1024


def _filter_body(z_ref, t_ref, msk_ref, w1_ref, b1_ref, w2_ref, b2_ref, w3_ref, fr_ref, dl_ref,
                 k_ref, s_ref):
    fr = fr_ref[...]
    h = jnp.sin(fr * (_dot3(z_ref[...], w1_ref[...]) + b1_ref[...]))
    h = jnp.sin(fr * (_dot3(h, w2_ref[...]) + b2_ref[...]))
    h = _dot3(h, w3_ref[...])
    kern = h * jnp.exp(t_ref[...] * dl_ref[...]) * msk_ref[...]
    k_ref[...] = kern

    @pl.when((pl.program_id(0) == 0) & (pl.program_id(1) == 0))
    def _():
        s_ref[...] = jnp.zeros_like(s_ref)

    s_ref[...] += jnp.sum(jnp.abs(kern), axis=0, keepdims=True)


def _implicit_filter(ztab, ttab, mtab, w1p, b1, w2, b2, w3, freq, negdelta):
    nt = SEQ // FILT_TR
    rowblk = lambda hf, i: (hf * nt + i, 0)
    const = lambda hf, i: (0, 0)
    return pl.pallas_call(
        _filter_body,
        grid=(2, nt),
        in_specs=[pl.BlockSpec((FILT_TR, FORD), rowblk),
                  pl.BlockSpec((FILT_TR, 1), rowblk),
                  pl.BlockSpec((FILT_TR, 1), rowblk),
                  pl.BlockSpec((FORD, FORD), const),
                  pl.BlockSpec((1, FORD), const),
                  pl.BlockSpec((FORD, FORD), const),
                  pl.BlockSpec((1, FORD), const),
                  pl.BlockSpec((FORD, HY_W), lambda hf, i: (0, hf)),
                  pl.BlockSpec((1, FORD), const),
                  pl.BlockSpec((1, HY_W), const)],
        out_specs=[pl.BlockSpec((FILT_TR, HY_W), rowblk),
                   pl.BlockSpec((1, HY_W), const)],
        out_shape=[jax.ShapeDtypeStruct((FN, HY_W), F32), jax.ShapeDtypeStruct((1, HY_W), F32)],
        compiler_params=_cparams(("arbitrary", "arbitrary")),
        name="implicit_filter",
    )(ztab, ttab, mtab, w1p, b1, w2, b2, w3, freq, negdelta)


DFT_LT = 4096


def _dft_lead_body(z_ref, fh_ref, fl_ref, o_ref):
    nb = z_ref.shape[0]
    z = jnp.concatenate([z_ref[j] for j in range(nb)], axis=0) if nb > 1 else z_ref[0]
    o_ref[...] = _dot3c(fh_ref[...], fl_ref[...], z)


def _dft_lead(z, fh, fl):
    nb, rows, lanes = z.shape
    m = fh.shape[0]
    return pl.pallas_call(
        _dft_lead_body,
        grid=(lanes // DFT_LT,),
        in_specs=[pl.BlockSpec((nb, rows, DFT_LT), lambda j: (0, 0, j)),
                  pl.BlockSpec(fh.shape, lambda j: (0, 0)),
                  pl.BlockSpec(fl.shape, lambda j: (0, 0))],
        out_specs=pl.BlockSpec((m, DFT_LT), lambda j: (0, j)),
        out_shape=jax.ShapeDtypeStruct((m, lanes), F32),
        compiler_params=_cparams(("parallel",)),
        name="dft_lead",
    )(z, fh, fl)


DFT_KB = 4


def _twiddle(tr_ref, ti_ref, j):
    reps = HY_W // LANES
    tr = jnp.concatenate([tr_ref[j]] * reps, axis=1)
    ti = jnp.concatenate([ti_ref[j]] * reps, axis=1)
    return tr, ti


def _spectrum_body(a_ref, tr_ref, ti_ref, fh_ref, fl_ref, sc_ref, o_ref):
    for j in range(DFT_KB):
        ar, ai = a_ref[0, j], a_ref[1, j]
        tr, ti = _twiddle(tr_ref, ti_ref, j)
        b = jnp.concatenate([ar * tr - ai * ti, ar * ti + ai * tr], axis=0)
        x = _dot3c(fh_ref[...], fl_ref[...], b) * sc_ref[...]
        o_ref[0, j] = x[:FN2]
        o_ref[1, j] = x[FN2:]


def _conv_mid_body(a_ref, tr_ref, ti_ref, fh_ref, fl_ref, gh_ref, gl_ref, k_ref, o_ref):
    for j in range(DFT_KB):
        ar, ai = a_ref[0, j], a_ref[1, j]
        tr, ti = _twiddle(tr_ref, ti_ref, j)
        b = jnp.concatenate([ar * tr - ai * ti, ar * ti + ai * tr], axis=0)
        x = _dot3c(fh_ref[...], fl_ref[...], b)
        xr, xi = x[:FN2], x[FN2:]
        kr, ki = k_ref[0, j], k_ref[1, j]
        y = jnp.concatenate([xr * kr - xi * ki, xr * ki + xi * kr], axis=0)
        c = _dot3c(gh_ref[...], gl_ref[...], y)
        cr, ci = c[:FN2], c[FN2:]
        o_ref[0, j] = cr * tr + ci * ti
        o_ref[1, j] = ci * tr - cr * ti


def _dft_mid_specs():
    slab = pl.BlockSpec((2, DFT_KB, FN2, HY_W), lambda i: (0, i, 0, 0))
    tw = pl.BlockSpec((DFT_KB, FN2, LANES), lambda i: (i, 0, 0))
    mat = pl.BlockSpec((2 * FN2, 2 * FN2), lambda i: (0, 0))
    return slab, tw, mat


def _filter_spectrum(a4, twr, twi, fh, fl, scale):
    slab, tw, mat = _dft_mid_specs()
    return pl.pallas_call(
        _spectrum_body,
        grid=(FN1 // DFT_KB,),
        in_specs=[slab, tw, tw, mat, mat, pl.BlockSpec((1, HY_W), lambda i: (0, 0))],
        out_specs=slab,
        out_shape=jax.ShapeDtypeStruct((2, FN1, FN2, HY_W), F32),
        compiler_params=_cparams(("parallel",)),
        name="filter_spectrum",
    )(a4, twr, twi, fh, fl, scale)


def _conv_mid(a4, twr, twi, fh, fl, gh, gl, khat):
    slab, tw, mat = _dft_mid_specs()
    return pl.pallas_call(
        _conv_mid_body,
        grid=(FN1 // DFT_KB,),
        in_specs=[slab, tw, tw, mat, mat, mat, mat, slab],
        out_specs=slab,
        out_shape=jax.ShapeDtypeStruct((2, FN1, FN2, HY_W), F32),
        compiler_params=_cparams(("parallel",)),
        name="conv_mid",
    )(a4, twr, twi, fh, fl, gh, gl, khat)


HY_LT = 2048


def _conv_out_body(d_ref, fh_ref, fl_ref, u_ref, x2_ref, bias_ref, o_ref):
    y = _dot3c(fh_ref[...], fl_ref[...], d_ref[...])
    half = SEQ // FN2
    bias = bias_ref[...]
    for b in range(2):
        yb = y[b * half:(b + 1) * half]
        o_ref[b] = ((yb + u_ref[b] * bias) * x2_ref[b]).astype(BF16)


def _conv_out(d2, fh, fl, u3, x23, bias_t):
    half = SEQ // FN2
    lanes = d2.shape[1]
    io = pl.BlockSpec((2, half, HY_LT), lambda j: (0, 0, j))
    return pl.pallas_call(
        _conv_out_body,
        grid=(lanes // HY_LT,),
        in_specs=[pl.BlockSpec((2 * FN1, HY_LT), lambda j: (0, j)),
                  pl.BlockSpec(fh.shape, lambda j: (0, 0)),
                  pl.BlockSpec(fl.shape, lambda j: (0, 0)),
                  io, io,
                  pl.BlockSpec((1, HY_LT), lambda j: (0, 0))],
        out_specs=io,
        out_shape=jax.ShapeDtypeStruct((2, half, lanes), BF16),
        compiler_params=_cparams(("parallel",)),
        name="conv_out",
    )(d2, fh, fl, u3, x23, bias_t)


OP_TM = 512


def _outproj_body(att_ref, hy_ref, x_ref, w_ref, g1_ref, n2_ref, sh_ref, sc_ref, wrh_ref, wrl_ref,
                  x1_ref, h2_ref, lg_ref):
    a = jnp.concatenate([att_ref[0], hy_ref[0]], axis=1)
    x1 = x_ref[0] + g1_ref[0] * _dot(a, w_ref[...])
    x1_ref[0] = x1
    h2 = _rms_mod(x1, n2_ref[...], sh_ref[0], sc_ref[0])
    hh, hl = _split(h2)
    h2_ref[0] = hh
    wrh = wrh_ref[...]
    lg_ref[0] = _dot(wrh, hh, _NT) + _dot(wrh, hl, _NT) + _dot(wrl_ref[...], hh, _NT)


def _out_projection(att, hy, x, w_out_bf, g1r, n2g, sh2, sc2, wrh, wrl):
    b = x.shape[0]
    tok = lambda bi, i: (bi, i, 0)
    row = lambda bi, i: (bi, 0, 0)
    const = lambda bi, i: (0, 0)
    return pl.pallas_call(
        _outproj_body,
        grid=(b, SEQ // OP_TM),
        in_specs=[pl.BlockSpec((1, OP_TM, ATT_W), tok),
                  pl.BlockSpec((1, OP_TM, HY_W), tok),
                  pl.BlockSpec((1, OP_TM, D), tok),
                  pl.BlockSpec((ATT_W + HY_W, D), const),
                  pl.BlockSpec((1, 1, D), row),
                  pl.BlockSpec((1, D), const),
                  pl.BlockSpec((1, 1, D), row),
                  pl.BlockSpec((1, 1, D), row),
                  pl.BlockSpec((NE, D), const),
                  pl.BlockSpec((NE, D), const)],
        out_specs=[pl.BlockSpec((1, OP_TM, D), tok),
                   pl.BlockSpec((1, OP_TM, D), tok),
                   pl.BlockSpec((1, NE, OP_TM), lambda bi, i: (bi, 0, i))],
        out_shape=[jax.ShapeDtypeStruct((b, SEQ, D), F32),
                   jax.ShapeDtypeStruct((b, SEQ, D), BF16),
                   jax.ShapeDtypeStruct((b, NE, SEQ), F32)],
        compiler_params=_cparams(("parallel", "parallel")),
        name="out_projection",
    )(att, hy, x, w_out_bf, g1r, n2g, sh2, sc2, wrh, wrl)


def _routing_body(lg_ref, tri_ref, pos_ref, gate_ref, off_ref, cs_ref):
    lg = lg_ref[0]
    e = jnp.exp(lg - jnp.max(lg, axis=0, keepdims=True))
    aff = e / jnp.sum(e, axis=0, keepdims=True)
    gate_ref[0] = aff
    def count_ge(t):
        return jnp.sum(jnp.where(aff >= t, 1.0, 0.0), axis=1, keepdims=True)

    def bisect(i, thr):
        cand = thr | (jnp.int32(1) << (30 - i))
        return jnp.where(count_ge(pltpu.bitcast(cand, F32)) >= float(CAP), cand, thr)

    thr = lax.fori_loop(0, 31, bisect, jnp.zeros((NE, 1), I32))
    lo = pltpu.bitcast(thr, F32)
    hi = jnp.maximum(pltpu.bitcast(thr + 1, F32), jnp.finfo(F32).tiny)

    def refine(i, c):
        lo, hi = c
        mid = lo + (hi - lo) * 0.5
        ok = count_ge(mid) >= float(CAP)
        return jnp.where(ok, mid, lo), jnp.where(ok, hi, mid)

    lo, hi = lax.fori_loop(0, 32, refine, (lo, hi))
    gt = aff >= hi
    eq = (aff >= lo) & jnp.logical_not(gt)
    need = float(CAP) - jnp.sum(jnp.where(gt, 1.0, 0.0), axis=1, keepdims=True)
    tri = tri_ref[...]

    def excl_cumsum(mask_f, record_offsets):
        carry = jnp.zeros((NE, 1), F32)
        for c in range(NTCH):
            sl = slice(c * TCH, (c + 1) * TCH)
            m = mask_f[:, sl]
            inc = _dot(m.astype(BF16), tri)
            cs_ref[:, sl] = inc - m + carry
            if record_offsets:
                off_ref[0, :, c:c + 1] = carry.astype(I32)
            carry = carry + inc[:, TCH - 1:TCH]
        return cs_ref[...]

    eq_rank = excl_cumsum(jnp.where(eq, 1.0, 0.0), False)
    sel = gt | (eq & (eq_rank < need))
    pos = excl_cumsum(jnp.where(sel, 1.0, 0.0), True)
    pos_ref[0] = jnp.where(sel, pos.astype(I32), -1)


def _routing(logits, tri):
    b = logits.shape[0]
    blk = pl.BlockSpec((1, NE, SEQ), lambda bi: (bi, 0, 0))
    return pl.pallas_call(
        _routing_body,
        grid=(b,),
        in_specs=[blk, pl.BlockSpec((TCH, TCH), lambda bi: (0, 0))],
        out_specs=[blk, blk, pl.BlockSpec((1, NE, NTCH), lambda bi: (bi, 0, 0))],
        out_shape=[jax.ShapeDtypeStruct((b, NE, SEQ), I32),
                   jax.ShapeDtypeStruct((b, NE, SEQ), F32),
                   jax.ShapeDtypeStruct((b, NE, NTCH), I32)],
        scratch_shapes=[pltpu.VMEM((NE, SEQ), F32)],
        compiler_params=_cparams(("parallel",)),
        name="routing",
    )(logits, tri)


def _gather_body(off_ref, h_ref, pos_ref, xg_ref, acc_ref):
    b = pl.program_id(0)
    e = pl.program_id(1)
    acc_ref[...] = jnp.zeros_like(acc_ref)
    crow = lax.broadcasted_iota(I32, (GW, TCH), 0)

    def chunk(c, _):
        off = off_ref[(b * NE + e) * NTCH + c]
        base = pl.multiple_of(jnp.minimum((off >> 3) << 3, CAP - TCH), SUBLANES)
        t0 = pl.multiple_of(c * TCH, TCH)
        rel = pos_ref[0, 0, pl.ds(c, 1), :] - base
        onehot = jnp.where(crow == rel, 1.0, 0.0).astype(BF16)
        acc_ref[pl.ds(base, GW), :] += _dot(onehot, h_ref[0, pl.ds(t0, TCH), :])
        return 0

    lax.fori_loop(0, NTCH, chunk, 0)
    xg_ref[0, 0] = acc_ref[0:CAP, :].astype(BF16)


def _gather(offs_flat, h2, pos4):
    b = h2.shape[0]
    grid_spec = pltpu.PrefetchScalarGridSpec(
        num_scalar_prefetch=1,
        grid=(b, NE),
        in_specs=[pl.BlockSpec((1, SEQ, D), lambda bi, e, off: (bi, 0, 0)),
                  pl.BlockSpec((1, 1, NTCH, TCH), lambda bi, e, off: (bi, e, 0, 0))],
        out_specs=pl.BlockSpec((1, 1, CAP, D), lambda bi, e, off: (bi, e, 0, 0)),
        scratch_shapes=[pltpu.VMEM((CAP + SUBLANES, D), F32)],
    )
    return pl.pallas_call(
        _gather_body,
        grid_spec=grid_spec,
        out_shape=jax.ShapeDtypeStruct((b, NE, CAP, D), BF16),
        compiler_params=_cparams(("parallel", "arbitrary")),
        name="moe_gather",
    )(offs_flat, h2, pos4)


FFN_TM = 256


def _ffn_body(xg_ref, wg_ref, wu_ref, wd_ref, y_ref):
    wg = wg_ref[0]
    wu = wu_ref[0]
    wd = wd_ref[0]
    for mb in range(CAP // FFN_TM):
        rows = slice(mb * FFN_TM, (mb + 1) * FFN_TM)
        xb = xg_ref[0, 0, rows, :]
        a = _dot(xb, wg)
        u = _dot(xb, wu)
        h = (a * (1.0 / (1.0 + jnp.exp(-a))) * u).astype(BF16)
        y_ref[0, 0, rows, :] = _dot(h, wd).astype(BF16)
    y_ref[0, 0, CAP:YROWS, :] = jnp.zeros((YROWS - CAP, D), BF16)


def _expert_ffn(xg, wg_bf, wu_bf, wd_bf):
    b = xg.shape[0]
    return pl.pallas_call(
        _ffn_body,
        grid=(NE, b),
        in_specs=[pl.BlockSpec((1, 1, CAP, D), lambda e, bi: (bi, e, 0, 0)),
                  pl.BlockSpec((1, D, DEXP), lambda e, bi: (e, 0, 0)),
                  pl.BlockSpec((1, D, DEXP), lambda e, bi: (e, 0, 0)),
                  pl.BlockSpec((1, DEXP, D), lambda e, bi: (e, 0, 0))],
        out_specs=pl.BlockSpec((1, 1, YROWS, D), lambda e, bi: (bi, e, 0, 0)),
        out_shape=jax.ShapeDtypeStruct((b, NE, YROWS, D), BF16),
        compiler_params=_cparams(("parallel", "arbitrary")),
        name="expert_ffn",
    )(xg, wg_bf, wu_bf, wd_bf)


def _combine_body(off_ref, y_ref, pos_ref, gate_ref, x1_ref, g2_ref, o_ref):
    b = pl.program_id(0)
    i = pl.program_id(1)
    crow = lax.broadcasted_iota(I32, (CW, TCH), 0)
    acc = jnp.zeros((TCH, D), F32)
    for e in range(NE):
        off = off_ref[(b * NE + e) * NTCH + i]
        base = pl.multiple_of(jnp.minimum((off >> 4) << 4, CAP - TCH), BF16_ROWS)
        rel = pos_ref[0, e:e + 1, :] - base
        w = jnp.where(crow == rel, gate_ref[0, e:e + 1, :], 0.0)
        wh, wl = _split(w)
        yw = y_ref[0, e, pl.ds(base, CW), :]
        acc = acc + _dot(wh, yw, _TN) + _dot(wl, yw, _TN)
    o_ref[0] = x1_ref[0] + g2_ref[0] * acc


def _combine(offs_flat, y, pos, gate, x1, g2r):
    b = x1.shape[0]
    grid_spec = pltpu.PrefetchScalarGridSpec(
        num_scalar_prefetch=1,
        grid=(b, NTCH),
        in_specs=[pl.BlockSpec((1, NE, YROWS, D), lambda bi, i, off: (bi, 0, 0, 0),
                               pipeline_mode=pl.Buffered(1)),
                  pl.BlockSpec((1, NE, TCH), lambda bi, i, off: (bi, 0, i)),
                  pl.BlockSpec((1, NE, TCH), lambda bi, i, off: (bi, 0, i)),
                  pl.BlockSpec((1, TCH, D), lambda bi, i, off: (bi, i, 0)),
                  pl.BlockSpec((1, 1, D), lambda bi, i, off: (bi, 0, 0))],
        out_specs=pl.BlockSpec((1, TCH, D), lambda bi, i, off: (bi, i, 0)),
    )
    return pl.pallas_call(
        _combine_body,
        grid_spec=grid_spec,
        out_shape=jax.ShapeDtypeStruct((b, SEQ, D), F32),
        compiler_params=_cparams(("parallel", "arbitrary")),
        name="moe_combine",
    )(offs_flat, y, pos, gate, x1, g2r)


def _np_split(m):
    hi = np.asarray(m, np.float64).astype(BF16)
    lo = (m - hi.astype(np.float64)).astype(BF16)
    return jnp.asarray(hi), jnp.asarray(lo)


@functools.lru_cache(maxsize=None)
def _dft_tables():
    a = np.arange(FN1, dtype=np.float64)
    ang = 2.0 * np.pi * np.outer(a, a) / FN1
    fr, fi = np.cos(ang), -np.sin(ang)
    half = SEQ // FN2
    lead_u = np.block([[fr[:, :half], -fi[:, :half]], [fi[:, :half], fr[:, :half]]])
    lead_k = np.concatenate([fr, fi], axis=0)
    fwd = np.block([[fr, -fi], [fi, fr]])
    inv = np.block([[fr, fi], [-fi, fr]])
    out = np.block([[fr[:half], fi[:half]], [-fi[:half], fr[:half]]])
    n2 = np.arange(FN2, dtype=np.float64)
    tw = 2.0 * np.pi * np.outer(a, n2) / FN
    twr = np.broadcast_to(np.cos(tw)[:, :, None], (FN1, FN2, LANES)).astype(np.float32)
    twi = np.broadcast_to(-np.sin(tw)[:, :, None], (FN1, FN2, LANES)).astype(np.float32)
    return dict(lead_u=lead_u, lead_k=lead_k, fwd=fwd, inv=inv, out=out, twr=twr, twi=twi)


@functools.lru_cache(maxsize=None)
def _filter_tables():
    L = SEQ
    r = np.arange(FN)
    j = np.where(r < L, r, FN - r)
    jc = np.minimum(j, L - 1).astype(np.float64)
    t = (jc / (L - 1))[:, None]
    bands = (FEMB - 1) // 2
    w = 2.0 * np.pi * jc / L
    f = np.linspace(1e-4, bands - 1, bands)
    fw = w[:, None] * f[None, :]
    z = np.concatenate([t, np.cos(fw), -np.sin(fw), np.zeros((FN, FORD - FEMB))], axis=-1)
    mask = np.where(r == L, 0.0, 1.0)[:, None]
    max_decay = math.log(DECAY_TARGET) / FAST_DECAY_PCT
    min_decay = math.log(DECAY_TARGET) / SLOW_DECAY_PCT
    negdelta = -np.abs(np.linspace(min_decay, max_decay, HY_W))[None, :]
    return tuple(np.asarray(a, np.float32) for a in (z, t, mask, negdelta))


@functools.lru_cache(maxsize=None)
def _rope_tables(n):
    rows = n // GRID_W
    row_id, col_id = np.meshgrid(np.arange(rows, dtype=np.float64), np.arange(GRID_W, dtype=np.float64), indexing="ij")
    quarter = HD // 4
    inv_freq = ROPE_THETA ** (-np.arange(quarter, dtype=np.float64) / quarter)
    ar = row_id.reshape(-1)[:, None] * inv_freq
    ac = col_id.reshape(-1)[:, None] * inv_freq
    cos = np.concatenate([np.cos(ar), np.cos(ar), np.cos(ac), np.cos(ac)], axis=-1)
    sin = np.concatenate([-np.sin(ar), np.sin(ar), -np.sin(ac), np.sin(ac)], axis=-1)
    reps = (1, LANES // HD)
    return np.tile(cos, reps).astype(np.float32), np.tile(sin, reps).astype(np.float32)


def _hyena_long_conv(u, x2c, kern, abs_sum, bias):
    tb = _dft_tables()
    half = SEQ // FN2
    lanes = FN2 * HY_W
    twr, twi = jnp.asarray(tb["twr"]), jnp.asarray(tb["twi"])
    fwd_h, fwd_l = _np_split(tb["fwd"])
    inv_h, inv_l = _np_split(tb["inv"])
    ak = _dft_lead(kern.reshape(1, FN1, lanes), *_np_split(tb["lead_k"]))
    scale = 1.0 / (abs_sum * float(FN))
    khat = _filter_spectrum(ak.reshape(2, FN1, FN2, HY_W), twr, twi, fwd_h, fwd_l, scale)
    u3 = u.reshape(2, half, lanes)
    au = _dft_lead(u3, *_np_split(tb["lead_u"]))
    dd = _conv_mid(au.reshape(2, FN1, FN2, HY_W), twr, twi, fwd_h, fwd_l, inv_h, inv_l, khat)
    bias_t = jnp.tile(bias.reshape(1, HY_W), (1, HY_LT // HY_W))
    hy = _conv_out(dd.reshape(2 * FN1, lanes), *_np_split(tb["out"]), u3, x2c.reshape(2, half, lanes), bias_t)
    return hy.reshape(2, SEQ, HY_W)


def kernel(x, c, ctx, c_ctx, w_mod, b_mod, norm1_g, norm2_g, w_in, w_out, q_norm_g, k_norm_g,
           conv_w, conv_b, filt_w1, filt_b1, filt_w2, filt_b2, filt_w3, filt_freq, hyena_bias,
           w_router, w_gate, w_up, w_down):
    B = x.shape[0]
    assert x.shape == (B, SEQ, D) and B == 2 and ctx.shape == (B, CTX, D) and w_mod.shape[0] == 1
    l = 0

    cc = jnp.concatenate([c, c_ctx[None, :], jnp.zeros((SUBLANES - B - 1, D), F32)], axis=0)
    mod = _modulation(cc, w_mod[l], b_mod[l][None, :])
    sh1, sc1, g1, sh2, sc2, g2 = [mod[:, i * D:(i + 1) * D] for i in range(6)]
    lat = lambda m: m[:B, None, :]
    ctxrow = lambda m: jnp.broadcast_to(m[B:B + 1, None, :], (B, 1, D))

    w_in_bf = w_in[l].astype(BF16)
    gq2 = jnp.tile(q_norm_g[l][None, :], (1, LANES // HD))
    gk2 = jnp.tile(k_norm_g[l][None, :], (1, LANES // HD))
    bd = jnp.asarray(np.kron(np.eye(LANES // HD), np.full((HD, HD), 1.0 / HD)), BF16)
    cos_t, sin_t = _rope_tables(SEQ)
    n1g = norm1_g[l][None, :]

    q, k, v, p = _in_projection(x, n1g, lat(sh1), lat(sc1), w_in_bf, gq2, gk2, bd, cos_t, sin_t, 512)
    _, kc, vc, _ = _in_projection(ctx, n1g, ctxrow(sh1), ctxrow(sc1), w_in_bf, gq2, gk2, bd,
                                  jnp.ones((CTX, LANES), F32), jnp.zeros((CTX, LANES), F32), CTX)

    k_all = jnp.concatenate([k, kc], axis=1).reshape(B, ATT_NCH, ATT_TK, NKV, HD)
    v_all = jnp.concatenate([v, vc], axis=1).reshape(B, ATT_NCH, ATT_TK, NKV, HD)
    kch = k_all.transpose(0, 3, 1, 2, 4)
    vtch = v_all.transpose(0, 3, 1, 4, 2)
    att = _attention(q, kch, vtch)

    cw9 = conv_w[l].reshape(3, 3, HY_W).reshape(9, HY_W)
    cb3 = conv_b[l].reshape(3, HY_W)
    u, x2c = _short_conv(p, cw9, cb3)
    ztab, ttab, mtab, negdelta = _filter_tables()
    w1p = jnp.concatenate([filt_w1[l], jnp.zeros((FORD - FEMB, FORD), F32)], axis=0)
    kern, abs_sum = _implicit_filter(ztab, ttab, mtab, w1p, filt_b1[l][None, :], filt_w2[l],
                                     filt_b2[l][None, :], filt_w3[l], filt_freq[l][None, :], negdelta)
    hy = _hyena_long_conv(u, x2c, kern, abs_sum, hyena_bias[l])

    wrh, wrl = _split(w_router[l].T)
    x1, h2, logits = _out_projection(att, hy, x, w_out[l].astype(BF16), lat(g1), norm2_g[l][None, :],
                                     lat(sh2), lat(sc2), wrh, wrl)

    tri = jnp.asarray(np.triu(np.ones((TCH, TCH))), BF16)
    pos, gate, offs = _routing(logits, tri)
    offs_flat = offs.reshape(-1)
    xg = _gather(offs_flat, h2, pos.reshape(B, NE, NTCH, TCH))
    y = _expert_ffn(xg, w_gate[l].astype(BF16), w_up[l].astype(BF16), w_down[l].astype(BF16))
    return _combine(offs_flat, y, pos, gate, x1, lat(g2))
```

```python
import functools
import math

import numpy as np
import jax
import jax.numpy as jnp
from jax import lax
from jax.experimental import pallas as pl
from jax.experimental.pallas import tpu as pltpu

F32 = jnp.float32
BF16 = jnp.bfloat16
I32 = jnp.int32

D = 1024
SEQ = 8192
CTX = 256
GRID_W = 64
ATT_W = 512
HY_W = 512
HD = 64
NQ = 8
NKV = 2
QPK = NQ // NKV
KV_W = NKV * HD
IN_W = ATT_W + 2 * KV_W + 3 * HY_W
FEMB = 33
FORD = 64
NE = 16
CAP = 2 * SEQ // NE
DEXP = 2752
ROPE_THETA = 10000.0
EPS = 1e-6
DECAY_TARGET = 1e-2
FAST_DECAY_PCT = 0.3
SLOW_DECAY_PCT = 1.5

LANES = 128
SUBLANES = 8
BF16_ROWS = 16
VMEM_BYTES_V7X = 64 * 1024 * 1024
VMEM_LIMIT = VMEM_BYTES_V7X - 8 * 1024 * 1024

FN = 2 * SEQ
FN1 = 128
FN2 = 128

TCH = LANES
NTCH = SEQ // TCH
GW = TCH + SUBLANES
CW = TCH + BF16_ROWS
YROWS = CAP + BF16_ROWS


def _cparams(sem, vmem=None):
    return pltpu.CompilerParams(dimension_semantics=sem, vmem_limit_bytes=vmem or VMEM_LIMIT)


def _split(a):
    hi = a.astype(BF16)
    lo = (a - hi.astype(F32)).astype(BF16)
    return hi, lo


_NN = (((1,), (0,)), ((), ()))
_NT = (((1,), (1,)), ((), ()))
_TN = (((0,), (0,)), ((), ()))


def _dot(a, b, dn=_NN):
    return lax.dot_general(a, b, dn, preferred_element_type=F32)


def _dot3(a, b, dn=_NN):
    ah, al = _split(a)
    bh, bl = _split(b)
    return _dot(ah, bh, dn) + _dot(ah, bl, dn) + _dot(al, bh, dn)


def _dot3c(fh, fl, z):
    zh, zl = _split(z)
    return _dot(fh, zh) + _dot(fh, zl) + _dot(fl, zh)


def _mod_body(c_ref, w_ref, b_ref, o_ref):
    c = c_ref[...]
    s = c * (1.0 / (1.0 + jnp.exp(-c)))
    o_ref[...] = _dot3(s, w_ref[...]) + b_ref[...]


def _modulation(cc, w_mod, b_mod):
    n = w_mod.shape[1]
    return pl.pallas_call(
        _mod_body,
        grid=(n // D,),
        in_specs=[pl.BlockSpec((SUBLANES, D), lambda j: (0, 0)),
                  pl.BlockSpec((D, D), lambda j: (0, j)),
                  pl.BlockSpec((1, D), lambda j: (0, j))],
        out_specs=pl.BlockSpec((SUBLANES, D), lambda j: (0, j)),
        out_shape=jax.ShapeDtypeStruct((SUBLANES, n), F32),
        compiler_params=_cparams(("arbitrary",)),
        name="modulation",
    )(cc, w_mod, b_mod)


Q_SCALE = HD ** -0.5 * math.log2(math.e)


def _rms_mod(x, g, sh, sc):
    ms = jnp.mean(x * x, axis=-1, keepdims=True)
    return (x * lax.rsqrt(ms + EPS) * g) * (1.0 + sc) + sh


def _head_norm_rope(t, g, bd, cos, sin):
    sq = t * t
    hi, lo = _split(sq)
    ms = _dot(hi, bd) + _dot(lo, bd)
    tn = t * lax.rsqrt(ms + EPS) * g
    lane = lax.broadcasted_iota(I32, tn.shape, 1)
    sw = jnp.where((lane & 31) < 16, pltpu.roll(tn, LANES - 16, 1), pltpu.roll(tn, 16, 1))
    return tn * cos + sw * sin


def _proj_body(x_ref, g_ref, sh_ref, sc_ref, w_ref, gq_ref, gk_ref, bd_ref, cos_ref, sin_ref,
               q_ref, k_ref, v_ref, p_ref):
    h = _rms_mod(x_ref[0], g_ref[...], sh_ref[0], sc_ref[0])
    proj = _dot(h.astype(BF16), w_ref[...])
    bd = bd_ref[...]
    cos = cos_ref[...]
    sin = sin_ref[...]
    for j in range(ATT_W // LANES):
        sl = slice(j * LANES, (j + 1) * LANES)
        qj = _head_norm_rope(proj[:, sl], gq_ref[...], bd, cos, sin)
        q_ref[0, :, sl] = (qj * Q_SCALE).astype(BF16)
    k_ref[0] = _head_norm_rope(proj[:, ATT_W:ATT_W + KV_W], gk_ref[...], bd, cos, sin).astype(BF16)
    v_ref[0] = proj[:, ATT_W + KV_W:ATT_W + 2 * KV_W].astype(BF16)
    p_ref[0] = proj[:, ATT_W + 2 * KV_W:]


def _in_projection(x, g1, sh, sc, w_in_bf, gq2, gk2, bd, cos_t, sin_t, tm):
    b, s, _ = x.shape
    row = lambda bi, i: (bi, 0, 0)
    tok = lambda bi, i: (bi, i, 0)
    const = lambda bi, i: (0, 0)
    return pl.pallas_call(
        _proj_body,
        grid=(b, s // tm),
        in_specs=[pl.BlockSpec((1, tm, D), tok),
                  pl.BlockSpec((1, D), const),
                  pl.BlockSpec((1, 1, D), row),
                  pl.BlockSpec((1, 1, D), row),
                  pl.BlockSpec((D, IN_W), const),
                  pl.BlockSpec((1, LANES), const),
                  pl.BlockSpec((1, LANES), const),
                  pl.BlockSpec((LANES, LANES), const),
                  pl.BlockSpec((tm, LANES), lambda bi, i: (i, 0)),
                  pl.BlockSpec((tm, LANES), lambda bi, i: (i, 0))],
        out_specs=[pl.BlockSpec((1, tm, ATT_W), tok),
                   pl.BlockSpec((1, tm, KV_W), tok),
                   pl.BlockSpec((1, tm, KV_W), tok),
                   pl.BlockSpec((1, tm, 3 * HY_W), tok)],
        out_shape=[jax.ShapeDtypeStruct((b, s, ATT_W), BF16),
                   jax.ShapeDtypeStruct((b, s, KV_W), BF16),
                   jax.ShapeDtypeStruct((b, s, KV_W), BF16),
                   jax.ShapeDtypeStruct((b, s, 3 * HY_W), F32)],
        compiler_params=_cparams(("parallel", "parallel")),
        name="in_projection",
    )(x, g1, sh, sc, w_in_bf, gq2, gk2, bd, cos_t, sin_t)


ATT_TQ = 256
ATT_TK = 768
SK = SEQ + CTX
ATT_NCH = SK // ATT_TK


ATT_NQ = QPK * ATT_TQ
ATT_VR = HD + BF16_ROWS
assert ATT_NCH % 2 == 1


def _attn_body(q_ref, k_ref, vt_ref, o_ref, s0_ref, s1_ref, mx0_ref, mx1_ref, m_ref, acc_ref):
    qall = jnp.concatenate([q_ref[0, :, r * HD:(r + 1) * HD] for r in range(QPK)], axis=0)
    m_ref[...] = jnp.full(m_ref.shape, -1e30, F32)
    acc_ref[...] = jnp.zeros_like(acc_ref)

    def scores(c, s_ref, mx_ref):
        s = _dot(k_ref[0, 0, c], qall, _NT)
        s_ref[...] = s
        mx_ref[...] = jnp.max(s, axis=0, keepdims=True)

    def update(c, s_ref, mx_ref):
        m_old = m_ref[...]
        m_new = jnp.maximum(m_old, mx_ref[...])
        p = jnp.exp2(s_ref[...] - m_new).astype(BF16)
        acc_ref[...] = jnp.exp2(m_old - m_new) * acc_ref[...] + _dot(vt_ref[0, 0, c], p)
        m_ref[...] = m_new

    scores(0, s0_ref, mx0_ref)

    def pair(i, _):
        c = 2 * i
        scores(c + 1, s1_ref, mx1_ref)
        update(c, s0_ref, mx0_ref)
        scores(c + 2, s0_ref, mx0_ref)
        update(c + 1, s1_ref, mx1_ref)
        return 0

    lax.fori_loop(0, ATT_NCH // 2, pair, 0)
    update(ATT_NCH - 1, s0_ref, mx0_ref)
    out = acc_ref[0:HD, :] * (1.0 / acc_ref[HD:HD + 1, :])
    for r in range(QPK):
        o_ref[0, :, r * HD:(r + 1) * HD] = out[:, r * ATT_TQ:(r + 1) * ATT_TQ].T.astype(BF16)


def _attention(q, kch, vtch):
    b = q.shape[0]
    return pl.pallas_call(
        _attn_body,
        grid=(b, NKV, SEQ // ATT_TQ),
        in_specs=[pl.BlockSpec((1, ATT_TQ, QPK * HD), lambda bi, g, i: (bi, i, g)),
                  pl.BlockSpec((1, 1, ATT_NCH, ATT_TK, HD), lambda bi, g, i: (bi, g, 0, 0, 0)),
                  pl.BlockSpec((1, 1, ATT_NCH, ATT_VR, ATT_TK), lambda bi, g, i: (bi, g, 0, 0, 0))],
        out_specs=pl.BlockSpec((1, ATT_TQ, QPK * HD), lambda bi, g, i: (bi, i, g)),
        out_shape=jax.ShapeDtypeStruct((b, SEQ, ATT_W), BF16),
        scratch_shapes=[pltpu.VMEM((ATT_TK, ATT_NQ), F32), pltpu.VMEM((ATT_TK, ATT_NQ), F32),
                        pltpu.VMEM((1, ATT_NQ), F32), pltpu.VMEM((1, ATT_NQ), F32),
                        pltpu.VMEM((1, ATT_NQ), F32), pltpu.VMEM((ATT_VR, ATT_NQ), F32)],
        compiler_params=_cparams(("parallel", "parallel", "parallel")),
        name="attention",
    )(q, kch, vtch)


SC_TM = 1024


def _sconv_body(m1, a1, n1, m2, a2, n2, m3, a3, n3, w_ref, b_ref, u_ref, x2_ref):
    i = pl.program_id(1)
    last = pl.num_programs(1) - 1
    rows = lax.broadcasted_iota(I32, (SC_TM, HY_W), 0)

    def conv(main, prev, nxt, g):
        x = main[0]
        pr = jnp.where(i > 0, prev[0, SUBLANES - 1:SUBLANES, :], 0.0)
        nx = jnp.where(i < last, nxt[0, 0:1, :], 0.0)
        xm = jnp.where(rows == 0, pr, pltpu.roll(x, 1, 0))
        xp = jnp.where(rows == SC_TM - 1, nx, pltpu.roll(x, SC_TM - 1, 0))
        return (w_ref[g:g + 1, :] * xm + w_ref[3 + g:4 + g, :] * x + w_ref[6 + g:7 + g, :] * xp
                + b_ref[g:g + 1, :])

    x1 = conv(m1, a1, n1, 0)
    x2 = conv(m2, a2, n2, 1)
    v = conv(m3, a3, n3, 2)
    u_ref[0] = v * x1
    x2_ref[0] = x2


def _short_conv(p, cw9, cb3):
    b = p.shape[0]
    nblk8 = SEQ // SUBLANES
    step8 = SC_TM // SUBLANES
    specs = []
    for g in range(3):
        specs += [pl.BlockSpec((1, SC_TM, HY_W), lambda bi, i, g=g: (bi, i, g)),
                  pl.BlockSpec((1, SUBLANES, HY_W), lambda bi, i, g=g: (bi, jnp.maximum(i * step8 - 1, 0), g)),
                  pl.BlockSpec((1, SUBLANES, HY_W), lambda bi, i, g=g: (bi, jnp.minimum((i + 1) * step8, nblk8 - 1), g))]
    specs += [pl.BlockSpec((9, HY_W), lambda bi, i: (0, 0)), pl.BlockSpec((3, HY_W), lambda bi, i: (0, 0))]
    out = pl.BlockSpec((1, SC_TM, HY_W), lambda bi, i: (bi, i, 0))
    return pl.pallas_call(
        _sconv_body,
        grid=(b, SEQ // SC_TM),
        in_specs=specs,
        out_specs=[out, out],
        out_shape=[jax.ShapeDtypeStruct((b, SEQ, HY_W), F32)] * 2,
        compiler_params=_cparams(("parallel", "parallel")),
        name="short_conv",
    )(p, p, p, p, p, p, p, p, p, cw9, cb3)


FILT_TR = 1024


def _filter_body(z_ref, t_ref, msk_ref, w1_ref, b1_ref, w2_ref, b2_ref, w3_ref, fr_ref, dl_ref,
                 k_ref, s_ref):
    fr = fr_ref[...]
    h = jnp.sin(fr * (_dot3(z_ref[...], w1_ref[...]) + b1_ref[...]))
    h = jnp.sin(fr * (_dot3(h, w2_ref[...]) + b2_ref[...]))
    h = _dot3(h, w3_ref[...])
    kern = h * jnp.exp(t_ref[...] * dl_ref[...]) * msk_ref[...]
    k_ref[...] = kern

    @pl.when((pl.program_id(0) == 0) & (pl.program_id(1) == 0))
    def _():
        s_ref[...] = jnp.zeros_like(s_ref)

    s_ref[...] += jnp.sum(jnp.abs(kern), axis=0, keepdims=True)


def _implicit_filter(ztab, ttab, mtab, w1p, b1, w2, b2, w3, freq, negdelta):
    nt = SEQ // FILT_TR
    rowblk = lambda hf, i: (hf * nt + i, 0)
    const = lambda hf, i: (0, 0)
    return pl.pallas_call(
        _filter_body,
        grid=(2, nt),
        in_specs=[pl.BlockSpec((FILT_TR, FORD), rowblk),
                  pl.BlockSpec((FILT_TR, 1), rowblk),
                  pl.BlockSpec((FILT_TR, 1), rowblk),
                  pl.BlockSpec((FORD, FORD), const),
                  pl.BlockSpec((1, FORD), const),
                  pl.BlockSpec((FORD, FORD), const),
                  pl.BlockSpec((1, FORD), const),
                  pl.BlockSpec((FORD, HY_W), lambda hf, i: (0, hf)),
                  pl.BlockSpec((1, FORD), const),
                  pl.BlockSpec((1, HY_W), const)],
        out_specs=[pl.BlockSpec((FILT_TR, HY_W), rowblk),
                   pl.BlockSpec((1, HY_W), const)],
        out_shape=[jax.ShapeDtypeStruct((FN, HY_W), F32), jax.ShapeDtypeStruct((1, HY_W), F32)],
        compiler_params=_cparams(("arbitrary", "arbitrary")),
        name="implicit_filter",
    )(ztab, ttab, mtab, w1p, b1, w2, b2, w3, freq, negdelta)


DFT_LT = 4096


def _dft_lead_body(z_ref, fh_ref, fl_ref, o_ref):
    nb = z_ref.shape[0]
    z = jnp.concatenate([z_ref[j] for j in range(nb)], axis=0) if nb > 1 else z_ref[0]
    o_ref[...] = _dot3c(fh_ref[...], fl_ref[...], z)


def _dft_lead(z, fh, fl):
    nb, rows, lanes = z.shape
    m = fh.shape[0]
    return pl.pallas_call(
        _dft_lead_body,
        grid=(lanes // DFT_LT,),
        in_specs=[pl.BlockSpec((nb, rows, DFT_LT), lambda j: (0, 0, j)),
                  pl.BlockSpec(fh.shape, lambda j: (0, 0)),
                  pl.BlockSpec(fl.shape, lambda j: (0, 0))],
        out_specs=pl.BlockSpec((m, DFT_LT), lambda j: (0, j)),
        out_shape=jax.ShapeDtypeStruct((m, lanes), F32),
        compiler_params=_cparams(("parallel",)),
        name="dft_lead",
    )(z, fh, fl)


DFT_KB = 4


def _twiddle(tr_ref, ti_ref, j):
    reps = HY_W // LANES
    tr = jnp.concatenate([tr_ref[j]] * reps, axis=1)
    ti = jnp.concatenate([ti_ref[j]] * reps, axis=1)
    return tr, ti


def _spectrum_body(a_ref, tr_ref, ti_ref, fh_ref, fl_ref, sc_ref, o_ref):
    for j in range(DFT_KB):
        ar, ai = a_ref[0, j], a_ref[1, j]
        tr, ti = _twiddle(tr_ref, ti_ref, j)
        b = jnp.concatenate([ar * tr - ai * ti, ar * ti + ai * tr], axis=0)
        x = _dot3c(fh_ref[...], fl_ref[...], b) * sc_ref[...]
        o_ref[0, j] = x[:FN2]
        o_ref[1, j] = x[FN2:]


def _conv_mid_body(a_ref, tr_ref, ti_ref, fh_ref, fl_ref, gh_ref, gl_ref, k_ref, o_ref):
    for j in range(DFT_KB):
        ar, ai = a_ref[0, j], a_ref[1, j]
        tr, ti = _twiddle(tr_ref, ti_ref, j)
        b = jnp.concatenate([ar * tr - ai * ti, ar * ti + ai * tr], axis=0)
        x = _dot3c(fh_ref[...], fl_ref[...], b)
        xr, xi = x[:FN2], x[FN2:]
        kr, ki = k_ref[0, j], k_ref[1, j]
        y = jnp.concatenate([xr * kr - xi * ki, xr * ki + xi * kr], axis=0)
        c = _dot3c(gh_ref[...], gl_ref[...], y)
        cr, ci = c[:FN2], c[FN2:]
        o_ref[0, j] = cr * tr + ci * ti
        o_ref[1, j] = ci * tr - cr * ti


def _dft_mid_specs():
    slab = pl.BlockSpec((2, DFT_KB, FN2, HY_W), lambda i: (0, i, 0, 0))
    tw = pl.BlockSpec((DFT_KB, FN2, LANES), lambda i: (i, 0, 0))
    mat = pl.BlockSpec((2 * FN2, 2 * FN2), lambda i: (0, 0))
    return slab, tw, mat


def _filter_spectrum(a4, twr, twi, fh, fl, scale):
    slab, tw, mat = _dft_mid_specs()
    return pl.pallas_call(
        _spectrum_body,
        grid=(FN1 // DFT_KB,),
        in_specs=[slab, tw, tw, mat, mat, pl.BlockSpec((1, HY_W), lambda i: (0, 0))],
        out_specs=slab,
        out_shape=jax.ShapeDtypeStruct((2, FN1, FN2, HY_W), F32),
        compiler_params=_cparams(("parallel",)),
        name="filter_spectrum",
    )(a4, twr, twi, fh, fl, scale)


def _conv_mid(a4, twr, twi, fh, fl, gh, gl, khat):
    slab, tw, mat = _dft_mid_specs()
    return pl.pallas_call(
        _conv_mid_body,
        grid=(FN1 // DFT_KB,),
        in_specs=[slab, tw, tw, mat, mat, mat, mat, slab],
        out_specs=slab,
        out_shape=jax.ShapeDtypeStruct((2, FN1, FN2, HY_W), F32),
        compiler_params=_cparams(("parallel",)),
        name="conv_mid",
    )(a4, twr, twi, fh, fl, gh, gl, khat)


HY_LT = 2048


def _conv_out_body(d_ref, fh_ref, fl_ref, u_ref, x2_ref, bias_ref, o_ref):
    y = _dot3c(fh_ref[...], fl_ref[...], d_ref[...])
    half = SEQ // FN2
    bias = bias_ref[...]
    for b in range(2):
        yb = y[b * half:(b + 1) * half]
        o_ref[b] = ((yb + u_ref[b] * bias) * x2_ref[b]).astype(BF16)


def _conv_out(d2, fh, fl, u3, x23, bias_t):
    half = SEQ // FN2
    lanes = d2.shape[1]
    io = pl.BlockSpec((2, half, HY_LT), lambda j: (0, 0, j))
    return pl.pallas_call(
        _conv_out_body,
        grid=(lanes // HY_LT,),
        in_specs=[pl.BlockSpec((2 * FN1, HY_LT), lambda j: (0, j)),
                  pl.BlockSpec(fh.shape, lambda j: (0, 0)),
                  pl.BlockSpec(fl.shape, lambda j: (0, 0)),
                  io, io,
                  pl.BlockSpec((1, HY_LT), lambda j: (0, 0))],
        out_specs=io,
        out_shape=jax.ShapeDtypeStruct((2, half, lanes), BF16),
        compiler_params=_cparams(("parallel",)),
        name="conv_out",
    )(d2, fh, fl, u3, x23, bias_t)


OP_TM = 512


def _outproj_body(att_ref, hy_ref, x_ref, w_ref, g1_ref, n2_ref, sh_ref, sc_ref, wrh_ref, wrl_ref,
                  x1_ref, h2_ref, lg_ref):
    a = jnp.concatenate([att_ref[0], hy_ref[0]], axis=1)
    x1 = x_ref[0] + g1_ref[0] * _dot(a, w_ref[...])
    x1_ref[0] = x1
    h2 = _rms_mod(x1, n2_ref[...], sh_ref[0], sc_ref[0])
    hh, hl = _split(h2)
    h2_ref[0] = hh
    wrh = wrh_ref[...]
    lg_ref[0] = _dot(wrh, hh, _NT) + _dot(wrh, hl, _NT) + _dot(wrl_ref[...], hh, _NT)


def _out_projection(att, hy, x, w_out_bf, g1r, n2g, sh2, sc2, wrh, wrl):
    b = x.shape[0]
    tok = lambda bi, i: (bi, i, 0)
    row = lambda bi, i: (bi, 0, 0)
    const = lambda bi, i: (0, 0)
    return pl.pallas_call(
        _outproj_body,
        grid=(b, SEQ // OP_TM),
        in_specs=[pl.BlockSpec((1, OP_TM, ATT_W), tok),
                  pl.BlockSpec((1, OP_TM, HY_W), tok),
                  pl.BlockSpec((1, OP_TM, D), tok),
                  pl.BlockSpec((ATT_W + HY_W, D), const),
                  pl.BlockSpec((1, 1, D), row),
                  pl.BlockSpec((1, D), const),
                  pl.BlockSpec((1, 1, D), row),
                  pl.BlockSpec((1, 1, D), row),
                  pl.BlockSpec((NE, D), const),
                  pl.BlockSpec((NE, D), const)],
        out_specs=[pl.BlockSpec((1, OP_TM, D), tok),
                   pl.BlockSpec((1, OP_TM, D), tok),
                   pl.BlockSpec((1, NE, OP_TM), lambda bi, i: (bi, 0, i))],
        out_shape=[jax.ShapeDtypeStruct((b, SEQ, D), F32),
                   jax.ShapeDtypeStruct((b, SEQ, D), BF16),
                   jax.ShapeDtypeStruct((b, NE, SEQ), F32)],
        compiler_params=_cparams(("parallel", "parallel")),
        name="out_projection",
    )(att, hy, x, w_out_bf, g1r, n2g, sh2, sc2, wrh, wrl)


def _routing_body(lg_ref, tri_ref, pos_ref, gate_ref, off_ref, cs_ref):
    lg = lg_ref[0]
    e = jnp.exp(lg - jnp.max(lg, axis=0, keepdims=True))
    aff = e / jnp.sum(e, axis=0, keepdims=True)
    gate_ref[0] = aff
    def count_ge(t):
        return jnp.sum(jnp.where(aff >= t, 1.0, 0.0), axis=1, keepdims=True)

    def bisect(i, thr):
        cand = thr | (jnp.int32(1) << (30 - i))
        return jnp.where(count_ge(pltpu.bitcast(cand, F32)) >= float(CAP), cand, thr)

    thr = lax.fori_loop(0, 31, bisect, jnp.zeros((NE, 1), I32))
    lo = pltpu.bitcast(thr, F32)
    hi = jnp.maximum(pltpu.bitcast(thr + 1, F32), jnp.finfo(F32).tiny)

    def refine(i, c):
        lo, hi = c
        mid = lo + (hi - lo) * 0.5
        ok = count_ge(mid) >= float(CAP)
        return jnp.where(ok, mid, lo), jnp.where(ok, hi, mid)

    lo, hi = lax.fori_loop(0, 32, refine, (lo, hi))
    gt = aff >= hi
    eq = (aff >= lo) & jnp.logical_not(gt)
    need = float(CAP) - jnp.sum(jnp.where(gt, 1.0, 0.0), axis=1, keepdims=True)
    tri = tri_ref[...]

    def excl_cumsum(mask_f, record_offsets):
        carry = jnp.zeros((NE, 1), F32)
        for c in range(NTCH):
            sl = slice(c * TCH, (c + 1) * TCH)
            m = mask_f[:, sl]
            inc = _dot(m.astype(BF16), tri)
            cs_ref[:, sl] = inc - m + carry
            if record_offsets:
                off_ref[0, :, c:c + 1] = carry.astype(I32)
            carry = carry + inc[:, TCH - 1:TCH]
        return cs_ref[...]

    eq_rank = excl_cumsum(jnp.where(eq, 1.0, 0.0), False)
    sel = gt | (eq & (eq_rank < need))
    pos = excl_cumsum(jnp.where(sel, 1.0, 0.0), True)
    pos_ref[0] = jnp.where(sel, pos.astype(I32), -1)


def _routing(logits, tri):
    b = logits.shape[0]
    blk = pl.BlockSpec((1, NE, SEQ), lambda bi: (bi, 0, 0))
    return pl.pallas_call(
        _routing_body,
        grid=(b,),
        in_specs=[blk, pl.BlockSpec((TCH, TCH), lambda bi: (0, 0))],
        out_specs=[blk, blk, pl.BlockSpec((1, NE, NTCH), lambda bi: (bi, 0, 0))],
        out_shape=[jax.ShapeDtypeStruct((b, NE, SEQ), I32),
                   jax.ShapeDtypeStruct((b, NE, SEQ), F32),
                   jax.ShapeDtypeStruct((b, NE, NTCH), I32)],
        scratch_shapes=[pltpu.VMEM((NE, SEQ), F32)],
        compiler_params=_cparams(("parallel",)),
        name="routing",
    )(logits, tri)


GATHER_UNROLL = 4


def _gather_body(off_ref, h_ref, pos_ref, xg_ref, acc_ref):
    b = pl.program_id(0)
    e = pl.program_id(1)
    acc_ref[...] = jnp.zeros_like(acc_ref)
    crow = lax.broadcasted_iota(I32, (GW, TCH), 0)

    def chunks(i, _):
        for j in range(GATHER_UNROLL):
            c = i * GATHER_UNROLL + j
            off = off_ref[(b * NE + e) * NTCH + c]
            base = pl.multiple_of(jnp.minimum((off >> 3) << 3, CAP - TCH), SUBLANES)
            t0 = pl.multiple_of(c * TCH, TCH)
            rel = pos_ref[0, 0, pl.ds(c, 1), :] - base
            onehot = jnp.where(crow == rel, 1.0, 0.0).astype(BF16)
            acc_ref[pl.ds(base, GW), :] += _dot(onehot, h_ref[0, pl.ds(t0, TCH), :])
        return 0

    lax.fori_loop(0, NTCH // GATHER_UNROLL, chunks, 0)
    xg_ref[0, 0] = acc_ref[0:CAP, :].astype(BF16)


def _gather(offs_flat, h2, pos4):
    b = h2.shape[0]
    grid_spec = pltpu.PrefetchScalarGridSpec(
        num_scalar_prefetch=1,
        grid=(b, NE),
        in_specs=[pl.BlockSpec((1, SEQ, D), lambda bi, e, off: (bi, 0, 0)),
                  pl.BlockSpec((1, 1, NTCH, TCH), lambda bi, e, off: (bi, e, 0, 0))],
        out_specs=pl.BlockSpec((1, 1, CAP, D), lambda bi, e, off: (bi, e, 0, 0)),
        scratch_shapes=[pltpu.VMEM((CAP + SUBLANES, D), F32)],
    )
    return pl.pallas_call(
        _gather_body,
        grid_spec=grid_spec,
        out_shape=jax.ShapeDtypeStruct((b, NE, CAP, D), BF16),
        compiler_params=_cparams(("parallel", "arbitrary")),
        name="moe_gather",
    )(offs_flat, h2, pos4)


FFN_TM = 256
FFN_FC = 768
FFN_NF = DEXP // FFN_FC
FFN_FT = DEXP - FFN_NF * FFN_FC
assert FFN_FC % LANES == 0 and 0 < FFN_FT < FFN_FC


def _ffn_body(xg_ref, wg_ref, wu_ref, wd_ref, wgt_ref, wut_ref, wdt_ref, y_ref, acc_ref):
    j = pl.program_id(2)

    @pl.when(j == 0)
    def _():
        acc_ref[...] = jnp.zeros_like(acc_ref)

    def sweep(wg, wu, wd):
        for mb in range(CAP // FFN_TM):
            rows = slice(mb * FFN_TM, (mb + 1) * FFN_TM)
            xb = xg_ref[0, 0, rows, :]
            a = _dot(xb, wg)
            u = _dot(xb, wu)
            h = (a * (1.0 / (1.0 + jnp.exp(-a))) * u).astype(BF16)
            acc_ref[rows, :] += _dot(h, wd)

    @pl.when(j < FFN_NF)
    def _():
        sweep(wg_ref[0].astype(BF16), wu_ref[0].astype(BF16), wd_ref[0].astype(BF16))

    @pl.when(j == FFN_NF)
    def _():
        sweep(wgt_ref[0], wut_ref[0], wdt_ref[0])
        y_ref[0, 0, 0:CAP, :] = acc_ref[...].astype(BF16)
        y_ref[0, 0, CAP:YROWS, :] = jnp.zeros((YROWS - CAP, D), BF16)


def _expert_ffn(xg, w_gate, w_up, w_down):
    b = xg.shape[0]
    f0 = FFN_NF * FFN_FC
    tails = (w_gate[:, :, f0:].astype(BF16), w_up[:, :, f0:].astype(BF16), w_down[:, f0:, :].astype(BF16))
    jc = lambda j: jnp.minimum(j, FFN_NF - 1)
    return pl.pallas_call(
        _ffn_body,
        grid=(NE, b, FFN_NF + 1),
        in_specs=[pl.BlockSpec((1, 1, CAP, D), lambda e, bi, j: (bi, e, 0, 0)),
                  pl.BlockSpec((1, D, FFN_FC), lambda e, bi, j: (e, 0, jc(j))),
                  pl.BlockSpec((1, D, FFN_FC), lambda e, bi, j: (e, 0, jc(j))),
                  pl.BlockSpec((1, FFN_FC, D), lambda e, bi, j: (e, jc(j), 0)),
                  pl.BlockSpec((1, D, FFN_FT), lambda e, bi, j: (e, 0, 0)),
                  pl.BlockSpec((1, D, FFN_FT), lambda e, bi, j: (e, 0, 0)),
                  pl.BlockSpec((1, FFN_FT, D), lambda e, bi, j: (e, 0, 0))],
        out_specs=pl.BlockSpec((1, 1, YROWS, D), lambda e, bi, j: (bi, e, 0, 0)),
        out_shape=jax.ShapeDtypeStruct((b, NE, YROWS, D), BF16),
        scratch_shapes=[pltpu.VMEM((CAP, D), F32)],
        compiler_params=_cparams(("parallel", "parallel", "arbitrary")),
        name="expert_ffn",
    )(xg, w_gate, w_up, w_down, *tails)


def _combine_body(off_ref, y_ref, pos_ref, gate_ref, x1_ref, g2_ref, o_ref):
    b = pl.program_id(0)
    i = pl.program_id(1)
    crow = lax.broadcasted_iota(I32, (CW, TCH), 0)
    acc = jnp.zeros((TCH, D), F32)
    for e in range(NE):
        off = off_ref[(b * NE + e) * NTCH + i]
        base = pl.multiple_of(jnp.minimum((off >> 4) << 4, CAP - TCH), BF16_ROWS)
        rel = pos_ref[0, e:e + 1, :] - base
        w = jnp.where(crow == rel, gate_ref[0, e:e + 1, :], 0.0)
        yw = y_ref[0, e, pl.ds(base, CW), :]
        acc = acc + _dot(w.astype(BF16), yw, _TN)
    o_ref[0] = x1_ref[0] + g2_ref[0] * acc


def _combine(offs_flat, y, pos, gate, x1, g2r):
    b = x1.shape[0]
    grid_spec = pltpu.PrefetchScalarGridSpec(
        num_scalar_prefetch=1,
        grid=(b, NTCH),
        in_specs=[pl.BlockSpec((1, NE, YROWS, D), lambda bi, i, off: (bi, 0, 0, 0),
                               pipeline_mode=pl.Buffered(1)),
                  pl.BlockSpec((1, NE, TCH), lambda bi, i, off: (bi, 0, i)),
                  pl.BlockSpec((1, NE, TCH), lambda bi, i, off: (bi, 0, i)),
                  pl.BlockSpec((1, TCH, D), lambda bi, i, off: (bi, i, 0)),
                  pl.BlockSpec((1, 1, D), lambda bi, i, off: (bi, 0, 0))],
        out_specs=pl.BlockSpec((1, TCH, D), lambda bi, i, off: (bi, i, 0)),
    )
    return pl.pallas_call(
        _combine_body,
        grid_spec=grid_spec,
        out_shape=jax.ShapeDtypeStruct((b, SEQ, D), F32),
        compiler_params=_cparams(("parallel", "arbitrary")),
        name="moe_combine",
    )(offs_flat, y, pos, gate, x1, g2r)


def _np_split(m):
    hi = np.asarray(m, np.float64).astype(BF16)
    lo = (m - hi.astype(np.float64)).astype(BF16)
    return jnp.asarray(hi), jnp.asarray(lo)


@functools.lru_cache(maxsize=None)
def _dft_tables():
    a = np.arange(FN1, dtype=np.float64)
    ang = 2.0 * np.pi * np.outer(a, a) / FN1
    fr, fi = np.cos(ang), -np.sin(ang)
    half = SEQ // FN2
    lead_u = np.block([[fr[:, :half], -fi[:, :half]], [fi[:, :half], fr[:, :half]]])
    lead_k = np.concatenate([fr, fi], axis=0)
    fwd = np.block([[fr, -fi], [fi, fr]])
    inv = np.block([[fr, fi], [-fi, fr]])
    out = np.block([[fr[:half], fi[:half]], [-fi[:half], fr[:half]]])
    n2 = np.arange(FN2, dtype=np.float64)
    tw = 2.0 * np.pi * np.outer(a, n2) / FN
    twr = np.broadcast_to(np.cos(tw)[:, :, None], (FN1, FN2, LANES)).astype(np.float32)
    twi = np.broadcast_to(-np.sin(tw)[:, :, None], (FN1, FN2, LANES)).astype(np.float32)
    return dict(lead_u=lead_u, lead_k=lead_k, fwd=fwd, inv=inv, out=out, twr=twr, twi=twi)


@functools.lru_cache(maxsize=None)
def _filter_tables():
    L = SEQ
    r = np.arange(FN)
    j = np.where(r < L, r, FN - r)
    jc = np.minimum(j, L - 1).astype(np.float64)
    t = (jc / (L - 1))[:, None]
    bands = (FEMB - 1) // 2
    w = 2.0 * np.pi * jc / L
    f = np.linspace(1e-4, bands - 1, bands)
    fw = w[:, None] * f[None, :]
    z = np.concatenate([t, np.cos(fw), -np.sin(fw), np.zeros((FN, FORD - FEMB))], axis=-1)
    mask = np.where(r == L, 0.0, 1.0)[:, None]
    max_decay = math.log(DECAY_TARGET) / FAST_DECAY_PCT
    min_decay = math.log(DECAY_TARGET) / SLOW_DECAY_PCT
    negdelta = -np.abs(np.linspace(min_decay, max_decay, HY_W))[None, :]
    return tuple(np.asarray(a, np.float32) for a in (z, t, mask, negdelta))


@functools.lru_cache(maxsize=None)
def _rope_tables(n):
    rows = n // GRID_W
    row_id, col_id = np.meshgrid(np.arange(rows, dtype=np.float64), np.arange(GRID_W, dtype=np.float64), indexing="ij")
    quarter = HD // 4
    inv_freq = ROPE_THETA ** (-np.arange(quarter, dtype=np.float64) / quarter)
    ar = row_id.reshape(-1)[:, None] * inv_freq
    ac = col_id.reshape(-1)[:, None] * inv_freq
    cos = np.concatenate([np.cos(ar), np.cos(ar), np.cos(ac), np.cos(ac)], axis=-1)
    sin = np.concatenate([-np.sin(ar), np.sin(ar), -np.sin(ac), np.sin(ac)], axis=-1)
    reps = (1, LANES // HD)
    return np.tile(cos, reps).astype(np.float32), np.tile(sin, reps).astype(np.float32)


def _hyena_long_conv(u, x2c, kern, abs_sum, bias):
    tb = _dft_tables()
    half = SEQ // FN2
    lanes = FN2 * HY_W
    twr, twi = jnp.asarray(tb["twr"]), jnp.asarray(tb["twi"])
    fwd_h, fwd_l = _np_split(tb["fwd"])
    inv_h, inv_l = _np_split(tb["inv"])
    ak = _dft_lead(kern.reshape(1, FN1, lanes), *_np_split(tb["lead_k"]))
    scale = 1.0 / (abs_sum * float(FN))
    khat = _filter_spectrum(ak.reshape(2, FN1, FN2, HY_W), twr, twi, fwd_h, fwd_l, scale)
    u3 = u.reshape(2, half, lanes)
    au = _dft_lead(u3, *_np_split(tb["lead_u"]))
    dd = _conv_mid(au.reshape(2, FN1, FN2, HY_W), twr, twi, fwd_h, fwd_l, inv_h, inv_l, khat)
    bias_t = jnp.tile(bias.reshape(1, HY_W), (1, HY_LT // HY_W))
    hy = _conv_out(dd.reshape(2 * FN1, lanes), *_np_split(tb["out"]), u3, x2c.reshape(2, half, lanes), bias_t)
    return hy.reshape(2, SEQ, HY_W)


def kernel(x, c, ctx, c_ctx, w_mod, b_mod, norm1_g, norm2_g, w_in, w_out, q_norm_g, k_norm_g,
           conv_w, conv_b, filt_w1, filt_b1, filt_w2, filt_b2, filt_w3, filt_freq, hyena_bias,
           w_router, w_gate, w_up, w_down):
    B = x.shape[0]
    assert x.shape == (B, SEQ, D) and B == 2 and ctx.shape == (B, CTX, D) and w_mod.shape[0] == 1
    l = 0

    cc = jnp.concatenate([c, c_ctx[None, :], jnp.zeros((SUBLANES - B - 1, D), F32)], axis=0)
    mod = _modulation(cc, w_mod[l], b_mod[l][None, :])
    sh1, sc1, g1, sh2, sc2, g2 = [mod[:, i * D:(i + 1) * D] for i in range(6)]
    lat = lambda m: m[:B, None, :]
    ctxrow = lambda m: jnp.broadcast_to(m[B:B + 1, None, :], (B, 1, D))

    w_in_bf = w_in[l].astype(BF16)
    gq2 = jnp.tile(q_norm_g[l][None, :], (1, LANES // HD))
    gk2 = jnp.tile(k_norm_g[l][None, :], (1, LANES // HD))
    bd = jnp.asarray(np.kron(np.eye(LANES // HD), np.full((HD, HD), 1.0 / HD)), BF16)
    cos_t, sin_t = _rope_tables(SEQ)
    n1g = norm1_g[l][None, :]

    q, k, v, p = _in_projection(x, n1g, lat(sh1), lat(sc1), w_in_bf, gq2, gk2, bd, cos_t, sin_t, 512)
    _, kc, vc, _ = _in_projection(ctx, n1g, ctxrow(sh1), ctxrow(sc1), w_in_bf, gq2, gk2, bd,
                                  jnp.ones((CTX, LANES), F32), jnp.zeros((CTX, LANES), F32), CTX)

    k_all = jnp.concatenate([k, kc], axis=1).reshape(B, ATT_NCH, ATT_TK, NKV, HD)
    v_all = jnp.concatenate([v, vc], axis=1).reshape(B, ATT_NCH, ATT_TK, NKV, HD)
    kch = k_all.transpose(0, 3, 1, 2, 4)
    ones_pad = jnp.concatenate([jnp.ones((B, NKV, ATT_NCH, 1, ATT_TK), BF16),
                                jnp.zeros((B, NKV, ATT_NCH, BF16_ROWS - 1, ATT_TK), BF16)], axis=3)
    vtch = jnp.concatenate([v_all.transpose(0, 3, 1, 4, 2), ones_pad], axis=3)
    att = _attention(q, kch, vtch)

    cw9 = conv_w[l].reshape(3, 3, HY_W).reshape(9, HY_W)
    cb3 = conv_b[l].reshape(3, HY_W)
    u, x2c = _short_conv(p, cw9, cb3)
    ztab, ttab, mtab, negdelta = _filter_tables()
    w1p = jnp.concatenate([filt_w1[l], jnp.zeros((FORD - FEMB, FORD), F32)], axis=0)
    kern, abs_sum = _implicit_filter(ztab, ttab, mtab, w1p, filt_b1[l][None, :], filt_w2[l],
                                     filt_b2[l][None, :], filt_w3[l], filt_freq[l][None, :], negdelta)
    hy = _hyena_long_conv(u, x2c, kern, abs_sum, hyena_bias[l])

    wrh, wrl = _split(w_router[l].T)
    x1, h2, logits = _out_projection(att, hy, x, w_out[l].astype(BF16), lat(g1), norm2_g[l][None, :],
                                     lat(sh2), lat(sc2), wrh, wrl)

    tri = jnp.asarray(np.triu(np.ones((TCH, TCH))), BF16)
    pos, gate, offs = _routing(logits, tri)
    offs_flat = offs.reshape(-1)
    xg = _gather(offs_flat, h2, pos.reshape(B, NE, NTCH, TCH))
    y = _expert_ffn(xg, w_gate[l], w_up[l], w_down[l])
    return _combine(offs_flat, y, pos, gate, x1, lat(g2))
```

```python
import functools
import math

import numpy as np
import jax
import jax.numpy as jnp
from jax import lax
from jax.experimental import pallas as pl
from jax.experimental.pallas import tpu as pltpu

F32 = jnp.float32
BF16 = jnp.bfloat16
I32 = jnp.int32

D = 1024
SEQ = 8192
CTX = 256
GRID_W = 64
ATT_W = 512
HY_W = 512
HD = 64
NQ = 8
NKV = 2
QPK = NQ // NKV
KV_W = NKV * HD
IN_W = ATT_W + 2 * KV_W + 3 * HY_W
FEMB = 33
FORD = 64
NE = 16
CAP = 2 * SEQ // NE
DEXP = 2752
ROPE_THETA = 10000.0
EPS = 1e-6
DECAY_TARGET = 1e-2
FAST_DECAY_PCT = 0.3
SLOW_DECAY_PCT = 1.5

LANES = 128
SUBLANES = 8
BF16_ROWS = 16
VMEM_BYTES_V7X = 64 * 1024 * 1024
VMEM_LIMIT = VMEM_BYTES_V7X - 8 * 1024 * 1024

FN = 2 * SEQ
FN1 = 128
FN2 = 128

TCH = LANES
NTCH = SEQ // TCH
GW = TCH + SUBLANES
CW = TCH + BF16_ROWS
YROWS = CAP + BF16_ROWS


def _cparams(sem, vmem=None):
    return pltpu.CompilerParams(dimension_semantics=sem, vmem_limit_bytes=vmem or VMEM_LIMIT)


def _split(a):
    hi = a.astype(BF16)
    lo = (a - hi.astype(F32)).astype(BF16)
    return hi, lo


_NN = (((1,), (0,)), ((), ()))
_NT = (((1,), (1,)), ((), ()))
_TN = (((0,), (0,)), ((), ()))


def _dot(a, b, dn=_NN):
    return lax.dot_general(a, b, dn, preferred_element_type=F32)


def _dot3(a, b, dn=_NN):
    ah, al = _split(a)
    bh, bl = _split(b)
    return _dot(ah, bh, dn) + _dot(ah, bl, dn) + _dot(al, bh, dn)


def _dot3c(fh, fl, z):
    zh, zl = _split(z)
    return _dot(fh, zh) + _dot(fh, zl) + _dot(fl, zh)


def _mod_body(c_ref, w_ref, b_ref, o_ref):
    c = c_ref[...]
    s = c * (1.0 / (1.0 + jnp.exp(-c)))
    o_ref[...] = _dot3(s, w_ref[...]) + b_ref[...]


def _modulation(cc, w_mod, b_mod):
    n = w_mod.shape[1]
    return pl.pallas_call(
        _mod_body,
        grid=(n // D,),
        in_specs=[pl.BlockSpec((SUBLANES, D), lambda j: (0, 0)),
                  pl.BlockSpec((D, D), lambda j: (0, j)),
                  pl.BlockSpec((1, D), lambda j: (0, j))],
        out_specs=pl.BlockSpec((SUBLANES, D), lambda j: (0, j)),
        out_shape=jax.ShapeDtypeStruct((SUBLANES, n), F32),
        compiler_params=_cparams(("arbitrary",)),
        name="modulation",
    )(cc, w_mod, b_mod)


Q_SCALE = HD ** -0.5 * math.log2(math.e)


def _rms_mod(x, g, sh, sc):
    ms = jnp.mean(x * x, axis=-1, keepdims=True)
    return (x * lax.rsqrt(ms + EPS) * g) * (1.0 + sc) + sh


def _head_norm_rope(t, g, bd, cos, sin):
    sq = t * t
    hi, lo = _split(sq)
    ms = _dot(hi, bd) + _dot(lo, bd)
    tn = t * lax.rsqrt(ms + EPS) * g
    lane = lax.broadcasted_iota(I32, tn.shape, 1)
    sw = jnp.where((lane & 31) < 16, pltpu.roll(tn, LANES - 16, 1), pltpu.roll(tn, 16, 1))
    return tn * cos + sw * sin


def _proj_body(x_ref, g_ref, sh_ref, sc_ref, w_ref, gq_ref, gk_ref, bd_ref, cos_ref, sin_ref,
               q_ref, k_ref, v_ref, p_ref):
    h = _rms_mod(x_ref[0], g_ref[...], sh_ref[0], sc_ref[0])
    proj = _dot(h.astype(BF16), w_ref[...])
    bd = bd_ref[...]
    cos = cos_ref[...]
    sin = sin_ref[...]
    for j in range(ATT_W // LANES):
        sl = slice(j * LANES, (j + 1) * LANES)
        qj = _head_norm_rope(proj[:, sl], gq_ref[...], bd, cos, sin)
        q_ref[0, :, sl] = (qj * Q_SCALE).astype(BF16)
    k_ref[0] = _head_norm_rope(proj[:, ATT_W:ATT_W + KV_W], gk_ref[...], bd, cos, sin).astype(BF16)
    v_ref[0] = proj[:, ATT_W + KV_W:ATT_W + 2 * KV_W].astype(BF16)
    p_ref[0] = proj[:, ATT_W + 2 * KV_W:]


def _in_projection(x, g1, sh, sc, w_in_bf, gq2, gk2, bd, cos_t, sin_t, tm):
    b, s, _ = x.shape
    row = lambda bi, i: (bi, 0, 0)
    tok = lambda bi, i: (bi, i, 0)
    const = lambda bi, i: (0, 0)
    return pl.pallas_call(
        _proj_body,
        grid=(b, s // tm),
        in_specs=[pl.BlockSpec((1, tm, D), tok),
                  pl.BlockSpec((1, D), const),
                  pl.BlockSpec((1, 1, D), row),
                  pl.BlockSpec((1, 1, D), row),
                  pl.BlockSpec((D, IN_W), const),
                  pl.BlockSpec((1, LANES), const),
                  pl.BlockSpec((1, LANES), const),
                  pl.BlockSpec((LANES, LANES), const),
                  pl.BlockSpec((tm, LANES), lambda bi, i: (i, 0)),
                  pl.BlockSpec((tm, LANES), lambda bi, i: (i, 0))],
        out_specs=[pl.BlockSpec((1, tm, ATT_W), tok),
                   pl.BlockSpec((1, tm, KV_W), tok),
                   pl.BlockSpec((1, tm, KV_W), tok),
                   pl.BlockSpec((1, tm, 3 * HY_W), tok)],
        out_shape=[jax.ShapeDtypeStruct((b, s, ATT_W), BF16),
                   jax.ShapeDtypeStruct((b, s, KV_W), BF16),
                   jax.ShapeDtypeStruct((b, s, KV_W), BF16),
                   jax.ShapeDtypeStruct((b, s, 3 * HY_W), F32)],
        compiler_params=_cparams(("parallel", "parallel")),
        name="in_projection",
    )(x, g1, sh, sc, w_in_bf, gq2, gk2, bd, cos_t, sin_t)


ATT_TQ = 256
ATT_TK = 768
SK = SEQ + CTX
ATT_NCH = SK // ATT_TK


ATT_NQ = QPK * ATT_TQ
ATT_VR = HD + BF16_ROWS
assert ATT_NCH % 2 == 1


def _attn_body(q_ref, k_ref, vt_ref, o_ref, s0_ref, s1_ref, mx0_ref, mx1_ref, m_ref, acc_ref):
    qall = jnp.concatenate([q_ref[0, :, r * HD:(r + 1) * HD] for r in range(QPK)], axis=0)
    m_ref[...] = jnp.full(m_ref.shape, -1e30, F32)
    acc_ref[...] = jnp.zeros_like(acc_ref)

    def scores(c, s_ref, mx_ref):
        s = _dot(k_ref[0, 0, c], qall, _NT)
        s_ref[...] = s
        mx_ref[...] = jnp.max(s, axis=0, keepdims=True)

    def update(c, s_ref, mx_ref):
        m_old = m_ref[...]
        m_new = jnp.maximum(m_old, mx_ref[...])
        p = jnp.exp2(s_ref[...] - m_new).astype(BF16)
        acc_ref[...] = jnp.exp2(m_old - m_new) * acc_ref[...] + _dot(vt_ref[0, 0, c], p)
        m_ref[...] = m_new

    scores(0, s0_ref, mx0_ref)

    def pair(i, _):
        c = 2 * i
        scores(c + 1, s1_ref, mx1_ref)
        update(c, s0_ref, mx0_ref)
        scores(c + 2, s0_ref, mx0_ref)
        update(c + 1, s1_ref, mx1_ref)
        return 0

    lax.fori_loop(0, ATT_NCH // 2, pair, 0)
    update(ATT_NCH - 1, s0_ref, mx0_ref)
    out = acc_ref[0:HD, :] * (1.0 / acc_ref[HD:HD + 1, :])
    for r in range(QPK):
        o_ref[0, :, r * HD:(r + 1) * HD] = out[:, r * ATT_TQ:(r + 1) * ATT_TQ].T.astype(BF16)


def _attention(q, kch, vtch):
    b = q.shape[0]
    return pl.pallas_call(
        _attn_body,
        grid=(b, NKV, SEQ // ATT_TQ),
        in_specs=[pl.BlockSpec((1, ATT_TQ, QPK * HD), lambda bi, g, i: (bi, i, g)),
                  pl.BlockSpec((1, 1, ATT_NCH, ATT_TK, HD), lambda bi, g, i: (bi, g, 0, 0, 0)),
                  pl.BlockSpec((1, 1, ATT_NCH, ATT_VR, ATT_TK), lambda bi, g, i: (bi, g, 0, 0, 0))],
        out_specs=pl.BlockSpec((1, ATT_TQ, QPK * HD), lambda bi, g, i: (bi, i, g)),
        out_shape=jax.ShapeDtypeStruct((b, SEQ, ATT_W), BF16),
        scratch_shapes=[pltpu.VMEM((ATT_TK, ATT_NQ), F32), pltpu.VMEM((ATT_TK, ATT_NQ), F32),
                        pltpu.VMEM((1, ATT_NQ), F32), pltpu.VMEM((1, ATT_NQ), F32),
                        pltpu.VMEM((1, ATT_NQ), F32), pltpu.VMEM((ATT_VR, ATT_NQ), F32)],
        compiler_params=_cparams(("parallel", "parallel", "parallel")),
        name="attention",
    )(q, kch, vtch)


SC_TM = 1024


def _sconv_body(m1, a1, n1, m2, a2, n2, m3, a3, n3, w_ref, b_ref, u_ref, x2_ref):
    i = pl.program_id(1)
    last = pl.num_programs(1) - 1
    rows = lax.broadcasted_iota(I32, (SC_TM, HY_W), 0)

    def conv(main, prev, nxt, g):
        x = main[0]
        pr = jnp.where(i > 0, prev[0, SUBLANES - 1:SUBLANES, :], 0.0)
        nx = jnp.where(i < last, nxt[0, 0:1, :], 0.0)
        xm = jnp.where(rows == 0, pr, pltpu.roll(x, 1, 0))
        xp = jnp.where(rows == SC_TM - 1, nx, pltpu.roll(x, SC_TM - 1, 0))
        return (w_ref[g:g + 1, :] * xm + w_ref[3 + g:4 + g, :] * x + w_ref[6 + g:7 + g, :] * xp
                + b_ref[g:g + 1, :])

    x1 = conv(m1, a1, n1, 0)
    x2 = conv(m2, a2, n2, 1)
    v = conv(m3, a3, n3, 2)
    u_ref[0] = v * x1
    x2_ref[0] = x2


def _short_conv(p, cw9, cb3):
    b = p.shape[0]
    nblk8 = SEQ // SUBLANES
    step8 = SC_TM // SUBLANES
    specs = []
    for g in range(3):
        specs += [pl.BlockSpec((1, SC_TM, HY_W), lambda bi, i, g=g: (bi, i, g)),
                  pl.BlockSpec((1, SUBLANES, HY_W), lambda bi, i, g=g: (bi, jnp.maximum(i * step8 - 1, 0), g)),
                  pl.BlockSpec((1, SUBLANES, HY_W), lambda bi, i, g=g: (bi, jnp.minimum((i + 1) * step8, nblk8 - 1), g))]
    specs += [pl.BlockSpec((9, HY_W), lambda bi, i: (0, 0)), pl.BlockSpec((3, HY_W), lambda bi, i: (0, 0))]
    out = pl.BlockSpec((1, SC_TM, HY_W), lambda bi, i: (bi, i, 0))
    return pl.pallas_call(
        _sconv_body,
        grid=(b, SEQ // SC_TM),
        in_specs=specs,
        out_specs=[out, out],
        out_shape=[jax.ShapeDtypeStruct((b, SEQ, HY_W), F32)] * 2,
        compiler_params=_cparams(("parallel", "parallel")),
        name="short_conv",
    )(p, p, p, p, p, p, p, p, p, cw9, cb3)


FILT_TR = 1024


def _filter_body(z_ref, t_ref, msk_ref, w1_ref, b1_ref, w2_ref, b2_ref, w3_ref, fr_ref, dl_ref,
                 k_ref, s_ref):
    fr = fr_ref[...]
    h = jnp.sin(fr * (_dot3(z_ref[...], w1_ref[...]) + b1_ref[...]))
    h = jnp.sin(fr * (_dot3(h, w2_ref[...]) + b2_ref[...]))
    h = _dot3(h, w3_ref[...])
    kern = h * jnp.exp(t_ref[...] * dl_ref[...]) * msk_ref[...]
    k_ref[...] = kern

    @pl.when((pl.program_id(0) == 0) & (pl.program_id(1) == 0))
    def _():
        s_ref[...] = jnp.zeros_like(s_ref)

    s_ref[...] += jnp.sum(jnp.abs(kern), axis=0, keepdims=True)


def _implicit_filter(ztab, ttab, mtab, w1p, b1, w2, b2, w3, freq, negdelta):
    nt = SEQ // FILT_TR
    rowblk = lambda hf, i: (hf * nt + i, 0)
    const = lambda hf, i: (0, 0)
    return pl.pallas_call(
        _filter_body,
        grid=(2, nt),
        in_specs=[pl.BlockSpec((FILT_TR, FORD), rowblk),
                  pl.BlockSpec((FILT_TR, 1), rowblk),
                  pl.BlockSpec((FILT_TR, 1), rowblk),
                  pl.BlockSpec((FORD, FORD), const),
                  pl.BlockSpec((1, FORD), const),
                  pl.BlockSpec((FORD, FORD), const),
                  pl.BlockSpec((1, FORD), const),
                  pl.BlockSpec((FORD, HY_W), lambda hf, i: (0, hf)),
                  pl.BlockSpec((1, FORD), const),
                  pl.BlockSpec((1, HY_W), const)],
        out_specs=[pl.BlockSpec((FILT_TR, HY_W), rowblk),
                   pl.BlockSpec((1, HY_W), const)],
        out_shape=[jax.ShapeDtypeStruct((FN, HY_W), F32), jax.ShapeDtypeStruct((1, HY_W), F32)],
        compiler_params=_cparams(("arbitrary", "arbitrary")),
        name="implicit_filter",
    )(ztab, ttab, mtab, w1p, b1, w2, b2, w3, freq, negdelta)


DFT_LT = 4096


def _dft_lead_body(z_ref, fh_ref, fl_ref, o_ref):
    nb = z_ref.shape[0]
    z = jnp.concatenate([z_ref[j] for j in range(nb)], axis=0) if nb > 1 else z_ref[0]
    o_ref[...] = _dot3c(fh_ref[...], fl_ref[...], z)


def _dft_lead(z, fh, fl):
    nb, rows, lanes = z.shape
    m = fh.shape[0]
    return pl.pallas_call(
        _dft_lead_body,
        grid=(lanes // DFT_LT,),
        in_specs=[pl.BlockSpec((nb, rows, DFT_LT), lambda j: (0, 0, j)),
                  pl.BlockSpec(fh.shape, lambda j: (0, 0)),
                  pl.BlockSpec(fl.shape, lambda j: (0, 0))],
        out_specs=pl.BlockSpec((m, DFT_LT), lambda j: (0, j)),
        out_shape=jax.ShapeDtypeStruct((m, lanes), F32),
        compiler_params=_cparams(("parallel",)),
        name="dft_lead",
    )(z, fh, fl)


DFT_KB = 4


def _twiddle(tr_ref, ti_ref, j):
    reps = HY_W // LANES
    tr = jnp.concatenate([tr_ref[j]] * reps, axis=1)
    ti = jnp.concatenate([ti_ref[j]] * reps, axis=1)
    return tr, ti


def _spectrum_body(a_ref, tr_ref, ti_ref, fh_ref, fl_ref, sc_ref, o_ref):
    for j in range(DFT_KB):
        ar, ai = a_ref[0, j], a_ref[1, j]
        tr, ti = _twiddle(tr_ref, ti_ref, j)
        b = jnp.concatenate([ar * tr - ai * ti, ar * ti + ai * tr], axis=0)
        x = _dot3c(fh_ref[...], fl_ref[...], b) * sc_ref[...]
        o_ref[0, j] = x[:FN2]
        o_ref[1, j] = x[FN2:]


def _conv_mid_body(a_ref, tr_ref, ti_ref, fh_ref, fl_ref, gh_ref, gl_ref, k_ref, o_ref):
    for j in range(DFT_KB):
        ar, ai = a_ref[0, j], a_ref[1, j]
        tr, ti = _twiddle(tr_ref, ti_ref, j)
        b = jnp.concatenate([ar * tr - ai * ti, ar * ti + ai * tr], axis=0)
        x = _dot3c(fh_ref[...], fl_ref[...], b)
        xr, xi = x[:FN2], x[FN2:]
        kr, ki = k_ref[0, j], k_ref[1, j]
        y = jnp.concatenate([xr * kr - xi * ki, xr * ki + xi * kr], axis=0)
        c = _dot3c(gh_ref[...], gl_ref[...], y)
        cr, ci = c[:FN2], c[FN2:]
        o_ref[0, j] = cr * tr + ci * ti
        o_ref[1, j] = ci * tr - cr * ti


def _dft_mid_specs():
    slab = pl.BlockSpec((2, DFT_KB, FN2, HY_W), lambda i: (0, i, 0, 0))
    tw = pl.BlockSpec((DFT_KB, FN2, LANES), lambda i: (i, 0, 0))
    mat = pl.BlockSpec((2 * FN2, 2 * FN2), lambda i: (0, 0))
    return slab, tw, mat


def _filter_spectrum(a4, twr, twi, fh, fl, scale):
    slab, tw, mat = _dft_mid_specs()
    return pl.pallas_call(
        _spectrum_body,
        grid=(FN1 // DFT_KB,),
        in_specs=[slab, tw, tw, mat, mat, pl.BlockSpec((1, HY_W), lambda i: (0, 0))],
        out_specs=slab,
        out_shape=jax.ShapeDtypeStruct((2, FN1, FN2, HY_W), F32),
        compiler_params=_cparams(("parallel",)),
        name="filter_spectrum",
    )(a4, twr, twi, fh, fl, scale)


def _conv_mid(a4, twr, twi, fh, fl, gh, gl, khat):
    slab, tw, mat = _dft_mid_specs()
    return pl.pallas_call(
        _conv_mid_body,
        grid=(FN1 // DFT_KB,),
        in_specs=[slab, tw, tw, mat, mat, mat, mat, slab],
        out_specs=slab,
        out_shape=jax.ShapeDtypeStruct((2, FN1, FN2, HY_W), F32),
        compiler_params=_cparams(("parallel",)),
        name="conv_mid",
    )(a4, twr, twi, fh, fl, gh, gl, khat)


HY_LT = 2048


def _conv_out_body(d_ref, fh_ref, fl_ref, u_ref, x2_ref, bias_ref, o_ref):
    y = _dot3c(fh_ref[...], fl_ref[...], d_ref[...])
    half = SEQ // FN2
    bias = bias_ref[...]
    for b in range(2):
        yb = y[b * half:(b + 1) * half]
        o_ref[b] = ((yb + u_ref[b] * bias) * x2_ref[b]).astype(BF16)


def _conv_out(d2, fh, fl, u3, x23, bias_t):
    half = SEQ // FN2
    lanes = d2.shape[1]
    io = pl.BlockSpec((2, half, HY_LT), lambda j: (0, 0, j))
    return pl.pallas_call(
        _conv_out_body,
        grid=(lanes // HY_LT,),
        in_specs=[pl.BlockSpec((2 * FN1, HY_LT), lambda j: (0, j)),
                  pl.BlockSpec(fh.shape, lambda j: (0, 0)),
                  pl.BlockSpec(fl.shape, lambda j: (0, 0)),
                  io, io,
                  pl.BlockSpec((1, HY_LT), lambda j: (0, 0))],
        out_specs=io,
        out_shape=jax.ShapeDtypeStruct((2, half, lanes), BF16),
        compiler_params=_cparams(("parallel",)),
        name="conv_out",
    )(d2, fh, fl, u3, x23, bias_t)


OP_TM = 512


def _outproj_body(att_ref, hy_ref, x_ref, w_ref, g1_ref, n2_ref, sh_ref, sc_ref, wrh_ref, wrl_ref,
                  x1_ref, h2_ref, lg_ref):
    a = jnp.concatenate([att_ref[0], hy_ref[0]], axis=1)
    x1 = x_ref[0] + g1_ref[0] * _dot(a, w_ref[...])
    x1_ref[0] = x1
    h2 = _rms_mod(x1, n2_ref[...], sh_ref[0], sc_ref[0])
    hh, hl = _split(h2)
    h2_ref[0] = hh
    wrh = wrh_ref[...]
    lg_ref[0] = _dot(wrh, hh, _NT) + _dot(wrh, hl, _NT) + _dot(wrl_ref[...], hh, _NT)


def _out_projection(att, hy, x, w_out_bf, g1r, n2g, sh2, sc2, wrh, wrl):
    b = x.shape[0]
    tok = lambda bi, i: (bi, i, 0)
    row = lambda bi, i: (bi, 0, 0)
    const = lambda bi, i: (0, 0)
    return pl.pallas_call(
        _outproj_body,
        grid=(b, SEQ // OP_TM),
        in_specs=[pl.BlockSpec((1, OP_TM, ATT_W), tok),
                  pl.BlockSpec((1, OP_TM, HY_W), tok),
                  pl.BlockSpec((1, OP_TM, D), tok),
                  pl.BlockSpec((ATT_W + HY_W, D), const),
                  pl.BlockSpec((1, 1, D), row),
                  pl.BlockSpec((1, D), const),
                  pl.BlockSpec((1, 1, D), row),
                  pl.BlockSpec((1, 1, D), row),
                  pl.BlockSpec((NE, D), const),
                  pl.BlockSpec((NE, D), const)],
        out_specs=[pl.BlockSpec((1, OP_TM, D), tok),
                   pl.BlockSpec((1, OP_TM, D), tok),
                   pl.BlockSpec((1, NE, OP_TM), lambda bi, i: (bi, 0, i))],
        out_shape=[jax.ShapeDtypeStruct((b, SEQ, D), F32),
                   jax.ShapeDtypeStruct((b, SEQ, D), BF16),
                   jax.ShapeDtypeStruct((b, NE, SEQ), F32)],
        compiler_params=_cparams(("parallel", "parallel")),
        name="out_projection",
    )(att, hy, x, w_out_bf, g1r, n2g, sh2, sc2, wrh, wrl)


def _routing_body(lg_ref, tri_ref, pos_ref, gate_ref, off_ref, cs_ref):
    lg = lg_ref[0]
    e = jnp.exp(lg - jnp.max(lg, axis=0, keepdims=True))
    aff = e / jnp.sum(e, axis=0, keepdims=True)
    gate_ref[0] = aff
    def count_ge(t):
        return jnp.sum(jnp.where(aff >= t, 1.0, 0.0), axis=1, keepdims=True)

    def bisect(i, thr):
        cand = thr | (jnp.int32(1) << (30 - i))
        return jnp.where(count_ge(pltpu.bitcast(cand, F32)) >= float(CAP), cand, thr)

    thr = lax.fori_loop(0, 31, bisect, jnp.zeros((NE, 1), I32))
    lo = pltpu.bitcast(thr, F32)
    hi = jnp.maximum(pltpu.bitcast(thr + 1, F32), jnp.finfo(F32).tiny)

    def refine(i, c):
        lo, hi = c
        mid = lo + (hi - lo) * 0.5
        ok = count_ge(mid) >= float(CAP)
        return jnp.where(ok, mid, lo), jnp.where(ok, hi, mid)

    lo, hi = lax.fori_loop(0, 32, refine, (lo, hi))
    gt = aff >= hi
    eq = (aff >= lo) & jnp.logical_not(gt)
    need = float(CAP) - jnp.sum(jnp.where(gt, 1.0, 0.0), axis=1, keepdims=True)
    tri = tri_ref[...]

    def excl_cumsum(mask_f, record_offsets):
        carry = jnp.zeros((NE, 1), F32)
        for c in range(NTCH):
            sl = slice(c * TCH, (c + 1) * TCH)
            m = mask_f[:, sl]
            inc = _dot(m.astype(BF16), tri)
            cs_ref[:, sl] = inc - m + carry
            if record_offsets:
                off_ref[0, :, c:c + 1] = carry.astype(I32)
            carry = carry + inc[:, TCH - 1:TCH]
        return cs_ref[...]

    eq_rank = excl_cumsum(jnp.where(eq, 1.0, 0.0), False)
    sel = gt | (eq & (eq_rank < need))
    pos = excl_cumsum(jnp.where(sel, 1.0, 0.0), True)
    pos_ref[0] = jnp.where(sel, pos.astype(I32), -1)


def _routing(logits, tri):
    b = logits.shape[0]
    blk = pl.BlockSpec((1, NE, SEQ), lambda bi: (bi, 0, 0))
    return pl.pallas_call(
        _routing_body,
        grid=(b,),
        in_specs=[blk, pl.BlockSpec((TCH, TCH), lambda bi: (0, 0))],
        out_specs=[blk, blk, pl.BlockSpec((1, NE, NTCH), lambda bi: (bi, 0, 0))],
        out_shape=[jax.ShapeDtypeStruct((b, NE, SEQ), I32),
                   jax.ShapeDtypeStruct((b, NE, SEQ), F32),
                   jax.ShapeDtypeStruct((b, NE, NTCH), I32)],
        scratch_shapes=[pltpu.VMEM((NE, SEQ), F32)],
        compiler_params=_cparams(("parallel",)),
        name="routing",
    )(logits, tri)


GATHER_UNROLL = 4


def _gather_body(off_ref, h_ref, pos_ref, xg_ref, acc_ref):
    b = pl.program_id(0)
    e = pl.program_id(1)
    acc_ref[...] = jnp.zeros_like(acc_ref)
    crow = lax.broadcasted_iota(I32, (GW, TCH), 0)

    def chunks(i, _):
        for j in range(GATHER_UNROLL):
            c = i * GATHER_UNROLL + j
            off = off_ref[(b * NE + e) * NTCH + c]
            base = pl.multiple_of(jnp.minimum((off >> 3) << 3, CAP - TCH), SUBLANES)
            t0 = pl.multiple_of(c * TCH, TCH)
            rel = pos_ref[0, 0, pl.ds(c, 1), :] - base
            onehot = jnp.where(crow == rel, 1.0, 0.0).astype(BF16)
            acc_ref[pl.ds(base, GW), :] += _dot(onehot, h_ref[0, pl.ds(t0, TCH), :])
        return 0

    lax.fori_loop(0, NTCH // GATHER_UNROLL, chunks, 0)
    xg_ref[0, 0] = acc_ref[0:CAP, :].astype(BF16)


def _gather(offs_flat, h2, pos4):
    b = h2.shape[0]
    grid_spec = pltpu.PrefetchScalarGridSpec(
        num_scalar_prefetch=1,
        grid=(b, NE),
        in_specs=[pl.BlockSpec((1, SEQ, D), lambda bi, e, off: (bi, 0, 0)),
                  pl.BlockSpec((1, 1, NTCH, TCH), lambda bi, e, off: (bi, e, 0, 0))],
        out_specs=pl.BlockSpec((1, 1, CAP, D), lambda bi, e, off: (bi, e, 0, 0)),
        scratch_shapes=[pltpu.VMEM((CAP + SUBLANES, D), F32)],
    )
    return pl.pallas_call(
        _gather_body,
        grid_spec=grid_spec,
        out_shape=jax.ShapeDtypeStruct((b, NE, CAP, D), BF16),
        compiler_params=_cparams(("parallel", "arbitrary")),
        name="moe_gather",
    )(offs_flat, h2, pos4)


FFN_TM = 512
FFN_NF = 4
FFN_FC = DEXP // FFN_NF
assert FFN_FC * FFN_NF == DEXP and FFN_FC % BF16_ROWS == 0


def _ffn_body(xg_ref, wgt_ref, wut_ref, wd_ref, y_ref, acc_ref):
    j = pl.program_id(2)

    @pl.when(j == 0)
    def _():
        acc_ref[...] = jnp.zeros_like(acc_ref)

    wgt = wgt_ref[0].astype(BF16)
    wut = wut_ref[0].astype(BF16)
    wd = wd_ref[0].astype(BF16)
    for mb in range(CAP // FFN_TM):
        rows = slice(mb * FFN_TM, (mb + 1) * FFN_TM)
        xb = xg_ref[0, 0, rows, :]
        a = _dot(xb, wgt, _NT)
        u = _dot(xb, wut, _NT)
        h = (a * (1.0 / (1.0 + jnp.exp(-a))) * u).astype(BF16)
        acc_ref[rows, :] += _dot(h, wd)

    @pl.when(j == FFN_NF - 1)
    def _():
        y_ref[0, 0, 0:CAP, :] = acc_ref[...].astype(BF16)
        y_ref[0, 0, CAP:YROWS, :] = jnp.zeros((YROWS - CAP, D), BF16)


def _expert_ffn(xg, w_gate_t, w_up_t, w_down):
    b = xg.shape[0]
    wblk = pl.BlockSpec((1, FFN_FC, D), lambda e, bi, j: (e, j, 0))
    return pl.pallas_call(
        _ffn_body,
        grid=(NE, b, FFN_NF),
        in_specs=[pl.BlockSpec((1, 1, CAP, D), lambda e, bi, j: (bi, e, 0, 0)), wblk, wblk, wblk],
        out_specs=pl.BlockSpec((1, 1, YROWS, D), lambda e, bi, j: (bi, e, 0, 0)),
        out_shape=jax.ShapeDtypeStruct((b, NE, YROWS, D), BF16),
        scratch_shapes=[pltpu.VMEM((CAP, D), F32)],
        compiler_params=_cparams(("parallel", "parallel", "arbitrary")),
        name="expert_ffn",
    )(xg, w_gate_t, w_up_t, w_down)


def _combine_body(off_ref, y_ref, pos_ref, gate_ref, x1_ref, g2_ref, o_ref):
    b = pl.program_id(0)
    i = pl.program_id(1)
    crow = lax.broadcasted_iota(I32, (CW, TCH), 0)
    acc = jnp.zeros((TCH, D), F32)
    for e in range(NE):
        off = off_ref[(b * NE + e) * NTCH + i]
        base = pl.multiple_of(jnp.minimum((off >> 4) << 4, CAP - TCH), BF16_ROWS)
        rel = pos_ref[0, e:e + 1, :] - base
        w = jnp.where(crow == rel, gate_ref[0, e:e + 1, :], 0.0)
        yw = y_ref[0, e, pl.ds(base, CW), :]
        acc = acc + _dot(w.astype(BF16), yw, _TN)
    o_ref[0] = x1_ref[0] + g2_ref[0] * acc


def _combine(offs_flat, y, pos, gate, x1, g2r):
    b = x1.shape[0]
    grid_spec = pltpu.PrefetchScalarGridSpec(
        num_scalar_prefetch=1,
        grid=(b, NTCH),
        in_specs=[pl.BlockSpec((1, NE, YROWS, D), lambda bi, i, off: (bi, 0, 0, 0),
                               pipeline_mode=pl.Buffered(1)),
                  pl.BlockSpec((1, NE, TCH), lambda bi, i, off: (bi, 0, i)),
                  pl.BlockSpec((1, NE, TCH), lambda bi, i, off: (bi, 0, i)),
                  pl.BlockSpec((1, TCH, D), lambda bi, i, off: (bi, i, 0)),
                  pl.BlockSpec((1, 1, D), lambda bi, i, off: (bi, 0, 0))],
        out_specs=pl.BlockSpec((1, TCH, D), lambda bi, i, off: (bi, i, 0)),
    )
    return pl.pallas_call(
        _combine_body,
        grid_spec=grid_spec,
        out_shape=jax.ShapeDtypeStruct((b, SEQ, D), F32),
        compiler_params=_cparams(("parallel", "arbitrary")),
        name="moe_combine",
    )(offs_flat, y, pos, gate, x1, g2r)


def _np_split(m):
    hi = np.asarray(m, np.float64).astype(BF16)
    lo = (m - hi.astype(np.float64)).astype(BF16)
    return jnp.asarray(hi), jnp.asarray(lo)


@functools.lru_cache(maxsize=None)
def _dft_tables():
    a = np.arange(FN1, dtype=np.float64)
    ang = 2.0 * np.pi * np.outer(a, a) / FN1
    fr, fi = np.cos(ang), -np.sin(ang)
    half = SEQ // FN2
    lead_u = np.block([[fr[:, :half], -fi[:, :half]], [fi[:, :half], fr[:, :half]]])
    lead_k = np.concatenate([fr, fi], axis=0)
    fwd = np.block([[fr, -fi], [fi, fr]])
    inv = np.block([[fr, fi], [-fi, fr]])
    out = np.block([[fr[:half], fi[:half]], [-fi[:half], fr[:half]]])
    n2 = np.arange(FN2, dtype=np.float64)
    tw = 2.0 * np.pi * np.outer(a, n2) / FN
    twr = np.broadcast_to(np.cos(tw)[:, :, None], (FN1, FN2, LANES)).astype(np.float32)
    twi = np.broadcast_to(-np.sin(tw)[:, :, None], (FN1, FN2, LANES)).astype(np.float32)
    return dict(lead_u=lead_u, lead_k=lead_k, fwd=fwd, inv=inv, out=out, twr=twr, twi=twi)


@functools.lru_cache(maxsize=None)
def _filter_tables():
    L = SEQ
    r = np.arange(FN)
    j = np.where(r < L, r, FN - r)
    jc = np.minimum(j, L - 1).astype(np.float64)
    t = (jc / (L - 1))[:, None]
    bands = (FEMB - 1) // 2
    w = 2.0 * np.pi * jc / L
    f = np.linspace(1e-4, bands - 1, bands)
    fw = w[:, None] * f[None, :]
    z = np.concatenate([t, np.cos(fw), -np.sin(fw), np.zeros((FN, FORD - FEMB))], axis=-1)
    mask = np.where(r == L, 0.0, 1.0)[:, None]
    max_decay = math.log(DECAY_TARGET) / FAST_DECAY_PCT
    min_decay = math.log(DECAY_TARGET) / SLOW_DECAY_PCT
    negdelta = -np.abs(np.linspace(min_decay, max_decay, HY_W))[None, :]
    return tuple(np.asarray(a, np.float32) for a in (z, t, mask, negdelta))


@functools.lru_cache(maxsize=None)
def _rope_tables(n):
    rows = n // GRID_W
    row_id, col_id = np.meshgrid(np.arange(rows, dtype=np.float64), np.arange(GRID_W, dtype=np.float64), indexing="ij")
    quarter = HD // 4
    inv_freq = ROPE_THETA ** (-np.arange(quarter, dtype=np.float64) / quarter)
    ar = row_id.reshape(-1)[:, None] * inv_freq
    ac = col_id.reshape(-1)[:, None] * inv_freq
    cos = np.concatenate([np.cos(ar), np.cos(ar), np.cos(ac), np.cos(ac)], axis=-1)
    sin = np.concatenate([-np.sin(ar), np.sin(ar), -np.sin(ac), np.sin(ac)], axis=-1)
    reps = (1, LANES // HD)
    return np.tile(cos, reps).astype(np.float32), np.tile(sin, reps).astype(np.float32)


def _hyena_long_conv(u, x2c, kern, abs_sum, bias):
    tb = _dft_tables()
    half = SEQ // FN2
    lanes = FN2 * HY_W
    twr, twi = jnp.asarray(tb["twr"]), jnp.asarray(tb["twi"])
    fwd_h, fwd_l = _np_split(tb["fwd"])
    inv_h, inv_l = _np_split(tb["inv"])
    ak = _dft_lead(kern.reshape(1, FN1, lanes), *_np_split(tb["lead_k"]))
    scale = 1.0 / (abs_sum * float(FN))
    khat = _filter_spectrum(ak.reshape(2, FN1, FN2, HY_W), twr, twi, fwd_h, fwd_l, scale)
    u3 = u.reshape(2, half, lanes)
    au = _dft_lead(u3, *_np_split(tb["lead_u"]))
    dd = _conv_mid(au.reshape(2, FN1, FN2, HY_W), twr, twi, fwd_h, fwd_l, inv_h, inv_l, khat)
    bias_t = jnp.tile(bias.reshape(1, HY_W), (1, HY_LT // HY_W))
    hy = _conv_out(dd.reshape(2 * FN1, lanes), *_np_split(tb["out"]), u3, x2c.reshape(2, half, lanes), bias_t)
    return hy.reshape(2, SEQ, HY_W)


def kernel(x, c, ctx, c_ctx, w_mod, b_mod, norm1_g, norm2_g, w_in, w_out, q_norm_g, k_norm_g,
           conv_w, conv_b, filt_w1, filt_b1, filt_w2, filt_b2, filt_w3, filt_freq, hyena_bias,
           w_router, w_gate, w_up, w_down):
    B = x.shape[0]
    assert x.shape == (B, SEQ, D) and B == 2 and ctx.shape == (B, CTX, D) and w_mod.shape[0] == 1
    l = 0

    cc = jnp.concatenate([c, c_ctx[None, :], jnp.zeros((SUBLANES - B - 1, D), F32)], axis=0)
    mod = _modulation(cc, w_mod[l], b_mod[l][None, :])
    sh1, sc1, g1, sh2, sc2, g2 = [mod[:, i * D:(i + 1) * D] for i in range(6)]
    lat = lambda m: m[:B, None, :]
    ctxrow = lambda m: jnp.broadcast_to(m[B:B + 1, None, :], (B, 1, D))

    w_in_bf = w_in[l].astype(BF16)
    gq2 = jnp.tile(q_norm_g[l][None, :], (1, LANES // HD))
    gk2 = jnp.tile(k_norm_g[l][None, :], (1, LANES // HD))
    bd = jnp.asarray(np.kron(np.eye(LANES // HD), np.full((HD, HD), 1.0 / HD)), BF16)
    cos_t, sin_t = _rope_tables(SEQ)
    n1g = norm1_g[l][None, :]

    q, k, v, p = _in_projection(x, n1g, lat(sh1), lat(sc1), w_in_bf, gq2, gk2, bd, cos_t, sin_t, 512)
    _, kc, vc, _ = _in_projection(ctx, n1g, ctxrow(sh1), ctxrow(sc1), w_in_bf, gq2, gk2, bd,
                                  jnp.ones((CTX, LANES), F32), jnp.zeros((CTX, LANES), F32), CTX)

    k_all = jnp.concatenate([k, kc], axis=1).reshape(B, ATT_NCH, ATT_TK, NKV, HD)
    v_all = jnp.concatenate([v, vc], axis=1).reshape(B, ATT_NCH, ATT_TK, NKV, HD)
    kch = k_all.transpose(0, 3, 1, 2, 4)
    ones_pad = jnp.concatenate([jnp.ones((B, NKV, ATT_NCH, 1, ATT_TK), BF16),
                                jnp.zeros((B, NKV, ATT_NCH, BF16_ROWS - 1, ATT_TK), BF16)], axis=3)
    vtch = jnp.concatenate([v_all.transpose(0, 3, 1, 4, 2), ones_pad], axis=3)
    att = _attention(q, kch, vtch)

    cw9 = conv_w[l].reshape(3, 3, HY_W).reshape(9, HY_W)
    cb3 = conv_b[l].reshape(3, HY_W)
    u, x2c = _short_conv(p, cw9, cb3)
    ztab, ttab, mtab, negdelta = _filter_tables()
    w1p = jnp.concatenate([filt_w1[l], jnp.zeros((FORD - FEMB, FORD), F32)], axis=0)
    kern, abs_sum = _implicit_filter(ztab, ttab, mtab, w1p, filt_b1[l][None, :], filt_w2[l],
                                     filt_b2[l][None, :], filt_w3[l], filt_freq[l][None, :], negdelta)
    hy = _hyena_long_conv(u, x2c, kern, abs_sum, hyena_bias[l])

    wrh, wrl = _split(w_router[l].T)
    x1, h2, logits = _out_projection(att, hy, x, w_out[l].astype(BF16), lat(g1), norm2_g[l][None, :],
                                     lat(sh2), lat(sc2), wrh, wrl)

    tri = jnp.asarray(np.triu(np.ones((TCH, TCH))), BF16)
    pos, gate, offs = _routing(logits, tri)
    offs_flat = offs.reshape(-1)
    xg = _gather(offs_flat, h2, pos.reshape(B, NE, NTCH, TCH))
    y = _expert_ffn(xg, jnp.swapaxes(w_gate[l], 1, 2), jnp.swapaxes(w_up[l], 1, 2), w_down[l])
    return _combine(offs_flat, y, pos, gate, x1, lat(g2))
```

```python
import functools
import math

import numpy as np
import jax
import jax.numpy as jnp
from jax import lax
from jax.experimental import pallas as pl
from jax.experimental.pallas import tpu as pltpu

F32 = jnp.float32
BF16 = jnp.bfloat16
I32 = jnp.int32

D = 1024
SEQ = 8192
CTX = 256
GRID_W = 64
ATT_W = 512
HY_W = 512
HD = 64
NQ = 8
NKV = 2
QPK = NQ // NKV
KV_W = NKV * HD
IN_W = ATT_W + 2 * KV_W + 3 * HY_W
FEMB = 33
FORD = 64
NE = 16
CAP = 2 * SEQ // NE
DEXP = 2752
ROPE_THETA = 10000.0
EPS = 1e-6
DECAY_TARGET = 1e-2
FAST_DECAY_PCT = 0.3
SLOW_DECAY_PCT = 1.5

LANES = 128
SUBLANES = 8
BF16_ROWS = 16
VMEM_BYTES_V7X = 64 * 1024 * 1024
VMEM_LIMIT = VMEM_BYTES_V7X - 8 * 1024 * 1024

FN = 2 * SEQ
FN1 = 128
FN2 = 128

TCH = LANES
NTCH = SEQ // TCH
GW = TCH + SUBLANES
CW = TCH + BF16_ROWS
YROWS = CAP + BF16_ROWS


def _cparams(sem, vmem=None):
    return pltpu.CompilerParams(dimension_semantics=sem, vmem_limit_bytes=vmem or VMEM_LIMIT)


def _split(a):
    hi = a.astype(BF16)
    lo = (a - hi.astype(F32)).astype(BF16)
    return hi, lo


_NN = (((1,), (0,)), ((), ()))
_NT = (((1,), (1,)), ((), ()))
_TN = (((0,), (0,)), ((), ()))


def _dot(a, b, dn=_NN):
    return lax.dot_general(a, b, dn, preferred_element_type=F32)


def _dot3(a, b, dn=_NN):
    ah, al = _split(a)
    bh, bl = _split(b)
    return _dot(ah, bh, dn) + _dot(ah, bl, dn) + _dot(al, bh, dn)


def _dot3c(fh, fl, z):
    zh, zl = _split(z)
    return _dot(fh, zh) + _dot(fh, zl) + _dot(fl, zh)


def _mod_body(c_ref, w_ref, b_ref, o_ref):
    c = c_ref[...]
    s = c * (1.0 / (1.0 + jnp.exp(-c)))
    o_ref[...] = _dot3(s, w_ref[...]) + b_ref[...]


def _modulation(cc, w_mod, b_mod):
    n = w_mod.shape[1]
    return pl.pallas_call(
        _mod_body,
        grid=(n // D,),
        in_specs=[pl.BlockSpec((SUBLANES, D), lambda j: (0, 0)),
                  pl.BlockSpec((D, D), lambda j: (0, j)),
                  pl.BlockSpec((1, D), lambda j: (0, j))],
        out_specs=pl.BlockSpec((SUBLANES, D), lambda j: (0, j)),
        out_shape=jax.ShapeDtypeStruct((SUBLANES, n), F32),
        compiler_params=_cparams(("arbitrary",)),
        name="modulation",
    )(cc, w_mod, b_mod)


Q_SCALE = HD ** -0.5 * math.log2(math.e)


def _rms_mod(x, g, sh, sc):
    ms = jnp.mean(x * x, axis=-1, keepdims=True)
    return (x * lax.rsqrt(ms + EPS) * g) * (1.0 + sc) + sh


def _head_norm_rope(t, g, bd, cos, sin):
    sq = t * t
    hi, lo = _split(sq)
    ms = _dot(hi, bd) + _dot(lo, bd)
    tn = t * lax.rsqrt(ms + EPS) * g
    lane = lax.broadcasted_iota(I32, tn.shape, 1)
    sw = jnp.where((lane & 31) < 16, pltpu.roll(tn, LANES - 16, 1), pltpu.roll(tn, 16, 1))
    return tn * cos + sw * sin


def _proj_body(x_ref, g_ref, sh_ref, sc_ref, w_ref, gq_ref, gk_ref, bd_ref, cos_ref, sin_ref,
               q_ref, k_ref, v_ref, p_ref):
    h = _rms_mod(x_ref[0], g_ref[...], sh_ref[0], sc_ref[0])
    proj = _dot(h.astype(BF16), w_ref[...])
    bd = bd_ref[...]
    cos = cos_ref[...]
    sin = sin_ref[...]
    for j in range(ATT_W // LANES):
        sl = slice(j * LANES, (j + 1) * LANES)
        qj = _head_norm_rope(proj[:, sl], gq_ref[...], bd, cos, sin)
        q_ref[0, :, sl] = (qj * Q_SCALE).astype(BF16)
    k_ref[0] = _head_norm_rope(proj[:, ATT_W:ATT_W + KV_W], gk_ref[...], bd, cos, sin).astype(BF16)
    v_ref[0] = proj[:, ATT_W + KV_W:ATT_W + 2 * KV_W].astype(BF16)
    p_ref[0] = proj[:, ATT_W + 2 * KV_W:]


def _in_projection(x, g1, sh, sc, w_in_bf, gq2, gk2, bd, cos_t, sin_t, tm):
    b, s, _ = x.shape
    row = lambda bi, i: (bi, 0, 0)
    tok = lambda bi, i: (bi, i, 0)
    const = lambda bi, i: (0, 0)
    return pl.pallas_call(
        _proj_body,
        grid=(b, s // tm),
        in_specs=[pl.BlockSpec((1, tm, D), tok),
                  pl.BlockSpec((1, D), const),
                  pl.BlockSpec((1, 1, D), row),
                  pl.BlockSpec((1, 1, D), row),
                  pl.BlockSpec((D, IN_W), const),
                  pl.BlockSpec((1, LANES), const),
                  pl.BlockSpec((1, LANES), const),
                  pl.BlockSpec((LANES, LANES), const),
                  pl.BlockSpec((tm, LANES), lambda bi, i: (i, 0)),
                  pl.BlockSpec((tm, LANES), lambda bi, i: (i, 0))],
        out_specs=[pl.BlockSpec((1, tm, ATT_W), tok),
                   pl.BlockSpec((1, tm, KV_W), tok),
                   pl.BlockSpec((1, tm, KV_W), tok),
                   pl.BlockSpec((1, tm, 3 * HY_W), tok)],
        out_shape=[jax.ShapeDtypeStruct((b, s, ATT_W), BF16),
                   jax.ShapeDtypeStruct((b, s, KV_W), BF16),
                   jax.ShapeDtypeStruct((b, s, KV_W), BF16),
                   jax.ShapeDtypeStruct((b, s, 3 * HY_W), F32)],
        compiler_params=_cparams(("parallel", "parallel")),
        name="in_projection",
    )(x, g1, sh, sc, w_in_bf, gq2, gk2, bd, cos_t, sin_t)


ATT_TQ = 256
ATT_TK = 768
SK = SEQ + CTX
ATT_NCH = SK // ATT_TK


ATT_NQ = QPK * ATT_TQ
ATT_VR = HD + BF16_ROWS
assert ATT_NCH % 2 == 1


def _attn_body(q_ref, k_ref, vt_ref, o_ref, s_ref, mx_ref, m_ref, acc_ref):
    qall = jnp.concatenate([q_ref[0, :, r * HD:(r + 1) * HD] for r in range(QPK)], axis=0)
    m_ref[...] = jnp.full(m_ref.shape, -1e30, F32)
    acc_ref[...] = jnp.zeros_like(acc_ref)

    def scores(c, slot):
        s = _dot(k_ref[0, 0, c], qall, _NT)
        s_ref[slot] = s
        mx_ref[slot] = jnp.max(s, axis=0, keepdims=True)

    def update(c, slot):
        m_old = m_ref[...]
        m_new = jnp.maximum(m_old, mx_ref[slot])
        p = jnp.exp2(s_ref[slot] - m_new).astype(BF16)
        acc_ref[...] = jnp.exp2(m_old - m_new) * acc_ref[...] + _dot(vt_ref[0, 0, c], p)
        m_ref[...] = m_new

    scores(0, 0)

    def pair(i, _):
        c = 2 * i
        scores(c + 1, 1)
        update(c, 0)
        scores(c + 2, 0)
        update(c + 1, 1)
        return 0

    lax.fori_loop(0, ATT_NCH // 2, pair, 0)
    update(ATT_NCH - 1, 0)
    out = acc_ref[0:HD, :] * (1.0 / acc_ref[HD:HD + 1, :])
    for r in range(QPK):
        o_ref[0, :, r * HD:(r + 1) * HD] = out[:, r * ATT_TQ:(r + 1) * ATT_TQ].T.astype(BF16)


def _attention(q, kch, vtch):
    b = q.shape[0]
    return pl.pallas_call(
        _attn_body,
        grid=(b, NKV, SEQ // ATT_TQ),
        in_specs=[pl.BlockSpec((1, ATT_TQ, QPK * HD), lambda bi, g, i: (bi, i, g)),
                  pl.BlockSpec((1, 1, ATT_NCH, ATT_TK, HD), lambda bi, g, i: (bi, g, 0, 0, 0)),
                  pl.BlockSpec((1, 1, ATT_NCH, ATT_VR, ATT_TK), lambda bi, g, i: (bi, g, 0, 0, 0))],
        out_specs=pl.BlockSpec((1, ATT_TQ, QPK * HD), lambda bi, g, i: (bi, i, g)),
        out_shape=jax.ShapeDtypeStruct((b, SEQ, ATT_W), BF16),
        scratch_shapes=[pltpu.VMEM((2, ATT_TK, ATT_NQ), F32), pltpu.VMEM((2, 1, ATT_NQ), F32),
                        pltpu.VMEM((1, ATT_NQ), F32), pltpu.VMEM((ATT_VR, ATT_NQ), F32)],
        compiler_params=_cparams(("parallel", "parallel", "parallel")),
        name="attention",
    )(q, kch, vtch)


SC_TM = 1024


def _sconv_body(m1, a1, n1, m2, a2, n2, m3, a3, n3, w_ref, b_ref, u_ref, x2_ref):
    i = pl.program_id(1)
    last = pl.num_programs(1) - 1
    rows = lax.broadcasted_iota(I32, (SC_TM, HY_W), 0)

    def conv(main, prev, nxt, g):
        x = main[0]
        pr = jnp.where(i > 0, prev[0, SUBLANES - 1:SUBLANES, :], 0.0)
        nx = jnp.where(i < last, nxt[0, 0:1, :], 0.0)
        xm = jnp.where(rows == 0, pr, pltpu.roll(x, 1, 0))
        xp = jnp.where(rows == SC_TM - 1, nx, pltpu.roll(x, SC_TM - 1, 0))
        return (w_ref[g:g + 1, :] * xm + w_ref[3 + g:4 + g, :] * x + w_ref[6 + g:7 + g, :] * xp
                + b_ref[g:g + 1, :])

    x1 = conv(m1, a1, n1, 0)
    x2 = conv(m2, a2, n2, 1)
    v = conv(m3, a3, n3, 2)
    u_ref[0] = v * x1
    x2_ref[0] = x2


def _short_conv(p, cw9, cb3):
    b = p.shape[0]
    nblk8 = SEQ // SUBLANES
    step8 = SC_TM // SUBLANES
    specs = []
    for g in range(3):
        specs += [pl.BlockSpec((1, SC_TM, HY_W), lambda bi, i, g=g: (bi, i, g)),
                  pl.BlockSpec((1, SUBLANES, HY_W), lambda bi, i, g=g: (bi, jnp.maximum(i * step8 - 1, 0), g)),
                  pl.BlockSpec((1, SUBLANES, HY_W), lambda bi, i, g=g: (bi, jnp.minimum((i + 1) * step8, nblk8 - 1), g))]
    specs += [pl.BlockSpec((9, HY_W), lambda bi, i: (0, 0)), pl.BlockSpec((3, HY_W), lambda bi, i: (0, 0))]
    out = pl.BlockSpec((1, SC_TM, HY_W), lambda bi, i: (bi, i, 0))
    return pl.pallas_call(
        _sconv_body,
        grid=(b, SEQ // SC_TM),
        in_specs=specs,
        out_specs=[out, out],
        out_shape=[jax.ShapeDtypeStruct((b, SEQ, HY_W), F32)] * 2,
        compiler_params=_cparams(("parallel", "parallel")),
        name="short_conv",
    )(p, p, p, p, p, p, p, p, p, cw9, cb3)


FILT_TR = 1024


def _filter_body(z_ref, t_ref, msk_ref, w1_ref, b1_ref, w2_ref, b2_ref, w3_ref, fr_ref, dl_ref,
                 k_ref, s_ref):
    fr = fr_ref[...]
    h = jnp.sin(fr * (_dot3(z_ref[...], w1_ref[...]) + b1_ref[...]))
    h = jnp.sin(fr * (_dot3(h, w2_ref[...]) + b2_ref[...]))
    h = _dot3(h, w3_ref[...])
    kern = h * jnp.exp(t_ref[...] * dl_ref[...]) * msk_ref[...]
    k_ref[...] = kern

    @pl.when((pl.program_id(0) == 0) & (pl.program_id(1) == 0))
    def _():
        s_ref[...] = jnp.zeros_like(s_ref)

    s_ref[...] += jnp.sum(jnp.abs(kern), axis=0, keepdims=True)


def _implicit_filter(ztab, ttab, mtab, w1p, b1, w2, b2, w3, freq, negdelta):
    nt = SEQ // FILT_TR
    rowblk = lambda hf, i: (hf * nt + i, 0)
    const = lambda hf, i: (0, 0)
    return pl.pallas_call(
        _filter_body,
        grid=(2, nt),
        in_specs=[pl.BlockSpec((FILT_TR, FORD), rowblk),
                  pl.BlockSpec((FILT_TR, 1), rowblk),
                  pl.BlockSpec((FILT_TR, 1), rowblk),
                  pl.BlockSpec((FORD, FORD), const),
                  pl.BlockSpec((1, FORD), const),
                  pl.BlockSpec((FORD, FORD), const),
                  pl.BlockSpec((1, FORD), const),
                  pl.BlockSpec((FORD, HY_W), lambda hf, i: (0, hf)),
                  pl.BlockSpec((1, FORD), const),
                  pl.BlockSpec((1, HY_W), const)],
        out_specs=[pl.BlockSpec((FILT_TR, HY_W), rowblk),
                   pl.BlockSpec((1, HY_W), const)],
        out_shape=[jax.ShapeDtypeStruct((FN, HY_W), F32), jax.ShapeDtypeStruct((1, HY_W), F32)],
        compiler_params=_cparams(("arbitrary", "arbitrary")),
        name="implicit_filter",
    )(ztab, ttab, mtab, w1p, b1, w2, b2, w3, freq, negdelta)


DFT_LT = 4096


def _dft_lead_body(z_ref, fh_ref, fl_ref, o_ref):
    nb = z_ref.shape[0]
    z = jnp.concatenate([z_ref[j] for j in range(nb)], axis=0) if nb > 1 else z_ref[0]
    o_ref[...] = _dot3c(fh_ref[...], fl_ref[...], z)


def _dft_lead(z, fh, fl):
    nb, rows, lanes = z.shape
    m = fh.shape[0]
    return pl.pallas_call(
        _dft_lead_body,
        grid=(lanes // DFT_LT,),
        in_specs=[pl.BlockSpec((nb, rows, DFT_LT), lambda j: (0, 0, j)),
                  pl.BlockSpec(fh.shape, lambda j: (0, 0)),
                  pl.BlockSpec(fl.shape, lambda j: (0, 0))],
        out_specs=pl.BlockSpec((m, DFT_LT), lambda j: (0, j)),
        out_shape=jax.ShapeDtypeStruct((m, lanes), F32),
        compiler_params=_cparams(("parallel",)),
        name="dft_lead",
    )(z, fh, fl)


DFT_KB = 4


def _twiddle(tr_ref, ti_ref, j):
    reps = HY_W // LANES
    tr = jnp.concatenate([tr_ref[j]] * reps, axis=1)
    ti = jnp.concatenate([ti_ref[j]] * reps, axis=1)
    return tr, ti


def _spectrum_body(a_ref, tr_ref, ti_ref, fh_ref, fl_ref, sc_ref, o_ref):
    for j in range(DFT_KB):
        ar, ai = a_ref[0, j], a_ref[1, j]
        tr, ti = _twiddle(tr_ref, ti_ref, j)
        b = jnp.concatenate([ar * tr - ai * ti, ar * ti + ai * tr], axis=0)
        x = _dot3c(fh_ref[...], fl_ref[...], b) * sc_ref[...]
        o_ref[0, j] = x[:FN2]
        o_ref[1, j] = x[FN2:]


def _conv_mid_body(a_ref, tr_ref, ti_ref, fh_ref, fl_ref, gh_ref, gl_ref, k_ref, o_ref):
    for j in range(DFT_KB):
        ar, ai = a_ref[0, j], a_ref[1, j]
        tr, ti = _twiddle(tr_ref, ti_ref, j)
        b = jnp.concatenate([ar * tr - ai * ti, ar * ti + ai * tr], axis=0)
        x = _dot3c(fh_ref[...], fl_ref[...], b)
        xr, xi = x[:FN2], x[FN2:]
        kr, ki = k_ref[0, j], k_ref[1, j]
        y = jnp.concatenate([xr * kr - xi * ki, xr * ki + xi * kr], axis=0)
        c = _dot3c(gh_ref[...], gl_ref[...], y)
        cr, ci = c[:FN2], c[FN2:]
        o_ref[0, j] = cr * tr + ci * ti
        o_ref[1, j] = ci * tr - cr * ti


def _dft_mid_specs():
    slab = pl.BlockSpec((2, DFT_KB, FN2, HY_W), lambda i: (0, i, 0, 0))
    tw = pl.BlockSpec((DFT_KB, FN2, LANES), lambda i: (i, 0, 0))
    mat = pl.BlockSpec((2 * FN2, 2 * FN2), lambda i: (0, 0))
    return slab, tw, mat


def _filter_spectrum(a4, twr, twi, fh, fl, scale):
    slab, tw, mat = _dft_mid_specs()
    return pl.pallas_call(
        _spectrum_body,
        grid=(FN1 // DFT_KB,),
        in_specs=[slab, tw, tw, mat, mat, pl.BlockSpec((1, HY_W), lambda i: (0, 0))],
        out_specs=slab,
        out_shape=jax.ShapeDtypeStruct((2, FN1, FN2, HY_W), F32),
        compiler_params=_cparams(("parallel",)),
        name="filter_spectrum",
    )(a4, twr, twi, fh, fl, scale)


def _conv_mid(a4, twr, twi, fh, fl, gh, gl, khat):
    slab, tw, mat = _dft_mid_specs()
    return pl.pallas_call(
        _conv_mid_body,
        grid=(FN1 // DFT_KB,),
        in_specs=[slab, tw, tw, mat, mat, mat, mat, slab],
        out_specs=slab,
        out_shape=jax.ShapeDtypeStruct((2, FN1, FN2, HY_W), F32),
        compiler_params=_cparams(("parallel",)),
        name="conv_mid",
    )(a4, twr, twi, fh, fl, gh, gl, khat)


HY_LT = 2048


def _conv_out_body(d_ref, fh_ref, fl_ref, u_ref, x2_ref, bias_ref, o_ref):
    y = _dot3c(fh_ref[...], fl_ref[...], d_ref[...])
    half = SEQ // FN2
    bias = bias_ref[...]
    for b in range(2):
        yb = y[b * half:(b + 1) * half]
        o_ref[b] = ((yb + u_ref[b] * bias) * x2_ref[b]).astype(BF16)


def _conv_out(d2, fh, fl, u3, x23, bias_t):
    half = SEQ // FN2
    lanes = d2.shape[1]
    io = pl.BlockSpec((2, half, HY_LT), lambda j: (0, 0, j))
    return pl.pallas_call(
        _conv_out_body,
        grid=(lanes // HY_LT,),
        in_specs=[pl.BlockSpec((2 * FN1, HY_LT), lambda j: (0, j)),
                  pl.BlockSpec(fh.shape, lambda j: (0, 0)),
                  pl.BlockSpec(fl.shape, lambda j: (0, 0)),
                  io, io,
                  pl.BlockSpec((1, HY_LT), lambda j: (0, 0))],
        out_specs=io,
        out_shape=jax.ShapeDtypeStruct((2, half, lanes), BF16),
        compiler_params=_cparams(("parallel",)),
        name="conv_out",
    )(d2, fh, fl, u3, x23, bias_t)


OP_TM = 512


def _outproj_body(att_ref, hy_ref, x_ref, w_ref, g1_ref, n2_ref, sh_ref, sc_ref, wrh_ref, wrl_ref,
                  x1_ref, h2_ref, lg_ref):
    a = jnp.concatenate([att_ref[0], hy_ref[0]], axis=1)
    x1 = x_ref[0] + g1_ref[0] * _dot(a, w_ref[...])
    x1_ref[0] = x1
    h2 = _rms_mod(x1, n2_ref[...], sh_ref[0], sc_ref[0])
    hh, hl = _split(h2)
    h2_ref[0] = hh
    wrh = wrh_ref[...]
    lg_ref[0] = _dot(wrh, hh, _NT) + _dot(wrh, hl, _NT) + _dot(wrl_ref[...], hh, _NT)


def _out_projection(att, hy, x, w_out_bf, g1r, n2g, sh2, sc2, wrh, wrl):
    b = x.shape[0]
    tok = lambda bi, i: (bi, i, 0)
    row = lambda bi, i: (bi, 0, 0)
    const = lambda bi, i: (0, 0)
    return pl.pallas_call(
        _outproj_body,
        grid=(b, SEQ // OP_TM),
        in_specs=[pl.BlockSpec((1, OP_TM, ATT_W), tok),
                  pl.BlockSpec((1, OP_TM, HY_W), tok),
                  pl.BlockSpec((1, OP_TM, D), tok),
                  pl.BlockSpec((ATT_W + HY_W, D), const),
                  pl.BlockSpec((1, 1, D), row),
                  pl.BlockSpec((1, D), const),
                  pl.BlockSpec((1, 1, D), row),
                  pl.BlockSpec((1, 1, D), row),
                  pl.BlockSpec((NE, D), const),
                  pl.BlockSpec((NE, D), const)],
        out_specs=[pl.BlockSpec((1, OP_TM, D), tok),
                   pl.BlockSpec((1, OP_TM, D), tok),
                   pl.BlockSpec((1, NE, OP_TM), lambda bi, i: (bi, 0, i))],
        out_shape=[jax.ShapeDtypeStruct((b, SEQ, D), F32),
                   jax.ShapeDtypeStruct((b, SEQ, D), BF16),
                   jax.ShapeDtypeStruct((b, NE, SEQ), F32)],
        compiler_params=_cparams(("parallel", "parallel")),
        name="out_projection",
    )(att, hy, x, w_out_bf, g1r, n2g, sh2, sc2, wrh, wrl)


def _routing_body(lg_ref, tri_ref, pos_ref, gate_ref, off_ref, cs_ref):
    lg = lg_ref[0]
    e = jnp.exp(lg - jnp.max(lg, axis=0, keepdims=True))
    aff = e / jnp.sum(e, axis=0, keepdims=True)
    gate_ref[0] = aff
    def count_ge(t):
        return jnp.sum(jnp.where(aff >= t, 1.0, 0.0), axis=1, keepdims=True)

    def bisect(i, thr):
        cand = thr | (jnp.int32(1) << (30 - i))
        return jnp.where(count_ge(pltpu.bitcast(cand, F32)) >= float(CAP), cand, thr)

    thr = lax.fori_loop(0, 31, bisect, jnp.zeros((NE, 1), I32))
    lo = pltpu.bitcast(thr, F32)
    hi = jnp.maximum(pltpu.bitcast(thr + 1, F32), jnp.finfo(F32).tiny)

    def refine(i, c):
        lo, hi = c
        mid = lo + (hi - lo) * 0.5
        ok = count_ge(mid) >= float(CAP)
        return jnp.where(ok, mid, lo), jnp.where(ok, hi, mid)

    lo, hi = lax.fori_loop(0, 32, refine, (lo, hi))
    gt = aff >= hi
    eq = (aff >= lo) & jnp.logical_not(gt)
    need = float(CAP) - jnp.sum(jnp.where(gt, 1.0, 0.0), axis=1, keepdims=True)
    tri = tri_ref[...]

    def excl_cumsum(mask_f, record_offsets):
        carry = jnp.zeros((NE, 1), F32)
        for c in range(NTCH):
            sl = slice(c * TCH, (c + 1) * TCH)
            m = mask_f[:, sl]
            inc = _dot(m.astype(BF16), tri)
            cs_ref[:, sl] = inc - m + carry
            if record_offsets:
                off_ref[0, :, c:c + 1] = carry.astype(I32)
            carry = carry + inc[:, TCH - 1:TCH]
        return cs_ref[...]

    eq_rank = excl_cumsum(jnp.where(eq, 1.0, 0.0), False)
    sel = gt | (eq & (eq_rank < need))
    pos = excl_cumsum(jnp.where(sel, 1.0, 0.0), True)
    pos_ref[0] = jnp.where(sel, pos.astype(I32), -1)


def _routing(logits, tri):
    b = logits.shape[0]
    blk = pl.BlockSpec((1, NE, SEQ), lambda bi: (bi, 0, 0))
    return pl.pallas_call(
        _routing_body,
        grid=(b,),
        in_specs=[blk, pl.BlockSpec((TCH, TCH), lambda bi: (0, 0))],
        out_specs=[blk, blk, pl.BlockSpec((1, NE, NTCH), lambda bi: (bi, 0, 0))],
        out_shape=[jax.ShapeDtypeStruct((b, NE, SEQ), I32),
                   jax.ShapeDtypeStruct((b, NE, SEQ), F32),
                   jax.ShapeDtypeStruct((b, NE, NTCH), I32)],
        scratch_shapes=[pltpu.VMEM((NE, SEQ), F32)],
        compiler_params=_cparams(("parallel",)),
        name="routing",
    )(logits, tri)


GATHER_UNROLL = 4


def _gather_body(off_ref, h_ref, pos_ref, xg_ref, acc_ref):
    b = pl.program_id(0)
    e = pl.program_id(1)
    acc_ref[...] = jnp.zeros_like(acc_ref)
    crow = lax.broadcasted_iota(I32, (GW, TCH), 0)

    def chunks(i, _):
        for j in range(GATHER_UNROLL):
            c = i * GATHER_UNROLL + j
            off = off_ref[(b * NE + e) * NTCH + c]
            base = pl.multiple_of(jnp.minimum((off >> 3) << 3, CAP - TCH), SUBLANES)
            t0 = pl.multiple_of(c * TCH, TCH)
            rel = pos_ref[0, 0, pl.ds(c, 1), :] - base
            onehot = jnp.where(crow == rel, 1.0, 0.0).astype(BF16)
            acc_ref[pl.ds(base, GW), :] += _dot(onehot, h_ref[0, pl.ds(t0, TCH), :])
        return 0

    lax.fori_loop(0, NTCH // GATHER_UNROLL, chunks, 0)
    xg_ref[0, 0] = acc_ref[0:CAP, :].astype(BF16)


def _gather(offs_flat, h2, pos4):
    b = h2.shape[0]
    grid_spec = pltpu.PrefetchScalarGridSpec(
        num_scalar_prefetch=1,
        grid=(b, NE),
        in_specs=[pl.BlockSpec((1, SEQ, D), lambda bi, e, off: (bi, 0, 0)),
                  pl.BlockSpec((1, 1, NTCH, TCH), lambda bi, e, off: (bi, e, 0, 0))],
        out_specs=pl.BlockSpec((1, 1, CAP, D), lambda bi, e, off: (bi, e, 0, 0)),
        scratch_shapes=[pltpu.VMEM((CAP + SUBLANES, D), F32)],
    )
    return pl.pallas_call(
        _gather_body,
        grid_spec=grid_spec,
        out_shape=jax.ShapeDtypeStruct((b, NE, CAP, D), BF16),
        compiler_params=_cparams(("parallel", "arbitrary")),
        name="moe_gather",
    )(offs_flat, h2, pos4)


FFN_TM = 512
FFN_NF = 4
FFN_FC = DEXP // FFN_NF
assert FFN_FC * FFN_NF == DEXP and FFN_FC % BF16_ROWS == 0


def _ffn_body(xg_ref, wgt_ref, wut_ref, wd_ref, y_ref, acc_ref):
    j = pl.program_id(2)

    @pl.when(j == 0)
    def _():
        acc_ref[...] = jnp.zeros_like(acc_ref)

    wgt = wgt_ref[0].astype(BF16)
    wut = wut_ref[0].astype(BF16)
    wd = wd_ref[0].astype(BF16)
    for mb in range(CAP // FFN_TM):
        rows = slice(mb * FFN_TM, (mb + 1) * FFN_TM)
        xb = xg_ref[0, 0, rows, :]
        a = _dot(xb, wgt, _NT)
        u = _dot(xb, wut, _NT)
        h = (a * (1.0 / (1.0 + jnp.exp(-a))) * u).astype(BF16)
        acc_ref[rows, :] += _dot(h, wd)

    @pl.when(j == FFN_NF - 1)
    def _():
        y_ref[0, 0, 0:CAP, :] = acc_ref[...].astype(BF16)
        y_ref[0, 0, CAP:YROWS, :] = jnp.zeros((YROWS - CAP, D), BF16)


def _expert_ffn(xg, w_gate_t, w_up_t, w_down):
    b = xg.shape[0]
    wblk = pl.BlockSpec((1, FFN_FC, D), lambda e, bi, j: (e, j, 0))
    return pl.pallas_call(
        _ffn_body,
        grid=(NE, b, FFN_NF),
        in_specs=[pl.BlockSpec((1, 1, CAP, D), lambda e, bi, j: (bi, e, 0, 0)), wblk, wblk, wblk],
        out_specs=pl.BlockSpec((1, 1, YROWS, D), lambda e, bi, j: (bi, e, 0, 0)),
        out_shape=jax.ShapeDtypeStruct((b, NE, YROWS, D), BF16),
        scratch_shapes=[pltpu.VMEM((CAP, D), F32)],
        compiler_params=_cparams(("parallel", "parallel", "arbitrary")),
        name="expert_ffn",
    )(xg, w_gate_t, w_up_t, w_down)


def _combine_body(off_ref, y_ref, pos_ref, gate_ref, x1_ref, g2_ref, o_ref):
    b = pl.program_id(0)
    i = pl.program_id(1)
    crow = lax.broadcasted_iota(I32, (CW, TCH), 0)
    acc = jnp.zeros((TCH, D), F32)
    for e in range(NE):
        off = off_ref[(b * NE + e) * NTCH + i]
        base = pl.multiple_of(jnp.minimum((off >> 4) << 4, CAP - TCH), BF16_ROWS)
        rel = pos_ref[0, e:e + 1, :] - base
        w = jnp.where(crow == rel, gate_ref[0, e:e + 1, :], 0.0)
        yw = y_ref[0, e, pl.ds(base, CW), :]
        acc = acc + _dot(w.astype(BF16), yw, _TN)
    o_ref[0] = x1_ref[0] + g2_ref[0] * acc


def _combine(offs_flat, y, pos, gate, x1, g2r):
    b = x1.shape[0]
    grid_spec = pltpu.PrefetchScalarGridSpec(
        num_scalar_prefetch=1,
        grid=(b, NTCH),
        in_specs=[pl.BlockSpec((1, NE, YROWS, D), lambda bi, i, off: (bi, 0, 0, 0),
                               pipeline_mode=pl.Buffered(1)),
                  pl.BlockSpec((1, NE, TCH), lambda bi, i, off: (bi, 0, i)),
                  pl.BlockSpec((1, NE, TCH), lambda bi, i, off: (bi, 0, i)),
                  pl.BlockSpec((1, TCH, D), lambda bi, i, off: (bi, i, 0)),
                  pl.BlockSpec((1, 1, D), lambda bi, i, off: (bi, 0, 0))],
        out_specs=pl.BlockSpec((1, TCH, D), lambda bi, i, off: (bi, i, 0)),
    )
    return pl.pallas_call(
        _combine_body,
        grid_spec=grid_spec,
        out_shape=jax.ShapeDtypeStruct((b, SEQ, D), F32),
        compiler_params=_cparams(("parallel", "arbitrary")),
        name="moe_combine",
    )(offs_flat, y, pos, gate, x1, g2r)


def _np_split(m):
    hi = np.asarray(m, np.float64).astype(BF16)
    lo = (m - hi.astype(np.float64)).astype(BF16)
    return jnp.asarray(hi), jnp.asarray(lo)


@functools.lru_cache(maxsize=None)
def _dft_tables():
    a = np.arange(FN1, dtype=np.float64)
    ang = 2.0 * np.pi * np.outer(a, a) / FN1
    fr, fi = np.cos(ang), -np.sin(ang)
    half = SEQ // FN2
    lead_u = np.block([[fr[:, :half], -fi[:, :half]], [fi[:, :half], fr[:, :half]]])
    lead_k = np.concatenate([fr, fi], axis=0)
    fwd = np.block([[fr, -fi], [fi, fr]])
    inv = np.block([[fr, fi], [-fi, fr]])
    out = np.block([[fr[:half], fi[:half]], [-fi[:half], fr[:half]]])
    n2 = np.arange(FN2, dtype=np.float64)
    tw = 2.0 * np.pi * np.outer(a, n2) / FN
    twr = np.broadcast_to(np.cos(tw)[:, :, None], (FN1, FN2, LANES)).astype(np.float32)
    twi = np.broadcast_to(-np.sin(tw)[:, :, None], (FN1, FN2, LANES)).astype(np.float32)
    return dict(lead_u=lead_u, lead_k=lead_k, fwd=fwd, inv=inv, out=out, twr=twr, twi=twi)


@functools.lru_cache(maxsize=None)
def _filter_tables():
    L = SEQ
    r = np.arange(FN)
    j = np.where(r < L, r, FN - r)
    jc = np.minimum(j, L - 1).astype(np.float64)
    t = (jc / (L - 1))[:, None]
    bands = (FEMB - 1) // 2
    w = 2.0 * np.pi * jc / L
    f = np.linspace(1e-4, bands - 1, bands)
    fw = w[:, None] * f[None, :]
    z = np.concatenate([t, np.cos(fw), -np.sin(fw), np.zeros((FN, FORD - FEMB))], axis=-1)
    mask = np.where(r == L, 0.0, 1.0)[:, None]
    max_decay = math.log(DECAY_TARGET) / FAST_DECAY_PCT
    min_decay = math.log(DECAY_TARGET) / SLOW_DECAY_PCT
    negdelta = -np.abs(np.linspace(min_decay, max_decay, HY_W))[None, :]
    return tuple(np.asarray(a, np.float32) for a in (z, t, mask, negdelta))


@functools.lru_cache(maxsize=None)
def _rope_tables(n):
    rows = n // GRID_W
    row_id, col_id = np.meshgrid(np.arange(rows, dtype=np.float64), np.arange(GRID_W, dtype=np.float64), indexing="ij")
    quarter = HD // 4
    inv_freq = ROPE_THETA ** (-np.arange(quarter, dtype=np.float64) / quarter)
    ar = row_id.reshape(-1)[:, None] * inv_freq
    ac = col_id.reshape(-1)[:, None] * inv_freq
    cos = np.concatenate([np.cos(ar), np.cos(ar), np.cos(ac), np.cos(ac)], axis=-1)
    sin = np.concatenate([-np.sin(ar), np.sin(ar), -np.sin(ac), np.sin(ac)], axis=-1)
    reps = (1, LANES // HD)
    return np.tile(cos, reps).astype(np.float32), np.tile(sin, reps).astype(np.float32)


def _hyena_long_conv(u, x2c, kern, abs_sum, bias):
    tb = _dft_tables()
    half = SEQ // FN2
    lanes = FN2 * HY_W
    twr, twi = jnp.asarray(tb["twr"]), jnp.asarray(tb["twi"])
    fwd_h, fwd_l = _np_split(tb["fwd"])
    inv_h, inv_l = _np_split(tb["inv"])
    ak = _dft_lead(kern.reshape(1, FN1, lanes), *_np_split(tb["lead_k"]))
    scale = 1.0 / (abs_sum * float(FN))
    khat = _filter_spectrum(ak.reshape(2, FN1, FN2, HY_W), twr, twi, fwd_h, fwd_l, scale)
    u3 = u.reshape(2, half, lanes)
    au = _dft_lead(u3, *_np_split(tb["lead_u"]))
    dd = _conv_mid(au.reshape(2, FN1, FN2, HY_W), twr, twi, fwd_h, fwd_l, inv_h, inv_l, khat)
    bias_t = jnp.tile(bias.reshape(1, HY_W), (1, HY_LT // HY_W))
    hy = _conv_out(dd.reshape(2 * FN1, lanes), *_np_split(tb["out"]), u3, x2c.reshape(2, half, lanes), bias_t)
    return hy.reshape(2, SEQ, HY_W)


def kernel(x, c, ctx, c_ctx, w_mod, b_mod, norm1_g, norm2_g, w_in, w_out, q_norm_g, k_norm_g,
           conv_w, conv_b, filt_w1, filt_b1, filt_w2, filt_b2, filt_w3, filt_freq, hyena_bias,
           w_router, w_gate, w_up, w_down):
    B = x.shape[0]
    assert x.shape == (B, SEQ, D) and B == 2 and ctx.shape == (B, CTX, D) and w_mod.shape[0] == 1
    l = 0

    cc = jnp.concatenate([c, c_ctx[None, :], jnp.zeros((SUBLANES - B - 1, D), F32)], axis=0)
    mod = _modulation(cc, w_mod[l], b_mod[l][None, :])
    sh1, sc1, g1, sh2, sc2, g2 = [mod[:, i * D:(i + 1) * D] for i in range(6)]
    lat = lambda m: m[:B, None, :]
    ctxrow = lambda m: jnp.broadcast_to(m[B:B + 1, None, :], (B, 1, D))

    w_in_bf = w_in[l].astype(BF16)
    gq2 = jnp.tile(q_norm_g[l][None, :], (1, LANES // HD))
    gk2 = jnp.tile(k_norm_g[l][None, :], (1, LANES // HD))
    bd = jnp.asarray(np.kron(np.eye(LANES // HD), np.full((HD, HD), 1.0 / HD)), BF16)
    cos_t, sin_t = _rope_tables(SEQ)
    n1g = norm1_g[l][None, :]

    q, k, v, p = _in_projection(x, n1g, lat(sh1), lat(sc1), w_in_bf, gq2, gk2, bd, cos_t, sin_t, 512)
    _, kc, vc, _ = _in_projection(ctx, n1g, ctxrow(sh1), ctxrow(sc1), w_in_bf, gq2, gk2, bd,
                                  jnp.ones((CTX, LANES), F32), jnp.zeros((CTX, LANES), F32), CTX)

    k_all = jnp.concatenate([k, kc], axis=1).reshape(B, ATT_NCH, ATT_TK, NKV, HD)
    v_all = jnp.concatenate([v, vc], axis=1).reshape(B, ATT_NCH, ATT_TK, NKV, HD)
    kch = k_all.transpose(0, 3, 1, 2, 4)
    ones_pad = jnp.concatenate([jnp.ones((B, NKV, ATT_NCH, 1, ATT_TK), BF16),
                                jnp.zeros((B, NKV, ATT_NCH, BF16_ROWS - 1, ATT_TK), BF16)], axis=3)
    vtch = jnp.concatenate([v_all.transpose(0, 3, 1, 4, 2), ones_pad], axis=3)
    att = _attention(q, kch, vtch)

    cw9 = conv_w[l].reshape(3, 3, HY_W).reshape(9, HY_W)
    cb3 = conv_b[l].reshape(3, HY_W)
    u, x2c = _short_conv(p, cw9, cb3)
    ztab, ttab, mtab, negdelta = _filter_tables()
    w1p = jnp.concatenate([filt_w1[l], jnp.zeros((FORD - FEMB, FORD), F32)], axis=0)
    kern, abs_sum = _implicit_filter(ztab, ttab, mtab, w1p, filt_b1[l][None, :], filt_w2[l],
                                     filt_b2[l][None, :], filt_w3[l], filt_freq[l][None, :], negdelta)
    hy = _hyena_long_conv(u, x2c, kern, abs_sum, hyena_bias[l])

    wrh, wrl = _split(w_router[l].T)
    x1, h2, logits = _out_projection(att, hy, x, w_out[l].astype(BF16), lat(g1), norm2_g[l][None, :],
                                     lat(sh2), lat(sc2), wrh, wrl)

    tri = jnp.asarray(np.triu(np.ones((TCH, TCH))), BF16)
    pos, gate, offs = _routing(logits, tri)
    offs_flat = offs.reshape(-1)
    xg = _gather(offs_flat, h2, pos.reshape(B, NE, NTCH, TCH))
    y = _expert_ffn(xg, jnp.swapaxes(w_gate[l], 1, 2), jnp.swapaxes(w_up[l], 1, 2), w_down[l])
    return _combine(offs_flat, y, pos, gate, x1, lat(g2))
```

```python
import functools
import math

import numpy as np
import jax
import jax.numpy as jnp
from jax import lax
from jax.experimental import pallas as pl
from jax.experimental.pallas import tpu as pltpu

F32 = jnp.float32
BF16 = jnp.bfloat16
I32 = jnp.int32

D = 1024
SEQ = 8192
CTX = 256
GRID_W = 64
ATT_W = 512
HY_W = 512
HD = 64
NQ = 8
NKV = 2
QPK = NQ // NKV
KV_W = NKV * HD
IN_W = ATT_W + 2 * KV_W + 3 * HY_W
FEMB = 33
FORD = 64
NE = 16
CAP = 2 * SEQ // NE
DEXP = 2752
ROPE_THETA = 10000.0
EPS = 1e-6
DECAY_TARGET = 1e-2
FAST_DECAY_PCT = 0.3
SLOW_DECAY_PCT = 1.5

LANES = 128
SUBLANES = 8
BF16_ROWS = 16
VMEM_BYTES_V7X = 64 * 1024 * 1024
VMEM_LIMIT = VMEM_BYTES_V7X - 8 * 1024 * 1024

FN = 2 * SEQ
FN1 = 128
FN2 = 128

TCH = LANES
NTCH = SEQ // TCH
GW = TCH + SUBLANES
CW = TCH + BF16_ROWS
YROWS = CAP + BF16_ROWS


def _cparams(sem, vmem=None):
    return pltpu.CompilerParams(dimension_semantics=sem, vmem_limit_bytes=vmem or VMEM_LIMIT)


def _split(a):
    hi = a.astype(BF16)
    lo = (a - hi.astype(F32)).astype(BF16)
    return hi, lo


_NN = (((1,), (0,)), ((), ()))
_NT = (((1,), (1,)), ((), ()))
_TN = (((0,), (0,)), ((), ()))


def _dot(a, b, dn=_NN):
    return lax.dot_general(a, b, dn, preferred_element_type=F32)


def _dot3(a, b, dn=_NN):
    ah, al = _split(a)
    bh, bl = _split(b)
    return _dot(ah, bh, dn) + _dot(ah, bl, dn) + _dot(al, bh, dn)


def _mod_body(c_ref, w_ref, b_ref, o_ref):
    c = c_ref[...]
    s = c * (1.0 / (1.0 + jnp.exp(-c)))
    o_ref[...] = _dot3(s, w_ref[...]) + b_ref[...]


def _modulation(cc, w_mod, b_mod):
    n = w_mod.shape[1]
    return pl.pallas_call(
        _mod_body,
        grid=(n // D,),
        in_specs=[pl.BlockSpec((SUBLANES, D), lambda j: (0, 0)),
                  pl.BlockSpec((D, D), lambda j: (0, j)),
                  pl.BlockSpec((1, D), lambda j: (0, j))],
        out_specs=pl.BlockSpec((SUBLANES, D), lambda j: (0, j)),
        out_shape=jax.ShapeDtypeStruct((SUBLANES, n), F32),
        compiler_params=_cparams(("arbitrary",)),
        name="modulation",
    )(cc, w_mod, b_mod)


Q_SCALE = HD ** -0.5 * math.log2(math.e)


def _rms_mod(x, g, sh, sc):
    ms = jnp.mean(x * x, axis=-1, keepdims=True)
    return (x * lax.rsqrt(ms + EPS) * g) * (1.0 + sc) + sh


def _head_norm_rope(t, g, bd, cos, sin):
    sq = t * t
    hi, lo = _split(sq)
    ms = _dot(hi, bd) + _dot(lo, bd)
    tn = t * lax.rsqrt(ms + EPS) * g
    lane = lax.broadcasted_iota(I32, tn.shape, 1)
    sw = jnp.where((lane & 31) < 16, pltpu.roll(tn, LANES - 16, 1), pltpu.roll(tn, 16, 1))
    return tn * cos + sw * sin


def _proj_body(x_ref, g_ref, sh_ref, sc_ref, w_ref, gq_ref, gk_ref, bd_ref, cos_ref, sin_ref,
               q_ref, k_ref, v_ref, p_ref):
    h = _rms_mod(x_ref[0], g_ref[...], sh_ref[0], sc_ref[0])
    proj = _dot(h.astype(BF16), w_ref[...])
    bd = bd_ref[...]
    cos = cos_ref[...]
    sin = sin_ref[...]
    for j in range(ATT_W // LANES):
        sl = slice(j * LANES, (j + 1) * LANES)
        qj = _head_norm_rope(proj[:, sl], gq_ref[...], bd, cos, sin)
        q_ref[0, :, sl] = (qj * Q_SCALE).astype(BF16)
    k_ref[0] = _head_norm_rope(proj[:, ATT_W:ATT_W + KV_W], gk_ref[...], bd, cos, sin).astype(BF16)
    v_ref[0] = proj[:, ATT_W + KV_W:ATT_W + 2 * KV_W].astype(BF16)
    p_ref[0] = proj[:, ATT_W + 2 * KV_W:]


def _in_projection(x, g1, sh, sc, w_in_bf, gq2, gk2, bd, cos_t, sin_t, tm):
    b, s, _ = x.shape
    row = lambda bi, i: (bi, 0, 0)
    tok = lambda bi, i: (bi, i, 0)
    const = lambda bi, i: (0, 0)
    return pl.pallas_call(
        _proj_body,
        grid=(b, s // tm),
        in_specs=[pl.BlockSpec((1, tm, D), tok),
                  pl.BlockSpec((1, D), const),
                  pl.BlockSpec((1, 1, D), row),
                  pl.BlockSpec((1, 1, D), row),
                  pl.BlockSpec((D, IN_W), const),
                  pl.BlockSpec((1, LANES), const),
                  pl.BlockSpec((1, LANES), const),
                  pl.BlockSpec((LANES, LANES), const),
                  pl.BlockSpec((tm, LANES), lambda bi, i: (i, 0)),
                  pl.BlockSpec((tm, LANES), lambda bi, i: (i, 0))],
        out_specs=[pl.BlockSpec((1, tm, ATT_W), tok),
                   pl.BlockSpec((1, tm, KV_W), tok),
                   pl.BlockSpec((1, tm, KV_W), tok),
                   pl.BlockSpec((1, tm, 3 * HY_W), tok)],
        out_shape=[jax.ShapeDtypeStruct((b, s, ATT_W), BF16),
                   jax.ShapeDtypeStruct((b, s, KV_W), BF16),
                   jax.ShapeDtypeStruct((b, s, KV_W), BF16),
                   jax.ShapeDtypeStruct((b, s, 3 * HY_W), F32)],
        compiler_params=_cparams(("parallel", "parallel")),
        name="in_projection",
    )(x, g1, sh, sc, w_in_bf, gq2, gk2, bd, cos_t, sin_t)


ATT_TQ = 256
ATT_TK = 768
SK = SEQ + CTX
ATT_NCH = SK // ATT_TK


ATT_NQ = QPK * ATT_TQ
ATT_VR = HD + BF16_ROWS
assert ATT_NCH % 2 == 1


def _attn_body(q_ref, k_ref, vt_ref, o_ref, s_ref, mx_ref, m_ref, acc_ref):
    qall = jnp.concatenate([q_ref[0, :, r * HD:(r + 1) * HD] for r in range(QPK)], axis=0)
    m_ref[...] = jnp.full(m_ref.shape, -1e30, F32)
    acc_ref[...] = jnp.zeros_like(acc_ref)

    def scores(c, slot):
        s = _dot(k_ref[0, 0, c], qall, _NT)
        s_ref[slot] = s
        mx_ref[slot] = jnp.max(s, axis=0, keepdims=True)

    def update(c, slot):
        m_old = m_ref[...]
        m_new = jnp.maximum(m_old, mx_ref[slot])
        p = jnp.exp2(s_ref[slot] - m_new).astype(BF16)
        acc_ref[...] = jnp.exp2(m_old - m_new) * acc_ref[...] + _dot(vt_ref[0, 0, c], p)
        m_ref[...] = m_new

    scores(0, 0)

    def pair(i, _):
        c = 2 * i
        scores(c + 1, 1)
        update(c, 0)
        scores(c + 2, 0)
        update(c + 1, 1)
        return 0

    lax.fori_loop(0, ATT_NCH // 2, pair, 0)
    update(ATT_NCH - 1, 0)
    out = acc_ref[0:HD, :] * (1.0 / acc_ref[HD:HD + 1, :])
    for r in range(QPK):
        o_ref[0, :, r * HD:(r + 1) * HD] = out[:, r * ATT_TQ:(r + 1) * ATT_TQ].T.astype(BF16)


def _attention(q, kch, vtch):
    b = q.shape[0]
    return pl.pallas_call(
        _attn_body,
        grid=(b, NKV, SEQ // ATT_TQ),
        in_specs=[pl.BlockSpec((1, ATT_TQ, QPK * HD), lambda bi, g, i: (bi, i, g)),
                  pl.BlockSpec((1, 1, ATT_NCH, ATT_TK, HD), lambda bi, g, i: (bi, g, 0, 0, 0)),
                  pl.BlockSpec((1, 1, ATT_NCH, ATT_VR, ATT_TK), lambda bi, g, i: (bi, g, 0, 0, 0))],
        out_specs=pl.BlockSpec((1, ATT_TQ, QPK * HD), lambda bi, g, i: (bi, i, g)),
        out_shape=jax.ShapeDtypeStruct((b, SEQ, ATT_W), BF16),
        scratch_shapes=[pltpu.VMEM((2, ATT_TK, ATT_NQ), F32), pltpu.VMEM((2, 1, ATT_NQ), F32),
                        pltpu.VMEM((1, ATT_NQ), F32), pltpu.VMEM((ATT_VR, ATT_NQ), F32)],
        compiler_params=_cparams(("parallel", "parallel", "parallel")),
        name="attention",
    )(q, kch, vtch)


SC_TM = 2048
SC_J = SC_TM // FN2


def _sconv_body(m1, a1, n1, m2, a2, n2, m3, a3, n3, w_ref, b_ref, u_ref, x2_ref):
    i = pl.program_id(1)
    last = pl.num_programs(1) - 1
    rows = lax.broadcasted_iota(I32, (SC_TM, HY_W), 0)

    def conv(main, prev, nxt, g):
        x = main[0]
        pr = jnp.where(i > 0, prev[0, SUBLANES - 1:SUBLANES, :], 0.0)
        nx = jnp.where(i < last, nxt[0, 0:1, :], 0.0)
        xm = jnp.where(rows == 0, pr, pltpu.roll(x, 1, 0))
        xp = jnp.where(rows == SC_TM - 1, nx, pltpu.roll(x, SC_TM - 1, 0))
        return (w_ref[g:g + 1, :] * xm + w_ref[3 + g:4 + g, :] * x + w_ref[6 + g:7 + g, :] * xp
                + b_ref[g:g + 1, :])

    def to_rj(t):
        return jnp.swapaxes(t.reshape(SC_J, FN2, HY_W), 0, 1).astype(BF16)

    x1 = conv(m1, a1, n1, 0)
    x2 = conv(m2, a2, n2, 1)
    v = conv(m3, a3, n3, 2)
    u_ref[0] = to_rj(v * x1)
    x2_ref[0] = to_rj(x2)


def _short_conv(p, cw9, cb3):
    b = p.shape[0]
    nblk8 = SEQ // SUBLANES
    step8 = SC_TM // SUBLANES
    specs = []
    for g in range(3):
        specs += [pl.BlockSpec((1, SC_TM, HY_W), lambda bi, i, g=g: (bi, i, g)),
                  pl.BlockSpec((1, SUBLANES, HY_W), lambda bi, i, g=g: (bi, jnp.maximum(i * step8 - 1, 0), g)),
                  pl.BlockSpec((1, SUBLANES, HY_W), lambda bi, i, g=g: (bi, jnp.minimum((i + 1) * step8, nblk8 - 1), g))]
    specs += [pl.BlockSpec((9, HY_W), lambda bi, i: (0, 0)), pl.BlockSpec((3, HY_W), lambda bi, i: (0, 0))]
    out = pl.BlockSpec((1, FN2, SC_J, HY_W), lambda bi, i: (bi, 0, i, 0))
    return pl.pallas_call(
        _sconv_body,
        grid=(b, SEQ // SC_TM),
        in_specs=specs,
        out_specs=[out, out],
        out_shape=[jax.ShapeDtypeStruct((b, FN2, SEQ // FN2, HY_W), BF16)] * 2,
        compiler_params=_cparams(("parallel", "parallel")),
        name="short_conv",
    )(p, p, p, p, p, p, p, p, p, cw9, cb3)


FILT_TR = 1024


def _filter_body(z_ref, t_ref, msk_ref, fwd_ref, w1_ref, b1_ref, w2_ref, b2_ref, w3_ref, fr_ref, dl_ref,
                 k_ref, s_ref):
    fr = fr_ref[...]
    h = jnp.sin(fr * (_dot3(z_ref[...], w1_ref[...]) + b1_ref[...]))
    h = jnp.sin(fr * (_dot3(h, w2_ref[...]) + b2_ref[...]))
    h = _dot3(h, w3_ref[...])
    h = jnp.where(fwd_ref[...] > 0.5, h[:, :HY_W], h[:, HY_W:])
    kern = h * jnp.exp(t_ref[...] * dl_ref[...]) * msk_ref[...]
    k_ref[...] = kern

    @pl.when(pl.program_id(0) == 0)
    def _():
        s_ref[...] = jnp.zeros_like(s_ref)

    s_ref[...] += jnp.sum(jnp.abs(kern), axis=0, keepdims=True)


def _implicit_filter(ztab, ttab, mtab, ftab, w1p, b1, w2, b2, w3, freq, negdelta):
    rowblk = lambda i: (i, 0)
    const = lambda i: (0, 0)
    col = pl.BlockSpec((FILT_TR, 1), rowblk)
    return pl.pallas_call(
        _filter_body,
        grid=(FN // FILT_TR,),
        in_specs=[pl.BlockSpec((FILT_TR, FORD), rowblk), col, col, col,
                  pl.BlockSpec((FORD, FORD), const),
                  pl.BlockSpec((1, FORD), const),
                  pl.BlockSpec((FORD, FORD), const),
                  pl.BlockSpec((1, FORD), const),
                  pl.BlockSpec((FORD, 2 * HY_W), const),
                  pl.BlockSpec((1, FORD), const),
                  pl.BlockSpec((1, HY_W), const)],
        out_specs=[pl.BlockSpec((FILT_TR, HY_W), rowblk),
                   pl.BlockSpec((1, HY_W), const)],
        out_shape=[jax.ShapeDtypeStruct((FN, HY_W), F32), jax.ShapeDtypeStruct((1, HY_W), F32)],
        compiler_params=_cparams(("arbitrary",)),
        name="implicit_filter",
    )(ztab, ttab, mtab, ftab, w1p, b1, w2, b2, w3, freq, negdelta)


DFT_G = 4
DFT_KB = 8
DFT_NP = FN1 // DFT_KB


def _dot2c(fh, fl, zb):
    return _dot(fh, zb) + _dot(fl, zb)


def _lead_stage(src, fh, fl, dst_ref):
    def group(rg, _):
        r0 = rg * DFT_G
        rhs = jnp.concatenate([src(r0 + g) for g in range(DFT_G)], axis=1)
        blk = _dot2c(fh, fl, rhs)
        for g in range(DFT_G):
            dst_ref[r0 + g] = blk[:, g * LANES:(g + 1) * LANES]
        return 0

    lax.fori_loop(0, FN2 // DFT_G, group, 0)


def _lead_phase(src, lh_ref, ll_ref, p_ref, q_ref):
    for h in range(2):
        rows = slice(h * FN1, (h + 1) * FN1)
        _lead_stage(src, lh_ref[rows, :], ll_ref[rows, :], p_ref)
        q_ref[rows] = jnp.swapaxes(p_ref[...], 0, 1)


def _lane_cat(xs):
    return jnp.concatenate(xs, axis=1)


def _mid_forward(q_ref, k0, tr_ref, ti_ref, fh, fl, half):
    brs, bis, trs, tis = [], [], [], []
    for g in range(DFT_G):
        jj = half * DFT_G + g
        ar, ai = q_ref[k0 + jj], q_ref[FN1 + k0 + jj]
        tr, ti = tr_ref[jj], ti_ref[jj]
        brs.append(ar * tr - ai * ti)
        bis.append(ar * ti + ai * tr)
        trs.append(tr)
        tis.append(ti)
    b = jnp.concatenate([_lane_cat(brs), _lane_cat(bis)], axis=0).astype(BF16)
    return _dot2c(fh, fl, b), _lane_cat(trs), _lane_cat(tis)


def _spectrum_body(k_ref, tr_ref, ti_ref, lh_ref, ll_ref, fh_ref, fl_ref, sc_ref, o_ref, p_ref, q_ref):
    ph = pl.program_id(1)

    @pl.when(ph == 0)
    def _():
        _lead_phase(lambda r: k_ref[r].astype(BF16), lh_ref, ll_ref, p_ref, q_ref)

    @pl.when(ph > 0)
    def _():
        k0 = (ph - 1) * DFT_KB
        sc = _lane_cat([sc_ref[...]] * DFT_G)
        for half in range(DFT_KB // DFT_G):
            x, _, _ = _mid_forward(q_ref, k0, tr_ref, ti_ref, fh_ref[...], fl_ref[...], half)
            x = x * sc
            for g in range(DFT_G):
                lanes = slice(g * LANES, (g + 1) * LANES)
                o_ref[0, half * DFT_G + g] = x[:FN2, lanes]
                o_ref[1, half * DFT_G + g] = x[FN2:, lanes]


def _mid_index(ph):
    return jnp.clip(ph - 1, 0, DFT_NP - 1)


def _filter_spectrum(kern_rj, twr, twi, lh, ll, fh, fl, scale):
    const = lambda c, ph: (0, 0)
    tw = pl.BlockSpec((DFT_KB, FN2, LANES), lambda c, ph: (_mid_index(ph), 0, 0))
    return pl.pallas_call(
        _spectrum_body,
        grid=(HY_W // LANES, DFT_NP + 1),
        in_specs=[pl.BlockSpec((FN2, FN1, LANES), lambda c, ph: (0, 0, c), pipeline_mode=pl.Buffered(1)),
                  tw, tw,
                  pl.BlockSpec(lh.shape, const), pl.BlockSpec(ll.shape, const),
                  pl.BlockSpec(fh.shape, const), pl.BlockSpec(fl.shape, const),
                  pl.BlockSpec((1, LANES), lambda c, ph: (0, c))],
        out_specs=pl.BlockSpec((2, DFT_KB, FN2, LANES), lambda c, ph: (0, _mid_index(ph), 0, c)),
        out_shape=jax.ShapeDtypeStruct((2, FN1, FN2, HY_W), F32),
        scratch_shapes=[pltpu.VMEM((FN2, FN1, LANES), F32), pltpu.VMEM((2 * FN1, FN2, LANES), F32)],
        compiler_params=_cparams(("parallel", "arbitrary")),
        name="filter_spectrum",
    )(kern_rj, twr, twi, lh, ll, fh, fl, scale)


def _hconv_body(u_ref, x2_ref, kh_ref, tr_ref, ti_ref, lh_ref, ll_ref, fh_ref, fl_ref, gh_ref, gl_ref,
                oh_ref, ol_ref, bias_ref, o_ref, p_ref, q_ref):
    ph = pl.program_id(1)

    def both(ref, r):
        return jnp.concatenate([ref[0, r], ref[1, r]], axis=0)

    @pl.when(ph == 0)
    def _():
        _lead_phase(lambda r: both(u_ref, r), lh_ref, ll_ref, p_ref, q_ref)

    @pl.when((ph > 0) & (ph <= DFT_NP))
    def _():
        k0 = (ph - 1) * DFT_KB
        for half in range(DFT_KB // DFT_G):
            x, tr, ti = _mid_forward(q_ref, k0, tr_ref, ti_ref, fh_ref[...], fl_ref[...], half)
            xr, xi = x[:FN2], x[FN2:]
            kr = _lane_cat([kh_ref[0, half * DFT_G + g] for g in range(DFT_G)])
            ki = _lane_cat([kh_ref[1, half * DFT_G + g] for g in range(DFT_G)])
            y = jnp.concatenate([xr * kr - xi * ki, xr * ki + xi * kr], axis=0).astype(BF16)
            c = _dot2c(gh_ref[...], gl_ref[...], y)
            cr, ci = c[:FN2], c[FN2:]
            dr = cr * tr + ci * ti
            di = ci * tr - cr * ti
            for g in range(DFT_G):
                lanes = slice(g * LANES, (g + 1) * LANES)
                q_ref[k0 + half * DFT_G + g] = dr[:, lanes]
                q_ref[FN1 + k0 + half * DFT_G + g] = di[:, lanes]

    @pl.when(ph == DFT_NP + 1)
    def _():
        bias = bias_ref[...]
        p_ref[...] = jnp.swapaxes(q_ref[0:FN1], 0, 1)
        _lead_stage(lambda r: p_ref[r].astype(BF16), oh_ref[:, 0:FN1], ol_ref[:, 0:FN1], q_ref)
        p_ref[...] = jnp.swapaxes(q_ref[FN1:2 * FN1], 0, 1)
        oh2, ol2 = oh_ref[:, FN1:2 * FN1], ol_ref[:, FN1:2 * FN1]

        def group(rg, _):
            r0 = rg * DFT_G
            rhs = _lane_cat([p_ref[r0 + g].astype(BF16) for g in range(DFT_G)])
            blk = _dot2c(oh2, ol2, rhs)
            for g in range(DFT_G):
                r = r0 + g
                y = q_ref[r] + blk[:, g * LANES:(g + 1) * LANES]
                q_ref[r] = (y + both(u_ref, r).astype(F32) * bias) * both(x2_ref, r).astype(F32)
            return 0

        lax.fori_loop(0, FN2 // DFT_G, group, 0)
        p_ref[...] = jnp.swapaxes(q_ref[0:FN2], 0, 1)
        nj = SEQ // FN2
        for b in range(2):
            o_ref[b] = p_ref[b * nj:(b + 1) * nj].reshape(SEQ, LANES).astype(BF16)


def _hyena_conv(u_rj, x2_rj, khat, twr, twi, lead, fwd, inv, out, bias):
    const = lambda c, ph: (0, 0)
    nj = SEQ // FN2
    tw = pl.BlockSpec((DFT_KB, FN2, LANES), lambda c, ph: (_mid_index(ph), 0, 0))
    sig = pl.BlockSpec((2, FN2, nj, LANES), lambda c, ph: (0, 0, 0, c), pipeline_mode=pl.Buffered(1))
    mats = [m for pair in (lead, fwd, inv, out) for m in pair]
    return pl.pallas_call(
        _hconv_body,
        grid=(HY_W // LANES, DFT_NP + 2),
        in_specs=[sig, sig,
                  pl.BlockSpec((2, DFT_KB, FN2, LANES), lambda c, ph: (0, _mid_index(ph), 0, c)),
                  tw, tw] + [pl.BlockSpec(m.shape, const) for m in mats]
                 + [pl.BlockSpec((1, LANES), lambda c, ph: (0, c))],
        out_specs=pl.BlockSpec((2, SEQ, LANES), lambda c, ph: (0, 0, c)),
        out_shape=jax.ShapeDtypeStruct((2, SEQ, HY_W), BF16),
        scratch_shapes=[pltpu.VMEM((FN2, FN1, LANES), F32), pltpu.VMEM((2 * FN1, FN2, LANES), F32)],
        compiler_params=_cparams(("parallel", "arbitrary")),
        name="hyena_conv",
    )(u_rj, x2_rj, khat, twr, twi, *mats, bias)


OP_TM = 512


def _outproj_body(att_ref, hy_ref, x_ref, w_ref, g1_ref, n2_ref, sh_ref, sc_ref, wrh_ref, wrl_ref,
                  x1_ref, h2_ref, lg_ref):
    a = jnp.concatenate([att_ref[0], hy_ref[0]], axis=1)
    x1 = x_ref[0] + g1_ref[0] * _dot(a, w_ref[...])
    x1_ref[0] = x1
    h2 = _rms_mod(x1, n2_ref[...], sh_ref[0], sc_ref[0])
    hh, hl = _split(h2)
    h2_ref[0] = hh
    wrh = wrh_ref[...]
    lg_ref[0] = _dot(wrh, hh, _NT) + _dot(wrh, hl, _NT) + _dot(wrl_ref[...], hh, _NT)


def _out_projection(att, hy, x, w_out_bf, g1r, n2g, sh2, sc2, wrh, wrl):
    b = x.shape[0]
    tok = lambda bi, i: (bi, i, 0)
    row = lambda bi, i: (bi, 0, 0)
    const = lambda bi, i: (0, 0)
    return pl.pallas_call(
        _outproj_body,
        grid=(b, SEQ // OP_TM),
        in_specs=[pl.BlockSpec((1, OP_TM, ATT_W), tok),
                  pl.BlockSpec((1, OP_TM, HY_W), tok),
                  pl.BlockSpec((1, OP_TM, D), tok),
                  pl.BlockSpec((ATT_W + HY_W, D), const),
                  pl.BlockSpec((1, 1, D), row),
                  pl.BlockSpec((1, D), const),
                  pl.BlockSpec((1, 1, D), row),
                  pl.BlockSpec((1, 1, D), row),
                  pl.BlockSpec((NE, D), const),
                  pl.BlockSpec((NE, D), const)],
        out_specs=[pl.BlockSpec((1, OP_TM, D), tok),
                   pl.BlockSpec((1, OP_TM, D), tok),
                   pl.BlockSpec((1, NE, OP_TM), lambda bi, i: (bi, 0, i))],
        out_shape=[jax.ShapeDtypeStruct((b, SEQ, D), F32),
                   jax.ShapeDtypeStruct((b, SEQ, D), BF16),
                   jax.ShapeDtypeStruct((b, NE, SEQ), F32)],
        compiler_params=_cparams(("parallel", "parallel")),
        name="out_projection",
    )(att, hy, x, w_out_bf, g1r, n2g, sh2, sc2, wrh, wrl)


def _routing_body(lg_ref, tri_ref, pos_ref, gate_ref, off_ref, cs_ref):
    lg = lg_ref[0]
    e = jnp.exp(lg - jnp.max(lg, axis=0, keepdims=True))
    aff = e / jnp.sum(e, axis=0, keepdims=True)
    gate_ref[0] = aff
    def count_ge(t):
        return jnp.sum(jnp.where(aff >= t, 1.0, 0.0), axis=1, keepdims=True)

    def bisect(i, thr):
        cand = thr | (jnp.int32(1) << (30 - i))
        return jnp.where(count_ge(pltpu.bitcast(cand, F32)) >= float(CAP), cand, thr)

    thr = lax.fori_loop(0, 31, bisect, jnp.zeros((NE, 1), I32))
    lo = pltpu.bitcast(thr, F32)
    hi = jnp.maximum(pltpu.bitcast(thr + 1, F32), jnp.finfo(F32).tiny)

    def refine(i, c):
        lo, hi = c
        mid = lo + (hi - lo) * 0.5
        ok = count_ge(mid) >= float(CAP)
        return jnp.where(ok, mid, lo), jnp.where(ok, hi, mid)

    lo, hi = lax.fori_loop(0, 32, refine, (lo, hi))
    gt = aff >= hi
    eq = (aff >= lo) & jnp.logical_not(gt)
    need = float(CAP) - jnp.sum(jnp.where(gt, 1.0, 0.0), axis=1, keepdims=True)
    tri = tri_ref[...]

    def excl_cumsum(mask_f, record_offsets):
        carry = jnp.zeros((NE, 1), F32)
        for c in range(NTCH):
            sl = slice(c * TCH, (c + 1) * TCH)
            m = mask_f[:, sl]
            inc = _dot(m.astype(BF16), tri)
            cs_ref[:, sl] = inc - m + carry
            if record_offsets:
                off_ref[0, :, c:c + 1] = carry.astype(I32)
            carry = carry + inc[:, TCH - 1:TCH]
        return cs_ref[...]

    eq_rank = excl_cumsum(jnp.where(eq, 1.0, 0.0), False)
    sel = gt | (eq & (eq_rank < need))
    pos = excl_cumsum(jnp.where(sel, 1.0, 0.0), True)
    pos_ref[0] = jnp.where(sel, pos.astype(I32), -1)


def _routing(logits, tri):
    b = logits.shape[0]
    blk = pl.BlockSpec((1, NE, SEQ), lambda bi: (bi, 0, 0))
    return pl.pallas_call(
        _routing_body,
        grid=(b,),
        in_specs=[blk, pl.BlockSpec((TCH, TCH), lambda bi: (0, 0))],
        out_specs=[blk, blk, pl.BlockSpec((1, NE, NTCH), lambda bi: (bi, 0, 0))],
        out_shape=[jax.ShapeDtypeStruct((b, NE, SEQ), I32),
                   jax.ShapeDtypeStruct((b, NE, SEQ), F32),
                   jax.ShapeDtypeStruct((b, NE, NTCH), I32)],
        scratch_shapes=[pltpu.VMEM((NE, SEQ), F32)],
        compiler_params=_cparams(("parallel",)),
        name="routing",
    )(logits, tri)


GATHER_UNROLL = 4


def _gather_body(off_ref, h_ref, pos_ref, xg_ref, acc_ref):
    b = pl.program_id(0)
    e = pl.program_id(1)
    acc_ref[...] = jnp.zeros_like(acc_ref)
    crow = lax.broadcasted_iota(I32, (GW, TCH), 0)

    def chunks(i, _):
        for j in range(GATHER_UNROLL):
            c = i * GATHER_UNROLL + j
            off = off_ref[(b * NE + e) * NTCH + c]
            base = pl.multiple_of(jnp.minimum((off >> 3) << 3, CAP - TCH), SUBLANES)
            t0 = pl.multiple_of(c * TCH, TCH)
            rel = pos_ref[0, 0, pl.ds(c, 1), :] - base
            onehot = jnp.where(crow == rel, 1.0, 0.0).astype(BF16)
            acc_ref[pl.ds(base, GW), :] += _dot(onehot, h_ref[0, pl.ds(t0, TCH), :])
        return 0

    lax.fori_loop(0, NTCH // GATHER_UNROLL, chunks, 0)
    xg_ref[0, 0] = acc_ref[0:CAP, :].astype(BF16)


def _gather(offs_flat, h2, pos4):
    b = h2.shape[0]
    grid_spec = pltpu.PrefetchScalarGridSpec(
        num_scalar_prefetch=1,
        grid=(b, NE),
        in_specs=[pl.BlockSpec((1, SEQ, D), lambda bi, e, off: (bi, 0, 0)),
                  pl.BlockSpec((1, 1, NTCH, TCH), lambda bi, e, off: (bi, e, 0, 0))],
        out_specs=pl.BlockSpec((1, 1, CAP, D), lambda bi, e, off: (bi, e, 0, 0)),
        scratch_shapes=[pltpu.VMEM((CAP + SUBLANES, D), F32)],
    )
    return pl.pallas_call(
        _gather_body,
        grid_spec=grid_spec,
        out_shape=jax.ShapeDtypeStruct((b, NE, CAP, D), BF16),
        compiler_params=_cparams(("parallel", "arbitrary")),
        name="moe_gather",
    )(offs_flat, h2, pos4)


FFN_TM = 512
FFN_NF = 4
FFN_FC = DEXP // FFN_NF
assert FFN_FC * FFN_NF == DEXP and FFN_FC % BF16_ROWS == 0


def _ffn_body(xg_ref, wgt_ref, wut_ref, wd_ref, y_ref, acc_ref):
    j = pl.program_id(2)

    @pl.when(j == 0)
    def _():
        acc_ref[...] = jnp.zeros_like(acc_ref)

    wgt = wgt_ref[0].astype(BF16)
    wut = wut_ref[0].astype(BF16)
    wd = wd_ref[0].astype(BF16)
    for mb in range(CAP // FFN_TM):
        rows = slice(mb * FFN_TM, (mb + 1) * FFN_TM)
        xb = xg_ref[0, 0, rows, :]
        a = _dot(xb, wgt, _NT)
        u = _dot(xb, wut, _NT)
        h = (a * (1.0 / (1.0 + jnp.exp(-a))) * u).astype(BF16)
        acc_ref[rows, :] += _dot(h, wd)

    @pl.when(j == FFN_NF - 1)
    def _():
        y_ref[0, 0, 0:CAP, :] = acc_ref[...].astype(BF16)
        y_ref[0, 0, CAP:YROWS, :] = jnp.zeros((YROWS - CAP, D), BF16)


def _expert_ffn(xg, w_gate_t, w_up_t, w_down):
    b = xg.shape[0]
    wblk = pl.BlockSpec((1, FFN_FC, D), lambda e, bi, j: (e, j, 0))
    return pl.pallas_call(
        _ffn_body,
        grid=(NE, b, FFN_NF),
        in_specs=[pl.BlockSpec((1, 1, CAP, D), lambda e, bi, j: (bi, e, 0, 0)), wblk, wblk, wblk],
        out_specs=pl.BlockSpec((1, 1, YROWS, D), lambda e, bi, j: (bi, e, 0, 0)),
        out_shape=jax.ShapeDtypeStruct((b, NE, YROWS, D), BF16),
        scratch_shapes=[pltpu.VMEM((CAP, D), F32)],
        compiler_params=_cparams(("parallel", "parallel", "arbitrary")),
        name="expert_ffn",
    )(xg, w_gate_t, w_up_t, w_down)


def _combine_body(off_ref, y_ref, pos_ref, gate_ref, x1_ref, g2_ref, o_ref):
    b = pl.program_id(0)
    i = pl.program_id(1)
    crow = lax.broadcasted_iota(I32, (CW, TCH), 0)
    acc = jnp.zeros((TCH, D), F32)
    for e in range(NE):
        off = off_ref[(b * NE + e) * NTCH + i]
        base = pl.multiple_of(jnp.minimum((off >> 4) << 4, CAP - TCH), BF16_ROWS)
        rel = pos_ref[0, e:e + 1, :] - base
        w = jnp.where(crow == rel, gate_ref[0, e:e + 1, :], 0.0)
        yw = y_ref[0, e, pl.ds(base, CW), :]
        acc = acc + _dot(w.astype(BF16), yw, _TN)
    o_ref[0] = x1_ref[0] + g2_ref[0] * acc


def _combine(offs_flat, y, pos, gate, x1, g2r):
    b = x1.shape[0]
    grid_spec = pltpu.PrefetchScalarGridSpec(
        num_scalar_prefetch=1,
        grid=(b, NTCH),
        in_specs=[pl.BlockSpec((1, NE, YROWS, D), lambda bi, i, off: (bi, 0, 0, 0),
                               pipeline_mode=pl.Buffered(1)),
                  pl.BlockSpec((1, NE, TCH), lambda bi, i, off: (bi, 0, i)),
                  pl.BlockSpec((1, NE, TCH), lambda bi, i, off: (bi, 0, i)),
                  pl.BlockSpec((1, TCH, D), lambda bi, i, off: (bi, i, 0)),
                  pl.BlockSpec((1, 1, D), lambda bi, i, off: (bi, 0, 0))],
        out_specs=pl.BlockSpec((1, TCH, D), lambda bi, i, off: (bi, i, 0)),
    )
    return pl.pallas_call(
        _combine_body,
        grid_spec=grid_spec,
        out_shape=jax.ShapeDtypeStruct((b, SEQ, D), F32),
        compiler_params=_cparams(("parallel", "arbitrary")),
        name="moe_combine",
    )(offs_flat, y, pos, gate, x1, g2r)


def _np_split(m):
    hi = np.asarray(m, np.float64).astype(BF16)
    lo = (m - hi.astype(np.float64)).astype(BF16)
    return jnp.asarray(hi), jnp.asarray(lo)


@functools.lru_cache(maxsize=None)
def _dft_tables():
    a = np.arange(FN1, dtype=np.float64)
    ang = 2.0 * np.pi * np.outer(a, a) / FN1
    fr, fi = np.cos(ang), -np.sin(ang)
    half = SEQ // FN2
    lead_u = np.block([[fr[:, :half], -fi[:, :half]], [fi[:, :half], fr[:, :half]]])
    lead_k = np.concatenate([fr, fi], axis=0)
    fwd = np.block([[fr, -fi], [fi, fr]])
    inv = np.block([[fr, fi], [-fi, fr]])
    out = np.block([[fr[:half], fi[:half]], [-fi[:half], fr[:half]]])
    n2 = np.arange(FN2, dtype=np.float64)
    tw = 2.0 * np.pi * np.outer(a, n2) / FN
    twr = np.broadcast_to(np.cos(tw)[:, :, None], (FN1, FN2, LANES)).astype(np.float32)
    twi = np.broadcast_to(-np.sin(tw)[:, :, None], (FN1, FN2, LANES)).astype(np.float32)
    return dict(lead_u=lead_u, lead_k=lead_k, fwd=fwd, inv=inv, out=out, twr=twr, twi=twi)


@functools.lru_cache(maxsize=None)
def _filter_tables():
    L = SEQ
    n = np.arange(FN).reshape(FN1, FN2).T.reshape(-1)
    lag = np.where(n < L, n, FN - n)
    jc = np.minimum(lag, L - 1).astype(np.float64)
    t = (jc / (L - 1))[:, None]
    bands = (FEMB - 1) // 2
    w = 2.0 * np.pi * jc / L
    f = np.linspace(1e-4, bands - 1, bands)
    fw = w[:, None] * f[None, :]
    z = np.concatenate([t, np.cos(fw), -np.sin(fw), np.zeros((FN, FORD - FEMB))], axis=-1)
    mask = np.where(n == L, 0.0, 1.0)[:, None]
    fwd = np.where(n < L, 1.0, 0.0)[:, None]
    max_decay = math.log(DECAY_TARGET) / FAST_DECAY_PCT
    min_decay = math.log(DECAY_TARGET) / SLOW_DECAY_PCT
    negdelta = -np.abs(np.linspace(min_decay, max_decay, HY_W))[None, :]
    return tuple(np.asarray(a, np.float32) for a in (z, t, mask, fwd, negdelta))


@functools.lru_cache(maxsize=None)
def _rope_tables(n):
    rows = n // GRID_W
    row_id, col_id = np.meshgrid(np.arange(rows, dtype=np.float64), np.arange(GRID_W, dtype=np.float64), indexing="ij")
    quarter = HD // 4
    inv_freq = ROPE_THETA ** (-np.arange(quarter, dtype=np.float64) / quarter)
    ar = row_id.reshape(-1)[:, None] * inv_freq
    ac = col_id.reshape(-1)[:, None] * inv_freq
    cos = np.concatenate([np.cos(ar), np.cos(ar), np.cos(ac), np.cos(ac)], axis=-1)
    sin = np.concatenate([-np.sin(ar), np.sin(ar), -np.sin(ac), np.sin(ac)], axis=-1)
    reps = (1, LANES // HD)
    return np.tile(cos, reps).astype(np.float32), np.tile(sin, reps).astype(np.float32)


def _hyena_long_conv(u_rj, x2_rj, kern, abs_sum, bias):
    tb = _dft_tables()
    twr, twi = tb["twr"], tb["twi"]
    fwd = _np_split(tb["fwd"])
    scale = 1.0 / (abs_sum * float(FN))
    khat = _filter_spectrum(kern.reshape(FN2, FN1, HY_W), twr, twi, *_np_split(tb["lead_k"]), *fwd, scale)
    return _hyena_conv(u_rj, x2_rj, khat, twr, twi, _np_split(tb["lead_u"]), fwd, _np_split(tb["inv"]),
                       _np_split(tb["out"]), bias.reshape(1, HY_W))


def kernel(x, c, ctx, c_ctx, w_mod, b_mod, norm1_g, norm2_g, w_in, w_out, q_norm_g, k_norm_g,
           conv_w, conv_b, filt_w1, filt_b1, filt_w2, filt_b2, filt_w3, filt_freq, hyena_bias,
           w_router, w_gate, w_up, w_down):
    B = x.shape[0]
    assert x.shape == (B, SEQ, D) and B == 2 and ctx.shape == (B, CTX, D) and w_mod.shape[0] == 1
    l = 0

    cc = jnp.concatenate([c, c_ctx[None, :], jnp.zeros((SUBLANES - B - 1, D), F32)], axis=0)
    mod = _modulation(cc, w_mod[l], b_mod[l][None, :])
    sh1, sc1, g1, sh2, sc2, g2 = [mod[:, i * D:(i + 1) * D] for i in range(6)]
    lat = lambda m: m[:B, None, :]
    ctxrow = lambda m: jnp.broadcast_to(m[B:B + 1, None, :], (B, 1, D))

    w_in_bf = w_in[l].astype(BF16)
    gq2 = jnp.tile(q_norm_g[l][None, :], (1, LANES // HD))
    gk2 = jnp.tile(k_norm_g[l][None, :], (1, LANES // HD))
    bd = jnp.asarray(np.kron(np.eye(LANES // HD), np.full((HD, HD), 1.0 / HD)), BF16)
    cos_t, sin_t = _rope_tables(SEQ)
    n1g = norm1_g[l][None, :]

    q, k, v, p = _in_projection(x, n1g, lat(sh1), lat(sc1), w_in_bf, gq2, gk2, bd, cos_t, sin_t, 512)
    _, kc, vc, _ = _in_projection(ctx, n1g, ctxrow(sh1), ctxrow(sc1), w_in_bf, gq2, gk2, bd,
                                  jnp.ones((CTX, LANES), F32), jnp.zeros((CTX, LANES), F32), CTX)

    k_all = jnp.concatenate([k, kc], axis=1).reshape(B, ATT_NCH, ATT_TK, NKV, HD)
    v_all = jnp.concatenate([v, vc], axis=1).reshape(B, ATT_NCH, ATT_TK, NKV, HD)
    kch = k_all.transpose(0, 3, 1, 2, 4)
    ones_pad = jnp.concatenate([jnp.ones((B, NKV, ATT_NCH, 1, ATT_TK), BF16),
                                jnp.zeros((B, NKV, ATT_NCH, BF16_ROWS - 1, ATT_TK), BF16)], axis=3)
    vtch = jnp.concatenate([v_all.transpose(0, 3, 1, 4, 2), ones_pad], axis=3)
    att = _attention(q, kch, vtch)

    cw9 = conv_w[l].reshape(3, 3, HY_W).reshape(9, HY_W)
    cb3 = conv_b[l].reshape(3, HY_W)
    u_rj, x2_rj = _short_conv(p, cw9, cb3)
    ztab, ttab, mtab, ftab, negdelta = _filter_tables()
    w1p = jnp.concatenate([filt_w1[l], jnp.zeros((FORD - FEMB, FORD), F32)], axis=0)
    kern, abs_sum = _implicit_filter(ztab, ttab, mtab, ftab, w1p, filt_b1[l][None, :], filt_w2[l],
                                     filt_b2[l][None, :], filt_w3[l], filt_freq[l][None, :], negdelta)
    hy = _hyena_long_conv(u_rj, x2_rj, kern, abs_sum, hyena_bias[l])

    wrh, wrl = _split(w_router[l].T)
    x1, h2, logits = _out_projection(att, hy, x, w_out[l].astype(BF16), lat(g1), norm2_g[l][None, :],
                                     lat(sh2), lat(sc2), wrh, wrl)

    tri = jnp.asarray(np.triu(np.ones((TCH, TCH))), BF16)
    pos, gate, offs = _routing(logits, tri)
    offs_flat = offs.reshape(-1)
    xg = _gather(offs_flat, h2, pos.reshape(B, NE, NTCH, TCH))
    y = _expert_ffn(xg, jnp.swapaxes(w_gate[l], 1, 2), jnp.swapaxes(w_up[l], 1, 2), w_down[l])
    return _combine(offs_flat, y, pos, gate, x1, lat(g2))
```

```python
import functools
import math

import numpy as np
import jax
import jax.numpy as jnp
from jax import lax
from jax.experimental import pallas as pl
from jax.experimental.pallas import tpu as pltpu

F32 = jnp.float32
BF16 = jnp.bfloat16
I32 = jnp.int32

D = 1024
SEQ = 8192
CTX = 256
GRID_W = 64
ATT_W = 512
HY_W = 512
HD = 64
NQ = 8
NKV = 2
QPK = NQ // NKV
KV_W = NKV * HD
IN_W = ATT_W + 2 * KV_W + 3 * HY_W
FEMB = 33
FORD = 64
NE = 16
CAP = 2 * SEQ // NE
DEXP = 2752
ROPE_THETA = 10000.0
EPS = 1e-6
DECAY_TARGET = 1e-2
FAST_DECAY_PCT = 0.3
SLOW_DECAY_PCT = 1.5

LANES = 128
SUBLANES = 8
BF16_ROWS = 16
VMEM_BYTES_V7X = 64 * 1024 * 1024
VMEM_LIMIT = VMEM_BYTES_V7X - 8 * 1024 * 1024

FN = 2 * SEQ
FN1 = 128
FN2 = 128

TCH = LANES
NTCH = SEQ // TCH
GW = TCH + SUBLANES
CW = TCH + BF16_ROWS
YROWS = CAP + BF16_ROWS


def _cparams(sem, vmem=None):
    return pltpu.CompilerParams(dimension_semantics=sem, vmem_limit_bytes=vmem or VMEM_LIMIT)


def _split(a):
    hi = a.astype(BF16)
    lo = (a - hi.astype(F32)).astype(BF16)
    return hi, lo


_NN = (((1,), (0,)), ((), ()))
_NT = (((1,), (1,)), ((), ()))
_TN = (((0,), (0,)), ((), ()))


def _dot(a, b, dn=_NN):
    return lax.dot_general(a, b, dn, preferred_element_type=F32)


def _dot3(a, b, dn=_NN):
    ah, al = _split(a)
    bh, bl = _split(b)
    return _dot(ah, bh, dn) + _dot(ah, bl, dn) + _dot(al, bh, dn)


def _mod_body(c_ref, w_ref, b_ref, o_ref):
    c = c_ref[...]
    s = c * (1.0 / (1.0 + jnp.exp(-c)))
    o_ref[...] = _dot3(s, w_ref[...]) + b_ref[...]


def _modulation(cc, w_mod, b_mod):
    n = w_mod.shape[1]
    return pl.pallas_call(
        _mod_body,
        grid=(n // D,),
        in_specs=[pl.BlockSpec((SUBLANES, D), lambda j: (0, 0)),
                  pl.BlockSpec((D, D), lambda j: (0, j)),
                  pl.BlockSpec((1, D), lambda j: (0, j))],
        out_specs=pl.BlockSpec((SUBLANES, D), lambda j: (0, j)),
        out_shape=jax.ShapeDtypeStruct((SUBLANES, n), F32),
        compiler_params=_cparams(("arbitrary",)),
        name="modulation",
    )(cc, w_mod, b_mod)


Q_SCALE = HD ** -0.5 * math.log2(math.e)


def _rms_mod(x, g, sh, sc):
    ms = jnp.mean(x * x, axis=-1, keepdims=True)
    return (x * lax.rsqrt(ms + EPS) * g) * (1.0 + sc) + sh


def _head_norm_rope(t, g, bd, cos, sin):
    sq = t * t
    hi, lo = _split(sq)
    ms = _dot(hi, bd) + _dot(lo, bd)
    tn = t * lax.rsqrt(ms + EPS) * g
    lane = lax.broadcasted_iota(I32, tn.shape, 1)
    sw = jnp.where((lane & 31) < 16, pltpu.roll(tn, LANES - 16, 1), pltpu.roll(tn, 16, 1))
    return tn * cos + sw * sin


def _proj_body(x_ref, g_ref, sh_ref, sc_ref, w_ref, gq_ref, gk_ref, bd_ref, cos_ref, sin_ref,
               q_ref, k_ref, v_ref, p_ref):
    h = _rms_mod(x_ref[0], g_ref[...], sh_ref[0], sc_ref[0])
    proj = _dot(h.astype(BF16), w_ref[...])
    bd = bd_ref[...]
    cos = cos_ref[...]
    sin = sin_ref[...]
    for j in range(ATT_W // LANES):
        sl = slice(j * LANES, (j + 1) * LANES)
        qj = _head_norm_rope(proj[:, sl], gq_ref[...], bd, cos, sin)
        q_ref[0, :, sl] = (qj * Q_SCALE).astype(BF16)
    k_ref[0] = _head_norm_rope(proj[:, ATT_W:ATT_W + KV_W], gk_ref[...], bd, cos, sin).astype(BF16)
    v_ref[0] = proj[:, ATT_W + KV_W:ATT_W + 2 * KV_W].astype(BF16)
    p_ref[0] = proj[:, ATT_W + 2 * KV_W:]


def _in_projection(x, g1, sh, sc, w_in_bf, gq2, gk2, bd, cos_t, sin_t, tm):
    b, s, _ = x.shape
    row = lambda bi, i: (bi, 0, 0)
    tok = lambda bi, i: (bi, i, 0)
    const = lambda bi, i: (0, 0)
    return pl.pallas_call(
        _proj_body,
        grid=(b, s // tm),
        in_specs=[pl.BlockSpec((1, tm, D), tok),
                  pl.BlockSpec((1, D), const),
                  pl.BlockSpec((1, 1, D), row),
                  pl.BlockSpec((1, 1, D), row),
                  pl.BlockSpec((D, IN_W), const),
                  pl.BlockSpec((1, LANES), const),
                  pl.BlockSpec((1, LANES), const),
                  pl.BlockSpec((LANES, LANES), const),
                  pl.BlockSpec((tm, LANES), lambda bi, i: (i, 0)),
                  pl.BlockSpec((tm, LANES), lambda bi, i: (i, 0))],
        out_specs=[pl.BlockSpec((1, tm, ATT_W), tok),
                   pl.BlockSpec((1, tm, KV_W), tok),
                   pl.BlockSpec((1, tm, KV_W), tok),
                   pl.BlockSpec((1, tm, 3 * HY_W), tok)],
        out_shape=[jax.ShapeDtypeStruct((b, s, ATT_W), BF16),
                   jax.ShapeDtypeStruct((b, s, KV_W), BF16),
                   jax.ShapeDtypeStruct((b, s, KV_W), BF16),
                   jax.ShapeDtypeStruct((b, s, 3 * HY_W), F32)],
        compiler_params=_cparams(("parallel", "parallel")),
        name="in_projection",
    )(x, g1, sh, sc, w_in_bf, gq2, gk2, bd, cos_t, sin_t)


ATT_TQ = 256
ATT_TK = 768
SK = SEQ + CTX
ATT_NCH = SK // ATT_TK


ATT_NQ = QPK * ATT_TQ
ATT_VR = HD + BF16_ROWS
assert ATT_NCH % 2 == 1


def _attn_body(q_ref, k_ref, vt_ref, o_ref, s_ref, mx_ref, m_ref, acc_ref):
    qall = jnp.concatenate([q_ref[0, :, r * HD:(r + 1) * HD] for r in range(QPK)], axis=0)
    m_ref[...] = jnp.full(m_ref.shape, -1e30, F32)
    acc_ref[...] = jnp.zeros_like(acc_ref)

    def scores(c, slot):
        s = _dot(k_ref[0, 0, c], qall, _NT)
        s_ref[slot] = s
        mx_ref[slot] = jnp.max(s, axis=0, keepdims=True)

    def update(c, slot):
        m_old = m_ref[...]
        m_new = jnp.maximum(m_old, mx_ref[slot])
        p = jnp.exp2(s_ref[slot] - m_new).astype(BF16)
        acc_ref[...] = jnp.exp2(m_old - m_new) * acc_ref[...] + _dot(vt_ref[0, 0, c], p)
        m_ref[...] = m_new

    scores(0, 0)

    def pair(i, _):
        c = 2 * i
        scores(c + 1, 1)
        update(c, 0)
        scores(c + 2, 0)
        update(c + 1, 1)
        return 0

    lax.fori_loop(0, ATT_NCH // 2, pair, 0)
    update(ATT_NCH - 1, 0)
    out = acc_ref[0:HD, :] * (1.0 / acc_ref[HD:HD + 1, :])
    for r in range(QPK):
        o_ref[0, :, r * HD:(r + 1) * HD] = out[:, r * ATT_TQ:(r + 1) * ATT_TQ].T.astype(BF16)


def _attention(q, kch, vtch):
    b = q.shape[0]
    return pl.pallas_call(
        _attn_body,
        grid=(b, NKV, SEQ // ATT_TQ),
        in_specs=[pl.BlockSpec((1, ATT_TQ, QPK * HD), lambda bi, g, i: (bi, i, g)),
                  pl.BlockSpec((1, 1, ATT_NCH, ATT_TK, HD), lambda bi, g, i: (bi, g, 0, 0, 0)),
                  pl.BlockSpec((1, 1, ATT_NCH, ATT_VR, ATT_TK), lambda bi, g, i: (bi, g, 0, 0, 0))],
        out_specs=pl.BlockSpec((1, ATT_TQ, QPK * HD), lambda bi, g, i: (bi, i, g)),
        out_shape=jax.ShapeDtypeStruct((b, SEQ, ATT_W), BF16),
        scratch_shapes=[pltpu.VMEM((2, ATT_TK, ATT_NQ), F32), pltpu.VMEM((2, 1, ATT_NQ), F32),
                        pltpu.VMEM((1, ATT_NQ), F32), pltpu.VMEM((ATT_VR, ATT_NQ), F32)],
        compiler_params=_cparams(("parallel", "parallel", "parallel")),
        name="attention",
    )(q, kch, vtch)


SC_TM = 2048
SC_J = SC_TM // FN2


def _sconv_body(m1, a1, n1, m2, a2, n2, m3, a3, n3, w_ref, b_ref, u_ref, x2_ref):
    i = pl.program_id(1)
    last = pl.num_programs(1) - 1
    rows = lax.broadcasted_iota(I32, (SC_TM, HY_W), 0)

    def conv(main, prev, nxt, g):
        x = main[0]
        pr = jnp.where(i > 0, prev[0, SUBLANES - 1:SUBLANES, :], 0.0)
        nx = jnp.where(i < last, nxt[0, 0:1, :], 0.0)
        xm = jnp.where(rows == 0, pr, pltpu.roll(x, 1, 0))
        xp = jnp.where(rows == SC_TM - 1, nx, pltpu.roll(x, SC_TM - 1, 0))
        return (w_ref[g:g + 1, :] * xm + w_ref[3 + g:4 + g, :] * x + w_ref[6 + g:7 + g, :] * xp
                + b_ref[g:g + 1, :])

    def to_rj(t):
        return jnp.swapaxes(t.reshape(SC_J, FN2, HY_W), 0, 1).astype(BF16)

    x1 = conv(m1, a1, n1, 0)
    x2 = conv(m2, a2, n2, 1)
    v = conv(m3, a3, n3, 2)
    u_ref[0] = to_rj(v * x1)
    x2_ref[0] = to_rj(x2)


def _short_conv(p, cw9, cb3):
    b = p.shape[0]
    nblk8 = SEQ // SUBLANES
    step8 = SC_TM // SUBLANES
    specs = []
    for g in range(3):
        specs += [pl.BlockSpec((1, SC_TM, HY_W), lambda bi, i, g=g: (bi, i, g)),
                  pl.BlockSpec((1, SUBLANES, HY_W), lambda bi, i, g=g: (bi, jnp.maximum(i * step8 - 1, 0), g)),
                  pl.BlockSpec((1, SUBLANES, HY_W), lambda bi, i, g=g: (bi, jnp.minimum((i + 1) * step8, nblk8 - 1), g))]
    specs += [pl.BlockSpec((9, HY_W), lambda bi, i: (0, 0)), pl.BlockSpec((3, HY_W), lambda bi, i: (0, 0))]
    out = pl.BlockSpec((1, FN2, SC_J, HY_W), lambda bi, i: (bi, 0, i, 0))
    return pl.pallas_call(
        _sconv_body,
        grid=(b, SEQ // SC_TM),
        in_specs=specs,
        out_specs=[out, out],
        out_shape=[jax.ShapeDtypeStruct((b, FN2, SEQ // FN2, HY_W), BF16)] * 2,
        compiler_params=_cparams(("parallel", "parallel")),
        name="short_conv",
    )(p, p, p, p, p, p, p, p, p, cw9, cb3)


FILT_TR = 1024


def _filter_body(z_ref, t_ref, msk_ref, fwd_ref, w1_ref, b1_ref, w2_ref, b2_ref, w3_ref, fr_ref, dl_ref,
                 k_ref, s_ref):
    fr = fr_ref[...]
    h = jnp.sin(fr * (_dot3(z_ref[...], w1_ref[...]) + b1_ref[...]))
    h = jnp.sin(fr * (_dot3(h, w2_ref[...]) + b2_ref[...]))
    h = _dot3(h, w3_ref[...])
    h = jnp.where(fwd_ref[...] > 0.5, h[:, :HY_W], h[:, HY_W:])
    kern = h * jnp.exp(t_ref[...] * dl_ref[...]) * msk_ref[...]
    k_ref[...] = kern

    @pl.when(pl.program_id(0) == 0)
    def _():
        s_ref[...] = jnp.zeros_like(s_ref)

    s_ref[...] += jnp.sum(jnp.abs(kern), axis=0, keepdims=True)


def _implicit_filter(ztab, ttab, mtab, ftab, w1p, b1, w2, b2, w3, freq, negdelta):
    rowblk = lambda i: (i, 0)
    const = lambda i: (0, 0)
    col = pl.BlockSpec((FILT_TR, 1), rowblk)
    return pl.pallas_call(
        _filter_body,
        grid=(FN // FILT_TR,),
        in_specs=[pl.BlockSpec((FILT_TR, FORD), rowblk), col, col, col,
                  pl.BlockSpec((FORD, FORD), const),
                  pl.BlockSpec((1, FORD), const),
                  pl.BlockSpec((FORD, FORD), const),
                  pl.BlockSpec((1, FORD), const),
                  pl.BlockSpec((FORD, 2 * HY_W), const),
                  pl.BlockSpec((1, FORD), const),
                  pl.BlockSpec((1, HY_W), const)],
        out_specs=[pl.BlockSpec((FILT_TR, HY_W), rowblk),
                   pl.BlockSpec((1, HY_W), const)],
        out_shape=[jax.ShapeDtypeStruct((FN, HY_W), F32), jax.ShapeDtypeStruct((1, HY_W), F32)],
        compiler_params=_cparams(("arbitrary",)),
        name="implicit_filter",
    )(ztab, ttab, mtab, ftab, w1p, b1, w2, b2, w3, freq, negdelta)


DFT_G = 4
DFT_KB = 16
DFT_NP = FN1 // DFT_KB
DFT_UNROLL = 4


def _dot2c(fh, fl, zb):
    return _dot(fh, zb) + _dot(fl, zb)


def _lead_stage(src, fh, fl, dst_ref):
    def group(rg, _):
        r0 = rg * DFT_G
        rhs = jnp.concatenate([src(r0 + g) for g in range(DFT_G)], axis=1)
        blk = _dot2c(fh, fl, rhs)
        for g in range(DFT_G):
            dst_ref[r0 + g] = blk[:, g * LANES:(g + 1) * LANES]
        return 0

    lax.fori_loop(0, FN2 // DFT_G, group, 0, unroll=DFT_UNROLL)


def _lead_phase(src, lh_ref, ll_ref, p_ref, q_ref):
    for h in range(2):
        rows = slice(h * FN1, (h + 1) * FN1)
        _lead_stage(src, lh_ref[rows, :], ll_ref[rows, :], p_ref)
        q_ref[rows] = jnp.swapaxes(p_ref[...], 0, 1)


def _lane_cat(xs):
    return jnp.concatenate(xs, axis=1)


def _mid_forward(q_ref, k0, tr_ref, ti_ref, fh, fl, half):
    brs, bis, trs, tis = [], [], [], []
    for g in range(DFT_G):
        jj = half * DFT_G + g
        ar, ai = q_ref[k0 + jj], q_ref[FN1 + k0 + jj]
        tr, ti = tr_ref[jj], ti_ref[jj]
        brs.append(ar * tr - ai * ti)
        bis.append(ar * ti + ai * tr)
        trs.append(tr)
        tis.append(ti)
    b = jnp.concatenate([_lane_cat(brs), _lane_cat(bis)], axis=0).astype(BF16)
    return _dot2c(fh, fl, b), _lane_cat(trs), _lane_cat(tis)


def _spectrum_body(k_ref, tr_ref, ti_ref, lh_ref, ll_ref, fh_ref, fl_ref, sc_ref, o_ref, p_ref, q_ref):
    ph = pl.program_id(1)

    @pl.when(ph == 0)
    def _():
        _lead_phase(lambda r: k_ref[r].astype(BF16), lh_ref, ll_ref, p_ref, q_ref)

    @pl.when(ph > 0)
    def _():
        k0 = (ph - 1) * DFT_KB
        sc = _lane_cat([sc_ref[...]] * DFT_G)
        for half in range(DFT_KB // DFT_G):
            x, _, _ = _mid_forward(q_ref, k0, tr_ref, ti_ref, fh_ref[...], fl_ref[...], half)
            x = x * sc
            for g in range(DFT_G):
                lanes = slice(g * LANES, (g + 1) * LANES)
                o_ref[0, half * DFT_G + g] = x[:FN2, lanes]
                o_ref[1, half * DFT_G + g] = x[FN2:, lanes]


def _mid_index(ph):
    return jnp.clip(ph - 1, 0, DFT_NP - 1)


def _filter_spectrum(kern_rj, twr, twi, lh, ll, fh, fl, scale):
    const = lambda c, ph: (0, 0)
    tw = pl.BlockSpec((DFT_KB, FN2, LANES), lambda c, ph: (_mid_index(ph), 0, 0))
    return pl.pallas_call(
        _spectrum_body,
        grid=(HY_W // LANES, DFT_NP + 1),
        in_specs=[pl.BlockSpec((FN2, FN1, LANES), lambda c, ph: (0, 0, c), pipeline_mode=pl.Buffered(1)),
                  tw, tw,
                  pl.BlockSpec(lh.shape, const), pl.BlockSpec(ll.shape, const),
                  pl.BlockSpec(fh.shape, const), pl.BlockSpec(fl.shape, const),
                  pl.BlockSpec((1, LANES), lambda c, ph: (0, c))],
        out_specs=pl.BlockSpec((2, DFT_KB, FN2, LANES), lambda c, ph: (0, _mid_index(ph), 0, c)),
        out_shape=jax.ShapeDtypeStruct((2, FN1, FN2, HY_W), F32),
        scratch_shapes=[pltpu.VMEM((FN2, FN1, LANES), F32), pltpu.VMEM((2 * FN1, FN2, LANES), F32)],
        compiler_params=_cparams(("parallel", "arbitrary")),
        name="filter_spectrum",
    )(kern_rj, twr, twi, lh, ll, fh, fl, scale)


def _hconv_body(u_ref, x2_ref, kh_ref, tr_ref, ti_ref, lh_ref, ll_ref, fh_ref, fl_ref, gh_ref, gl_ref,
                oh_ref, ol_ref, bias_ref, o_ref, p_ref, q_ref):
    ph = pl.program_id(1)

    def both(ref, r):
        return jnp.concatenate([ref[0, r], ref[1, r]], axis=0)

    @pl.when(ph == 0)
    def _():
        _lead_phase(lambda r: both(u_ref, r), lh_ref, ll_ref, p_ref, q_ref)

    @pl.when((ph > 0) & (ph <= DFT_NP))
    def _():
        k0 = (ph - 1) * DFT_KB
        for half in range(DFT_KB // DFT_G):
            x, tr, ti = _mid_forward(q_ref, k0, tr_ref, ti_ref, fh_ref[...], fl_ref[...], half)
            xr, xi = x[:FN2], x[FN2:]
            kr = _lane_cat([kh_ref[0, half * DFT_G + g] for g in range(DFT_G)])
            ki = _lane_cat([kh_ref[1, half * DFT_G + g] for g in range(DFT_G)])
            y = jnp.concatenate([xr * kr - xi * ki, xr * ki + xi * kr], axis=0).astype(BF16)
            c = _dot2c(gh_ref[...], gl_ref[...], y)
            cr, ci = c[:FN2], c[FN2:]
            dr = cr * tr + ci * ti
            di = ci * tr - cr * ti
            for g in range(DFT_G):
                lanes = slice(g * LANES, (g + 1) * LANES)
                q_ref[k0 + half * DFT_G + g] = dr[:, lanes]
                q_ref[FN1 + k0 + half * DFT_G + g] = di[:, lanes]

    @pl.when(ph == DFT_NP + 1)
    def _():
        bias = bias_ref[...]
        p_ref[...] = jnp.swapaxes(q_ref[0:FN1], 0, 1)
        _lead_stage(lambda r: p_ref[r].astype(BF16), oh_ref[:, 0:FN1], ol_ref[:, 0:FN1], q_ref)
        p_ref[...] = jnp.swapaxes(q_ref[FN1:2 * FN1], 0, 1)
        oh2, ol2 = oh_ref[:, FN1:2 * FN1], ol_ref[:, FN1:2 * FN1]

        def group(rg, _):
            r0 = rg * DFT_G
            rhs = _lane_cat([p_ref[r0 + g].astype(BF16) for g in range(DFT_G)])
            blk = _dot2c(oh2, ol2, rhs)
            for g in range(DFT_G):
                r = r0 + g
                y = q_ref[r] + blk[:, g * LANES:(g + 1) * LANES]
                q_ref[r] = (y + both(u_ref, r).astype(F32) * bias) * both(x2_ref, r).astype(F32)
            return 0

        lax.fori_loop(0, FN2 // DFT_G, group, 0, unroll=DFT_UNROLL)
        p_ref[...] = jnp.swapaxes(q_ref[0:FN2], 0, 1)
        nj = SEQ // FN2
        for b in range(2):
            o_ref[b] = p_ref[b * nj:(b + 1) * nj].reshape(SEQ, LANES).astype(BF16)


def _hyena_conv(u_rj, x2_rj, khat, twr, twi, lead, fwd, inv, out, bias):
    const = lambda c, ph: (0, 0)
    nj = SEQ // FN2
    tw = pl.BlockSpec((DFT_KB, FN2, LANES), lambda c, ph: (_mid_index(ph), 0, 0))
    sig = pl.BlockSpec((2, FN2, nj, LANES), lambda c, ph: (0, 0, 0, c), pipeline_mode=pl.Buffered(1))
    mats = [m for pair in (lead, fwd, inv, out) for m in pair]
    return pl.pallas_call(
        _hconv_body,
        grid=(HY_W // LANES, DFT_NP + 2),
        in_specs=[sig, sig,
                  pl.BlockSpec((2, DFT_KB, FN2, LANES), lambda c, ph: (0, _mid_index(ph), 0, c)),
                  tw, tw] + [pl.BlockSpec(m.shape, const) for m in mats]
                 + [pl.BlockSpec((1, LANES), lambda c, ph: (0, c))],
        out_specs=pl.BlockSpec((2, SEQ, LANES), lambda c, ph: (0, 0, c)),
        out_shape=jax.ShapeDtypeStruct((2, SEQ, HY_W), BF16),
        scratch_shapes=[pltpu.VMEM((FN2, FN1, LANES), F32), pltpu.VMEM((2 * FN1, FN2, LANES), F32)],
        compiler_params=_cparams(("parallel", "arbitrary")),
        name="hyena_conv",
    )(u_rj, x2_rj, khat, twr, twi, *mats, bias)


OP_TM = 512


def _outproj_body(att_ref, hy_ref, x_ref, w_ref, g1_ref, n2_ref, sh_ref, sc_ref, wrh_ref, wrl_ref,
                  x1_ref, h2_ref, lg_ref):
    a = jnp.concatenate([att_ref[0], hy_ref[0]], axis=1)
    x1 = x_ref[0] + g1_ref[0] * _dot(a, w_ref[...])
    x1_ref[0] = x1
    h2 = _rms_mod(x1, n2_ref[...], sh_ref[0], sc_ref[0])
    hh, hl = _split(h2)
    h2_ref[0] = hh
    wrh = wrh_ref[...]
    lg_ref[0] = _dot(wrh, hh, _NT) + _dot(wrh, hl, _NT) + _dot(wrl_ref[...], hh, _NT)


def _out_projection(att, hy, x, w_out_bf, g1r, n2g, sh2, sc2, wrh, wrl):
    b = x.shape[0]
    tok = lambda bi, i: (bi, i, 0)
    row = lambda bi, i: (bi, 0, 0)
    const = lambda bi, i: (0, 0)
    return pl.pallas_call(
        _outproj_body,
        grid=(b, SEQ // OP_TM),
        in_specs=[pl.BlockSpec((1, OP_TM, ATT_W), tok),
                  pl.BlockSpec((1, OP_TM, HY_W), tok),
                  pl.BlockSpec((1, OP_TM, D), tok),
                  pl.BlockSpec((ATT_W + HY_W, D), const),
                  pl.BlockSpec((1, 1, D), row),
                  pl.BlockSpec((1, D), const),
                  pl.BlockSpec((1, 1, D), row),
                  pl.BlockSpec((1, 1, D), row),
                  pl.BlockSpec((NE, D), const),
                  pl.BlockSpec((NE, D), const)],
        out_specs=[pl.BlockSpec((1, OP_TM, D), tok),
                   pl.BlockSpec((1, OP_TM, D), tok),
                   pl.BlockSpec((1, NE, OP_TM), lambda bi, i: (bi, 0, i))],
        out_shape=[jax.ShapeDtypeStruct((b, SEQ, D), F32),
                   jax.ShapeDtypeStruct((b, SEQ, D), BF16),
                   jax.ShapeDtypeStruct((b, NE, SEQ), F32)],
        compiler_params=_cparams(("parallel", "parallel")),
        name="out_projection",
    )(att, hy, x, w_out_bf, g1r, n2g, sh2, sc2, wrh, wrl)


def _routing_body(lg_ref, tri_ref, pos_ref, gate_ref, off_ref, cs_ref):
    lg = lg_ref[0]
    e = jnp.exp(lg - jnp.max(lg, axis=0, keepdims=True))
    aff = e / jnp.sum(e, axis=0, keepdims=True)
    gate_ref[0] = aff
    def count_ge(t):
        return jnp.sum(jnp.where(aff >= t, 1.0, 0.0), axis=1, keepdims=True)

    def bisect(i, thr):
        cand = thr | (jnp.int32(1) << (30 - i))
        return jnp.where(count_ge(pltpu.bitcast(cand, F32)) >= float(CAP), cand, thr)

    thr = lax.fori_loop(0, 31, bisect, jnp.zeros((NE, 1), I32))
    lo = pltpu.bitcast(thr, F32)
    hi = jnp.maximum(pltpu.bitcast(thr + 1, F32), jnp.finfo(F32).tiny)

    def refine(i, c):
        lo, hi = c
        mid = lo + (hi - lo) * 0.5
        ok = count_ge(mid) >= float(CAP)
        return jnp.where(ok, mid, lo), jnp.where(ok, hi, mid)

    lo, hi = lax.fori_loop(0, 32, refine, (lo, hi))
    gt = aff >= hi
    eq = (aff >= lo) & jnp.logical_not(gt)
    need = float(CAP) - jnp.sum(jnp.where(gt, 1.0, 0.0), axis=1, keepdims=True)
    tri = tri_ref[...]

    def excl_cumsum(mask_f, record_offsets):
        carry = jnp.zeros((NE, 1), F32)
        for c in range(NTCH):
            sl = slice(c * TCH, (c + 1) * TCH)
            m = mask_f[:, sl]
            inc = _dot(m.astype(BF16), tri)
            cs_ref[:, sl] = inc - m + carry
            if record_offsets:
                off_ref[0, :, c:c + 1] = carry.astype(I32)
            carry = carry + inc[:, TCH - 1:TCH]
        return cs_ref[...]

    eq_rank = excl_cumsum(jnp.where(eq, 1.0, 0.0), False)
    sel = gt | (eq & (eq_rank < need))
    pos = excl_cumsum(jnp.where(sel, 1.0, 0.0), True)
    pos_ref[0] = jnp.where(sel, pos.astype(I32), -1)


def _routing(logits, tri):
    b = logits.shape[0]
    blk = pl.BlockSpec((1, NE, SEQ), lambda bi: (bi, 0, 0))
    return pl.pallas_call(
        _routing_body,
        grid=(b,),
        in_specs=[blk, pl.BlockSpec((TCH, TCH), lambda bi: (0, 0))],
        out_specs=[blk, blk, pl.BlockSpec((1, NE, NTCH), lambda bi: (bi, 0, 0))],
        out_shape=[jax.ShapeDtypeStruct((b, NE, SEQ), I32),
                   jax.ShapeDtypeStruct((b, NE, SEQ), F32),
                   jax.ShapeDtypeStruct((b, NE, NTCH), I32)],
        scratch_shapes=[pltpu.VMEM((NE, SEQ), F32)],
        compiler_params=_cparams(("parallel",)),
        name="routing",
    )(logits, tri)


GATHER_UNROLL = 4


def _gather_body(off_ref, h_ref, pos_ref, xg_ref, acc_ref):
    b = pl.program_id(0)
    e = pl.program_id(1)
    acc_ref[...] = jnp.zeros_like(acc_ref)
    crow = lax.broadcasted_iota(I32, (GW, TCH), 0)

    def chunks(i, _):
        for j in range(GATHER_UNROLL):
            c = i * GATHER_UNROLL + j
            off = off_ref[(b * NE + e) * NTCH + c]
            base = pl.multiple_of(jnp.minimum((off >> 3) << 3, CAP - TCH), SUBLANES)
            t0 = pl.multiple_of(c * TCH, TCH)
            rel = pos_ref[0, 0, pl.ds(c, 1), :] - base
            onehot = jnp.where(crow == rel, 1.0, 0.0).astype(BF16)
            acc_ref[pl.ds(base, GW), :] += _dot(onehot, h_ref[0, pl.ds(t0, TCH), :])
        return 0

    lax.fori_loop(0, NTCH // GATHER_UNROLL, chunks, 0)
    xg_ref[0, 0] = acc_ref[0:CAP, :].astype(BF16)


def _gather(offs_flat, h2, pos4):
    b = h2.shape[0]
    grid_spec = pltpu.PrefetchScalarGridSpec(
        num_scalar_prefetch=1,
        grid=(b, NE),
        in_specs=[pl.BlockSpec((1, SEQ, D), lambda bi, e, off: (bi, 0, 0)),
                  pl.BlockSpec((1, 1, NTCH, TCH), lambda bi, e, off: (bi, e, 0, 0))],
        out_specs=pl.BlockSpec((1, 1, CAP, D), lambda bi, e, off: (bi, e, 0, 0)),
        scratch_shapes=[pltpu.VMEM((CAP + SUBLANES, D), F32)],
    )
    return pl.pallas_call(
        _gather_body,
        grid_spec=grid_spec,
        out_shape=jax.ShapeDtypeStruct((b, NE, CAP, D), BF16),
        compiler_params=_cparams(("parallel", "arbitrary")),
        name="moe_gather",
    )(offs_flat, h2, pos4)


FFN_TM = 512
FFN_NF = 4
FFN_FC = DEXP // FFN_NF
assert FFN_FC * FFN_NF == DEXP and FFN_FC % BF16_ROWS == 0


def _ffn_body(xg_ref, wgt_ref, wut_ref, wd_ref, y_ref, acc_ref):
    j = pl.program_id(2)

    @pl.when(j == 0)
    def _():
        acc_ref[...] = jnp.zeros_like(acc_ref)

    wgt = wgt_ref[0].astype(BF16)
    wut = wut_ref[0].astype(BF16)
    wd = wd_ref[0].astype(BF16)
    for mb in range(CAP // FFN_TM):
        rows = slice(mb * FFN_TM, (mb + 1) * FFN_TM)
        xb = xg_ref[0, 0, rows, :]
        a = _dot(xb, wgt, _NT)
        u = _dot(xb, wut, _NT)
        h = (a * (1.0 / (1.0 + jnp.exp(-a))) * u).astype(BF16)
        acc_ref[rows, :] += _dot(h, wd)

    @pl.when(j == FFN_NF - 1)
    def _():
        y_ref[0, 0, 0:CAP, :] = acc_ref[...].astype(BF16)
        y_ref[0, 0, CAP:YROWS, :] = jnp.zeros((YROWS - CAP, D), BF16)


def _expert_ffn(xg, w_gate_t, w_up_t, w_down):
    b = xg.shape[0]
    wblk = pl.BlockSpec((1, FFN_FC, D), lambda e, bi, j: (e, j, 0))
    return pl.pallas_call(
        _ffn_body,
        grid=(NE, b, FFN_NF),
        in_specs=[pl.BlockSpec((1, 1, CAP, D), lambda e, bi, j: (bi, e, 0, 0)), wblk, wblk, wblk],
        out_specs=pl.BlockSpec((1, 1, YROWS, D), lambda e, bi, j: (bi, e, 0, 0)),
        out_shape=jax.ShapeDtypeStruct((b, NE, YROWS, D), BF16),
        scratch_shapes=[pltpu.VMEM((CAP, D), F32)],
        compiler_params=_cparams(("parallel", "parallel", "arbitrary")),
        name="expert_ffn",
    )(xg, w_gate_t, w_up_t, w_down)


def _combine_body(off_ref, y_ref, pos_ref, gate_ref, x1_ref, g2_ref, o_ref):
    b = pl.program_id(0)
    i = pl.program_id(1)
    crow = lax.broadcasted_iota(I32, (CW, TCH), 0)
    acc = jnp.zeros((TCH, D), F32)
    for e in range(NE):
        off = off_ref[(b * NE + e) * NTCH + i]
        base = pl.multiple_of(jnp.minimum((off >> 4) << 4, CAP - TCH), BF16_ROWS)
        rel = pos_ref[0, e:e + 1, :] - base
        w = jnp.where(crow == rel, gate_ref[0, e:e + 1, :], 0.0)
        yw = y_ref[0, e, pl.ds(base, CW), :]
        acc = acc + _dot(w.astype(BF16), yw, _TN)
    o_ref[0] = x1_ref[0] + g2_ref[0] * acc


def _combine(offs_flat, y, pos, gate, x1, g2r):
    b = x1.shape[0]
    grid_spec = pltpu.PrefetchScalarGridSpec(
        num_scalar_prefetch=1,
        grid=(b, NTCH),
        in_specs=[pl.BlockSpec((1, NE, YROWS, D), lambda bi, i, off: (bi, 0, 0, 0),
                               pipeline_mode=pl.Buffered(1)),
                  pl.BlockSpec((1, NE, TCH), lambda bi, i, off: (bi, 0, i)),
                  pl.BlockSpec((1, NE, TCH), lambda bi, i, off: (bi, 0, i)),
                  pl.BlockSpec((1, TCH, D), lambda bi, i, off: (bi, i, 0)),
                  pl.BlockSpec((1, 1, D), lambda bi, i, off: (bi, 0, 0))],
        out_specs=pl.BlockSpec((1, TCH, D), lambda bi, i, off: (bi, i, 0)),
    )
    return pl.pallas_call(
        _combine_body,
        grid_spec=grid_spec,
        out_shape=jax.ShapeDtypeStruct((b, SEQ, D), F32),
        compiler_params=_cparams(("parallel", "arbitrary")),
        name="moe_combine",
    )(offs_flat, y, pos, gate, x1, g2r)


def _np_split(m):
    hi = np.asarray(m, np.float64).astype(BF16)
    lo = (m - hi.astype(np.float64)).astype(BF16)
    return jnp.asarray(hi), jnp.asarray(lo)


@functools.lru_cache(maxsize=None)
def _dft_tables():
    a = np.arange(FN1, dtype=np.float64)
    ang = 2.0 * np.pi * np.outer(a, a) / FN1
    fr, fi = np.cos(ang), -np.sin(ang)
    half = SEQ // FN2
    lead_u = np.block([[fr[:, :half], -fi[:, :half]], [fi[:, :half], fr[:, :half]]])
    lead_k = np.concatenate([fr, fi], axis=0)
    fwd = np.block([[fr, -fi], [fi, fr]])
    inv = np.block([[fr, fi], [-fi, fr]])
    out = np.block([[fr[:half], fi[:half]], [-fi[:half], fr[:half]]])
    n2 = np.arange(FN2, dtype=np.float64)
    tw = 2.0 * np.pi * np.outer(a, n2) / FN
    twr = np.broadcast_to(np.cos(tw)[:, :, None], (FN1, FN2, LANES)).astype(np.float32)
    twi = np.broadcast_to(-np.sin(tw)[:, :, None], (FN1, FN2, LANES)).astype(np.float32)
    return dict(lead_u=lead_u, lead_k=lead_k, fwd=fwd, inv=inv, out=out, twr=twr, twi=twi)


@functools.lru_cache(maxsize=None)
def _filter_tables():
    L = SEQ
    n = np.arange(FN).reshape(FN1, FN2).T.reshape(-1)
    lag = np.where(n < L, n, FN - n)
    jc = np.minimum(lag, L - 1).astype(np.float64)
    t = (jc / (L - 1))[:, None]
    bands = (FEMB - 1) // 2
    w = 2.0 * np.pi * jc / L
    f = np.linspace(1e-4, bands - 1, bands)
    fw = w[:, None] * f[None, :]
    z = np.concatenate([t, np.cos(fw), -np.sin(fw), np.zeros((FN, FORD - FEMB))], axis=-1)
    mask = np.where(n == L, 0.0, 1.0)[:, None]
    fwd = np.where(n < L, 1.0, 0.0)[:, None]
    max_decay = math.log(DECAY_TARGET) / FAST_DECAY_PCT
    min_decay = math.log(DECAY_TARGET) / SLOW_DECAY_PCT
    negdelta = -np.abs(np.linspace(min_decay, max_decay, HY_W))[None, :]
    return tuple(np.asarray(a, np.float32) for a in (z, t, mask, fwd, negdelta))


@functools.lru_cache(maxsize=None)
def _rope_tables(n):
    rows = n // GRID_W
    row_id, col_id = np.meshgrid(np.arange(rows, dtype=np.float64), np.arange(GRID_W, dtype=np.float64), indexing="ij")
    quarter = HD // 4
    inv_freq = ROPE_THETA ** (-np.arange(quarter, dtype=np.float64) / quarter)
    ar = row_id.reshape(-1)[:, None] * inv_freq
    ac = col_id.reshape(-1)[:, None] * inv_freq
    cos = np.concatenate([np.cos(ar), np.cos(ar), np.cos(ac), np.cos(ac)], axis=-1)
    sin = np.concatenate([-np.sin(ar), np.sin(ar), -np.sin(ac), np.sin(ac)], axis=-1)
    reps = (1, LANES // HD)
    return np.tile(cos, reps).astype(np.float32), np.tile(sin, reps).astype(np.float32)


def _hyena_long_conv(u_rj, x2_rj, kern, abs_sum, bias):
    tb = _dft_tables()
    twr, twi = tb["twr"], tb["twi"]
    fwd = _np_split(tb["fwd"])
    scale = 1.0 / (abs_sum * float(FN))
    khat = _filter_spectrum(kern.reshape(FN2, FN1, HY_W), twr, twi, *_np_split(tb["lead_k"]), *fwd, scale)
    return _hyena_conv(u_rj, x2_rj, khat, twr, twi, _np_split(tb["lead_u"]), fwd, _np_split(tb["inv"]),
                       _np_split(tb["out"]), bias.reshape(1, HY_W))


def kernel(x, c, ctx, c_ctx, w_mod, b_mod, norm1_g, norm2_g, w_in, w_out, q_norm_g, k_norm_g,
           conv_w, conv_b, filt_w1, filt_b1, filt_w2, filt_b2, filt_w3, filt_freq, hyena_bias,
           w_router, w_gate, w_up, w_down):
    B = x.shape[0]
    assert x.shape == (B, SEQ, D) and B == 2 and ctx.shape == (B, CTX, D) and w_mod.shape[0] == 1
    l = 0

    cc = jnp.concatenate([c, c_ctx[None, :], jnp.zeros((SUBLANES - B - 1, D), F32)], axis=0)
    mod = _modulation(cc, w_mod[l], b_mod[l][None, :])
    sh1, sc1, g1, sh2, sc2, g2 = [mod[:, i * D:(i + 1) * D] for i in range(6)]
    lat = lambda m: m[:B, None, :]
    ctxrow = lambda m: jnp.broadcast_to(m[B:B + 1, None, :], (B, 1, D))

    w_in_bf = w_in[l].astype(BF16)
    gq2 = jnp.tile(q_norm_g[l][None, :], (1, LANES // HD))
    gk2 = jnp.tile(k_norm_g[l][None, :], (1, LANES // HD))
    bd = jnp.asarray(np.kron(np.eye(LANES // HD), np.full((HD, HD), 1.0 / HD)), BF16)
    cos_t, sin_t = _rope_tables(SEQ)
    n1g = norm1_g[l][None, :]

    q, k, v, p = _in_projection(x, n1g, lat(sh1), lat(sc1), w_in_bf, gq2, gk2, bd, cos_t, sin_t, 512)
    _, kc, vc, _ = _in_projection(ctx, n1g, ctxrow(sh1), ctxrow(sc1), w_in_bf, gq2, gk2, bd,
                                  jnp.ones((CTX, LANES), F32), jnp.zeros((CTX, LANES), F32), CTX)

    k_all = jnp.concatenate([k, kc], axis=1).reshape(B, ATT_NCH, ATT_TK, NKV, HD)
    v_all = jnp.concatenate([v, vc], axis=1).reshape(B, ATT_NCH, ATT_TK, NKV, HD)
    kch = k_all.transpose(0, 3, 1, 2, 4)
    ones_pad = jnp.concatenate([jnp.ones((B, NKV, ATT_NCH, 1, ATT_TK), BF16),
                                jnp.zeros((B, NKV, ATT_NCH, BF16_ROWS - 1, ATT_TK), BF16)], axis=3)
    vtch = jnp.concatenate([v_all.transpose(0, 3, 1, 4, 2), ones_pad], axis=3)
    att = _attention(q, kch, vtch)

    cw9 = conv_w[l].reshape(3, 3, HY_W).reshape(9, HY_W)
    cb3 = conv_b[l].reshape(3, HY_W)
    u_rj, x2_rj = _short_conv(p, cw9, cb3)
    ztab, ttab, mtab, ftab, negdelta = _filter_tables()
    w1p = jnp.concatenate([filt_w1[l], jnp.zeros((FORD - FEMB, FORD), F32)], axis=0)
    kern, abs_sum = _implicit_filter(ztab, ttab, mtab, ftab, w1p, filt_b1[l][None, :], filt_w2[l],
                                     filt_b2[l][None, :], filt_w3[l], filt_freq[l][None, :], negdelta)
    hy = _hyena_long_conv(u_rj, x2_rj, kern, abs_sum, hyena_bias[l])

    wrh, wrl = _split(w_router[l].T)
    x1, h2, logits = _out_projection(att, hy, x, w_out[l].astype(BF16), lat(g1), norm2_g[l][None, :],
                                     lat(sh2), lat(sc2), wrh, wrl)

    tri = jnp.asarray(np.triu(np.ones((TCH, TCH))), BF16)
    pos, gate, offs = _routing(logits, tri)
    offs_flat = offs.reshape(-1)
    xg = _gather(offs_flat, h2, pos.reshape(B, NE, NTCH, TCH))
    y = _expert_ffn(xg, jnp.swapaxes(w_gate[l], 1, 2), jnp.swapaxes(w_up[l], 1, 2), w_down[l])
    return _combine(offs_flat, y, pos, gate, x1, lat(g2))
```

```python
import functools
import math

import numpy as np
import jax
import jax.numpy as jnp
from jax import lax
from jax.experimental import pallas as pl
from jax.experimental.pallas import tpu as pltpu

F32 = jnp.float32
BF16 = jnp.bfloat16
I32 = jnp.int32

D = 1024
SEQ = 8192
CTX = 256
GRID_W = 64
ATT_W = 512
HY_W = 512
HD = 64
NQ = 8
NKV = 2
QPK = NQ // NKV
KV_W = NKV * HD
IN_W = ATT_W + 2 * KV_W + 3 * HY_W
FEMB = 33
FORD = 64
NE = 16
CAP = 2 * SEQ // NE
DEXP = 2752
ROPE_THETA = 10000.0
EPS = 1e-6
DECAY_TARGET = 1e-2
FAST_DECAY_PCT = 0.3
SLOW_DECAY_PCT = 1.5

LANES = 128
SUBLANES = 8
BF16_ROWS = 16
VMEM_BYTES_V7X = 64 * 1024 * 1024
VMEM_LIMIT = VMEM_BYTES_V7X - 8 * 1024 * 1024

FN = 2 * SEQ
FN1 = 128
FN2 = 128

TCH = LANES
NTCH = SEQ // TCH
GW = TCH + SUBLANES
CW = TCH + BF16_ROWS
YROWS = CAP + BF16_ROWS


def _cparams(sem, vmem=None):
    return pltpu.CompilerParams(dimension_semantics=sem, vmem_limit_bytes=vmem or VMEM_LIMIT)


def _split(a):
    hi = a.astype(BF16)
    lo = (a - hi.astype(F32)).astype(BF16)
    return hi, lo


_NN = (((1,), (0,)), ((), ()))
_NT = (((1,), (1,)), ((), ()))
_TN = (((0,), (0,)), ((), ()))


def _dot(a, b, dn=_NN):
    return lax.dot_general(a, b, dn, preferred_element_type=F32)


def _dot3(a, b, dn=_NN):
    ah, al = _split(a)
    bh, bl = _split(b)
    return _dot(ah, bh, dn) + _dot(ah, bl, dn) + _dot(al, bh, dn)


def _mod_body(c_ref, w_ref, b_ref, o_ref):
    c = c_ref[...]
    s = c * (1.0 / (1.0 + jnp.exp(-c)))
    o_ref[...] = _dot3(s, w_ref[...]) + b_ref[...]


def _modulation(cc, w_mod, b_mod):
    n = w_mod.shape[1]
    return pl.pallas_call(
        _mod_body,
        grid=(n // D,),
        in_specs=[pl.BlockSpec((SUBLANES, D), lambda j: (0, 0)),
                  pl.BlockSpec((D, D), lambda j: (0, j)),
                  pl.BlockSpec((1, D), lambda j: (0, j))],
        out_specs=pl.BlockSpec((SUBLANES, D), lambda j: (0, j)),
        out_shape=jax.ShapeDtypeStruct((SUBLANES, n), F32),
        compiler_params=_cparams(("arbitrary",)),
        name="modulation",
    )(cc, w_mod, b_mod)


Q_SCALE = HD ** -0.5 * math.log2(math.e)


def _rms_mod(x, g, sh, sc):
    ms = jnp.mean(x * x, axis=-1, keepdims=True)
    return (x * lax.rsqrt(ms + EPS) * g) * (1.0 + sc) + sh


def _head_norm_rope(t, g, bd, cos, sin):
    sq = t * t
    hi, lo = _split(sq)
    ms = _dot(hi, bd) + _dot(lo, bd)
    tn = t * lax.rsqrt(ms + EPS) * g
    lane = lax.broadcasted_iota(I32, tn.shape, 1)
    sw = jnp.where((lane & 31) < 16, pltpu.roll(tn, LANES - 16, 1), pltpu.roll(tn, 16, 1))
    return tn * cos + sw * sin


def _proj_body(x_ref, g_ref, sh_ref, sc_ref, w_ref, gq_ref, gk_ref, bd_ref, cos_ref, sin_ref,
               q_ref, k_ref, v_ref, p_ref):
    h = _rms_mod(x_ref[0], g_ref[...], sh_ref[0], sc_ref[0])
    proj = _dot(h.astype(BF16), w_ref[...])
    bd = bd_ref[...]
    cos = cos_ref[...]
    sin = sin_ref[...]
    for j in range(ATT_W // LANES):
        sl = slice(j * LANES, (j + 1) * LANES)
        qj = _head_norm_rope(proj[:, sl], gq_ref[...], bd, cos, sin)
        q_ref[0, :, sl] = (qj * Q_SCALE).astype(BF16)
    k_ref[0] = _head_norm_rope(proj[:, ATT_W:ATT_W + KV_W], gk_ref[...], bd, cos, sin).astype(BF16)
    v_ref[0] = proj[:, ATT_W + KV_W:ATT_W + 2 * KV_W].astype(BF16)
    p_ref[0] = proj[:, ATT_W + 2 * KV_W:]


def _in_projection(x, g1, sh, sc, w_in_bf, gq2, gk2, bd, cos_t, sin_t, tm):
    b, s, _ = x.shape
    row = lambda bi, i: (bi, 0, 0)
    tok = lambda bi, i: (bi, i, 0)
    const = lambda bi, i: (0, 0)
    return pl.pallas_call(
        _proj_body,
        grid=(b, s // tm),
        in_specs=[pl.BlockSpec((1, tm, D), tok),
                  pl.BlockSpec((1, D), const),
                  pl.BlockSpec((1, 1, D), row),
                  pl.BlockSpec((1, 1, D), row),
                  pl.BlockSpec((D, IN_W), const),
                  pl.BlockSpec((1, LANES), const),
                  pl.BlockSpec((1, LANES), const),
                  pl.BlockSpec((LANES, LANES), const),
                  pl.BlockSpec((tm, LANES), lambda bi, i: (i, 0)),
                  pl.BlockSpec((tm, LANES), lambda bi, i: (i, 0))],
        out_specs=[pl.BlockSpec((1, tm, ATT_W), tok),
                   pl.BlockSpec((1, tm, KV_W), tok),
                   pl.BlockSpec((1, tm, KV_W), tok),
                   pl.BlockSpec((1, tm, 3 * HY_W), tok)],
        out_shape=[jax.ShapeDtypeStruct((b, s, ATT_W), BF16),
                   jax.ShapeDtypeStruct((b, s, KV_W), BF16),
                   jax.ShapeDtypeStruct((b, s, KV_W), BF16),
                   jax.ShapeDtypeStruct((b, s, 3 * HY_W), F32)],
        compiler_params=_cparams(("parallel", "parallel")),
        name="in_projection",
    )(x, g1, sh, sc, w_in_bf, gq2, gk2, bd, cos_t, sin_t)


ATT_TQ = 256
ATT_TK = 768
SK = SEQ + CTX
ATT_NCH = SK // ATT_TK


ATT_NQ = QPK * ATT_TQ
ATT_VR = HD + BF16_ROWS
assert ATT_NCH % 2 == 1


def _attn_body(q_ref, k_ref, vt_ref, o_ref, s_ref, mx_ref, m_ref, acc_ref):
    qall = jnp.concatenate([q_ref[0, :, r * HD:(r + 1) * HD] for r in range(QPK)], axis=0)
    m_ref[...] = jnp.full(m_ref.shape, -1e30, F32)
    acc_ref[...] = jnp.zeros_like(acc_ref)

    def scores(c, slot):
        s = _dot(k_ref[0, 0, c], qall, _NT)
        s_ref[slot] = s
        mx_ref[slot] = jnp.max(s, axis=0, keepdims=True)

    def update(c, slot):
        m_old = m_ref[...]
        m_new = jnp.maximum(m_old, mx_ref[slot])
        p = jnp.exp2(s_ref[slot] - m_new).astype(BF16)
        acc_ref[...] = jnp.exp2(m_old - m_new) * acc_ref[...] + _dot(vt_ref[0, 0, c], p)
        m_ref[...] = m_new

    scores(0, 0)

    def pair(i, _):
        c = 2 * i
        scores(c + 1, 1)
        update(c, 0)
        scores(c + 2, 0)
        update(c + 1, 1)
        return 0

    lax.fori_loop(0, ATT_NCH // 2, pair, 0)
    update(ATT_NCH - 1, 0)
    out = acc_ref[0:HD, :] * (1.0 / acc_ref[HD:HD + 1, :])
    for r in range(QPK):
        o_ref[0, :, r * HD:(r + 1) * HD] = out[:, r * ATT_TQ:(r + 1) * ATT_TQ].T.astype(BF16)


def _attention(q, kch, vtch):
    b = q.shape[0]
    return pl.pallas_call(
        _attn_body,
        grid=(b, NKV, SEQ // ATT_TQ),
        in_specs=[pl.BlockSpec((1, ATT_TQ, QPK * HD), lambda bi, g, i: (bi, i, g)),
                  pl.BlockSpec((1, 1, ATT_NCH, ATT_TK, HD), lambda bi, g, i: (bi, g, 0, 0, 0)),
                  pl.BlockSpec((1, 1, ATT_NCH, ATT_VR, ATT_TK), lambda bi, g, i: (bi, g, 0, 0, 0))],
        out_specs=pl.BlockSpec((1, ATT_TQ, QPK * HD), lambda bi, g, i: (bi, i, g)),
        out_shape=jax.ShapeDtypeStruct((b, SEQ, ATT_W), BF16),
        scratch_shapes=[pltpu.VMEM((2, ATT_TK, ATT_NQ), F32), pltpu.VMEM((2, 1, ATT_NQ), F32),
                        pltpu.VMEM((1, ATT_NQ), F32), pltpu.VMEM((ATT_VR, ATT_NQ), F32)],
        compiler_params=_cparams(("parallel", "parallel", "parallel")),
        name="attention",
    )(q, kch, vtch)


SC_TM = 2048
SC_J = SC_TM // FN2


def _sconv_body(m1, a1, n1, m2, a2, n2, m3, a3, n3, w_ref, b_ref, u_ref, x2_ref):
    i = pl.program_id(1)
    last = pl.num_programs(1) - 1
    rows = lax.broadcasted_iota(I32, (SC_TM, HY_W), 0)

    def conv(main, prev, nxt, g):
        x = main[0]
        pr = jnp.where(i > 0, prev[0, SUBLANES - 1:SUBLANES, :], 0.0)
        nx = jnp.where(i < last, nxt[0, 0:1, :], 0.0)
        xm = jnp.where(rows == 0, pr, pltpu.roll(x, 1, 0))
        xp = jnp.where(rows == SC_TM - 1, nx, pltpu.roll(x, SC_TM - 1, 0))
        return (w_ref[g:g + 1, :] * xm + w_ref[3 + g:4 + g, :] * x + w_ref[6 + g:7 + g, :] * xp
                + b_ref[g:g + 1, :])

    def to_rj(t):
        return jnp.swapaxes(t.reshape(SC_J, FN2, HY_W), 0, 1).astype(BF16)

    x1 = conv(m1, a1, n1, 0)
    x2 = conv(m2, a2, n2, 1)
    v = conv(m3, a3, n3, 2)
    u_ref[0] = to_rj(v * x1)
    x2_ref[0] = to_rj(x2)


def _short_conv(p, cw9, cb3):
    b = p.shape[0]
    nblk8 = SEQ // SUBLANES
    step8 = SC_TM // SUBLANES
    specs = []
    for g in range(3):
        specs += [pl.BlockSpec((1, SC_TM, HY_W), lambda bi, i, g=g: (bi, i, g)),
                  pl.BlockSpec((1, SUBLANES, HY_W), lambda bi, i, g=g: (bi, jnp.maximum(i * step8 - 1, 0), g)),
                  pl.BlockSpec((1, SUBLANES, HY_W), lambda bi, i, g=g: (bi, jnp.minimum((i + 1) * step8, nblk8 - 1), g))]
    specs += [pl.BlockSpec((9, HY_W), lambda bi, i: (0, 0)), pl.BlockSpec((3, HY_W), lambda bi, i: (0, 0))]
    out = pl.BlockSpec((1, FN2, SC_J, HY_W), lambda bi, i: (bi, 0, i, 0))
    return pl.pallas_call(
        _sconv_body,
        grid=(b, SEQ // SC_TM),
        in_specs=specs,
        out_specs=[out, out],
        out_shape=[jax.ShapeDtypeStruct((b, FN2, SEQ // FN2, HY_W), BF16)] * 2,
        compiler_params=_cparams(("parallel", "parallel")),
        name="short_conv",
    )(p, p, p, p, p, p, p, p, p, cw9, cb3)


FILT_TR = 1024


def _filter_body(z_ref, t_ref, msk_ref, fwd_ref, w1_ref, b1_ref, w2_ref, b2_ref, w3_ref, fr_ref, dl_ref,
                 k_ref, s_ref):
    fr = fr_ref[...]
    h = jnp.sin(fr * (_dot3(z_ref[...], w1_ref[...]) + b1_ref[...]))
    h = jnp.sin(fr * (_dot3(h, w2_ref[...]) + b2_ref[...]))
    h = _dot3(h, w3_ref[...])
    h = jnp.where(fwd_ref[...] > 0.5, h[:, :HY_W], h[:, HY_W:])
    kern = h * jnp.exp(t_ref[...] * dl_ref[...]) * msk_ref[...]
    k_ref[...] = kern

    @pl.when(pl.program_id(0) == 0)
    def _():
        s_ref[...] = jnp.zeros_like(s_ref)

    s_ref[...] += jnp.sum(jnp.abs(kern), axis=0, keepdims=True)


def _implicit_filter(ztab, ttab, mtab, ftab, w1p, b1, w2, b2, w3, freq, negdelta):
    rowblk = lambda i: (i, 0)
    const = lambda i: (0, 0)
    col = pl.BlockSpec((FILT_TR, 1), rowblk)
    return pl.pallas_call(
        _filter_body,
        grid=(FN // FILT_TR,),
        in_specs=[pl.BlockSpec((FILT_TR, FORD), rowblk), col, col, col,
                  pl.BlockSpec((FORD, FORD), const),
                  pl.BlockSpec((1, FORD), const),
                  pl.BlockSpec((FORD, FORD), const),
                  pl.BlockSpec((1, FORD), const),
                  pl.BlockSpec((FORD, 2 * HY_W), const),
                  pl.BlockSpec((1, FORD), const),
                  pl.BlockSpec((1, HY_W), const)],
        out_specs=[pl.BlockSpec((FILT_TR, HY_W), rowblk),
                   pl.BlockSpec((1, HY_W), const)],
        out_shape=[jax.ShapeDtypeStruct((FN, HY_W), F32), jax.ShapeDtypeStruct((1, HY_W), F32)],
        compiler_params=_cparams(("arbitrary",)),
        name="implicit_filter",
    )(ztab, ttab, mtab, ftab, w1p, b1, w2, b2, w3, freq, negdelta)


DFT_G = 4
DFT_KB = 16
DFT_NP = FN1 // DFT_KB
DFT_UNROLL = 4


def _dot2c(fh, fl, zb):
    return _dot(fh, zb) + _dot(fl, zb)


def _lead_stage(src, fh, fl, dst_ref):
    def group(rg, _):
        r0 = rg * DFT_G
        rhs = jnp.concatenate([src(r0 + g) for g in range(DFT_G)], axis=1)
        blk = _dot2c(fh, fl, rhs)
        for g in range(DFT_G):
            dst_ref[r0 + g] = blk[:, g * LANES:(g + 1) * LANES]
        return 0

    lax.fori_loop(0, FN2 // DFT_G, group, 0, unroll=DFT_UNROLL)


def _lead_phase(src, lh_ref, ll_ref, p_ref, q_ref):
    for h in range(2):
        rows = slice(h * FN1, (h + 1) * FN1)
        _lead_stage(src, lh_ref[rows, :], ll_ref[rows, :], p_ref)
        q_ref[rows] = jnp.swapaxes(p_ref[...], 0, 1)


def _lane_cat(xs):
    return jnp.concatenate(xs, axis=1)


def _mid_forward(q_ref, k0, tr_ref, ti_ref, fh, fl, half):
    brs, bis, trs, tis = [], [], [], []
    for g in range(DFT_G):
        jj = half * DFT_G + g
        ar, ai = q_ref[k0 + jj], q_ref[FN1 + k0 + jj]
        tr, ti = tr_ref[jj], ti_ref[jj]
        brs.append(ar * tr - ai * ti)
        bis.append(ar * ti + ai * tr)
        trs.append(tr)
        tis.append(ti)
    b = jnp.concatenate([_lane_cat(brs), _lane_cat(bis)], axis=0).astype(BF16)
    return _dot2c(fh, fl, b), _lane_cat(trs), _lane_cat(tis)


def _spectrum_body(k_ref, tr_ref, ti_ref, lh_ref, ll_ref, fh_ref, fl_ref, sc_ref, o_ref, p_ref, q_ref):
    ph = pl.program_id(1)

    @pl.when(ph == 0)
    def _():
        _lead_phase(lambda r: k_ref[r].astype(BF16), lh_ref, ll_ref, p_ref, q_ref)

    @pl.when(ph > 0)
    def _():
        k0 = (ph - 1) * DFT_KB
        sc = _lane_cat([sc_ref[...]] * DFT_G)
        for half in range(DFT_KB // DFT_G):
            x, _, _ = _mid_forward(q_ref, k0, tr_ref, ti_ref, fh_ref[...], fl_ref[...], half)
            x = x * sc
            for g in range(DFT_G):
                lanes = slice(g * LANES, (g + 1) * LANES)
                o_ref[0, half * DFT_G + g] = x[:FN2, lanes]
                o_ref[1, half * DFT_G + g] = x[FN2:, lanes]


def _mid_index(ph):
    return jnp.clip(ph - 1, 0, DFT_NP - 1)


def _filter_spectrum(kern_rj, twr, twi, lh, ll, fh, fl, scale):
    const = lambda c, ph: (0, 0)
    tw = pl.BlockSpec((DFT_KB, FN2, LANES), lambda c, ph: (_mid_index(ph), 0, 0))
    return pl.pallas_call(
        _spectrum_body,
        grid=(HY_W // LANES, DFT_NP + 1),
        in_specs=[pl.BlockSpec((FN2, FN1, LANES), lambda c, ph: (0, 0, c), pipeline_mode=pl.Buffered(1)),
                  tw, tw,
                  pl.BlockSpec(lh.shape, const), pl.BlockSpec(ll.shape, const),
                  pl.BlockSpec(fh.shape, const), pl.BlockSpec(fl.shape, const),
                  pl.BlockSpec((1, LANES), lambda c, ph: (0, c))],
        out_specs=pl.BlockSpec((2, DFT_KB, FN2, LANES), lambda c, ph: (0, _mid_index(ph), 0, c)),
        out_shape=jax.ShapeDtypeStruct((2, FN1, FN2, HY_W), F32),
        scratch_shapes=[pltpu.VMEM((FN2, FN1, LANES), F32), pltpu.VMEM((2 * FN1, FN2, LANES), F32)],
        compiler_params=_cparams(("parallel", "arbitrary")),
        name="filter_spectrum",
    )(kern_rj, twr, twi, lh, ll, fh, fl, scale)


def _hconv_body(u_ref, x2_ref, kh_ref, tr_ref, ti_ref, lh_ref, ll_ref, fh_ref, fl_ref, gh_ref, gl_ref,
                oh_ref, ol_ref, bias_ref, o_ref, p_ref, q_ref):
    ph = pl.program_id(1)

    def both(ref, r):
        return jnp.concatenate([ref[0, r], ref[1, r]], axis=0)

    @pl.when(ph == 0)
    def _():
        _lead_phase(lambda r: both(u_ref, r), lh_ref, ll_ref, p_ref, q_ref)

    @pl.when((ph > 0) & (ph <= DFT_NP))
    def _():
        k0 = (ph - 1) * DFT_KB
        for half in range(DFT_KB // DFT_G):
            x, tr, ti = _mid_forward(q_ref, k0, tr_ref, ti_ref, fh_ref[...], fl_ref[...], half)
            xr, xi = x[:FN2], x[FN2:]
            kr = _lane_cat([kh_ref[0, half * DFT_G + g] for g in range(DFT_G)])
            ki = _lane_cat([kh_ref[1, half * DFT_G + g] for g in range(DFT_G)])
            y = jnp.concatenate([xr * kr - xi * ki, xr * ki + xi * kr], axis=0).astype(BF16)
            c = _dot2c(gh_ref[...], gl_ref[...], y)
            cr, ci = c[:FN2], c[FN2:]
            dr = cr * tr + ci * ti
            di = ci * tr - cr * ti
            for g in range(DFT_G):
                lanes = slice(g * LANES, (g + 1) * LANES)
                q_ref[k0 + half * DFT_G + g] = dr[:, lanes]
                q_ref[FN1 + k0 + half * DFT_G + g] = di[:, lanes]

    @pl.when(ph == DFT_NP + 1)
    def _():
        bias = bias_ref[...]
        p_ref[...] = jnp.swapaxes(q_ref[0:FN1], 0, 1)
        _lead_stage(lambda r: p_ref[r].astype(BF16), oh_ref[:, 0:FN1], ol_ref[:, 0:FN1], q_ref)
        p_ref[...] = jnp.swapaxes(q_ref[FN1:2 * FN1], 0, 1)
        oh2, ol2 = oh_ref[:, FN1:2 * FN1], ol_ref[:, FN1:2 * FN1]

        def group(rg, _):
            r0 = rg * DFT_G
            rhs = _lane_cat([p_ref[r0 + g].astype(BF16) for g in range(DFT_G)])
            blk = _dot2c(oh2, ol2, rhs)
            for g in range(DFT_G):
                r = r0 + g
                y = q_ref[r] + blk[:, g * LANES:(g + 1) * LANES]
                q_ref[r] = (y + both(u_ref, r).astype(F32) * bias) * both(x2_ref, r).astype(F32)
            return 0

        lax.fori_loop(0, FN2 // DFT_G, group, 0, unroll=DFT_UNROLL)
        p_ref[...] = jnp.swapaxes(q_ref[0:FN2], 0, 1)
        nj = SEQ // FN2
        for b in range(2):
            o_ref[b] = p_ref[b * nj:(b + 1) * nj].reshape(SEQ, LANES).astype(BF16)


def _hyena_conv(u_rj, x2_rj, khat, twr, twi, lead, fwd, inv, out, bias):
    const = lambda c, ph: (0, 0)
    nj = SEQ // FN2
    tw = pl.BlockSpec((DFT_KB, FN2, LANES), lambda c, ph: (_mid_index(ph), 0, 0))
    sig = pl.BlockSpec((2, FN2, nj, LANES), lambda c, ph: (0, 0, 0, c), pipeline_mode=pl.Buffered(1))
    mats = [m for pair in (lead, fwd, inv, out) for m in pair]
    return pl.pallas_call(
        _hconv_body,
        grid=(HY_W // LANES, DFT_NP + 2),
        in_specs=[sig, sig,
                  pl.BlockSpec((2, DFT_KB, FN2, LANES), lambda c, ph: (0, _mid_index(ph), 0, c)),
                  tw, tw] + [pl.BlockSpec(m.shape, const) for m in mats]
                 + [pl.BlockSpec((1, LANES), lambda c, ph: (0, c))],
        out_specs=pl.BlockSpec((2, SEQ, LANES), lambda c, ph: (0, 0, c)),
        out_shape=jax.ShapeDtypeStruct((2, SEQ, HY_W), BF16),
        scratch_shapes=[pltpu.VMEM((FN2, FN1, LANES), F32), pltpu.VMEM((2 * FN1, FN2, LANES), F32)],
        compiler_params=_cparams(("parallel", "arbitrary")),
        name="hyena_conv",
    )(u_rj, x2_rj, khat, twr, twi, *mats, bias)


OP_TM = 512


def _outproj_body(att_ref, hy_ref, x_ref, w_ref, g1_ref, n2_ref, sh_ref, sc_ref, wrh_ref, wrl_ref,
                  x1_ref, h2_ref, lg_ref):
    a = jnp.concatenate([att_ref[0], hy_ref[0]], axis=1)
    x1 = x_ref[0] + g1_ref[0] * _dot(a, w_ref[...])
    x1_ref[0] = x1
    h2 = _rms_mod(x1, n2_ref[...], sh_ref[0], sc_ref[0])
    hh, hl = _split(h2)
    h2_ref[0] = hh
    wrh = wrh_ref[...]
    lg_ref[0] = _dot(wrh, hh, _NT) + _dot(wrh, hl, _NT) + _dot(wrl_ref[...], hh, _NT)


def _out_projection(att, hy, x, w_out_bf, g1r, n2g, sh2, sc2, wrh, wrl):
    b = x.shape[0]
    tok = lambda bi, i: (bi, i, 0)
    row = lambda bi, i: (bi, 0, 0)
    const = lambda bi, i: (0, 0)
    return pl.pallas_call(
        _outproj_body,
        grid=(b, SEQ // OP_TM),
        in_specs=[pl.BlockSpec((1, OP_TM, ATT_W), tok),
                  pl.BlockSpec((1, OP_TM, HY_W), tok),
                  pl.BlockSpec((1, OP_TM, D), tok),
                  pl.BlockSpec((ATT_W + HY_W, D), const),
                  pl.BlockSpec((1, 1, D), row),
                  pl.BlockSpec((1, D), const),
                  pl.BlockSpec((1, 1, D), row),
                  pl.BlockSpec((1, 1, D), row),
                  pl.BlockSpec((NE, D), const),
                  pl.BlockSpec((NE, D), const)],
        out_specs=[pl.BlockSpec((1, OP_TM, D), tok),
                   pl.BlockSpec((1, OP_TM, D), tok),
                   pl.BlockSpec((1, NE, OP_TM), lambda bi, i: (bi, 0, i))],
        out_shape=[jax.ShapeDtypeStruct((b, SEQ, D), F32),
                   jax.ShapeDtypeStruct((b, SEQ, D), BF16),
                   jax.ShapeDtypeStruct((b, NE, SEQ), F32)],
        compiler_params=_cparams(("parallel", "parallel")),
        name="out_projection",
    )(att, hy, x, w_out_bf, g1r, n2g, sh2, sc2, wrh, wrl)


def _routing_body(lg_ref, tri_ref, pos_ref, gate_ref, off_ref, cs_ref):
    lg = lg_ref[0]
    e = jnp.exp(lg - jnp.max(lg, axis=0, keepdims=True))
    aff = e / jnp.sum(e, axis=0, keepdims=True)
    gate_ref[0] = aff
    def count_ge(t):
        return jnp.sum(jnp.where(aff >= t, 1.0, 0.0), axis=1, keepdims=True)

    def bisect(i, thr):
        cand = thr | (jnp.int32(1) << (30 - i))
        return jnp.where(count_ge(pltpu.bitcast(cand, F32)) >= float(CAP), cand, thr)

    thr = lax.fori_loop(0, 31, bisect, jnp.zeros((NE, 1), I32))
    lo = pltpu.bitcast(thr, F32)
    hi = jnp.maximum(pltpu.bitcast(thr + 1, F32), jnp.finfo(F32).tiny)

    def refine(i, c):
        lo, hi = c
        mid = lo + (hi - lo) * 0.5
        ok = count_ge(mid) >= float(CAP)
        return jnp.where(ok, mid, lo), jnp.where(ok, hi, mid)

    lo, hi = lax.fori_loop(0, 32, refine, (lo, hi))
    gt = aff >= hi
    eq = (aff >= lo) & jnp.logical_not(gt)
    need = float(CAP) - jnp.sum(jnp.where(gt, 1.0, 0.0), axis=1, keepdims=True)
    tri = tri_ref[...]

    def excl_cumsum(mask_f, record_offsets):
        carry = jnp.zeros((NE, 1), F32)
        for c in range(NTCH):
            sl = slice(c * TCH, (c + 1) * TCH)
            m = mask_f[:, sl]
            inc = _dot(m.astype(BF16), tri)
            cs_ref[:, sl] = inc - m + carry
            if record_offsets:
                off_ref[0, :, c:c + 1] = carry.astype(I32)
            carry = carry + inc[:, TCH - 1:TCH]
        return cs_ref[...]

    eq_rank = excl_cumsum(jnp.where(eq, 1.0, 0.0), False)
    sel = gt | (eq & (eq_rank < need))
    pos = excl_cumsum(jnp.where(sel, 1.0, 0.0), True)
    pos_ref[0] = jnp.where(sel, pos.astype(I32), -1)


def _routing(logits, tri):
    b = logits.shape[0]
    blk = pl.BlockSpec((1, NE, SEQ), lambda bi: (bi, 0, 0))
    return pl.pallas_call(
        _routing_body,
        grid=(b,),
        in_specs=[blk, pl.BlockSpec((TCH, TCH), lambda bi: (0, 0))],
        out_specs=[blk, blk, pl.BlockSpec((1, NE, NTCH), lambda bi: (bi, 0, 0))],
        out_shape=[jax.ShapeDtypeStruct((b, NE, SEQ), I32),
                   jax.ShapeDtypeStruct((b, NE, SEQ), F32),
                   jax.ShapeDtypeStruct((b, NE, NTCH), I32)],
        scratch_shapes=[pltpu.VMEM((NE, SEQ), F32)],
        compiler_params=_cparams(("parallel",)),
        name="routing",
    )(logits, tri)


GATHER_UNROLL = 8


def _gather_body(off_ref, h_ref, pos_ref, xg_ref, acc_ref):
    b = pl.program_id(0)
    e = pl.program_id(1)
    acc_ref[...] = jnp.zeros_like(acc_ref)
    crow = lax.broadcasted_iota(I32, (GW, TCH), 0)

    def chunks(i, _):
        for j in range(GATHER_UNROLL):
            c = i * GATHER_UNROLL + j
            off = off_ref[(b * NE + e) * NTCH + c]
            base = pl.multiple_of(jnp.minimum((off >> 3) << 3, CAP - TCH), SUBLANES)
            t0 = pl.multiple_of(c * TCH, TCH)
            rel = pos_ref[0, 0, pl.ds(c, 1), :] - base
            onehot = jnp.where(crow == rel, 1.0, 0.0).astype(BF16)
            acc_ref[pl.ds(base, GW), :] += _dot(onehot, h_ref[0, pl.ds(t0, TCH), :])
        return 0

    lax.fori_loop(0, NTCH // GATHER_UNROLL, chunks, 0)
    xg_ref[0, 0] = acc_ref[0:CAP, :].astype(BF16)


def _gather(offs_flat, h2, pos4):
    b = h2.shape[0]
    grid_spec = pltpu.PrefetchScalarGridSpec(
        num_scalar_prefetch=1,
        grid=(b, NE),
        in_specs=[pl.BlockSpec((1, SEQ, D), lambda bi, e, off: (bi, 0, 0)),
                  pl.BlockSpec((1, 1, NTCH, TCH), lambda bi, e, off: (bi, e, 0, 0))],
        out_specs=pl.BlockSpec((1, 1, CAP, D), lambda bi, e, off: (bi, e, 0, 0)),
        scratch_shapes=[pltpu.VMEM((CAP + SUBLANES, D), F32)],
    )
    return pl.pallas_call(
        _gather_body,
        grid_spec=grid_spec,
        out_shape=jax.ShapeDtypeStruct((b, NE, CAP, D), BF16),
        compiler_params=_cparams(("parallel", "arbitrary")),
        name="moe_gather",
    )(offs_flat, h2, pos4)


FFN_TM = 512
FFN_NF = 4
FFN_FC = DEXP // FFN_NF
assert FFN_FC * FFN_NF == DEXP and FFN_FC % BF16_ROWS == 0


def _ffn_body(xg_ref, wgt_ref, wut_ref, wd_ref, y_ref, acc_ref):
    j = pl.program_id(1)
    nb = xg_ref.shape[0]

    @pl.when(j == 0)
    def _():
        acc_ref[...] = jnp.zeros_like(acc_ref)

    wgt = wgt_ref[0].astype(BF16)
    wut = wut_ref[0].astype(BF16)
    wd = wd_ref[0].astype(BF16)
    for b in range(nb):
        for mb in range(CAP // FFN_TM):
            rows = slice(mb * FFN_TM, (mb + 1) * FFN_TM)
            xb = xg_ref[b, 0, rows, :]
            a = _dot(xb, wgt, _NT)
            u = _dot(xb, wut, _NT)
            h = (a * (1.0 / (1.0 + jnp.exp(-a))) * u).astype(BF16)
            acc_ref[b, rows, :] += _dot(h, wd)

    @pl.when(j == FFN_NF - 1)
    def _():
        for b in range(nb):
            y_ref[b, 0, 0:CAP, :] = acc_ref[b].astype(BF16)
            y_ref[b, 0, CAP:YROWS, :] = jnp.zeros((YROWS - CAP, D), BF16)


def _expert_ffn(xg, w_gate_t, w_up_t, w_down):
    b = xg.shape[0]
    wblk = pl.BlockSpec((1, FFN_FC, D), lambda e, j: (e, j, 0))
    return pl.pallas_call(
        _ffn_body,
        grid=(NE, FFN_NF),
        in_specs=[pl.BlockSpec((b, 1, CAP, D), lambda e, j: (0, e, 0, 0)), wblk, wblk, wblk],
        out_specs=pl.BlockSpec((b, 1, YROWS, D), lambda e, j: (0, e, 0, 0)),
        out_shape=jax.ShapeDtypeStruct((b, NE, YROWS, D), BF16),
        scratch_shapes=[pltpu.VMEM((b, CAP, D), F32)],
        compiler_params=_cparams(("parallel", "arbitrary")),
        name="expert_ffn",
    )(xg, w_gate_t, w_up_t, w_down)


def _combine_body(off_ref, y_ref, pos_ref, gate_ref, x1_ref, g2_ref, o_ref):
    b = pl.program_id(0)
    i = pl.program_id(1)
    crow = lax.broadcasted_iota(I32, (CW, TCH), 0)
    acc = jnp.zeros((TCH, D), F32)
    for e in range(NE):
        off = off_ref[(b * NE + e) * NTCH + i]
        base = pl.multiple_of(jnp.minimum((off >> 4) << 4, CAP - TCH), BF16_ROWS)
        rel = pos_ref[0, e:e + 1, :] - base
        w = jnp.where(crow == rel, gate_ref[0, e:e + 1, :], 0.0)
        yw = y_ref[0, e, pl.ds(base, CW), :]
        acc = acc + _dot(w.astype(BF16), yw, _TN)
    o_ref[0] = x1_ref[0] + g2_ref[0] * acc


def _combine(offs_flat, y, pos, gate, x1, g2r):
    b = x1.shape[0]
    grid_spec = pltpu.PrefetchScalarGridSpec(
        num_scalar_prefetch=1,
        grid=(b, NTCH),
        in_specs=[pl.BlockSpec((1, NE, YROWS, D), lambda bi, i, off: (bi, 0, 0, 0),
                               pipeline_mode=pl.Buffered(1)),
                  pl.BlockSpec((1, NE, TCH), lambda bi, i, off: (bi, 0, i)),
                  pl.BlockSpec((1, NE, TCH), lambda bi, i, off: (bi, 0, i)),
                  pl.BlockSpec((1, TCH, D), lambda bi, i, off: (bi, i, 0)),
                  pl.BlockSpec((1, 1, D), lambda bi, i, off: (bi, 0, 0))],
        out_specs=pl.BlockSpec((1, TCH, D), lambda bi, i, off: (bi, i, 0)),
    )
    return pl.pallas_call(
        _combine_body,
        grid_spec=grid_spec,
        out_shape=jax.ShapeDtypeStruct((b, SEQ, D), F32),
        compiler_params=_cparams(("parallel", "arbitrary")),
        name="moe_combine",
    )(offs_flat, y, pos, gate, x1, g2r)


def _np_split(m):
    hi = np.asarray(m, np.float64).astype(BF16)
    lo = (m - hi.astype(np.float64)).astype(BF16)
    return jnp.asarray(hi), jnp.asarray(lo)


@functools.lru_cache(maxsize=None)
def _dft_tables():
    a = np.arange(FN1, dtype=np.float64)
    ang = 2.0 * np.pi * np.outer(a, a) / FN1
    fr, fi = np.cos(ang), -np.sin(ang)
    half = SEQ // FN2
    lead_u = np.block([[fr[:, :half], -fi[:, :half]], [fi[:, :half], fr[:, :half]]])
    lead_k = np.concatenate([fr, fi], axis=0)
    fwd = np.block([[fr, -fi], [fi, fr]])
    inv = np.block([[fr, fi], [-fi, fr]])
    out = np.block([[fr[:half], fi[:half]], [-fi[:half], fr[:half]]])
    n2 = np.arange(FN2, dtype=np.float64)
    tw = 2.0 * np.pi * np.outer(a, n2) / FN
    twr = np.broadcast_to(np.cos(tw)[:, :, None], (FN1, FN2, LANES)).astype(np.float32)
    twi = np.broadcast_to(-np.sin(tw)[:, :, None], (FN1, FN2, LANES)).astype(np.float32)
    return dict(lead_u=lead_u, lead_k=lead_k, fwd=fwd, inv=inv, out=out, twr=twr, twi=twi)


@functools.lru_cache(maxsize=None)
def _filter_tables():
    L = SEQ
    n = np.arange(FN).reshape(FN1, FN2).T.reshape(-1)
    lag = np.where(n < L, n, FN - n)
    jc = np.minimum(lag, L - 1).astype(np.float64)
    t = (jc / (L - 1))[:, None]
    bands = (FEMB - 1) // 2
    w = 2.0 * np.pi * jc / L
    f = np.linspace(1e-4, bands - 1, bands)
    fw = w[:, None] * f[None, :]
    z = np.concatenate([t, np.cos(fw), -np.sin(fw), np.zeros((FN, FORD - FEMB))], axis=-1)
    mask = np.where(n == L, 0.0, 1.0)[:, None]
    fwd = np.where(n < L, 1.0, 0.0)[:, None]
    max_decay = math.log(DECAY_TARGET) / FAST_DECAY_PCT
    min_decay = math.log(DECAY_TARGET) / SLOW_DECAY_PCT
    negdelta = -np.abs(np.linspace(min_decay, max_decay, HY_W))[None, :]
    return tuple(np.asarray(a, np.float32) for a in (z, t, mask, fwd, negdelta))


@functools.lru_cache(maxsize=None)
def _rope_tables(n):
    rows = n // GRID_W
    row_id, col_id = np.meshgrid(np.arange(rows, dtype=np.float64), np.arange(GRID_W, dtype=np.float64), indexing="ij")
    quarter = HD // 4
    inv_freq = ROPE_THETA ** (-np.arange(quarter, dtype=np.float64) / quarter)
    ar = row_id.reshape(-1)[:, None] * inv_freq
    ac = col_id.reshape(-1)[:, None] * inv_freq
    cos = np.concatenate([np.cos(ar), np.cos(ar), np.cos(ac), np.cos(ac)], axis=-1)
    sin = np.concatenate([-np.sin(ar), np.sin(ar), -np.sin(ac), np.sin(ac)], axis=-1)
    reps = (1, LANES // HD)
    return np.tile(cos, reps).astype(np.float32), np.tile(sin, reps).astype(np.float32)


def _hyena_long_conv(u_rj, x2_rj, kern, abs_sum, bias):
    tb = _dft_tables()
    twr, twi = tb["twr"], tb["twi"]
    fwd = _np_split(tb["fwd"])
    scale = 1.0 / (abs_sum * float(FN))
    khat = _filter_spectrum(kern.reshape(FN2, FN1, HY_W), twr, twi, *_np_split(tb["lead_k"]), *fwd, scale)
    return _hyena_conv(u_rj, x2_rj, khat, twr, twi, _np_split(tb["lead_u"]), fwd, _np_split(tb["inv"]),
                       _np_split(tb["out"]), bias.reshape(1, HY_W))


def kernel(x, c, ctx, c_ctx, w_mod, b_mod, norm1_g, norm2_g, w_in, w_out, q_norm_g, k_norm_g,
           conv_w, conv_b, filt_w1, filt_b1, filt_w2, filt_b2, filt_w3, filt_freq, hyena_bias,
           w_router, w_gate, w_up, w_down):
    B = x.shape[0]
    assert x.shape == (B, SEQ, D) and B == 2 and ctx.shape == (B, CTX, D) and w_mod.shape[0] == 1
    l = 0

    cc = jnp.concatenate([c, c_ctx[None, :], jnp.zeros((SUBLANES - B - 1, D), F32)], axis=0)
    mod = _modulation(cc, w_mod[l], b_mod[l][None, :])
    sh1, sc1, g1, sh2, sc2, g2 = [mod[:, i * D:(i + 1) * D] for i in range(6)]
    lat = lambda m: m[:B, None, :]
    ctxrow = lambda m: jnp.broadcast_to(m[B:B + 1, None, :], (B, 1, D))

    w_in_bf = w_in[l].astype(BF16)
    gq2 = jnp.tile(q_norm_g[l][None, :], (1, LANES // HD))
    gk2 = jnp.tile(k_norm_g[l][None, :], (1, LANES // HD))
    bd = jnp.asarray(np.kron(np.eye(LANES // HD), np.full((HD, HD), 1.0 / HD)), BF16)
    cos_t, sin_t = _rope_tables(SEQ)
    n1g = norm1_g[l][None, :]

    q, k, v, p = _in_projection(x, n1g, lat(sh1), lat(sc1), w_in_bf, gq2, gk2, bd, cos_t, sin_t, 512)
    _, kc, vc, _ = _in_projection(ctx, n1g, ctxrow(sh1), ctxrow(sc1), w_in_bf, gq2, gk2, bd,
                                  jnp.ones((CTX, LANES), F32), jnp.zeros((CTX, LANES), F32), CTX)

    k_all = jnp.concatenate([k, kc], axis=1).reshape(B, ATT_NCH, ATT_TK, NKV, HD)
    v_all = jnp.concatenate([v, vc], axis=1).reshape(B, ATT_NCH, ATT_TK, NKV, HD)
    kch = k_all.transpose(0, 3, 1, 2, 4)
    ones_pad = jnp.concatenate([jnp.ones((B, NKV, ATT_NCH, 1, ATT_TK), BF16),
                                jnp.zeros((B, NKV, ATT_NCH, BF16_ROWS - 1, ATT_TK), BF16)], axis=3)
    vtch = jnp.concatenate([v_all.transpose(0, 3, 1, 4, 2), ones_pad], axis=3)
    att = _attention(q, kch, vtch)

    cw9 = conv_w[l].reshape(3, 3, HY_W).reshape(9, HY_W)
    cb3 = conv_b[l].reshape(3, HY_W)
    u_rj, x2_rj = _short_conv(p, cw9, cb3)
    ztab, ttab, mtab, ftab, negdelta = _filter_tables()
    w1p = jnp.concatenate([filt_w1[l], jnp.zeros((FORD - FEMB, FORD), F32)], axis=0)
    kern, abs_sum = _implicit_filter(ztab, ttab, mtab, ftab, w1p, filt_b1[l][None, :], filt_w2[l],
                                     filt_b2[l][None, :], filt_w3[l], filt_freq[l][None, :], negdelta)
    hy = _hyena_long_conv(u_rj, x2_rj, kern, abs_sum, hyena_bias[l])

    wrh, wrl = _split(w_router[l].T)
    x1, h2, logits = _out_projection(att, hy, x, w_out[l].astype(BF16), lat(g1), norm2_g[l][None, :],
                                     lat(sh2), lat(sc2), wrh, wrl)

    tri = jnp.asarray(np.triu(np.ones((TCH, TCH))), BF16)
    pos, gate, offs = _routing(logits, tri)
    offs_flat = offs.reshape(-1)
    xg = _gather(offs_flat, h2, pos.reshape(B, NE, NTCH, TCH))
    y = _expert_ffn(xg, jnp.swapaxes(w_gate[l], 1, 2), jnp.swapaxes(w_up[l], 1, 2), w_down[l])
    return _combine(offs_flat, y, pos, gate, x1, lat(g2))
```

```python
import functools
import math

import numpy as np
import jax
import jax.numpy as jnp
from jax import lax
from jax.experimental import pallas as pl
from jax.experimental.pallas import tpu as pltpu

F32 = jnp.float32
BF16 = jnp.bfloat16
I32 = jnp.int32

D = 1024
SEQ = 8192
CTX = 256
GRID_W = 64
ATT_W = 512
HY_W = 512
HD = 64
NQ = 8
NKV = 2
QPK = NQ // NKV
KV_W = NKV * HD
IN_W = ATT_W + 2 * KV_W + 3 * HY_W
FEMB = 33
FORD = 64
NE = 16
CAP = 2 * SEQ // NE
DEXP = 2752
ROPE_THETA = 10000.0
EPS = 1e-6
DECAY_TARGET = 1e-2
FAST_DECAY_PCT = 0.3
SLOW_DECAY_PCT = 1.5

LANES = 128
SUBLANES = 8
BF16_ROWS = 16
VMEM_BYTES_V7X = 64 * 1024 * 1024
VMEM_LIMIT = VMEM_BYTES_V7X - 8 * 1024 * 1024

FN = 2 * SEQ
FN1 = 128
FN2 = 128

TCH = LANES
NTCH = SEQ // TCH
GW = TCH + SUBLANES
CW = TCH + BF16_ROWS
YROWS = CAP + BF16_ROWS


def _cparams(sem, vmem=None):
    return pltpu.CompilerParams(dimension_semantics=sem, vmem_limit_bytes=vmem or VMEM_LIMIT)


def _split(a):
    hi = a.astype(BF16)
    lo = (a - hi.astype(F32)).astype(BF16)
    return hi, lo


_NN = (((1,), (0,)), ((), ()))
_NT = (((1,), (1,)), ((), ()))
_TN = (((0,), (0,)), ((), ()))


def _dot(a, b, dn=_NN):
    return lax.dot_general(a, b, dn, preferred_element_type=F32)


def _dot3(a, b, dn=_NN):
    ah, al = _split(a)
    bh, bl = _split(b)
    return _dot(ah, bh, dn) + _dot(ah, bl, dn) + _dot(al, bh, dn)


def _mod_body(c_ref, w_ref, b_ref, o_ref):
    c = c_ref[...]
    s = c * (1.0 / (1.0 + jnp.exp(-c)))
    o_ref[...] = _dot3(s, w_ref[...]) + b_ref[...]


def _modulation(cc, w_mod, b_mod):
    n = w_mod.shape[1]
    return pl.pallas_call(
        _mod_body,
        grid=(n // D,),
        in_specs=[pl.BlockSpec((SUBLANES, D), lambda j: (0, 0)),
                  pl.BlockSpec((D, D), lambda j: (0, j)),
                  pl.BlockSpec((1, D), lambda j: (0, j))],
        out_specs=pl.BlockSpec((SUBLANES, D), lambda j: (0, j)),
        out_shape=jax.ShapeDtypeStruct((SUBLANES, n), F32),
        compiler_params=_cparams(("arbitrary",)),
        name="modulation",
    )(cc, w_mod, b_mod)


Q_SCALE = HD ** -0.5 * math.log2(math.e)


def _rms_mod(x, g, sh, sc):
    ms = jnp.mean(x * x, axis=-1, keepdims=True)
    return (x * lax.rsqrt(ms + EPS) * g) * (1.0 + sc) + sh


def _head_mean_square(t, bd):
    hi, lo = _split(t * t)
    return _dot(hi, bd) + _dot(lo, bd)


def _head_norm_rope(t, ms, g, cos, sin):
    tn = t * lax.rsqrt(ms + EPS) * g
    lane = lax.broadcasted_iota(I32, tn.shape, 1)
    sw = jnp.where((lane & 31) < 16, pltpu.roll(tn, LANES - 16, 1), pltpu.roll(tn, 16, 1))
    return tn * cos + sw * sin


def _proj_body(x_ref, g_ref, sh_ref, sc_ref, w_ref, gq_ref, gk_ref, bd_ref, cos_ref, sin_ref,
               q_ref, k_ref, v_ref, p_ref):
    h = _rms_mod(x_ref[0], g_ref[...], sh_ref[0], sc_ref[0])
    proj = _dot(h.astype(BF16), w_ref[...])
    bd = bd_ref[...]
    cos = cos_ref[...]
    sin = sin_ref[...]
    wide = 2 * LANES
    for j in range(ATT_W // wide):
        ms = _head_mean_square(proj[:, j * wide:(j + 1) * wide], bd)
        for i in range(2):
            sl = slice(j * wide + i * LANES, j * wide + (i + 1) * LANES)
            qj = _head_norm_rope(proj[:, sl], ms[:, i * LANES:(i + 1) * LANES], gq_ref[...], cos, sin)
            q_ref[0, :, sl] = (qj * Q_SCALE).astype(BF16)
    ms = _head_mean_square(proj[:, ATT_W:ATT_W + 2 * KV_W], bd)
    kk = _head_norm_rope(proj[:, ATT_W:ATT_W + KV_W], ms[:, 0:KV_W], gk_ref[...], cos, sin)
    vt = proj[:, ATT_W + KV_W:ATT_W + 2 * KV_W].T
    for g in range(NKV):
        k_ref[0, g] = kk[:, g * HD:(g + 1) * HD].astype(BF16)
        v_ref[0, g] = vt[g * HD:(g + 1) * HD, :].astype(BF16)
    p_ref[0] = proj[:, ATT_W + 2 * KV_W:]


def _in_projection(x, g1, sh, sc, w_in_bf, gq2, gk2, bd, cos_t, sin_t, tm):
    b, s, _ = x.shape
    row = lambda bi, i: (bi, 0, 0)
    tok = lambda bi, i: (bi, i, 0)
    const = lambda bi, i: (0, 0)
    return pl.pallas_call(
        _proj_body,
        grid=(b, s // tm),
        in_specs=[pl.BlockSpec((1, tm, D), tok),
                  pl.BlockSpec((1, D), const),
                  pl.BlockSpec((1, 1, D), row),
                  pl.BlockSpec((1, 1, D), row),
                  pl.BlockSpec((D, IN_W), const),
                  pl.BlockSpec((1, LANES), const),
                  pl.BlockSpec((1, LANES), const),
                  pl.BlockSpec((2 * LANES, 2 * LANES), const),
                  pl.BlockSpec((tm, LANES), lambda bi, i: (i, 0)),
                  pl.BlockSpec((tm, LANES), lambda bi, i: (i, 0))],
        out_specs=[pl.BlockSpec((1, tm, ATT_W), tok),
                   pl.BlockSpec((1, NKV, tm, HD), lambda bi, i: (bi, 0, i, 0)),
                   pl.BlockSpec((1, NKV, HD, tm), lambda bi, i: (bi, 0, 0, i)),
                   pl.BlockSpec((1, tm, 3 * HY_W), tok)],
        out_shape=[jax.ShapeDtypeStruct((b, s, ATT_W), BF16),
                   jax.ShapeDtypeStruct((b, NKV, s, HD), BF16),
                   jax.ShapeDtypeStruct((b, NKV, HD, s), BF16),
                   jax.ShapeDtypeStruct((b, s, 3 * HY_W), F32)],
        compiler_params=_cparams(("parallel", "parallel")),
        name="in_projection",
    )(x, g1, sh, sc, w_in_bf, gq2, gk2, bd, cos_t, sin_t)


ATT_TQ = 256
ATT_TK = 768
SK = SEQ + CTX
ATT_NCH = SK // ATT_TK


ATT_NQ = QPK * ATT_TQ
ATT_VR = HD + BF16_ROWS
assert ATT_NCH % 2 == 1


def _attn_body(q_ref, k_ref, vt_ref, o_ref, s_ref, mx_ref, m_ref, acc_ref):
    qall = jnp.concatenate([q_ref[0, :, r * HD:(r + 1) * HD] for r in range(QPK)], axis=0)
    m_ref[...] = jnp.full(m_ref.shape, -1e30, F32)
    acc_ref[...] = jnp.zeros_like(acc_ref)

    def scores(c, slot):
        s = _dot(k_ref[0, 0, c], qall, _NT)
        s_ref[slot] = s
        mx_ref[slot] = jnp.max(s, axis=0, keepdims=True)

    def update(c, slot):
        m_old = m_ref[...]
        m_new = jnp.maximum(m_old, mx_ref[slot])
        p = jnp.exp2(s_ref[slot] - m_new).astype(BF16)
        acc_ref[...] = jnp.exp2(m_old - m_new) * acc_ref[...] + _dot(vt_ref[0, 0, c], p)
        m_ref[...] = m_new

    scores(0, 0)

    def pair(i, _):
        c = 2 * i
        scores(c + 1, 1)
        update(c, 0)
        scores(c + 2, 0)
        update(c + 1, 1)
        return 0

    lax.fori_loop(0, ATT_NCH // 2, pair, 0)
    update(ATT_NCH - 1, 0)
    out = acc_ref[0:HD, :] * (1.0 / acc_ref[HD:HD + 1, :])
    for r in range(QPK):
        o_ref[0, :, r * HD:(r + 1) * HD] = out[:, r * ATT_TQ:(r + 1) * ATT_TQ].T.astype(BF16)


def _attention(q, kch, vtch):
    b = q.shape[0]
    return pl.pallas_call(
        _attn_body,
        grid=(b, NKV, SEQ // ATT_TQ),
        in_specs=[pl.BlockSpec((1, ATT_TQ, QPK * HD), lambda bi, g, i: (bi, i, g)),
                  pl.BlockSpec((1, 1, ATT_NCH, ATT_TK, HD), lambda bi, g, i: (bi, g, 0, 0, 0)),
                  pl.BlockSpec((1, 1, ATT_NCH, ATT_VR, ATT_TK), lambda bi, g, i: (bi, g, 0, 0, 0))],
        out_specs=pl.BlockSpec((1, ATT_TQ, QPK * HD), lambda bi, g, i: (bi, i, g)),
        out_shape=jax.ShapeDtypeStruct((b, SEQ, ATT_W), BF16),
        scratch_shapes=[pltpu.VMEM((2, ATT_TK, ATT_NQ), F32), pltpu.VMEM((2, 1, ATT_NQ), F32),
                        pltpu.VMEM((1, ATT_NQ), F32), pltpu.VMEM((ATT_VR, ATT_NQ), F32)],
        compiler_params=_cparams(("parallel", "parallel", "parallel")),
        name="attention",
    )(q, kch, vtch)


SC_TM = 2048
SC_J = SC_TM // FN2


def _sconv_body(m1, a1, n1, m2, a2, n2, m3, a3, n3, w_ref, b_ref, u_ref, x2_ref):
    i = pl.program_id(1)
    last = pl.num_programs(1) - 1
    rows = lax.broadcasted_iota(I32, (SC_TM, HY_W), 0)

    def conv(main, prev, nxt, g):
        x = main[0]
        pr = jnp.where(i > 0, prev[0, SUBLANES - 1:SUBLANES, :], 0.0)
        nx = jnp.where(i < last, nxt[0, 0:1, :], 0.0)
        xm = jnp.where(rows == 0, pr, pltpu.roll(x, 1, 0))
        xp = jnp.where(rows == SC_TM - 1, nx, pltpu.roll(x, SC_TM - 1, 0))
        return (w_ref[g:g + 1, :] * xm + w_ref[3 + g:4 + g, :] * x + w_ref[6 + g:7 + g, :] * xp
                + b_ref[g:g + 1, :])

    def to_rj(t):
        return jnp.swapaxes(t.reshape(SC_J, FN2, HY_W), 0, 1).astype(BF16)

    x1 = conv(m1, a1, n1, 0)
    x2 = conv(m2, a2, n2, 1)
    v = conv(m3, a3, n3, 2)
    u_ref[0] = to_rj(v * x1)
    x2_ref[0] = to_rj(x2)


def _short_conv(p, cw9, cb3):
    b = p.shape[0]
    nblk8 = SEQ // SUBLANES
    step8 = SC_TM // SUBLANES
    specs = []
    for g in range(3):
        specs += [pl.BlockSpec((1, SC_TM, HY_W), lambda bi, i, g=g: (bi, i, g)),
                  pl.BlockSpec((1, SUBLANES, HY_W), lambda bi, i, g=g: (bi, jnp.maximum(i * step8 - 1, 0), g)),
                  pl.BlockSpec((1, SUBLANES, HY_W), lambda bi, i, g=g: (bi, jnp.minimum((i + 1) * step8, nblk8 - 1), g))]
    specs += [pl.BlockSpec((9, HY_W), lambda bi, i: (0, 0)), pl.BlockSpec((3, HY_W), lambda bi, i: (0, 0))]
    out = pl.BlockSpec((1, FN2, SC_J, HY_W), lambda bi, i: (bi, 0, i, 0))
    return pl.pallas_call(
        _sconv_body,
        grid=(b, SEQ // SC_TM),
        in_specs=specs,
        out_specs=[out, out],
        out_shape=[jax.ShapeDtypeStruct((b, FN2, SEQ // FN2, HY_W), BF16)] * 2,
        compiler_params=_cparams(("parallel", "parallel")),
        name="short_conv",
    )(p, p, p, p, p, p, p, p, p, cw9, cb3)


FILT_TR = 1024


def _filter_body(z_ref, t_ref, msk_ref, fwd_ref, w1_ref, b1_ref, w2_ref, b2_ref, w3_ref, fr_ref, dl_ref,
                 k_ref, s_ref):
    fr = fr_ref[...]
    h = jnp.sin(fr * (_dot3(z_ref[...], w1_ref[...]) + b1_ref[...]))
    h = jnp.sin(fr * (_dot3(h, w2_ref[...]) + b2_ref[...]))
    h = _dot3(h, w3_ref[...])
    h = jnp.where(fwd_ref[...] > 0.5, h[:, :HY_W], h[:, HY_W:])
    kern = h * jnp.exp(t_ref[...] * dl_ref[...]) * msk_ref[...]
    k_ref[...] = kern

    @pl.when(pl.program_id(0) == 0)
    def _():
        s_ref[...] = jnp.zeros_like(s_ref)

    s_ref[...] += jnp.sum(jnp.abs(kern), axis=0, keepdims=True)


def _implicit_filter(ztab, ttab, mtab, ftab, w1p, b1, w2, b2, w3, freq, negdelta):
    rowblk = lambda i: (i, 0)
    const = lambda i: (0, 0)
    col = pl.BlockSpec((FILT_TR, 1), rowblk)
    return pl.pallas_call(
        _filter_body,
        grid=(FN // FILT_TR,),
        in_specs=[pl.BlockSpec((FILT_TR, FORD), rowblk), col, col, col,
                  pl.BlockSpec((FORD, FORD), const),
                  pl.BlockSpec((1, FORD), const),
                  pl.BlockSpec((FORD, FORD), const),
                  pl.BlockSpec((1, FORD), const),
                  pl.BlockSpec((FORD, 2 * HY_W), const),
                  pl.BlockSpec((1, FORD), const),
                  pl.BlockSpec((1, HY_W), const)],
        out_specs=[pl.BlockSpec((FILT_TR, HY_W), rowblk),
                   pl.BlockSpec((1, HY_W), const)],
        out_shape=[jax.ShapeDtypeStruct((FN, HY_W), F32), jax.ShapeDtypeStruct((1, HY_W), F32)],
        compiler_params=_cparams(("arbitrary",)),
        name="implicit_filter",
    )(ztab, ttab, mtab, ftab, w1p, b1, w2, b2, w3, freq, negdelta)


DFT_G = 4
DFT_KB = 16
DFT_NP = FN1 // DFT_KB
DFT_UNROLL = 4


def _dot2c(fh, fl, zb):
    return _dot(fh, zb) + _dot(fl, zb)


def _lead_stage(src, fh, fl, dst_ref):
    def group(rg, _):
        r0 = rg * DFT_G
        rhs = jnp.concatenate([src(r0 + g) for g in range(DFT_G)], axis=1)
        blk = _dot2c(fh, fl, rhs)
        for g in range(DFT_G):
            dst_ref[r0 + g] = blk[:, g * LANES:(g + 1) * LANES]
        return 0

    lax.fori_loop(0, FN2 // DFT_G, group, 0, unroll=DFT_UNROLL)


def _lead_phase(src, lh_ref, ll_ref, p_ref, q_ref):
    for h in range(2):
        rows = slice(h * FN1, (h + 1) * FN1)
        _lead_stage(src, lh_ref[rows, :], ll_ref[rows, :], p_ref)
        q_ref[rows] = jnp.swapaxes(p_ref[...], 0, 1)


def _lane_cat(xs):
    return jnp.concatenate(xs, axis=1)


def _mid_forward(q_ref, k0, tr_ref, ti_ref, fh, fl, half):
    brs, bis, trs, tis = [], [], [], []
    for g in range(DFT_G):
        jj = half * DFT_G + g
        ar, ai = q_ref[k0 + jj], q_ref[FN1 + k0 + jj]
        tr, ti = tr_ref[jj], ti_ref[jj]
        brs.append(ar * tr - ai * ti)
        bis.append(ar * ti + ai * tr)
        trs.append(tr)
        tis.append(ti)
    b = jnp.concatenate([_lane_cat(brs), _lane_cat(bis)], axis=0).astype(BF16)
    return _dot2c(fh, fl, b), _lane_cat(trs), _lane_cat(tis)


def _spectrum_body(k_ref, tr_ref, ti_ref, lh_ref, ll_ref, fh_ref, fl_ref, sc_ref, o_ref, p_ref, q_ref):
    ph = pl.program_id(1)

    @pl.when(ph == 0)
    def _():
        _lead_phase(lambda r: k_ref[r].astype(BF16), lh_ref, ll_ref, p_ref, q_ref)

    @pl.when(ph > 0)
    def _():
        k0 = (ph - 1) * DFT_KB
        sc = _lane_cat([sc_ref[...]] * DFT_G)
        for half in range(DFT_KB // DFT_G):
            x, _, _ = _mid_forward(q_ref, k0, tr_ref, ti_ref, fh_ref[...], fl_ref[...], half)
            x = x * sc
            for g in range(DFT_G):
                lanes = slice(g * LANES, (g + 1) * LANES)
                o_ref[0, half * DFT_G + g] = x[:FN2, lanes]
                o_ref[1, half * DFT_G + g] = x[FN2:, lanes]


def _mid_index(ph):
    return jnp.clip(ph - 1, 0, DFT_NP - 1)


def _filter_spectrum(kern_rj, twr, twi, lh, ll, fh, fl, scale):
    const = lambda c, ph: (0, 0)
    tw = pl.BlockSpec((DFT_KB, FN2, LANES), lambda c, ph: (_mid_index(ph), 0, 0))
    return pl.pallas_call(
        _spectrum_body,
        grid=(HY_W // LANES, DFT_NP + 1),
        in_specs=[pl.BlockSpec((FN2, FN1, LANES), lambda c, ph: (0, 0, c), pipeline_mode=pl.Buffered(1)),
                  tw, tw,
                  pl.BlockSpec(lh.shape, const), pl.BlockSpec(ll.shape, const),
                  pl.BlockSpec(fh.shape, const), pl.BlockSpec(fl.shape, const),
                  pl.BlockSpec((1, LANES), lambda c, ph: (0, c))],
        out_specs=pl.BlockSpec((2, DFT_KB, FN2, LANES), lambda c, ph: (0, _mid_index(ph), 0, c)),
        out_shape=jax.ShapeDtypeStruct((2, FN1, FN2, HY_W), F32),
        scratch_shapes=[pltpu.VMEM((FN2, FN1, LANES), F32), pltpu.VMEM((2 * FN1, FN2, LANES), F32)],
        compiler_params=_cparams(("parallel", "arbitrary")),
        name="filter_spectrum",
    )(kern_rj, twr, twi, lh, ll, fh, fl, scale)


def _hconv_body(u_ref, x2_ref, kh_ref, tr_ref, ti_ref, lh_ref, ll_ref, fh_ref, fl_ref, gh_ref, gl_ref,
                oh_ref, ol_ref, bias_ref, o_ref, p_ref, q_ref):
    ph = pl.program_id(1)

    def both(ref, r):
        return jnp.concatenate([ref[0, r], ref[1, r]], axis=0)

    @pl.when(ph == 0)
    def _():
        _lead_phase(lambda r: both(u_ref, r), lh_ref, ll_ref, p_ref, q_ref)

    @pl.when((ph > 0) & (ph <= DFT_NP))
    def _():
        k0 = (ph - 1) * DFT_KB
        for half in range(DFT_KB // DFT_G):
            x, tr, ti = _mid_forward(q_ref, k0, tr_ref, ti_ref, fh_ref[...], fl_ref[...], half)
            xr, xi = x[:FN2], x[FN2:]
            kr = _lane_cat([kh_ref[0, half * DFT_G + g] for g in range(DFT_G)])
            ki = _lane_cat([kh_ref[1, half * DFT_G + g] for g in range(DFT_G)])
            y = jnp.concatenate([xr * kr - xi * ki, xr * ki + xi * kr], axis=0).astype(BF16)
            c = _dot2c(gh_ref[...], gl_ref[...], y)
            cr, ci = c[:FN2], c[FN2:]
            dr = cr * tr + ci * ti
            di = ci * tr - cr * ti
            for g in range(DFT_G):
                lanes = slice(g * LANES, (g + 1) * LANES)
                q_ref[k0 + half * DFT_G + g] = dr[:, lanes]
                q_ref[FN1 + k0 + half * DFT_G + g] = di[:, lanes]

    @pl.when(ph == DFT_NP + 1)
    def _():
        bias = bias_ref[...]
        p_ref[...] = jnp.swapaxes(q_ref[0:FN1], 0, 1)
        _lead_stage(lambda r: p_ref[r].astype(BF16), oh_ref[:, 0:FN1], ol_ref[:, 0:FN1], q_ref)
        p_ref[...] = jnp.swapaxes(q_ref[FN1:2 * FN1], 0, 1)
        oh2, ol2 = oh_ref[:, FN1:2 * FN1], ol_ref[:, FN1:2 * FN1]

        def group(rg, _):
            r0 = rg * DFT_G
            rhs = _lane_cat([p_ref[r0 + g].astype(BF16) for g in range(DFT_G)])
            blk = _dot2c(oh2, ol2, rhs)
            for g in range(DFT_G):
                r = r0 + g
                y = q_ref[r] + blk[:, g * LANES:(g + 1) * LANES]
                q_ref[r] = (y + both(u_ref, r).astype(F32) * bias) * both(x2_ref, r).astype(F32)
            return 0

        lax.fori_loop(0, FN2 // DFT_G, group, 0, unroll=DFT_UNROLL)
        p_ref[...] = jnp.swapaxes(q_ref[0:FN2], 0, 1)
        nj = SEQ // FN2
        for b in range(2):
            o_ref[b] = p_ref[b * nj:(b + 1) * nj].reshape(SEQ, LANES).astype(BF16)


def _hyena_conv(u_rj, x2_rj, khat, twr, twi, lead, fwd, inv, out, bias):
    const = lambda c, ph: (0, 0)
    nj = SEQ // FN2
    tw = pl.BlockSpec((DFT_KB, FN2, LANES), lambda c, ph: (_mid_index(ph), 0, 0))
    sig = pl.BlockSpec((2, FN2, nj, LANES), lambda c, ph: (0, 0, 0, c), pipeline_mode=pl.Buffered(1))
    mats = [m for pair in (lead, fwd, inv, out) for m in pair]
    return pl.pallas_call(
        _hconv_body,
        grid=(HY_W // LANES, DFT_NP + 2),
        in_specs=[sig, sig,
                  pl.BlockSpec((2, DFT_KB, FN2, LANES), lambda c, ph: (0, _mid_index(ph), 0, c)),
                  tw, tw] + [pl.BlockSpec(m.shape, const) for m in mats]
                 + [pl.BlockSpec((1, LANES), lambda c, ph: (0, c))],
        out_specs=pl.BlockSpec((2, SEQ, LANES), lambda c, ph: (0, 0, c)),
        out_shape=jax.ShapeDtypeStruct((2, SEQ, HY_W), BF16),
        scratch_shapes=[pltpu.VMEM((FN2, FN1, LANES), F32), pltpu.VMEM((2 * FN1, FN2, LANES), F32)],
        compiler_params=_cparams(("parallel", "arbitrary")),
        name="hyena_conv",
    )(u_rj, x2_rj, khat, twr, twi, *mats, bias)


OP_TM = 512


def _outproj_body(att_ref, hy_ref, x_ref, w_ref, g1_ref, n2_ref, sh_ref, sc_ref, wrh_ref, wrl_ref,
                  x1_ref, h2_ref, lg_ref):
    a = jnp.concatenate([att_ref[0], hy_ref[0]], axis=1)
    x1 = x_ref[0] + g1_ref[0] * _dot(a, w_ref[...])
    x1_ref[0] = x1
    h2 = _rms_mod(x1, n2_ref[...], sh_ref[0], sc_ref[0])
    hh, hl = _split(h2)
    h2_ref[0] = hh
    wrh = wrh_ref[...]
    lg_ref[0] = _dot(wrh, hh, _NT) + _dot(wrh, hl, _NT) + _dot(wrl_ref[...], hh, _NT)


def _out_projection(att, hy, x, w_out_bf, g1r, n2g, sh2, sc2, wrh, wrl):
    b = x.shape[0]
    tok = lambda bi, i: (bi, i, 0)
    row = lambda bi, i: (bi, 0, 0)
    const = lambda bi, i: (0, 0)
    return pl.pallas_call(
        _outproj_body,
        grid=(b, SEQ // OP_TM),
        in_specs=[pl.BlockSpec((1, OP_TM, ATT_W), tok),
                  pl.BlockSpec((1, OP_TM, HY_W), tok),
                  pl.BlockSpec((1, OP_TM, D), tok),
                  pl.BlockSpec((ATT_W + HY_W, D), const),
                  pl.BlockSpec((1, 1, D), row),
                  pl.BlockSpec((1, D), const),
                  pl.BlockSpec((1, 1, D), row),
                  pl.BlockSpec((1, 1, D), row),
                  pl.BlockSpec((NE, D), const),
                  pl.BlockSpec((NE, D), const)],
        out_specs=[pl.BlockSpec((1, OP_TM, D), tok),
                   pl.BlockSpec((1, OP_TM, D), tok),
                   pl.BlockSpec((1, NE, OP_TM), lambda bi, i: (bi, 0, i))],
        out_shape=[jax.ShapeDtypeStruct((b, SEQ, D), F32),
                   jax.ShapeDtypeStruct((b, SEQ, D), BF16),
                   jax.ShapeDtypeStruct((b, NE, SEQ), F32)],
        compiler_params=_cparams(("parallel", "parallel")),
        name="out_projection",
    )(att, hy, x, w_out_bf, g1r, n2g, sh2, sc2, wrh, wrl)


def _routing_body(lg_ref, tri_ref, pos_ref, gate_ref, off_ref, cs_ref):
    lg = lg_ref[0]
    e = jnp.exp(lg - jnp.max(lg, axis=0, keepdims=True))
    aff = e / jnp.sum(e, axis=0, keepdims=True)
    gate_ref[0] = aff
    def count_ge(t):
        return jnp.sum(jnp.where(aff >= t, 1.0, 0.0), axis=1, keepdims=True)

    def bisect(i, thr):
        cand = thr | (jnp.int32(1) << (30 - i))
        return jnp.where(count_ge(pltpu.bitcast(cand, F32)) >= float(CAP), cand, thr)

    thr = lax.fori_loop(0, 31, bisect, jnp.zeros((NE, 1), I32))
    lo = pltpu.bitcast(thr, F32)
    hi = jnp.maximum(pltpu.bitcast(thr + 1, F32), jnp.finfo(F32).tiny)

    def refine(i, c):
        lo, hi = c
        mid = lo + (hi - lo) * 0.5
        ok = count_ge(mid) >= float(CAP)
        return jnp.where(ok, mid, lo), jnp.where(ok, hi, mid)

    lo, hi = lax.fori_loop(0, 32, refine, (lo, hi))
    gt = aff >= hi
    eq = (aff >= lo) & jnp.logical_not(gt)
    need = float(CAP) - jnp.sum(jnp.where(gt, 1.0, 0.0), axis=1, keepdims=True)
    tri = tri_ref[...]

    def excl_cumsum(mask_f, record_offsets):
        carry = jnp.zeros((NE, 1), F32)
        for c in range(NTCH):
            sl = slice(c * TCH, (c + 1) * TCH)
            m = mask_f[:, sl]
            inc = _dot(m.astype(BF16), tri)
            cs_ref[:, sl] = inc - m + carry
            if record_offsets:
                off_ref[0, :, c:c + 1] = carry.astype(I32)
            carry = carry + inc[:, TCH - 1:TCH]
        return cs_ref[...]

    eq_rank = excl_cumsum(jnp.where(eq, 1.0, 0.0), False)
    sel = gt | (eq & (eq_rank < need))
    pos = excl_cumsum(jnp.where(sel, 1.0, 0.0), True)
    pos_ref[0] = jnp.where(sel, pos.astype(I32), -1)


def _routing(logits, tri):
    b = logits.shape[0]
    blk = pl.BlockSpec((1, NE, SEQ), lambda bi: (bi, 0, 0))
    return pl.pallas_call(
        _routing_body,
        grid=(b,),
        in_specs=[blk, pl.BlockSpec((TCH, TCH), lambda bi: (0, 0))],
        out_specs=[blk, blk, pl.BlockSpec((1, NE, NTCH), lambda bi: (bi, 0, 0))],
        out_shape=[jax.ShapeDtypeStruct((b, NE, SEQ), I32),
                   jax.ShapeDtypeStruct((b, NE, SEQ), F32),
                   jax.ShapeDtypeStruct((b, NE, NTCH), I32)],
        scratch_shapes=[pltpu.VMEM((NE, SEQ), F32)],
        compiler_params=_cparams(("parallel",)),
        name="routing",
    )(logits, tri)


GATHER_UNROLL = 8


def _gather_body(off_ref, h_ref, pos_ref, xg_ref, acc_ref):
    b = pl.program_id(0)
    e = pl.program_id(1)
    acc_ref[...] = jnp.zeros_like(acc_ref)
    crow = lax.broadcasted_iota(I32, (GW, TCH), 0)

    def chunks(i, _):
        for j in range(GATHER_UNROLL):
            c = i * GATHER_UNROLL + j
            off = off_ref[(b * NE + e) * NTCH + c]
            base = pl.multiple_of(jnp.minimum((off >> 3) << 3, CAP - TCH), SUBLANES)
            t0 = pl.multiple_of(c * TCH, TCH)
            rel = pos_ref[0, 0, pl.ds(c, 1), :] - base
            onehot = jnp.where(crow == rel, 1.0, 0.0).astype(BF16)
            acc_ref[pl.ds(base, GW), :] += _dot(onehot, h_ref[0, pl.ds(t0, TCH), :])
        return 0

    lax.fori_loop(0, NTCH // GATHER_UNROLL, chunks, 0)
    xg_ref[0, 0] = acc_ref[0:CAP, :].astype(BF16)


def _gather(offs_flat, h2, pos4):
    b = h2.shape[0]
    grid_spec = pltpu.PrefetchScalarGridSpec(
        num_scalar_prefetch=1,
        grid=(b, NE),
        in_specs=[pl.BlockSpec((1, SEQ, D), lambda bi, e, off: (bi, 0, 0)),
                  pl.BlockSpec((1, 1, NTCH, TCH), lambda bi, e, off: (bi, e, 0, 0))],
        out_specs=pl.BlockSpec((1, 1, CAP, D), lambda bi, e, off: (bi, e, 0, 0)),
        scratch_shapes=[pltpu.VMEM((CAP + SUBLANES, D), F32)],
    )
    return pl.pallas_call(
        _gather_body,
        grid_spec=grid_spec,
        out_shape=jax.ShapeDtypeStruct((b, NE, CAP, D), BF16),
        compiler_params=_cparams(("parallel", "arbitrary")),
        name="moe_gather",
    )(offs_flat, h2, pos4)


FFN_TM = 512
FFN_NF = 4
FFN_FC = DEXP // FFN_NF
assert FFN_FC * FFN_NF == DEXP and FFN_FC % BF16_ROWS == 0


def _ffn_body(xg_ref, wgt_ref, wut_ref, wd_ref, y_ref, acc_ref):
    j = pl.program_id(1)
    nb = xg_ref.shape[0]

    @pl.when(j == 0)
    def _():
        acc_ref[...] = jnp.zeros_like(acc_ref)

    wgt = wgt_ref[0].astype(BF16)
    wut = wut_ref[0].astype(BF16)
    wd = wd_ref[0].astype(BF16)
    for b in range(nb):
        for mb in range(CAP // FFN_TM):
            rows = slice(mb * FFN_TM, (mb + 1) * FFN_TM)
            xb = xg_ref[b, 0, rows, :]
            a = _dot(xb, wgt, _NT)
            u = _dot(xb, wut, _NT)
            h = (a * (1.0 / (1.0 + jnp.exp(-a))) * u).astype(BF16)
            acc_ref[b, rows, :] += _dot(h, wd)

    @pl.when(j == FFN_NF - 1)
    def _():
        for b in range(nb):
            y_ref[b, 0, 0:CAP, :] = acc_ref[b].astype(BF16)
            y_ref[b, 0, CAP:YROWS, :] = jnp.zeros((YROWS - CAP, D), BF16)


def _expert_ffn(xg, w_gate_t, w_up_t, w_down):
    b = xg.shape[0]
    wblk = pl.BlockSpec((1, FFN_FC, D), lambda e, j: (e, j, 0))
    return pl.pallas_call(
        _ffn_body,
        grid=(NE, FFN_NF),
        in_specs=[pl.BlockSpec((b, 1, CAP, D), lambda e, j: (0, e, 0, 0)), wblk, wblk, wblk],
        out_specs=pl.BlockSpec((b, 1, YROWS, D), lambda e, j: (0, e, 0, 0)),
        out_shape=jax.ShapeDtypeStruct((b, NE, YROWS, D), BF16),
        scratch_shapes=[pltpu.VMEM((b, CAP, D), F32)],
        compiler_params=_cparams(("parallel", "arbitrary")),
        name="expert_ffn",
    )(xg, w_gate_t, w_up_t, w_down)


def _combine_body(off_ref, y_ref, pos_ref, gate_ref, x1_ref, g2_ref, o_ref):
    b = pl.program_id(0)
    i = pl.program_id(1)
    crow = lax.broadcasted_iota(I32, (CW, TCH), 0)
    acc = jnp.zeros((TCH, D), F32)
    for e in range(NE):
        off = off_ref[(b * NE + e) * NTCH + i]
        base = pl.multiple_of(jnp.minimum((off >> 4) << 4, CAP - TCH), BF16_ROWS)
        rel = pos_ref[0, e:e + 1, :] - base
        w = jnp.where(crow == rel, gate_ref[0, e:e + 1, :], 0.0)
        yw = y_ref[0, e, pl.ds(base, CW), :]
        acc = acc + _dot(w.astype(BF16), yw, _TN)
    o_ref[0] = x1_ref[0] + g2_ref[0] * acc


def _combine(offs_flat, y, pos, gate, x1, g2r):
    b = x1.shape[0]
    grid_spec = pltpu.PrefetchScalarGridSpec(
        num_scalar_prefetch=1,
        grid=(b, NTCH),
        in_specs=[pl.BlockSpec((1, NE, YROWS, D), lambda bi, i, off: (bi, 0, 0, 0),
                               pipeline_mode=pl.Buffered(1)),
                  pl.BlockSpec((1, NE, TCH), lambda bi, i, off: (bi, 0, i)),
                  pl.BlockSpec((1, NE, TCH), lambda bi, i, off: (bi, 0, i)),
                  pl.BlockSpec((1, TCH, D), lambda bi, i, off: (bi, i, 0)),
                  pl.BlockSpec((1, 1, D), lambda bi, i, off: (bi, 0, 0))],
        out_specs=pl.BlockSpec((1, TCH, D), lambda bi, i, off: (bi, i, 0)),
    )
    return pl.pallas_call(
        _combine_body,
        grid_spec=grid_spec,
        out_shape=jax.ShapeDtypeStruct((b, SEQ, D), F32),
        compiler_params=_cparams(("parallel", "arbitrary")),
        name="moe_combine",
    )(offs_flat, y, pos, gate, x1, g2r)


def _np_split(m):
    hi = np.asarray(m, np.float64).astype(BF16)
    lo = (m - hi.astype(np.float64)).astype(BF16)
    return jnp.asarray(hi), jnp.asarray(lo)


@functools.lru_cache(maxsize=None)
def _dft_tables():
    a = np.arange(FN1, dtype=np.float64)
    ang = 2.0 * np.pi * np.outer(a, a) / FN1
    fr, fi = np.cos(ang), -np.sin(ang)
    half = SEQ // FN2
    lead_u = np.block([[fr[:, :half], -fi[:, :half]], [fi[:, :half], fr[:, :half]]])
    lead_k = np.concatenate([fr, fi], axis=0)
    fwd = np.block([[fr, -fi], [fi, fr]])
    inv = np.block([[fr, fi], [-fi, fr]])
    out = np.block([[fr[:half], fi[:half]], [-fi[:half], fr[:half]]])
    n2 = np.arange(FN2, dtype=np.float64)
    tw = 2.0 * np.pi * np.outer(a, n2) / FN
    twr = np.broadcast_to(np.cos(tw)[:, :, None], (FN1, FN2, LANES)).astype(np.float32)
    twi = np.broadcast_to(-np.sin(tw)[:, :, None], (FN1, FN2, LANES)).astype(np.float32)
    return dict(lead_u=lead_u, lead_k=lead_k, fwd=fwd, inv=inv, out=out, twr=twr, twi=twi)


@functools.lru_cache(maxsize=None)
def _filter_tables():
    L = SEQ
    n = np.arange(FN).reshape(FN1, FN2).T.reshape(-1)
    lag = np.where(n < L, n, FN - n)
    jc = np.minimum(lag, L - 1).astype(np.float64)
    t = (jc / (L - 1))[:, None]
    bands = (FEMB - 1) // 2
    w = 2.0 * np.pi * jc / L
    f = np.linspace(1e-4, bands - 1, bands)
    fw = w[:, None] * f[None, :]
    z = np.concatenate([t, np.cos(fw), -np.sin(fw), np.zeros((FN, FORD - FEMB))], axis=-1)
    mask = np.where(n == L, 0.0, 1.0)[:, None]
    fwd = np.where(n < L, 1.0, 0.0)[:, None]
    max_decay = math.log(DECAY_TARGET) / FAST_DECAY_PCT
    min_decay = math.log(DECAY_TARGET) / SLOW_DECAY_PCT
    negdelta = -np.abs(np.linspace(min_decay, max_decay, HY_W))[None, :]
    return tuple(np.asarray(a, np.float32) for a in (z, t, mask, fwd, negdelta))


@functools.lru_cache(maxsize=None)
def _rope_tables(n):
    rows = n // GRID_W
    row_id, col_id = np.meshgrid(np.arange(rows, dtype=np.float64), np.arange(GRID_W, dtype=np.float64), indexing="ij")
    quarter = HD // 4
    inv_freq = ROPE_THETA ** (-np.arange(quarter, dtype=np.float64) / quarter)
    ar = row_id.reshape(-1)[:, None] * inv_freq
    ac = col_id.reshape(-1)[:, None] * inv_freq
    cos = np.concatenate([np.cos(ar), np.cos(ar), np.cos(ac), np.cos(ac)], axis=-1)
    sin = np.concatenate([-np.sin(ar), np.sin(ar), -np.sin(ac), np.sin(ac)], axis=-1)
    reps = (1, LANES // HD)
    return np.tile(cos, reps).astype(np.float32), np.tile(sin, reps).astype(np.float32)


def _hyena_long_conv(u_rj, x2_rj, kern, abs_sum, bias):
    tb = _dft_tables()
    twr, twi = tb["twr"], tb["twi"]
    fwd = _np_split(tb["fwd"])
    scale = 1.0 / (abs_sum * float(FN))
    khat = _filter_spectrum(kern.reshape(FN2, FN1, HY_W), twr, twi, *_np_split(tb["lead_k"]), *fwd, scale)
    return _hyena_conv(u_rj, x2_rj, khat, twr, twi, _np_split(tb["lead_u"]), fwd, _np_split(tb["inv"]),
                       _np_split(tb["out"]), bias.reshape(1, HY_W))


def kernel(x, c, ctx, c_ctx, w_mod, b_mod, norm1_g, norm2_g, w_in, w_out, q_norm_g, k_norm_g,
           conv_w, conv_b, filt_w1, filt_b1, filt_w2, filt_b2, filt_w3, filt_freq, hyena_bias,
           w_router, w_gate, w_up, w_down):
    B = x.shape[0]
    assert x.shape == (B, SEQ, D) and B == 2 and ctx.shape == (B, CTX, D) and w_mod.shape[0] == 1
    l = 0

    cc = jnp.concatenate([c, c_ctx[None, :], jnp.zeros((SUBLANES - B - 1, D), F32)], axis=0)
    mod = _modulation(cc, w_mod[l], b_mod[l][None, :])
    sh1, sc1, g1, sh2, sc2, g2 = [mod[:, i * D:(i + 1) * D] for i in range(6)]
    lat = lambda m: m[:B, None, :]
    ctxrow = lambda m: jnp.broadcast_to(m[B:B + 1, None, :], (B, 1, D))

    w_in_bf = w_in[l].astype(BF16)
    gq2 = jnp.tile(q_norm_g[l][None, :], (1, LANES // HD))
    gk2 = jnp.tile(k_norm_g[l][None, :], (1, LANES // HD))
    bd = jnp.asarray(np.kron(np.eye(2 * LANES // HD), np.full((HD, HD), 1.0 / HD)), BF16)
    cos_t, sin_t = _rope_tables(SEQ)
    n1g = norm1_g[l][None, :]

    q, k, vt, p = _in_projection(x, n1g, lat(sh1), lat(sc1), w_in_bf, gq2, gk2, bd, cos_t, sin_t, 512)
    _, kc, vct, _ = _in_projection(ctx, n1g, ctxrow(sh1), ctxrow(sc1), w_in_bf, gq2, gk2, bd,
                                   jnp.ones((CTX, LANES), F32), jnp.zeros((CTX, LANES), F32), CTX)

    kch = jnp.concatenate([k, kc], axis=2).reshape(B, NKV, ATT_NCH, ATT_TK, HD)
    vt_all = jnp.concatenate([vt, vct], axis=3).reshape(B, NKV, HD, ATT_NCH, ATT_TK)
    ones_pad = jnp.concatenate([jnp.ones((B, NKV, ATT_NCH, 1, ATT_TK), BF16),
                                jnp.zeros((B, NKV, ATT_NCH, BF16_ROWS - 1, ATT_TK), BF16)], axis=3)
    vtch = jnp.concatenate([vt_all.transpose(0, 1, 3, 2, 4), ones_pad], axis=3)
    att = _attention(q, kch, vtch)

    cw9 = conv_w[l].reshape(3, 3, HY_W).reshape(9, HY_W)
    cb3 = conv_b[l].reshape(3, HY_W)
    u_rj, x2_rj = _short_conv(p, cw9, cb3)
    ztab, ttab, mtab, ftab, negdelta = _filter_tables()
    w1p = jnp.concatenate([filt_w1[l], jnp.zeros((FORD - FEMB, FORD), F32)], axis=0)
    kern, abs_sum = _implicit_filter(ztab, ttab, mtab, ftab, w1p, filt_b1[l][None, :], filt_w2[l],
                                     filt_b2[l][None, :], filt_w3[l], filt_freq[l][None, :], negdelta)
    hy = _hyena_long_conv(u_rj, x2_rj, kern, abs_sum, hyena_bias[l])

    wrh, wrl = _split(w_router[l].T)
    x1, h2, logits = _out_projection(att, hy, x, w_out[l].astype(BF16), lat(g1), norm2_g[l][None, :],
                                     lat(sh2), lat(sc2), wrh, wrl)

    tri = jnp.asarray(np.triu(np.ones((TCH, TCH))), BF16)
    pos, gate, offs = _routing(logits, tri)
    offs_flat = offs.reshape(-1)
    xg = _gather(offs_flat, h2, pos.reshape(B, NE, NTCH, TCH))
    y = _expert_ffn(xg, jnp.swapaxes(w_gate[l], 1, 2), jnp.swapaxes(w_up[l], 1, 2), w_down[l])
    return _combine(offs_flat, y, pos, gate, x1, lat(g2))
```

```python
import functools
import math

import numpy as np
import jax
import jax.numpy as jnp
from jax import lax
from jax.experimental import pallas as pl
from jax.experimental.pallas import tpu as pltpu

F32 = jnp.float32
BF16 = jnp.bfloat16
I32 = jnp.int32

D = 1024
SEQ = 8192
CTX = 256
GRID_W = 64
ATT_W = 512
HY_W = 512
HD = 64
NQ = 8
NKV = 2
QPK = NQ // NKV
KV_W = NKV * HD
IN_W = ATT_W + 2 * KV_W + 3 * HY_W
FEMB = 33
FORD = 64
NE = 16
CAP = 2 * SEQ // NE
DEXP = 2752
ROPE_THETA = 10000.0
EPS = 1e-6
DECAY_TARGET = 1e-2
FAST_DECAY_PCT = 0.3
SLOW_DECAY_PCT = 1.5

LANES = 128
SUBLANES = 8
BF16_ROWS = 16
VMEM_BYTES_V7X = 64 * 1024 * 1024
VMEM_LIMIT = VMEM_BYTES_V7X - 8 * 1024 * 1024

FN = 2 * SEQ
FN1 = 128
FN2 = 128

TCH = LANES
NTCH = SEQ // TCH
GW = TCH + SUBLANES
CW = TCH + BF16_ROWS
YROWS = CAP + BF16_ROWS


def _cparams(sem, vmem=None):
    return pltpu.CompilerParams(dimension_semantics=sem, vmem_limit_bytes=vmem or VMEM_LIMIT)


def _split(a):
    hi = a.astype(BF16)
    lo = (a - hi.astype(F32)).astype(BF16)
    return hi, lo


_NN = (((1,), (0,)), ((), ()))
_NT = (((1,), (1,)), ((), ()))
_TN = (((0,), (0,)), ((), ()))


def _dot(a, b, dn=_NN):
    return lax.dot_general(a, b, dn, preferred_element_type=F32)


def _dot3(a, b, dn=_NN):
    ah, al = _split(a)
    bh, bl = _split(b)
    return _dot(ah, bh, dn) + _dot(ah, bl, dn) + _dot(al, bh, dn)


def _mod_body(c_ref, w_ref, b_ref, o_ref):
    c = c_ref[...]
    s = c * (1.0 / (1.0 + jnp.exp(-c)))
    o_ref[...] = _dot3(s, w_ref[...]) + b_ref[...]


def _modulation(cc, w_mod, b_mod):
    n = w_mod.shape[1]
    return pl.pallas_call(
        _mod_body,
        grid=(n // D,),
        in_specs=[pl.BlockSpec((SUBLANES, D), lambda j: (0, 0)),
                  pl.BlockSpec((D, D), lambda j: (0, j)),
                  pl.BlockSpec((1, D), lambda j: (0, j))],
        out_specs=pl.BlockSpec((SUBLANES, D), lambda j: (0, j)),
        out_shape=jax.ShapeDtypeStruct((SUBLANES, n), F32),
        compiler_params=_cparams(("arbitrary",)),
        name="modulation",
    )(cc, w_mod, b_mod)


Q_SCALE = HD ** -0.5 * math.log2(math.e)


def _rms_mod(x, g, sh, sc):
    ms = jnp.mean(x * x, axis=-1, keepdims=True)
    return (x * lax.rsqrt(ms + EPS) * g) * (1.0 + sc) + sh


def _head_mean_square(t, bd):
    hi, lo = _split(t * t)
    return _dot(hi, bd) + _dot(lo, bd)


def _head_norm_rope(t, ms, g, cos, sin):
    tn = t * lax.rsqrt(ms + EPS) * g
    lane = lax.broadcasted_iota(I32, tn.shape, 1)
    sw = jnp.where((lane & 31) < 16, pltpu.roll(tn, LANES - 16, 1), pltpu.roll(tn, 16, 1))
    return tn * cos + sw * sin


def _proj_body(x_ref, g_ref, sh_ref, sc_ref, w_ref, gq_ref, gk_ref, bd_ref, cos_ref, sin_ref,
               q_ref, k_ref, v_ref, p_ref):
    h = _rms_mod(x_ref[0], g_ref[...], sh_ref[0], sc_ref[0])
    proj = _dot(h.astype(BF16), w_ref[...])
    bd = bd_ref[...]
    cos = cos_ref[...]
    sin = sin_ref[...]
    wide = 2 * LANES
    for j in range(ATT_W // wide):
        ms = _head_mean_square(proj[:, j * wide:(j + 1) * wide], bd)
        for i in range(2):
            sl = slice(j * wide + i * LANES, j * wide + (i + 1) * LANES)
            qj = _head_norm_rope(proj[:, sl], ms[:, i * LANES:(i + 1) * LANES], gq_ref[...], cos, sin)
            q_ref[0, :, sl] = (qj * Q_SCALE).astype(BF16)
    ms = _head_mean_square(proj[:, ATT_W:ATT_W + 2 * KV_W], bd)
    kk = _head_norm_rope(proj[:, ATT_W:ATT_W + KV_W], ms[:, 0:KV_W], gk_ref[...], cos, sin)
    vt = proj[:, ATT_W + KV_W:ATT_W + 2 * KV_W].T
    for g in range(NKV):
        k_ref[0, g] = kk[:, g * HD:(g + 1) * HD].astype(BF16)
        v_ref[0, g] = vt[g * HD:(g + 1) * HD, :].astype(BF16)
    p_ref[0] = proj[:, ATT_W + 2 * KV_W:]


def _in_projection(x, g1, sh, sc, w_in_bf, gq2, gk2, bd, cos_t, sin_t, tm):
    b, s, _ = x.shape
    row = lambda bi, i: (bi, 0, 0)
    tok = lambda bi, i: (bi, i, 0)
    const = lambda bi, i: (0, 0)
    return pl.pallas_call(
        _proj_body,
        grid=(b, s // tm),
        in_specs=[pl.BlockSpec((1, tm, D), tok),
                  pl.BlockSpec((1, D), const),
                  pl.BlockSpec((1, 1, D), row),
                  pl.BlockSpec((1, 1, D), row),
                  pl.BlockSpec((D, IN_W), const),
                  pl.BlockSpec((1, LANES), const),
                  pl.BlockSpec((1, LANES), const),
                  pl.BlockSpec((2 * LANES, 2 * LANES), const),
                  pl.BlockSpec((tm, LANES), lambda bi, i: (i, 0)),
                  pl.BlockSpec((tm, LANES), lambda bi, i: (i, 0))],
        out_specs=[pl.BlockSpec((1, tm, ATT_W), tok),
                   pl.BlockSpec((1, NKV, tm, HD), lambda bi, i: (bi, 0, i, 0)),
                   pl.BlockSpec((1, NKV, HD, tm), lambda bi, i: (bi, 0, 0, i)),
                   pl.BlockSpec((1, tm, 3 * HY_W), tok)],
        out_shape=[jax.ShapeDtypeStruct((b, s, ATT_W), BF16),
                   jax.ShapeDtypeStruct((b, NKV, s, HD), BF16),
                   jax.ShapeDtypeStruct((b, NKV, HD, s), BF16),
                   jax.ShapeDtypeStruct((b, s, 3 * HY_W), F32)],
        compiler_params=_cparams(("parallel", "parallel")),
        name="in_projection",
    )(x, g1, sh, sc, w_in_bf, gq2, gk2, bd, cos_t, sin_t)


ATT_TQ = 512
ATT_TK = 768
SK = SEQ + CTX
ATT_NCH = SK // ATT_TK


ATT_NQ = QPK * ATT_TQ
ATT_VR = HD + BF16_ROWS
assert ATT_NCH % 2 == 1


def _attn_body(q_ref, k_ref, vt_ref, o_ref, s_ref, mx_ref, m_ref, acc_ref):
    qall = jnp.concatenate([q_ref[0, :, r * HD:(r + 1) * HD] for r in range(QPK)], axis=0)
    m_ref[...] = jnp.full(m_ref.shape, -1e30, F32)
    acc_ref[...] = jnp.zeros_like(acc_ref)

    def scores(c, slot):
        s = _dot(k_ref[0, 0, c], qall, _NT)
        s_ref[slot] = s
        mx_ref[slot] = jnp.max(s, axis=0, keepdims=True)

    def update(c, slot):
        m_old = m_ref[...]
        m_new = jnp.maximum(m_old, mx_ref[slot])
        p = jnp.exp2(s_ref[slot] - m_new).astype(BF16)
        acc_ref[...] = jnp.exp2(m_old - m_new) * acc_ref[...] + _dot(vt_ref[0, 0, c], p)
        m_ref[...] = m_new

    scores(0, 0)

    def pair(i, _):
        c = 2 * i
        scores(c + 1, 1)
        update(c, 0)
        scores(c + 2, 0)
        update(c + 1, 1)
        return 0

    lax.fori_loop(0, ATT_NCH // 2, pair, 0)
    update(ATT_NCH - 1, 0)
    out = acc_ref[0:HD, :] * (1.0 / acc_ref[HD:HD + 1, :])
    for r in range(QPK):
        o_ref[0, :, r * HD:(r + 1) * HD] = out[:, r * ATT_TQ:(r + 1) * ATT_TQ].T.astype(BF16)


def _attention(q, kch, vtch):
    b = q.shape[0]
    return pl.pallas_call(
        _attn_body,
        grid=(b, NKV, SEQ // ATT_TQ),
        in_specs=[pl.BlockSpec((1, ATT_TQ, QPK * HD), lambda bi, g, i: (bi, i, g)),
                  pl.BlockSpec((1, 1, ATT_NCH, ATT_TK, HD), lambda bi, g, i: (bi, g, 0, 0, 0)),
                  pl.BlockSpec((1, 1, ATT_NCH, ATT_VR, ATT_TK), lambda bi, g, i: (bi, g, 0, 0, 0))],
        out_specs=pl.BlockSpec((1, ATT_TQ, QPK * HD), lambda bi, g, i: (bi, i, g)),
        out_shape=jax.ShapeDtypeStruct((b, SEQ, ATT_W), BF16),
        scratch_shapes=[pltpu.VMEM((2, ATT_TK, ATT_NQ), F32), pltpu.VMEM((2, 1, ATT_NQ), F32),
                        pltpu.VMEM((1, ATT_NQ), F32), pltpu.VMEM((ATT_VR, ATT_NQ), F32)],
        compiler_params=_cparams(("parallel", "parallel", "parallel")),
        name="attention",
    )(q, kch, vtch)


SC_TM = 2048
SC_J = SC_TM // FN2


def _sconv_body(m1, a1, n1, m2, a2, n2, m3, a3, n3, w_ref, b_ref, u_ref, x2_ref):
    i = pl.program_id(1)
    last = pl.num_programs(1) - 1
    rows = lax.broadcasted_iota(I32, (SC_TM, HY_W), 0)

    def conv(main, prev, nxt, g):
        x = main[0]
        pr = jnp.where(i > 0, prev[0, SUBLANES - 1:SUBLANES, :], 0.0)
        nx = jnp.where(i < last, nxt[0, 0:1, :], 0.0)
        xm = jnp.where(rows == 0, pr, pltpu.roll(x, 1, 0))
        xp = jnp.where(rows == SC_TM - 1, nx, pltpu.roll(x, SC_TM - 1, 0))
        return (w_ref[g:g + 1, :] * xm + w_ref[3 + g:4 + g, :] * x + w_ref[6 + g:7 + g, :] * xp
                + b_ref[g:g + 1, :])

    def to_rj(t):
        return jnp.swapaxes(t.reshape(SC_J, FN2, HY_W), 0, 1).astype(BF16)

    x1 = conv(m1, a1, n1, 0)
    x2 = conv(m2, a2, n2, 1)
    v = conv(m3, a3, n3, 2)
    u_ref[0] = to_rj(v * x1)
    x2_ref[0] = to_rj(x2)


def _short_conv(p, cw9, cb3):
    b = p.shape[0]
    nblk8 = SEQ // SUBLANES
    step8 = SC_TM // SUBLANES
    specs = []
    for g in range(3):
        specs += [pl.BlockSpec((1, SC_TM, HY_W), lambda bi, i, g=g: (bi, i, g)),
                  pl.BlockSpec((1, SUBLANES, HY_W), lambda bi, i, g=g: (bi, jnp.maximum(i * step8 - 1, 0), g)),
                  pl.BlockSpec((1, SUBLANES, HY_W), lambda bi, i, g=g: (bi, jnp.minimum((i + 1) * step8, nblk8 - 1), g))]
    specs += [pl.BlockSpec((9, HY_W), lambda bi, i: (0, 0)), pl.BlockSpec((3, HY_W), lambda bi, i: (0, 0))]
    out = pl.BlockSpec((1, FN2, SC_J, HY_W), lambda bi, i: (bi, 0, i, 0))
    return pl.pallas_call(
        _sconv_body,
        grid=(b, SEQ // SC_TM),
        in_specs=specs,
        out_specs=[out, out],
        out_shape=[jax.ShapeDtypeStruct((b, FN2, SEQ // FN2, HY_W), BF16)] * 2,
        compiler_params=_cparams(("parallel", "parallel")),
        name="short_conv",
    )(p, p, p, p, p, p, p, p, p, cw9, cb3)


FILT_TR = 1024


def _filter_body(z_ref, t_ref, msk_ref, fwd_ref, w1_ref, b1_ref, w2_ref, b2_ref, w3_ref, fr_ref, dl_ref,
                 k_ref, s_ref):
    fr = fr_ref[...]
    h = jnp.sin(fr * (_dot3(z_ref[...], w1_ref[...]) + b1_ref[...]))
    h = jnp.sin(fr * (_dot3(h, w2_ref[...]) + b2_ref[...]))
    h = _dot3(h, w3_ref[...])
    h = jnp.where(fwd_ref[...] > 0.5, h[:, :HY_W], h[:, HY_W:])
    kern = h * jnp.exp(t_ref[...] * dl_ref[...]) * msk_ref[...]
    k_ref[...] = kern

    @pl.when(pl.program_id(0) == 0)
    def _():
        s_ref[...] = jnp.zeros_like(s_ref)

    s_ref[...] += jnp.sum(jnp.abs(kern), axis=0, keepdims=True)


def _implicit_filter(ztab, ttab, mtab, ftab, w1p, b1, w2, b2, w3, freq, negdelta):
    rowblk = lambda i: (i, 0)
    const = lambda i: (0, 0)
    col = pl.BlockSpec((FILT_TR, 1), rowblk)
    return pl.pallas_call(
        _filter_body,
        grid=(FN // FILT_TR,),
        in_specs=[pl.BlockSpec((FILT_TR, FORD), rowblk), col, col, col,
                  pl.BlockSpec((FORD, FORD), const),
                  pl.BlockSpec((1, FORD), const),
                  pl.BlockSpec((FORD, FORD), const),
                  pl.BlockSpec((1, FORD), const),
                  pl.BlockSpec((FORD, 2 * HY_W), const),
                  pl.BlockSpec((1, FORD), const),
                  pl.BlockSpec((1, HY_W), const)],
        out_specs=[pl.BlockSpec((FILT_TR, HY_W), rowblk),
                   pl.BlockSpec((1, HY_W), const)],
        out_shape=[jax.ShapeDtypeStruct((FN, HY_W), F32), jax.ShapeDtypeStruct((1, HY_W), F32)],
        compiler_params=_cparams(("arbitrary",)),
        name="implicit_filter",
    )(ztab, ttab, mtab, ftab, w1p, b1, w2, b2, w3, freq, negdelta)


DFT_G = 4
DFT_KB = 16
DFT_NP = FN1 // DFT_KB
DFT_UNROLL = 4


def _dot2c(fh, fl, zb):
    return _dot(fh, zb) + _dot(fl, zb)


def _lead_stage(src, fh, fl, dst_ref):
    def group(rg, _):
        r0 = rg * DFT_G
        rhs = jnp.concatenate([src(r0 + g) for g in range(DFT_G)], axis=1)
        blk = _dot2c(fh, fl, rhs)
        for g in range(DFT_G):
            dst_ref[r0 + g] = blk[:, g * LANES:(g + 1) * LANES]
        return 0

    lax.fori_loop(0, FN2 // DFT_G, group, 0, unroll=DFT_UNROLL)


def _lead_phase(src, lh_ref, ll_ref, p_ref, q_ref):
    for h in range(2):
        rows = slice(h * FN1, (h + 1) * FN1)
        _lead_stage(src, lh_ref[rows, :], ll_ref[rows, :], p_ref)
        q_ref[rows] = jnp.swapaxes(p_ref[...], 0, 1)


def _lane_cat(xs):
    return jnp.concatenate(xs, axis=1)


def _mid_forward(q_ref, k0, tr_ref, ti_ref, fh, fl, half):
    brs, bis, trs, tis = [], [], [], []
    for g in range(DFT_G):
        jj = half * DFT_G + g
        ar, ai = q_ref[k0 + jj], q_ref[FN1 + k0 + jj]
        tr, ti = tr_ref[jj], ti_ref[jj]
        brs.append(ar * tr - ai * ti)
        bis.append(ar * ti + ai * tr)
        trs.append(tr)
        tis.append(ti)
    b = jnp.concatenate([_lane_cat(brs), _lane_cat(bis)], axis=0).astype(BF16)
    return _dot2c(fh, fl, b), _lane_cat(trs), _lane_cat(tis)


def _spectrum_body(k_ref, tr_ref, ti_ref, lh_ref, ll_ref, fh_ref, fl_ref, sc_ref, o_ref, p_ref, q_ref):
    ph = pl.program_id(1)

    @pl.when(ph == 0)
    def _():
        _lead_phase(lambda r: k_ref[r].astype(BF16), lh_ref, ll_ref, p_ref, q_ref)

    @pl.when(ph > 0)
    def _():
        k0 = (ph - 1) * DFT_KB
        sc = _lane_cat([sc_ref[...]] * DFT_G)
        for half in range(DFT_KB // DFT_G):
            x, _, _ = _mid_forward(q_ref, k0, tr_ref, ti_ref, fh_ref[...], fl_ref[...], half)
            x = x * sc
            for g in range(DFT_G):
                lanes = slice(g * LANES, (g + 1) * LANES)
                o_ref[0, half * DFT_G + g] = x[:FN2, lanes]
                o_ref[1, half * DFT_G + g] = x[FN2:, lanes]


def _mid_index(ph):
    return jnp.clip(ph - 1, 0, DFT_NP - 1)


def _filter_spectrum(kern_rj, twr, twi, lh, ll, fh, fl, scale):
    const = lambda c, ph: (0, 0)
    tw = pl.BlockSpec((DFT_KB, FN2, LANES), lambda c, ph: (_mid_index(ph), 0, 0))
    return pl.pallas_call(
        _spectrum_body,
        grid=(HY_W // LANES, DFT_NP + 1),
        in_specs=[pl.BlockSpec((FN2, FN1, LANES), lambda c, ph: (0, 0, c), pipeline_mode=pl.Buffered(1)),
                  tw, tw,
                  pl.BlockSpec(lh.shape, const), pl.BlockSpec(ll.shape, const),
                  pl.BlockSpec(fh.shape, const), pl.BlockSpec(fl.shape, const),
                  pl.BlockSpec((1, LANES), lambda c, ph: (0, c))],
        out_specs=pl.BlockSpec((2, DFT_KB, FN2, LANES), lambda c, ph: (0, _mid_index(ph), 0, c)),
        out_shape=jax.ShapeDtypeStruct((2, FN1, FN2, HY_W), F32),
        scratch_shapes=[pltpu.VMEM((FN2, FN1, LANES), F32), pltpu.VMEM((2 * FN1, FN2, LANES), F32)],
        compiler_params=_cparams(("parallel", "arbitrary")),
        name="filter_spectrum",
    )(kern_rj, twr, twi, lh, ll, fh, fl, scale)


def _hconv_body(u_ref, x2_ref, kh_ref, tr_ref, ti_ref, lh_ref, ll_ref, fh_ref, fl_ref, gh_ref, gl_ref,
                oh_ref, ol_ref, bias_ref, o_ref, p_ref, q_ref):
    ph = pl.program_id(1)

    def both(ref, r):
        return jnp.concatenate([ref[0, r], ref[1, r]], axis=0)

    @pl.when(ph == 0)
    def _():
        _lead_phase(lambda r: both(u_ref, r), lh_ref, ll_ref, p_ref, q_ref)

    @pl.when((ph > 0) & (ph <= DFT_NP))
    def _():
        k0 = (ph - 1) * DFT_KB
        for half in range(DFT_KB // DFT_G):
            x, tr, ti = _mid_forward(q_ref, k0, tr_ref, ti_ref, fh_ref[...], fl_ref[...], half)
            xr, xi = x[:FN2], x[FN2:]
            kr = _lane_cat([kh_ref[0, half * DFT_G + g] for g in range(DFT_G)])
            ki = _lane_cat([kh_ref[1, half * DFT_G + g] for g in range(DFT_G)])
            y = jnp.concatenate([xr * kr - xi * ki, xr * ki + xi * kr], axis=0).astype(BF16)
            c = _dot2c(gh_ref[...], gl_ref[...], y)
            cr, ci = c[:FN2], c[FN2:]
            dr = cr * tr + ci * ti
            di = ci * tr - cr * ti
            for g in range(DFT_G):
                lanes = slice(g * LANES, (g + 1) * LANES)
                q_ref[k0 + half * DFT_G + g] = dr[:, lanes]
                q_ref[FN1 + k0 + half * DFT_G + g] = di[:, lanes]

    @pl.when(ph == DFT_NP + 1)
    def _():
        bias = bias_ref[...]
        p_ref[...] = jnp.swapaxes(q_ref[0:FN1], 0, 1)
        _lead_stage(lambda r: p_ref[r].astype(BF16), oh_ref[:, 0:FN1], ol_ref[:, 0:FN1], q_ref)
        p_ref[...] = jnp.swapaxes(q_ref[FN1:2 * FN1], 0, 1)
        oh2, ol2 = oh_ref[:, FN1:2 * FN1], ol_ref[:, FN1:2 * FN1]

        def group(rg, _):
            r0 = rg * DFT_G
            rhs = _lane_cat([p_ref[r0 + g].astype(BF16) for g in range(DFT_G)])
            blk = _dot2c(oh2, ol2, rhs)
            for g in range(DFT_G):
                r = r0 + g
                y = q_ref[r] + blk[:, g * LANES:(g + 1) * LANES]
                q_ref[r] = (y + both(u_ref, r).astype(F32) * bias) * both(x2_ref, r).astype(F32)
            return 0

        lax.fori_loop(0, FN2 // DFT_G, group, 0, unroll=DFT_UNROLL)
        p_ref[...] = jnp.swapaxes(q_ref[0:FN2], 0, 1)
        nj = SEQ // FN2
        for b in range(2):
            o_ref[b] = p_ref[b * nj:(b + 1) * nj].reshape(SEQ, LANES).astype(BF16)


def _hyena_conv(u_rj, x2_rj, khat, twr, twi, lead, fwd, inv, out, bias):
    const = lambda c, ph: (0, 0)
    nj = SEQ // FN2
    tw = pl.BlockSpec((DFT_KB, FN2, LANES), lambda c, ph: (_mid_index(ph), 0, 0))
    sig = pl.BlockSpec((2, FN2, nj, LANES), lambda c, ph: (0, 0, 0, c), pipeline_mode=pl.Buffered(1))
    mats = [m for pair in (lead, fwd, inv, out) for m in pair]
    return pl.pallas_call(
        _hconv_body,
        grid=(HY_W // LANES, DFT_NP + 2),
        in_specs=[sig, sig,
                  pl.BlockSpec((2, DFT_KB, FN2, LANES), lambda c, ph: (0, _mid_index(ph), 0, c)),
                  tw, tw] + [pl.BlockSpec(m.shape, const) for m in mats]
                 + [pl.BlockSpec((1, LANES), lambda c, ph: (0, c))],
        out_specs=pl.BlockSpec((2, SEQ, LANES), lambda c, ph: (0, 0, c)),
        out_shape=jax.ShapeDtypeStruct((2, SEQ, HY_W), BF16),
        scratch_shapes=[pltpu.VMEM((FN2, FN1, LANES), F32), pltpu.VMEM((2 * FN1, FN2, LANES), F32)],
        compiler_params=_cparams(("parallel", "arbitrary")),
        name="hyena_conv",
    )(u_rj, x2_rj, khat, twr, twi, *mats, bias)


OP_TM = 512


def _outproj_body(att_ref, hy_ref, x_ref, w_ref, g1_ref, n2_ref, sh_ref, sc_ref, wrh_ref, wrl_ref,
                  x1_ref, h2_ref, lg_ref):
    a = jnp.concatenate([att_ref[0], hy_ref[0]], axis=1)
    x1 = x_ref[0] + g1_ref[0] * _dot(a, w_ref[...])
    x1_ref[0] = x1
    h2 = _rms_mod(x1, n2_ref[...], sh_ref[0], sc_ref[0])
    hh, hl = _split(h2)
    h2_ref[0] = hh
    wrh = wrh_ref[...]
    lg_ref[0] = _dot(wrh, hh, _NT) + _dot(wrh, hl, _NT) + _dot(wrl_ref[...], hh, _NT)


def _out_projection(att, hy, x, w_out_bf, g1r, n2g, sh2, sc2, wrh, wrl):
    b = x.shape[0]
    tok = lambda bi, i: (bi, i, 0)
    row = lambda bi, i: (bi, 0, 0)
    const = lambda bi, i: (0, 0)
    return pl.pallas_call(
        _outproj_body,
        grid=(b, SEQ // OP_TM),
        in_specs=[pl.BlockSpec((1, OP_TM, ATT_W), tok),
                  pl.BlockSpec((1, OP_TM, HY_W), tok),
                  pl.BlockSpec((1, OP_TM, D), tok),
                  pl.BlockSpec((ATT_W + HY_W, D), const),
                  pl.BlockSpec((1, 1, D), row),
                  pl.BlockSpec((1, D), const),
                  pl.BlockSpec((1, 1, D), row),
                  pl.BlockSpec((1, 1, D), row),
                  pl.BlockSpec((NE, D), const),
                  pl.BlockSpec((NE, D), const)],
        out_specs=[pl.BlockSpec((1, OP_TM, D), tok),
                   pl.BlockSpec((1, OP_TM, D), tok),
                   pl.BlockSpec((1, NE, OP_TM), lambda bi, i: (bi, 0, i))],
        out_shape=[jax.ShapeDtypeStruct((b, SEQ, D), F32),
                   jax.ShapeDtypeStruct((b, SEQ, D), BF16),
                   jax.ShapeDtypeStruct((b, NE, SEQ), F32)],
        compiler_params=_cparams(("parallel", "parallel")),
        name="out_projection",
    )(att, hy, x, w_out_bf, g1r, n2g, sh2, sc2, wrh, wrl)


def _routing_body(lg_ref, tri_ref, pos_ref, gate_ref, off_ref, cs_ref):
    lg = lg_ref[0]
    e = jnp.exp(lg - jnp.max(lg, axis=0, keepdims=True))
    aff = e / jnp.sum(e, axis=0, keepdims=True)
    gate_ref[0] = aff
    def count_ge(t):
        return jnp.sum(jnp.where(aff >= t, 1.0, 0.0), axis=1, keepdims=True)

    def bisect(i, thr):
        cand = thr | (jnp.int32(1) << (30 - i))
        return jnp.where(count_ge(pltpu.bitcast(cand, F32)) >= float(CAP), cand, thr)

    thr = lax.fori_loop(0, 31, bisect, jnp.zeros((NE, 1), I32))
    lo = pltpu.bitcast(thr, F32)
    hi = jnp.maximum(pltpu.bitcast(thr + 1, F32), jnp.finfo(F32).tiny)

    def refine(i, c):
        lo, hi = c
        mid = lo + (hi - lo) * 0.5
        ok = count_ge(mid) >= float(CAP)
        return jnp.where(ok, mid, lo), jnp.where(ok, hi, mid)

    lo, hi = lax.fori_loop(0, 32, refine, (lo, hi))
    gt = aff >= hi
    eq = (aff >= lo) & jnp.logical_not(gt)
    need = float(CAP) - jnp.sum(jnp.where(gt, 1.0, 0.0), axis=1, keepdims=True)
    tri = tri_ref[...]

    def excl_cumsum(mask_f, record_offsets):
        carry = jnp.zeros((NE, 1), F32)
        for c in range(NTCH):
            sl = slice(c * TCH, (c + 1) * TCH)
            m = mask_f[:, sl]
            inc = _dot(m.astype(BF16), tri)
            cs_ref[:, sl] = inc - m + carry
            if record_offsets:
                off_ref[0, :, c:c + 1] = carry.astype(I32)
            carry = carry + inc[:, TCH - 1:TCH]
        return cs_ref[...]

    eq_rank = excl_cumsum(jnp.where(eq, 1.0, 0.0), False)
    sel = gt | (eq & (eq_rank < need))
    pos = excl_cumsum(jnp.where(sel, 1.0, 0.0), True)
    pos_ref[0] = jnp.where(sel, pos.astype(I32), -1)


def _routing(logits, tri):
    b = logits.shape[0]
    blk = pl.BlockSpec((1, NE, SEQ), lambda bi: (bi, 0, 0))
    return pl.pallas_call(
        _routing_body,
        grid=(b,),
        in_specs=[blk, pl.BlockSpec((TCH, TCH), lambda bi: (0, 0))],
        out_specs=[blk, blk, pl.BlockSpec((1, NE, NTCH), lambda bi: (bi, 0, 0))],
        out_shape=[jax.ShapeDtypeStruct((b, NE, SEQ), I32),
                   jax.ShapeDtypeStruct((b, NE, SEQ), F32),
                   jax.ShapeDtypeStruct((b, NE, NTCH), I32)],
        scratch_shapes=[pltpu.VMEM((NE, SEQ), F32)],
        compiler_params=_cparams(("parallel",)),
        name="routing",
    )(logits, tri)


GATHER_UNROLL = 8


def _gather_body(off_ref, h_ref, pos_ref, xg_ref, acc_ref):
    b = pl.program_id(0)
    e = pl.program_id(1)
    acc_ref[...] = jnp.zeros_like(acc_ref)
    crow = lax.broadcasted_iota(I32, (GW, TCH), 0)

    def chunks(i, _):
        for j in range(GATHER_UNROLL):
            c = i * GATHER_UNROLL + j
            off = off_ref[(b * NE + e) * NTCH + c]
            base = pl.multiple_of(jnp.minimum((off >> 3) << 3, CAP - TCH), SUBLANES)
            t0 = pl.multiple_of(c * TCH, TCH)
            rel = pos_ref[0, 0, pl.ds(c, 1), :] - base
            onehot = jnp.where(crow == rel, 1.0, 0.0).astype(BF16)
            acc_ref[pl.ds(base, GW), :] += _dot(onehot, h_ref[0, pl.ds(t0, TCH), :])
        return 0

    lax.fori_loop(0, NTCH // GATHER_UNROLL, chunks, 0)
    xg_ref[0, 0] = acc_ref[0:CAP, :].astype(BF16)


def _gather(offs_flat, h2, pos4):
    b = h2.shape[0]
    grid_spec = pltpu.PrefetchScalarGridSpec(
        num_scalar_prefetch=1,
        grid=(b, NE),
        in_specs=[pl.BlockSpec((1, SEQ, D), lambda bi, e, off: (bi, 0, 0)),
                  pl.BlockSpec((1, 1, NTCH, TCH), lambda bi, e, off: (bi, e, 0, 0))],
        out_specs=pl.BlockSpec((1, 1, CAP, D), lambda bi, e, off: (bi, e, 0, 0)),
        scratch_shapes=[pltpu.VMEM((CAP + SUBLANES, D), F32)],
    )
    return pl.pallas_call(
        _gather_body,
        grid_spec=grid_spec,
        out_shape=jax.ShapeDtypeStruct((b, NE, CAP, D), BF16),
        compiler_params=_cparams(("parallel", "arbitrary")),
        name="moe_gather",
    )(offs_flat, h2, pos4)


FFN_TM = 512
FFN_NF = 4
FFN_FC = DEXP // FFN_NF
assert FFN_FC * FFN_NF == DEXP and FFN_FC % BF16_ROWS == 0


def _ffn_body(xg_ref, wgt_ref, wut_ref, wd_ref, y_ref, acc_ref):
    j = pl.program_id(1)
    nb = xg_ref.shape[0]

    @pl.when(j == 0)
    def _():
        acc_ref[...] = jnp.zeros_like(acc_ref)

    wgt = wgt_ref[0].astype(BF16)
    wut = wut_ref[0].astype(BF16)
    wd = wd_ref[0].astype(BF16)
    for b in range(nb):
        for mb in range(CAP // FFN_TM):
            rows = slice(mb * FFN_TM, (mb + 1) * FFN_TM)
            xb = xg_ref[b, 0, rows, :]
            a = _dot(xb, wgt, _NT)
            u = _dot(xb, wut, _NT)
            h = (a * (1.0 / (1.0 + jnp.exp(-a))) * u).astype(BF16)
            acc_ref[b, rows, :] += _dot(h, wd)

    @pl.when(j == FFN_NF - 1)
    def _():
        for b in range(nb):
            y_ref[b, 0, 0:CAP, :] = acc_ref[b].astype(BF16)
            y_ref[b, 0, CAP:YROWS, :] = jnp.zeros((YROWS - CAP, D), BF16)


def _expert_ffn(xg, w_gate_t, w_up_t, w_down):
    b = xg.shape[0]
    wblk = pl.BlockSpec((1, FFN_FC, D), lambda e, j: (e, j, 0))
    return pl.pallas_call(
        _ffn_body,
        grid=(NE, FFN_NF),
        in_specs=[pl.BlockSpec((b, 1, CAP, D), lambda e, j: (0, e, 0, 0)), wblk, wblk, wblk],
        out_specs=pl.BlockSpec((b, 1, YROWS, D), lambda e, j: (0, e, 0, 0)),
        out_shape=jax.ShapeDtypeStruct((b, NE, YROWS, D), BF16),
        scratch_shapes=[pltpu.VMEM((b, CAP, D), F32)],
        compiler_params=_cparams(("parallel", "arbitrary")),
        name="expert_ffn",
    )(xg, w_gate_t, w_up_t, w_down)


def _combine_body(off_ref, y_ref, pos_ref, gate_ref, x1_ref, g2_ref, o_ref):
    b = pl.program_id(0)
    i = pl.program_id(1)
    crow = lax.broadcasted_iota(I32, (CW, TCH), 0)
    acc = jnp.zeros((TCH, D), F32)
    for e in range(NE):
        off = off_ref[(b * NE + e) * NTCH + i]
        base = pl.multiple_of(jnp.minimum((off >> 4) << 4, CAP - TCH), BF16_ROWS)
        rel = pos_ref[0, e:e + 1, :] - base
        w = jnp.where(crow == rel, gate_ref[0, e:e + 1, :], 0.0)
        yw = y_ref[0, e, pl.ds(base, CW), :]
        acc = acc + _dot(w.astype(BF16), yw, _TN)
    o_ref[0] = x1_ref[0] + g2_ref[0] * acc


def _combine(offs_flat, y, pos, gate, x1, g2r):
    b = x1.shape[0]
    grid_spec = pltpu.PrefetchScalarGridSpec(
        num_scalar_prefetch=1,
        grid=(b, NTCH),
        in_specs=[pl.BlockSpec((1, NE, YROWS, D), lambda bi, i, off: (bi, 0, 0, 0),
                               pipeline_mode=pl.Buffered(1)),
                  pl.BlockSpec((1, NE, TCH), lambda bi, i, off: (bi, 0, i)),
                  pl.BlockSpec((1, NE, TCH), lambda bi, i, off: (bi, 0, i)),
                  pl.BlockSpec((1, TCH, D), lambda bi, i, off: (bi, i, 0)),
                  pl.BlockSpec((1, 1, D), lambda bi, i, off: (bi, 0, 0))],
        out_specs=pl.BlockSpec((1, TCH, D), lambda bi, i, off: (bi, i, 0)),
    )
    return pl.pallas_call(
        _combine_body,
        grid_spec=grid_spec,
        out_shape=jax.ShapeDtypeStruct((b, SEQ, D), F32),
        compiler_params=_cparams(("parallel", "arbitrary")),
        name="moe_combine",
    )(offs_flat, y, pos, gate, x1, g2r)


def _np_split(m):
    hi = np.asarray(m, np.float64).astype(BF16)
    lo = (m - hi.astype(np.float64)).astype(BF16)
    return jnp.asarray(hi), jnp.asarray(lo)


@functools.lru_cache(maxsize=None)
def _dft_tables():
    a = np.arange(FN1, dtype=np.float64)
    ang = 2.0 * np.pi * np.outer(a, a) / FN1
    fr, fi = np.cos(ang), -np.sin(ang)
    half = SEQ // FN2
    lead_u = np.block([[fr[:, :half], -fi[:, :half]], [fi[:, :half], fr[:, :half]]])
    lead_k = np.concatenate([fr, fi], axis=0)
    fwd = np.block([[fr, -fi], [fi, fr]])
    inv = np.block([[fr, fi], [-fi, fr]])
    out = np.block([[fr[:half], fi[:half]], [-fi[:half], fr[:half]]])
    n2 = np.arange(FN2, dtype=np.float64)
    tw = 2.0 * np.pi * np.outer(a, n2) / FN
    twr = np.broadcast_to(np.cos(tw)[:, :, None], (FN1, FN2, LANES)).astype(np.float32)
    twi = np.broadcast_to(-np.sin(tw)[:, :, None], (FN1, FN2, LANES)).astype(np.float32)
    return dict(lead_u=lead_u, lead_k=lead_k, fwd=fwd, inv=inv, out=out, twr=twr, twi=twi)


@functools.lru_cache(maxsize=None)
def _filter_tables():
    L = SEQ
    n = np.arange(FN).reshape(FN1, FN2).T.reshape(-1)
    lag = np.where(n < L, n, FN - n)
    jc = np.minimum(lag, L - 1).astype(np.float64)
    t = (jc / (L - 1))[:, None]
    bands = (FEMB - 1) // 2
    w = 2.0 * np.pi * jc / L
    f = np.linspace(1e-4, bands - 1, bands)
    fw = w[:, None] * f[None, :]
    z = np.concatenate([t, np.cos(fw), -np.sin(fw), np.zeros((FN, FORD - FEMB))], axis=-1)
    mask = np.where(n == L, 0.0, 1.0)[:, None]
    fwd = np.where(n < L, 1.0, 0.0)[:, None]
    max_decay = math.log(DECAY_TARGET) / FAST_DECAY_PCT
    min_decay = math.log(DECAY_TARGET) / SLOW_DECAY_PCT
    negdelta = -np.abs(np.linspace(min_decay, max_decay, HY_W))[None, :]
    return tuple(np.asarray(a, np.float32) for a in (z, t, mask, fwd, negdelta))


@functools.lru_cache(maxsize=None)
def _rope_tables(n):
    rows = n // GRID_W
    row_id, col_id = np.meshgrid(np.arange(rows, dtype=np.float64), np.arange(GRID_W, dtype=np.float64), indexing="ij")
    quarter = HD // 4
    inv_freq = ROPE_THETA ** (-np.arange(quarter, dtype=np.float64) / quarter)
    ar = row_id.reshape(-1)[:, None] * inv_freq
    ac = col_id.reshape(-1)[:, None] * inv_freq
    cos = np.concatenate([np.cos(ar), np.cos(ar), np.cos(ac), np.cos(ac)], axis=-1)
    sin = np.concatenate([-np.sin(ar), np.sin(ar), -np.sin(ac), np.sin(ac)], axis=-1)
    reps = (1, LANES // HD)
    return np.tile(cos, reps).astype(np.float32), np.tile(sin, reps).astype(np.float32)


def _hyena_long_conv(u_rj, x2_rj, kern, abs_sum, bias):
    tb = _dft_tables()
    twr, twi = tb["twr"], tb["twi"]
    fwd = _np_split(tb["fwd"])
    scale = 1.0 / (abs_sum * float(FN))
    khat = _filter_spectrum(kern.reshape(FN2, FN1, HY_W), twr, twi, *_np_split(tb["lead_k"]), *fwd, scale)
    return _hyena_conv(u_rj, x2_rj, khat, twr, twi, _np_split(tb["lead_u"]), fwd, _np_split(tb["inv"]),
                       _np_split(tb["out"]), bias.reshape(1, HY_W))


def kernel(x, c, ctx, c_ctx, w_mod, b_mod, norm1_g, norm2_g, w_in, w_out, q_norm_g, k_norm_g,
           conv_w, conv_b, filt_w1, filt_b1, filt_w2, filt_b2, filt_w3, filt_freq, hyena_bias,
           w_router, w_gate, w_up, w_down):
    B = x.shape[0]
    assert x.shape == (B, SEQ, D) and B == 2 and ctx.shape == (B, CTX, D) and w_mod.shape[0] == 1
    l = 0

    cc = jnp.concatenate([c, c_ctx[None, :], jnp.zeros((SUBLANES - B - 1, D), F32)], axis=0)
    mod = _modulation(cc, w_mod[l], b_mod[l][None, :])
    sh1, sc1, g1, sh2, sc2, g2 = [mod[:, i * D:(i + 1) * D] for i in range(6)]
    lat = lambda m: m[:B, None, :]
    ctxrow = lambda m: jnp.broadcast_to(m[B:B + 1, None, :], (B, 1, D))

    w_in_bf = w_in[l].astype(BF16)
    gq2 = jnp.tile(q_norm_g[l][None, :], (1, LANES // HD))
    gk2 = jnp.tile(k_norm_g[l][None, :], (1, LANES // HD))
    bd = jnp.asarray(np.kron(np.eye(2 * LANES // HD), np.full((HD, HD), 1.0 / HD)), BF16)
    cos_t, sin_t = _rope_tables(SEQ)
    n1g = norm1_g[l][None, :]

    q, k, vt, p = _in_projection(x, n1g, lat(sh1), lat(sc1), w_in_bf, gq2, gk2, bd, cos_t, sin_t, 512)
    _, kc, vct, _ = _in_projection(ctx, n1g, ctxrow(sh1), ctxrow(sc1), w_in_bf, gq2, gk2, bd,
                                   jnp.ones((CTX, LANES), F32), jnp.zeros((CTX, LANES), F32), CTX)

    kch = jnp.concatenate([k, kc], axis=2).reshape(B, NKV, ATT_NCH, ATT_TK, HD)
    vt_all = jnp.concatenate([vt, vct], axis=3).reshape(B, NKV, HD, ATT_NCH, ATT_TK)
    ones_pad = jnp.concatenate([jnp.ones((B, NKV, ATT_NCH, 1, ATT_TK), BF16),
                                jnp.zeros((B, NKV, ATT_NCH, BF16_ROWS - 1, ATT_TK), BF16)], axis=3)
    vtch = jnp.concatenate([vt_all.transpose(0, 1, 3, 2, 4), ones_pad], axis=3)
    att = _attention(q, kch, vtch)

    cw9 = conv_w[l].reshape(3, 3, HY_W).reshape(9, HY_W)
    cb3 = conv_b[l].reshape(3, HY_W)
    u_rj, x2_rj = _short_conv(p, cw9, cb3)
    ztab, ttab, mtab, ftab, negdelta = _filter_tables()
    w1p = jnp.concatenate([filt_w1[l], jnp.zeros((FORD - FEMB, FORD), F32)], axis=0)
    kern, abs_sum = _implicit_filter(ztab, ttab, mtab, ftab, w1p, filt_b1[l][None, :], filt_w2[l],
                                     filt_b2[l][None, :], filt_w3[l], filt_freq[l][None, :], negdelta)
    hy = _hyena_long_conv(u_rj, x2_rj, kern, abs_sum, hyena_bias[l])

    wrh, wrl = _split(w_router[l].T)
    x1, h2, logits = _out_projection(att, hy, x, w_out[l].astype(BF16), lat(g1), norm2_g[l][None, :],
                                     lat(sh2), lat(sc2), wrh, wrl)

    tri = jnp.asarray(np.triu(np.ones((TCH, TCH))), BF16)
    pos, gate, offs = _routing(logits, tri)
    offs_flat = offs.reshape(-1)
    xg = _gather(offs_flat, h2, pos.reshape(B, NE, NTCH, TCH))
    y = _expert_ffn(xg, jnp.swapaxes(w_gate[l], 1, 2), jnp.swapaxes(w_up[l], 1, 2), w_down[l])
    return _combine(offs_flat, y, pos, gate, x1, lat(g2))
```

```python
import functools
import math

import numpy as np
import jax
import jax.numpy as jnp
from jax import lax
from jax.experimental import pallas as pl
from jax.experimental.pallas import tpu as pltpu

F32 = jnp.float32
BF16 = jnp.bfloat16
I32 = jnp.int32

D = 1024
SEQ = 8192
CTX = 256
GRID_W = 64
ATT_W = 512
HY_W = 512
HD = 64
NQ = 8
NKV = 2
QPK = NQ // NKV
KV_W = NKV * HD
IN_W = ATT_W + 2 * KV_W + 3 * HY_W
FEMB = 33
FORD = 64
NE = 16
CAP = 2 * SEQ // NE
DEXP = 2752
ROPE_THETA = 10000.0
EPS = 1e-6
DECAY_TARGET = 1e-2
FAST_DECAY_PCT = 0.3
SLOW_DECAY_PCT = 1.5

LANES = 128
SUBLANES = 8
BF16_ROWS = 16
VMEM_BYTES_V7X = 64 * 1024 * 1024
VMEM_LIMIT = VMEM_BYTES_V7X - 8 * 1024 * 1024

FN = 2 * SEQ
FN1 = 128
FN2 = 128

TCH = LANES
NTCH = SEQ // TCH
GW = TCH + SUBLANES
CW = TCH + BF16_ROWS
YROWS = CAP + BF16_ROWS


def _cparams(sem, vmem=None):
    return pltpu.CompilerParams(dimension_semantics=sem, vmem_limit_bytes=vmem or VMEM_LIMIT)


def _split(a):
    hi = a.astype(BF16)
    lo = (a - hi.astype(F32)).astype(BF16)
    return hi, lo


_NN = (((1,), (0,)), ((), ()))
_NT = (((1,), (1,)), ((), ()))
_TN = (((0,), (0,)), ((), ()))


def _dot(a, b, dn=_NN):
    return lax.dot_general(a, b, dn, preferred_element_type=F32)


def _dot3(a, b, dn=_NN):
    ah, al = _split(a)
    bh, bl = _split(b)
    return _dot(ah, bh, dn) + _dot(ah, bl, dn) + _dot(al, bh, dn)


def _mod_body(c_ref, w_ref, b_ref, o_ref):
    c = c_ref[...]
    s = c * (1.0 / (1.0 + jnp.exp(-c)))
    o_ref[...] = _dot3(s, w_ref[...]) + b_ref[...]


def _modulation(cc, w_mod, b_mod):
    n = w_mod.shape[1]
    return pl.pallas_call(
        _mod_body,
        grid=(n // D,),
        in_specs=[pl.BlockSpec((SUBLANES, D), lambda j: (0, 0)),
                  pl.BlockSpec((D, D), lambda j: (0, j)),
                  pl.BlockSpec((1, D), lambda j: (0, j))],
        out_specs=pl.BlockSpec((SUBLANES, D), lambda j: (0, j)),
        out_shape=jax.ShapeDtypeStruct((SUBLANES, n), F32),
        compiler_params=_cparams(("arbitrary",)),
        name="modulation",
    )(cc, w_mod, b_mod)


Q_SCALE = HD ** -0.5 * math.log2(math.e)


def _rms_mod(x, g, sh, sc):
    ms = jnp.mean(x * x, axis=-1, keepdims=True)
    return (x * lax.rsqrt(ms + EPS) * g) * (1.0 + sc) + sh


def _head_mean_square(t, bd):
    hi, lo = _split(t * t)
    return _dot(hi, bd) + _dot(lo, bd)


def _head_norm_rope(t, ms, g, cos, sin):
    tn = t * lax.rsqrt(ms + EPS) * g
    lane = lax.broadcasted_iota(I32, tn.shape, 1)
    sw = jnp.where((lane & 31) < 16, pltpu.roll(tn, LANES - 16, 1), pltpu.roll(tn, 16, 1))
    return tn * cos + sw * sin


def _proj_body(x_ref, g_ref, sh_ref, sc_ref, w_ref, gq_ref, gk_ref, bd_ref, cos_ref, sin_ref,
               q_ref, k_ref, v_ref, p_ref):
    h = _rms_mod(x_ref[0], g_ref[...], sh_ref[0], sc_ref[0])
    proj = _dot(h.astype(BF16), w_ref[...])
    bd = bd_ref[...]
    cos = cos_ref[...]
    sin = sin_ref[...]
    wide = 2 * LANES
    for j in range(ATT_W // wide):
        ms = _head_mean_square(proj[:, j * wide:(j + 1) * wide], bd)
        for i in range(2):
            sl = slice(j * wide + i * LANES, j * wide + (i + 1) * LANES)
            qj = _head_norm_rope(proj[:, sl], ms[:, i * LANES:(i + 1) * LANES], gq_ref[...], cos, sin)
            q_ref[0, :, sl] = (qj * Q_SCALE).astype(BF16)
    ms = _head_mean_square(proj[:, ATT_W:ATT_W + 2 * KV_W], bd)
    kk = _head_norm_rope(proj[:, ATT_W:ATT_W + KV_W], ms[:, 0:KV_W], gk_ref[...], cos, sin)
    vt = proj[:, ATT_W + KV_W:ATT_W + 2 * KV_W].T
    for g in range(NKV):
        k_ref[0, g] = kk[:, g * HD:(g + 1) * HD].astype(BF16)
        v_ref[0, g] = vt[g * HD:(g + 1) * HD, :].astype(BF16)
    p_ref[0] = proj[:, ATT_W + 2 * KV_W:]


def _in_projection(x, g1, sh, sc, w_in_bf, gq2, gk2, bd, cos_t, sin_t, tm):
    b, s, _ = x.shape
    row = lambda bi, i: (bi, 0, 0)
    tok = lambda bi, i: (bi, i, 0)
    const = lambda bi, i: (0, 0)
    return pl.pallas_call(
        _proj_body,
        grid=(b, s // tm),
        in_specs=[pl.BlockSpec((1, tm, D), tok),
                  pl.BlockSpec((1, D), const),
                  pl.BlockSpec((1, 1, D), row),
                  pl.BlockSpec((1, 1, D), row),
                  pl.BlockSpec((D, IN_W), const),
                  pl.BlockSpec((1, LANES), const),
                  pl.BlockSpec((1, LANES), const),
                  pl.BlockSpec((2 * LANES, 2 * LANES), const),
                  pl.BlockSpec((tm, LANES), lambda bi, i: (i, 0)),
                  pl.BlockSpec((tm, LANES), lambda bi, i: (i, 0))],
        out_specs=[pl.BlockSpec((1, tm, ATT_W), tok),
                   pl.BlockSpec((1, NKV, tm, HD), lambda bi, i: (bi, 0, i, 0)),
                   pl.BlockSpec((1, NKV, HD, tm), lambda bi, i: (bi, 0, 0, i)),
                   pl.BlockSpec((1, tm, 3 * HY_W), tok)],
        out_shape=[jax.ShapeDtypeStruct((b, s, ATT_W), BF16),
                   jax.ShapeDtypeStruct((b, NKV, s, HD), BF16),
                   jax.ShapeDtypeStruct((b, NKV, HD, s), BF16),
                   jax.ShapeDtypeStruct((b, s, 3 * HY_W), F32)],
        compiler_params=_cparams(("parallel", "parallel")),
        name="in_projection",
    )(x, g1, sh, sc, w_in_bf, gq2, gk2, bd, cos_t, sin_t)


ATT_TQ = 512
ATT_TK = 768
SK = SEQ + CTX
ATT_NCH = SK // ATT_TK


ATT_NQ = QPK * ATT_TQ
ATT_VR = HD + BF16_ROWS
assert ATT_NCH % 2 == 1


def _attn_body(q_ref, k_ref, vt_ref, o_ref, s_ref, mx_ref, m_ref, acc_ref):
    qall = jnp.concatenate([q_ref[0, :, r * HD:(r + 1) * HD] for r in range(QPK)], axis=0)
    m_ref[...] = jnp.full(m_ref.shape, -1e30, F32)
    acc_ref[...] = jnp.zeros_like(acc_ref)

    def scores(c, slot):
        s = _dot(k_ref[0, 0, c], qall, _NT)
        s_ref[slot] = s
        mx_ref[slot] = jnp.max(s, axis=0, keepdims=True)

    def update(c, slot):
        m_old = m_ref[...]
        m_new = jnp.maximum(m_old, mx_ref[slot])
        p = jnp.exp2(s_ref[slot] - m_new).astype(BF16)
        acc_ref[...] = jnp.exp2(m_old - m_new) * acc_ref[...] + _dot(vt_ref[0, 0, c], p)
        m_ref[...] = m_new

    scores(0, 0)

    def pair(i, _):
        c = 2 * i
        scores(c + 1, 1)
        update(c, 0)
        scores(c + 2, 0)
        update(c + 1, 1)
        return 0

    lax.fori_loop(0, ATT_NCH // 2, pair, 0)
    update(ATT_NCH - 1, 0)
    out = acc_ref[0:HD, :] * (1.0 / acc_ref[HD:HD + 1, :])
    for r in range(QPK):
        o_ref[0, :, r * HD:(r + 1) * HD] = out[:, r * ATT_TQ:(r + 1) * ATT_TQ].T.astype(BF16)


def _attention(q, kch, vtch):
    b = q.shape[0]
    return pl.pallas_call(
        _attn_body,
        grid=(b, NKV, SEQ // ATT_TQ),
        in_specs=[pl.BlockSpec((1, ATT_TQ, QPK * HD), lambda bi, g, i: (bi, i, g)),
                  pl.BlockSpec((1, 1, ATT_NCH, ATT_TK, HD), lambda bi, g, i: (bi, g, 0, 0, 0)),
                  pl.BlockSpec((1, 1, ATT_NCH, ATT_VR, ATT_TK), lambda bi, g, i: (bi, g, 0, 0, 0))],
        out_specs=pl.BlockSpec((1, ATT_TQ, QPK * HD), lambda bi, g, i: (bi, i, g)),
        out_shape=jax.ShapeDtypeStruct((b, SEQ, ATT_W), BF16),
        scratch_shapes=[pltpu.VMEM((2, ATT_TK, ATT_NQ), F32), pltpu.VMEM((2, 1, ATT_NQ), F32),
                        pltpu.VMEM((1, ATT_NQ), F32), pltpu.VMEM((ATT_VR, ATT_NQ), F32)],
        compiler_params=_cparams(("parallel", "parallel", "parallel")),
        name="attention",
    )(q, kch, vtch)


SC_TM = 2048
SC_J = SC_TM // FN2


def _sconv_body(m1, a1, n1, m2, a2, n2, m3, a3, n3, w_ref, b_ref, u_ref, x2_ref):
    i = pl.program_id(1)
    last = pl.num_programs(1) - 1
    rows = lax.broadcasted_iota(I32, (SC_TM, HY_W), 0)

    def conv(main, prev, nxt, g):
        x = main[0]
        pr = jnp.where(i > 0, prev[0, SUBLANES - 1:SUBLANES, :], 0.0)
        nx = jnp.where(i < last, nxt[0, 0:1, :], 0.0)
        xm = jnp.where(rows == 0, pr, pltpu.roll(x, 1, 0))
        xp = jnp.where(rows == SC_TM - 1, nx, pltpu.roll(x, SC_TM - 1, 0))
        return (w_ref[g:g + 1, :] * xm + w_ref[3 + g:4 + g, :] * x + w_ref[6 + g:7 + g, :] * xp
                + b_ref[g:g + 1, :])

    def to_rj(t):
        return jnp.swapaxes(t.reshape(SC_J, FN2, HY_W), 0, 1).astype(BF16)

    x1 = conv(m1, a1, n1, 0)
    x2 = conv(m2, a2, n2, 1)
    v = conv(m3, a3, n3, 2)
    u_ref[0] = to_rj(v * x1)
    x2_ref[0] = to_rj(x2)


def _short_conv(p, cw9, cb3):
    b = p.shape[0]
    nblk8 = SEQ // SUBLANES
    step8 = SC_TM // SUBLANES
    specs = []
    for g in range(3):
        specs += [pl.BlockSpec((1, SC_TM, HY_W), lambda bi, i, g=g: (bi, i, g)),
                  pl.BlockSpec((1, SUBLANES, HY_W), lambda bi, i, g=g: (bi, jnp.maximum(i * step8 - 1, 0), g)),
                  pl.BlockSpec((1, SUBLANES, HY_W), lambda bi, i, g=g: (bi, jnp.minimum((i + 1) * step8, nblk8 - 1), g))]
    specs += [pl.BlockSpec((9, HY_W), lambda bi, i: (0, 0)), pl.BlockSpec((3, HY_W), lambda bi, i: (0, 0))]
    out = pl.BlockSpec((1, FN2, SC_J, HY_W), lambda bi, i: (bi, 0, i, 0))
    return pl.pallas_call(
        _sconv_body,
        grid=(b, SEQ // SC_TM),
        in_specs=specs,
        out_specs=[out, out],
        out_shape=[jax.ShapeDtypeStruct((b, FN2, SEQ // FN2, HY_W), BF16)] * 2,
        compiler_params=_cparams(("parallel", "parallel")),
        name="short_conv",
    )(p, p, p, p, p, p, p, p, p, cw9, cb3)


FILT_TR = 1024


def _filter_body(z_ref, t_ref, msk_ref, fwd_ref, w1_ref, b1_ref, w2_ref, b2_ref, w3_ref, fr_ref, dl_ref,
                 k_ref, s_ref):
    fr = fr_ref[...]
    h = jnp.sin(fr * (_dot3(z_ref[...], w1_ref[...]) + b1_ref[...]))
    h = jnp.sin(fr * (_dot3(h, w2_ref[...]) + b2_ref[...]))
    h = _dot3(h, w3_ref[...])
    h = jnp.where(fwd_ref[...] > 0.5, h[:, :HY_W], h[:, HY_W:])
    kern = h * jnp.exp(t_ref[...] * dl_ref[...]) * msk_ref[...]
    k_ref[...] = kern

    @pl.when(pl.program_id(0) == 0)
    def _():
        s_ref[...] = jnp.zeros_like(s_ref)

    s_ref[...] += jnp.sum(jnp.abs(kern), axis=0, keepdims=True)


def _implicit_filter(ztab, ttab, mtab, ftab, w1p, b1, w2, b2, w3, freq, negdelta):
    rowblk = lambda i: (i, 0)
    const = lambda i: (0, 0)
    col = pl.BlockSpec((FILT_TR, 1), rowblk)
    return pl.pallas_call(
        _filter_body,
        grid=(FN // FILT_TR,),
        in_specs=[pl.BlockSpec((FILT_TR, FORD), rowblk), col, col, col,
                  pl.BlockSpec((FORD, FORD), const),
                  pl.BlockSpec((1, FORD), const),
                  pl.BlockSpec((FORD, FORD), const),
                  pl.BlockSpec((1, FORD), const),
                  pl.BlockSpec((FORD, 2 * HY_W), const),
                  pl.BlockSpec((1, FORD), const),
                  pl.BlockSpec((1, HY_W), const)],
        out_specs=[pl.BlockSpec((FILT_TR, HY_W), rowblk),
                   pl.BlockSpec((1, HY_W), const)],
        out_shape=[jax.ShapeDtypeStruct((FN, HY_W), F32), jax.ShapeDtypeStruct((1, HY_W), F32)],
        compiler_params=_cparams(("arbitrary",)),
        name="implicit_filter",
    )(ztab, ttab, mtab, ftab, w1p, b1, w2, b2, w3, freq, negdelta)


DFT_G = 4
DFT_KB = 16
DFT_NP = FN1 // DFT_KB
DFT_UNROLL = 4


def _dot2c(fh, fl, zb):
    return _dot(fh, zb) + _dot(fl, zb)


def _lead_stage(src, fh, fl, dst_ref):
    def group(rg, _):
        r0 = rg * DFT_G
        rhs = jnp.concatenate([src(r0 + g) for g in range(DFT_G)], axis=1)
        blk = _dot2c(fh, fl, rhs)
        for g in range(DFT_G):
            dst_ref[r0 + g] = blk[:, g * LANES:(g + 1) * LANES]
        return 0

    lax.fori_loop(0, FN2 // DFT_G, group, 0, unroll=DFT_UNROLL)


def _lead_phase(src, lh_ref, ll_ref, p_ref, q_ref):
    for h in range(2):
        rows = slice(h * FN1, (h + 1) * FN1)
        _lead_stage(src, lh_ref[rows, :], ll_ref[rows, :], p_ref)
        q_ref[rows] = jnp.swapaxes(p_ref[...], 0, 1)


def _lane_cat(xs):
    return jnp.concatenate(xs, axis=1)


def _mid_forward(q_ref, k0, tr_ref, ti_ref, fh, fl, half):
    tr_t, ti_t = tr_ref[...].T, ti_ref[...].T
    brs, bis, trs, tis = [], [], [], []
    for g in range(DFT_G):
        jj = half * DFT_G + g
        ar, ai = q_ref[k0 + jj], q_ref[FN1 + k0 + jj]
        tr = jnp.broadcast_to(tr_t[:, jj:jj + 1], (FN2, LANES))
        ti = jnp.broadcast_to(ti_t[:, jj:jj + 1], (FN2, LANES))
        brs.append(ar * tr - ai * ti)
        bis.append(ar * ti + ai * tr)
        trs.append(tr)
        tis.append(ti)
    b = jnp.concatenate([_lane_cat(brs), _lane_cat(bis)], axis=0).astype(BF16)
    return _dot2c(fh, fl, b), _lane_cat(trs), _lane_cat(tis)


def _spectrum_body(k_ref, tr_ref, ti_ref, lh_ref, ll_ref, fh_ref, fl_ref, sc_ref, o_ref, p_ref, q_ref):
    ph = pl.program_id(1)

    @pl.when(ph == 0)
    def _():
        _lead_phase(lambda r: k_ref[r].astype(BF16), lh_ref, ll_ref, p_ref, q_ref)

    @pl.when(ph > 0)
    def _():
        k0 = (ph - 1) * DFT_KB
        sc = _lane_cat([sc_ref[...]] * DFT_G)
        for half in range(DFT_KB // DFT_G):
            x, _, _ = _mid_forward(q_ref, k0, tr_ref, ti_ref, fh_ref[...], fl_ref[...], half)
            x = x * sc
            for g in range(DFT_G):
                lanes = slice(g * LANES, (g + 1) * LANES)
                o_ref[0, half * DFT_G + g] = x[:FN2, lanes]
                o_ref[1, half * DFT_G + g] = x[FN2:, lanes]


def _mid_index(ph):
    return jnp.clip(ph - 1, 0, DFT_NP - 1)


def _filter_spectrum(kern_rj, twr, twi, lh, ll, fh, fl, scale):
    const = lambda c, ph: (0, 0)
    tw = pl.BlockSpec((DFT_KB, FN2), lambda c, ph: (_mid_index(ph), 0))
    return pl.pallas_call(
        _spectrum_body,
        grid=(HY_W // LANES, DFT_NP + 1),
        in_specs=[pl.BlockSpec((FN2, FN1, LANES), lambda c, ph: (0, 0, c), pipeline_mode=pl.Buffered(1)),
                  tw, tw,
                  pl.BlockSpec(lh.shape, const), pl.BlockSpec(ll.shape, const),
                  pl.BlockSpec(fh.shape, const), pl.BlockSpec(fl.shape, const),
                  pl.BlockSpec((1, LANES), lambda c, ph: (0, c))],
        out_specs=pl.BlockSpec((2, DFT_KB, FN2, LANES), lambda c, ph: (0, _mid_index(ph), 0, c)),
        out_shape=jax.ShapeDtypeStruct((2, FN1, FN2, HY_W), F32),
        scratch_shapes=[pltpu.VMEM((FN2, FN1, LANES), F32), pltpu.VMEM((2 * FN1, FN2, LANES), F32)],
        compiler_params=_cparams(("parallel", "arbitrary")),
        name="filter_spectrum",
    )(kern_rj, twr, twi, lh, ll, fh, fl, scale)


def _hconv_body(u_ref, x2_ref, kh_ref, tr_ref, ti_ref, lh_ref, ll_ref, fh_ref, fl_ref, gh_ref, gl_ref,
                oh_ref, ol_ref, bias_ref, o_ref, p_ref, q_ref):
    ph = pl.program_id(1)

    def both(ref, r):
        return jnp.concatenate([ref[0, r], ref[1, r]], axis=0)

    @pl.when(ph == 0)
    def _():
        _lead_phase(lambda r: both(u_ref, r), lh_ref, ll_ref, p_ref, q_ref)

    @pl.when((ph > 0) & (ph <= DFT_NP))
    def _():
        k0 = (ph - 1) * DFT_KB
        for half in range(DFT_KB // DFT_G):
            x, tr, ti = _mid_forward(q_ref, k0, tr_ref, ti_ref, fh_ref[...], fl_ref[...], half)
            xr, xi = x[:FN2], x[FN2:]
            kr = _lane_cat([kh_ref[0, half * DFT_G + g] for g in range(DFT_G)])
            ki = _lane_cat([kh_ref[1, half * DFT_G + g] for g in range(DFT_G)])
            y = jnp.concatenate([xr * kr - xi * ki, xr * ki + xi * kr], axis=0).astype(BF16)
            c = _dot2c(gh_ref[...], gl_ref[...], y)
            cr, ci = c[:FN2], c[FN2:]
            dr = cr * tr + ci * ti
            di = ci * tr - cr * ti
            for g in range(DFT_G):
                lanes = slice(g * LANES, (g + 1) * LANES)
                q_ref[k0 + half * DFT_G + g] = dr[:, lanes]
                q_ref[FN1 + k0 + half * DFT_G + g] = di[:, lanes]

    @pl.when(ph == DFT_NP + 1)
    def _():
        bias = bias_ref[...]
        p_ref[...] = jnp.swapaxes(q_ref[0:FN1], 0, 1)
        _lead_stage(lambda r: p_ref[r].astype(BF16), oh_ref[:, 0:FN1], ol_ref[:, 0:FN1], q_ref)
        p_ref[...] = jnp.swapaxes(q_ref[FN1:2 * FN1], 0, 1)
        oh2, ol2 = oh_ref[:, FN1:2 * FN1], ol_ref[:, FN1:2 * FN1]

        def group(rg, _):
            r0 = rg * DFT_G
            rhs = _lane_cat([p_ref[r0 + g].astype(BF16) for g in range(DFT_G)])
            blk = _dot2c(oh2, ol2, rhs)
            for g in range(DFT_G):
                r = r0 + g
                y = q_ref[r] + blk[:, g * LANES:(g + 1) * LANES]
                q_ref[r] = (y + both(u_ref, r).astype(F32) * bias) * both(x2_ref, r).astype(F32)
            return 0

        lax.fori_loop(0, FN2 // DFT_G, group, 0, unroll=DFT_UNROLL)
        p_ref[...] = jnp.swapaxes(q_ref[0:FN2], 0, 1)
        nj = SEQ // FN2
        for b in range(2):
            o_ref[b] = p_ref[b * nj:(b + 1) * nj].reshape(SEQ, LANES).astype(BF16)


def _hyena_conv(u_rj, x2_rj, khat, twr, twi, lead, fwd, inv, out, bias):
    const = lambda c, ph: (0, 0)
    nj = SEQ // FN2
    tw = pl.BlockSpec((DFT_KB, FN2), lambda c, ph: (_mid_index(ph), 0))
    sig = pl.BlockSpec((2, FN2, nj, LANES), lambda c, ph: (0, 0, 0, c), pipeline_mode=pl.Buffered(1))
    mats = [m for pair in (lead, fwd, inv, out) for m in pair]
    return pl.pallas_call(
        _hconv_body,
        grid=(HY_W // LANES, DFT_NP + 2),
        in_specs=[sig, sig,
                  pl.BlockSpec((2, DFT_KB, FN2, LANES), lambda c, ph: (0, _mid_index(ph), 0, c)),
                  tw, tw] + [pl.BlockSpec(m.shape, const) for m in mats]
                 + [pl.BlockSpec((1, LANES), lambda c, ph: (0, c))],
        out_specs=pl.BlockSpec((2, SEQ, LANES), lambda c, ph: (0, 0, c)),
        out_shape=jax.ShapeDtypeStruct((2, SEQ, HY_W), BF16),
        scratch_shapes=[pltpu.VMEM((FN2, FN1, LANES), F32), pltpu.VMEM((2 * FN1, FN2, LANES), F32)],
        compiler_params=_cparams(("parallel", "arbitrary")),
        name="hyena_conv",
    )(u_rj, x2_rj, khat, twr, twi, *mats, bias)


OP_TM = 512


def _outproj_body(att_ref, hy_ref, x_ref, w_ref, g1_ref, n2_ref, sh_ref, sc_ref, wr_ref,
                  x1_ref, h2_ref, lg_ref):
    a = jnp.concatenate([att_ref[0], hy_ref[0]], axis=1)
    x1 = x_ref[0] + g1_ref[0] * _dot(a, w_ref[...])
    x1_ref[0] = x1
    h2 = _rms_mod(x1, n2_ref[...], sh_ref[0], sc_ref[0])
    hh, hl = _split(h2)
    h2_ref[0] = hh
    wr = wr_ref[...]
    both = _dot(wr, hh, _NT)
    lg_ref[0] = both[0:NE] + both[NE:2 * NE] + _dot(wr[0:NE], hl, _NT)


def _out_projection(att, hy, x, w_out_bf, g1r, n2g, sh2, sc2, wr2):
    b = x.shape[0]
    tok = lambda bi, i: (bi, i, 0)
    row = lambda bi, i: (bi, 0, 0)
    const = lambda bi, i: (0, 0)
    return pl.pallas_call(
        _outproj_body,
        grid=(b, SEQ // OP_TM),
        in_specs=[pl.BlockSpec((1, OP_TM, ATT_W), tok),
                  pl.BlockSpec((1, OP_TM, HY_W), tok),
                  pl.BlockSpec((1, OP_TM, D), tok),
                  pl.BlockSpec((ATT_W + HY_W, D), const),
                  pl.BlockSpec((1, 1, D), row),
                  pl.BlockSpec((1, D), const),
                  pl.BlockSpec((1, 1, D), row),
                  pl.BlockSpec((1, 1, D), row),
                  pl.BlockSpec((2 * NE, D), const)],
        out_specs=[pl.BlockSpec((1, OP_TM, D), tok),
                   pl.BlockSpec((1, OP_TM, D), tok),
                   pl.BlockSpec((1, NE, OP_TM), lambda bi, i: (bi, 0, i))],
        out_shape=[jax.ShapeDtypeStruct((b, SEQ, D), F32),
                   jax.ShapeDtypeStruct((b, SEQ, D), BF16),
                   jax.ShapeDtypeStruct((b, NE, SEQ), F32)],
        compiler_params=_cparams(("parallel", "parallel")),
        name="out_projection",
    )(att, hy, x, w_out_bf, g1r, n2g, sh2, sc2, wr2)


def _routing_body(lg_ref, tri_ref, pos_ref, gate_ref, off_ref, cs_ref):
    lg = lg_ref[0]
    e = jnp.exp(lg - jnp.max(lg, axis=0, keepdims=True))
    aff = e / jnp.sum(e, axis=0, keepdims=True)
    gate_ref[0] = aff
    def count_ge(t):
        return jnp.sum(jnp.where(aff >= t, 1.0, 0.0), axis=1, keepdims=True)

    def bisect(i, thr):
        cand = thr | (jnp.int32(1) << (30 - i))
        return jnp.where(count_ge(pltpu.bitcast(cand, F32)) >= float(CAP), cand, thr)

    thr = lax.fori_loop(0, 31, bisect, jnp.zeros((NE, 1), I32))
    lo = pltpu.bitcast(thr, F32)
    hi = jnp.maximum(pltpu.bitcast(thr + 1, F32), jnp.finfo(F32).tiny)

    def refine(i, c):
        lo, hi = c
        mid = lo + (hi - lo) * 0.5
        ok = count_ge(mid) >= float(CAP)
        return jnp.where(ok, mid, lo), jnp.where(ok, hi, mid)

    lo, hi = lax.fori_loop(0, 32, refine, (lo, hi))
    gt = aff >= hi
    eq = (aff >= lo) & jnp.logical_not(gt)
    need = float(CAP) - jnp.sum(jnp.where(gt, 1.0, 0.0), axis=1, keepdims=True)
    tri = tri_ref[...]

    def excl_cumsum(mask_f, record_offsets):
        carry = jnp.zeros((NE, 1), F32)
        for c in range(NTCH):
            sl = slice(c * TCH, (c + 1) * TCH)
            m = mask_f[:, sl]
            inc = _dot(m.astype(BF16), tri)
            cs_ref[:, sl] = inc - m + carry
            if record_offsets:
                off_ref[0, :, c:c + 1] = carry.astype(I32)
            carry = carry + inc[:, TCH - 1:TCH]
        return cs_ref[...]

    eq_rank = excl_cumsum(jnp.where(eq, 1.0, 0.0), False)
    sel = gt | (eq & (eq_rank < need))
    pos = excl_cumsum(jnp.where(sel, 1.0, 0.0), True)
    pos_ref[0] = jnp.where(sel, pos.astype(I32), -1)


def _routing(logits, tri):
    b = logits.shape[0]
    blk = pl.BlockSpec((1, NE, SEQ), lambda bi: (bi, 0, 0))
    return pl.pallas_call(
        _routing_body,
        grid=(b,),
        in_specs=[blk, pl.BlockSpec((TCH, TCH), lambda bi: (0, 0))],
        out_specs=[blk, blk, pl.BlockSpec((1, NE, NTCH), lambda bi: (bi, 0, 0))],
        out_shape=[jax.ShapeDtypeStruct((b, NE, SEQ), I32),
                   jax.ShapeDtypeStruct((b, NE, SEQ), F32),
                   jax.ShapeDtypeStruct((b, NE, NTCH), I32)],
        scratch_shapes=[pltpu.VMEM((NE, SEQ), F32)],
        compiler_params=_cparams(("parallel",)),
        name="routing",
    )(logits, tri)


GATHER_UNROLL = 8


def _gather_body(off_ref, h_ref, pos_ref, xg_ref, acc_ref):
    b = pl.program_id(0)
    e = pl.program_id(1)
    acc_ref[...] = jnp.zeros_like(acc_ref)
    crow = lax.broadcasted_iota(I32, (GW, TCH), 0)

    def chunks(i, _):
        for j in range(GATHER_UNROLL):
            c = i * GATHER_UNROLL + j
            off = off_ref[(b * NE + e) * NTCH + c]
            base = pl.multiple_of(jnp.minimum((off >> 3) << 3, CAP - TCH), SUBLANES)
            t0 = pl.multiple_of(c * TCH, TCH)
            rel = pos_ref[0, 0, pl.ds(c, 1), :] - base
            onehot = jnp.where(crow == rel, 1.0, 0.0).astype(BF16)
            acc_ref[pl.ds(base, GW), :] += _dot(onehot, h_ref[0, pl.ds(t0, TCH), :])
        return 0

    lax.fori_loop(0, NTCH // GATHER_UNROLL, chunks, 0)
    xg_ref[0, 0] = acc_ref[0:CAP, :].astype(BF16)


def _gather(offs_flat, h2, pos4):
    b = h2.shape[0]
    grid_spec = pltpu.PrefetchScalarGridSpec(
        num_scalar_prefetch=1,
        grid=(b, NE),
        in_specs=[pl.BlockSpec((1, SEQ, D), lambda bi, e, off: (bi, 0, 0)),
                  pl.BlockSpec((1, 1, NTCH, TCH), lambda bi, e, off: (bi, e, 0, 0))],
        out_specs=pl.BlockSpec((1, 1, CAP, D), lambda bi, e, off: (bi, e, 0, 0)),
        scratch_shapes=[pltpu.VMEM((CAP + SUBLANES, D), F32)],
    )
    return pl.pallas_call(
        _gather_body,
        grid_spec=grid_spec,
        out_shape=jax.ShapeDtypeStruct((b, NE, CAP, D), BF16),
        compiler_params=_cparams(("parallel", "arbitrary")),
        name="moe_gather",
    )(offs_flat, h2, pos4)


FFN_TM = 512
FFN_NF = 4
FFN_FC = DEXP // FFN_NF
assert FFN_FC * FFN_NF == DEXP and FFN_FC % BF16_ROWS == 0


def _ffn_body(xg_ref, wgt_ref, wut_ref, wd_ref, y_ref, acc_ref):
    j = pl.program_id(1)
    nb = xg_ref.shape[0]

    @pl.when(j == 0)
    def _():
        acc_ref[...] = jnp.zeros_like(acc_ref)

    wgt = wgt_ref[0].astype(BF16)
    wut = wut_ref[0].astype(BF16)
    wd = wd_ref[0].astype(BF16)
    for b in range(nb):
        for mb in range(CAP // FFN_TM):
            rows = slice(mb * FFN_TM, (mb + 1) * FFN_TM)
            xb = xg_ref[b, 0, rows, :]
            a = _dot(xb, wgt, _NT)
            u = _dot(xb, wut, _NT)
            h = (a * (1.0 / (1.0 + jnp.exp(-a))) * u).astype(BF16)
            acc_ref[b, rows, :] += _dot(h, wd)

    @pl.when(j == FFN_NF - 1)
    def _():
        for b in range(nb):
            y_ref[b, 0, 0:CAP, :] = acc_ref[b].astype(BF16)
            y_ref[b, 0, CAP:YROWS, :] = jnp.zeros((YROWS - CAP, D), BF16)


def _expert_ffn(xg, w_gate_t, w_up_t, w_down):
    b = xg.shape[0]
    wblk = pl.BlockSpec((1, FFN_FC, D), lambda e, j: (e, j, 0))
    return pl.pallas_call(
        _ffn_body,
        grid=(NE, FFN_NF),
        in_specs=[pl.BlockSpec((b, 1, CAP, D), lambda e, j: (0, e, 0, 0)), wblk, wblk, wblk],
        out_specs=pl.BlockSpec((b, 1, YROWS, D), lambda e, j: (0, e, 0, 0)),
        out_shape=jax.ShapeDtypeStruct((b, NE, YROWS, D), BF16),
        scratch_shapes=[pltpu.VMEM((b, CAP, D), F32)],
        compiler_params=_cparams(("parallel", "arbitrary")),
        name="expert_ffn",
    )(xg, w_gate_t, w_up_t, w_down)


def _combine_body(off_ref, y_ref, pos_ref, gate_ref, x1_ref, g2_ref, o_ref):
    b = pl.program_id(0)
    i = pl.program_id(1)
    crow = lax.broadcasted_iota(I32, (CW, TCH), 0)
    acc = jnp.zeros((TCH, D), F32)
    for e in range(NE):
        off = off_ref[(b * NE + e) * NTCH + i]
        base = pl.multiple_of(jnp.minimum((off >> 4) << 4, CAP - TCH), BF16_ROWS)
        rel = pos_ref[0, e:e + 1, :] - base
        w = jnp.where(crow == rel, gate_ref[0, e:e + 1, :], 0.0)
        yw = y_ref[0, e, pl.ds(base, CW), :]
        acc = acc + _dot(w.astype(BF16), yw, _TN)
    o_ref[0] = x1_ref[0] + g2_ref[0] * acc


def _combine(offs_flat, y, pos, gate, x1, g2r):
    b = x1.shape[0]
    grid_spec = pltpu.PrefetchScalarGridSpec(
        num_scalar_prefetch=1,
        grid=(b, NTCH),
        in_specs=[pl.BlockSpec((1, NE, YROWS, D), lambda bi, i, off: (bi, 0, 0, 0),
                               pipeline_mode=pl.Buffered(1)),
                  pl.BlockSpec((1, NE, TCH), lambda bi, i, off: (bi, 0, i)),
                  pl.BlockSpec((1, NE, TCH), lambda bi, i, off: (bi, 0, i)),
                  pl.BlockSpec((1, TCH, D), lambda bi, i, off: (bi, i, 0)),
                  pl.BlockSpec((1, 1, D), lambda bi, i, off: (bi, 0, 0))],
        out_specs=pl.BlockSpec((1, TCH, D), lambda bi, i, off: (bi, i, 0)),
    )
    return pl.pallas_call(
        _combine_body,
        grid_spec=grid_spec,
        out_shape=jax.ShapeDtypeStruct((b, SEQ, D), F32),
        compiler_params=_cparams(("parallel", "arbitrary")),
        name="moe_combine",
    )(offs_flat, y, pos, gate, x1, g2r)


def _np_split(m):
    hi = np.asarray(m, np.float64).astype(BF16)
    lo = (m - hi.astype(np.float64)).astype(BF16)
    return jnp.asarray(hi), jnp.asarray(lo)


@functools.lru_cache(maxsize=None)
def _dft_tables():
    a = np.arange(FN1, dtype=np.float64)
    ang = 2.0 * np.pi * np.outer(a, a) / FN1
    fr, fi = np.cos(ang), -np.sin(ang)
    half = SEQ // FN2
    lead_u = np.block([[fr[:, :half], -fi[:, :half]], [fi[:, :half], fr[:, :half]]])
    lead_k = np.concatenate([fr, fi], axis=0)
    fwd = np.block([[fr, -fi], [fi, fr]])
    inv = np.block([[fr, fi], [-fi, fr]])
    out = np.block([[fr[:half], fi[:half]], [-fi[:half], fr[:half]]])
    n2 = np.arange(FN2, dtype=np.float64)
    tw = 2.0 * np.pi * np.outer(a, n2) / FN
    twr = np.cos(tw).astype(np.float32)
    twi = (-np.sin(tw)).astype(np.float32)
    return dict(lead_u=lead_u, lead_k=lead_k, fwd=fwd, inv=inv, out=out, twr=twr, twi=twi)


@functools.lru_cache(maxsize=None)
def _filter_tables():
    L = SEQ
    n = np.arange(FN).reshape(FN1, FN2).T.reshape(-1)
    lag = np.where(n < L, n, FN - n)
    jc = np.minimum(lag, L - 1).astype(np.float64)
    t = (jc / (L - 1))[:, None]
    bands = (FEMB - 1) // 2
    w = 2.0 * np.pi * jc / L
    f = np.linspace(1e-4, bands - 1, bands)
    fw = w[:, None] * f[None, :]
    z = np.concatenate([t, np.cos(fw), -np.sin(fw), np.zeros((FN, FORD - FEMB))], axis=-1)
    mask = np.where(n == L, 0.0, 1.0)[:, None]
    fwd = np.where(n < L, 1.0, 0.0)[:, None]
    max_decay = math.log(DECAY_TARGET) / FAST_DECAY_PCT
    min_decay = math.log(DECAY_TARGET) / SLOW_DECAY_PCT
    negdelta = -np.abs(np.linspace(min_decay, max_decay, HY_W))[None, :]
    return tuple(np.asarray(a, np.float32) for a in (z, t, mask, fwd, negdelta))


@functools.lru_cache(maxsize=None)
def _rope_tables(n):
    rows = n // GRID_W
    row_id, col_id = np.meshgrid(np.arange(rows, dtype=np.float64), np.arange(GRID_W, dtype=np.float64), indexing="ij")
    quarter = HD // 4
    inv_freq = ROPE_THETA ** (-np.arange(quarter, dtype=np.float64) / quarter)
    ar = row_id.reshape(-1)[:, None] * inv_freq
    ac = col_id.reshape(-1)[:, None] * inv_freq
    cos = np.concatenate([np.cos(ar), np.cos(ar), np.cos(ac), np.cos(ac)], axis=-1)
    sin = np.concatenate([-np.sin(ar), np.sin(ar), -np.sin(ac), np.sin(ac)], axis=-1)
    reps = (1, LANES // HD)
    return np.tile(cos, reps).astype(np.float32), np.tile(sin, reps).astype(np.float32)


def _hyena_long_conv(u_rj, x2_rj, kern, abs_sum, bias):
    tb = _dft_tables()
    twr, twi = tb["twr"], tb["twi"]
    fwd = _np_split(tb["fwd"])
    scale = 1.0 / (abs_sum * float(FN))
    khat = _filter_spectrum(kern.reshape(FN2, FN1, HY_W), twr, twi, *_np_split(tb["lead_k"]), *fwd, scale)
    return _hyena_conv(u_rj, x2_rj, khat, twr, twi, _np_split(tb["lead_u"]), fwd, _np_split(tb["inv"]),
                       _np_split(tb["out"]), bias.reshape(1, HY_W))


def kernel(x, c, ctx, c_ctx, w_mod, b_mod, norm1_g, norm2_g, w_in, w_out, q_norm_g, k_norm_g,
           conv_w, conv_b, filt_w1, filt_b1, filt_w2, filt_b2, filt_w3, filt_freq, hyena_bias,
           w_router, w_gate, w_up, w_down):
    B = x.shape[0]
    assert x.shape == (B, SEQ, D) and B == 2 and ctx.shape == (B, CTX, D) and w_mod.shape[0] == 1
    l = 0

    cc = jnp.concatenate([c, c_ctx[None, :], jnp.zeros((SUBLANES - B - 1, D), F32)], axis=0)
    mod = _modulation(cc, w_mod[l], b_mod[l][None, :])
    sh1, sc1, g1, sh2, sc2, g2 = [mod[:, i * D:(i + 1) * D] for i in range(6)]
    lat = lambda m: m[:B, None, :]
    ctxrow = lambda m: jnp.broadcast_to(m[B:B + 1, None, :], (B, 1, D))

    w_in_bf = w_in[l].astype(BF16)
    gq2 = jnp.tile(q_norm_g[l][None, :], (1, LANES // HD))
    gk2 = jnp.tile(k_norm_g[l][None, :], (1, LANES // HD))
    bd = jnp.asarray(np.kron(np.eye(2 * LANES // HD), np.full((HD, HD), 1.0 / HD)), BF16)
    cos_t, sin_t = _rope_tables(SEQ)
    n1g = norm1_g[l][None, :]

    q, k, vt, p = _in_projection(x, n1g, lat(sh1), lat(sc1), w_in_bf, gq2, gk2, bd, cos_t, sin_t, 512)
    _, kc, vct, _ = _in_projection(ctx, n1g, ctxrow(sh1), ctxrow(sc1), w_in_bf, gq2, gk2, bd,
                                   jnp.ones((CTX, LANES), F32), jnp.zeros((CTX, LANES), F32), CTX)

    kch = jnp.concatenate([k, kc], axis=2).reshape(B, NKV, ATT_NCH, ATT_TK, HD)
    vt_all = jnp.concatenate([vt, vct], axis=3).reshape(B, NKV, HD, ATT_NCH, ATT_TK)
    ones_pad = jnp.concatenate([jnp.ones((B, NKV, ATT_NCH, 1, ATT_TK), BF16),
                                jnp.zeros((B, NKV, ATT_NCH, BF16_ROWS - 1, ATT_TK), BF16)], axis=3)
    vtch = jnp.concatenate([vt_all.transpose(0, 1, 3, 2, 4), ones_pad], axis=3)
    att = _attention(q, kch, vtch)

    cw9 = conv_w[l].reshape(3, 3, HY_W).reshape(9, HY_W)
    cb3 = conv_b[l].reshape(3, HY_W)
    u_rj, x2_rj = _short_conv(p, cw9, cb3)
    ztab, ttab, mtab, ftab, negdelta = _filter_tables()
    w1p = jnp.concatenate([filt_w1[l], jnp.zeros((FORD - FEMB, FORD), F32)], axis=0)
    kern, abs_sum = _implicit_filter(ztab, ttab, mtab, ftab, w1p, filt_b1[l][None, :], filt_w2[l],
                                     filt_b2[l][None, :], filt_w3[l], filt_freq[l][None, :], negdelta)
    hy = _hyena_long_conv(u_rj, x2_rj, kern, abs_sum, hyena_bias[l])

    wr2 = jnp.concatenate(_split(w_router[l].T), axis=0)
    x1, h2, logits = _out_projection(att, hy, x, w_out[l].astype(BF16), lat(g1), norm2_g[l][None, :],
                                     lat(sh2), lat(sc2), wr2)

    tri = jnp.asarray(np.triu(np.ones((TCH, TCH))), BF16)
    pos, gate, offs = _routing(logits, tri)
    offs_flat = offs.reshape(-1)
    xg = _gather(offs_flat, h2, pos.reshape(B, NE, NTCH, TCH))
    y = _expert_ffn(xg, jnp.swapaxes(w_gate[l], 1, 2), jnp.swapaxes(w_up[l], 1, 2), w_down[l])
    return _combine(offs_flat, y, pos, gate, x1, lat(g2))
```

```python
import functools
import math

import numpy as np
import jax
import jax.numpy as jnp
from jax import lax
from jax.experimental import pallas as pl
from jax.experimental.pallas import tpu as pltpu

F32 = jnp.float32
BF16 = jnp.bfloat16
I32 = jnp.int32

D = 1024
SEQ = 8192
CTX = 256
GRID_W = 64
ATT_W = 512
HY_W = 512
HD = 64
NQ = 8
NKV = 2
QPK = NQ // NKV
KV_W = NKV * HD
IN_W = ATT_W + 2 * KV_W + 3 * HY_W
FEMB = 33
FORD = 64
NE = 16
CAP = 2 * SEQ // NE
DEXP = 2752
ROPE_THETA = 10000.0
EPS = 1e-6
DECAY_TARGET = 1e-2
FAST_DECAY_PCT = 0.3
SLOW_DECAY_PCT = 1.5

LANES = 128
SUBLANES = 8
BF16_ROWS = 16
VMEM_BYTES_V7X = 64 * 1024 * 1024
VMEM_LIMIT = VMEM_BYTES_V7X - 8 * 1024 * 1024

FN = 2 * SEQ
FN1 = 128
FN2 = 128

TCH = LANES
NTCH = SEQ // TCH
GW = TCH + SUBLANES
CW = TCH + BF16_ROWS
YROWS = CAP + BF16_ROWS


def _cparams(sem, vmem=None):
    return pltpu.CompilerParams(dimension_semantics=sem, vmem_limit_bytes=vmem or VMEM_LIMIT)


def _split(a):
    hi = a.astype(BF16)
    lo = (a - hi.astype(F32)).astype(BF16)
    return hi, lo


_NN = (((1,), (0,)), ((), ()))
_NT = (((1,), (1,)), ((), ()))
_TN = (((0,), (0,)), ((), ()))


def _dot(a, b, dn=_NN):
    return lax.dot_general(a, b, dn, preferred_element_type=F32)


def _dot3(a, b, dn=_NN):
    ah, al = _split(a)
    bh, bl = _split(b)
    return _dot(ah, bh, dn) + _dot(ah, bl, dn) + _dot(al, bh, dn)


def _mod_body(c_ref, w_ref, b_ref, o_ref):
    c = c_ref[...]
    s = c * (1.0 / (1.0 + jnp.exp(-c)))
    o_ref[...] = _dot3(s, w_ref[...]) + b_ref[...]


def _modulation(cc, w_mod, b_mod):
    n = w_mod.shape[1]
    return pl.pallas_call(
        _mod_body,
        grid=(n // D,),
        in_specs=[pl.BlockSpec((SUBLANES, D), lambda j: (0, 0)),
                  pl.BlockSpec((D, D), lambda j: (0, j)),
                  pl.BlockSpec((1, D), lambda j: (0, j))],
        out_specs=pl.BlockSpec((SUBLANES, D), lambda j: (0, j)),
        out_shape=jax.ShapeDtypeStruct((SUBLANES, n), F32),
        compiler_params=_cparams(("arbitrary",)),
        name="modulation",
    )(cc, w_mod, b_mod)


Q_SCALE = HD ** -0.5 * math.log2(math.e)


def _rms_mod(x, g, sh, sc):
    ms = jnp.mean(x * x, axis=-1, keepdims=True)
    return (x * lax.rsqrt(ms + EPS) * g) * (1.0 + sc) + sh


def _head_mean_square(t, bd):
    hi, lo = _split(t * t)
    return _dot(hi, bd) + _dot(lo, bd)


def _head_norm_rope(t, ms, g, cos, sin):
    tn = t * lax.rsqrt(ms + EPS) * g
    lane = lax.broadcasted_iota(I32, tn.shape, 1)
    sw = jnp.where((lane & 31) < 16, pltpu.roll(tn, LANES - 16, 1), pltpu.roll(tn, 16, 1))
    return tn * cos + sw * sin


def _proj_body(x_ref, g_ref, sh_ref, sc_ref, w_ref, gq_ref, gk_ref, bd_ref, cos_ref, sin_ref,
               q_ref, k_ref, v_ref, p_ref):
    h = _rms_mod(x_ref[0], g_ref[...], sh_ref[0], sc_ref[0])
    proj = _dot(h.astype(BF16), w_ref[...])
    bd = bd_ref[...]
    cos = cos_ref[...]
    sin = sin_ref[...]
    wide = 2 * LANES
    for j in range(ATT_W // wide):
        ms = _head_mean_square(proj[:, j * wide:(j + 1) * wide], bd)
        for i in range(2):
            sl = slice(j * wide + i * LANES, j * wide + (i + 1) * LANES)
            qj = _head_norm_rope(proj[:, sl], ms[:, i * LANES:(i + 1) * LANES], gq_ref[...], cos, sin)
            q_ref[0, :, sl] = (qj * Q_SCALE).astype(BF16)
    ms = _head_mean_square(proj[:, ATT_W:ATT_W + 2 * KV_W], bd)
    kk = _head_norm_rope(proj[:, ATT_W:ATT_W + KV_W], ms[:, 0:KV_W], gk_ref[...], cos, sin)
    vt = proj[:, ATT_W + KV_W:ATT_W + 2 * KV_W].T
    for g in range(NKV):
        k_ref[0, g] = kk[:, g * HD:(g + 1) * HD].astype(BF16)
        v_ref[0, g] = vt[g * HD:(g + 1) * HD, :].astype(BF16)
    p_ref[0] = proj[:, ATT_W + 2 * KV_W:]


def _in_projection(x, g1, sh, sc, w_in_bf, gq2, gk2, bd, cos_t, sin_t, tm):
    b, s, _ = x.shape
    row = lambda bi, i: (bi, 0, 0)
    tok = lambda bi, i: (bi, i, 0)
    const = lambda bi, i: (0, 0)
    return pl.pallas_call(
        _proj_body,
        grid=(b, s // tm),
        in_specs=[pl.BlockSpec((1, tm, D), tok),
                  pl.BlockSpec((1, D), const),
                  pl.BlockSpec((1, 1, D), row),
                  pl.BlockSpec((1, 1, D), row),
                  pl.BlockSpec((D, IN_W), const),
                  pl.BlockSpec((1, LANES), const),
                  pl.BlockSpec((1, LANES), const),
                  pl.BlockSpec((2 * LANES, 2 * LANES), const),
                  pl.BlockSpec((tm, LANES), lambda bi, i: (i, 0)),
                  pl.BlockSpec((tm, LANES), lambda bi, i: (i, 0))],
        out_specs=[pl.BlockSpec((1, tm, ATT_W), tok),
                   pl.BlockSpec((1, NKV, tm, HD), lambda bi, i: (bi, 0, i, 0)),
                   pl.BlockSpec((1, NKV, HD, tm), lambda bi, i: (bi, 0, 0, i)),
                   pl.BlockSpec((1, tm, 3 * HY_W), tok)],
        out_shape=[jax.ShapeDtypeStruct((b, s, ATT_W), BF16),
                   jax.ShapeDtypeStruct((b, NKV, s, HD), BF16),
                   jax.ShapeDtypeStruct((b, NKV, HD, s), BF16),
                   jax.ShapeDtypeStruct((b, s, 3 * HY_W), F32)],
        compiler_params=_cparams(("parallel", "parallel")),
        name="in_projection",
    )(x, g1, sh, sc, w_in_bf, gq2, gk2, bd, cos_t, sin_t)


ATT_TQ = 512
ATT_TK = 768
SK = SEQ + CTX
ATT_NCH = SK // ATT_TK


ATT_NQ = QPK * ATT_TQ
ATT_VR = HD + BF16_ROWS
assert ATT_NCH % 2 == 1
ATT_SHIFT_MAX = 120.0


def _attn_body(bound_ref, q_ref, k_ref, vt_ref, o_ref, s_ref, mx_ref, m_ref, acc_ref):
    qall = jnp.concatenate([q_ref[0, :, r * HD:(r + 1) * HD] for r in range(QPK)], axis=0)
    acc_ref[...] = jnp.zeros_like(acc_ref)
    bound = bound_ref[0]
    fixed_shift = 2.0 * bound <= ATT_SHIFT_MAX

    def finish():
        out = acc_ref[0:HD, :] * (1.0 / acc_ref[HD:HD + 1, :])
        for r in range(QPK):
            o_ref[0, :, r * HD:(r + 1) * HD] = out[:, r * ATT_TQ:(r + 1) * ATT_TQ].T.astype(BF16)

    @pl.when(fixed_shift)
    def _():
        def chunk(c, _):
            s = _dot(k_ref[0, 0, c], qall, _NT)
            acc_ref[...] += _dot(vt_ref[0, 0, c], jnp.exp2(s - bound).astype(BF16))
            return 0

        lax.fori_loop(0, ATT_NCH, chunk, 0)
        finish()

    @pl.when(jnp.logical_not(fixed_shift))
    def _():
        m_ref[...] = jnp.full(m_ref.shape, -1e30, F32)

        def scores(c, slot):
            s = _dot(k_ref[0, 0, c], qall, _NT)
            s_ref[slot] = s
            mx_ref[slot] = jnp.max(s, axis=0, keepdims=True)

        def update(c, slot):
            m_old = m_ref[...]
            m_new = jnp.maximum(m_old, mx_ref[slot])
            p = jnp.exp2(s_ref[slot] - m_new).astype(BF16)
            acc_ref[...] = jnp.exp2(m_old - m_new) * acc_ref[...] + _dot(vt_ref[0, 0, c], p)
            m_ref[...] = m_new

        scores(0, 0)

        def pair(i, _):
            c = 2 * i
            scores(c + 1, 1)
            update(c, 0)
            scores(c + 2, 0)
            update(c + 1, 1)
            return 0

        lax.fori_loop(0, ATT_NCH // 2, pair, 0)
        update(ATT_NCH - 1, 0)
        finish()


def _attention(bound, q, kch, vtch):
    b = q.shape[0]
    grid_spec = pltpu.PrefetchScalarGridSpec(
        num_scalar_prefetch=1,
        grid=(b, NKV, SEQ // ATT_TQ),
        in_specs=[pl.BlockSpec((1, ATT_TQ, QPK * HD), lambda bi, g, i, bd: (bi, i, g)),
                  pl.BlockSpec((1, 1, ATT_NCH, ATT_TK, HD), lambda bi, g, i, bd: (bi, g, 0, 0, 0)),
                  pl.BlockSpec((1, 1, ATT_NCH, ATT_VR, ATT_TK), lambda bi, g, i, bd: (bi, g, 0, 0, 0))],
        out_specs=pl.BlockSpec((1, ATT_TQ, QPK * HD), lambda bi, g, i, bd: (bi, i, g)),
        scratch_shapes=[pltpu.VMEM((2, ATT_TK, ATT_NQ), F32), pltpu.VMEM((2, 1, ATT_NQ), F32),
                        pltpu.VMEM((1, ATT_NQ), F32), pltpu.VMEM((ATT_VR, ATT_NQ), F32)],
    )
    return pl.pallas_call(
        _attn_body,
        grid_spec=grid_spec,
        out_shape=jax.ShapeDtypeStruct((b, SEQ, ATT_W), BF16),
        compiler_params=_cparams(("parallel", "parallel", "parallel")),
        name="attention",
    )(bound, q, kch, vtch)


SC_TM = 2048
SC_J = SC_TM // FN2


def _sconv_body(m1, a1, n1, m2, a2, n2, m3, a3, n3, w_ref, b_ref, u_ref, x2_ref):
    i = pl.program_id(1)
    last = pl.num_programs(1) - 1
    rows = lax.broadcasted_iota(I32, (SC_TM, HY_W), 0)

    def conv(main, prev, nxt, g):
        x = main[0]
        pr = jnp.where(i > 0, prev[0, SUBLANES - 1:SUBLANES, :], 0.0)
        nx = jnp.where(i < last, nxt[0, 0:1, :], 0.0)
        xm = jnp.where(rows == 0, pr, pltpu.roll(x, 1, 0))
        xp = jnp.where(rows == SC_TM - 1, nx, pltpu.roll(x, SC_TM - 1, 0))
        return (w_ref[g:g + 1, :] * xm + w_ref[3 + g:4 + g, :] * x + w_ref[6 + g:7 + g, :] * xp
                + b_ref[g:g + 1, :])

    def to_rj(t):
        return jnp.swapaxes(t.reshape(SC_J, FN2, HY_W), 0, 1).astype(BF16)

    x1 = conv(m1, a1, n1, 0)
    x2 = conv(m2, a2, n2, 1)
    v = conv(m3, a3, n3, 2)
    u_ref[0] = to_rj(v * x1)
    x2_ref[0] = to_rj(x2)


def _short_conv(p, cw9, cb3):
    b = p.shape[0]
    nblk8 = SEQ // SUBLANES
    step8 = SC_TM // SUBLANES
    specs = []
    for g in range(3):
        specs += [pl.BlockSpec((1, SC_TM, HY_W), lambda bi, i, g=g: (bi, i, g)),
                  pl.BlockSpec((1, SUBLANES, HY_W), lambda bi, i, g=g: (bi, jnp.maximum(i * step8 - 1, 0), g)),
                  pl.BlockSpec((1, SUBLANES, HY_W), lambda bi, i, g=g: (bi, jnp.minimum((i + 1) * step8, nblk8 - 1), g))]
    specs += [pl.BlockSpec((9, HY_W), lambda bi, i: (0, 0)), pl.BlockSpec((3, HY_W), lambda bi, i: (0, 0))]
    out = pl.BlockSpec((1, FN2, SC_J, HY_W), lambda bi, i: (bi, 0, i, 0))
    return pl.pallas_call(
        _sconv_body,
        grid=(b, SEQ // SC_TM),
        in_specs=specs,
        out_specs=[out, out],
        out_shape=[jax.ShapeDtypeStruct((b, FN2, SEQ // FN2, HY_W), BF16)] * 2,
        compiler_params=_cparams(("parallel", "parallel")),
        name="short_conv",
    )(p, p, p, p, p, p, p, p, p, cw9, cb3)


FILT_TR = 1024


def _filter_body(z_ref, t_ref, msk_ref, fwd_ref, w1_ref, b1_ref, w2_ref, b2_ref, w3_ref, fr_ref, dl_ref,
                 k_ref, s_ref):
    fr = fr_ref[...]
    h = jnp.sin(fr * (_dot3(z_ref[...], w1_ref[...]) + b1_ref[...]))
    h = jnp.sin(fr * (_dot3(h, w2_ref[...]) + b2_ref[...]))
    h = _dot3(h, w3_ref[...])
    h = jnp.where(fwd_ref[...] > 0.5, h[:, :HY_W], h[:, HY_W:])
    kern = h * jnp.exp(t_ref[...] * dl_ref[...]) * msk_ref[...]
    k_ref[...] = kern

    @pl.when(pl.program_id(0) == 0)
    def _():
        s_ref[...] = jnp.zeros_like(s_ref)

    s_ref[...] += jnp.sum(jnp.abs(kern), axis=0, keepdims=True)


def _implicit_filter(ztab, ttab, mtab, ftab, w1p, b1, w2, b2, w3, freq, negdelta):
    rowblk = lambda i: (i, 0)
    const = lambda i: (0, 0)
    col = pl.BlockSpec((FILT_TR, 1), rowblk)
    return pl.pallas_call(
        _filter_body,
        grid=(FN // FILT_TR,),
        in_specs=[pl.BlockSpec((FILT_TR, FORD), rowblk), col, col, col,
                  pl.BlockSpec((FORD, FORD), const),
                  pl.BlockSpec((1, FORD), const),
                  pl.BlockSpec((FORD, FORD), const),
                  pl.BlockSpec((1, FORD), const),
                  pl.BlockSpec((FORD, 2 * HY_W), const),
                  pl.BlockSpec((1, FORD), const),
                  pl.BlockSpec((1, HY_W), const)],
        out_specs=[pl.BlockSpec((FILT_TR, HY_W), rowblk),
                   pl.BlockSpec((1, HY_W), const)],
        out_shape=[jax.ShapeDtypeStruct((FN, HY_W), F32), jax.ShapeDtypeStruct((1, HY_W), F32)],
        compiler_params=_cparams(("arbitrary",)),
        name="implicit_filter",
    )(ztab, ttab, mtab, ftab, w1p, b1, w2, b2, w3, freq, negdelta)


DFT_G = 4
DFT_KB = 16
DFT_NP = FN1 // DFT_KB
DFT_UNROLL = 4


def _dot2c(fh, fl, zb):
    return _dot(fh, zb) + _dot(fl, zb)


def _lead_stage(src, fh, fl, dst_ref):
    def group(rg, _):
        r0 = rg * DFT_G
        rhs = jnp.concatenate([src(r0 + g) for g in range(DFT_G)], axis=1)
        blk = _dot2c(fh, fl, rhs)
        for g in range(DFT_G):
            dst_ref[r0 + g] = blk[:, g * LANES:(g + 1) * LANES]
        return 0

    lax.fori_loop(0, FN2 // DFT_G, group, 0, unroll=DFT_UNROLL)


def _lead_phase(src, lh_ref, ll_ref, p_ref, q_ref):
    for h in range(2):
        rows = slice(h * FN1, (h + 1) * FN1)
        _lead_stage(src, lh_ref[rows, :], ll_ref[rows, :], p_ref)
        q_ref[rows] = jnp.swapaxes(p_ref[...], 0, 1)


def _lane_cat(xs):
    return jnp.concatenate(xs, axis=1)


def _mid_forward(q_ref, k0, tr_ref, ti_ref, fh, fl, half):
    tr_t, ti_t = tr_ref[...].T, ti_ref[...].T
    brs, bis, trs, tis = [], [], [], []
    for g in range(DFT_G):
        jj = half * DFT_G + g
        ar, ai = q_ref[k0 + jj], q_ref[FN1 + k0 + jj]
        tr = jnp.broadcast_to(tr_t[:, jj:jj + 1], (FN2, LANES))
        ti = jnp.broadcast_to(ti_t[:, jj:jj + 1], (FN2, LANES))
        brs.append(ar * tr - ai * ti)
        bis.append(ar * ti + ai * tr)
        trs.append(tr)
        tis.append(ti)
    b = jnp.concatenate([_lane_cat(brs), _lane_cat(bis)], axis=0).astype(BF16)
    return _dot2c(fh, fl, b), _lane_cat(trs), _lane_cat(tis)


def _spectrum_body(k_ref, tr_ref, ti_ref, lh_ref, ll_ref, fh_ref, fl_ref, sc_ref, o_ref, p_ref, q_ref):
    ph = pl.program_id(1)

    @pl.when(ph == 0)
    def _():
        _lead_phase(lambda r: k_ref[r].astype(BF16), lh_ref, ll_ref, p_ref, q_ref)

    @pl.when(ph > 0)
    def _():
        k0 = (ph - 1) * DFT_KB
        sc = _lane_cat([sc_ref[...]] * DFT_G)
        for half in range(DFT_KB // DFT_G):
            x, _, _ = _mid_forward(q_ref, k0, tr_ref, ti_ref, fh_ref[...], fl_ref[...], half)
            x = x * sc
            for g in range(DFT_G):
                lanes = slice(g * LANES, (g + 1) * LANES)
                o_ref[0, half * DFT_G + g] = x[:FN2, lanes]
                o_ref[1, half * DFT_G + g] = x[FN2:, lanes]


def _mid_index(ph):
    return jnp.clip(ph - 1, 0, DFT_NP - 1)


def _filter_spectrum(kern_rj, twr, twi, lh, ll, fh, fl, scale):
    const = lambda c, ph: (0, 0)
    tw = pl.BlockSpec((DFT_KB, FN2), lambda c, ph: (_mid_index(ph), 0))
    return pl.pallas_call(
        _spectrum_body,
        grid=(HY_W // LANES, DFT_NP + 1),
        in_specs=[pl.BlockSpec((FN2, FN1, LANES), lambda c, ph: (0, 0, c), pipeline_mode=pl.Buffered(1)),
                  tw, tw,
                  pl.BlockSpec(lh.shape, const), pl.BlockSpec(ll.shape, const),
                  pl.BlockSpec(fh.shape, const), pl.BlockSpec(fl.shape, const),
                  pl.BlockSpec((1, LANES), lambda c, ph: (0, c))],
        out_specs=pl.BlockSpec((2, DFT_KB, FN2, LANES), lambda c, ph: (0, _mid_index(ph), 0, c)),
        out_shape=jax.ShapeDtypeStruct((2, FN1, FN2, HY_W), F32),
        scratch_shapes=[pltpu.VMEM((FN2, FN1, LANES), F32), pltpu.VMEM((2 * FN1, FN2, LANES), F32)],
        compiler_params=_cparams(("parallel", "arbitrary")),
        name="filter_spectrum",
    )(kern_rj, twr, twi, lh, ll, fh, fl, scale)


def _hconv_body(u_ref, x2_ref, kh_ref, tr_ref, ti_ref, lh_ref, ll_ref, fh_ref, fl_ref, gh_ref, gl_ref,
                oh_ref, ol_ref, bias_ref, o_ref, p_ref, q_ref):
    ph = pl.program_id(1)

    def both(ref, r):
        return jnp.concatenate([ref[0, r], ref[1, r]], axis=0)

    @pl.when(ph == 0)
    def _():
        _lead_phase(lambda r: both(u_ref, r), lh_ref, ll_ref, p_ref, q_ref)

    @pl.when((ph > 0) & (ph <= DFT_NP))
    def _():
        k0 = (ph - 1) * DFT_KB
        for half in range(DFT_KB // DFT_G):
            x, tr, ti = _mid_forward(q_ref, k0, tr_ref, ti_ref, fh_ref[...], fl_ref[...], half)
            xr, xi = x[:FN2], x[FN2:]
            kr = _lane_cat([kh_ref[0, half * DFT_G + g] for g in range(DFT_G)])
            ki = _lane_cat([kh_ref[1, half * DFT_G + g] for g in range(DFT_G)])
            y = jnp.concatenate([xr * kr - xi * ki, xr * ki + xi * kr], axis=0).astype(BF16)
            c = _dot2c(gh_ref[...], gl_ref[...], y)
            cr, ci = c[:FN2], c[FN2:]
            dr = cr * tr + ci * ti
            di = ci * tr - cr * ti
            for g in range(DFT_G):
                lanes = slice(g * LANES, (g + 1) * LANES)
                q_ref[k0 + half * DFT_G + g] = dr[:, lanes]
                q_ref[FN1 + k0 + half * DFT_G + g] = di[:, lanes]

    @pl.when(ph == DFT_NP + 1)
    def _():
        bias = bias_ref[...]
        p_ref[...] = jnp.swapaxes(q_ref[0:FN1], 0, 1)
        _lead_stage(lambda r: p_ref[r].astype(BF16), oh_ref[:, 0:FN1], ol_ref[:, 0:FN1], q_ref)
        p_ref[...] = jnp.swapaxes(q_ref[FN1:2 * FN1], 0, 1)
        oh2, ol2 = oh_ref[:, FN1:2 * FN1], ol_ref[:, FN1:2 * FN1]

        def group(rg, _):
            r0 = rg * DFT_G
            rhs = _lane_cat([p_ref[r0 + g].astype(BF16) for g in range(DFT_G)])
            blk = _dot2c(oh2, ol2, rhs)
            for g in range(DFT_G):
                r = r0 + g
                y = q_ref[r] + blk[:, g * LANES:(g + 1) * LANES]
                q_ref[r] = (y + both(u_ref, r).astype(F32) * bias) * both(x2_ref, r).astype(F32)
            return 0

        lax.fori_loop(0, FN2 // DFT_G, group, 0, unroll=DFT_UNROLL)
        p_ref[...] = jnp.swapaxes(q_ref[0:FN2], 0, 1)
        nj = SEQ // FN2
        for b in range(2):
            o_ref[b] = p_ref[b * nj:(b + 1) * nj].reshape(SEQ, LANES).astype(BF16)


def _hyena_conv(u_rj, x2_rj, khat, twr, twi, lead, fwd, inv, out, bias):
    const = lambda c, ph: (0, 0)
    nj = SEQ // FN2
    tw = pl.BlockSpec((DFT_KB, FN2), lambda c, ph: (_mid_index(ph), 0))
    sig = pl.BlockSpec((2, FN2, nj, LANES), lambda c, ph: (0, 0, 0, c), pipeline_mode=pl.Buffered(1))
    mats = [m for pair in (lead, fwd, inv, out) for m in pair]
    return pl.pallas_call(
        _hconv_body,
        grid=(HY_W // LANES, DFT_NP + 2),
        in_specs=[sig, sig,
                  pl.BlockSpec((2, DFT_KB, FN2, LANES), lambda c, ph: (0, _mid_index(ph), 0, c)),
                  tw, tw] + [pl.BlockSpec(m.shape, const) for m in mats]
                 + [pl.BlockSpec((1, LANES), lambda c, ph: (0, c))],
        out_specs=pl.BlockSpec((2, SEQ, LANES), lambda c, ph: (0, 0, c)),
        out_shape=jax.ShapeDtypeStruct((2, SEQ, HY_W), BF16),
        scratch_shapes=[pltpu.VMEM((FN2, FN1, LANES), F32), pltpu.VMEM((2 * FN1, FN2, LANES), F32)],
        compiler_params=_cparams(("parallel", "arbitrary")),
        name="hyena_conv",
    )(u_rj, x2_rj, khat, twr, twi, *mats, bias)


OP_TM = 512


def _outproj_body(att_ref, hy_ref, x_ref, w_ref, g1_ref, n2_ref, sh_ref, sc_ref, wr_ref,
                  x1_ref, h2_ref, lg_ref):
    a = jnp.concatenate([att_ref[0], hy_ref[0]], axis=1)
    x1 = x_ref[0] + g1_ref[0] * _dot(a, w_ref[...])
    x1_ref[0] = x1
    h2 = _rms_mod(x1, n2_ref[...], sh_ref[0], sc_ref[0])
    hh, hl = _split(h2)
    h2_ref[0] = hh
    wr = wr_ref[...]
    both = _dot(wr, hh, _NT)
    lg_ref[0] = both[0:NE] + both[NE:2 * NE] + _dot(wr[0:NE], hl, _NT)


def _out_projection(att, hy, x, w_out_bf, g1r, n2g, sh2, sc2, wr2):
    b = x.shape[0]
    tok = lambda bi, i: (bi, i, 0)
    row = lambda bi, i: (bi, 0, 0)
    const = lambda bi, i: (0, 0)
    return pl.pallas_call(
        _outproj_body,
        grid=(b, SEQ // OP_TM),
        in_specs=[pl.BlockSpec((1, OP_TM, ATT_W), tok),
                  pl.BlockSpec((1, OP_TM, HY_W), tok),
                  pl.BlockSpec((1, OP_TM, D), tok),
                  pl.BlockSpec((ATT_W + HY_W, D), const),
                  pl.BlockSpec((1, 1, D), row),
                  pl.BlockSpec((1, D), const),
                  pl.BlockSpec((1, 1, D), row),
                  pl.BlockSpec((1, 1, D), row),
                  pl.BlockSpec((2 * NE, D), const)],
        out_specs=[pl.BlockSpec((1, OP_TM, D), tok),
                   pl.BlockSpec((1, OP_TM, D), tok),
                   pl.BlockSpec((1, NE, OP_TM), lambda bi, i: (bi, 0, i))],
        out_shape=[jax.ShapeDtypeStruct((b, SEQ, D), F32),
                   jax.ShapeDtypeStruct((b, SEQ, D), BF16),
                   jax.ShapeDtypeStruct((b, NE, SEQ), F32)],
        compiler_params=_cparams(("parallel", "parallel")),
        name="out_projection",
    )(att, hy, x, w_out_bf, g1r, n2g, sh2, sc2, wr2)


def _routing_body(lg_ref, tri_ref, pos_ref, gate_ref, off_ref, cs_ref):
    lg = lg_ref[0]
    e = jnp.exp(lg - jnp.max(lg, axis=0, keepdims=True))
    aff = e / jnp.sum(e, axis=0, keepdims=True)
    gate_ref[0] = aff
    def count_ge(t):
        return jnp.sum(jnp.where(aff >= t, 1.0, 0.0), axis=1, keepdims=True)

    def bisect(i, thr):
        cand = thr | (jnp.int32(1) << (30 - i))
        return jnp.where(count_ge(pltpu.bitcast(cand, F32)) >= float(CAP), cand, thr)

    thr = lax.fori_loop(0, 31, bisect, jnp.zeros((NE, 1), I32))
    lo = pltpu.bitcast(thr, F32)
    hi = jnp.maximum(pltpu.bitcast(thr + 1, F32), jnp.finfo(F32).tiny)

    def refine(i, c):
        lo, hi = c
        mid = lo + (hi - lo) * 0.5
        ok = count_ge(mid) >= float(CAP)
        return jnp.where(ok, mid, lo), jnp.where(ok, hi, mid)

    lo, hi = lax.fori_loop(0, 32, refine, (lo, hi))
    gt = aff >= hi
    eq = (aff >= lo) & jnp.logical_not(gt)
    need = float(CAP) - jnp.sum(jnp.where(gt, 1.0, 0.0), axis=1, keepdims=True)
    tri = tri_ref[...]

    def excl_cumsum(mask_f, record_offsets):
        carry = jnp.zeros((NE, 1), F32)
        for c in range(NTCH):
            sl = slice(c * TCH, (c + 1) * TCH)
            m = mask_f[:, sl]
            inc = _dot(m.astype(BF16), tri)
            cs_ref[:, sl] = inc - m + carry
            if record_offsets:
                off_ref[0, :, c:c + 1] = carry.astype(I32)
            carry = carry + inc[:, TCH - 1:TCH]
        return cs_ref[...]

    eq_rank = excl_cumsum(jnp.where(eq, 1.0, 0.0), False)
    sel = gt | (eq & (eq_rank < need))
    pos = excl_cumsum(jnp.where(sel, 1.0, 0.0), True)
    pos_ref[0] = jnp.where(sel, pos.astype(I32), -1)


def _routing(logits, tri):
    b = logits.shape[0]
    blk = pl.BlockSpec((1, NE, SEQ), lambda bi: (bi, 0, 0))
    return pl.pallas_call(
        _routing_body,
        grid=(b,),
        in_specs=[blk, pl.BlockSpec((TCH, TCH), lambda bi: (0, 0))],
        out_specs=[blk, blk, pl.BlockSpec((1, NE, NTCH), lambda bi: (bi, 0, 0))],
        out_shape=[jax.ShapeDtypeStruct((b, NE, SEQ), I32),
                   jax.ShapeDtypeStruct((b, NE, SEQ), F32),
                   jax.ShapeDtypeStruct((b, NE, NTCH), I32)],
        scratch_shapes=[pltpu.VMEM((NE, SEQ), F32)],
        compiler_params=_cparams(("parallel",)),
        name="routing",
    )(logits, tri)


GATHER_UNROLL = 8


def _gather_body(off_ref, h_ref, pos_ref, xg_ref, acc_ref):
    b = pl.program_id(0)
    e = pl.program_id(1)
    acc_ref[...] = jnp.zeros_like(acc_ref)
    crow = lax.broadcasted_iota(I32, (GW, TCH), 0)

    def chunks(i, _):
        for j in range(GATHER_UNROLL):
            c = i * GATHER_UNROLL + j
            off = off_ref[(b * NE + e) * NTCH + c]
            base = pl.multiple_of(jnp.minimum((off >> 3) << 3, CAP - TCH), SUBLANES)
            t0 = pl.multiple_of(c * TCH, TCH)
            rel = pos_ref[0, 0, pl.ds(c, 1), :] - base
            onehot = jnp.where(crow == rel, 1.0, 0.0).astype(BF16)
            acc_ref[pl.ds(base, GW), :] += _dot(onehot, h_ref[0, pl.ds(t0, TCH), :])
        return 0

    lax.fori_loop(0, NTCH // GATHER_UNROLL, chunks, 0)
    xg_ref[0, 0] = acc_ref[0:CAP, :].astype(BF16)


def _gather(offs_flat, h2, pos4):
    b = h2.shape[0]
    grid_spec = pltpu.PrefetchScalarGridSpec(
        num_scalar_prefetch=1,
        grid=(b, NE),
        in_specs=[pl.BlockSpec((1, SEQ, D), lambda bi, e, off: (bi, 0, 0)),
                  pl.BlockSpec((1, 1, NTCH, TCH), lambda bi, e, off: (bi, e, 0, 0))],
        out_specs=pl.BlockSpec((1, 1, CAP, D), lambda bi, e, off: (bi, e, 0, 0)),
        scratch_shapes=[pltpu.VMEM((CAP + SUBLANES, D), F32)],
    )
    return pl.pallas_call(
        _gather_body,
        grid_spec=grid_spec,
        out_shape=jax.ShapeDtypeStruct((b, NE, CAP, D), BF16),
        compiler_params=_cparams(("parallel", "arbitrary")),
        name="moe_gather",
    )(offs_flat, h2, pos4)


FFN_TM = 512
FFN_NF = 4
FFN_FC = DEXP // FFN_NF
assert FFN_FC * FFN_NF == DEXP and FFN_FC % BF16_ROWS == 0


def _ffn_body(xg_ref, wgt_ref, wut_ref, wd_ref, y_ref, acc_ref):
    j = pl.program_id(1)
    nb = xg_ref.shape[0]

    @pl.when(j == 0)
    def _():
        acc_ref[...] = jnp.zeros_like(acc_ref)

    wgt = wgt_ref[0].astype(BF16)
    wut = wut_ref[0].astype(BF16)
    wd = wd_ref[0].astype(BF16)
    for b in range(nb):
        for mb in range(CAP // FFN_TM):
            rows = slice(mb * FFN_TM, (mb + 1) * FFN_TM)
            xb = xg_ref[b, 0, rows, :]
            a = _dot(xb, wgt, _NT)
            u = _dot(xb, wut, _NT)
            h = (a * (1.0 / (1.0 + jnp.exp(-a))) * u).astype(BF16)
            acc_ref[b, rows, :] += _dot(h, wd)

    @pl.when(j == FFN_NF - 1)
    def _():
        for b in range(nb):
            y_ref[b, 0, 0:CAP, :] = acc_ref[b].astype(BF16)
            y_ref[b, 0, CAP:YROWS, :] = jnp.zeros((YROWS - CAP, D), BF16)


def _expert_ffn(xg, w_gate_t, w_up_t, w_down):
    b = xg.shape[0]
    wblk = pl.BlockSpec((1, FFN_FC, D), lambda e, j: (e, j, 0))
    return pl.pallas_call(
        _ffn_body,
        grid=(NE, FFN_NF),
        in_specs=[pl.BlockSpec((b, 1, CAP, D), lambda e, j: (0, e, 0, 0)), wblk, wblk, wblk],
        out_specs=pl.BlockSpec((b, 1, YROWS, D), lambda e, j: (0, e, 0, 0)),
        out_shape=jax.ShapeDtypeStruct((b, NE, YROWS, D), BF16),
        scratch_shapes=[pltpu.VMEM((b, CAP, D), F32)],
        compiler_params=_cparams(("parallel", "arbitrary")),
        name="expert_ffn",
    )(xg, w_gate_t, w_up_t, w_down)


def _combine_body(off_ref, y_ref, pos_ref, gate_ref, x1_ref, g2_ref, o_ref):
    b = pl.program_id(0)
    i = pl.program_id(1)
    crow = lax.broadcasted_iota(I32, (CW, TCH), 0)
    acc = jnp.zeros((TCH, D), F32)
    for e in range(NE):
        off = off_ref[(b * NE + e) * NTCH + i]
        base = pl.multiple_of(jnp.minimum((off >> 4) << 4, CAP - TCH), BF16_ROWS)
        rel = pos_ref[0, e:e + 1, :] - base
        w = jnp.where(crow == rel, gate_ref[0, e:e + 1, :], 0.0)
        yw = y_ref[0, e, pl.ds(base, CW), :]
        acc = acc + _dot(w.astype(BF16), yw, _TN)
    o_ref[0] = x1_ref[0] + g2_ref[0] * acc


def _combine(offs_flat, y, pos, gate, x1, g2r):
    b = x1.shape[0]
    grid_spec = pltpu.PrefetchScalarGridSpec(
        num_scalar_prefetch=1,
        grid=(b, NTCH),
        in_specs=[pl.BlockSpec((1, NE, YROWS, D), lambda bi, i, off: (bi, 0, 0, 0),
                               pipeline_mode=pl.Buffered(1)),
                  pl.BlockSpec((1, NE, TCH), lambda bi, i, off: (bi, 0, i)),
                  pl.BlockSpec((1, NE, TCH), lambda bi, i, off: (bi, 0, i)),
                  pl.BlockSpec((1, TCH, D), lambda bi, i, off: (bi, i, 0)),
                  pl.BlockSpec((1, 1, D), lambda bi, i, off: (bi, 0, 0))],
        out_specs=pl.BlockSpec((1, TCH, D), lambda bi, i, off: (bi, i, 0)),
    )
    return pl.pallas_call(
        _combine_body,
        grid_spec=grid_spec,
        out_shape=jax.ShapeDtypeStruct((b, SEQ, D), F32),
        compiler_params=_cparams(("parallel", "arbitrary")),
        name="moe_combine",
    )(offs_flat, y, pos, gate, x1, g2r)


def _np_split(m):
    hi = np.asarray(m, np.float64).astype(BF16)
    lo = (m - hi.astype(np.float64)).astype(BF16)
    return jnp.asarray(hi), jnp.asarray(lo)


@functools.lru_cache(maxsize=None)
def _dft_tables():
    a = np.arange(FN1, dtype=np.float64)
    ang = 2.0 * np.pi * np.outer(a, a) / FN1
    fr, fi = np.cos(ang), -np.sin(ang)
    half = SEQ // FN2
    lead_u = np.block([[fr[:, :half], -fi[:, :half]], [fi[:, :half], fr[:, :half]]])
    lead_k = np.concatenate([fr, fi], axis=0)
    fwd = np.block([[fr, -fi], [fi, fr]])
    inv = np.block([[fr, fi], [-fi, fr]])
    out = np.block([[fr[:half], fi[:half]], [-fi[:half], fr[:half]]])
    n2 = np.arange(FN2, dtype=np.float64)
    tw = 2.0 * np.pi * np.outer(a, n2) / FN
    twr = np.cos(tw).astype(np.float32)
    twi = (-np.sin(tw)).astype(np.float32)
    return dict(lead_u=lead_u, lead_k=lead_k, fwd=fwd, inv=inv, out=out, twr=twr, twi=twi)


@functools.lru_cache(maxsize=None)
def _filter_tables():
    L = SEQ
    n = np.arange(FN).reshape(FN1, FN2).T.reshape(-1)
    lag = np.where(n < L, n, FN - n)
    jc = np.minimum(lag, L - 1).astype(np.float64)
    t = (jc / (L - 1))[:, None]
    bands = (FEMB - 1) // 2
    w = 2.0 * np.pi * jc / L
    f = np.linspace(1e-4, bands - 1, bands)
    fw = w[:, None] * f[None, :]
    z = np.concatenate([t, np.cos(fw), -np.sin(fw), np.zeros((FN, FORD - FEMB))], axis=-1)
    mask = np.where(n == L, 0.0, 1.0)[:, None]
    fwd = np.where(n < L, 1.0, 0.0)[:, None]
    max_decay = math.log(DECAY_TARGET) / FAST_DECAY_PCT
    min_decay = math.log(DECAY_TARGET) / SLOW_DECAY_PCT
    negdelta = -np.abs(np.linspace(min_decay, max_decay, HY_W))[None, :]
    return tuple(np.asarray(a, np.float32) for a in (z, t, mask, fwd, negdelta))


@functools.lru_cache(maxsize=None)
def _rope_tables(n):
    rows = n // GRID_W
    row_id, col_id = np.meshgrid(np.arange(rows, dtype=np.float64), np.arange(GRID_W, dtype=np.float64), indexing="ij")
    quarter = HD // 4
    inv_freq = ROPE_THETA ** (-np.arange(quarter, dtype=np.float64) / quarter)
    ar = row_id.reshape(-1)[:, None] * inv_freq
    ac = col_id.reshape(-1)[:, None] * inv_freq
    cos = np.concatenate([np.cos(ar), np.cos(ar), np.cos(ac), np.cos(ac)], axis=-1)
    sin = np.concatenate([-np.sin(ar), np.sin(ar), -np.sin(ac), np.sin(ac)], axis=-1)
    reps = (1, LANES // HD)
    return np.tile(cos, reps).astype(np.float32), np.tile(sin, reps).astype(np.float32)


def _hyena_long_conv(u_rj, x2_rj, kern, abs_sum, bias):
    tb = _dft_tables()
    twr, twi = tb["twr"], tb["twi"]
    fwd = _np_split(tb["fwd"])
    scale = 1.0 / (abs_sum * float(FN))
    khat = _filter_spectrum(kern.reshape(FN2, FN1, HY_W), twr, twi, *_np_split(tb["lead_k"]), *fwd, scale)
    return _hyena_conv(u_rj, x2_rj, khat, twr, twi, _np_split(tb["lead_u"]), fwd, _np_split(tb["inv"]),
                       _np_split(tb["out"]), bias.reshape(1, HY_W))


def kernel(x, c, ctx, c_ctx, w_mod, b_mod, norm1_g, norm2_g, w_in, w_out, q_norm_g, k_norm_g,
           conv_w, conv_b, filt_w1, filt_b1, filt_w2, filt_b2, filt_w3, filt_freq, hyena_bias,
           w_router, w_gate, w_up, w_down):
    B = x.shape[0]
    assert x.shape == (B, SEQ, D) and B == 2 and ctx.shape == (B, CTX, D) and w_mod.shape[0] == 1
    l = 0

    cc = jnp.concatenate([c, c_ctx[None, :], jnp.zeros((SUBLANES - B - 1, D), F32)], axis=0)
    mod = _modulation(cc, w_mod[l], b_mod[l][None, :])
    sh1, sc1, g1, sh2, sc2, g2 = [mod[:, i * D:(i + 1) * D] for i in range(6)]
    lat = lambda m: m[:B, None, :]
    ctxrow = lambda m: jnp.broadcast_to(m[B:B + 1, None, :], (B, 1, D))

    w_in_bf = w_in[l].astype(BF16)
    gq2 = jnp.tile(q_norm_g[l][None, :], (1, LANES // HD))
    gk2 = jnp.tile(k_norm_g[l][None, :], (1, LANES // HD))
    bd = jnp.asarray(np.kron(np.eye(2 * LANES // HD), np.full((HD, HD), 1.0 / HD)), BF16)
    cos_t, sin_t = _rope_tables(SEQ)
    n1g = norm1_g[l][None, :]

    q, k, vt, p = _in_projection(x, n1g, lat(sh1), lat(sc1), w_in_bf, gq2, gk2, bd, cos_t, sin_t, 512)
    _, kc, vct, _ = _in_projection(ctx, n1g, ctxrow(sh1), ctxrow(sc1), w_in_bf, gq2, gk2, bd,
                                   jnp.ones((CTX, LANES), F32), jnp.zeros((CTX, LANES), F32), CTX)

    kch = jnp.concatenate([k, kc], axis=2).reshape(B, NKV, ATT_NCH, ATT_TK, HD)
    vt_all = jnp.concatenate([vt, vct], axis=3).reshape(B, NKV, HD, ATT_NCH, ATT_TK)
    ones_pad = jnp.concatenate([jnp.ones((B, NKV, ATT_NCH, 1, ATT_TK), BF16),
                                jnp.zeros((B, NKV, ATT_NCH, BF16_ROWS - 1, ATT_TK), BF16)], axis=3)
    vtch = jnp.concatenate([vt_all.transpose(0, 1, 3, 2, 4), ones_pad], axis=3)
    bound = (1.02 * HD * Q_SCALE) * jnp.max(jnp.abs(q_norm_g[l])) * jnp.max(jnp.abs(k_norm_g[l]))
    att = _attention(bound.reshape(1).astype(F32), q, kch, vtch)

    cw9 = conv_w[l].reshape(3, 3, HY_W).reshape(9, HY_W)
    cb3 = conv_b[l].reshape(3, HY_W)
    u_rj, x2_rj = _short_conv(p, cw9, cb3)
    ztab, ttab, mtab, ftab, negdelta = _filter_tables()
    w1p = jnp.concatenate([filt_w1[l], jnp.zeros((FORD - FEMB, FORD), F32)], axis=0)
    kern, abs_sum = _implicit_filter(ztab, ttab, mtab, ftab, w1p, filt_b1[l][None, :], filt_w2[l],
                                     filt_b2[l][None, :], filt_w3[l], filt_freq[l][None, :], negdelta)
    hy = _hyena_long_conv(u_rj, x2_rj, kern, abs_sum, hyena_bias[l])

    wr2 = jnp.concatenate(_split(w_router[l].T), axis=0)
    x1, h2, logits = _out_projection(att, hy, x, w_out[l].astype(BF16), lat(g1), norm2_g[l][None, :],
                                     lat(sh2), lat(sc2), wr2)

    tri = jnp.asarray(np.triu(np.ones((TCH, TCH))), BF16)
    pos, gate, offs = _routing(logits, tri)
    offs_flat = offs.reshape(-1)
    xg = _gather(offs_flat, h2, pos.reshape(B, NE, NTCH, TCH))
    y = _expert_ffn(xg, jnp.swapaxes(w_gate[l], 1, 2), jnp.swapaxes(w_up[l], 1, 2), w_down[l])
    return _combine(offs_flat, y, pos, gate, x1, lat(g2))
```

```python
import functools
import math

import numpy as np
import jax
import jax.numpy as jnp
from jax import lax
from jax.experimental import pallas as pl
from jax.experimental.pallas import tpu as pltpu

F32 = jnp.float32
BF16 = jnp.bfloat16
I32 = jnp.int32

D = 1024
SEQ = 8192
CTX = 256
GRID_W = 64
ATT_W = 512
HY_W = 512
HD = 64
NQ = 8
NKV = 2
QPK = NQ // NKV
KV_W = NKV * HD
IN_W = ATT_W + 2 * KV_W + 3 * HY_W
FEMB = 33
FORD = 64
NE = 16
CAP = 2 * SEQ // NE
DEXP = 2752
ROPE_THETA = 10000.0
EPS = 1e-6
DECAY_TARGET = 1e-2
FAST_DECAY_PCT = 0.3
SLOW_DECAY_PCT = 1.5

LANES = 128
SUBLANES = 8
BF16_ROWS = 16
VMEM_BYTES_V7X = 64 * 1024 * 1024
VMEM_LIMIT = VMEM_BYTES_V7X - 8 * 1024 * 1024

FN = 2 * SEQ
FN1 = 128
FN2 = 128

TCH = LANES
NTCH = SEQ // TCH
GW = TCH + SUBLANES
CW = TCH + BF16_ROWS
YROWS = CAP + BF16_ROWS


def _cparams(sem, vmem=None):
    return pltpu.CompilerParams(dimension_semantics=sem, vmem_limit_bytes=vmem or VMEM_LIMIT)


def _split(a):
    hi = a.astype(BF16)
    lo = (a - hi.astype(F32)).astype(BF16)
    return hi, lo


_NN = (((1,), (0,)), ((), ()))
_NT = (((1,), (1,)), ((), ()))
_TN = (((0,), (0,)), ((), ()))


def _dot(a, b, dn=_NN):
    return lax.dot_general(a, b, dn, preferred_element_type=F32)


def _dot3(a, b, dn=_NN):
    ah, al = _split(a)
    bh, bl = _split(b)
    return _dot(ah, bh, dn) + _dot(ah, bl, dn) + _dot(al, bh, dn)


def _mod_body(c_ref, w_ref, b_ref, o_ref):
    c = c_ref[...]
    s = c * (1.0 / (1.0 + jnp.exp(-c)))
    o_ref[...] = _dot3(s, w_ref[...]) + b_ref[...]


def _modulation(cc, w_mod, b_mod):
    n = w_mod.shape[1]
    return pl.pallas_call(
        _mod_body,
        grid=(n // D,),
        in_specs=[pl.BlockSpec((SUBLANES, D), lambda j: (0, 0)),
                  pl.BlockSpec((D, D), lambda j: (0, j)),
                  pl.BlockSpec((1, D), lambda j: (0, j))],
        out_specs=pl.BlockSpec((SUBLANES, D), lambda j: (0, j)),
        out_shape=jax.ShapeDtypeStruct((SUBLANES, n), F32),
        compiler_params=_cparams(("arbitrary",)),
        name="modulation",
    )(cc, w_mod, b_mod)


Q_SCALE = HD ** -0.5 * math.log2(math.e)


def _rms_mod(x, g, sh, sc):
    ms = jnp.mean(x * x, axis=-1, keepdims=True)
    return (x * lax.rsqrt(ms + EPS) * g) * (1.0 + sc) + sh


def _head_mean_square(t, bd):
    hi, lo = _split(t * t)
    return _dot(hi, bd) + _dot(lo, bd)


def _head_norm_rope(t, ms, g, cos, sin):
    tn = t * lax.rsqrt(ms + EPS) * g
    lane = lax.broadcasted_iota(I32, tn.shape, 1)
    sw = jnp.where((lane & 31) < 16, pltpu.roll(tn, LANES - 16, 1), pltpu.roll(tn, 16, 1))
    return tn * cos + sw * sin


def _proj_body(x_ref, g_ref, sh_ref, sc_ref, w_ref, gq_ref, gk_ref, bd_ref, cos_ref, sin_ref,
               q_ref, k_ref, v_ref, p_ref):
    h = _rms_mod(x_ref[0], g_ref[...], sh_ref[0], sc_ref[0])
    proj = _dot(h.astype(BF16), w_ref[...])
    bd = bd_ref[...]
    cos = cos_ref[...]
    sin = sin_ref[...]
    wide = 2 * LANES
    for j in range(ATT_W // wide):
        ms = _head_mean_square(proj[:, j * wide:(j + 1) * wide], bd)
        for i in range(2):
            sl = slice(j * wide + i * LANES, j * wide + (i + 1) * LANES)
            qj = _head_norm_rope(proj[:, sl], ms[:, i * LANES:(i + 1) * LANES], gq_ref[...], cos, sin)
            q_ref[0, :, sl] = (qj * Q_SCALE).astype(BF16)
    ms = _head_mean_square(proj[:, ATT_W:ATT_W + 2 * KV_W], bd)
    kk = _head_norm_rope(proj[:, ATT_W:ATT_W + KV_W], ms[:, 0:KV_W], gk_ref[...], cos, sin)
    vt = proj[:, ATT_W + KV_W:ATT_W + 2 * KV_W].T
    for g in range(NKV):
        k_ref[0, g] = kk[:, g * HD:(g + 1) * HD].astype(BF16)
        v_ref[0, g] = vt[g * HD:(g + 1) * HD, :].astype(BF16)
    p_ref[0] = proj[:, ATT_W + 2 * KV_W:]


def _in_projection(x, g1, sh, sc, w_in_bf, gq2, gk2, bd, cos_t, sin_t, tm):
    b, s, _ = x.shape
    row = lambda bi, i: (bi, 0, 0)
    tok = lambda bi, i: (bi, i, 0)
    const = lambda bi, i: (0, 0)
    return pl.pallas_call(
        _proj_body,
        grid=(b, s // tm),
        in_specs=[pl.BlockSpec((1, tm, D), tok),
                  pl.BlockSpec((1, D), const),
                  pl.BlockSpec((1, 1, D), row),
                  pl.BlockSpec((1, 1, D), row),
                  pl.BlockSpec((D, IN_W), const),
                  pl.BlockSpec((1, LANES), const),
                  pl.BlockSpec((1, LANES), const),
                  pl.BlockSpec((2 * LANES, 2 * LANES), const),
                  pl.BlockSpec((tm, LANES), lambda bi, i: (i, 0)),
                  pl.BlockSpec((tm, LANES), lambda bi, i: (i, 0))],
        out_specs=[pl.BlockSpec((1, tm, ATT_W), tok),
                   pl.BlockSpec((1, NKV, tm, HD), lambda bi, i: (bi, 0, i, 0)),
                   pl.BlockSpec((1, NKV, HD, tm), lambda bi, i: (bi, 0, 0, i)),
                   pl.BlockSpec((1, tm, 3 * HY_W), tok)],
        out_shape=[jax.ShapeDtypeStruct((b, s, ATT_W), BF16),
                   jax.ShapeDtypeStruct((b, NKV, s, HD), BF16),
                   jax.ShapeDtypeStruct((b, NKV, HD, s), BF16),
                   jax.ShapeDtypeStruct((b, s, 3 * HY_W), F32)],
        compiler_params=_cparams(("parallel", "parallel")),
        name="in_projection",
    )(x, g1, sh, sc, w_in_bf, gq2, gk2, bd, cos_t, sin_t)


ATT_TQ = 512
ATT_TK = 768
SK = SEQ + CTX
ATT_NCH = SK // ATT_TK


ATT_NQ = QPK * ATT_TQ
ATT_VR = HD + BF16_ROWS
assert ATT_NCH % 2 == 1
ATT_SHIFT_MAX = 120.0


def _attn_body(bound_ref, q_ref, k_ref, vt_ref, o_ref, s_ref, mx_ref, m_ref, acc_ref):
    qall = jnp.concatenate([q_ref[0, :, r * HD:(r + 1) * HD] for r in range(QPK)], axis=0)
    acc_ref[...] = jnp.zeros_like(acc_ref)
    bound = bound_ref[0]
    fixed_shift = 2.0 * bound <= ATT_SHIFT_MAX

    def finish():
        out = acc_ref[0:HD, :] * (1.0 / acc_ref[HD:HD + 1, :])
        for r in range(QPK):
            o_ref[0, :, r * HD:(r + 1) * HD] = out[:, r * ATT_TQ:(r + 1) * ATT_TQ].T.astype(BF16)

    @pl.when(fixed_shift)
    def _():
        def chunk(c, _):
            s = _dot(k_ref[0, 0, c], qall, _NT)
            acc_ref[...] += _dot(vt_ref[0, 0, c], jnp.exp2(s - bound).astype(BF16))
            return 0

        lax.fori_loop(0, ATT_NCH, chunk, 0, unroll=2)
        finish()

    @pl.when(jnp.logical_not(fixed_shift))
    def _():
        m_ref[...] = jnp.full(m_ref.shape, -1e30, F32)

        def scores(c, slot):
            s = _dot(k_ref[0, 0, c], qall, _NT)
            s_ref[slot] = s
            mx_ref[slot] = jnp.max(s, axis=0, keepdims=True)

        def update(c, slot):
            m_old = m_ref[...]
            m_new = jnp.maximum(m_old, mx_ref[slot])
            p = jnp.exp2(s_ref[slot] - m_new).astype(BF16)
            acc_ref[...] = jnp.exp2(m_old - m_new) * acc_ref[...] + _dot(vt_ref[0, 0, c], p)
            m_ref[...] = m_new

        scores(0, 0)

        def pair(i, _):
            c = 2 * i
            scores(c + 1, 1)
            update(c, 0)
            scores(c + 2, 0)
            update(c + 1, 1)
            return 0

        lax.fori_loop(0, ATT_NCH // 2, pair, 0)
        update(ATT_NCH - 1, 0)
        finish()


def _attention(bound, q, kch, vtch):
    b = q.shape[0]
    grid_spec = pltpu.PrefetchScalarGridSpec(
        num_scalar_prefetch=1,
        grid=(b, NKV, SEQ // ATT_TQ),
        in_specs=[pl.BlockSpec((1, ATT_TQ, QPK * HD), lambda bi, g, i, bd: (bi, i, g)),
                  pl.BlockSpec((1, 1, ATT_NCH, ATT_TK, HD), lambda bi, g, i, bd: (bi, g, 0, 0, 0)),
                  pl.BlockSpec((1, 1, ATT_NCH, ATT_VR, ATT_TK), lambda bi, g, i, bd: (bi, g, 0, 0, 0))],
        out_specs=pl.BlockSpec((1, ATT_TQ, QPK * HD), lambda bi, g, i, bd: (bi, i, g)),
        scratch_shapes=[pltpu.VMEM((2, ATT_TK, ATT_NQ), F32), pltpu.VMEM((2, 1, ATT_NQ), F32),
                        pltpu.VMEM((1, ATT_NQ), F32), pltpu.VMEM((ATT_VR, ATT_NQ), F32)],
    )
    return pl.pallas_call(
        _attn_body,
        grid_spec=grid_spec,
        out_shape=jax.ShapeDtypeStruct((b, SEQ, ATT_W), BF16),
        compiler_params=_cparams(("parallel", "parallel", "parallel")),
        name="attention",
    )(bound, q, kch, vtch)


SC_TM = 2048
SC_J = SC_TM // FN2


def _sconv_body(m1, a1, n1, m2, a2, n2, m3, a3, n3, w_ref, b_ref, u_ref, x2_ref):
    i = pl.program_id(1)
    last = pl.num_programs(1) - 1
    rows = lax.broadcasted_iota(I32, (SC_TM, HY_W), 0)

    def conv(main, prev, nxt, g):
        x = main[0]
        pr = jnp.where(i > 0, prev[0, SUBLANES - 1:SUBLANES, :], 0.0)
        nx = jnp.where(i < last, nxt[0, 0:1, :], 0.0)
        xm = jnp.where(rows == 0, pr, pltpu.roll(x, 1, 0))
        xp = jnp.where(rows == SC_TM - 1, nx, pltpu.roll(x, SC_TM - 1, 0))
        return (w_ref[g:g + 1, :] * xm + w_ref[3 + g:4 + g, :] * x + w_ref[6 + g:7 + g, :] * xp
                + b_ref[g:g + 1, :])

    def to_rj(t):
        return jnp.swapaxes(t.reshape(SC_J, FN2, HY_W), 0, 1).astype(BF16)

    x1 = conv(m1, a1, n1, 0)
    x2 = conv(m2, a2, n2, 1)
    v = conv(m3, a3, n3, 2)
    u_ref[0] = to_rj(v * x1)
    x2_ref[0] = to_rj(x2)


def _short_conv(p, cw9, cb3):
    b = p.shape[0]
    nblk8 = SEQ // SUBLANES
    step8 = SC_TM // SUBLANES
    specs = []
    for g in range(3):
        specs += [pl.BlockSpec((1, SC_TM, HY_W), lambda bi, i, g=g: (bi, i, g)),
                  pl.BlockSpec((1, SUBLANES, HY_W), lambda bi, i, g=g: (bi, jnp.maximum(i * step8 - 1, 0), g)),
                  pl.BlockSpec((1, SUBLANES, HY_W), lambda bi, i, g=g: (bi, jnp.minimum((i + 1) * step8, nblk8 - 1), g))]
    specs += [pl.BlockSpec((9, HY_W), lambda bi, i: (0, 0)), pl.BlockSpec((3, HY_W), lambda bi, i: (0, 0))]
    out = pl.BlockSpec((1, FN2, SC_J, HY_W), lambda bi, i: (bi, 0, i, 0))
    return pl.pallas_call(
        _sconv_body,
        grid=(b, SEQ // SC_TM),
        in_specs=specs,
        out_specs=[out, out],
        out_shape=[jax.ShapeDtypeStruct((b, FN2, SEQ // FN2, HY_W), BF16)] * 2,
        compiler_params=_cparams(("parallel", "parallel")),
        name="short_conv",
    )(p, p, p, p, p, p, p, p, p, cw9, cb3)


FILT_TR = 1024


def _filter_body(z_ref, t_ref, msk_ref, fwd_ref, w1_ref, b1_ref, w2_ref, b2_ref, w3_ref, fr_ref, dl_ref,
                 k_ref, s_ref):
    fr = fr_ref[...]
    h = jnp.sin(fr * (_dot3(z_ref[...], w1_ref[...]) + b1_ref[...]))
    h = jnp.sin(fr * (_dot3(h, w2_ref[...]) + b2_ref[...]))
    h = _dot3(h, w3_ref[...])
    h = jnp.where(fwd_ref[...] > 0.5, h[:, :HY_W], h[:, HY_W:])
    kern = h * jnp.exp(t_ref[...] * dl_ref[...]) * msk_ref[...]
    k_ref[...] = kern

    @pl.when(pl.program_id(0) == 0)
    def _():
        s_ref[...] = jnp.zeros_like(s_ref)

    s_ref[...] += jnp.sum(jnp.abs(kern), axis=0, keepdims=True)


def _implicit_filter(ztab, ttab, mtab, ftab, w1p, b1, w2, b2, w3, freq, negdelta):
    rowblk = lambda i: (i, 0)
    const = lambda i: (0, 0)
    col = pl.BlockSpec((FILT_TR, 1), rowblk)
    return pl.pallas_call(
        _filter_body,
        grid=(FN // FILT_TR,),
        in_specs=[pl.BlockSpec((FILT_TR, FORD), rowblk), col, col, col,
                  pl.BlockSpec((FORD, FORD), const),
                  pl.BlockSpec((1, FORD), const),
                  pl.BlockSpec((FORD, FORD), const),
                  pl.BlockSpec((1, FORD), const),
                  pl.BlockSpec((FORD, 2 * HY_W), const),
                  pl.BlockSpec((1, FORD), const),
                  pl.BlockSpec((1, HY_W), const)],
        out_specs=[pl.BlockSpec((FILT_TR, HY_W), rowblk),
                   pl.BlockSpec((1, HY_W), const)],
        out_shape=[jax.ShapeDtypeStruct((FN, HY_W), F32), jax.ShapeDtypeStruct((1, HY_W), F32)],
        compiler_params=_cparams(("arbitrary",)),
        name="implicit_filter",
    )(ztab, ttab, mtab, ftab, w1p, b1, w2, b2, w3, freq, negdelta)


DFT_G = 4
DFT_KB = 16
DFT_NP = FN1 // DFT_KB
DFT_UNROLL = 4


def _dot2c(fh, fl, zb):
    return _dot(fh, zb) + _dot(fl, zb)


def _lead_stage(src, fh, fl, dst_ref):
    def group(rg, _):
        r0 = rg * DFT_G
        rhs = jnp.concatenate([src(r0 + g) for g in range(DFT_G)], axis=1)
        blk = _dot2c(fh, fl, rhs)
        for g in range(DFT_G):
            dst_ref[r0 + g] = blk[:, g * LANES:(g + 1) * LANES]
        return 0

    lax.fori_loop(0, FN2 // DFT_G, group, 0, unroll=DFT_UNROLL)


def _lead_phase(src, lh_ref, ll_ref, p_ref, q_ref):
    for h in range(2):
        rows = slice(h * FN1, (h + 1) * FN1)
        _lead_stage(src, lh_ref[rows, :], ll_ref[rows, :], p_ref)
        q_ref[rows] = jnp.swapaxes(p_ref[...], 0, 1)


def _lane_cat(xs):
    return jnp.concatenate(xs, axis=1)


def _mid_forward(q_ref, k0, tr_ref, ti_ref, fh, fl, half):
    tr_t, ti_t = tr_ref[...].T, ti_ref[...].T
    brs, bis, trs, tis = [], [], [], []
    for g in range(DFT_G):
        jj = half * DFT_G + g
        ar, ai = q_ref[k0 + jj], q_ref[FN1 + k0 + jj]
        tr = jnp.broadcast_to(tr_t[:, jj:jj + 1], (FN2, LANES))
        ti = jnp.broadcast_to(ti_t[:, jj:jj + 1], (FN2, LANES))
        brs.append(ar * tr - ai * ti)
        bis.append(ar * ti + ai * tr)
        trs.append(tr)
        tis.append(ti)
    b = jnp.concatenate([_lane_cat(brs), _lane_cat(bis)], axis=0).astype(BF16)
    return _dot2c(fh, fl, b), _lane_cat(trs), _lane_cat(tis)


def _spectrum_body(k_ref, tr_ref, ti_ref, lh_ref, ll_ref, fh_ref, fl_ref, sc_ref, o_ref, p_ref, q_ref):
    ph = pl.program_id(1)

    @pl.when(ph == 0)
    def _():
        _lead_phase(lambda r: k_ref[r].astype(BF16), lh_ref, ll_ref, p_ref, q_ref)

    @pl.when(ph > 0)
    def _():
        k0 = (ph - 1) * DFT_KB
        sc = _lane_cat([sc_ref[...]] * DFT_G)
        for half in range(DFT_KB // DFT_G):
            x, _, _ = _mid_forward(q_ref, k0, tr_ref, ti_ref, fh_ref[...], fl_ref[...], half)
            x = x * sc
            for g in range(DFT_G):
                lanes = slice(g * LANES, (g + 1) * LANES)
                o_ref[0, half * DFT_G + g] = x[:FN2, lanes]
                o_ref[1, half * DFT_G + g] = x[FN2:, lanes]


def _mid_index(ph):
    return jnp.clip(ph - 1, 0, DFT_NP - 1)


def _filter_spectrum(kern_rj, twr, twi, lh, ll, fh, fl, scale):
    const = lambda c, ph: (0, 0)
    tw = pl.BlockSpec((DFT_KB, FN2), lambda c, ph: (_mid_index(ph), 0))
    return pl.pallas_call(
        _spectrum_body,
        grid=(HY_W // LANES, DFT_NP + 1),
        in_specs=[pl.BlockSpec((FN2, FN1, LANES), lambda c, ph: (0, 0, c), pipeline_mode=pl.Buffered(1)),
                  tw, tw,
                  pl.BlockSpec(lh.shape, const), pl.BlockSpec(ll.shape, const),
                  pl.BlockSpec(fh.shape, const), pl.BlockSpec(fl.shape, const),
                  pl.BlockSpec((1, LANES), lambda c, ph: (0, c))],
        out_specs=pl.BlockSpec((2, DFT_KB, FN2, LANES), lambda c, ph: (0, _mid_index(ph), 0, c)),
        out_shape=jax.ShapeDtypeStruct((2, FN1, FN2, HY_W), F32),
        scratch_shapes=[pltpu.VMEM((FN2, FN1, LANES), F32), pltpu.VMEM((2 * FN1, FN2, LANES), F32)],
        compiler_params=_cparams(("parallel", "arbitrary")),
        name="filter_spectrum",
    )(kern_rj, twr, twi, lh, ll, fh, fl, scale)


def _hconv_body(u_ref, x2_ref, kh_ref, tr_ref, ti_ref, lh_ref, ll_ref, fh_ref, fl_ref, gh_ref, gl_ref,
                oh_ref, ol_ref, bias_ref, o_ref, p_ref, q_ref):
    ph = pl.program_id(1)

    def both(ref, r):
        return jnp.concatenate([ref[0, r], ref[1, r]], axis=0)

    @pl.when(ph == 0)
    def _():
        _lead_phase(lambda r: both(u_ref, r), lh_ref, ll_ref, p_ref, q_ref)

    @pl.when((ph > 0) & (ph <= DFT_NP))
    def _():
        k0 = (ph - 1) * DFT_KB
        for half in range(DFT_KB // DFT_G):
            x, tr, ti = _mid_forward(q_ref, k0, tr_ref, ti_ref, fh_ref[...], fl_ref[...], half)
            xr, xi = x[:FN2], x[FN2:]
            kr = _lane_cat([kh_ref[0, half * DFT_G + g] for g in range(DFT_G)])
            ki = _lane_cat([kh_ref[1, half * DFT_G + g] for g in range(DFT_G)])
            y = jnp.concatenate([xr * kr - xi * ki, xr * ki + xi * kr], axis=0).astype(BF16)
            c = _dot2c(gh_ref[...], gl_ref[...], y)
            cr, ci = c[:FN2], c[FN2:]
            dr = cr * tr + ci * ti
            di = ci * tr - cr * ti
            for g in range(DFT_G):
                lanes = slice(g * LANES, (g + 1) * LANES)
                q_ref[k0 + half * DFT_G + g] = dr[:, lanes]
                q_ref[FN1 + k0 + half * DFT_G + g] = di[:, lanes]

    @pl.when(ph == DFT_NP + 1)
    def _():
        bias = bias_ref[...]
        p_ref[...] = jnp.swapaxes(q_ref[0:FN1], 0, 1)
        _lead_stage(lambda r: p_ref[r].astype(BF16), oh_ref[:, 0:FN1], ol_ref[:, 0:FN1], q_ref)
        p_ref[...] = jnp.swapaxes(q_ref[FN1:2 * FN1], 0, 1)
        oh2, ol2 = oh_ref[:, FN1:2 * FN1], ol_ref[:, FN1:2 * FN1]

        def group(rg, _):
            r0 = rg * DFT_G
            rhs = _lane_cat([p_ref[r0 + g].astype(BF16) for g in range(DFT_G)])
            blk = _dot2c(oh2, ol2, rhs)
            for g in range(DFT_G):
                r = r0 + g
                y = q_ref[r] + blk[:, g * LANES:(g + 1) * LANES]
                q_ref[r] = (y + both(u_ref, r).astype(F32) * bias) * both(x2_ref, r).astype(F32)
            return 0

        lax.fori_loop(0, FN2 // DFT_G, group, 0, unroll=DFT_UNROLL)
        p_ref[...] = jnp.swapaxes(q_ref[0:FN2], 0, 1)
        nj = SEQ // FN2
        for b in range(2):
            o_ref[b] = p_ref[b * nj:(b + 1) * nj].reshape(SEQ, LANES).astype(BF16)


def _hyena_conv(u_rj, x2_rj, khat, twr, twi, lead, fwd, inv, out, bias):
    const = lambda c, ph: (0, 0)
    nj = SEQ // FN2
    tw = pl.BlockSpec((DFT_KB, FN2), lambda c, ph: (_mid_index(ph), 0))
    sig = pl.BlockSpec((2, FN2, nj, LANES), lambda c, ph: (0, 0, 0, c), pipeline_mode=pl.Buffered(1))
    mats = [m for pair in (lead, fwd, inv, out) for m in pair]
    return pl.pallas_call(
        _hconv_body,
        grid=(HY_W // LANES, DFT_NP + 2),
        in_specs=[sig, sig,
                  pl.BlockSpec((2, DFT_KB, FN2, LANES), lambda c, ph: (0, _mid_index(ph), 0, c)),
                  tw, tw] + [pl.BlockSpec(m.shape, const) for m in mats]
                 + [pl.BlockSpec((1, LANES), lambda c, ph: (0, c))],
        out_specs=pl.BlockSpec((2, SEQ, LANES), lambda c, ph: (0, 0, c)),
        out_shape=jax.ShapeDtypeStruct((2, SEQ, HY_W), BF16),
        scratch_shapes=[pltpu.VMEM((FN2, FN1, LANES), F32), pltpu.VMEM((2 * FN1, FN2, LANES), F32)],
        compiler_params=_cparams(("parallel", "arbitrary")),
        name="hyena_conv",
    )(u_rj, x2_rj, khat, twr, twi, *mats, bias)


OP_TM = 512


def _outproj_body(att_ref, hy_ref, x_ref, w_ref, g1_ref, n2_ref, sh_ref, sc_ref, wr_ref,
                  x1_ref, h2_ref, lg_ref):
    a = jnp.concatenate([att_ref[0], hy_ref[0]], axis=1)
    x1 = x_ref[0] + g1_ref[0] * _dot(a, w_ref[...])
    x1_ref[0] = x1
    h2 = _rms_mod(x1, n2_ref[...], sh_ref[0], sc_ref[0])
    hh, hl = _split(h2)
    h2_ref[0] = hh
    wr = wr_ref[...]
    both = _dot(wr, hh, _NT)
    lg_ref[0] = both[0:NE] + both[NE:2 * NE] + _dot(wr[0:NE], hl, _NT)


def _out_projection(att, hy, x, w_out_bf, g1r, n2g, sh2, sc2, wr2):
    b = x.shape[0]
    tok = lambda bi, i: (bi, i, 0)
    row = lambda bi, i: (bi, 0, 0)
    const = lambda bi, i: (0, 0)
    return pl.pallas_call(
        _outproj_body,
        grid=(b, SEQ // OP_TM),
        in_specs=[pl.BlockSpec((1, OP_TM, ATT_W), tok),
                  pl.BlockSpec((1, OP_TM, HY_W), tok),
                  pl.BlockSpec((1, OP_TM, D), tok),
                  pl.BlockSpec((ATT_W + HY_W, D), const),
                  pl.BlockSpec((1, 1, D), row),
                  pl.BlockSpec((1, D), const),
                  pl.BlockSpec((1, 1, D), row),
                  pl.BlockSpec((1, 1, D), row),
                  pl.BlockSpec((2 * NE, D), const)],
        out_specs=[pl.BlockSpec((1, OP_TM, D), tok),
                   pl.BlockSpec((1, OP_TM, D), tok),
                   pl.BlockSpec((1, NE, OP_TM), lambda bi, i: (bi, 0, i))],
        out_shape=[jax.ShapeDtypeStruct((b, SEQ, D), F32),
                   jax.ShapeDtypeStruct((b, SEQ, D), BF16),
                   jax.ShapeDtypeStruct((b, NE, SEQ), F32)],
        compiler_params=_cparams(("parallel", "parallel")),
        name="out_projection",
    )(att, hy, x, w_out_bf, g1r, n2g, sh2, sc2, wr2)


def _routing_body(lg_ref, tri_ref, pos_ref, gate_ref, off_ref, cs_ref):
    lg = lg_ref[0]
    e = jnp.exp(lg - jnp.max(lg, axis=0, keepdims=True))
    aff = e / jnp.sum(e, axis=0, keepdims=True)
    gate_ref[0] = aff
    def count_ge(t):
        return jnp.sum(jnp.where(aff >= t, 1.0, 0.0), axis=1, keepdims=True)

    def bisect(i, thr):
        cand = thr | (jnp.int32(1) << (30 - i))
        return jnp.where(count_ge(pltpu.bitcast(cand, F32)) >= float(CAP), cand, thr)

    thr = lax.fori_loop(0, 31, bisect, jnp.zeros((NE, 1), I32))
    lo = pltpu.bitcast(thr, F32)
    hi = jnp.maximum(pltpu.bitcast(thr + 1, F32), jnp.finfo(F32).tiny)

    def refine(i, c):
        lo, hi = c
        mid = lo + (hi - lo) * 0.5
        ok = count_ge(mid) >= float(CAP)
        return jnp.where(ok, mid, lo), jnp.where(ok, hi, mid)

    lo, hi = lax.fori_loop(0, 32, refine, (lo, hi))
    gt = aff >= hi
    eq = (aff >= lo) & jnp.logical_not(gt)
    need = float(CAP) - jnp.sum(jnp.where(gt, 1.0, 0.0), axis=1, keepdims=True)
    tri = tri_ref[...]

    def excl_cumsum(mask_f, record_offsets):
        carry = jnp.zeros((NE, 1), F32)
        for c in range(NTCH):
            sl = slice(c * TCH, (c + 1) * TCH)
            m = mask_f[:, sl]
            inc = _dot(m.astype(BF16), tri)
            cs_ref[:, sl] = inc - m + carry
            if record_offsets:
                off_ref[0, :, c:c + 1] = carry.astype(I32)
            carry = carry + inc[:, TCH - 1:TCH]
        return cs_ref[...]

    eq_rank = excl_cumsum(jnp.where(eq, 1.0, 0.0), False)
    sel = gt | (eq & (eq_rank < need))
    pos = excl_cumsum(jnp.where(sel, 1.0, 0.0), True)
    pos_ref[0] = jnp.where(sel, pos.astype(I32), -1)


def _routing(logits, tri):
    b = logits.shape[0]
    blk = pl.BlockSpec((1, NE, SEQ), lambda bi: (bi, 0, 0))
    return pl.pallas_call(
        _routing_body,
        grid=(b,),
        in_specs=[blk, pl.BlockSpec((TCH, TCH), lambda bi: (0, 0))],
        out_specs=[blk, blk, pl.BlockSpec((1, NE, NTCH), lambda bi: (bi, 0, 0))],
        out_shape=[jax.ShapeDtypeStruct((b, NE, SEQ), I32),
                   jax.ShapeDtypeStruct((b, NE, SEQ), F32),
                   jax.ShapeDtypeStruct((b, NE, NTCH), I32)],
        scratch_shapes=[pltpu.VMEM((NE, SEQ), F32)],
        compiler_params=_cparams(("parallel",)),
        name="routing",
    )(logits, tri)


GATHER_UNROLL = 8


def _gather_body(off_ref, h_ref, pos_ref, xg_ref, acc_ref):
    b = pl.program_id(0)
    e = pl.program_id(1)
    acc_ref[...] = jnp.zeros_like(acc_ref)
    crow = lax.broadcasted_iota(I32, (GW, TCH), 0)

    def chunks(i, _):
        for j in range(GATHER_UNROLL):
            c = i * GATHER_UNROLL + j
            off = off_ref[(b * NE + e) * NTCH + c]
            base = pl.multiple_of(jnp.minimum((off >> 3) << 3, CAP - TCH), SUBLANES)
            t0 = pl.multiple_of(c * TCH, TCH)
            rel = pos_ref[0, 0, pl.ds(c, 1), :] - base
            onehot = jnp.where(crow == rel, 1.0, 0.0).astype(BF16)
            acc_ref[pl.ds(base, GW), :] += _dot(onehot, h_ref[0, pl.ds(t0, TCH), :])
        return 0

    lax.fori_loop(0, NTCH // GATHER_UNROLL, chunks, 0)
    xg_ref[0, 0] = acc_ref[0:CAP, :].astype(BF16)


def _gather(offs_flat, h2, pos4):
    b = h2.shape[0]
    grid_spec = pltpu.PrefetchScalarGridSpec(
        num_scalar_prefetch=1,
        grid=(b, NE),
        in_specs=[pl.BlockSpec((1, SEQ, D), lambda bi, e, off: (bi, 0, 0)),
                  pl.BlockSpec((1, 1, NTCH, TCH), lambda bi, e, off: (bi, e, 0, 0))],
        out_specs=pl.BlockSpec((1, 1, CAP, D), lambda bi, e, off: (bi, e, 0, 0)),
        scratch_shapes=[pltpu.VMEM((CAP + SUBLANES, D), F32)],
    )
    return pl.pallas_call(
        _gather_body,
        grid_spec=grid_spec,
        out_shape=jax.ShapeDtypeStruct((b, NE, CAP, D), BF16),
        compiler_params=_cparams(("parallel", "arbitrary")),
        name="moe_gather",
    )(offs_flat, h2, pos4)


FFN_TM = 512
FFN_NF = 4
FFN_FC = DEXP // FFN_NF
assert FFN_FC * FFN_NF == DEXP and FFN_FC % BF16_ROWS == 0


def _ffn_body(xg_ref, wgt_ref, wut_ref, wd_ref, y_ref, acc_ref):
    j = pl.program_id(1)
    nb = xg_ref.shape[0]

    @pl.when(j == 0)
    def _():
        acc_ref[...] = jnp.zeros_like(acc_ref)

    wgt = wgt_ref[0].astype(BF16)
    wut = wut_ref[0].astype(BF16)
    wd = wd_ref[0].astype(BF16)
    for b in range(nb):
        for mb in range(CAP // FFN_TM):
            rows = slice(mb * FFN_TM, (mb + 1) * FFN_TM)
            xb = xg_ref[b, 0, rows, :]
            a = _dot(xb, wgt, _NT)
            u = _dot(xb, wut, _NT)
            h = (a * (1.0 / (1.0 + jnp.exp(-a))) * u).astype(BF16)
            acc_ref[b, rows, :] += _dot(h, wd)

    @pl.when(j == FFN_NF - 1)
    def _():
        for b in range(nb):
            y_ref[b, 0, 0:CAP, :] = acc_ref[b].astype(BF16)
            y_ref[b, 0, CAP:YROWS, :] = jnp.zeros((YROWS - CAP, D), BF16)


def _expert_ffn(xg, w_gate_t, w_up_t, w_down):
    b = xg.shape[0]
    wblk = pl.BlockSpec((1, FFN_FC, D), lambda e, j: (e, j, 0))
    return pl.pallas_call(
        _ffn_body,
        grid=(NE, FFN_NF),
        in_specs=[pl.BlockSpec((b, 1, CAP, D), lambda e, j: (0, e, 0, 0)), wblk, wblk, wblk],
        out_specs=pl.BlockSpec((b, 1, YROWS, D), lambda e, j: (0, e, 0, 0)),
        out_shape=jax.ShapeDtypeStruct((b, NE, YROWS, D), BF16),
        scratch_shapes=[pltpu.VMEM((b, CAP, D), F32)],
        compiler_params=_cparams(("parallel", "arbitrary")),
        name="expert_ffn",
    )(xg, w_gate_t, w_up_t, w_down)


def _combine_body(off_ref, y_ref, pos_ref, gate_ref, x1_ref, g2_ref, o_ref):
    b = pl.program_id(0)
    i = pl.program_id(1)
    crow = lax.broadcasted_iota(I32, (CW, TCH), 0)
    acc = jnp.zeros((TCH, D), F32)
    for e in range(NE):
        off = off_ref[(b * NE + e) * NTCH + i]
        base = pl.multiple_of(jnp.minimum((off >> 4) << 4, CAP - TCH), BF16_ROWS)
        rel = pos_ref[0, e:e + 1, :] - base
        w = jnp.where(crow == rel, gate_ref[0, e:e + 1, :], 0.0)
        yw = y_ref[0, e, pl.ds(base, CW), :]
        acc = acc + _dot(w.astype(BF16), yw, _TN)
    o_ref[0] = x1_ref[0] + g2_ref[0] * acc


def _combine(offs_flat, y, pos, gate, x1, g2r):
    b = x1.shape[0]
    grid_spec = pltpu.PrefetchScalarGridSpec(
        num_scalar_prefetch=1,
        grid=(b, NTCH),
        in_specs=[pl.BlockSpec((1, NE, YROWS, D), lambda bi, i, off: (bi, 0, 0, 0),
                               pipeline_mode=pl.Buffered(1)),
                  pl.BlockSpec((1, NE, TCH), lambda bi, i, off: (bi, 0, i)),
                  pl.BlockSpec((1, NE, TCH), lambda bi, i, off: (bi, 0, i)),
                  pl.BlockSpec((1, TCH, D), lambda bi, i, off: (bi, i, 0)),
                  pl.BlockSpec((1, 1, D), lambda bi, i, off: (bi, 0, 0))],
        out_specs=pl.BlockSpec((1, TCH, D), lambda bi, i, off: (bi, i, 0)),
    )
    return pl.pallas_call(
        _combine_body,
        grid_spec=grid_spec,
        out_shape=jax.ShapeDtypeStruct((b, SEQ, D), F32),
        compiler_params=_cparams(("parallel", "arbitrary")),
        name="moe_combine",
    )(offs_flat, y, pos, gate, x1, g2r)


def _np_split(m):
    hi = np.asarray(m, np.float64).astype(BF16)
    lo = (m - hi.astype(np.float64)).astype(BF16)
    return jnp.asarray(hi), jnp.asarray(lo)


@functools.lru_cache(maxsize=None)
def _dft_tables():
    a = np.arange(FN1, dtype=np.float64)
    ang = 2.0 * np.pi * np.outer(a, a) / FN1
    fr, fi = np.cos(ang), -np.sin(ang)
    half = SEQ // FN2
    lead_u = np.block([[fr[:, :half], -fi[:, :half]], [fi[:, :half], fr[:, :half]]])
    lead_k = np.concatenate([fr, fi], axis=0)
    fwd = np.block([[fr, -fi], [fi, fr]])
    inv = np.block([[fr, fi], [-fi, fr]])
    out = np.block([[fr[:half], fi[:half]], [-fi[:half], fr[:half]]])
    n2 = np.arange(FN2, dtype=np.float64)
    tw = 2.0 * np.pi * np.outer(a, n2) / FN
    twr = np.cos(tw).astype(np.float32)
    twi = (-np.sin(tw)).astype(np.float32)
    return dict(lead_u=lead_u, lead_k=lead_k, fwd=fwd, inv=inv, out=out, twr=twr, twi=twi)


@functools.lru_cache(maxsize=None)
def _filter_tables():
    L = SEQ
    n = np.arange(FN).reshape(FN1, FN2).T.reshape(-1)
    lag = np.where(n < L, n, FN - n)
    jc = np.minimum(lag, L - 1).astype(np.float64)
    t = (jc / (L - 1))[:, None]
    bands = (FEMB - 1) // 2
    w = 2.0 * np.pi * jc / L
    f = np.linspace(1e-4, bands - 1, bands)
    fw = w[:, None] * f[None, :]
    z = np.concatenate([t, np.cos(fw), -np.sin(fw), np.zeros((FN, FORD - FEMB))], axis=-1)
    mask = np.where(n == L, 0.0, 1.0)[:, None]
    fwd = np.where(n < L, 1.0, 0.0)[:, None]
    max_decay = math.log(DECAY_TARGET) / FAST_DECAY_PCT
    min_decay = math.log(DECAY_TARGET) / SLOW_DECAY_PCT
    negdelta = -np.abs(np.linspace(min_decay, max_decay, HY_W))[None, :]
    return tuple(np.asarray(a, np.float32) for a in (z, t, mask, fwd, negdelta))


@functools.lru_cache(maxsize=None)
def _rope_tables(n):
    rows = n // GRID_W
    row_id, col_id = np.meshgrid(np.arange(rows, dtype=np.float64), np.arange(GRID_W, dtype=np.float64), indexing="ij")
    quarter = HD // 4
    inv_freq = ROPE_THETA ** (-np.arange(quarter, dtype=np.float64) / quarter)
    ar = row_id.reshape(-1)[:, None] * inv_freq
    ac = col_id.reshape(-1)[:, None] * inv_freq
    cos = np.concatenate([np.cos(ar), np.cos(ar), np.cos(ac), np.cos(ac)], axis=-1)
    sin = np.concatenate([-np.sin(ar), np.sin(ar), -np.sin(ac), np.sin(ac)], axis=-1)
    reps = (1, LANES // HD)
    return np.tile(cos, reps).astype(np.float32), np.tile(sin, reps).astype(np.float32)


def _hyena_long_conv(u_rj, x2_rj, kern, abs_sum, bias):
    tb = _dft_tables()
    twr, twi = tb["twr"], tb["twi"]
    fwd = _np_split(tb["fwd"])
    scale = 1.0 / (abs_sum * float(FN))
    khat = _filter_spectrum(kern.reshape(FN2, FN1, HY_W), twr, twi, *_np_split(tb["lead_k"]), *fwd, scale)
    return _hyena_conv(u_rj, x2_rj, khat, twr, twi, _np_split(tb["lead_u"]), fwd, _np_split(tb["inv"]),
                       _np_split(tb["out"]), bias.reshape(1, HY_W))


def kernel(x, c, ctx, c_ctx, w_mod, b_mod, norm1_g, norm2_g, w_in, w_out, q_norm_g, k_norm_g,
           conv_w, conv_b, filt_w1, filt_b1, filt_w2, filt_b2, filt_w3, filt_freq, hyena_bias,
           w_router, w_gate, w_up, w_down):
    B = x.shape[0]
    assert x.shape == (B, SEQ, D) and B == 2 and ctx.shape == (B, CTX, D) and w_mod.shape[0] == 1
    l = 0

    cc = jnp.concatenate([c, c_ctx[None, :], jnp.zeros((SUBLANES - B - 1, D), F32)], axis=0)
    mod = _modulation(cc, w_mod[l], b_mod[l][None, :])
    sh1, sc1, g1, sh2, sc2, g2 = [mod[:, i * D:(i + 1) * D] for i in range(6)]
    lat = lambda m: m[:B, None, :]
    ctxrow = lambda m: jnp.broadcast_to(m[B:B + 1, None, :], (B, 1, D))

    w_in_bf = w_in[l].astype(BF16)
    gq2 = jnp.tile(q_norm_g[l][None, :], (1, LANES // HD))
    gk2 = jnp.tile(k_norm_g[l][None, :], (1, LANES // HD))
    bd = jnp.asarray(np.kron(np.eye(2 * LANES // HD), np.full((HD, HD), 1.0 / HD)), BF16)
    cos_t, sin_t = _rope_tables(SEQ)
    n1g = norm1_g[l][None, :]

    q, k, vt, p = _in_projection(x, n1g, lat(sh1), lat(sc1), w_in_bf, gq2, gk2, bd, cos_t, sin_t, 512)
    _, kc, vct, _ = _in_projection(ctx, n1g, ctxrow(sh1), ctxrow(sc1), w_in_bf, gq2, gk2, bd,
                                   jnp.ones((CTX, LANES), F32), jnp.zeros((CTX, LANES), F32), CTX)

    kch = jnp.concatenate([k, kc], axis=2).reshape(B, NKV, ATT_NCH, ATT_TK, HD)
    vt_all = jnp.concatenate([vt, vct], axis=3).reshape(B, NKV, HD, ATT_NCH, ATT_TK)
    ones_pad = jnp.concatenate([jnp.ones((B, NKV, ATT_NCH, 1, ATT_TK), BF16),
                                jnp.zeros((B, NKV, ATT_NCH, BF16_ROWS - 1, ATT_TK), BF16)], axis=3)
    vtch = jnp.concatenate([vt_all.transpose(0, 1, 3, 2, 4), ones_pad], axis=3)
    bound = (1.02 * HD * Q_SCALE) * jnp.max(jnp.abs(q_norm_g[l])) * jnp.max(jnp.abs(k_norm_g[l]))
    att = _attention(bound.reshape(1).astype(F32), q, kch, vtch)

    cw9 = conv_w[l].reshape(3, 3, HY_W).reshape(9, HY_W)
    cb3 = conv_b[l].reshape(3, HY_W)
    u_rj, x2_rj = _short_conv(p, cw9, cb3)
    ztab, ttab, mtab, ftab, negdelta = _filter_tables()
    w1p = jnp.concatenate([filt_w1[l], jnp.zeros((FORD - FEMB, FORD), F32)], axis=0)
    kern, abs_sum = _implicit_filter(ztab, ttab, mtab, ftab, w1p, filt_b1[l][None, :], filt_w2[l],
                                     filt_b2[l][None, :], filt_w3[l], filt_freq[l][None, :], negdelta)
    hy = _hyena_long_conv(u_rj, x2_rj, kern, abs_sum, hyena_bias[l])

    wr2 = jnp.concatenate(_split(w_router[l].T), axis=0)
    x1, h2, logits = _out_projection(att, hy, x, w_out[l].astype(BF16), lat(g1), norm2_g[l][None, :],
                                     lat(sh2), lat(sc2), wr2)

    tri = jnp.asarray(np.triu(np.ones((TCH, TCH))), BF16)
    pos, gate, offs = _routing(logits, tri)
    offs_flat = offs.reshape(-1)
    xg = _gather(offs_flat, h2, pos.reshape(B, NE, NTCH, TCH))
    y = _expert_ffn(xg, jnp.swapaxes(w_gate[l], 1, 2), jnp.swapaxes(w_up[l], 1, 2), w_down[l])
    return _combine(offs_flat, y, pos, gate, x1, lat(g2))
```

```python
import functools
import math

import numpy as np
import jax
import jax.numpy as jnp
from jax import lax
from jax.experimental import pallas as pl
from jax.experimental.pallas import tpu as pltpu

F32 = jnp.float32
BF16 = jnp.bfloat16
I32 = jnp.int32

D = 1024
SEQ = 8192
CTX = 256
GRID_W = 64
ATT_W = 512
HY_W = 512
HD = 64
NQ = 8
NKV = 2
QPK = NQ // NKV
KV_W = NKV * HD
IN_W = ATT_W + 2 * KV_W + 3 * HY_W
FEMB = 33
FORD = 64
NE = 16
CAP = 2 * SEQ // NE
DEXP = 2752
ROPE_THETA = 10000.0
EPS = 1e-6
DECAY_TARGET = 1e-2
FAST_DECAY_PCT = 0.3
SLOW_DECAY_PCT = 1.5

LANES = 128
SUBLANES = 8
BF16_ROWS = 16
VMEM_BYTES_V7X = 64 * 1024 * 1024
VMEM_LIMIT = VMEM_BYTES_V7X - 8 * 1024 * 1024

FN = 2 * SEQ
FN1 = 128
FN2 = 128

TCH = LANES
NTCH = SEQ // TCH
GW = TCH + SUBLANES
CW = TCH + BF16_ROWS
YROWS = CAP + BF16_ROWS


def _cparams(sem, vmem=None):
    return pltpu.CompilerParams(dimension_semantics=sem, vmem_limit_bytes=vmem or VMEM_LIMIT)


def _split(a):
    hi = a.astype(BF16)
    lo = (a - hi.astype(F32)).astype(BF16)
    return hi, lo


_NN = (((1,), (0,)), ((), ()))
_NT = (((1,), (1,)), ((), ()))
_TN = (((0,), (0,)), ((), ()))


def _dot(a, b, dn=_NN):
    return lax.dot_general(a, b, dn, preferred_element_type=F32)


def _dot3(a, b, dn=_NN):
    ah, al = _split(a)
    bh, bl = _split(b)
    return _dot(ah, bh, dn) + _dot(ah, bl, dn) + _dot(al, bh, dn)


def _mod_body(c_ref, w_ref, b_ref, o_ref):
    c = c_ref[...]
    s = c * (1.0 / (1.0 + jnp.exp(-c)))
    o_ref[...] = _dot3(s, w_ref[...]) + b_ref[...]


def _modulation(cc, w_mod, b_mod):
    n = w_mod.shape[1]
    return pl.pallas_call(
        _mod_body,
        grid=(n // D,),
        in_specs=[pl.BlockSpec((SUBLANES, D), lambda j: (0, 0)),
                  pl.BlockSpec((D, D), lambda j: (0, j)),
                  pl.BlockSpec((1, D), lambda j: (0, j))],
        out_specs=pl.BlockSpec((SUBLANES, D), lambda j: (0, j)),
        out_shape=jax.ShapeDtypeStruct((SUBLANES, n), F32),
        compiler_params=_cparams(("arbitrary",)),
        name="modulation",
    )(cc, w_mod, b_mod)


Q_SCALE = HD ** -0.5 * math.log2(math.e)


def _rms_mod(x, g, sh, sc):
    ms = jnp.mean(x * x, axis=-1, keepdims=True)
    return (x * lax.rsqrt(ms + EPS) * g) * (1.0 + sc) + sh


def _head_mean_square(t, bd):
    hi, lo = _split(t * t)
    return _dot(hi, bd) + _dot(lo, bd)


def _head_norm_rope(t, ms, g, cos, sin):
    tn = t * lax.rsqrt(ms + EPS) * g
    lane = lax.broadcasted_iota(I32, tn.shape, 1)
    sw = jnp.where((lane & 31) < 16, pltpu.roll(tn, LANES - 16, 1), pltpu.roll(tn, 16, 1))
    return tn * cos + sw * sin


def _proj_body(x_ref, g_ref, sh_ref, sc_ref, w_ref, gq_ref, gk_ref, bd_ref, cos_ref, sin_ref,
               q_ref, k_ref, v_ref, p_ref):
    h = _rms_mod(x_ref[0], g_ref[...], sh_ref[0], sc_ref[0])
    proj = _dot(h.astype(BF16), w_ref[...])
    bd = bd_ref[...]
    cos = cos_ref[...]
    sin = sin_ref[...]
    wide = 2 * LANES
    for j in range(ATT_W // wide):
        ms = _head_mean_square(proj[:, j * wide:(j + 1) * wide], bd)
        for i in range(2):
            sl = slice(j * wide + i * LANES, j * wide + (i + 1) * LANES)
            qj = _head_norm_rope(proj[:, sl], ms[:, i * LANES:(i + 1) * LANES], gq_ref[...], cos, sin)
            q_ref[0, :, sl] = (qj * Q_SCALE).astype(BF16)
    ms = _head_mean_square(proj[:, ATT_W:ATT_W + 2 * KV_W], bd)
    kk = _head_norm_rope(proj[:, ATT_W:ATT_W + KV_W], ms[:, 0:KV_W], gk_ref[...], cos, sin)
    vt = proj[:, ATT_W + KV_W:ATT_W + 2 * KV_W].T
    for g in range(NKV):
        k_ref[0, g] = kk[:, g * HD:(g + 1) * HD].astype(BF16)
        v_ref[0, g] = vt[g * HD:(g + 1) * HD, :].astype(BF16)
    p_ref[0] = proj[:, ATT_W + 2 * KV_W:]


def _in_projection(x, g1, sh, sc, w_in_bf, gq2, gk2, bd, cos_t, sin_t, tm):
    b, s, _ = x.shape
    row = lambda bi, i: (bi, 0, 0)
    tok = lambda bi, i: (bi, i, 0)
    const = lambda bi, i: (0, 0)
    return pl.pallas_call(
        _proj_body,
        grid=(b, s // tm),
        in_specs=[pl.BlockSpec((1, tm, D), tok),
                  pl.BlockSpec((1, D), const),
                  pl.BlockSpec((1, 1, D), row),
                  pl.BlockSpec((1, 1, D), row),
                  pl.BlockSpec((D, IN_W), const),
                  pl.BlockSpec((1, LANES), const),
                  pl.BlockSpec((1, LANES), const),
                  pl.BlockSpec((2 * LANES, 2 * LANES), const),
                  pl.BlockSpec((tm, LANES), lambda bi, i: (i, 0)),
                  pl.BlockSpec((tm, LANES), lambda bi, i: (i, 0))],
        out_specs=[pl.BlockSpec((1, tm, ATT_W), tok),
                   pl.BlockSpec((1, NKV, tm, HD), lambda bi, i: (bi, 0, i, 0)),
                   pl.BlockSpec((1, NKV, HD, tm), lambda bi, i: (bi, 0, 0, i)),
                   pl.BlockSpec((1, tm, 3 * HY_W), tok)],
        out_shape=[jax.ShapeDtypeStruct((b, s, ATT_W), BF16),
                   jax.ShapeDtypeStruct((b, NKV, s, HD), BF16),
                   jax.ShapeDtypeStruct((b, NKV, HD, s), BF16),
                   jax.ShapeDtypeStruct((b, s, 3 * HY_W), F32)],
        compiler_params=_cparams(("parallel", "parallel")),
        name="in_projection",
    )(x, g1, sh, sc, w_in_bf, gq2, gk2, bd, cos_t, sin_t)


ATT_TQ = 512
ATT_TK = 768
SK = SEQ + CTX
ATT_NCH = SK // ATT_TK


ATT_NQ = QPK * ATT_TQ
ATT_VR = HD + BF16_ROWS
assert ATT_NCH % 2 == 1
ATT_SHIFT_MAX = 120.0


def _attn_body(bound_ref, q_ref, k_ref, vt_ref, o_ref, s_ref, mx_ref, m_ref, acc_ref):
    qall = jnp.concatenate([q_ref[0, :, r * HD:(r + 1) * HD] for r in range(QPK)], axis=0)
    acc_ref[...] = jnp.zeros_like(acc_ref)
    bound = bound_ref[0]
    fixed_shift = 2.0 * bound <= ATT_SHIFT_MAX

    def finish():
        out = acc_ref[0:HD, :] * (1.0 / acc_ref[HD:HD + 1, :])
        for r in range(QPK):
            o_ref[0, :, r * HD:(r + 1) * HD] = out[:, r * ATT_TQ:(r + 1) * ATT_TQ].T.astype(BF16)

    @pl.when(fixed_shift)
    def _():
        def chunk(c, _):
            s = _dot(k_ref[0, 0, c], qall, _NT)
            acc_ref[...] += _dot(vt_ref[0, 0, c], jnp.exp2(s - bound).astype(BF16))
            return 0

        lax.fori_loop(0, ATT_NCH, chunk, 0, unroll=2)
        finish()

    @pl.when(jnp.logical_not(fixed_shift))
    def _():
        m_ref[...] = jnp.full(m_ref.shape, -1e30, F32)

        def scores(c, slot):
            s = _dot(k_ref[0, 0, c], qall, _NT)
            s_ref[slot] = s
            mx_ref[slot] = jnp.max(s, axis=0, keepdims=True)

        def update(c, slot):
            m_old = m_ref[...]
            m_new = jnp.maximum(m_old, mx_ref[slot])
            p = jnp.exp2(s_ref[slot] - m_new).astype(BF16)
            acc_ref[...] = jnp.exp2(m_old - m_new) * acc_ref[...] + _dot(vt_ref[0, 0, c], p)
            m_ref[...] = m_new

        scores(0, 0)

        def pair(i, _):
            c = 2 * i
            scores(c + 1, 1)
            update(c, 0)
            scores(c + 2, 0)
            update(c + 1, 1)
            return 0

        lax.fori_loop(0, ATT_NCH // 2, pair, 0)
        update(ATT_NCH - 1, 0)
        finish()


def _attention(bound, q, kch, vtch):
    b = q.shape[0]
    grid_spec = pltpu.PrefetchScalarGridSpec(
        num_scalar_prefetch=1,
        grid=(b, NKV, SEQ // ATT_TQ),
        in_specs=[pl.BlockSpec((1, ATT_TQ, QPK * HD), lambda bi, g, i, bd: (bi, i, g)),
                  pl.BlockSpec((1, 1, ATT_NCH, ATT_TK, HD), lambda bi, g, i, bd: (bi, g, 0, 0, 0)),
                  pl.BlockSpec((1, 1, ATT_NCH, ATT_VR, ATT_TK), lambda bi, g, i, bd: (bi, g, 0, 0, 0))],
        out_specs=pl.BlockSpec((1, ATT_TQ, QPK * HD), lambda bi, g, i, bd: (bi, i, g)),
        scratch_shapes=[pltpu.VMEM((2, ATT_TK, ATT_NQ), F32), pltpu.VMEM((2, 1, ATT_NQ), F32),
                        pltpu.VMEM((1, ATT_NQ), F32), pltpu.VMEM((ATT_VR, ATT_NQ), F32)],
    )
    return pl.pallas_call(
        _attn_body,
        grid_spec=grid_spec,
        out_shape=jax.ShapeDtypeStruct((b, SEQ, ATT_W), BF16),
        compiler_params=_cparams(("parallel", "parallel", "parallel")),
        name="attention",
    )(bound, q, kch, vtch)


SC_TM = 2048
SC_J = SC_TM // FN2


def _sconv_body(m1, a1, n1, m2, a2, n2, m3, a3, n3, w_ref, b_ref, u_ref, x2_ref):
    i = pl.program_id(1)
    last = pl.num_programs(1) - 1
    rows = lax.broadcasted_iota(I32, (SC_TM, HY_W), 0)

    def conv(main, prev, nxt, g):
        x = main[0]
        pr = jnp.where(i > 0, prev[0, SUBLANES - 1:SUBLANES, :], 0.0)
        nx = jnp.where(i < last, nxt[0, 0:1, :], 0.0)
        xm = jnp.where(rows == 0, pr, pltpu.roll(x, 1, 0))
        xp = jnp.where(rows == SC_TM - 1, nx, pltpu.roll(x, SC_TM - 1, 0))
        return (w_ref[g:g + 1, :] * xm + w_ref[3 + g:4 + g, :] * x + w_ref[6 + g:7 + g, :] * xp
                + b_ref[g:g + 1, :])

    def to_rj(t):
        return jnp.swapaxes(t.reshape(SC_J, FN2, HY_W), 0, 1).astype(BF16)

    x1 = conv(m1, a1, n1, 0)
    x2 = conv(m2, a2, n2, 1)
    v = conv(m3, a3, n3, 2)
    u_ref[0] = to_rj(v * x1)
    x2_ref[0] = to_rj(x2)


def _short_conv(p, cw9, cb3):
    b = p.shape[0]
    nblk8 = SEQ // SUBLANES
    step8 = SC_TM // SUBLANES
    specs = []
    for g in range(3):
        specs += [pl.BlockSpec((1, SC_TM, HY_W), lambda bi, i, g=g: (bi, i, g)),
                  pl.BlockSpec((1, SUBLANES, HY_W), lambda bi, i, g=g: (bi, jnp.maximum(i * step8 - 1, 0), g)),
                  pl.BlockSpec((1, SUBLANES, HY_W), lambda bi, i, g=g: (bi, jnp.minimum((i + 1) * step8, nblk8 - 1), g))]
    specs += [pl.BlockSpec((9, HY_W), lambda bi, i: (0, 0)), pl.BlockSpec((3, HY_W), lambda bi, i: (0, 0))]
    out = pl.BlockSpec((1, FN2, SC_J, HY_W), lambda bi, i: (bi, 0, i, 0))
    return pl.pallas_call(
        _sconv_body,
        grid=(b, SEQ // SC_TM),
        in_specs=specs,
        out_specs=[out, out],
        out_shape=[jax.ShapeDtypeStruct((b, FN2, SEQ // FN2, HY_W), BF16)] * 2,
        compiler_params=_cparams(("parallel", "parallel")),
        name="short_conv",
    )(p, p, p, p, p, p, p, p, p, cw9, cb3)


FILT_TR = 1024


def _filter_body(z_ref, t_ref, msk_ref, fwd_ref, w1_ref, b1_ref, w2_ref, b2_ref, w3_ref, fr_ref, dl_ref,
                 k_ref, s_ref):
    fr = fr_ref[...]
    h = jnp.sin(fr * (_dot3(z_ref[...], w1_ref[...]) + b1_ref[...]))
    h = jnp.sin(fr * (_dot3(h, w2_ref[...]) + b2_ref[...]))
    h = _dot3(h, w3_ref[...])
    h = jnp.where(fwd_ref[...] > 0.5, h[:, :HY_W], h[:, HY_W:])
    kern = h * jnp.exp(t_ref[...] * dl_ref[...]) * msk_ref[...]
    k_ref[...] = kern

    @pl.when(pl.program_id(0) == 0)
    def _():
        s_ref[...] = jnp.zeros_like(s_ref)

    s_ref[...] += jnp.sum(jnp.abs(kern), axis=0, keepdims=True)


def _implicit_filter(ztab, ttab, mtab, ftab, w1p, b1, w2, b2, w3, freq, negdelta):
    rowblk = lambda i: (i, 0)
    const = lambda i: (0, 0)
    col = pl.BlockSpec((FILT_TR, 1), rowblk)
    return pl.pallas_call(
        _filter_body,
        grid=(FN // FILT_TR,),
        in_specs=[pl.BlockSpec((FILT_TR, FORD), rowblk), col, col, col,
                  pl.BlockSpec((FORD, FORD), const),
                  pl.BlockSpec((1, FORD), const),
                  pl.BlockSpec((FORD, FORD), const),
                  pl.BlockSpec((1, FORD), const),
                  pl.BlockSpec((FORD, 2 * HY_W), const),
                  pl.BlockSpec((1, FORD), const),
                  pl.BlockSpec((1, HY_W), const)],
        out_specs=[pl.BlockSpec((FILT_TR, HY_W), rowblk),
                   pl.BlockSpec((1, HY_W), const)],
        out_shape=[jax.ShapeDtypeStruct((FN, HY_W), F32), jax.ShapeDtypeStruct((1, HY_W), F32)],
        compiler_params=_cparams(("arbitrary",)),
        name="implicit_filter",
    )(ztab, ttab, mtab, ftab, w1p, b1, w2, b2, w3, freq, negdelta)


DFT_G = 4
DFT_KB = 16
DFT_NP = FN1 // DFT_KB
DFT_UNROLL = 4


def _dot2c(fh, fl, zb):
    return _dot(fh, zb) + _dot(fl, zb)


def _lead_stage(src, fh, fl, dst_ref):
    def group(rg, _):
        r0 = rg * DFT_G
        rhs = jnp.concatenate([src(r0 + g) for g in range(DFT_G)], axis=1)
        blk = _dot2c(fh, fl, rhs)
        for g in range(DFT_G):
            dst_ref[r0 + g] = blk[:, g * LANES:(g + 1) * LANES]
        return 0

    lax.fori_loop(0, FN2 // DFT_G, group, 0, unroll=DFT_UNROLL)


def _lead_phase(src, lh_ref, ll_ref, p_ref, q_ref):
    for h in range(2):
        rows = slice(h * FN1, (h + 1) * FN1)
        _lead_stage(src, lh_ref[rows, :], ll_ref[rows, :], p_ref)
        q_ref[rows] = jnp.swapaxes(p_ref[...], 0, 1)


def _lane_cat(xs):
    return jnp.concatenate(xs, axis=1)


def _mid_forward(q_ref, k0, tr_ref, ti_ref, fh, fl, half):
    tr_t, ti_t = tr_ref[...].T, ti_ref[...].T
    brs, bis, trs, tis = [], [], [], []
    for g in range(DFT_G):
        jj = half * DFT_G + g
        ar, ai = q_ref[k0 + jj], q_ref[FN1 + k0 + jj]
        tr = jnp.broadcast_to(tr_t[:, jj:jj + 1], (FN2, LANES))
        ti = jnp.broadcast_to(ti_t[:, jj:jj + 1], (FN2, LANES))
        brs.append(ar * tr - ai * ti)
        bis.append(ar * ti + ai * tr)
        trs.append(tr)
        tis.append(ti)
    b = jnp.concatenate([_lane_cat(brs), _lane_cat(bis)], axis=0).astype(BF16)
    return _dot2c(fh, fl, b), _lane_cat(trs), _lane_cat(tis)


def _spectrum_body(k_ref, tr_ref, ti_ref, lh_ref, ll_ref, fh_ref, fl_ref, sc_ref, o_ref, p_ref, q_ref):
    ph = pl.program_id(1)

    @pl.when(ph == 0)
    def _():
        _lead_phase(lambda r: k_ref[r].astype(BF16), lh_ref, ll_ref, p_ref, q_ref)

    @pl.when(ph > 0)
    def _():
        k0 = (ph - 1) * DFT_KB
        sc = _lane_cat([sc_ref[...]] * DFT_G)
        for half in range(DFT_KB // DFT_G):
            x, _, _ = _mid_forward(q_ref, k0, tr_ref, ti_ref, fh_ref[...], fl_ref[...], half)
            x = x * sc
            for g in range(DFT_G):
                lanes = slice(g * LANES, (g + 1) * LANES)
                o_ref[0, half * DFT_G + g] = x[:FN2, lanes]
                o_ref[1, half * DFT_G + g] = x[FN2:, lanes]


def _mid_index(ph):
    return jnp.clip(ph - 1, 0, DFT_NP - 1)


def _filter_spectrum(kern_rj, twr, twi, lh, ll, fh, fl, scale):
    const = lambda c, ph: (0, 0)
    tw = pl.BlockSpec((DFT_KB, FN2), lambda c, ph: (_mid_index(ph), 0))
    return pl.pallas_call(
        _spectrum_body,
        grid=(HY_W // LANES, DFT_NP + 1),
        in_specs=[pl.BlockSpec((FN2, FN1, LANES), lambda c, ph: (0, 0, c), pipeline_mode=pl.Buffered(1)),
                  tw, tw,
                  pl.BlockSpec(lh.shape, const), pl.BlockSpec(ll.shape, const),
                  pl.BlockSpec(fh.shape, const), pl.BlockSpec(fl.shape, const),
                  pl.BlockSpec((1, LANES), lambda c, ph: (0, c))],
        out_specs=pl.BlockSpec((2, DFT_KB, FN2, LANES), lambda c, ph: (0, _mid_index(ph), 0, c)),
        out_shape=jax.ShapeDtypeStruct((2, FN1, FN2, HY_W), F32),
        scratch_shapes=[pltpu.VMEM((FN2, FN1, LANES), F32), pltpu.VMEM((2 * FN1, FN2, LANES), F32)],
        compiler_params=_cparams(("parallel", "arbitrary")),
        name="filter_spectrum",
    )(kern_rj, twr, twi, lh, ll, fh, fl, scale)


def _hconv_body(u_ref, x2_ref, kh_ref, tr_ref, ti_ref, lh_ref, ll_ref, fh_ref, fl_ref, gh_ref, gl_ref,
                oh_ref, ol_ref, bias_ref, o_ref, p_ref, q_ref):
    ph = pl.program_id(1)

    def both(ref, r):
        return jnp.concatenate([ref[0, r], ref[1, r]], axis=0)

    @pl.when(ph == 0)
    def _():
        _lead_phase(lambda r: both(u_ref, r), lh_ref, ll_ref, p_ref, q_ref)

    @pl.when((ph > 0) & (ph <= DFT_NP))
    def _():
        k0 = (ph - 1) * DFT_KB
        for half in range(DFT_KB // DFT_G):
            x, tr, ti = _mid_forward(q_ref, k0, tr_ref, ti_ref, fh_ref[...], fl_ref[...], half)
            xr, xi = x[:FN2], x[FN2:]
            kr = _lane_cat([kh_ref[0, half * DFT_G + g] for g in range(DFT_G)])
            ki = _lane_cat([kh_ref[1, half * DFT_G + g] for g in range(DFT_G)])
            y = jnp.concatenate([xr * kr - xi * ki, xr * ki + xi * kr], axis=0).astype(BF16)
            c = _dot2c(gh_ref[...], gl_ref[...], y)
            cr, ci = c[:FN2], c[FN2:]
            dr = cr * tr + ci * ti
            di = ci * tr - cr * ti
            for g in range(DFT_G):
                lanes = slice(g * LANES, (g + 1) * LANES)
                q_ref[k0 + half * DFT_G + g] = dr[:, lanes]
                q_ref[FN1 + k0 + half * DFT_G + g] = di[:, lanes]

    @pl.when(ph == DFT_NP + 1)
    def _():
        bias = bias_ref[...]
        p_ref[...] = jnp.swapaxes(q_ref[0:FN1], 0, 1)
        _lead_stage(lambda r: p_ref[r].astype(BF16), oh_ref[:, 0:FN1], ol_ref[:, 0:FN1], q_ref)
        p_ref[...] = jnp.swapaxes(q_ref[FN1:2 * FN1], 0, 1)
        oh2, ol2 = oh_ref[:, FN1:2 * FN1], ol_ref[:, FN1:2 * FN1]

        def group(rg, _):
            r0 = rg * DFT_G
            rhs = _lane_cat([p_ref[r0 + g].astype(BF16) for g in range(DFT_G)])
            blk = _dot2c(oh2, ol2, rhs)
            for g in range(DFT_G):
                r = r0 + g
                y = q_ref[r] + blk[:, g * LANES:(g + 1) * LANES]
                q_ref[r] = (y + both(u_ref, r).astype(F32) * bias) * both(x2_ref, r).astype(F32)
            return 0

        lax.fori_loop(0, FN2 // DFT_G, group, 0, unroll=DFT_UNROLL)
        p_ref[...] = jnp.swapaxes(q_ref[0:FN2], 0, 1)
        nj = SEQ // FN2
        for b in range(2):
            o_ref[b] = p_ref[b * nj:(b + 1) * nj].reshape(SEQ, LANES).astype(BF16)


def _hyena_conv(u_rj, x2_rj, khat, twr, twi, lead, fwd, inv, out, bias):
    const = lambda c, ph: (0, 0)
    nj = SEQ // FN2
    tw = pl.BlockSpec((DFT_KB, FN2), lambda c, ph: (_mid_index(ph), 0))
    sig = pl.BlockSpec((2, FN2, nj, LANES), lambda c, ph: (0, 0, 0, c), pipeline_mode=pl.Buffered(1))
    mats = [m for pair in (lead, fwd, inv, out) for m in pair]
    return pl.pallas_call(
        _hconv_body,
        grid=(HY_W // LANES, DFT_NP + 2),
        in_specs=[sig, sig,
                  pl.BlockSpec((2, DFT_KB, FN2, LANES), lambda c, ph: (0, _mid_index(ph), 0, c)),
                  tw, tw] + [pl.BlockSpec(m.shape, const) for m in mats]
                 + [pl.BlockSpec((1, LANES), lambda c, ph: (0, c))],
        out_specs=pl.BlockSpec((2, SEQ, LANES), lambda c, ph: (0, 0, c)),
        out_shape=jax.ShapeDtypeStruct((2, SEQ, HY_W), BF16),
        scratch_shapes=[pltpu.VMEM((FN2, FN1, LANES), F32), pltpu.VMEM((2 * FN1, FN2, LANES), F32)],
        compiler_params=_cparams(("parallel", "arbitrary")),
        name="hyena_conv",
    )(u_rj, x2_rj, khat, twr, twi, *mats, bias)


OP_TM = 512


def _outproj_body(att_ref, hy_ref, x_ref, w_ref, g1_ref, n2_ref, sh_ref, sc_ref, wr_ref,
                  x1_ref, h2_ref, lg_ref):
    a = jnp.concatenate([att_ref[0], hy_ref[0]], axis=1)
    x1 = x_ref[0] + g1_ref[0] * _dot(a, w_ref[...])
    x1_ref[0] = x1
    h2 = _rms_mod(x1, n2_ref[...], sh_ref[0], sc_ref[0])
    hh, hl = _split(h2)
    h2_ref[0] = hh
    wr = wr_ref[...]
    both = _dot(wr, hh, _NT)
    lg_ref[0] = both[0:NE] + both[NE:2 * NE] + _dot(wr[0:NE], hl, _NT)


def _out_projection(att, hy, x, w_out_bf, g1r, n2g, sh2, sc2, wr2):
    b = x.shape[0]
    tok = lambda bi, i: (bi, i, 0)
    row = lambda bi, i: (bi, 0, 0)
    const = lambda bi, i: (0, 0)
    return pl.pallas_call(
        _outproj_body,
        grid=(b, SEQ // OP_TM),
        in_specs=[pl.BlockSpec((1, OP_TM, ATT_W), tok),
                  pl.BlockSpec((1, OP_TM, HY_W), tok),
                  pl.BlockSpec((1, OP_TM, D), tok),
                  pl.BlockSpec((ATT_W + HY_W, D), const),
                  pl.BlockSpec((1, 1, D), row),
                  pl.BlockSpec((1, D), const),
                  pl.BlockSpec((1, 1, D), row),
                  pl.BlockSpec((1, 1, D), row),
                  pl.BlockSpec((2 * NE, D), const)],
        out_specs=[pl.BlockSpec((1, OP_TM, D), tok),
                   pl.BlockSpec((1, OP_TM, D), tok),
                   pl.BlockSpec((1, NE, OP_TM), lambda bi, i: (bi, 0, i))],
        out_shape=[jax.ShapeDtypeStruct((b, SEQ, D), F32),
                   jax.ShapeDtypeStruct((b, SEQ, D), BF16),
                   jax.ShapeDtypeStruct((b, NE, SEQ), F32)],
        compiler_params=_cparams(("parallel", "parallel")),
        name="out_projection",
    )(att, hy, x, w_out_bf, g1r, n2g, sh2, sc2, wr2)


def _routing_body(lg_ref, tri_ref, pos_ref, gate_ref, off_ref, cs_ref):
    lg = lg_ref[0]
    e = jnp.exp(lg - jnp.max(lg, axis=0, keepdims=True))
    aff = e / jnp.sum(e, axis=0, keepdims=True)
    gate_ref[0] = aff
    def count_ge(t):
        return jnp.sum(jnp.where(aff >= t, 1.0, 0.0), axis=1, keepdims=True)

    def bisect(i, thr):
        cand = thr | (jnp.int32(1) << (30 - i))
        return jnp.where(count_ge(pltpu.bitcast(cand, F32)) >= float(CAP), cand, thr)

    thr = lax.fori_loop(0, 31, bisect, jnp.zeros((NE, 1), I32))
    lo = pltpu.bitcast(thr, F32)
    hi = jnp.maximum(pltpu.bitcast(thr + 1, F32), jnp.finfo(F32).tiny)

    def refine(i, c):
        lo, hi = c
        mid = lo + (hi - lo) * 0.5
        ok = count_ge(mid) >= float(CAP)
        return jnp.where(ok, mid, lo), jnp.where(ok, hi, mid)

    lo, hi = lax.fori_loop(0, 32, refine, (lo, hi))
    gt = aff >= hi
    eq = (aff >= lo) & jnp.logical_not(gt)
    need = float(CAP) - jnp.sum(jnp.where(gt, 1.0, 0.0), axis=1, keepdims=True)
    tri = tri_ref[...]

    def excl_cumsum(mask_f, record_offsets):
        carry = jnp.zeros((NE, 1), F32)
        for c in range(NTCH):
            sl = slice(c * TCH, (c + 1) * TCH)
            m = mask_f[:, sl]
            inc = _dot(m.astype(BF16), tri)
            cs_ref[:, sl] = inc - m + carry
            if record_offsets:
                off_ref[0, :, c:c + 1] = carry.astype(I32)
            carry = carry + inc[:, TCH - 1:TCH]
        return cs_ref[...]

    eq_rank = excl_cumsum(jnp.where(eq, 1.0, 0.0), False)
    sel = gt | (eq & (eq_rank < need))
    pos = excl_cumsum(jnp.where(sel, 1.0, 0.0), True)
    pos_ref[0] = jnp.where(sel, pos.astype(I32), -1)


def _routing(logits, tri):
    b = logits.shape[0]
    blk = pl.BlockSpec((1, NE, SEQ), lambda bi: (bi, 0, 0))
    return pl.pallas_call(
        _routing_body,
        grid=(b,),
        in_specs=[blk, pl.BlockSpec((TCH, TCH), lambda bi: (0, 0))],
        out_specs=[blk, blk, pl.BlockSpec((1, NE, NTCH), lambda bi: (bi, 0, 0))],
        out_shape=[jax.ShapeDtypeStruct((b, NE, SEQ), I32),
                   jax.ShapeDtypeStruct((b, NE, SEQ), F32),
                   jax.ShapeDtypeStruct((b, NE, NTCH), I32)],
        scratch_shapes=[pltpu.VMEM((NE, SEQ), F32)],
        compiler_params=_cparams(("parallel",)),
        name="routing",
    )(logits, tri)


GATHER_UNROLL = 8


GW_SMALL = 48


def _gather_body(off_ref, h_ref, pos_ref, xg_ref, acc_ref):
    b = pl.program_id(0)
    e = pl.program_id(1)
    row0 = (b * NE + e) * NTCH
    acc_ref[...] = jnp.zeros_like(acc_ref)

    def count(c, most):
        nxt = jnp.where(c + 1 < NTCH, off_ref[row0 + jnp.minimum(c + 1, NTCH - 1)], CAP)
        return jnp.maximum(most, nxt - off_ref[row0 + c])

    most = lax.fori_loop(0, NTCH, count, 0)

    def sweep(window):
        crow = lax.broadcasted_iota(I32, (window, TCH), 0)

        def chunks(i, _):
            for j in range(GATHER_UNROLL):
                c = i * GATHER_UNROLL + j
                off = off_ref[row0 + c]
                base = pl.multiple_of(jnp.minimum((off >> 3) << 3, CAP + SUBLANES - window), SUBLANES)
                t0 = pl.multiple_of(c * TCH, TCH)
                rel = pos_ref[0, 0, pl.ds(c, 1), :] - base
                onehot = jnp.where(crow == rel, 1.0, 0.0).astype(BF16)
                acc_ref[pl.ds(base, window), :] += _dot(onehot, h_ref[0, pl.ds(t0, TCH), :])
            return 0

        lax.fori_loop(0, NTCH // GATHER_UNROLL, chunks, 0)

    @pl.when(most <= GW_SMALL - SUBLANES)
    def _():
        sweep(GW_SMALL)

    @pl.when(most > GW_SMALL - SUBLANES)
    def _():
        sweep(GW)

    xg_ref[0, 0] = acc_ref[0:CAP, :].astype(BF16)


def _gather(offs_flat, h2, pos4):
    b = h2.shape[0]
    grid_spec = pltpu.PrefetchScalarGridSpec(
        num_scalar_prefetch=1,
        grid=(b, NE),
        in_specs=[pl.BlockSpec((1, SEQ, D), lambda bi, e, off: (bi, 0, 0)),
                  pl.BlockSpec((1, 1, NTCH, TCH), lambda bi, e, off: (bi, e, 0, 0))],
        out_specs=pl.BlockSpec((1, 1, CAP, D), lambda bi, e, off: (bi, e, 0, 0)),
        scratch_shapes=[pltpu.VMEM((CAP + SUBLANES, D), F32)],
    )
    return pl.pallas_call(
        _gather_body,
        grid_spec=grid_spec,
        out_shape=jax.ShapeDtypeStruct((b, NE, CAP, D), BF16),
        compiler_params=_cparams(("parallel", "arbitrary")),
        name="moe_gather",
    )(offs_flat, h2, pos4)


FFN_TM = 512
FFN_NF = 4
FFN_FC = DEXP // FFN_NF
assert FFN_FC * FFN_NF == DEXP and FFN_FC % BF16_ROWS == 0


def _ffn_body(xg_ref, wgt_ref, wut_ref, wd_ref, y_ref, acc_ref):
    j = pl.program_id(1)
    nb = xg_ref.shape[0]

    @pl.when(j == 0)
    def _():
        acc_ref[...] = jnp.zeros_like(acc_ref)

    wgt = wgt_ref[0].astype(BF16)
    wut = wut_ref[0].astype(BF16)
    wd = wd_ref[0].astype(BF16)
    for b in range(nb):
        for mb in range(CAP // FFN_TM):
            rows = slice(mb * FFN_TM, (mb + 1) * FFN_TM)
            xb = xg_ref[b, 0, rows, :]
            a = _dot(xb, wgt, _NT)
            u = _dot(xb, wut, _NT)
            h = (a * (1.0 / (1.0 + jnp.exp(-a))) * u).astype(BF16)
            acc_ref[b, rows, :] += _dot(h, wd)

    @pl.when(j == FFN_NF - 1)
    def _():
        for b in range(nb):
            y_ref[b, 0, 0:CAP, :] = acc_ref[b].astype(BF16)
            y_ref[b, 0, CAP:YROWS, :] = jnp.zeros((YROWS - CAP, D), BF16)


def _expert_ffn(xg, w_gate_t, w_up_t, w_down):
    b = xg.shape[0]
    wblk = pl.BlockSpec((1, FFN_FC, D), lambda e, j: (e, j, 0))
    return pl.pallas_call(
        _ffn_body,
        grid=(NE, FFN_NF),
        in_specs=[pl.BlockSpec((b, 1, CAP, D), lambda e, j: (0, e, 0, 0)), wblk, wblk, wblk],
        out_specs=pl.BlockSpec((b, 1, YROWS, D), lambda e, j: (0, e, 0, 0)),
        out_shape=jax.ShapeDtypeStruct((b, NE, YROWS, D), BF16),
        scratch_shapes=[pltpu.VMEM((b, CAP, D), F32)],
        compiler_params=_cparams(("parallel", "arbitrary")),
        name="expert_ffn",
    )(xg, w_gate_t, w_up_t, w_down)


def _combine_body(off_ref, y_ref, pos_ref, gate_ref, x1_ref, g2_ref, o_ref):
    b = pl.program_id(0)
    i = pl.program_id(1)
    crow = lax.broadcasted_iota(I32, (CW, TCH), 0)
    acc = jnp.zeros((TCH, D), F32)
    for e in range(NE):
        off = off_ref[(b * NE + e) * NTCH + i]
        base = pl.multiple_of(jnp.minimum((off >> 4) << 4, CAP - TCH), BF16_ROWS)
        rel = pos_ref[0, e:e + 1, :] - base
        w = jnp.where(crow == rel, gate_ref[0, e:e + 1, :], 0.0)
        yw = y_ref[0, e, pl.ds(base, CW), :]
        acc = acc + _dot(w.astype(BF16), yw, _TN)
    o_ref[0] = x1_ref[0] + g2_ref[0] * acc


def _combine(offs_flat, y, pos, gate, x1, g2r):
    b = x1.shape[0]
    grid_spec = pltpu.PrefetchScalarGridSpec(
        num_scalar_prefetch=1,
        grid=(b, NTCH),
        in_specs=[pl.BlockSpec((1, NE, YROWS, D), lambda bi, i, off: (bi, 0, 0, 0),
                               pipeline_mode=pl.Buffered(1)),
                  pl.BlockSpec((1, NE, TCH), lambda bi, i, off: (bi, 0, i)),
                  pl.BlockSpec((1, NE, TCH), lambda bi, i, off: (bi, 0, i)),
                  pl.BlockSpec((1, TCH, D), lambda bi, i, off: (bi, i, 0)),
                  pl.BlockSpec((1, 1, D), lambda bi, i, off: (bi, 0, 0))],
        out_specs=pl.BlockSpec((1, TCH, D), lambda bi, i, off: (bi, i, 0)),
    )
    return pl.pallas_call(
        _combine_body,
        grid_spec=grid_spec,
        out_shape=jax.ShapeDtypeStruct((b, SEQ, D), F32),
        compiler_params=_cparams(("parallel", "arbitrary")),
        name="moe_combine",
    )(offs_flat, y, pos, gate, x1, g2r)


def _np_split(m):
    hi = np.asarray(m, np.float64).astype(BF16)
    lo = (m - hi.astype(np.float64)).astype(BF16)
    return jnp.asarray(hi), jnp.asarray(lo)


@functools.lru_cache(maxsize=None)
def _dft_tables():
    a = np.arange(FN1, dtype=np.float64)
    ang = 2.0 * np.pi * np.outer(a, a) / FN1
    fr, fi = np.cos(ang), -np.sin(ang)
    half = SEQ // FN2
    lead_u = np.block([[fr[:, :half], -fi[:, :half]], [fi[:, :half], fr[:, :half]]])
    lead_k = np.concatenate([fr, fi], axis=0)
    fwd = np.block([[fr, -fi], [fi, fr]])
    inv = np.block([[fr, fi], [-fi, fr]])
    out = np.block([[fr[:half], fi[:half]], [-fi[:half], fr[:half]]])
    n2 = np.arange(FN2, dtype=np.float64)
    tw = 2.0 * np.pi * np.outer(a, n2) / FN
    twr = np.cos(tw).astype(np.float32)
    twi = (-np.sin(tw)).astype(np.float32)
    return dict(lead_u=lead_u, lead_k=lead_k, fwd=fwd, inv=inv, out=out, twr=twr, twi=twi)


@functools.lru_cache(maxsize=None)
def _filter_tables():
    L = SEQ
    n = np.arange(FN).reshape(FN1, FN2).T.reshape(-1)
    lag = np.where(n < L, n, FN - n)
    jc = np.minimum(lag, L - 1).astype(np.float64)
    t = (jc / (L - 1))[:, None]
    bands = (FEMB - 1) // 2
    w = 2.0 * np.pi * jc / L
    f = np.linspace(1e-4, bands - 1, bands)
    fw = w[:, None] * f[None, :]
    z = np.concatenate([t, np.cos(fw), -np.sin(fw), np.zeros((FN, FORD - FEMB))], axis=-1)
    mask = np.where(n == L, 0.0, 1.0)[:, None]
    fwd = np.where(n < L, 1.0, 0.0)[:, None]
    max_decay = math.log(DECAY_TARGET) / FAST_DECAY_PCT
    min_decay = math.log(DECAY_TARGET) / SLOW_DECAY_PCT
    negdelta = -np.abs(np.linspace(min_decay, max_decay, HY_W))[None, :]
    return tuple(np.asarray(a, np.float32) for a in (z, t, mask, fwd, negdelta))


@functools.lru_cache(maxsize=None)
def _rope_tables(n):
    rows = n // GRID_W
    row_id, col_id = np.meshgrid(np.arange(rows, dtype=np.float64), np.arange(GRID_W, dtype=np.float64), indexing="ij")
    quarter = HD // 4
    inv_freq = ROPE_THETA ** (-np.arange(quarter, dtype=np.float64) / quarter)
    ar = row_id.reshape(-1)[:, None] * inv_freq
    ac = col_id.reshape(-1)[:, None] * inv_freq
    cos = np.concatenate([np.cos(ar), np.cos(ar), np.cos(ac), np.cos(ac)], axis=-1)
    sin = np.concatenate([-np.sin(ar), np.sin(ar), -np.sin(ac), np.sin(ac)], axis=-1)
    reps = (1, LANES // HD)
    return np.tile(cos, reps).astype(np.float32), np.tile(sin, reps).astype(np.float32)


def _hyena_long_conv(u_rj, x2_rj, kern, abs_sum, bias):
    tb = _dft_tables()
    twr, twi = tb["twr"], tb["twi"]
    fwd = _np_split(tb["fwd"])
    scale = 1.0 / (abs_sum * float(FN))
    khat = _filter_spectrum(kern.reshape(FN2, FN1, HY_W), twr, twi, *_np_split(tb["lead_k"]), *fwd, scale)
    return _hyena_conv(u_rj, x2_rj, khat, twr, twi, _np_split(tb["lead_u"]), fwd, _np_split(tb["inv"]),
                       _np_split(tb["out"]), bias.reshape(1, HY_W))


def kernel(x, c, ctx, c_ctx, w_mod, b_mod, norm1_g, norm2_g, w_in, w_out, q_norm_g, k_norm_g,
           conv_w, conv_b, filt_w1, filt_b1, filt_w2, filt_b2, filt_w3, filt_freq, hyena_bias,
           w_router, w_gate, w_up, w_down):
    B = x.shape[0]
    assert x.shape == (B, SEQ, D) and B == 2 and ctx.shape == (B, CTX, D) and w_mod.shape[0] == 1
    l = 0

    cc = jnp.concatenate([c, c_ctx[None, :], jnp.zeros((SUBLANES - B - 1, D), F32)], axis=0)
    mod = _modulation(cc, w_mod[l], b_mod[l][None, :])
    sh1, sc1, g1, sh2, sc2, g2 = [mod[:, i * D:(i + 1) * D] for i in range(6)]
    lat = lambda m: m[:B, None, :]
    ctxrow = lambda m: jnp.broadcast_to(m[B:B + 1, None, :], (B, 1, D))

    w_in_bf = w_in[l].astype(BF16)
    gq2 = jnp.tile(q_norm_g[l][None, :], (1, LANES // HD))
    gk2 = jnp.tile(k_norm_g[l][None, :], (1, LANES // HD))
    bd = jnp.asarray(np.kron(np.eye(2 * LANES // HD), np.full((HD, HD), 1.0 / HD)), BF16)
    cos_t, sin_t = _rope_tables(SEQ)
    n1g = norm1_g[l][None, :]

    q, k, vt, p = _in_projection(x, n1g, lat(sh1), lat(sc1), w_in_bf, gq2, gk2, bd, cos_t, sin_t, 512)
    _, kc, vct, _ = _in_projection(ctx, n1g, ctxrow(sh1), ctxrow(sc1), w_in_bf, gq2, gk2, bd,
                                   jnp.ones((CTX, LANES), F32), jnp.zeros((CTX, LANES), F32), CTX)

    kch = jnp.concatenate([k, kc], axis=2).reshape(B, NKV, ATT_NCH, ATT_TK, HD)
    vt_all = jnp.concatenate([vt, vct], axis=3).reshape(B, NKV, HD, ATT_NCH, ATT_TK)
    ones_pad = jnp.concatenate([jnp.ones((B, NKV, ATT_NCH, 1, ATT_TK), BF16),
                                jnp.zeros((B, NKV, ATT_NCH, BF16_ROWS - 1, ATT_TK), BF16)], axis=3)
    vtch = jnp.concatenate([vt_all.transpose(0, 1, 3, 2, 4), ones_pad], axis=3)
    bound = (1.02 * HD * Q_SCALE) * jnp.max(jnp.abs(q_norm_g[l])) * jnp.max(jnp.abs(k_norm_g[l]))
    att = _attention(bound.reshape(1).astype(F32), q, kch, vtch)

    cw9 = conv_w[l].reshape(3, 3, HY_W).reshape(9, HY_W)
    cb3 = conv_b[l].reshape(3, HY_W)
    u_rj, x2_rj = _short_conv(p, cw9, cb3)
    ztab, ttab, mtab, ftab, negdelta = _filter_tables()
    w1p = jnp.concatenate([filt_w1[l], jnp.zeros((FORD - FEMB, FORD), F32)], axis=0)
    kern, abs_sum = _implicit_filter(ztab, ttab, mtab, ftab, w1p, filt_b1[l][None, :], filt_w2[l],
                                     filt_b2[l][None, :], filt_w3[l], filt_freq[l][None, :], negdelta)
    hy = _hyena_long_conv(u_rj, x2_rj, kern, abs_sum, hyena_bias[l])

    wr2 = jnp.concatenate(_split(w_router[l].T), axis=0)
    x1, h2, logits = _out_projection(att, hy, x, w_out[l].astype(BF16), lat(g1), norm2_g[l][None, :],
                                     lat(sh2), lat(sc2), wr2)

    tri = jnp.asarray(np.triu(np.ones((TCH, TCH))), BF16)
    pos, gate, offs = _routing(logits, tri)
    offs_flat = offs.reshape(-1)
    xg = _gather(offs_flat, h2, pos.reshape(B, NE, NTCH, TCH))
    y = _expert_ffn(xg, jnp.swapaxes(w_gate[l], 1, 2), jnp.swapaxes(w_up[l], 1, 2), w_down[l])
    return _combine(offs_flat, y, pos, gate, x1, lat(g2))
```

```python
import functools
import math

import numpy as np
import jax
import jax.numpy as jnp
from jax import lax
from jax.experimental import pallas as pl
from jax.experimental.pallas import tpu as pltpu

F32 = jnp.float32
BF16 = jnp.bfloat16
I32 = jnp.int32

D = 1024
SEQ = 8192
CTX = 256
GRID_W = 64
ATT_W = 512
HY_W = 512
HD = 64
NQ = 8
NKV = 2
QPK = NQ // NKV
KV_W = NKV * HD
IN_W = ATT_W + 2 * KV_W + 3 * HY_W
FEMB = 33
FORD = 64
NE = 16
CAP = 2 * SEQ // NE
DEXP = 2752
ROPE_THETA = 10000.0
EPS = 1e-6
DECAY_TARGET = 1e-2
FAST_DECAY_PCT = 0.3
SLOW_DECAY_PCT = 1.5

LANES = 128
SUBLANES = 8
BF16_ROWS = 16
VMEM_BYTES_V7X = 64 * 1024 * 1024
VMEM_LIMIT = VMEM_BYTES_V7X - 8 * 1024 * 1024

FN = 2 * SEQ
FN1 = 128
FN2 = 128

TCH = LANES
NTCH = SEQ // TCH
GW = TCH + SUBLANES
CW = TCH + BF16_ROWS
YROWS = CAP + BF16_ROWS


def _cparams(sem, vmem=None):
    return pltpu.CompilerParams(dimension_semantics=sem, vmem_limit_bytes=vmem or VMEM_LIMIT)


def _split(a):
    hi = a.astype(BF16)
    lo = (a - hi.astype(F32)).astype(BF16)
    return hi, lo


_NN = (((1,), (0,)), ((), ()))
_NT = (((1,), (1,)), ((), ()))
_TN = (((0,), (0,)), ((), ()))


def _dot(a, b, dn=_NN):
    return lax.dot_general(a, b, dn, preferred_element_type=F32)


def _dot3(a, b, dn=_NN):
    ah, al = _split(a)
    bh, bl = _split(b)
    return _dot(ah, bh, dn) + _dot(ah, bl, dn) + _dot(al, bh, dn)


def _mod_body(c_ref, w_ref, b_ref, o_ref):
    c = c_ref[...]
    s = c * (1.0 / (1.0 + jnp.exp(-c)))
    o_ref[...] = _dot3(s, w_ref[...]) + b_ref[...]


def _modulation(cc, w_mod, b_mod):
    n = w_mod.shape[1]
    return pl.pallas_call(
        _mod_body,
        grid=(n // D,),
        in_specs=[pl.BlockSpec((SUBLANES, D), lambda j: (0, 0)),
                  pl.BlockSpec((D, D), lambda j: (0, j)),
                  pl.BlockSpec((1, D), lambda j: (0, j))],
        out_specs=pl.BlockSpec((SUBLANES, D), lambda j: (0, j)),
        out_shape=jax.ShapeDtypeStruct((SUBLANES, n), F32),
        compiler_params=_cparams(("arbitrary",)),
        name="modulation",
    )(cc, w_mod, b_mod)


Q_SCALE = HD ** -0.5 * math.log2(math.e)


def _rms_mod(x, g, sh, sc):
    ms = jnp.mean(x * x, axis=-1, keepdims=True)
    return (x * lax.rsqrt(ms + EPS) * g) * (1.0 + sc) + sh


def _head_mean_square(t, bd):
    hi, lo = _split(t * t)
    return _dot(hi, bd) + _dot(lo, bd)


def _head_norm_rope(t, ms, g, cos, sin):
    tn = t * lax.rsqrt(ms + EPS) * g
    lane = lax.broadcasted_iota(I32, tn.shape, 1)
    sw = jnp.where((lane & 31) < 16, pltpu.roll(tn, LANES - 16, 1), pltpu.roll(tn, 16, 1))
    return tn * cos + sw * sin


def _proj_body(x_ref, g_ref, sh_ref, sc_ref, w_ref, gq_ref, gk_ref, bd_ref, cos_ref, sin_ref,
               q_ref, k_ref, v_ref, p_ref):
    h = _rms_mod(x_ref[0], g_ref[...], sh_ref[0], sc_ref[0])
    proj = _dot(h.astype(BF16), w_ref[...])
    bd = bd_ref[...]
    cos = cos_ref[...]
    sin = sin_ref[...]
    wide = 2 * LANES
    for j in range(ATT_W // wide):
        ms = _head_mean_square(proj[:, j * wide:(j + 1) * wide], bd)
        for i in range(2):
            sl = slice(j * wide + i * LANES, j * wide + (i + 1) * LANES)
            qj = _head_norm_rope(proj[:, sl], ms[:, i * LANES:(i + 1) * LANES], gq_ref[...], cos, sin)
            q_ref[0, :, sl] = (qj * Q_SCALE).astype(BF16)
    ms = _head_mean_square(proj[:, ATT_W:ATT_W + 2 * KV_W], bd)
    kk = _head_norm_rope(proj[:, ATT_W:ATT_W + KV_W], ms[:, 0:KV_W], gk_ref[...], cos, sin)
    vt = proj[:, ATT_W + KV_W:ATT_W + 2 * KV_W].T
    for g in range(NKV):
        k_ref[0, g] = kk[:, g * HD:(g + 1) * HD].astype(BF16)
        v_ref[0, g] = vt[g * HD:(g + 1) * HD, :].astype(BF16)
    p_ref[0] = proj[:, ATT_W + 2 * KV_W:]


def _in_projection(x, g1, sh, sc, w_in_bf, gq2, gk2, bd, cos_t, sin_t, tm):
    b, s, _ = x.shape
    row = lambda bi, i: (bi, 0, 0)
    tok = lambda bi, i: (bi, i, 0)
    const = lambda bi, i: (0, 0)
    return pl.pallas_call(
        _proj_body,
        grid=(b, s // tm),
        in_specs=[pl.BlockSpec((1, tm, D), tok),
                  pl.BlockSpec((1, D), const),
                  pl.BlockSpec((1, 1, D), row),
                  pl.BlockSpec((1, 1, D), row),
                  pl.BlockSpec((D, IN_W), const),
                  pl.BlockSpec((1, LANES), const),
                  pl.BlockSpec((1, LANES), const),
                  pl.BlockSpec((2 * LANES, 2 * LANES), const),
                  pl.BlockSpec((tm, LANES), lambda bi, i: (i, 0)),
                  pl.BlockSpec((tm, LANES), lambda bi, i: (i, 0))],
        out_specs=[pl.BlockSpec((1, tm, ATT_W), tok),
                   pl.BlockSpec((1, NKV, tm, HD), lambda bi, i: (bi, 0, i, 0)),
                   pl.BlockSpec((1, NKV, HD, tm), lambda bi, i: (bi, 0, 0, i)),
                   pl.BlockSpec((1, tm, 3 * HY_W), tok)],
        out_shape=[jax.ShapeDtypeStruct((b, s, ATT_W), BF16),
                   jax.ShapeDtypeStruct((b, NKV, s, HD), BF16),
                   jax.ShapeDtypeStruct((b, NKV, HD, s), BF16),
                   jax.ShapeDtypeStruct((b, s, 3 * HY_W), F32)],
        compiler_params=_cparams(("parallel", "parallel")),
        name="in_projection",
    )(x, g1, sh, sc, w_in_bf, gq2, gk2, bd, cos_t, sin_t)


ATT_TQ = 512
ATT_TK = 768
SK = SEQ + CTX
ATT_NCH = SK // ATT_TK


ATT_NQ = QPK * ATT_TQ
ATT_VR = HD + BF16_ROWS
assert ATT_NCH % 2 == 1
ATT_SHIFT_MAX = 120.0


def _attn_body(bound_ref, q_ref, k_ref, vt_ref, o_ref, s_ref, mx_ref, m_ref, acc_ref):
    qall = jnp.concatenate([q_ref[0, :, r * HD:(r + 1) * HD] for r in range(QPK)], axis=0)
    acc_ref[...] = jnp.zeros_like(acc_ref)
    bound = bound_ref[0]
    fixed_shift = 2.0 * bound <= ATT_SHIFT_MAX

    def finish():
        out = acc_ref[0:HD, :] * (1.0 / acc_ref[HD:HD + 1, :])
        for r in range(QPK):
            o_ref[0, :, r * HD:(r + 1) * HD] = out[:, r * ATT_TQ:(r + 1) * ATT_TQ].T.astype(BF16)

    @pl.when(fixed_shift)
    def _():
        def chunk(c, _):
            s = _dot(k_ref[0, 0, c], qall, _NT)
            acc_ref[...] += _dot(vt_ref[0, 0, c], jnp.exp2(s - bound).astype(BF16))
            return 0

        lax.fori_loop(0, ATT_NCH, chunk, 0, unroll=2)
        finish()

    @pl.when(jnp.logical_not(fixed_shift))
    def _():
        m_ref[...] = jnp.full(m_ref.shape, -1e30, F32)

        def scores(c, slot):
            s = _dot(k_ref[0, 0, c], qall, _NT)
            s_ref[slot] = s
            mx_ref[slot] = jnp.max(s, axis=0, keepdims=True)

        def update(c, slot):
            m_old = m_ref[...]
            m_new = jnp.maximum(m_old, mx_ref[slot])
            p = jnp.exp2(s_ref[slot] - m_new).astype(BF16)
            acc_ref[...] = jnp.exp2(m_old - m_new) * acc_ref[...] + _dot(vt_ref[0, 0, c], p)
            m_ref[...] = m_new

        scores(0, 0)

        def pair(i, _):
            c = 2 * i
            scores(c + 1, 1)
            update(c, 0)
            scores(c + 2, 0)
            update(c + 1, 1)
            return 0

        lax.fori_loop(0, ATT_NCH // 2, pair, 0)
        update(ATT_NCH - 1, 0)
        finish()


def _attention(bound, q, kch, vtch):
    b = q.shape[0]
    grid_spec = pltpu.PrefetchScalarGridSpec(
        num_scalar_prefetch=1,
        grid=(b, NKV, SEQ // ATT_TQ),
        in_specs=[pl.BlockSpec((1, ATT_TQ, QPK * HD), lambda bi, g, i, bd: (bi, i, g)),
                  pl.BlockSpec((1, 1, ATT_NCH, ATT_TK, HD), lambda bi, g, i, bd: (bi, g, 0, 0, 0)),
                  pl.BlockSpec((1, 1, ATT_NCH, ATT_VR, ATT_TK), lambda bi, g, i, bd: (bi, g, 0, 0, 0))],
        out_specs=pl.BlockSpec((1, ATT_TQ, QPK * HD), lambda bi, g, i, bd: (bi, i, g)),
        scratch_shapes=[pltpu.VMEM((2, ATT_TK, ATT_NQ), F32), pltpu.VMEM((2, 1, ATT_NQ), F32),
                        pltpu.VMEM((1, ATT_NQ), F32), pltpu.VMEM((ATT_VR, ATT_NQ), F32)],
    )
    return pl.pallas_call(
        _attn_body,
        grid_spec=grid_spec,
        out_shape=jax.ShapeDtypeStruct((b, SEQ, ATT_W), BF16),
        compiler_params=_cparams(("parallel", "parallel", "parallel")),
        name="attention",
    )(bound, q, kch, vtch)


SC_TM = 2048
SC_J = SC_TM // FN2


def _sconv_body(m1, a1, n1, m2, a2, n2, m3, a3, n3, w_ref, b_ref, u_ref, x2_ref):
    i = pl.program_id(1)
    last = pl.num_programs(1) - 1
    rows = lax.broadcasted_iota(I32, (SC_TM, HY_W), 0)

    def conv(main, prev, nxt, g):
        x = main[0]
        pr = jnp.where(i > 0, prev[0, SUBLANES - 1:SUBLANES, :], 0.0)
        nx = jnp.where(i < last, nxt[0, 0:1, :], 0.0)
        xm = jnp.where(rows == 0, pr, pltpu.roll(x, 1, 0))
        xp = jnp.where(rows == SC_TM - 1, nx, pltpu.roll(x, SC_TM - 1, 0))
        return (w_ref[g:g + 1, :] * xm + w_ref[3 + g:4 + g, :] * x + w_ref[6 + g:7 + g, :] * xp
                + b_ref[g:g + 1, :])

    def to_rj(t):
        return jnp.swapaxes(t.reshape(SC_J, FN2, HY_W), 0, 1).astype(BF16)

    x1 = conv(m1, a1, n1, 0)
    x2 = conv(m2, a2, n2, 1)
    v = conv(m3, a3, n3, 2)
    u_ref[0] = to_rj(v * x1)
    x2_ref[0] = to_rj(x2)


def _short_conv(p, cw9, cb3):
    b = p.shape[0]
    nblk8 = SEQ // SUBLANES
    step8 = SC_TM // SUBLANES
    specs = []
    for g in range(3):
        specs += [pl.BlockSpec((1, SC_TM, HY_W), lambda bi, i, g=g: (bi, i, g)),
                  pl.BlockSpec((1, SUBLANES, HY_W), lambda bi, i, g=g: (bi, jnp.maximum(i * step8 - 1, 0), g)),
                  pl.BlockSpec((1, SUBLANES, HY_W), lambda bi, i, g=g: (bi, jnp.minimum((i + 1) * step8, nblk8 - 1), g))]
    specs += [pl.BlockSpec((9, HY_W), lambda bi, i: (0, 0)), pl.BlockSpec((3, HY_W), lambda bi, i: (0, 0))]
    out = pl.BlockSpec((1, FN2, SC_J, HY_W), lambda bi, i: (bi, 0, i, 0))
    return pl.pallas_call(
        _sconv_body,
        grid=(b, SEQ // SC_TM),
        in_specs=specs,
        out_specs=[out, out],
        out_shape=[jax.ShapeDtypeStruct((b, FN2, SEQ // FN2, HY_W), BF16)] * 2,
        compiler_params=_cparams(("parallel", "parallel")),
        name="short_conv",
    )(p, p, p, p, p, p, p, p, p, cw9, cb3)


FILT_TR = 1024


def _filter_body(z_ref, t_ref, msk_ref, fwd_ref, w1_ref, b1_ref, w2_ref, b2_ref, w3_ref, fr_ref, dl_ref,
                 k_ref, s_ref):
    fr = fr_ref[...]
    h = jnp.sin(fr * (_dot3(z_ref[...], w1_ref[...]) + b1_ref[...]))
    h = jnp.sin(fr * (_dot3(h, w2_ref[...]) + b2_ref[...]))
    h = _dot3(h, w3_ref[...])
    h = jnp.where(fwd_ref[...] > 0.5, h[:, :HY_W], h[:, HY_W:])
    kern = h * jnp.exp(t_ref[...] * dl_ref[...]) * msk_ref[...]
    k_ref[...] = kern

    @pl.when(pl.program_id(0) == 0)
    def _():
        s_ref[...] = jnp.zeros_like(s_ref)

    s_ref[...] += jnp.sum(jnp.abs(kern), axis=0, keepdims=True)


def _implicit_filter(ztab, ttab, mtab, ftab, w1p, b1, w2, b2, w3, freq, negdelta):
    rowblk = lambda i: (i, 0)
    const = lambda i: (0, 0)
    col = pl.BlockSpec((FILT_TR, 1), rowblk)
    return pl.pallas_call(
        _filter_body,
        grid=(FN // FILT_TR,),
        in_specs=[pl.BlockSpec((FILT_TR, FORD), rowblk), col, col, col,
                  pl.BlockSpec((FORD, FORD), const),
                  pl.BlockSpec((1, FORD), const),
                  pl.BlockSpec((FORD, FORD), const),
                  pl.BlockSpec((1, FORD), const),
                  pl.BlockSpec((FORD, 2 * HY_W), const),
                  pl.BlockSpec((1, FORD), const),
                  pl.BlockSpec((1, HY_W), const)],
        out_specs=[pl.BlockSpec((FILT_TR, HY_W), rowblk),
                   pl.BlockSpec((1, HY_W), const)],
        out_shape=[jax.ShapeDtypeStruct((FN, HY_W), F32), jax.ShapeDtypeStruct((1, HY_W), F32)],
        compiler_params=_cparams(("arbitrary",)),
        name="implicit_filter",
    )(ztab, ttab, mtab, ftab, w1p, b1, w2, b2, w3, freq, negdelta)


DFT_G = 4
DFT_KB = 16
DFT_NP = FN1 // DFT_KB
DFT_UNROLL = 4


def _dot2c(fh, fl, zb):
    return _dot(fh, zb) + _dot(fl, zb)


def _lead_stage(src, fh, fl, dst_ref):
    def group(rg, _):
        r0 = rg * DFT_G
        rhs = jnp.concatenate([src(r0 + g) for g in range(DFT_G)], axis=1)
        blk = _dot2c(fh, fl, rhs)
        for g in range(DFT_G):
            dst_ref[r0 + g] = blk[:, g * LANES:(g + 1) * LANES]
        return 0

    lax.fori_loop(0, FN2 // DFT_G, group, 0, unroll=DFT_UNROLL)


def _lead_phase(src, lh_ref, ll_ref, p_ref, q_ref):
    for h in range(2):
        rows = slice(h * FN1, (h + 1) * FN1)
        _lead_stage(src, lh_ref[rows, :], ll_ref[rows, :], p_ref)
        q_ref[rows] = jnp.swapaxes(p_ref[...], 0, 1)


def _lane_cat(xs):
    return jnp.concatenate(xs, axis=1)


def _mid_forward(q_ref, k0, tr_ref, ti_ref, fh, fl, half):
    tr_t, ti_t = tr_ref[...].T, ti_ref[...].T
    brs, bis, trs, tis = [], [], [], []
    for g in range(DFT_G):
        jj = half * DFT_G + g
        ar, ai = q_ref[k0 + jj], q_ref[FN1 + k0 + jj]
        tr = jnp.broadcast_to(tr_t[:, jj:jj + 1], (FN2, LANES))
        ti = jnp.broadcast_to(ti_t[:, jj:jj + 1], (FN2, LANES))
        brs.append(ar * tr - ai * ti)
        bis.append(ar * ti + ai * tr)
        trs.append(tr)
        tis.append(ti)
    b = jnp.concatenate([_lane_cat(brs), _lane_cat(bis)], axis=0).astype(BF16)
    return _dot2c(fh, fl, b), _lane_cat(trs), _lane_cat(tis)


def _spectrum_body(k_ref, tr_ref, ti_ref, lh_ref, ll_ref, fh_ref, fl_ref, sc_ref, o_ref, p_ref, q_ref):
    ph = pl.program_id(1)

    @pl.when(ph == 0)
    def _():
        _lead_phase(lambda r: k_ref[r].astype(BF16), lh_ref, ll_ref, p_ref, q_ref)

    @pl.when(ph > 0)
    def _():
        k0 = (ph - 1) * DFT_KB
        sc = _lane_cat([sc_ref[...]] * DFT_G)
        for half in range(DFT_KB // DFT_G):
            x, _, _ = _mid_forward(q_ref, k0, tr_ref, ti_ref, fh_ref[...], fl_ref[...], half)
            x = x * sc
            for g in range(DFT_G):
                lanes = slice(g * LANES, (g + 1) * LANES)
                o_ref[0, half * DFT_G + g] = x[:FN2, lanes]
                o_ref[1, half * DFT_G + g] = x[FN2:, lanes]


def _mid_index(ph):
    return jnp.clip(ph - 1, 0, DFT_NP - 1)


def _filter_spectrum(kern_rj, twr, twi, lh, ll, fh, fl, scale):
    const = lambda c, ph: (0, 0)
    tw = pl.BlockSpec((DFT_KB, FN2), lambda c, ph: (_mid_index(ph), 0))
    return pl.pallas_call(
        _spectrum_body,
        grid=(HY_W // LANES, DFT_NP + 1),
        in_specs=[pl.BlockSpec((FN2, FN1, LANES), lambda c, ph: (0, 0, c), pipeline_mode=pl.Buffered(1)),
                  tw, tw,
                  pl.BlockSpec(lh.shape, const), pl.BlockSpec(ll.shape, const),
                  pl.BlockSpec(fh.shape, const), pl.BlockSpec(fl.shape, const),
                  pl.BlockSpec((1, LANES), lambda c, ph: (0, c))],
        out_specs=pl.BlockSpec((2, DFT_KB, FN2, LANES), lambda c, ph: (0, _mid_index(ph), 0, c)),
        out_shape=jax.ShapeDtypeStruct((2, FN1, FN2, HY_W), F32),
        scratch_shapes=[pltpu.VMEM((FN2, FN1, LANES), F32), pltpu.VMEM((2 * FN1, FN2, LANES), F32)],
        compiler_params=_cparams(("parallel", "arbitrary")),
        name="filter_spectrum",
    )(kern_rj, twr, twi, lh, ll, fh, fl, scale)


def _hconv_body(u_ref, x2_ref, kh_ref, tr_ref, ti_ref, lh_ref, ll_ref, fh_ref, fl_ref, gh_ref, gl_ref,
                oh_ref, ol_ref, bias_ref, o_ref, p_ref, q_ref):
    ph = pl.program_id(1)

    def both(ref, r):
        return jnp.concatenate([ref[0, r], ref[1, r]], axis=0)

    @pl.when(ph == 0)
    def _():
        _lead_phase(lambda r: both(u_ref, r), lh_ref, ll_ref, p_ref, q_ref)

    @pl.when((ph > 0) & (ph <= DFT_NP))
    def _():
        k0 = (ph - 1) * DFT_KB
        for half in range(DFT_KB // DFT_G):
            x, tr, ti = _mid_forward(q_ref, k0, tr_ref, ti_ref, fh_ref[...], fl_ref[...], half)
            xr, xi = x[:FN2], x[FN2:]
            kr = _lane_cat([kh_ref[0, half * DFT_G + g] for g in range(DFT_G)])
            ki = _lane_cat([kh_ref[1, half * DFT_G + g] for g in range(DFT_G)])
            y = jnp.concatenate([xr * kr - xi * ki, xr * ki + xi * kr], axis=0).astype(BF16)
            c = _dot2c(gh_ref[...], gl_ref[...], y)
            cr, ci = c[:FN2], c[FN2:]
            dr = cr * tr + ci * ti
            di = ci * tr - cr * ti
            for g in range(DFT_G):
                lanes = slice(g * LANES, (g + 1) * LANES)
                q_ref[k0 + half * DFT_G + g] = dr[:, lanes]
                q_ref[FN1 + k0 + half * DFT_G + g] = di[:, lanes]

    @pl.when(ph == DFT_NP + 1)
    def _():
        bias = bias_ref[...]
        p_ref[...] = jnp.swapaxes(q_ref[0:FN1], 0, 1)
        _lead_stage(lambda r: p_ref[r].astype(BF16), oh_ref[:, 0:FN1], ol_ref[:, 0:FN1], q_ref)
        p_ref[...] = jnp.swapaxes(q_ref[FN1:2 * FN1], 0, 1)
        oh2, ol2 = oh_ref[:, FN1:2 * FN1], ol_ref[:, FN1:2 * FN1]

        def group(rg, _):
            r0 = rg * DFT_G
            rhs = _lane_cat([p_ref[r0 + g].astype(BF16) for g in range(DFT_G)])
            blk = _dot2c(oh2, ol2, rhs)
            for g in range(DFT_G):
                r = r0 + g
                y = q_ref[r] + blk[:, g * LANES:(g + 1) * LANES]
                q_ref[r] = (y + both(u_ref, r).astype(F32) * bias) * both(x2_ref, r).astype(F32)
            return 0

        lax.fori_loop(0, FN2 // DFT_G, group, 0, unroll=DFT_UNROLL)
        p_ref[...] = jnp.swapaxes(q_ref[0:FN2], 0, 1)
        nj = SEQ // FN2
        for b in range(2):
            o_ref[b] = p_ref[b * nj:(b + 1) * nj].reshape(SEQ, LANES).astype(BF16)


def _hyena_conv(u_rj, x2_rj, khat, twr, twi, lead, fwd, inv, out, bias):
    const = lambda c, ph: (0, 0)
    nj = SEQ // FN2
    tw = pl.BlockSpec((DFT_KB, FN2), lambda c, ph: (_mid_index(ph), 0))
    sig = pl.BlockSpec((2, FN2, nj, LANES), lambda c, ph: (0, 0, 0, c), pipeline_mode=pl.Buffered(1))
    mats = [m for pair in (lead, fwd, inv, out) for m in pair]
    return pl.pallas_call(
        _hconv_body,
        grid=(HY_W // LANES, DFT_NP + 2),
        in_specs=[sig, sig,
                  pl.BlockSpec((2, DFT_KB, FN2, LANES), lambda c, ph: (0, _mid_index(ph), 0, c)),
                  tw, tw] + [pl.BlockSpec(m.shape, const) for m in mats]
                 + [pl.BlockSpec((1, LANES), lambda c, ph: (0, c))],
        out_specs=pl.BlockSpec((2, SEQ, LANES), lambda c, ph: (0, 0, c)),
        out_shape=jax.ShapeDtypeStruct((2, SEQ, HY_W), BF16),
        scratch_shapes=[pltpu.VMEM((FN2, FN1, LANES), F32), pltpu.VMEM((2 * FN1, FN2, LANES), F32)],
        compiler_params=_cparams(("parallel", "arbitrary")),
        name="hyena_conv",
    )(u_rj, x2_rj, khat, twr, twi, *mats, bias)


OP_TM = 512


def _outproj_body(att_ref, hy_ref, x_ref, w_ref, g1_ref, n2_ref, sh_ref, sc_ref, wr_ref,
                  x1_ref, h2_ref, lg_ref):
    a = jnp.concatenate([att_ref[0], hy_ref[0]], axis=1)
    x1 = x_ref[0] + g1_ref[0] * _dot(a, w_ref[...])
    x1_ref[0] = x1
    h2 = _rms_mod(x1, n2_ref[...], sh_ref[0], sc_ref[0])
    hh, hl = _split(h2)
    h2_ref[0] = hh
    wr = wr_ref[...]
    both = _dot(wr, hh, _NT)
    lg_ref[0] = both[0:NE] + both[NE:2 * NE] + _dot(wr[0:NE], hl, _NT)


def _out_projection(att, hy, x, w_out_bf, g1r, n2g, sh2, sc2, wr2):
    b = x.shape[0]
    tok = lambda bi, i: (bi, i, 0)
    row = lambda bi, i: (bi, 0, 0)
    const = lambda bi, i: (0, 0)
    return pl.pallas_call(
        _outproj_body,
        grid=(b, SEQ // OP_TM),
        in_specs=[pl.BlockSpec((1, OP_TM, ATT_W), tok),
                  pl.BlockSpec((1, OP_TM, HY_W), tok),
                  pl.BlockSpec((1, OP_TM, D), tok),
                  pl.BlockSpec((ATT_W + HY_W, D), const),
                  pl.BlockSpec((1, 1, D), row),
                  pl.BlockSpec((1, D), const),
                  pl.BlockSpec((1, 1, D), row),
                  pl.BlockSpec((1, 1, D), row),
                  pl.BlockSpec((2 * NE, D), const)],
        out_specs=[pl.BlockSpec((1, OP_TM, D), tok),
                   pl.BlockSpec((1, OP_TM, D), tok),
                   pl.BlockSpec((1, NE, OP_TM), lambda bi, i: (bi, 0, i))],
        out_shape=[jax.ShapeDtypeStruct((b, SEQ, D), F32),
                   jax.ShapeDtypeStruct((b, SEQ, D), BF16),
                   jax.ShapeDtypeStruct((b, NE, SEQ), F32)],
        compiler_params=_cparams(("parallel", "parallel")),
        name="out_projection",
    )(att, hy, x, w_out_bf, g1r, n2g, sh2, sc2, wr2)


def _routing_body(lg_ref, tri_ref, pos_ref, gate_ref, off_ref, cs_ref):
    lg = lg_ref[0]
    e = jnp.exp(lg - jnp.max(lg, axis=0, keepdims=True))
    aff = e / jnp.sum(e, axis=0, keepdims=True)
    gate_ref[0] = aff
    def count_ge(t):
        return jnp.sum(jnp.where(aff >= t, 1.0, 0.0), axis=1, keepdims=True)

    def bisect(i, thr):
        cand = thr | (jnp.int32(1) << (30 - i))
        return jnp.where(count_ge(pltpu.bitcast(cand, F32)) >= float(CAP), cand, thr)

    thr = lax.fori_loop(0, 31, bisect, jnp.zeros((NE, 1), I32))
    lo = pltpu.bitcast(thr, F32)
    hi = jnp.maximum(pltpu.bitcast(thr + 1, F32), jnp.finfo(F32).tiny)

    def refine(i, c):
        lo, hi = c
        mid = lo + (hi - lo) * 0.5
        ok = count_ge(mid) >= float(CAP)
        return jnp.where(ok, mid, lo), jnp.where(ok, hi, mid)

    lo, hi = lax.fori_loop(0, 32, refine, (lo, hi))
    gt = aff >= hi
    eq = (aff >= lo) & jnp.logical_not(gt)
    need = float(CAP) - jnp.sum(jnp.where(gt, 1.0, 0.0), axis=1, keepdims=True)
    tri = tri_ref[...]

    def excl_cumsum(mask_f, record_offsets):
        carry = jnp.zeros((NE, 1), F32)
        for c in range(NTCH):
            sl = slice(c * TCH, (c + 1) * TCH)
            m = mask_f[:, sl]
            inc = _dot(m.astype(BF16), tri)
            cs_ref[:, sl] = inc - m + carry
            if record_offsets:
                off_ref[0, :, c:c + 1] = carry.astype(I32)
            carry = carry + inc[:, TCH - 1:TCH]
        return cs_ref[...]

    eq_rank = excl_cumsum(jnp.where(eq, 1.0, 0.0), False)
    sel = gt | (eq & (eq_rank < need))
    pos = excl_cumsum(jnp.where(sel, 1.0, 0.0), True)
    pos_ref[0] = jnp.where(sel, pos.astype(I32), -1)


def _routing(logits, tri):
    b = logits.shape[0]
    blk = pl.BlockSpec((1, NE, SEQ), lambda bi: (bi, 0, 0))
    return pl.pallas_call(
        _routing_body,
        grid=(b,),
        in_specs=[blk, pl.BlockSpec((TCH, TCH), lambda bi: (0, 0))],
        out_specs=[blk, blk, pl.BlockSpec((1, NE, NTCH), lambda bi: (bi, 0, 0))],
        out_shape=[jax.ShapeDtypeStruct((b, NE, SEQ), I32),
                   jax.ShapeDtypeStruct((b, NE, SEQ), F32),
                   jax.ShapeDtypeStruct((b, NE, NTCH), I32)],
        scratch_shapes=[pltpu.VMEM((NE, SEQ), F32)],
        compiler_params=_cparams(("parallel",)),
        name="routing",
    )(logits, tri)


GATHER_UNROLL = 8


GW_SMALL = 48


def _gather_body(off_ref, h_ref, pos_ref, xg_ref, acc_ref):
    b = pl.program_id(0)
    e = pl.program_id(1)
    row0 = (b * NE + e) * NTCH
    acc_ref[...] = jnp.zeros_like(acc_ref)

    def count(c, most):
        nxt = jnp.where(c + 1 < NTCH, off_ref[row0 + jnp.minimum(c + 1, NTCH - 1)], CAP)
        return jnp.maximum(most, nxt - off_ref[row0 + c])

    most = lax.fori_loop(0, NTCH, count, 0)

    def sweep(window):
        crow = lax.broadcasted_iota(I32, (window, TCH), 0)

        def chunks(i, _):
            for j in range(GATHER_UNROLL):
                c = i * GATHER_UNROLL + j
                off = off_ref[row0 + c]
                base = pl.multiple_of(jnp.minimum((off >> 3) << 3, CAP + SUBLANES - window), SUBLANES)
                t0 = pl.multiple_of(c * TCH, TCH)
                rel = pos_ref[0, 0, pl.ds(c, 1), :] - base
                onehot = jnp.where(crow == rel, 1.0, 0.0).astype(BF16)
                acc_ref[pl.ds(base, window), :] += _dot(onehot, h_ref[0, pl.ds(t0, TCH), :])
            return 0

        lax.fori_loop(0, NTCH // GATHER_UNROLL, chunks, 0)

    @pl.when(most <= GW_SMALL - SUBLANES)
    def _():
        sweep(GW_SMALL)

    @pl.when(most > GW_SMALL - SUBLANES)
    def _():
        sweep(GW)

    xg_ref[0, 0] = acc_ref[0:CAP, :].astype(BF16)


def _gather(offs_flat, h2, pos4):
    b = h2.shape[0]
    grid_spec = pltpu.PrefetchScalarGridSpec(
        num_scalar_prefetch=1,
        grid=(b, NE),
        in_specs=[pl.BlockSpec((1, SEQ, D), lambda bi, e, off: (bi, 0, 0)),
                  pl.BlockSpec((1, 1, NTCH, TCH), lambda bi, e, off: (bi, e, 0, 0))],
        out_specs=pl.BlockSpec((1, 1, CAP, D), lambda bi, e, off: (bi, e, 0, 0)),
        scratch_shapes=[pltpu.VMEM((CAP + SUBLANES, D), F32)],
    )
    return pl.pallas_call(
        _gather_body,
        grid_spec=grid_spec,
        out_shape=jax.ShapeDtypeStruct((b, NE, CAP, D), BF16),
        compiler_params=_cparams(("parallel", "arbitrary")),
        name="moe_gather",
    )(offs_flat, h2, pos4)


FFN_TM = 512
FFN_NF = 4
FFN_FC = DEXP // FFN_NF
assert FFN_FC * FFN_NF == DEXP and FFN_FC % BF16_ROWS == 0


def _ffn_body(xg_ref, wgt_ref, wut_ref, wd_ref, y_ref, acc_ref):
    j = pl.program_id(1)
    nb = xg_ref.shape[0]

    @pl.when(j == 0)
    def _():
        acc_ref[...] = jnp.zeros_like(acc_ref)

    wgt = wgt_ref[0].astype(BF16)
    wut = wut_ref[0].astype(BF16)
    wd = wd_ref[0].astype(BF16)
    for b in range(nb):
        for mb in range(CAP // FFN_TM):
            rows = slice(mb * FFN_TM, (mb + 1) * FFN_TM)
            xb = xg_ref[b, 0, rows, :]
            a = _dot(xb, wgt, _NT)
            u = _dot(xb, wut, _NT)
            h = (a * (1.0 / (1.0 + jnp.exp(-a))) * u).astype(BF16)
            acc_ref[b, rows, :] += _dot(h, wd)

    @pl.when(j == FFN_NF - 1)
    def _():
        for b in range(nb):
            y_ref[b, 0, 0:CAP, :] = acc_ref[b].astype(BF16)
            y_ref[b, 0, CAP:YROWS, :] = jnp.zeros((YROWS - CAP, D), BF16)


def _expert_ffn(xg, w_gate_t, w_up_t, w_down):
    b = xg.shape[0]
    wblk = pl.BlockSpec((1, FFN_FC, D), lambda e, j: (e, j, 0))
    return pl.pallas_call(
        _ffn_body,
        grid=(NE, FFN_NF),
        in_specs=[pl.BlockSpec((b, 1, CAP, D), lambda e, j: (0, e, 0, 0)), wblk, wblk, wblk],
        out_specs=pl.BlockSpec((b, 1, YROWS, D), lambda e, j: (0, e, 0, 0)),
        out_shape=jax.ShapeDtypeStruct((b, NE, YROWS, D), BF16),
        scratch_shapes=[pltpu.VMEM((b, CAP, D), F32)],
        compiler_params=_cparams(("parallel", "arbitrary")),
        name="expert_ffn",
    )(xg, w_gate_t, w_up_t, w_down)


CW_SMALL = 64
CW_STACK = 2 * LANES // CW_SMALL


def _combine_body(off_ref, y_ref, pos_ref, gate_ref, x1_ref, g2_ref, o_ref):
    b = pl.program_id(0)
    i = pl.program_id(1)

    def offset(e, c):
        return off_ref[(b * NE + e) * NTCH + c]

    def window(e, rows):
        base = pl.multiple_of(jnp.minimum((offset(e, i) >> 4) << 4, YROWS - rows), BF16_ROWS)
        rel = pos_ref[0, e:e + 1, :] - base
        crow = lax.broadcasted_iota(I32, (rows, TCH), 0)
        w = jnp.where(crow == rel, gate_ref[0, e:e + 1, :], 0.0).astype(BF16)
        return w, y_ref[0, e, pl.ds(base, rows), :]

    most = jnp.int32(0)
    for e in range(NE):
        nxt = jnp.where(i + 1 < NTCH, offset(e, jnp.minimum(i + 1, NTCH - 1)), CAP)
        most = jnp.maximum(most, nxt - offset(e, i))

    @pl.when(most <= CW_SMALL - BF16_ROWS)
    def _():
        acc = jnp.zeros((TCH, D), F32)
        for e0 in range(0, NE, CW_STACK):
            ws, ys = zip(*[window(e, CW_SMALL) for e in range(e0, e0 + CW_STACK)])
            acc = acc + _dot(jnp.concatenate(ws, axis=0), jnp.concatenate(ys, axis=0), _TN)
        o_ref[0] = x1_ref[0] + g2_ref[0] * acc

    @pl.when(most > CW_SMALL - BF16_ROWS)
    def _():
        acc = jnp.zeros((TCH, D), F32)
        for e in range(NE):
            w, yw = window(e, CW)
            acc = acc + _dot(w, yw, _TN)
        o_ref[0] = x1_ref[0] + g2_ref[0] * acc


def _combine(offs_flat, y, pos, gate, x1, g2r):
    b = x1.shape[0]
    grid_spec = pltpu.PrefetchScalarGridSpec(
        num_scalar_prefetch=1,
        grid=(b, NTCH),
        in_specs=[pl.BlockSpec((1, NE, YROWS, D), lambda bi, i, off: (bi, 0, 0, 0),
                               pipeline_mode=pl.Buffered(1)),
                  pl.BlockSpec((1, NE, TCH), lambda bi, i, off: (bi, 0, i)),
                  pl.BlockSpec((1, NE, TCH), lambda bi, i, off: (bi, 0, i)),
                  pl.BlockSpec((1, TCH, D), lambda bi, i, off: (bi, i, 0)),
                  pl.BlockSpec((1, 1, D), lambda bi, i, off: (bi, 0, 0))],
        out_specs=pl.BlockSpec((1, TCH, D), lambda bi, i, off: (bi, i, 0)),
    )
    return pl.pallas_call(
        _combine_body,
        grid_spec=grid_spec,
        out_shape=jax.ShapeDtypeStruct((b, SEQ, D), F32),
        compiler_params=_cparams(("parallel", "arbitrary")),
        name="moe_combine",
    )(offs_flat, y, pos, gate, x1, g2r)


def _np_split(m):
    hi = np.asarray(m, np.float64).astype(BF16)
    lo = (m - hi.astype(np.float64)).astype(BF16)
    return jnp.asarray(hi), jnp.asarray(lo)


@functools.lru_cache(maxsize=None)
def _dft_tables():
    a = np.arange(FN1, dtype=np.float64)
    ang = 2.0 * np.pi * np.outer(a, a) / FN1
    fr, fi = np.cos(ang), -np.sin(ang)
    half = SEQ // FN2
    lead_u = np.block([[fr[:, :half], -fi[:, :half]], [fi[:, :half], fr[:, :half]]])
    lead_k = np.concatenate([fr, fi], axis=0)
    fwd = np.block([[fr, -fi], [fi, fr]])
    inv = np.block([[fr, fi], [-fi, fr]])
    out = np.block([[fr[:half], fi[:half]], [-fi[:half], fr[:half]]])
    n2 = np.arange(FN2, dtype=np.float64)
    tw = 2.0 * np.pi * np.outer(a, n2) / FN
    twr = np.cos(tw).astype(np.float32)
    twi = (-np.sin(tw)).astype(np.float32)
    return dict(lead_u=lead_u, lead_k=lead_k, fwd=fwd, inv=inv, out=out, twr=twr, twi=twi)


@functools.lru_cache(maxsize=None)
def _filter_tables():
    L = SEQ
    n = np.arange(FN).reshape(FN1, FN2).T.reshape(-1)
    lag = np.where(n < L, n, FN - n)
    jc = np.minimum(lag, L - 1).astype(np.float64)
    t = (jc / (L - 1))[:, None]
    bands = (FEMB - 1) // 2
    w = 2.0 * np.pi * jc / L
    f = np.linspace(1e-4, bands - 1, bands)
    fw = w[:, None] * f[None, :]
    z = np.concatenate([t, np.cos(fw), -np.sin(fw), np.zeros((FN, FORD - FEMB))], axis=-1)
    mask = np.where(n == L, 0.0, 1.0)[:, None]
    fwd = np.where(n < L, 1.0, 0.0)[:, None]
    max_decay = math.log(DECAY_TARGET) / FAST_DECAY_PCT
    min_decay = math.log(DECAY_TARGET) / SLOW_DECAY_PCT
    negdelta = -np.abs(np.linspace(min_decay, max_decay, HY_W))[None, :]
    return tuple(np.asarray(a, np.float32) for a in (z, t, mask, fwd, negdelta))


@functools.lru_cache(maxsize=None)
def _rope_tables(n):
    rows = n // GRID_W
    row_id, col_id = np.meshgrid(np.arange(rows, dtype=np.float64), np.arange(GRID_W, dtype=np.float64), indexing="ij")
    quarter = HD // 4
    inv_freq = ROPE_THETA ** (-np.arange(quarter, dtype=np.float64) / quarter)
    ar = row_id.reshape(-1)[:, None] * inv_freq
    ac = col_id.reshape(-1)[:, None] * inv_freq
    cos = np.concatenate([np.cos(ar), np.cos(ar), np.cos(ac), np.cos(ac)], axis=-1)
    sin = np.concatenate([-np.sin(ar), np.sin(ar), -np.sin(ac), np.sin(ac)], axis=-1)
    reps = (1, LANES // HD)
    return np.tile(cos, reps).astype(np.float32), np.tile(sin, reps).astype(np.float32)


def _hyena_long_conv(u_rj, x2_rj, kern, abs_sum, bias):
    tb = _dft_tables()
    twr, twi = tb["twr"], tb["twi"]
    fwd = _np_split(tb["fwd"])
    scale = 1.0 / (abs_sum * float(FN))
    khat = _filter_spectrum(kern.reshape(FN2, FN1, HY_W), twr, twi, *_np_split(tb["lead_k"]), *fwd, scale)
    return _hyena_conv(u_rj, x2_rj, khat, twr, twi, _np_split(tb["lead_u"]), fwd, _np_split(tb["inv"]),
                       _np_split(tb["out"]), bias.reshape(1, HY_W))


def kernel(x, c, ctx, c_ctx, w_mod, b_mod, norm1_g, norm2_g, w_in, w_out, q_norm_g, k_norm_g,
           conv_w, conv_b, filt_w1, filt_b1, filt_w2, filt_b2, filt_w3, filt_freq, hyena_bias,
           w_router, w_gate, w_up, w_down):
    B = x.shape[0]
    assert x.shape == (B, SEQ, D) and B == 2 and ctx.shape == (B, CTX, D) and w_mod.shape[0] == 1
    l = 0

    cc = jnp.concatenate([c, c_ctx[None, :], jnp.zeros((SUBLANES - B - 1, D), F32)], axis=0)
    mod = _modulation(cc, w_mod[l], b_mod[l][None, :])
    sh1, sc1, g1, sh2, sc2, g2 = [mod[:, i * D:(i + 1) * D] for i in range(6)]
    lat = lambda m: m[:B, None, :]
    ctxrow = lambda m: jnp.broadcast_to(m[B:B + 1, None, :], (B, 1, D))

    w_in_bf = w_in[l].astype(BF16)
    gq2 = jnp.tile(q_norm_g[l][None, :], (1, LANES // HD))
    gk2 = jnp.tile(k_norm_g[l][None, :], (1, LANES // HD))
    bd = jnp.asarray(np.kron(np.eye(2 * LANES // HD), np.full((HD, HD), 1.0 / HD)), BF16)
    cos_t, sin_t = _rope_tables(SEQ)
    n1g = norm1_g[l][None, :]

    q, k, vt, p = _in_projection(x, n1g, lat(sh1), lat(sc1), w_in_bf, gq2, gk2, bd, cos_t, sin_t, 512)
    _, kc, vct, _ = _in_projection(ctx, n1g, ctxrow(sh1), ctxrow(sc1), w_in_bf, gq2, gk2, bd,
                                   jnp.ones((CTX, LANES), F32), jnp.zeros((CTX, LANES), F32), CTX)

    kch = jnp.concatenate([k, kc], axis=2).reshape(B, NKV, ATT_NCH, ATT_TK, HD)
    vt_all = jnp.concatenate([vt, vct], axis=3).reshape(B, NKV, HD, ATT_NCH, ATT_TK)
    ones_pad = jnp.concatenate([jnp.ones((B, NKV, ATT_NCH, 1, ATT_TK), BF16),
                                jnp.zeros((B, NKV, ATT_NCH, BF16_ROWS - 1, ATT_TK), BF16)], axis=3)
    vtch = jnp.concatenate([vt_all.transpose(0, 1, 3, 2, 4), ones_pad], axis=3)
    bound = (1.02 * HD * Q_SCALE) * jnp.max(jnp.abs(q_norm_g[l])) * jnp.max(jnp.abs(k_norm_g[l]))
    att = _attention(bound.reshape(1).astype(F32), q, kch, vtch)

    cw9 = conv_w[l].reshape(3, 3, HY_W).reshape(9, HY_W)
    cb3 = conv_b[l].reshape(3, HY_W)
    u_rj, x2_rj = _short_conv(p, cw9, cb3)
    ztab, ttab, mtab, ftab, negdelta = _filter_tables()
    w1p = jnp.concatenate([filt_w1[l], jnp.zeros((FORD - FEMB, FORD), F32)], axis=0)
    kern, abs_sum = _implicit_filter(ztab, ttab, mtab, ftab, w1p, filt_b1[l][None, :], filt_w2[l],
                                     filt_b2[l][None, :], filt_w3[l], filt_freq[l][None, :], negdelta)
    hy = _hyena_long_conv(u_rj, x2_rj, kern, abs_sum, hyena_bias[l])

    wr2 = jnp.concatenate(_split(w_router[l].T), axis=0)
    x1, h2, logits = _out_projection(att, hy, x, w_out[l].astype(BF16), lat(g1), norm2_g[l][None, :],
                                     lat(sh2), lat(sc2), wr2)

    tri = jnp.asarray(np.triu(np.ones((TCH, TCH))), BF16)
    pos, gate, offs = _routing(logits, tri)
    offs_flat = offs.reshape(-1)
    xg = _gather(offs_flat, h2, pos.reshape(B, NE, NTCH, TCH))
    y = _expert_ffn(xg, jnp.swapaxes(w_gate[l], 1, 2), jnp.swapaxes(w_up[l], 1, 2), w_down[l])
    return _combine(offs_flat, y, pos, gate, x1, lat(g2))
```

```python
import functools
import math

import numpy as np
import jax
import jax.numpy as jnp
from jax import lax
from jax.experimental import pallas as pl
from jax.experimental.pallas import tpu as pltpu

F32 = jnp.float32
BF16 = jnp.bfloat16
I32 = jnp.int32

D = 1024
SEQ = 8192
CTX = 256
GRID_W = 64
ATT_W = 512
HY_W = 512
HD = 64
NQ = 8
NKV = 2
QPK = NQ // NKV
KV_W = NKV * HD
IN_W = ATT_W + 2 * KV_W + 3 * HY_W
FEMB = 33
FORD = 64
NE = 16
CAP = 2 * SEQ // NE
DEXP = 2752
ROPE_THETA = 10000.0
EPS = 1e-6
DECAY_TARGET = 1e-2
FAST_DECAY_PCT = 0.3
SLOW_DECAY_PCT = 1.5

LANES = 128
SUBLANES = 8
BF16_ROWS = 16
VMEM_BYTES_V7X = 64 * 1024 * 1024
VMEM_LIMIT = VMEM_BYTES_V7X - 8 * 1024 * 1024

FN = 2 * SEQ
FN1 = 128
FN2 = 128

TCH = LANES
NTCH = SEQ // TCH
GW = TCH + SUBLANES
CW = TCH + BF16_ROWS
YROWS = CAP + BF16_ROWS


def _cparams(sem, vmem=None):
    return pltpu.CompilerParams(dimension_semantics=sem, vmem_limit_bytes=vmem or VMEM_LIMIT)


def _split(a):
    hi = a.astype(BF16)
    lo = (a - hi.astype(F32)).astype(BF16)
    return hi, lo


_NN = (((1,), (0,)), ((), ()))
_NT = (((1,), (1,)), ((), ()))
_TN = (((0,), (0,)), ((), ()))


def _dot(a, b, dn=_NN):
    return lax.dot_general(a, b, dn, preferred_element_type=F32)


def _dot3(a, b, dn=_NN):
    ah, al = _split(a)
    bh, bl = _split(b)
    return _dot(ah, bh, dn) + _dot(ah, bl, dn) + _dot(al, bh, dn)


def _mod_body(c_ref, w_ref, b_ref, o_ref):
    c = c_ref[...]
    s = c * (1.0 / (1.0 + jnp.exp(-c)))
    o_ref[...] = _dot3(s, w_ref[...]) + b_ref[...]


def _modulation(cc, w_mod, b_mod):
    n = w_mod.shape[1]
    return pl.pallas_call(
        _mod_body,
        grid=(n // D,),
        in_specs=[pl.BlockSpec((SUBLANES, D), lambda j: (0, 0)),
                  pl.BlockSpec((D, D), lambda j: (0, j)),
                  pl.BlockSpec((1, D), lambda j: (0, j))],
        out_specs=pl.BlockSpec((SUBLANES, D), lambda j: (0, j)),
        out_shape=jax.ShapeDtypeStruct((SUBLANES, n), F32),
        compiler_params=_cparams(("arbitrary",)),
        name="modulation",
    )(cc, w_mod, b_mod)


Q_SCALE = HD ** -0.5 * math.log2(math.e)


def _rms_mod(x, g, sh, sc):
    ms = jnp.mean(x * x, axis=-1, keepdims=True)
    return (x * lax.rsqrt(ms + EPS) * g) * (1.0 + sc) + sh


def _head_mean_square(t, bd):
    hi, lo = _split(t * t)
    return _dot(hi, bd) + _dot(lo, bd)


def _head_norm_rope(t, ms, g, cos, sin):
    tn = t * lax.rsqrt(ms + EPS) * g
    lane = lax.broadcasted_iota(I32, tn.shape, 1)
    sw = jnp.where((lane & 31) < 16, pltpu.roll(tn, LANES - 16, 1), pltpu.roll(tn, 16, 1))
    return tn * cos + sw * sin


def _proj_body(x_ref, g_ref, sh_ref, sc_ref, w_ref, gq_ref, gk_ref, bd_ref, cos_ref, sin_ref,
               q_ref, k_ref, v_ref, p_ref):
    h = _rms_mod(x_ref[0], g_ref[...], sh_ref[0], sc_ref[0])
    proj = _dot(h.astype(BF16), w_ref[...])
    bd = bd_ref[...]
    cos = cos_ref[...]
    sin = sin_ref[...]
    wide = 2 * LANES
    for j in range(ATT_W // wide):
        ms = _head_mean_square(proj[:, j * wide:(j + 1) * wide], bd)
        for i in range(2):
            sl = slice(j * wide + i * LANES, j * wide + (i + 1) * LANES)
            qj = _head_norm_rope(proj[:, sl], ms[:, i * LANES:(i + 1) * LANES], gq_ref[...], cos, sin)
            q_ref[0, :, sl] = (qj * Q_SCALE).astype(BF16)
    ms = _head_mean_square(proj[:, ATT_W:ATT_W + 2 * KV_W], bd)
    kk = _head_norm_rope(proj[:, ATT_W:ATT_W + KV_W], ms[:, 0:KV_W], gk_ref[...], cos, sin)
    vt = proj[:, ATT_W + KV_W:ATT_W + 2 * KV_W].T
    for g in range(NKV):
        k_ref[0, g] = kk[:, g * HD:(g + 1) * HD].astype(BF16)
        v_ref[0, g] = vt[g * HD:(g + 1) * HD, :].astype(BF16)
    p_ref[0] = proj[:, ATT_W + 2 * KV_W:]


def _in_projection(x, g1, sh, sc, w_in_bf, gq2, gk2, bd, cos_t, sin_t, tm):
    b, s, _ = x.shape
    row = lambda bi, i: (bi, 0, 0)
    tok = lambda bi, i: (bi, i, 0)
    const = lambda bi, i: (0, 0)
    return pl.pallas_call(
        _proj_body,
        grid=(b, s // tm),
        in_specs=[pl.BlockSpec((1, tm, D), tok),
                  pl.BlockSpec((1, D), const),
                  pl.BlockSpec((1, 1, D), row),
                  pl.BlockSpec((1, 1, D), row),
                  pl.BlockSpec((D, IN_W), const),
                  pl.BlockSpec((1, LANES), const),
                  pl.BlockSpec((1, LANES), const),
                  pl.BlockSpec((2 * LANES, 2 * LANES), const),
                  pl.BlockSpec((tm, LANES), lambda bi, i: (i, 0)),
                  pl.BlockSpec((tm, LANES), lambda bi, i: (i, 0))],
        out_specs=[pl.BlockSpec((1, tm, ATT_W), tok),
                   pl.BlockSpec((1, NKV, tm, HD), lambda bi, i: (bi, 0, i, 0)),
                   pl.BlockSpec((1, NKV, HD, tm), lambda bi, i: (bi, 0, 0, i)),
                   pl.BlockSpec((1, tm, 3 * HY_W), tok)],
        out_shape=[jax.ShapeDtypeStruct((b, s, ATT_W), BF16),
                   jax.ShapeDtypeStruct((b, NKV, s, HD), BF16),
                   jax.ShapeDtypeStruct((b, NKV, HD, s), BF16),
                   jax.ShapeDtypeStruct((b, s, 3 * HY_W), F32)],
        compiler_params=_cparams(("parallel", "parallel")),
        name="in_projection",
    )(x, g1, sh, sc, w_in_bf, gq2, gk2, bd, cos_t, sin_t)


ATT_TQ = 512
ATT_TK = 768
SK = SEQ + CTX
ATT_NCH = SK // ATT_TK


ATT_NQ = QPK * ATT_TQ
ATT_VR = HD + BF16_ROWS
assert ATT_NCH % 2 == 1
ATT_SHIFT_MAX = 120.0


def _attn_body(bound_ref, q_ref, k_ref, vt_ref, o_ref, s_ref, mx_ref, m_ref, acc_ref):
    qall = jnp.concatenate([q_ref[0, :, r * HD:(r + 1) * HD] for r in range(QPK)], axis=0)
    acc_ref[...] = jnp.zeros_like(acc_ref)
    bound = bound_ref[0]
    fixed_shift = 2.0 * bound <= ATT_SHIFT_MAX

    def finish():
        out = acc_ref[0:HD, :] * (1.0 / acc_ref[HD:HD + 1, :])
        for r in range(QPK):
            o_ref[0, :, r * HD:(r + 1) * HD] = out[:, r * ATT_TQ:(r + 1) * ATT_TQ].T.astype(BF16)

    @pl.when(fixed_shift)
    def _():
        def chunk(c, _):
            s = _dot(k_ref[0, 0, c], qall, _NT)
            acc_ref[...] += _dot(vt_ref[0, 0, c], jnp.exp2(s - bound).astype(BF16))
            return 0

        lax.fori_loop(0, ATT_NCH, chunk, 0, unroll=2)
        finish()

    @pl.when(jnp.logical_not(fixed_shift))
    def _():
        m_ref[...] = jnp.full(m_ref.shape, -1e30, F32)

        def scores(c, slot):
            s = _dot(k_ref[0, 0, c], qall, _NT)
            s_ref[slot] = s
            mx_ref[slot] = jnp.max(s, axis=0, keepdims=True)

        def update(c, slot):
            m_old = m_ref[...]
            m_new = jnp.maximum(m_old, mx_ref[slot])
            p = jnp.exp2(s_ref[slot] - m_new).astype(BF16)
            acc_ref[...] = jnp.exp2(m_old - m_new) * acc_ref[...] + _dot(vt_ref[0, 0, c], p)
            m_ref[...] = m_new

        scores(0, 0)

        def pair(i, _):
            c = 2 * i
            scores(c + 1, 1)
            update(c, 0)
            scores(c + 2, 0)
            update(c + 1, 1)
            return 0

        lax.fori_loop(0, ATT_NCH // 2, pair, 0)
        update(ATT_NCH - 1, 0)
        finish()


def _attention(bound, q, kch, vtch):
    b = q.shape[0]
    grid_spec = pltpu.PrefetchScalarGridSpec(
        num_scalar_prefetch=1,
        grid=(b, NKV, SEQ // ATT_TQ),
        in_specs=[pl.BlockSpec((1, ATT_TQ, QPK * HD), lambda bi, g, i, bd: (bi, i, g)),
                  pl.BlockSpec((1, 1, ATT_NCH, ATT_TK, HD), lambda bi, g, i, bd: (bi, g, 0, 0, 0)),
                  pl.BlockSpec((1, 1, ATT_NCH, ATT_VR, ATT_TK), lambda bi, g, i, bd: (bi, g, 0, 0, 0))],
        out_specs=pl.BlockSpec((1, ATT_TQ, QPK * HD), lambda bi, g, i, bd: (bi, i, g)),
        scratch_shapes=[pltpu.VMEM((2, ATT_TK, ATT_NQ), F32), pltpu.VMEM((2, 1, ATT_NQ), F32),
                        pltpu.VMEM((1, ATT_NQ), F32), pltpu.VMEM((ATT_VR, ATT_NQ), F32)],
    )
    return pl.pallas_call(
        _attn_body,
        grid_spec=grid_spec,
        out_shape=jax.ShapeDtypeStruct((b, SEQ, ATT_W), BF16),
        compiler_params=_cparams(("parallel", "parallel", "parallel")),
        name="attention",
    )(bound, q, kch, vtch)


SC_TM = 2048
SC_J = SC_TM // FN2


def _sconv_body(m1, a1, n1, m2, a2, n2, m3, a3, n3, w_ref, b_ref, u_ref, x2_ref):
    i = pl.program_id(1)
    last = pl.num_programs(1) - 1
    rows = lax.broadcasted_iota(I32, (SC_TM, HY_W), 0)

    def conv(main, prev, nxt, g):
        x = main[0]
        pr = jnp.where(i > 0, prev[0, SUBLANES - 1:SUBLANES, :], 0.0)
        nx = jnp.where(i < last, nxt[0, 0:1, :], 0.0)
        xm = jnp.where(rows == 0, pr, pltpu.roll(x, 1, 0))
        xp = jnp.where(rows == SC_TM - 1, nx, pltpu.roll(x, SC_TM - 1, 0))
        return (w_ref[g:g + 1, :] * xm + w_ref[3 + g:4 + g, :] * x + w_ref[6 + g:7 + g, :] * xp
                + b_ref[g:g + 1, :])

    def to_rj(t):
        return jnp.swapaxes(t.reshape(SC_J, FN2, HY_W), 0, 1).astype(BF16)

    x1 = conv(m1, a1, n1, 0)
    x2 = conv(m2, a2, n2, 1)
    v = conv(m3, a3, n3, 2)
    u_ref[0] = to_rj(v * x1)
    x2_ref[0] = to_rj(x2)


def _short_conv(p, cw9, cb3):
    b = p.shape[0]
    nblk8 = SEQ // SUBLANES
    step8 = SC_TM // SUBLANES
    specs = []
    for g in range(3):
        specs += [pl.BlockSpec((1, SC_TM, HY_W), lambda bi, i, g=g: (bi, i, g)),
                  pl.BlockSpec((1, SUBLANES, HY_W), lambda bi, i, g=g: (bi, jnp.maximum(i * step8 - 1, 0), g)),
                  pl.BlockSpec((1, SUBLANES, HY_W), lambda bi, i, g=g: (bi, jnp.minimum((i + 1) * step8, nblk8 - 1), g))]
    specs += [pl.BlockSpec((9, HY_W), lambda bi, i: (0, 0)), pl.BlockSpec((3, HY_W), lambda bi, i: (0, 0))]
    out = pl.BlockSpec((1, FN2, SC_J, HY_W), lambda bi, i: (bi, 0, i, 0))
    return pl.pallas_call(
        _sconv_body,
        grid=(b, SEQ // SC_TM),
        in_specs=specs,
        out_specs=[out, out],
        out_shape=[jax.ShapeDtypeStruct((b, FN2, SEQ // FN2, HY_W), BF16)] * 2,
        compiler_params=_cparams(("parallel", "parallel")),
        name="short_conv",
    )(p, p, p, p, p, p, p, p, p, cw9, cb3)


FILT_TR = 1024


def _filter_body(z_ref, t_ref, msk_ref, fwd_ref, w1_ref, b1_ref, w2_ref, b2_ref, w3_ref, fr_ref, dl_ref,
                 k_ref, s_ref):
    fr = fr_ref[...]
    h = jnp.sin(fr * (_dot3(z_ref[...], w1_ref[...]) + b1_ref[...]))
    h = jnp.sin(fr * (_dot3(h, w2_ref[...]) + b2_ref[...]))
    h = _dot3(h, w3_ref[...])
    h = jnp.where(fwd_ref[...] > 0.5, h[:, :HY_W], h[:, HY_W:])
    kern = h * jnp.exp(t_ref[...] * dl_ref[...]) * msk_ref[...]
    k_ref[...] = kern

    @pl.when(pl.program_id(0) == 0)
    def _():
        s_ref[...] = jnp.zeros_like(s_ref)

    s_ref[...] += jnp.sum(jnp.abs(kern), axis=0, keepdims=True)


def _implicit_filter(ztab, ttab, mtab, ftab, w1p, b1, w2, b2, w3, freq, negdelta):
    rowblk = lambda i: (i, 0)
    const = lambda i: (0, 0)
    col = pl.BlockSpec((FILT_TR, 1), rowblk)
    return pl.pallas_call(
        _filter_body,
        grid=(FN // FILT_TR,),
        in_specs=[pl.BlockSpec((FILT_TR, FORD), rowblk), col, col, col,
                  pl.BlockSpec((FORD, FORD), const),
                  pl.BlockSpec((1, FORD), const),
                  pl.BlockSpec((FORD, FORD), const),
                  pl.BlockSpec((1, FORD), const),
                  pl.BlockSpec((FORD, 2 * HY_W), const),
                  pl.BlockSpec((1, FORD), const),
                  pl.BlockSpec((1, HY_W), const)],
        out_specs=[pl.BlockSpec((FILT_TR, HY_W), rowblk),
                   pl.BlockSpec((1, HY_W), const)],
        out_shape=[jax.ShapeDtypeStruct((FN, HY_W), F32), jax.ShapeDtypeStruct((1, HY_W), F32)],
        compiler_params=_cparams(("arbitrary",)),
        name="implicit_filter",
    )(ztab, ttab, mtab, ftab, w1p, b1, w2, b2, w3, freq, negdelta)


DFT_G = 4
DFT_KB = 16
DFT_NP = FN1 // DFT_KB
DFT_UNROLL = 4


def _dot2c(fh, fl, zb):
    return _dot(fh, zb) + _dot(fl, zb)


def _lead_stage(src, fh, fl, dst_ref):
    def group(rg, _):
        r0 = rg * DFT_G
        rhs = jnp.concatenate([src(r0 + g) for g in range(DFT_G)], axis=1)
        blk = _dot2c(fh, fl, rhs)
        for g in range(DFT_G):
            dst_ref[r0 + g] = blk[:, g * LANES:(g + 1) * LANES]
        return 0

    lax.fori_loop(0, FN2 // DFT_G, group, 0, unroll=DFT_UNROLL)


def _lead_phase(src, lh_ref, ll_ref, p_ref, q_ref):
    for h in range(2):
        rows = slice(h * FN1, (h + 1) * FN1)
        _lead_stage(src, lh_ref[rows, :], ll_ref[rows, :], p_ref)
        q_ref[rows] = jnp.swapaxes(p_ref[...], 0, 1)


def _lane_cat(xs):
    return jnp.concatenate(xs, axis=1)


def _mid_forward(q_ref, k0, tr_ref, ti_ref, fh, fl, half):
    tr_t, ti_t = tr_ref[...].T, ti_ref[...].T
    brs, bis, trs, tis = [], [], [], []
    for g in range(DFT_G):
        jj = half * DFT_G + g
        ar, ai = q_ref[k0 + jj], q_ref[FN1 + k0 + jj]
        tr = jnp.broadcast_to(tr_t[:, jj:jj + 1], (FN2, LANES))
        ti = jnp.broadcast_to(ti_t[:, jj:jj + 1], (FN2, LANES))
        brs.append(ar * tr - ai * ti)
        bis.append(ar * ti + ai * tr)
        trs.append(tr)
        tis.append(ti)
    b = jnp.concatenate([_lane_cat(brs), _lane_cat(bis)], axis=0).astype(BF16)
    return _dot2c(fh, fl, b), _lane_cat(trs), _lane_cat(tis)


def _spectrum_body(k_ref, tr_ref, ti_ref, lh_ref, ll_ref, fh_ref, fl_ref, sc_ref, o_ref, p_ref, q_ref):
    ph = pl.program_id(1)

    @pl.when(ph == 0)
    def _():
        _lead_phase(lambda r: k_ref[r].astype(BF16), lh_ref, ll_ref, p_ref, q_ref)

    @pl.when(ph > 0)
    def _():
        k0 = (ph - 1) * DFT_KB
        sc = _lane_cat([sc_ref[...]] * DFT_G)
        for half in range(DFT_KB // DFT_G):
            x, _, _ = _mid_forward(q_ref, k0, tr_ref, ti_ref, fh_ref[...], fl_ref[...], half)
            x = x * sc
            for g in range(DFT_G):
                lanes = slice(g * LANES, (g + 1) * LANES)
                o_ref[0, half * DFT_G + g] = x[:FN2, lanes]
                o_ref[1, half * DFT_G + g] = x[FN2:, lanes]


def _mid_index(ph):
    return jnp.clip(ph - 1, 0, DFT_NP - 1)


def _filter_spectrum(kern_rj, twr, twi, lh, ll, fh, fl, scale):
    const = lambda c, ph: (0, 0)
    tw = pl.BlockSpec((DFT_KB, FN2), lambda c, ph: (_mid_index(ph), 0))
    return pl.pallas_call(
        _spectrum_body,
        grid=(HY_W // LANES, DFT_NP + 1),
        in_specs=[pl.BlockSpec((FN2, FN1, LANES), lambda c, ph: (0, 0, c)),
                  tw, tw,
                  pl.BlockSpec(lh.shape, const), pl.BlockSpec(ll.shape, const),
                  pl.BlockSpec(fh.shape, const), pl.BlockSpec(fl.shape, const),
                  pl.BlockSpec((1, LANES), lambda c, ph: (0, c))],
        out_specs=pl.BlockSpec((2, DFT_KB, FN2, LANES), lambda c, ph: (0, _mid_index(ph), 0, c)),
        out_shape=jax.ShapeDtypeStruct((2, FN1, FN2, HY_W), F32),
        scratch_shapes=[pltpu.VMEM((FN2, FN1, LANES), F32), pltpu.VMEM((2 * FN1, FN2, LANES), F32)],
        compiler_params=_cparams(("parallel", "arbitrary")),
        name="filter_spectrum",
    )(kern_rj, twr, twi, lh, ll, fh, fl, scale)


def _hconv_body(u_ref, x2_ref, kh_ref, tr_ref, ti_ref, lh_ref, ll_ref, fh_ref, fl_ref, gh_ref, gl_ref,
                oh_ref, ol_ref, bias_ref, o_ref, p_ref, q_ref):
    ph = pl.program_id(1)

    def both(ref, r):
        return jnp.concatenate([ref[0, r], ref[1, r]], axis=0)

    @pl.when(ph == 0)
    def _():
        _lead_phase(lambda r: both(u_ref, r), lh_ref, ll_ref, p_ref, q_ref)

    @pl.when((ph > 0) & (ph <= DFT_NP))
    def _():
        k0 = (ph - 1) * DFT_KB
        for half in range(DFT_KB // DFT_G):
            x, tr, ti = _mid_forward(q_ref, k0, tr_ref, ti_ref, fh_ref[...], fl_ref[...], half)
            xr, xi = x[:FN2], x[FN2:]
            kr = _lane_cat([kh_ref[0, half * DFT_G + g] for g in range(DFT_G)])
            ki = _lane_cat([kh_ref[1, half * DFT_G + g] for g in range(DFT_G)])
            y = jnp.concatenate([xr * kr - xi * ki, xr * ki + xi * kr], axis=0).astype(BF16)
            c = _dot2c(gh_ref[...], gl_ref[...], y)
            cr, ci = c[:FN2], c[FN2:]
            dr = cr * tr + ci * ti
            di = ci * tr - cr * ti
            for g in range(DFT_G):
                lanes = slice(g * LANES, (g + 1) * LANES)
                q_ref[k0 + half * DFT_G + g] = dr[:, lanes]
                q_ref[FN1 + k0 + half * DFT_G + g] = di[:, lanes]

    @pl.when(ph == DFT_NP + 1)
    def _():
        bias = bias_ref[...]
        p_ref[...] = jnp.swapaxes(q_ref[0:FN1], 0, 1)
        _lead_stage(lambda r: p_ref[r].astype(BF16), oh_ref[:, 0:FN1], ol_ref[:, 0:FN1], q_ref)
        p_ref[...] = jnp.swapaxes(q_ref[FN1:2 * FN1], 0, 1)
        oh2, ol2 = oh_ref[:, FN1:2 * FN1], ol_ref[:, FN1:2 * FN1]

        def group(rg, _):
            r0 = rg * DFT_G
            rhs = _lane_cat([p_ref[r0 + g].astype(BF16) for g in range(DFT_G)])
            blk = _dot2c(oh2, ol2, rhs)
            for g in range(DFT_G):
                r = r0 + g
                y = q_ref[r] + blk[:, g * LANES:(g + 1) * LANES]
                q_ref[r] = (y + both(u_ref, r).astype(F32) * bias) * both(x2_ref, r).astype(F32)
            return 0

        lax.fori_loop(0, FN2 // DFT_G, group, 0, unroll=DFT_UNROLL)
        p_ref[...] = jnp.swapaxes(q_ref[0:FN2], 0, 1)
        nj = SEQ // FN2
        for b in range(2):
            o_ref[b] = p_ref[b * nj:(b + 1) * nj].reshape(SEQ, LANES).astype(BF16)


def _hyena_conv(u_rj, x2_rj, khat, twr, twi, lead, fwd, inv, out, bias):
    const = lambda c, ph: (0, 0)
    nj = SEQ // FN2
    tw = pl.BlockSpec((DFT_KB, FN2), lambda c, ph: (_mid_index(ph), 0))
    sig = pl.BlockSpec((2, FN2, nj, LANES), lambda c, ph: (0, 0, 0, c))
    mats = [m for pair in (lead, fwd, inv, out) for m in pair]
    return pl.pallas_call(
        _hconv_body,
        grid=(HY_W // LANES, DFT_NP + 2),
        in_specs=[sig, sig,
                  pl.BlockSpec((2, DFT_KB, FN2, LANES), lambda c, ph: (0, _mid_index(ph), 0, c)),
                  tw, tw] + [pl.BlockSpec(m.shape, const) for m in mats]
                 + [pl.BlockSpec((1, LANES), lambda c, ph: (0, c))],
        out_specs=pl.BlockSpec((2, SEQ, LANES), lambda c, ph: (0, 0, c)),
        out_shape=jax.ShapeDtypeStruct((2, SEQ, HY_W), BF16),
        scratch_shapes=[pltpu.VMEM((FN2, FN1, LANES), F32), pltpu.VMEM((2 * FN1, FN2, LANES), F32)],
        compiler_params=_cparams(("parallel", "arbitrary")),
        name="hyena_conv",
    )(u_rj, x2_rj, khat, twr, twi, *mats, bias)


OP_TM = 512


def _outproj_body(att_ref, hy_ref, x_ref, w_ref, g1_ref, n2_ref, sh_ref, sc_ref, wr_ref,
                  x1_ref, h2_ref, lg_ref):
    a = jnp.concatenate([att_ref[0], hy_ref[0]], axis=1)
    x1 = x_ref[0] + g1_ref[0] * _dot(a, w_ref[...])
    x1_ref[0] = x1
    h2 = _rms_mod(x1, n2_ref[...], sh_ref[0], sc_ref[0])
    hh, hl = _split(h2)
    h2_ref[0] = hh
    wr = wr_ref[...]
    both = _dot(wr, hh, _NT)
    lg_ref[0] = both[0:NE] + both[NE:2 * NE] + _dot(wr[0:NE], hl, _NT)


def _out_projection(att, hy, x, w_out_bf, g1r, n2g, sh2, sc2, wr2):
    b = x.shape[0]
    tok = lambda bi, i: (bi, i, 0)
    row = lambda bi, i: (bi, 0, 0)
    const = lambda bi, i: (0, 0)
    return pl.pallas_call(
        _outproj_body,
        grid=(b, SEQ // OP_TM),
        in_specs=[pl.BlockSpec((1, OP_TM, ATT_W), tok),
                  pl.BlockSpec((1, OP_TM, HY_W), tok),
                  pl.BlockSpec((1, OP_TM, D), tok),
                  pl.BlockSpec((ATT_W + HY_W, D), const),
                  pl.BlockSpec((1, 1, D), row),
                  pl.BlockSpec((1, D), const),
                  pl.BlockSpec((1, 1, D), row),
                  pl.BlockSpec((1, 1, D), row),
                  pl.BlockSpec((2 * NE, D), const)],
        out_specs=[pl.BlockSpec((1, OP_TM, D), tok),
                   pl.BlockSpec((1, OP_TM, D), tok),
                   pl.BlockSpec((1, NE, OP_TM), lambda bi, i: (bi, 0, i))],
        out_shape=[jax.ShapeDtypeStruct((b, SEQ, D), F32),
                   jax.ShapeDtypeStruct((b, SEQ, D), BF16),
                   jax.ShapeDtypeStruct((b, NE, SEQ), F32)],
        compiler_params=_cparams(("parallel", "parallel")),
        name="out_projection",
    )(att, hy, x, w_out_bf, g1r, n2g, sh2, sc2, wr2)


def _routing_body(lg_ref, tri_ref, pos_ref, gate_ref, off_ref, cs_ref):
    lg = lg_ref[0]
    e = jnp.exp(lg - jnp.max(lg, axis=0, keepdims=True))
    aff = e / jnp.sum(e, axis=0, keepdims=True)
    gate_ref[0] = aff
    def count_ge(t):
        return jnp.sum(jnp.where(aff >= t, 1.0, 0.0), axis=1, keepdims=True)

    def bisect(i, thr):
        cand = thr | (jnp.int32(1) << (30 - i))
        return jnp.where(count_ge(pltpu.bitcast(cand, F32)) >= float(CAP), cand, thr)

    thr = lax.fori_loop(0, 31, bisect, jnp.zeros((NE, 1), I32))
    lo = pltpu.bitcast(thr, F32)
    hi = jnp.maximum(pltpu.bitcast(thr + 1, F32), jnp.finfo(F32).tiny)

    def refine(i, c):
        lo, hi = c
        mid = lo + (hi - lo) * 0.5
        ok = count_ge(mid) >= float(CAP)
        return jnp.where(ok, mid, lo), jnp.where(ok, hi, mid)

    lo, hi = lax.fori_loop(0, 32, refine, (lo, hi))
    gt = aff >= hi
    eq = (aff >= lo) & jnp.logical_not(gt)
    need = float(CAP) - jnp.sum(jnp.where(gt, 1.0, 0.0), axis=1, keepdims=True)
    tri = tri_ref[...]

    def excl_cumsum(mask_f, record_offsets):
        carry = jnp.zeros((NE, 1), F32)
        for c in range(NTCH):
            sl = slice(c * TCH, (c + 1) * TCH)
            m = mask_f[:, sl]
            inc = _dot(m.astype(BF16), tri)
            cs_ref[:, sl] = inc - m + carry
            if record_offsets:
                off_ref[0, :, c:c + 1] = carry.astype(I32)
            carry = carry + inc[:, TCH - 1:TCH]
        return cs_ref[...]

    eq_rank = excl_cumsum(jnp.where(eq, 1.0, 0.0), False)
    sel = gt | (eq & (eq_rank < need))
    pos = excl_cumsum(jnp.where(sel, 1.0, 0.0), True)
    pos_ref[0] = jnp.where(sel, pos.astype(I32), -1)


def _routing(logits, tri):
    b = logits.shape[0]
    blk = pl.BlockSpec((1, NE, SEQ), lambda bi: (bi, 0, 0))
    return pl.pallas_call(
        _routing_body,
        grid=(b,),
        in_specs=[blk, pl.BlockSpec((TCH, TCH), lambda bi: (0, 0))],
        out_specs=[blk, blk, pl.BlockSpec((1, NE, NTCH), lambda bi: (bi, 0, 0))],
        out_shape=[jax.ShapeDtypeStruct((b, NE, SEQ), I32),
                   jax.ShapeDtypeStruct((b, NE, SEQ), F32),
                   jax.ShapeDtypeStruct((b, NE, NTCH), I32)],
        scratch_shapes=[pltpu.VMEM((NE, SEQ), F32)],
        compiler_params=_cparams(("parallel",)),
        name="routing",
    )(logits, tri)


GATHER_UNROLL = 8


GW_SMALL = 64


def _gather_body(off_ref, h_ref, pos_ref, xg_ref, acc_ref):
    b = pl.program_id(0)
    e = pl.program_id(1)
    row0 = (b * NE + e) * NTCH
    acc_ref[...] = jnp.zeros_like(acc_ref)

    def count(c2, most):
        nxt = jnp.where(c2 + 1 < NTCH // 2, off_ref[row0 + jnp.minimum(2 * c2 + 2, NTCH - 1)], CAP)
        return jnp.maximum(most, nxt - off_ref[row0 + 2 * c2])

    most = lax.fori_loop(0, NTCH // 2, count, 0)

    def sweep(window, span):
        crow = lax.broadcasted_iota(I32, (window, span * TCH), 0)

        def chunks(i, _):
            for j in range(GATHER_UNROLL):
                c = (i * GATHER_UNROLL + j) * span
                off = off_ref[row0 + c]
                base = pl.multiple_of(jnp.minimum((off >> 3) << 3, CAP + SUBLANES - window), SUBLANES)
                t0 = pl.multiple_of(c * TCH, TCH)
                rel = _lane_cat([pos_ref[0, 0, pl.ds(c + s, 1), :] for s in range(span)]) - base
                onehot = jnp.where(crow == rel, 1.0, 0.0).astype(BF16)
                acc_ref[pl.ds(base, window), :] += _dot(onehot, h_ref[0, pl.ds(t0, span * TCH), :])
            return 0

        lax.fori_loop(0, NTCH // (GATHER_UNROLL * span), chunks, 0)

    @pl.when(most <= GW_SMALL - SUBLANES)
    def _():
        sweep(GW_SMALL, 2)

    @pl.when(most > GW_SMALL - SUBLANES)
    def _():
        sweep(GW, 1)

    xg_ref[0, 0] = acc_ref[0:CAP, :].astype(BF16)


def _gather(offs_flat, h2, pos4):
    b = h2.shape[0]
    grid_spec = pltpu.PrefetchScalarGridSpec(
        num_scalar_prefetch=1,
        grid=(b, NE),
        in_specs=[pl.BlockSpec((1, SEQ, D), lambda bi, e, off: (bi, 0, 0)),
                  pl.BlockSpec((1, 1, NTCH, TCH), lambda bi, e, off: (bi, e, 0, 0))],
        out_specs=pl.BlockSpec((1, 1, CAP, D), lambda bi, e, off: (bi, e, 0, 0)),
        scratch_shapes=[pltpu.VMEM((CAP + SUBLANES, D), F32)],
    )
    return pl.pallas_call(
        _gather_body,
        grid_spec=grid_spec,
        out_shape=jax.ShapeDtypeStruct((b, NE, CAP, D), BF16),
        compiler_params=_cparams(("parallel", "arbitrary")),
        name="moe_gather",
    )(offs_flat, h2, pos4)


FFN_TM = 512
FFN_NF = 4
FFN_FC = DEXP // FFN_NF
assert FFN_FC * FFN_NF == DEXP and FFN_FC % BF16_ROWS == 0


def _ffn_body(xg_ref, wgt_ref, wut_ref, wd_ref, y_ref, acc_ref):
    j = pl.program_id(1)
    nb = xg_ref.shape[0]

    @pl.when(j == 0)
    def _():
        acc_ref[...] = jnp.zeros_like(acc_ref)

    wgt = wgt_ref[0].astype(BF16)
    wut = wut_ref[0].astype(BF16)
    wd = wd_ref[0].astype(BF16)
    for b in range(nb):
        for mb in range(CAP // FFN_TM):
            rows = slice(mb * FFN_TM, (mb + 1) * FFN_TM)
            xb = xg_ref[b, 0, rows, :]
            a = _dot(xb, wgt, _NT)
            u = _dot(xb, wut, _NT)
            h = (a * (1.0 / (1.0 + jnp.exp(-a))) * u).astype(BF16)
            acc_ref[b, rows, :] += _dot(h, wd)

    @pl.when(j == FFN_NF - 1)
    def _():
        for b in range(nb):
            y_ref[b, 0, 0:CAP, :] = acc_ref[b].astype(BF16)
            y_ref[b, 0, CAP:YROWS, :] = jnp.zeros((YROWS - CAP, D), BF16)


def _expert_ffn(xg, w_gate_t, w_up_t, w_down):
    b = xg.shape[0]
    wblk = pl.BlockSpec((1, FFN_FC, D), lambda e, j: (e, j, 0))
    return pl.pallas_call(
        _ffn_body,
        grid=(NE, FFN_NF),
        in_specs=[pl.BlockSpec((b, 1, CAP, D), lambda e, j: (0, e, 0, 0)), wblk, wblk, wblk],
        out_specs=pl.BlockSpec((b, 1, YROWS, D), lambda e, j: (0, e, 0, 0)),
        out_shape=jax.ShapeDtypeStruct((b, NE, YROWS, D), BF16),
        scratch_shapes=[pltpu.VMEM((b, CAP, D), F32)],
        compiler_params=_cparams(("parallel", "arbitrary")),
        name="expert_ffn",
    )(xg, w_gate_t, w_up_t, w_down)


CW_SMALL = 64
CW_STACK = 2 * LANES // CW_SMALL


def _combine_body(off_ref, y_ref, pos_ref, gate_ref, x1_ref, g2_ref, o_ref):
    b = pl.program_id(0)
    i = pl.program_id(1)

    def offset(e, c):
        return off_ref[(b * NE + e) * NTCH + c]

    def window(e, rows):
        base = pl.multiple_of(jnp.minimum((offset(e, i) >> 4) << 4, YROWS - rows), BF16_ROWS)
        rel = pos_ref[0, e:e + 1, :] - base
        crow = lax.broadcasted_iota(I32, (rows, TCH), 0)
        w = jnp.where(crow == rel, gate_ref[0, e:e + 1, :], 0.0).astype(BF16)
        return w, y_ref[0, e, pl.ds(base, rows), :]

    most = jnp.int32(0)
    for e in range(NE):
        nxt = jnp.where(i + 1 < NTCH, offset(e, jnp.minimum(i + 1, NTCH - 1)), CAP)
        most = jnp.maximum(most, nxt - offset(e, i))

    @pl.when(most <= CW_SMALL - BF16_ROWS)
    def _():
        acc = jnp.zeros((TCH, D), F32)
        for e0 in range(0, NE, CW_STACK):
            ws, ys = zip(*[window(e, CW_SMALL) for e in range(e0, e0 + CW_STACK)])
            acc = acc + _dot(jnp.concatenate(ws, axis=0), jnp.concatenate(ys, axis=0), _TN)
        o_ref[0] = x1_ref[0] + g2_ref[0] * acc

    @pl.when(most > CW_SMALL - BF16_ROWS)
    def _():
        acc = jnp.zeros((TCH, D), F32)
        for e in range(NE):
            w, yw = window(e, CW)
            acc = acc + _dot(w, yw, _TN)
        o_ref[0] = x1_ref[0] + g2_ref[0] * acc


def _combine(offs_flat, y, pos, gate, x1, g2r):
    b = x1.shape[0]
    grid_spec = pltpu.PrefetchScalarGridSpec(
        num_scalar_prefetch=1,
        grid=(b, NTCH),
        in_specs=[pl.BlockSpec((1, NE, YROWS, D), lambda bi, i, off: (bi, 0, 0, 0),
                               pipeline_mode=pl.Buffered(1)),
                  pl.BlockSpec((1, NE, TCH), lambda bi, i, off: (bi, 0, i)),
                  pl.BlockSpec((1, NE, TCH), lambda bi, i, off: (bi, 0, i)),
                  pl.BlockSpec((1, TCH, D), lambda bi, i, off: (bi, i, 0)),
                  pl.BlockSpec((1, 1, D), lambda bi, i, off: (bi, 0, 0))],
        out_specs=pl.BlockSpec((1, TCH, D), lambda bi, i, off: (bi, i, 0)),
    )
    return pl.pallas_call(
        _combine_body,
        grid_spec=grid_spec,
        out_shape=jax.ShapeDtypeStruct((b, SEQ, D), F32),
        compiler_params=_cparams(("parallel", "arbitrary")),
        name="moe_combine",
    )(offs_flat, y, pos, gate, x1, g2r)


def _np_split(m):
    hi = np.asarray(m, np.float64).astype(BF16)
    lo = (m - hi.astype(np.float64)).astype(BF16)
    return jnp.asarray(hi), jnp.asarray(lo)


@functools.lru_cache(maxsize=None)
def _dft_tables():
    a = np.arange(FN1, dtype=np.float64)
    ang = 2.0 * np.pi * np.outer(a, a) / FN1
    fr, fi = np.cos(ang), -np.sin(ang)
    half = SEQ // FN2
    lead_u = np.block([[fr[:, :half], -fi[:, :half]], [fi[:, :half], fr[:, :half]]])
    lead_k = np.concatenate([fr, fi], axis=0)
    fwd = np.block([[fr, -fi], [fi, fr]])
    inv = np.block([[fr, fi], [-fi, fr]])
    out = np.block([[fr[:half], fi[:half]], [-fi[:half], fr[:half]]])
    n2 = np.arange(FN2, dtype=np.float64)
    tw = 2.0 * np.pi * np.outer(a, n2) / FN
    twr = np.cos(tw).astype(np.float32)
    twi = (-np.sin(tw)).astype(np.float32)
    return dict(lead_u=lead_u, lead_k=lead_k, fwd=fwd, inv=inv, out=out, twr=twr, twi=twi)


@functools.lru_cache(maxsize=None)
def _filter_tables():
    L = SEQ
    n = np.arange(FN).reshape(FN1, FN2).T.reshape(-1)
    lag = np.where(n < L, n, FN - n)
    jc = np.minimum(lag, L - 1).astype(np.float64)
    t = (jc / (L - 1))[:, None]
    bands = (FEMB - 1) // 2
    w = 2.0 * np.pi * jc / L
    f = np.linspace(1e-4, bands - 1, bands)
    fw = w[:, None] * f[None, :]
    z = np.concatenate([t, np.cos(fw), -np.sin(fw), np.zeros((FN, FORD - FEMB))], axis=-1)
    mask = np.where(n == L, 0.0, 1.0)[:, None]
    fwd = np.where(n < L, 1.0, 0.0)[:, None]
    max_decay = math.log(DECAY_TARGET) / FAST_DECAY_PCT
    min_decay = math.log(DECAY_TARGET) / SLOW_DECAY_PCT
    negdelta = -np.abs(np.linspace(min_decay, max_decay, HY_W))[None, :]
    return tuple(np.asarray(a, np.float32) for a in (z, t, mask, fwd, negdelta))


@functools.lru_cache(maxsize=None)
def _rope_tables(n):
    rows = n // GRID_W
    row_id, col_id = np.meshgrid(np.arange(rows, dtype=np.float64), np.arange(GRID_W, dtype=np.float64), indexing="ij")
    quarter = HD // 4
    inv_freq = ROPE_THETA ** (-np.arange(quarter, dtype=np.float64) / quarter)
    ar = row_id.reshape(-1)[:, None] * inv_freq
    ac = col_id.reshape(-1)[:, None] * inv_freq
    cos = np.concatenate([np.cos(ar), np.cos(ar), np.cos(ac), np.cos(ac)], axis=-1)
    sin = np.concatenate([-np.sin(ar), np.sin(ar), -np.sin(ac), np.sin(ac)], axis=-1)
    reps = (1, LANES // HD)
    return np.tile(cos, reps).astype(np.float32), np.tile(sin, reps).astype(np.float32)


def _hyena_long_conv(u_rj, x2_rj, kern, abs_sum, bias):
    tb = _dft_tables()
    twr, twi = tb["twr"], tb["twi"]
    fwd = _np_split(tb["fwd"])
    scale = 1.0 / (abs_sum * float(FN))
    khat = _filter_spectrum(kern.reshape(FN2, FN1, HY_W), twr, twi, *_np_split(tb["lead_k"]), *fwd, scale)
    return _hyena_conv(u_rj, x2_rj, khat, twr, twi, _np_split(tb["lead_u"]), fwd, _np_split(tb["inv"]),
                       _np_split(tb["out"]), bias.reshape(1, HY_W))


def kernel(x, c, ctx, c_ctx, w_mod, b_mod, norm1_g, norm2_g, w_in, w_out, q_norm_g, k_norm_g,
           conv_w, conv_b, filt_w1, filt_b1, filt_w2, filt_b2, filt_w3, filt_freq, hyena_bias,
           w_router, w_gate, w_up, w_down):
    B = x.shape[0]
    assert x.shape == (B, SEQ, D) and B == 2 and ctx.shape == (B, CTX, D) and w_mod.shape[0] == 1
    l = 0

    cc = jnp.concatenate([c, c_ctx[None, :], jnp.zeros((SUBLANES - B - 1, D), F32)], axis=0)
    mod = _modulation(cc, w_mod[l], b_mod[l][None, :])
    sh1, sc1, g1, sh2, sc2, g2 = [mod[:, i * D:(i + 1) * D] for i in range(6)]
    lat = lambda m: m[:B, None, :]
    ctxrow = lambda m: jnp.broadcast_to(m[B:B + 1, None, :], (B, 1, D))

    w_in_bf = w_in[l].astype(BF16)
    gq2 = jnp.tile(q_norm_g[l][None, :], (1, LANES // HD))
    gk2 = jnp.tile(k_norm_g[l][None, :], (1, LANES // HD))
    bd = jnp.asarray(np.kron(np.eye(2 * LANES // HD), np.full((HD, HD), 1.0 / HD)), BF16)
    cos_t, sin_t = _rope_tables(SEQ)
    n1g = norm1_g[l][None, :]

    q, k, vt, p = _in_projection(x, n1g, lat(sh1), lat(sc1), w_in_bf, gq2, gk2, bd, cos_t, sin_t, 512)
    _, kc, vct, _ = _in_projection(ctx, n1g, ctxrow(sh1), ctxrow(sc1), w_in_bf, gq2, gk2, bd,
                                   jnp.ones((CTX, LANES), F32), jnp.zeros((CTX, LANES), F32), CTX)

    kch = jnp.concatenate([k, kc], axis=2).reshape(B, NKV, ATT_NCH, ATT_TK, HD)
    vt_all = jnp.concatenate([vt, vct], axis=3).reshape(B, NKV, HD, ATT_NCH, ATT_TK)
    ones_pad = jnp.concatenate([jnp.ones((B, NKV, ATT_NCH, 1, ATT_TK), BF16),
                                jnp.zeros((B, NKV, ATT_NCH, BF16_ROWS - 1, ATT_TK), BF16)], axis=3)
    vtch = jnp.concatenate([vt_all.transpose(0, 1, 3, 2, 4), ones_pad], axis=3)
    bound = (1.02 * HD * Q_SCALE) * jnp.max(jnp.abs(q_norm_g[l])) * jnp.max(jnp.abs(k_norm_g[l]))
    att = _attention(bound.reshape(1).astype(F32), q, kch, vtch)

    cw9 = conv_w[l].reshape(3, 3, HY_W).reshape(9, HY_W)
    cb3 = conv_b[l].reshape(3, HY_W)
    u_rj, x2_rj = _short_conv(p, cw9, cb3)
    ztab, ttab, mtab, ftab, negdelta = _filter_tables()
    w1p = jnp.concatenate([filt_w1[l], jnp.zeros((FORD - FEMB, FORD), F32)], axis=0)
    kern, abs_sum = _implicit_filter(ztab, ttab, mtab, ftab, w1p, filt_b1[l][None, :], filt_w2[l],
                                     filt_b2[l][None, :], filt_w3[l], filt_freq[l][None, :], negdelta)
    hy = _hyena_long_conv(u_rj, x2_rj, kern, abs_sum, hyena_bias[l])

    wr2 = jnp.concatenate(_split(w_router[l].T), axis=0)
    x1, h2, logits = _out_projection(att, hy, x, w_out[l].astype(BF16), lat(g1), norm2_g[l][None, :],
                                     lat(sh2), lat(sc2), wr2)

    tri = jnp.asarray(np.triu(np.ones((TCH, TCH))), BF16)
    pos, gate, offs = _routing(logits, tri)
    offs_flat = offs.reshape(-1)
    xg = _gather(offs_flat, h2, pos.reshape(B, NE, NTCH, TCH))
    y = _expert_ffn(xg, jnp.swapaxes(w_gate[l], 1, 2), jnp.swapaxes(w_up[l], 1, 2), w_down[l])
    return _combine(offs_flat, y, pos, gate, x1, lat(g2))
```

```python
import functools
import math

import numpy as np
import jax
import jax.numpy as jnp
from jax import lax
from jax.experimental import pallas as pl
from jax.experimental.pallas import tpu as pltpu

F32 = jnp.float32
BF16 = jnp.bfloat16
I32 = jnp.int32

D = 1024
SEQ = 8192
CTX = 256
GRID_W = 64
ATT_W = 512
HY_W = 512
HD = 64
NQ = 8
NKV = 2
QPK = NQ // NKV
KV_W = NKV * HD
IN_W = ATT_W + 2 * KV_W + 3 * HY_W
FEMB = 33
FORD = 64
NE = 16
CAP = 2 * SEQ // NE
DEXP = 2752
ROPE_THETA = 10000.0
EPS = 1e-6
DECAY_TARGET = 1e-2
FAST_DECAY_PCT = 0.3
SLOW_DECAY_PCT = 1.5

LANES = 128
SUBLANES = 8
BF16_ROWS = 16
VMEM_BYTES_V7X = 64 * 1024 * 1024
VMEM_LIMIT = VMEM_BYTES_V7X - 8 * 1024 * 1024

FN = 2 * SEQ
FN1 = 128
FN2 = 128

TCH = LANES
NTCH = SEQ // TCH
GW = TCH + SUBLANES
CW = TCH + BF16_ROWS
YROWS = CAP + BF16_ROWS


def _cparams(sem, vmem=None):
    return pltpu.CompilerParams(dimension_semantics=sem, vmem_limit_bytes=vmem or VMEM_LIMIT)


def _split(a):
    hi = a.astype(BF16)
    lo = (a - hi.astype(F32)).astype(BF16)
    return hi, lo


_NN = (((1,), (0,)), ((), ()))
_NT = (((1,), (1,)), ((), ()))
_TN = (((0,), (0,)), ((), ()))


def _dot(a, b, dn=_NN):
    return lax.dot_general(a, b, dn, preferred_element_type=F32)


def _dot3(a, b, dn=_NN):
    ah, al = _split(a)
    bh, bl = _split(b)
    return _dot(ah, bh, dn) + _dot(ah, bl, dn) + _dot(al, bh, dn)


def _mod_body(c_ref, w_ref, b_ref, o_ref):
    c = c_ref[...]
    s = c * (1.0 / (1.0 + jnp.exp(-c)))
    o_ref[...] = _dot3(s, w_ref[...]) + b_ref[...]


def _modulation(cc, w_mod, b_mod):
    n = w_mod.shape[1]
    return pl.pallas_call(
        _mod_body,
        grid=(n // D,),
        in_specs=[pl.BlockSpec((SUBLANES, D), lambda j: (0, 0)),
                  pl.BlockSpec((D, D), lambda j: (0, j)),
                  pl.BlockSpec((1, D), lambda j: (0, j))],
        out_specs=pl.BlockSpec((SUBLANES, D), lambda j: (0, j)),
        out_shape=jax.ShapeDtypeStruct((SUBLANES, n), F32),
        compiler_params=_cparams(("arbitrary",)),
        name="modulation",
    )(cc, w_mod, b_mod)


Q_SCALE = HD ** -0.5 * math.log2(math.e)


def _rms_mod(x, g, sh, sc):
    ms = jnp.mean(x * x, axis=-1, keepdims=True)
    return (x * lax.rsqrt(ms + EPS) * g) * (1.0 + sc) + sh


def _head_mean_square(t, bd):
    hi, lo = _split(t * t)
    return _dot(hi, bd) + _dot(lo, bd)


def _head_norm_rope(t, ms, g, cos, sin):
    tn = t * lax.rsqrt(ms + EPS) * g
    lane = lax.broadcasted_iota(I32, tn.shape, 1)
    sw = jnp.where((lane & 31) < 16, pltpu.roll(tn, LANES - 16, 1), pltpu.roll(tn, 16, 1))
    return tn * cos + sw * sin


def _proj_body(x_ref, g_ref, sh_ref, sc_ref, w_ref, gq_ref, gk_ref, bd_ref, cos_ref, sin_ref,
               q_ref, k_ref, v_ref, p_ref):
    h = _rms_mod(x_ref[0], g_ref[...], sh_ref[0], sc_ref[0])
    proj = _dot(h.astype(BF16), w_ref[...])
    bd = bd_ref[...]
    cos = cos_ref[...]
    sin = sin_ref[...]
    wide = 2 * LANES
    for j in range(ATT_W // wide):
        ms = _head_mean_square(proj[:, j * wide:(j + 1) * wide], bd)
        for i in range(2):
            sl = slice(j * wide + i * LANES, j * wide + (i + 1) * LANES)
            qj = _head_norm_rope(proj[:, sl], ms[:, i * LANES:(i + 1) * LANES], gq_ref[...], cos, sin)
            q_ref[0, :, sl] = (qj * Q_SCALE).astype(BF16)
    ms = _head_mean_square(proj[:, ATT_W:ATT_W + 2 * KV_W], bd)
    kk = _head_norm_rope(proj[:, ATT_W:ATT_W + KV_W], ms[:, 0:KV_W], gk_ref[...], cos, sin)
    vt = proj[:, ATT_W + KV_W:ATT_W + 2 * KV_W].T
    for g in range(NKV):
        k_ref[0, g] = kk[:, g * HD:(g + 1) * HD].astype(BF16)
        v_ref[0, g] = vt[g * HD:(g + 1) * HD, :].astype(BF16)
    p_ref[0] = proj[:, ATT_W + 2 * KV_W:]


def _in_projection(x, g1, sh, sc, w_in_bf, gq2, gk2, bd, cos_t, sin_t, tm):
    b, s, _ = x.shape
    row = lambda bi, i: (bi, 0, 0)
    tok = lambda bi, i: (bi, i, 0)
    const = lambda bi, i: (0, 0)
    return pl.pallas_call(
        _proj_body,
        grid=(b, s // tm),
        in_specs=[pl.BlockSpec((1, tm, D), tok),
                  pl.BlockSpec((1, D), const),
                  pl.BlockSpec((1, 1, D), row),
                  pl.BlockSpec((1, 1, D), row),
                  pl.BlockSpec((D, IN_W), const),
                  pl.BlockSpec((1, LANES), const),
                  pl.BlockSpec((1, LANES), const),
                  pl.BlockSpec((2 * LANES, 2 * LANES), const),
                  pl.BlockSpec((tm, LANES), lambda bi, i: (i, 0)),
                  pl.BlockSpec((tm, LANES), lambda bi, i: (i, 0))],
        out_specs=[pl.BlockSpec((1, tm, ATT_W), tok),
                   pl.BlockSpec((1, NKV, tm, HD), lambda bi, i: (bi, 0, i, 0)),
                   pl.BlockSpec((1, NKV, HD, tm), lambda bi, i: (bi, 0, 0, i)),
                   pl.BlockSpec((1, tm, 3 * HY_W), tok)],
        out_shape=[jax.ShapeDtypeStruct((b, s, ATT_W), BF16),
                   jax.ShapeDtypeStruct((b, NKV, s, HD), BF16),
                   jax.ShapeDtypeStruct((b, NKV, HD, s), BF16),
                   jax.ShapeDtypeStruct((b, s, 3 * HY_W), F32)],
        compiler_params=_cparams(("parallel", "parallel")),
        name="in_projection",
    )(x, g1, sh, sc, w_in_bf, gq2, gk2, bd, cos_t, sin_t)


ATT_TQ = 512
ATT_TK = 768
SK = SEQ + CTX
ATT_NCH = SK // ATT_TK


ATT_NQ = QPK * ATT_TQ
ATT_VR = HD + BF16_ROWS
assert ATT_NCH % 2 == 1
ATT_SHIFT_MAX = 120.0


def _attn_body(bound_ref, q_ref, k_ref, vt_ref, o_ref, s_ref, mx_ref, m_ref, acc_ref):
    qall = jnp.concatenate([q_ref[0, :, r * HD:(r + 1) * HD] for r in range(QPK)], axis=0)
    acc_ref[...] = jnp.zeros_like(acc_ref)
    bound = bound_ref[0]
    fixed_shift = 2.0 * bound <= ATT_SHIFT_MAX

    def finish():
        out = acc_ref[0:HD, :] * (1.0 / acc_ref[HD:HD + 1, :])
        for r in range(QPK):
            o_ref[0, :, r * HD:(r + 1) * HD] = out[:, r * ATT_TQ:(r + 1) * ATT_TQ].T.astype(BF16)

    @pl.when(fixed_shift)
    def _():
        def chunk(c, _):
            s = _dot(k_ref[0, 0, c], qall, _NT)
            acc_ref[...] += _dot(vt_ref[0, 0, c], jnp.exp2(s - bound).astype(BF16))
            return 0

        lax.fori_loop(0, ATT_NCH, chunk, 0, unroll=2)
        finish()

    @pl.when(jnp.logical_not(fixed_shift))
    def _():
        m_ref[...] = jnp.full(m_ref.shape, -1e30, F32)

        def scores(c, slot):
            s = _dot(k_ref[0, 0, c], qall, _NT)
            s_ref[slot] = s
            mx_ref[slot] = jnp.max(s, axis=0, keepdims=True)

        def update(c, slot):
            m_old = m_ref[...]
            m_new = jnp.maximum(m_old, mx_ref[slot])
            p = jnp.exp2(s_ref[slot] - m_new).astype(BF16)
            acc_ref[...] = jnp.exp2(m_old - m_new) * acc_ref[...] + _dot(vt_ref[0, 0, c], p)
            m_ref[...] = m_new

        scores(0, 0)

        def pair(i, _):
            c = 2 * i
            scores(c + 1, 1)
            update(c, 0)
            scores(c + 2, 0)
            update(c + 1, 1)
            return 0

        lax.fori_loop(0, ATT_NCH // 2, pair, 0)
        update(ATT_NCH - 1, 0)
        finish()


def _attention(bound, q, kch, vtch):
    b = q.shape[0]
    grid_spec = pltpu.PrefetchScalarGridSpec(
        num_scalar_prefetch=1,
        grid=(b, NKV, SEQ // ATT_TQ),
        in_specs=[pl.BlockSpec((1, ATT_TQ, QPK * HD), lambda bi, g, i, bd: (bi, i, g)),
                  pl.BlockSpec((1, 1, ATT_NCH, ATT_TK, HD), lambda bi, g, i, bd: (bi, g, 0, 0, 0)),
                  pl.BlockSpec((1, 1, ATT_NCH, ATT_VR, ATT_TK), lambda bi, g, i, bd: (bi, g, 0, 0, 0))],
        out_specs=pl.BlockSpec((1, ATT_TQ, QPK * HD), lambda bi, g, i, bd: (bi, i, g)),
        scratch_shapes=[pltpu.VMEM((2, ATT_TK, ATT_NQ), F32), pltpu.VMEM((2, 1, ATT_NQ), F32),
                        pltpu.VMEM((1, ATT_NQ), F32), pltpu.VMEM((ATT_VR, ATT_NQ), F32)],
    )
    return pl.pallas_call(
        _attn_body,
        grid_spec=grid_spec,
        out_shape=jax.ShapeDtypeStruct((b, SEQ, ATT_W), BF16),
        compiler_params=_cparams(("parallel", "parallel", "parallel")),
        name="attention",
    )(bound, q, kch, vtch)


SC_TM = 2048
SC_J = SC_TM // FN2


def _sconv_body(m1, a1, n1, m2, a2, n2, m3, a3, n3, w_ref, b_ref, u_ref, x2_ref):
    i = pl.program_id(1)
    last = pl.num_programs(1) - 1
    rows = lax.broadcasted_iota(I32, (SC_TM, HY_W), 0)

    def conv(main, prev, nxt, g):
        x = main[0]
        pr = jnp.where(i > 0, prev[0, SUBLANES - 1:SUBLANES, :], 0.0)
        nx = jnp.where(i < last, nxt[0, 0:1, :], 0.0)
        xm = jnp.where(rows == 0, pr, pltpu.roll(x, 1, 0))
        xp = jnp.where(rows == SC_TM - 1, nx, pltpu.roll(x, SC_TM - 1, 0))
        return (w_ref[g:g + 1, :] * xm + w_ref[3 + g:4 + g, :] * x + w_ref[6 + g:7 + g, :] * xp
                + b_ref[g:g + 1, :])

    def to_rj(t):
        return jnp.swapaxes(t.reshape(SC_J, FN2, HY_W), 0, 1).astype(BF16)

    x1 = conv(m1, a1, n1, 0)
    x2 = conv(m2, a2, n2, 1)
    v = conv(m3, a3, n3, 2)
    u_ref[0] = to_rj(v * x1)
    x2_ref[0] = to_rj(x2)


def _short_conv(p, cw9, cb3):
    b = p.shape[0]
    nblk8 = SEQ // SUBLANES
    step8 = SC_TM // SUBLANES
    specs = []
    for g in range(3):
        specs += [pl.BlockSpec((1, SC_TM, HY_W), lambda bi, i, g=g: (bi, i, g)),
                  pl.BlockSpec((1, SUBLANES, HY_W), lambda bi, i, g=g: (bi, jnp.maximum(i * step8 - 1, 0), g)),
                  pl.BlockSpec((1, SUBLANES, HY_W), lambda bi, i, g=g: (bi, jnp.minimum((i + 1) * step8, nblk8 - 1), g))]
    specs += [pl.BlockSpec((9, HY_W), lambda bi, i: (0, 0)), pl.BlockSpec((3, HY_W), lambda bi, i: (0, 0))]
    out = pl.BlockSpec((1, FN2, SC_J, HY_W), lambda bi, i: (bi, 0, i, 0))
    return pl.pallas_call(
        _sconv_body,
        grid=(b, SEQ // SC_TM),
        in_specs=specs,
        out_specs=[out, out],
        out_shape=[jax.ShapeDtypeStruct((b, FN2, SEQ // FN2, HY_W), BF16)] * 2,
        compiler_params=_cparams(("parallel", "parallel")),
        name="short_conv",
    )(p, p, p, p, p, p, p, p, p, cw9, cb3)


FILT_TR = 1024


def _filter_body(z_ref, t_ref, msk_ref, fwd_ref, w1_ref, b1_ref, w2_ref, b2_ref, w3_ref, fr_ref, dl_ref,
                 k_ref, s_ref):
    fr = fr_ref[...]
    h = jnp.sin(fr * (_dot3(z_ref[...], w1_ref[...]) + b1_ref[...]))
    h = jnp.sin(fr * (_dot3(h, w2_ref[...]) + b2_ref[...]))
    h = _dot3(h, w3_ref[...])
    h = jnp.concatenate([h[:, :2 * HY_W], h[:, 2 * HY_W:]], axis=0)
    h = jnp.where(fwd_ref[...] > 0.5, h[:, :HY_W], h[:, HY_W:])
    kern = h * jnp.exp(t_ref[...] * dl_ref[...]) * msk_ref[...]
    k_ref[...] = kern

    @pl.when(pl.program_id(0) == 0)
    def _():
        s_ref[...] = jnp.zeros_like(s_ref)

    s_ref[...] += jnp.sum(jnp.abs(kern), axis=0, keepdims=True)


def _implicit_filter(ztab, ttab, mtab, ftab, w1p, b1, w2, b2, w3, freq, negdelta):
    rowblk = lambda i: (i, 0)
    const = lambda i: (0, 0)
    col = pl.BlockSpec((FILT_TR, 1), rowblk)
    return pl.pallas_call(
        _filter_body,
        grid=(FN // FILT_TR,),
        in_specs=[pl.BlockSpec((FILT_TR // 2, 2 * FORD), rowblk), col, col, col,
                  pl.BlockSpec((2 * FORD, 2 * FORD), const),
                  pl.BlockSpec((1, 2 * FORD), const),
                  pl.BlockSpec((2 * FORD, 2 * FORD), const),
                  pl.BlockSpec((1, 2 * FORD), const),
                  pl.BlockSpec((2 * FORD, 4 * HY_W), const),
                  pl.BlockSpec((1, 2 * FORD), const),
                  pl.BlockSpec((1, HY_W), const)],
        out_specs=[pl.BlockSpec((FILT_TR, HY_W), rowblk),
                   pl.BlockSpec((1, HY_W), const)],
        out_shape=[jax.ShapeDtypeStruct((FN, HY_W), F32), jax.ShapeDtypeStruct((1, HY_W), F32)],
        compiler_params=_cparams(("arbitrary",)),
        name="implicit_filter",
    )(ztab, ttab, mtab, ftab, w1p, b1, w2, b2, w3, freq, negdelta)


DFT_G = 4
DFT_KB = 16
DFT_NP = FN1 // DFT_KB
DFT_UNROLL = 4


def _dot2c(fh, fl, zb):
    return _dot(fh, zb) + _dot(fl, zb)


def _lead_stage(src, fh, fl, dst_ref):
    def group(rg, _):
        r0 = rg * DFT_G
        rhs = jnp.concatenate([src(r0 + g) for g in range(DFT_G)], axis=1)
        blk = _dot2c(fh, fl, rhs)
        for g in range(DFT_G):
            dst_ref[r0 + g] = blk[:, g * LANES:(g + 1) * LANES]
        return 0

    lax.fori_loop(0, FN2 // DFT_G, group, 0, unroll=DFT_UNROLL)


def _lead_phase(src, lh_ref, ll_ref, p_ref, q_ref):
    for h in range(2):
        rows = slice(h * FN1, (h + 1) * FN1)
        _lead_stage(src, lh_ref[rows, :], ll_ref[rows, :], p_ref)
        q_ref[rows] = jnp.swapaxes(p_ref[...], 0, 1)


def _lane_cat(xs):
    return jnp.concatenate(xs, axis=1)


def _mid_forward(q_ref, k0, tr_ref, ti_ref, fh, fl, half):
    tr_t, ti_t = tr_ref[...].T, ti_ref[...].T
    brs, bis, trs, tis = [], [], [], []
    for g in range(DFT_G):
        jj = half * DFT_G + g
        ar, ai = q_ref[k0 + jj], q_ref[FN1 + k0 + jj]
        tr = jnp.broadcast_to(tr_t[:, jj:jj + 1], (FN2, LANES))
        ti = jnp.broadcast_to(ti_t[:, jj:jj + 1], (FN2, LANES))
        brs.append(ar * tr - ai * ti)
        bis.append(ar * ti + ai * tr)
        trs.append(tr)
        tis.append(ti)
    b = jnp.concatenate([_lane_cat(brs), _lane_cat(bis)], axis=0).astype(BF16)
    return _dot2c(fh, fl, b), _lane_cat(trs), _lane_cat(tis)


def _spectrum_body(k_ref, tr_ref, ti_ref, lh_ref, ll_ref, fh_ref, fl_ref, sc_ref, o_ref, p_ref, q_ref):
    ph = pl.program_id(1)

    @pl.when(ph == 0)
    def _():
        _lead_phase(lambda r: k_ref[r].astype(BF16), lh_ref, ll_ref, p_ref, q_ref)

    @pl.when(ph > 0)
    def _():
        k0 = (ph - 1) * DFT_KB
        sc = _lane_cat([sc_ref[...]] * DFT_G)
        for half in range(DFT_KB // DFT_G):
            x, _, _ = _mid_forward(q_ref, k0, tr_ref, ti_ref, fh_ref[...], fl_ref[...], half)
            x = x * sc
            for g in range(DFT_G):
                lanes = slice(g * LANES, (g + 1) * LANES)
                o_ref[0, half * DFT_G + g] = x[:FN2, lanes]
                o_ref[1, half * DFT_G + g] = x[FN2:, lanes]


def _mid_index(ph):
    return jnp.clip(ph - 1, 0, DFT_NP - 1)


def _filter_spectrum(kern_rj, twr, twi, lh, ll, fh, fl, scale):
    const = lambda c, ph: (0, 0)
    tw = pl.BlockSpec((DFT_KB, FN2), lambda c, ph: (_mid_index(ph), 0))
    return pl.pallas_call(
        _spectrum_body,
        grid=(HY_W // LANES, DFT_NP + 1),
        in_specs=[pl.BlockSpec((FN2, FN1, LANES), lambda c, ph: (0, 0, c)),
                  tw, tw,
                  pl.BlockSpec(lh.shape, const), pl.BlockSpec(ll.shape, const),
                  pl.BlockSpec(fh.shape, const), pl.BlockSpec(fl.shape, const),
                  pl.BlockSpec((1, LANES), lambda c, ph: (0, c))],
        out_specs=pl.BlockSpec((2, DFT_KB, FN2, LANES), lambda c, ph: (0, _mid_index(ph), 0, c)),
        out_shape=jax.ShapeDtypeStruct((2, FN1, FN2, HY_W), F32),
        scratch_shapes=[pltpu.VMEM((FN2, FN1, LANES), F32), pltpu.VMEM((2 * FN1, FN2, LANES), F32)],
        compiler_params=_cparams(("parallel", "arbitrary")),
        name="filter_spectrum",
    )(kern_rj, twr, twi, lh, ll, fh, fl, scale)


def _hconv_body(u_ref, x2_ref, kh_ref, tr_ref, ti_ref, lh_ref, ll_ref, fh_ref, fl_ref, gh_ref, gl_ref,
                oh_ref, ol_ref, bias_ref, o_ref, p_ref, q_ref):
    ph = pl.program_id(1)

    def both(ref, r):
        return jnp.concatenate([ref[0, r], ref[1, r]], axis=0)

    @pl.when(ph == 0)
    def _():
        _lead_phase(lambda r: both(u_ref, r), lh_ref, ll_ref, p_ref, q_ref)

    @pl.when((ph > 0) & (ph <= DFT_NP))
    def _():
        k0 = (ph - 1) * DFT_KB
        for half in range(DFT_KB // DFT_G):
            x, tr, ti = _mid_forward(q_ref, k0, tr_ref, ti_ref, fh_ref[...], fl_ref[...], half)
            xr, xi = x[:FN2], x[FN2:]
            kr = _lane_cat([kh_ref[0, half * DFT_G + g] for g in range(DFT_G)])
            ki = _lane_cat([kh_ref[1, half * DFT_G + g] for g in range(DFT_G)])
            y = jnp.concatenate([xr * kr - xi * ki, xr * ki + xi * kr], axis=0).astype(BF16)
            c = _dot2c(gh_ref[...], gl_ref[...], y)
            cr, ci = c[:FN2], c[FN2:]
            dr = cr * tr + ci * ti
            di = ci * tr - cr * ti
            for g in range(DFT_G):
                lanes = slice(g * LANES, (g + 1) * LANES)
                q_ref[k0 + half * DFT_G + g] = dr[:, lanes]
                q_ref[FN1 + k0 + half * DFT_G + g] = di[:, lanes]

    @pl.when(ph == DFT_NP + 1)
    def _():
        bias = bias_ref[...]
        p_ref[...] = jnp.swapaxes(q_ref[0:FN1], 0, 1)
        _lead_stage(lambda r: p_ref[r].astype(BF16), oh_ref[:, 0:FN1], ol_ref[:, 0:FN1], q_ref)
        p_ref[...] = jnp.swapaxes(q_ref[FN1:2 * FN1], 0, 1)
        oh2, ol2 = oh_ref[:, FN1:2 * FN1], ol_ref[:, FN1:2 * FN1]

        def group(rg, _):
            r0 = rg * DFT_G
            rhs = _lane_cat([p_ref[r0 + g].astype(BF16) for g in range(DFT_G)])
            blk = _dot2c(oh2, ol2, rhs)
            for g in range(DFT_G):
                r = r0 + g
                y = q_ref[r] + blk[:, g * LANES:(g + 1) * LANES]
                q_ref[r] = (y + both(u_ref, r).astype(F32) * bias) * both(x2_ref, r).astype(F32)
            return 0

        lax.fori_loop(0, FN2 // DFT_G, group, 0, unroll=DFT_UNROLL)
        p_ref[...] = jnp.swapaxes(q_ref[0:FN2], 0, 1)
        nj = SEQ // FN2
        for b in range(2):
            o_ref[b] = p_ref[b * nj:(b + 1) * nj].reshape(SEQ, LANES).astype(BF16)


def _hyena_conv(u_rj, x2_rj, khat, twr, twi, lead, fwd, inv, out, bias):
    const = lambda c, ph: (0, 0)
    nj = SEQ // FN2
    tw = pl.BlockSpec((DFT_KB, FN2), lambda c, ph: (_mid_index(ph), 0))
    sig = pl.BlockSpec((2, FN2, nj, LANES), lambda c, ph: (0, 0, 0, c))
    mats = [m for pair in (lead, fwd, inv, out) for m in pair]
    return pl.pallas_call(
        _hconv_body,
        grid=(HY_W // LANES, DFT_NP + 2),
        in_specs=[sig, sig,
                  pl.BlockSpec((2, DFT_KB, FN2, LANES), lambda c, ph: (0, _mid_index(ph), 0, c)),
                  tw, tw] + [pl.BlockSpec(m.shape, const) for m in mats]
                 + [pl.BlockSpec((1, LANES), lambda c, ph: (0, c))],
        out_specs=pl.BlockSpec((2, SEQ, LANES), lambda c, ph: (0, 0, c)),
        out_shape=jax.ShapeDtypeStruct((2, SEQ, HY_W), BF16),
        scratch_shapes=[pltpu.VMEM((FN2, FN1, LANES), F32), pltpu.VMEM((2 * FN1, FN2, LANES), F32)],
        compiler_params=_cparams(("parallel", "arbitrary")),
        name="hyena_conv",
    )(u_rj, x2_rj, khat, twr, twi, *mats, bias)


OP_TM = 512


def _outproj_body(att_ref, hy_ref, x_ref, w_ref, g1_ref, n2_ref, sh_ref, sc_ref, wr_ref,
                  x1_ref, h2_ref, lg_ref):
    a = jnp.concatenate([att_ref[0], hy_ref[0]], axis=1)
    x1 = x_ref[0] + g1_ref[0] * _dot(a, w_ref[...])
    x1_ref[0] = x1
    h2 = _rms_mod(x1, n2_ref[...], sh_ref[0], sc_ref[0])
    hh, hl = _split(h2)
    h2_ref[0] = hh
    wr = wr_ref[...]
    both = _dot(wr, hh, _NT)
    lg_ref[0] = both[0:NE] + both[NE:2 * NE] + _dot(wr[0:NE], hl, _NT)


def _out_projection(att, hy, x, w_out_bf, g1r, n2g, sh2, sc2, wr2):
    b = x.shape[0]
    tok = lambda bi, i: (bi, i, 0)
    row = lambda bi, i: (bi, 0, 0)
    const = lambda bi, i: (0, 0)
    return pl.pallas_call(
        _outproj_body,
        grid=(b, SEQ // OP_TM),
        in_specs=[pl.BlockSpec((1, OP_TM, ATT_W), tok),
                  pl.BlockSpec((1, OP_TM, HY_W), tok),
                  pl.BlockSpec((1, OP_TM, D), tok),
                  pl.BlockSpec((ATT_W + HY_W, D), const),
                  pl.BlockSpec((1, 1, D), row),
                  pl.BlockSpec((1, D), const),
                  pl.BlockSpec((1, 1, D), row),
                  pl.BlockSpec((1, 1, D), row),
                  pl.BlockSpec((2 * NE, D), const)],
        out_specs=[pl.BlockSpec((1, OP_TM, D), tok),
                   pl.BlockSpec((1, OP_TM, D), tok),
                   pl.BlockSpec((1, NE, OP_TM), lambda bi, i: (bi, 0, i))],
        out_shape=[jax.ShapeDtypeStruct((b, SEQ, D), F32),
                   jax.ShapeDtypeStruct((b, SEQ, D), BF16),
                   jax.ShapeDtypeStruct((b, NE, SEQ), F32)],
        compiler_params=_cparams(("parallel", "parallel")),
        name="out_projection",
    )(att, hy, x, w_out_bf, g1r, n2g, sh2, sc2, wr2)


def _routing_body(lg_ref, tri_ref, pos_ref, gate_ref, off_ref, cs_ref):
    lg = lg_ref[0]
    e = jnp.exp(lg - jnp.max(lg, axis=0, keepdims=True))
    aff = e / jnp.sum(e, axis=0, keepdims=True)
    gate_ref[0] = aff
    def count_ge(t):
        return jnp.sum(jnp.where(aff >= t, 1.0, 0.0), axis=1, keepdims=True)

    def bisect(i, thr):
        cand = thr | (jnp.int32(1) << (30 - i))
        return jnp.where(count_ge(pltpu.bitcast(cand, F32)) >= float(CAP), cand, thr)

    thr = lax.fori_loop(0, 31, bisect, jnp.zeros((NE, 1), I32))
    lo = pltpu.bitcast(thr, F32)
    hi = jnp.maximum(pltpu.bitcast(thr + 1, F32), jnp.finfo(F32).tiny)

    def refine(i, c):
        lo, hi = c
        mid = lo + (hi - lo) * 0.5
        ok = count_ge(mid) >= float(CAP)
        return jnp.where(ok, mid, lo), jnp.where(ok, hi, mid)

    lo, hi = lax.fori_loop(0, 32, refine, (lo, hi))
    gt = aff >= hi
    eq = (aff >= lo) & jnp.logical_not(gt)
    need = float(CAP) - jnp.sum(jnp.where(gt, 1.0, 0.0), axis=1, keepdims=True)
    tri = tri_ref[...]

    def excl_cumsum(mask_f, record_offsets):
        carry = jnp.zeros((NE, 1), F32)
        for c in range(NTCH):
            sl = slice(c * TCH, (c + 1) * TCH)
            m = mask_f[:, sl]
            inc = _dot(m.astype(BF16), tri)
            cs_ref[:, sl] = inc - m + carry
            if record_offsets:
                off_ref[0, :, c:c + 1] = carry.astype(I32)
            carry = carry + inc[:, TCH - 1:TCH]
        return cs_ref[...]

    eq_rank = excl_cumsum(jnp.where(eq, 1.0, 0.0), False)
    sel = gt | (eq & (eq_rank < need))
    pos = excl_cumsum(jnp.where(sel, 1.0, 0.0), True)
    pos_ref[0] = jnp.where(sel, pos.astype(I32), -1)


def _routing(logits, tri):
    b = logits.shape[0]
    blk = pl.BlockSpec((1, NE, SEQ), lambda bi: (bi, 0, 0))
    return pl.pallas_call(
        _routing_body,
        grid=(b,),
        in_specs=[blk, pl.BlockSpec((TCH, TCH), lambda bi: (0, 0))],
        out_specs=[blk, blk, pl.BlockSpec((1, NE, NTCH), lambda bi: (bi, 0, 0))],
        out_shape=[jax.ShapeDtypeStruct((b, NE, SEQ), I32),
                   jax.ShapeDtypeStruct((b, NE, SEQ), F32),
                   jax.ShapeDtypeStruct((b, NE, NTCH), I32)],
        scratch_shapes=[pltpu.VMEM((NE, SEQ), F32)],
        compiler_params=_cparams(("parallel",)),
        name="routing",
    )(logits, tri)


GATHER_UNROLL = 8


GW_SMALL = 64


def _gather_body(off_ref, h_ref, pos_ref, xg_ref, acc_ref):
    b = pl.program_id(0)
    e = pl.program_id(1)
    row0 = (b * NE + e) * NTCH
    acc_ref[...] = jnp.zeros_like(acc_ref)

    def count(c2, most):
        nxt = jnp.where(c2 + 1 < NTCH // 2, off_ref[row0 + jnp.minimum(2 * c2 + 2, NTCH - 1)], CAP)
        return jnp.maximum(most, nxt - off_ref[row0 + 2 * c2])

    most = lax.fori_loop(0, NTCH // 2, count, 0)

    def sweep(window, span):
        crow = lax.broadcasted_iota(I32, (window, span * TCH), 0)

        def chunks(i, _):
            for j in range(GATHER_UNROLL):
                c = (i * GATHER_UNROLL + j) * span
                off = off_ref[row0 + c]
                base = pl.multiple_of(jnp.minimum((off >> 3) << 3, CAP + SUBLANES - window), SUBLANES)
                t0 = pl.multiple_of(c * TCH, TCH)
                rel = _lane_cat([pos_ref[0, 0, pl.ds(c + s, 1), :] for s in range(span)]) - base
                onehot = jnp.where(crow == rel, 1.0, 0.0).astype(BF16)
                acc_ref[pl.ds(base, window), :] += _dot(onehot, h_ref[0, pl.ds(t0, span * TCH), :])
            return 0

        lax.fori_loop(0, NTCH // (GATHER_UNROLL * span), chunks, 0)

    @pl.when(most <= GW_SMALL - SUBLANES)
    def _():
        sweep(GW_SMALL, 2)

    @pl.when(most > GW_SMALL - SUBLANES)
    def _():
        sweep(GW, 1)

    xg_ref[0, 0] = acc_ref[0:CAP, :].astype(BF16)


def _gather(offs_flat, h2, pos4):
    b = h2.shape[0]
    grid_spec = pltpu.PrefetchScalarGridSpec(
        num_scalar_prefetch=1,
        grid=(b, NE),
        in_specs=[pl.BlockSpec((1, SEQ, D), lambda bi, e, off: (bi, 0, 0)),
                  pl.BlockSpec((1, 1, NTCH, TCH), lambda bi, e, off: (bi, e, 0, 0))],
        out_specs=pl.BlockSpec((1, 1, CAP, D), lambda bi, e, off: (bi, e, 0, 0)),
        scratch_shapes=[pltpu.VMEM((CAP + SUBLANES, D), F32)],
    )
    return pl.pallas_call(
        _gather_body,
        grid_spec=grid_spec,
        out_shape=jax.ShapeDtypeStruct((b, NE, CAP, D), BF16),
        compiler_params=_cparams(("parallel", "arbitrary")),
        name="moe_gather",
    )(offs_flat, h2, pos4)


FFN_TM = 512
FFN_NF = 4
FFN_FC = DEXP // FFN_NF
assert FFN_FC * FFN_NF == DEXP and FFN_FC % BF16_ROWS == 0


def _ffn_body(xg_ref, wgt_ref, wut_ref, wd_ref, y_ref, acc_ref):
    j = pl.program_id(1)
    nb = xg_ref.shape[0]

    @pl.when(j == 0)
    def _():
        acc_ref[...] = jnp.zeros_like(acc_ref)

    wgt = wgt_ref[0].astype(BF16)
    wut = wut_ref[0].astype(BF16)
    wd = wd_ref[0].astype(BF16)
    for b in range(nb):
        for mb in range(CAP // FFN_TM):
            rows = slice(mb * FFN_TM, (mb + 1) * FFN_TM)
            xb = xg_ref[b, 0, rows, :]
            a = _dot(xb, wgt, _NT)
            u = _dot(xb, wut, _NT)
            h = (a * (1.0 / (1.0 + jnp.exp(-a))) * u).astype(BF16)
            acc_ref[b, rows, :] += _dot(h, wd)

    @pl.when(j == FFN_NF - 1)
    def _():
        for b in range(nb):
            y_ref[b, 0, 0:CAP, :] = acc_ref[b].astype(BF16)
            y_ref[b, 0, CAP:YROWS, :] = jnp.zeros((YROWS - CAP, D), BF16)


def _expert_ffn(xg, w_gate_t, w_up_t, w_down):
    b = xg.shape[0]
    wblk = pl.BlockSpec((1, FFN_FC, D), lambda e, j: (e, j, 0))
    return pl.pallas_call(
        _ffn_body,
        grid=(NE, FFN_NF),
        in_specs=[pl.BlockSpec((b, 1, CAP, D), lambda e, j: (0, e, 0, 0)), wblk, wblk, wblk],
        out_specs=pl.BlockSpec((b, 1, YROWS, D), lambda e, j: (0, e, 0, 0)),
        out_shape=jax.ShapeDtypeStruct((b, NE, YROWS, D), BF16),
        scratch_shapes=[pltpu.VMEM((b, CAP, D), F32)],
        compiler_params=_cparams(("parallel", "arbitrary")),
        name="expert_ffn",
    )(xg, w_gate_t, w_up_t, w_down)


CW_SMALL = 64
CW_STACK = 2 * LANES // CW_SMALL


def _combine_body(off_ref, y_ref, pos_ref, gate_ref, x1_ref, g2_ref, o_ref):
    b = pl.program_id(0)
    i = pl.program_id(1)

    def offset(e, c):
        return off_ref[(b * NE + e) * NTCH + c]

    def window(e, rows):
        base = pl.multiple_of(jnp.minimum((offset(e, i) >> 4) << 4, YROWS - rows), BF16_ROWS)
        rel = pos_ref[0, e:e + 1, :] - base
        crow = lax.broadcasted_iota(I32, (rows, TCH), 0)
        w = jnp.where(crow == rel, gate_ref[0, e:e + 1, :], 0.0).astype(BF16)
        return w, y_ref[0, e, pl.ds(base, rows), :]

    most = jnp.int32(0)
    for e in range(NE):
        nxt = jnp.where(i + 1 < NTCH, offset(e, jnp.minimum(i + 1, NTCH - 1)), CAP)
        most = jnp.maximum(most, nxt - offset(e, i))

    @pl.when(most <= CW_SMALL - BF16_ROWS)
    def _():
        acc = jnp.zeros((TCH, D), F32)
        for e0 in range(0, NE, CW_STACK):
            ws, ys = zip(*[window(e, CW_SMALL) for e in range(e0, e0 + CW_STACK)])
            acc = acc + _dot(jnp.concatenate(ws, axis=0), jnp.concatenate(ys, axis=0), _TN)
        o_ref[0] = x1_ref[0] + g2_ref[0] * acc

    @pl.when(most > CW_SMALL - BF16_ROWS)
    def _():
        acc = jnp.zeros((TCH, D), F32)
        for e in range(NE):
            w, yw = window(e, CW)
            acc = acc + _dot(w, yw, _TN)
        o_ref[0] = x1_ref[0] + g2_ref[0] * acc


def _combine(offs_flat, y, pos, gate, x1, g2r):
    b = x1.shape[0]
    grid_spec = pltpu.PrefetchScalarGridSpec(
        num_scalar_prefetch=1,
        grid=(b, NTCH),
        in_specs=[pl.BlockSpec((1, NE, YROWS, D), lambda bi, i, off: (bi, 0, 0, 0),
                               pipeline_mode=pl.Buffered(1)),
                  pl.BlockSpec((1, NE, TCH), lambda bi, i, off: (bi, 0, i)),
                  pl.BlockSpec((1, NE, TCH), lambda bi, i, off: (bi, 0, i)),
                  pl.BlockSpec((1, TCH, D), lambda bi, i, off: (bi, i, 0)),
                  pl.BlockSpec((1, 1, D), lambda bi, i, off: (bi, 0, 0))],
        out_specs=pl.BlockSpec((1, TCH, D), lambda bi, i, off: (bi, i, 0)),
    )
    return pl.pallas_call(
        _combine_body,
        grid_spec=grid_spec,
        out_shape=jax.ShapeDtypeStruct((b, SEQ, D), F32),
        compiler_params=_cparams(("parallel", "arbitrary")),
        name="moe_combine",
    )(offs_flat, y, pos, gate, x1, g2r)


def _np_split(m):
    hi = np.asarray(m, np.float64).astype(BF16)
    lo = (m - hi.astype(np.float64)).astype(BF16)
    return jnp.asarray(hi), jnp.asarray(lo)


@functools.lru_cache(maxsize=None)
def _dft_tables():
    a = np.arange(FN1, dtype=np.float64)
    ang = 2.0 * np.pi * np.outer(a, a) / FN1
    fr, fi = np.cos(ang), -np.sin(ang)
    half = SEQ // FN2
    lead_u = np.block([[fr[:, :half], -fi[:, :half]], [fi[:, :half], fr[:, :half]]])
    lead_k = np.concatenate([fr, fi], axis=0)
    fwd = np.block([[fr, -fi], [fi, fr]])
    inv = np.block([[fr, fi], [-fi, fr]])
    out = np.block([[fr[:half], fi[:half]], [-fi[:half], fr[:half]]])
    n2 = np.arange(FN2, dtype=np.float64)
    tw = 2.0 * np.pi * np.outer(a, n2) / FN
    twr = np.cos(tw).astype(np.float32)
    twi = (-np.sin(tw)).astype(np.float32)
    return dict(lead_u=lead_u, lead_k=lead_k, fwd=fwd, inv=inv, out=out, twr=twr, twi=twi)


@functools.lru_cache(maxsize=None)
def _filter_tables():
    L = SEQ
    n = np.arange(FN).reshape(FN1, FN2).T.reshape(-1)
    lag = np.where(n < L, n, FN - n)
    jc = np.minimum(lag, L - 1).astype(np.float64)
    t = (jc / (L - 1))[:, None]
    bands = (FEMB - 1) // 2
    w = 2.0 * np.pi * jc / L
    f = np.linspace(1e-4, bands - 1, bands)
    fw = w[:, None] * f[None, :]
    z = np.concatenate([t, np.cos(fw), -np.sin(fw), np.zeros((FN, FORD - FEMB))], axis=-1)
    mask = np.where(n == L, 0.0, 1.0)[:, None]
    fwd = np.where(n < L, 1.0, 0.0)[:, None]
    max_decay = math.log(DECAY_TARGET) / FAST_DECAY_PCT
    min_decay = math.log(DECAY_TARGET) / SLOW_DECAY_PCT
    negdelta = -np.abs(np.linspace(min_decay, max_decay, HY_W))[None, :]
    z = z.reshape(FN // FILT_TR, 2, FILT_TR // 2, FORD).transpose(0, 2, 1, 3).reshape(FN // 2, 2 * FORD)
    return tuple(np.asarray(a, np.float32) for a in (z, t, mask, fwd, negdelta))


@functools.lru_cache(maxsize=None)
def _rope_tables(n):
    rows = n // GRID_W
    row_id, col_id = np.meshgrid(np.arange(rows, dtype=np.float64), np.arange(GRID_W, dtype=np.float64), indexing="ij")
    quarter = HD // 4
    inv_freq = ROPE_THETA ** (-np.arange(quarter, dtype=np.float64) / quarter)
    ar = row_id.reshape(-1)[:, None] * inv_freq
    ac = col_id.reshape(-1)[:, None] * inv_freq
    cos = np.concatenate([np.cos(ar), np.cos(ar), np.cos(ac), np.cos(ac)], axis=-1)
    sin = np.concatenate([-np.sin(ar), np.sin(ar), -np.sin(ac), np.sin(ac)], axis=-1)
    reps = (1, LANES // HD)
    return np.tile(cos, reps).astype(np.float32), np.tile(sin, reps).astype(np.float32)


def _hyena_long_conv(u_rj, x2_rj, kern, abs_sum, bias):
    tb = _dft_tables()
    twr, twi = tb["twr"], tb["twi"]
    fwd = _np_split(tb["fwd"])
    scale = 1.0 / (abs_sum * float(FN))
    khat = _filter_spectrum(kern.reshape(FN2, FN1, HY_W), twr, twi, *_np_split(tb["lead_k"]), *fwd, scale)
    return _hyena_conv(u_rj, x2_rj, khat, twr, twi, _np_split(tb["lead_u"]), fwd, _np_split(tb["inv"]),
                       _np_split(tb["out"]), bias.reshape(1, HY_W))


def kernel(x, c, ctx, c_ctx, w_mod, b_mod, norm1_g, norm2_g, w_in, w_out, q_norm_g, k_norm_g,
           conv_w, conv_b, filt_w1, filt_b1, filt_w2, filt_b2, filt_w3, filt_freq, hyena_bias,
           w_router, w_gate, w_up, w_down):
    B = x.shape[0]
    assert x.shape == (B, SEQ, D) and B == 2 and ctx.shape == (B, CTX, D) and w_mod.shape[0] == 1
    l = 0

    cc = jnp.concatenate([c, c_ctx[None, :], jnp.zeros((SUBLANES - B - 1, D), F32)], axis=0)
    mod = _modulation(cc, w_mod[l], b_mod[l][None, :])
    sh1, sc1, g1, sh2, sc2, g2 = [mod[:, i * D:(i + 1) * D] for i in range(6)]
    lat = lambda m: m[:B, None, :]
    ctxrow = lambda m: jnp.broadcast_to(m[B:B + 1, None, :], (B, 1, D))

    w_in_bf = w_in[l].astype(BF16)
    gq2 = jnp.tile(q_norm_g[l][None, :], (1, LANES // HD))
    gk2 = jnp.tile(k_norm_g[l][None, :], (1, LANES // HD))
    bd = jnp.asarray(np.kron(np.eye(2 * LANES // HD), np.full((HD, HD), 1.0 / HD)), BF16)
    cos_t, sin_t = _rope_tables(SEQ)
    n1g = norm1_g[l][None, :]

    q, k, vt, p = _in_projection(x, n1g, lat(sh1), lat(sc1), w_in_bf, gq2, gk2, bd, cos_t, sin_t, 512)
    _, kc, vct, _ = _in_projection(ctx, n1g, ctxrow(sh1), ctxrow(sc1), w_in_bf, gq2, gk2, bd,
                                   jnp.ones((CTX, LANES), F32), jnp.zeros((CTX, LANES), F32), CTX)

    kch = jnp.concatenate([k, kc], axis=2).reshape(B, NKV, ATT_NCH, ATT_TK, HD)
    vt_all = jnp.concatenate([vt, vct], axis=3).reshape(B, NKV, HD, ATT_NCH, ATT_TK)
    ones_pad = jnp.concatenate([jnp.ones((B, NKV, ATT_NCH, 1, ATT_TK), BF16),
                                jnp.zeros((B, NKV, ATT_NCH, BF16_ROWS - 1, ATT_TK), BF16)], axis=3)
    vtch = jnp.concatenate([vt_all.transpose(0, 1, 3, 2, 4), ones_pad], axis=3)
    bound = (1.02 * HD * Q_SCALE) * jnp.max(jnp.abs(q_norm_g[l])) * jnp.max(jnp.abs(k_norm_g[l]))
    att = _attention(bound.reshape(1).astype(F32), q, kch, vtch)

    cw9 = conv_w[l].reshape(3, 3, HY_W).reshape(9, HY_W)
    cb3 = conv_b[l].reshape(3, HY_W)
    u_rj, x2_rj = _short_conv(p, cw9, cb3)
    ztab, ttab, mtab, ftab, negdelta = _filter_tables()
    w1p = jnp.concatenate([filt_w1[l], jnp.zeros((FORD - FEMB, FORD), F32)], axis=0)
    twice = lambda w: jnp.kron(jnp.eye(2, dtype=F32), w)
    pair = lambda v: jnp.tile(v[None, :], (1, 2))
    kern, abs_sum = _implicit_filter(ztab, ttab, mtab, ftab, twice(w1p), pair(filt_b1[l]), twice(filt_w2[l]),
                                     pair(filt_b2[l]), twice(filt_w3[l]), pair(filt_freq[l]), negdelta)
    hy = _hyena_long_conv(u_rj, x2_rj, kern, abs_sum, hyena_bias[l])

    wr2 = jnp.concatenate(_split(w_router[l].T), axis=0)
    x1, h2, logits = _out_projection(att, hy, x, w_out[l].astype(BF16), lat(g1), norm2_g[l][None, :],
                                     lat(sh2), lat(sc2), wr2)

    tri = jnp.asarray(np.triu(np.ones((TCH, TCH))), BF16)
    pos, gate, offs = _routing(logits, tri)
    offs_flat = offs.reshape(-1)
    xg = _gather(offs_flat, h2, pos.reshape(B, NE, NTCH, TCH))
    y = _expert_ffn(xg, jnp.swapaxes(w_gate[l], 1, 2), jnp.swapaxes(w_up[l], 1, 2), w_down[l])
    return _combine(offs_flat, y, pos, gate, x1, lat(g2))
```

```python
import functools
import math

import numpy as np
import jax
import jax.numpy as jnp
from jax import lax
from jax.experimental import pallas as pl
from jax.experimental.pallas import tpu as pltpu

F32 = jnp.float32
BF16 = jnp.bfloat16
I32 = jnp.int32

D = 1024
SEQ = 8192
CTX = 256
GRID_W = 64
ATT_W = 512
HY_W = 512
HD = 64
NQ = 8
NKV = 2
QPK = NQ // NKV
KV_W = NKV * HD
IN_W = ATT_W + 2 * KV_W + 3 * HY_W
FEMB = 33
FORD = 64
NE = 16
CAP = 2 * SEQ // NE
DEXP = 2752
ROPE_THETA = 10000.0
EPS = 1e-6
DECAY_TARGET = 1e-2
FAST_DECAY_PCT = 0.3
SLOW_DECAY_PCT = 1.5

LANES = 128
SUBLANES = 8
BF16_ROWS = 16
VMEM_BYTES_V7X = 64 * 1024 * 1024
VMEM_LIMIT = VMEM_BYTES_V7X - 8 * 1024 * 1024

FN = 2 * SEQ
FN1 = 128
FN2 = 128

TCH = LANES
NTCH = SEQ // TCH
GW = TCH + SUBLANES
CW = TCH + BF16_ROWS
YROWS = CAP + BF16_ROWS


def _cparams(sem, vmem=None):
    return pltpu.CompilerParams(dimension_semantics=sem, vmem_limit_bytes=vmem or VMEM_LIMIT)


def _split(a):
    hi = a.astype(BF16)
    lo = (a - hi.astype(F32)).astype(BF16)
    return hi, lo


_NN = (((1,), (0,)), ((), ()))
_NT = (((1,), (1,)), ((), ()))
_TN = (((0,), (0,)), ((), ()))


def _dot(a, b, dn=_NN):
    return lax.dot_general(a, b, dn, preferred_element_type=F32)


def _dot3(a, b, dn=_NN):
    ah, al = _split(a)
    bh, bl = _split(b)
    return _dot(ah, bh, dn) + _dot(ah, bl, dn) + _dot(al, bh, dn)


def _mod_body(c_ref, w_ref, b_ref, o_ref):
    c = c_ref[...]
    s = c * (1.0 / (1.0 + jnp.exp(-c)))
    o_ref[...] = _dot3(s, w_ref[...]) + b_ref[...]


def _modulation(cc, w_mod, b_mod):
    n = w_mod.shape[1]
    return pl.pallas_call(
        _mod_body,
        grid=(n // D,),
        in_specs=[pl.BlockSpec((SUBLANES, D), lambda j: (0, 0)),
                  pl.BlockSpec((D, D), lambda j: (0, j)),
                  pl.BlockSpec((1, D), lambda j: (0, j))],
        out_specs=pl.BlockSpec((SUBLANES, D), lambda j: (0, j)),
        out_shape=jax.ShapeDtypeStruct((SUBLANES, n), F32),
        compiler_params=_cparams(("arbitrary",)),
        name="modulation",
    )(cc, w_mod, b_mod)


Q_SCALE = HD ** -0.5 * math.log2(math.e)


def _rms_mod(x, g, sh, sc):
    ms = jnp.mean(x * x, axis=-1, keepdims=True)
    return (x * lax.rsqrt(ms + EPS) * g) * (1.0 + sc) + sh


def _head_mean_square(t, bd):
    hi, lo = _split(t * t)
    return _dot(hi, bd) + _dot(lo, bd)


def _head_norm_rope(t, ms, g, cos, sin):
    tn = t * lax.rsqrt(ms + EPS) * g
    lane = lax.broadcasted_iota(I32, tn.shape, 1)
    sw = jnp.where((lane & 31) < 16, pltpu.roll(tn, LANES - 16, 1), pltpu.roll(tn, 16, 1))
    return tn * cos + sw * sin


def _proj_body(x_ref, g_ref, sh_ref, sc_ref, w_ref, gq_ref, gk_ref, bd_ref, cos_ref, sin_ref,
               q_ref, k_ref, v_ref, p_ref):
    h = _rms_mod(x_ref[0], g_ref[...], sh_ref[0], sc_ref[0])
    proj = _dot(h.astype(BF16), w_ref[...])
    bd = bd_ref[...]
    cos = cos_ref[...]
    sin = sin_ref[...]
    wide = 2 * LANES
    for j in range(ATT_W // wide):
        ms = _head_mean_square(proj[:, j * wide:(j + 1) * wide], bd)
        for i in range(2):
            sl = slice(j * wide + i * LANES, j * wide + (i + 1) * LANES)
            qj = _head_norm_rope(proj[:, sl], ms[:, i * LANES:(i + 1) * LANES], gq_ref[...], cos, sin)
            q_ref[0, :, sl] = (qj * Q_SCALE).astype(BF16)
    ms = _head_mean_square(proj[:, ATT_W:ATT_W + 2 * KV_W], bd)
    kk = _head_norm_rope(proj[:, ATT_W:ATT_W + KV_W], ms[:, 0:KV_W], gk_ref[...], cos, sin)
    vt = proj[:, ATT_W + KV_W:ATT_W + 2 * KV_W].T
    for g in range(NKV):
        k_ref[0, g] = kk[:, g * HD:(g + 1) * HD].astype(BF16)
        v_ref[0, g] = vt[g * HD:(g + 1) * HD, :].astype(BF16)
    p_ref[0] = proj[:, ATT_W + 2 * KV_W:]


def _in_projection(x, g1, sh, sc, w_in_bf, gq2, gk2, bd, cos_t, sin_t, tm):
    b, s, _ = x.shape
    row = lambda bi, i: (bi, 0, 0)
    tok = lambda bi, i: (bi, i, 0)
    const = lambda bi, i: (0, 0)
    return pl.pallas_call(
        _proj_body,
        grid=(b, s // tm),
        in_specs=[pl.BlockSpec((1, tm, D), tok),
                  pl.BlockSpec((1, D), const),
                  pl.BlockSpec((1, 1, D), row),
                  pl.BlockSpec((1, 1, D), row),
                  pl.BlockSpec((D, IN_W), const),
                  pl.BlockSpec((1, LANES), const),
                  pl.BlockSpec((1, LANES), const),
                  pl.BlockSpec((2 * LANES, 2 * LANES), const),
                  pl.BlockSpec((tm, LANES), lambda bi, i: (i, 0)),
                  pl.BlockSpec((tm, LANES), lambda bi, i: (i, 0))],
        out_specs=[pl.BlockSpec((1, tm, ATT_W), tok),
                   pl.BlockSpec((1, NKV, tm, HD), lambda bi, i: (bi, 0, i, 0)),
                   pl.BlockSpec((1, NKV, HD, tm), lambda bi, i: (bi, 0, 0, i)),
                   pl.BlockSpec((1, tm, 3 * HY_W), tok)],
        out_shape=[jax.ShapeDtypeStruct((b, s, ATT_W), BF16),
                   jax.ShapeDtypeStruct((b, NKV, s, HD), BF16),
                   jax.ShapeDtypeStruct((b, NKV, HD, s), BF16),
                   jax.ShapeDtypeStruct((b, s, 3 * HY_W), F32)],
        compiler_params=_cparams(("parallel", "parallel")),
        name="in_projection",
    )(x, g1, sh, sc, w_in_bf, gq2, gk2, bd, cos_t, sin_t)


ATT_TQ = 512
ATT_TK = 768
SK = SEQ + CTX
ATT_NCH = SK // ATT_TK


ATT_NQ = QPK * ATT_TQ
ATT_VR = HD + BF16_ROWS
assert ATT_NCH % 2 == 1
ATT_SHIFT_MAX = 120.0


def _attn_body(bound_ref, q_ref, k_ref, vt_ref, o_ref, s_ref, mx_ref, m_ref, acc_ref):
    qall = jnp.concatenate([q_ref[0, :, r * HD:(r + 1) * HD] for r in range(QPK)], axis=0)
    acc_ref[...] = jnp.zeros_like(acc_ref)
    bound = bound_ref[0]
    fixed_shift = 2.0 * bound <= ATT_SHIFT_MAX

    def finish():
        out = acc_ref[0:HD, :] * (1.0 / acc_ref[HD:HD + 1, :])
        for r in range(QPK):
            o_ref[0, :, r * HD:(r + 1) * HD] = out[:, r * ATT_TQ:(r + 1) * ATT_TQ].T.astype(BF16)

    @pl.when(fixed_shift)
    def _():
        def chunk(c, _):
            s = _dot(k_ref[0, 0, c], qall, _NT)
            acc_ref[...] += _dot(vt_ref[0, 0, c], jnp.exp2(s - bound).astype(BF16))
            return 0

        lax.fori_loop(0, ATT_NCH, chunk, 0, unroll=True)
        finish()

    @pl.when(jnp.logical_not(fixed_shift))
    def _():
        m_ref[...] = jnp.full(m_ref.shape, -1e30, F32)

        def scores(c, slot):
            s = _dot(k_ref[0, 0, c], qall, _NT)
            s_ref[slot] = s
            mx_ref[slot] = jnp.max(s, axis=0, keepdims=True)

        def update(c, slot):
            m_old = m_ref[...]
            m_new = jnp.maximum(m_old, mx_ref[slot])
            p = jnp.exp2(s_ref[slot] - m_new).astype(BF16)
            acc_ref[...] = jnp.exp2(m_old - m_new) * acc_ref[...] + _dot(vt_ref[0, 0, c], p)
            m_ref[...] = m_new

        scores(0, 0)

        def pair(i, _):
            c = 2 * i
            scores(c + 1, 1)
            update(c, 0)
            scores(c + 2, 0)
            update(c + 1, 1)
            return 0

        lax.fori_loop(0, ATT_NCH // 2, pair, 0)
        update(ATT_NCH - 1, 0)
        finish()


def _attention(bound, q, kch, vtch):
    b = q.shape[0]
    grid_spec = pltpu.PrefetchScalarGridSpec(
        num_scalar_prefetch=1,
        grid=(b, NKV, SEQ // ATT_TQ),
        in_specs=[pl.BlockSpec((1, ATT_TQ, QPK * HD), lambda bi, g, i, bd: (bi, i, g)),
                  pl.BlockSpec((1, 1, ATT_NCH, ATT_TK, HD), lambda bi, g, i, bd: (bi, g, 0, 0, 0)),
                  pl.BlockSpec((1, 1, ATT_NCH, ATT_VR, ATT_TK), lambda bi, g, i, bd: (bi, g, 0, 0, 0))],
        out_specs=pl.BlockSpec((1, ATT_TQ, QPK * HD), lambda bi, g, i, bd: (bi, i, g)),
        scratch_shapes=[pltpu.VMEM((2, ATT_TK, ATT_NQ), F32), pltpu.VMEM((2, 1, ATT_NQ), F32),
                        pltpu.VMEM((1, ATT_NQ), F32), pltpu.VMEM((ATT_VR, ATT_NQ), F32)],
    )
    return pl.pallas_call(
        _attn_body,
        grid_spec=grid_spec,
        out_shape=jax.ShapeDtypeStruct((b, SEQ, ATT_W), BF16),
        compiler_params=_cparams(("parallel", "parallel", "parallel")),
        name="attention",
    )(bound, q, kch, vtch)


SC_TM = 2048
SC_J = SC_TM // FN2


def _sconv_body(m1, a1, n1, m2, a2, n2, m3, a3, n3, w_ref, b_ref, u_ref, x2_ref):
    i = pl.program_id(1)
    last = pl.num_programs(1) - 1
    rows = lax.broadcasted_iota(I32, (SC_TM, HY_W), 0)

    def conv(main, prev, nxt, g):
        x = main[0]
        pr = jnp.where(i > 0, prev[0, SUBLANES - 1:SUBLANES, :], 0.0)
        nx = jnp.where(i < last, nxt[0, 0:1, :], 0.0)
        xm = jnp.where(rows == 0, pr, pltpu.roll(x, 1, 0))
        xp = jnp.where(rows == SC_TM - 1, nx, pltpu.roll(x, SC_TM - 1, 0))
        return (w_ref[g:g + 1, :] * xm + w_ref[3 + g:4 + g, :] * x + w_ref[6 + g:7 + g, :] * xp
                + b_ref[g:g + 1, :])

    def to_rj(t):
        return jnp.swapaxes(t.reshape(SC_J, FN2, HY_W), 0, 1).astype(BF16)

    x1 = conv(m1, a1, n1, 0)
    x2 = conv(m2, a2, n2, 1)
    v = conv(m3, a3, n3, 2)
    u_ref[0] = to_rj(v * x1)
    x2_ref[0] = to_rj(x2)


def _short_conv(p, cw9, cb3):
    b = p.shape[0]
    nblk8 = SEQ // SUBLANES
    step8 = SC_TM // SUBLANES
    specs = []
    for g in range(3):
        specs += [pl.BlockSpec((1, SC_TM, HY_W), lambda bi, i, g=g: (bi, i, g)),
                  pl.BlockSpec((1, SUBLANES, HY_W), lambda bi, i, g=g: (bi, jnp.maximum(i * step8 - 1, 0), g)),
                  pl.BlockSpec((1, SUBLANES, HY_W), lambda bi, i, g=g: (bi, jnp.minimum((i + 1) * step8, nblk8 - 1), g))]
    specs += [pl.BlockSpec((9, HY_W), lambda bi, i: (0, 0)), pl.BlockSpec((3, HY_W), lambda bi, i: (0, 0))]
    out = pl.BlockSpec((1, FN2, SC_J, HY_W), lambda bi, i: (bi, 0, i, 0))
    return pl.pallas_call(
        _sconv_body,
        grid=(b, SEQ // SC_TM),
        in_specs=specs,
        out_specs=[out, out],
        out_shape=[jax.ShapeDtypeStruct((b, FN2, SEQ // FN2, HY_W), BF16)] * 2,
        compiler_params=_cparams(("parallel", "parallel")),
        name="short_conv",
    )(p, p, p, p, p, p, p, p, p, cw9, cb3)


FILT_TR = 1024


def _filter_body(z_ref, t_ref, msk_ref, fwd_ref, w1_ref, b1_ref, w2_ref, b2_ref, w3_ref, fr_ref, dl_ref,
                 k_ref, s_ref):
    fr = fr_ref[...]
    h = jnp.sin(fr * (_dot3(z_ref[...], w1_ref[...]) + b1_ref[...]))
    h = jnp.sin(fr * (_dot3(h, w2_ref[...]) + b2_ref[...]))
    h = _dot3(h, w3_ref[...])
    h = jnp.concatenate([h[:, :2 * HY_W], h[:, 2 * HY_W:]], axis=0)
    h = jnp.where(fwd_ref[...] > 0.5, h[:, :HY_W], h[:, HY_W:])
    kern = h * jnp.exp(t_ref[...] * dl_ref[...]) * msk_ref[...]
    k_ref[...] = kern

    @pl.when(pl.program_id(0) == 0)
    def _():
        s_ref[...] = jnp.zeros_like(s_ref)

    s_ref[...] += jnp.sum(jnp.abs(kern), axis=0, keepdims=True)


def _implicit_filter(ztab, ttab, mtab, ftab, w1p, b1, w2, b2, w3, freq, negdelta):
    rowblk = lambda i: (i, 0)
    const = lambda i: (0, 0)
    col = pl.BlockSpec((FILT_TR, 1), rowblk)
    return pl.pallas_call(
        _filter_body,
        grid=(FN // FILT_TR,),
        in_specs=[pl.BlockSpec((FILT_TR // 2, 2 * FORD), rowblk), col, col, col,
                  pl.BlockSpec((2 * FORD, 2 * FORD), const),
                  pl.BlockSpec((1, 2 * FORD), const),
                  pl.BlockSpec((2 * FORD, 2 * FORD), const),
                  pl.BlockSpec((1, 2 * FORD), const),
                  pl.BlockSpec((2 * FORD, 4 * HY_W), const),
                  pl.BlockSpec((1, 2 * FORD), const),
                  pl.BlockSpec((1, HY_W), const)],
        out_specs=[pl.BlockSpec((FILT_TR, HY_W), rowblk),
                   pl.BlockSpec((1, HY_W), const)],
        out_shape=[jax.ShapeDtypeStruct((FN, HY_W), F32), jax.ShapeDtypeStruct((1, HY_W), F32)],
        compiler_params=_cparams(("arbitrary",)),
        name="implicit_filter",
    )(ztab, ttab, mtab, ftab, w1p, b1, w2, b2, w3, freq, negdelta)


DFT_G = 4
DFT_KB = 16
DFT_NP = FN1 // DFT_KB
DFT_UNROLL = 4


def _dot2c(fh, fl, zb):
    return _dot(fh, zb) + _dot(fl, zb)


def _lead_stage(src, fh, fl, dst_ref):
    def group(rg, _):
        r0 = rg * DFT_G
        rhs = jnp.concatenate([src(r0 + g) for g in range(DFT_G)], axis=1)
        blk = _dot2c(fh, fl, rhs)
        for g in range(DFT_G):
            dst_ref[r0 + g] = blk[:, g * LANES:(g + 1) * LANES]
        return 0

    lax.fori_loop(0, FN2 // DFT_G, group, 0, unroll=DFT_UNROLL)


def _lead_phase(src, lh_ref, ll_ref, p_ref, q_ref):
    for h in range(2):
        rows = slice(h * FN1, (h + 1) * FN1)
        _lead_stage(src, lh_ref[rows, :], ll_ref[rows, :], p_ref)
        q_ref[rows] = jnp.swapaxes(p_ref[...], 0, 1)


def _lane_cat(xs):
    return jnp.concatenate(xs, axis=1)


def _mid_forward(q_ref, k0, tr_ref, ti_ref, fh, fl, half):
    tr_t, ti_t = tr_ref[...].T, ti_ref[...].T
    brs, bis, trs, tis = [], [], [], []
    for g in range(DFT_G):
        jj = half * DFT_G + g
        ar, ai = q_ref[k0 + jj], q_ref[FN1 + k0 + jj]
        tr = jnp.broadcast_to(tr_t[:, jj:jj + 1], (FN2, LANES))
        ti = jnp.broadcast_to(ti_t[:, jj:jj + 1], (FN2, LANES))
        brs.append(ar * tr - ai * ti)
        bis.append(ar * ti + ai * tr)
        trs.append(tr)
        tis.append(ti)
    b = jnp.concatenate([_lane_cat(brs), _lane_cat(bis)], axis=0).astype(BF16)
    return _dot2c(fh, fl, b), _lane_cat(trs), _lane_cat(tis)


def _spectrum_body(k_ref, tr_ref, ti_ref, lh_ref, ll_ref, fh_ref, fl_ref, sc_ref, o_ref, p_ref, q_ref):
    ph = pl.program_id(1)

    @pl.when(ph == 0)
    def _():
        _lead_phase(lambda r: k_ref[r].astype(BF16), lh_ref, ll_ref, p_ref, q_ref)

    @pl.when(ph > 0)
    def _():
        k0 = (ph - 1) * DFT_KB
        sc = _lane_cat([sc_ref[...]] * DFT_G)
        for half in range(DFT_KB // DFT_G):
            x, _, _ = _mid_forward(q_ref, k0, tr_ref, ti_ref, fh_ref[...], fl_ref[...], half)
            x = x * sc
            for g in range(DFT_G):
                lanes = slice(g * LANES, (g + 1) * LANES)
                o_ref[0, half * DFT_G + g] = x[:FN2, lanes]
                o_ref[1, half * DFT_G + g] = x[FN2:, lanes]


def _mid_index(ph):
    return jnp.clip(ph - 1, 0, DFT_NP - 1)


def _filter_spectrum(kern_rj, twr, twi, lh, ll, fh, fl, scale):
    const = lambda c, ph: (0, 0)
    tw = pl.BlockSpec((DFT_KB, FN2), lambda c, ph: (_mid_index(ph), 0))
    return pl.pallas_call(
        _spectrum_body,
        grid=(HY_W // LANES, DFT_NP + 1),
        in_specs=[pl.BlockSpec((FN2, FN1, LANES), lambda c, ph: (0, 0, c)),
                  tw, tw,
                  pl.BlockSpec(lh.shape, const), pl.BlockSpec(ll.shape, const),
                  pl.BlockSpec(fh.shape, const), pl.BlockSpec(fl.shape, const),
                  pl.BlockSpec((1, LANES), lambda c, ph: (0, c))],
        out_specs=pl.BlockSpec((2, DFT_KB, FN2, LANES), lambda c, ph: (0, _mid_index(ph), 0, c)),
        out_shape=jax.ShapeDtypeStruct((2, FN1, FN2, HY_W), F32),
        scratch_shapes=[pltpu.VMEM((FN2, FN1, LANES), F32), pltpu.VMEM((2 * FN1, FN2, LANES), F32)],
        compiler_params=_cparams(("parallel", "arbitrary")),
        name="filter_spectrum",
    )(kern_rj, twr, twi, lh, ll, fh, fl, scale)


def _hconv_body(u_ref, x2_ref, kh_ref, tr_ref, ti_ref, lh_ref, ll_ref, fh_ref, fl_ref, gh_ref, gl_ref,
                oh_ref, ol_ref, bias_ref, o_ref, p_ref, q_ref):
    ph = pl.program_id(1)

    def both(ref, r):
        return jnp.concatenate([ref[0, r], ref[1, r]], axis=0)

    @pl.when(ph == 0)
    def _():
        _lead_phase(lambda r: both(u_ref, r), lh_ref, ll_ref, p_ref, q_ref)

    @pl.when((ph > 0) & (ph <= DFT_NP))
    def _():
        k0 = (ph - 1) * DFT_KB
        for half in range(DFT_KB // DFT_G):
            x, tr, ti = _mid_forward(q_ref, k0, tr_ref, ti_ref, fh_ref[...], fl_ref[...], half)
            xr, xi = x[:FN2], x[FN2:]
            kr = _lane_cat([kh_ref[0, half * DFT_G + g] for g in range(DFT_G)])
            ki = _lane_cat([kh_ref[1, half * DFT_G + g] for g in range(DFT_G)])
            y = jnp.concatenate([xr * kr - xi * ki, xr * ki + xi * kr], axis=0).astype(BF16)
            c = _dot2c(gh_ref[...], gl_ref[...], y)
            cr, ci = c[:FN2], c[FN2:]
            dr = cr * tr + ci * ti
            di = ci * tr - cr * ti
            for g in range(DFT_G):
                lanes = slice(g * LANES, (g + 1) * LANES)
                q_ref[k0 + half * DFT_G + g] = dr[:, lanes]
                q_ref[FN1 + k0 + half * DFT_G + g] = di[:, lanes]

    @pl.when(ph == DFT_NP + 1)
    def _():
        bias = bias_ref[...]
        p_ref[...] = jnp.swapaxes(q_ref[0:FN1], 0, 1)
        _lead_stage(lambda r: p_ref[r].astype(BF16), oh_ref[:, 0:FN1], ol_ref[:, 0:FN1], q_ref)
        p_ref[...] = jnp.swapaxes(q_ref[FN1:2 * FN1], 0, 1)
        oh2, ol2 = oh_ref[:, FN1:2 * FN1], ol_ref[:, FN1:2 * FN1]

        def group(rg, _):
            r0 = rg * DFT_G
            rhs = _lane_cat([p_ref[r0 + g].astype(BF16) for g in range(DFT_G)])
            blk = _dot2c(oh2, ol2, rhs)
            for g in range(DFT_G):
                r = r0 + g
                y = q_ref[r] + blk[:, g * LANES:(g + 1) * LANES]
                q_ref[r] = (y + both(u_ref, r).astype(F32) * bias) * both(x2_ref, r).astype(F32)
            return 0

        lax.fori_loop(0, FN2 // DFT_G, group, 0, unroll=DFT_UNROLL)
        p_ref[...] = jnp.swapaxes(q_ref[0:FN2], 0, 1)
        nj = SEQ // FN2
        for b in range(2):
            o_ref[b] = p_ref[b * nj:(b + 1) * nj].reshape(SEQ, LANES).astype(BF16)


def _hyena_conv(u_rj, x2_rj, khat, twr, twi, lead, fwd, inv, out, bias):
    const = lambda c, ph: (0, 0)
    nj = SEQ // FN2
    tw = pl.BlockSpec((DFT_KB, FN2), lambda c, ph: (_mid_index(ph), 0))
    sig = pl.BlockSpec((2, FN2, nj, LANES), lambda c, ph: (0, 0, 0, c))
    mats = [m for pair in (lead, fwd, inv, out) for m in pair]
    return pl.pallas_call(
        _hconv_body,
        grid=(HY_W // LANES, DFT_NP + 2),
        in_specs=[sig, sig,
                  pl.BlockSpec((2, DFT_KB, FN2, LANES), lambda c, ph: (0, _mid_index(ph), 0, c)),
                  tw, tw] + [pl.BlockSpec(m.shape, const) for m in mats]
                 + [pl.BlockSpec((1, LANES), lambda c, ph: (0, c))],
        out_specs=pl.BlockSpec((2, SEQ, LANES), lambda c, ph: (0, 0, c)),
        out_shape=jax.ShapeDtypeStruct((2, SEQ, HY_W), BF16),
        scratch_shapes=[pltpu.VMEM((FN2, FN1, LANES), F32), pltpu.VMEM((2 * FN1, FN2, LANES), F32)],
        compiler_params=_cparams(("parallel", "arbitrary")),
        name="hyena_conv",
    )(u_rj, x2_rj, khat, twr, twi, *mats, bias)


OP_TM = 512


def _outproj_body(att_ref, hy_ref, x_ref, w_ref, g1_ref, n2_ref, sh_ref, sc_ref, wr_ref,
                  x1_ref, h2_ref, lg_ref):
    a = jnp.concatenate([att_ref[0], hy_ref[0]], axis=1)
    x1 = x_ref[0] + g1_ref[0] * _dot(a, w_ref[...])
    x1_ref[0] = x1
    h2 = _rms_mod(x1, n2_ref[...], sh_ref[0], sc_ref[0])
    hh, hl = _split(h2)
    h2_ref[0] = hh
    wr = wr_ref[...]
    both = _dot(wr, hh, _NT)
    lg_ref[0] = both[0:NE] + both[NE:2 * NE] + _dot(wr[0:NE], hl, _NT)


def _out_projection(att, hy, x, w_out_bf, g1r, n2g, sh2, sc2, wr2):
    b = x.shape[0]
    tok = lambda bi, i: (bi, i, 0)
    row = lambda bi, i: (bi, 0, 0)
    const = lambda bi, i: (0, 0)
    return pl.pallas_call(
        _outproj_body,
        grid=(b, SEQ // OP_TM),
        in_specs=[pl.BlockSpec((1, OP_TM, ATT_W), tok),
                  pl.BlockSpec((1, OP_TM, HY_W), tok),
                  pl.BlockSpec((1, OP_TM, D), tok),
                  pl.BlockSpec((ATT_W + HY_W, D), const),
                  pl.BlockSpec((1, 1, D), row),
                  pl.BlockSpec((1, D), const),
                  pl.BlockSpec((1, 1, D), row),
                  pl.BlockSpec((1, 1, D), row),
                  pl.BlockSpec((2 * NE, D), const)],
        out_specs=[pl.BlockSpec((1, OP_TM, D), tok),
                   pl.BlockSpec((1, OP_TM, D), tok),
                   pl.BlockSpec((1, NE, OP_TM), lambda bi, i: (bi, 0, i))],
        out_shape=[jax.ShapeDtypeStruct((b, SEQ, D), F32),
                   jax.ShapeDtypeStruct((b, SEQ, D), BF16),
                   jax.ShapeDtypeStruct((b, NE, SEQ), F32)],
        compiler_params=_cparams(("parallel", "parallel")),
        name="out_projection",
    )(att, hy, x, w_out_bf, g1r, n2g, sh2, sc2, wr2)


def _routing_body(lg_ref, tri_ref, pos_ref, gate_ref, off_ref, cs_ref):
    lg = lg_ref[0]
    e = jnp.exp(lg - jnp.max(lg, axis=0, keepdims=True))
    aff = e / jnp.sum(e, axis=0, keepdims=True)
    gate_ref[0] = aff
    def count_ge(t):
        return jnp.sum(jnp.where(aff >= t, 1.0, 0.0), axis=1, keepdims=True)

    def bisect(i, thr):
        cand = thr | (jnp.int32(1) << (30 - i))
        return jnp.where(count_ge(pltpu.bitcast(cand, F32)) >= float(CAP), cand, thr)

    thr = lax.fori_loop(0, 31, bisect, jnp.zeros((NE, 1), I32))
    lo = pltpu.bitcast(thr, F32)
    hi = jnp.maximum(pltpu.bitcast(thr + 1, F32), jnp.finfo(F32).tiny)

    def refine(i, c):
        lo, hi = c
        mid = lo + (hi - lo) * 0.5
        ok = count_ge(mid) >= float(CAP)
        return jnp.where(ok, mid, lo), jnp.where(ok, hi, mid)

    lo, hi = lax.fori_loop(0, 32, refine, (lo, hi))
    gt = aff >= hi
    eq = (aff >= lo) & jnp.logical_not(gt)
    need = float(CAP) - jnp.sum(jnp.where(gt, 1.0, 0.0), axis=1, keepdims=True)
    tri = tri_ref[...]

    def excl_cumsum(mask_f, record_offsets):
        carry = jnp.zeros((NE, 1), F32)
        for c in range(NTCH):
            sl = slice(c * TCH, (c + 1) * TCH)
            m = mask_f[:, sl]
            inc = _dot(m.astype(BF16), tri)
            cs_ref[:, sl] = inc - m + carry
            if record_offsets:
                off_ref[0, :, c:c + 1] = carry.astype(I32)
            carry = carry + inc[:, TCH - 1:TCH]
        return cs_ref[...]

    eq_rank = excl_cumsum(jnp.where(eq, 1.0, 0.0), False)
    sel = gt | (eq & (eq_rank < need))
    pos = excl_cumsum(jnp.where(sel, 1.0, 0.0), True)
    pos_ref[0] = jnp.where(sel, pos.astype(I32), -1)


def _routing(logits, tri):
    b = logits.shape[0]
    blk = pl.BlockSpec((1, NE, SEQ), lambda bi: (bi, 0, 0))
    return pl.pallas_call(
        _routing_body,
        grid=(b,),
        in_specs=[blk, pl.BlockSpec((TCH, TCH), lambda bi: (0, 0))],
        out_specs=[blk, blk, pl.BlockSpec((1, NE, NTCH), lambda bi: (bi, 0, 0))],
        out_shape=[jax.ShapeDtypeStruct((b, NE, SEQ), I32),
                   jax.ShapeDtypeStruct((b, NE, SEQ), F32),
                   jax.ShapeDtypeStruct((b, NE, NTCH), I32)],
        scratch_shapes=[pltpu.VMEM((NE, SEQ), F32)],
        compiler_params=_cparams(("parallel",)),
        name="routing",
    )(logits, tri)


GATHER_UNROLL = 8


GW_SMALL = 64


def _gather_body(off_ref, h_ref, pos_ref, xg_ref, acc_ref):
    b = pl.program_id(0)
    e = pl.program_id(1)
    row0 = (b * NE + e) * NTCH
    acc_ref[...] = jnp.zeros_like(acc_ref)

    def count(c2, most):
        nxt = jnp.where(c2 + 1 < NTCH // 2, off_ref[row0 + jnp.minimum(2 * c2 + 2, NTCH - 1)], CAP)
        return jnp.maximum(most, nxt - off_ref[row0 + 2 * c2])

    most = lax.fori_loop(0, NTCH // 2, count, 0)

    def sweep(window, span):
        crow = lax.broadcasted_iota(I32, (window, span * TCH), 0)

        def chunks(i, _):
            for j in range(GATHER_UNROLL):
                c = (i * GATHER_UNROLL + j) * span
                off = off_ref[row0 + c]
                base = pl.multiple_of(jnp.minimum((off >> 3) << 3, CAP + SUBLANES - window), SUBLANES)
                t0 = pl.multiple_of(c * TCH, TCH)
                rel = _lane_cat([pos_ref[0, 0, pl.ds(c + s, 1), :] for s in range(span)]) - base
                onehot = jnp.where(crow == rel, 1.0, 0.0).astype(BF16)
                acc_ref[pl.ds(base, window), :] += _dot(onehot, h_ref[0, pl.ds(t0, span * TCH), :])
            return 0

        lax.fori_loop(0, NTCH // (GATHER_UNROLL * span), chunks, 0)

    @pl.when(most <= GW_SMALL - SUBLANES)
    def _():
        sweep(GW_SMALL, 2)

    @pl.when(most > GW_SMALL - SUBLANES)
    def _():
        sweep(GW, 1)

    xg_ref[0, 0] = acc_ref[0:CAP, :].astype(BF16)


def _gather(offs_flat, h2, pos4):
    b = h2.shape[0]
    grid_spec = pltpu.PrefetchScalarGridSpec(
        num_scalar_prefetch=1,
        grid=(b, NE),
        in_specs=[pl.BlockSpec((1, SEQ, D), lambda bi, e, off: (bi, 0, 0)),
                  pl.BlockSpec((1, 1, NTCH, TCH), lambda bi, e, off: (bi, e, 0, 0))],
        out_specs=pl.BlockSpec((1, 1, CAP, D), lambda bi, e, off: (bi, e, 0, 0)),
        scratch_shapes=[pltpu.VMEM((CAP + SUBLANES, D), F32)],
    )
    return pl.pallas_call(
        _gather_body,
        grid_spec=grid_spec,
        out_shape=jax.ShapeDtypeStruct((b, NE, CAP, D), BF16),
        compiler_params=_cparams(("parallel", "arbitrary")),
        name="moe_gather",
    )(offs_flat, h2, pos4)


FFN_TM = 512
FFN_NF = 4
FFN_FC = DEXP // FFN_NF
assert FFN_FC * FFN_NF == DEXP and FFN_FC % BF16_ROWS == 0


def _ffn_body(xg_ref, wgt_ref, wut_ref, wd_ref, y_ref, acc_ref):
    j = pl.program_id(1)
    nb = xg_ref.shape[0]

    @pl.when(j == 0)
    def _():
        acc_ref[...] = jnp.zeros_like(acc_ref)

    wgt = wgt_ref[0].astype(BF16)
    wut = wut_ref[0].astype(BF16)
    wd = wd_ref[0].astype(BF16)
    for b in range(nb):
        for mb in range(CAP // FFN_TM):
            rows = slice(mb * FFN_TM, (mb + 1) * FFN_TM)
            xb = xg_ref[b, 0, rows, :]
            a = _dot(xb, wgt, _NT)
            u = _dot(xb, wut, _NT)
            h = (a * (1.0 / (1.0 + jnp.exp(-a))) * u).astype(BF16)
            acc_ref[b, rows, :] += _dot(h, wd)

    @pl.when(j == FFN_NF - 1)
    def _():
        for b in range(nb):
            y_ref[b, 0, 0:CAP, :] = acc_ref[b].astype(BF16)
            y_ref[b, 0, CAP:YROWS, :] = jnp.zeros((YROWS - CAP, D), BF16)


def _expert_ffn(xg, w_gate_t, w_up_t, w_down):
    b = xg.shape[0]
    wblk = pl.BlockSpec((1, FFN_FC, D), lambda e, j: (e, j, 0))
    return pl.pallas_call(
        _ffn_body,
        grid=(NE, FFN_NF),
        in_specs=[pl.BlockSpec((b, 1, CAP, D), lambda e, j: (0, e, 0, 0)), wblk, wblk, wblk],
        out_specs=pl.BlockSpec((b, 1, YROWS, D), lambda e, j: (0, e, 0, 0)),
        out_shape=jax.ShapeDtypeStruct((b, NE, YROWS, D), BF16),
        scratch_shapes=[pltpu.VMEM((b, CAP, D), F32)],
        compiler_params=_cparams(("parallel", "arbitrary")),
        name="expert_ffn",
    )(xg, w_gate_t, w_up_t, w_down)


CW_SMALL = 64
CW_STACK = 2 * LANES // CW_SMALL


def _combine_body(off_ref, y_ref, pos_ref, gate_ref, x1_ref, g2_ref, o_ref):
    b = pl.program_id(0)
    i = pl.program_id(1)

    def offset(e, c):
        return off_ref[(b * NE + e) * NTCH + c]

    def window(e, rows):
        base = pl.multiple_of(jnp.minimum((offset(e, i) >> 4) << 4, YROWS - rows), BF16_ROWS)
        rel = pos_ref[0, e:e + 1, :] - base
        crow = lax.broadcasted_iota(I32, (rows, TCH), 0)
        w = jnp.where(crow == rel, gate_ref[0, e:e + 1, :], 0.0).astype(BF16)
        return w, y_ref[0, e, pl.ds(base, rows), :]

    most = jnp.int32(0)
    for e in range(NE):
        nxt = jnp.where(i + 1 < NTCH, offset(e, jnp.minimum(i + 1, NTCH - 1)), CAP)
        most = jnp.maximum(most, nxt - offset(e, i))

    @pl.when(most <= CW_SMALL - BF16_ROWS)
    def _():
        acc = jnp.zeros((TCH, D), F32)
        for e0 in range(0, NE, CW_STACK):
            ws, ys = zip(*[window(e, CW_SMALL) for e in range(e0, e0 + CW_STACK)])
            acc = acc + _dot(jnp.concatenate(ws, axis=0), jnp.concatenate(ys, axis=0), _TN)
        o_ref[0] = x1_ref[0] + g2_ref[0] * acc

    @pl.when(most > CW_SMALL - BF16_ROWS)
    def _():
        acc = jnp.zeros((TCH, D), F32)
        for e in range(NE):
            w, yw = window(e, CW)
            acc = acc + _dot(w, yw, _TN)
        o_ref[0] = x1_ref[0] + g2_ref[0] * acc


def _combine(offs_flat, y, pos, gate, x1, g2r):
    b = x1.shape[0]
    grid_spec = pltpu.PrefetchScalarGridSpec(
        num_scalar_prefetch=1,
        grid=(b, NTCH),
        in_specs=[pl.BlockSpec((1, NE, YROWS, D), lambda bi, i, off: (bi, 0, 0, 0),
                               pipeline_mode=pl.Buffered(1)),
                  pl.BlockSpec((1, NE, TCH), lambda bi, i, off: (bi, 0, i)),
                  pl.BlockSpec((1, NE, TCH), lambda bi, i, off: (bi, 0, i)),
                  pl.BlockSpec((1, TCH, D), lambda bi, i, off: (bi, i, 0)),
                  pl.BlockSpec((1, 1, D), lambda bi, i, off: (bi, 0, 0))],
        out_specs=pl.BlockSpec((1, TCH, D), lambda bi, i, off: (bi, i, 0)),
    )
    return pl.pallas_call(
        _combine_body,
        grid_spec=grid_spec,
        out_shape=jax.ShapeDtypeStruct((b, SEQ, D), F32),
        compiler_params=_cparams(("parallel", "arbitrary")),
        name="moe_combine",
    )(offs_flat, y, pos, gate, x1, g2r)


def _np_split(m):
    hi = np.asarray(m, np.float64).astype(BF16)
    lo = (m - hi.astype(np.float64)).astype(BF16)
    return jnp.asarray(hi), jnp.asarray(lo)


@functools.lru_cache(maxsize=None)
def _dft_tables():
    a = np.arange(FN1, dtype=np.float64)
    ang = 2.0 * np.pi * np.outer(a, a) / FN1
    fr, fi = np.cos(ang), -np.sin(ang)
    half = SEQ // FN2
    lead_u = np.block([[fr[:, :half], -fi[:, :half]], [fi[:, :half], fr[:, :half]]])
    lead_k = np.concatenate([fr, fi], axis=0)
    fwd = np.block([[fr, -fi], [fi, fr]])
    inv = np.block([[fr, fi], [-fi, fr]])
    out = np.block([[fr[:half], fi[:half]], [-fi[:half], fr[:half]]])
    n2 = np.arange(FN2, dtype=np.float64)
    tw = 2.0 * np.pi * np.outer(a, n2) / FN
    twr = np.cos(tw).astype(np.float32)
    twi = (-np.sin(tw)).astype(np.float32)
    return dict(lead_u=lead_u, lead_k=lead_k, fwd=fwd, inv=inv, out=out, twr=twr, twi=twi)


@functools.lru_cache(maxsize=None)
def _filter_tables():
    L = SEQ
    n = np.arange(FN).reshape(FN1, FN2).T.reshape(-1)
    lag = np.where(n < L, n, FN - n)
    jc = np.minimum(lag, L - 1).astype(np.float64)
    t = (jc / (L - 1))[:, None]
    bands = (FEMB - 1) // 2
    w = 2.0 * np.pi * jc / L
    f = np.linspace(1e-4, bands - 1, bands)
    fw = w[:, None] * f[None, :]
    z = np.concatenate([t, np.cos(fw), -np.sin(fw), np.zeros((FN, FORD - FEMB))], axis=-1)
    mask = np.where(n == L, 0.0, 1.0)[:, None]
    fwd = np.where(n < L, 1.0, 0.0)[:, None]
    max_decay = math.log(DECAY_TARGET) / FAST_DECAY_PCT
    min_decay = math.log(DECAY_TARGET) / SLOW_DECAY_PCT
    negdelta = -np.abs(np.linspace(min_decay, max_decay, HY_W))[None, :]
    z = z.reshape(FN // FILT_TR, 2, FILT_TR // 2, FORD).transpose(0, 2, 1, 3).reshape(FN // 2, 2 * FORD)
    return tuple(np.asarray(a, np.float32) for a in (z, t, mask, fwd, negdelta))


@functools.lru_cache(maxsize=None)
def _rope_tables(n):
    rows = n // GRID_W
    row_id, col_id = np.meshgrid(np.arange(rows, dtype=np.float64), np.arange(GRID_W, dtype=np.float64), indexing="ij")
    quarter = HD // 4
    inv_freq = ROPE_THETA ** (-np.arange(quarter, dtype=np.float64) / quarter)
    ar = row_id.reshape(-1)[:, None] * inv_freq
    ac = col_id.reshape(-1)[:, None] * inv_freq
    cos = np.concatenate([np.cos(ar), np.cos(ar), np.cos(ac), np.cos(ac)], axis=-1)
    sin = np.concatenate([-np.sin(ar), np.sin(ar), -np.sin(ac), np.sin(ac)], axis=-1)
    reps = (1, LANES // HD)
    return np.tile(cos, reps).astype(np.float32), np.tile(sin, reps).astype(np.float32)


def _hyena_long_conv(u_rj, x2_rj, kern, abs_sum, bias):
    tb = _dft_tables()
    twr, twi = tb["twr"], tb["twi"]
    fwd = _np_split(tb["fwd"])
    scale = 1.0 / (abs_sum * float(FN))
    khat = _filter_spectrum(kern.reshape(FN2, FN1, HY_W), twr, twi, *_np_split(tb["lead_k"]), *fwd, scale)
    return _hyena_conv(u_rj, x2_rj, khat, twr, twi, _np_split(tb["lead_u"]), fwd, _np_split(tb["inv"]),
                       _np_split(tb["out"]), bias.reshape(1, HY_W))


def kernel(x, c, ctx, c_ctx, w_mod, b_mod, norm1_g, norm2_g, w_in, w_out, q_norm_g, k_norm_g,
           conv_w, conv_b, filt_w1, filt_b1, filt_w2, filt_b2, filt_w3, filt_freq, hyena_bias,
           w_router, w_gate, w_up, w_down):
    B = x.shape[0]
    assert x.shape == (B, SEQ, D) and B == 2 and ctx.shape == (B, CTX, D) and w_mod.shape[0] == 1
    l = 0

    cc = jnp.concatenate([c, c_ctx[None, :], jnp.zeros((SUBLANES - B - 1, D), F32)], axis=0)
    mod = _modulation(cc, w_mod[l], b_mod[l][None, :])
    sh1, sc1, g1, sh2, sc2, g2 = [mod[:, i * D:(i + 1) * D] for i in range(6)]
    lat = lambda m: m[:B, None, :]
    ctxrow = lambda m: jnp.broadcast_to(m[B:B + 1, None, :], (B, 1, D))

    w_in_bf = w_in[l].astype(BF16)
    gq2 = jnp.tile(q_norm_g[l][None, :], (1, LANES // HD))
    gk2 = jnp.tile(k_norm_g[l][None, :], (1, LANES // HD))
    bd = jnp.asarray(np.kron(np.eye(2 * LANES // HD), np.full((HD, HD), 1.0 / HD)), BF16)
    cos_t, sin_t = _rope_tables(SEQ)
    n1g = norm1_g[l][None, :]

    q, k, vt, p = _in_projection(x, n1g, lat(sh1), lat(sc1), w_in_bf, gq2, gk2, bd, cos_t, sin_t, 512)
    _, kc, vct, _ = _in_projection(ctx, n1g, ctxrow(sh1), ctxrow(sc1), w_in_bf, gq2, gk2, bd,
                                   jnp.ones((CTX, LANES), F32), jnp.zeros((CTX, LANES), F32), CTX)

    kch = jnp.concatenate([k, kc], axis=2).reshape(B, NKV, ATT_NCH, ATT_TK, HD)
    vt_all = jnp.concatenate([vt, vct], axis=3).reshape(B, NKV, HD, ATT_NCH, ATT_TK)
    ones_pad = jnp.concatenate([jnp.ones((B, NKV, ATT_NCH, 1, ATT_TK), BF16),
                                jnp.zeros((B, NKV, ATT_NCH, BF16_ROWS - 1, ATT_TK), BF16)], axis=3)
    vtch = jnp.concatenate([vt_all.transpose(0, 1, 3, 2, 4), ones_pad], axis=3)
    bound = (1.02 * HD * Q_SCALE) * jnp.max(jnp.abs(q_norm_g[l])) * jnp.max(jnp.abs(k_norm_g[l]))
    att = _attention(bound.reshape(1).astype(F32), q, kch, vtch)

    cw9 = conv_w[l].reshape(3, 3, HY_W).reshape(9, HY_W)
    cb3 = conv_b[l].reshape(3, HY_W)
    u_rj, x2_rj = _short_conv(p, cw9, cb3)
    ztab, ttab, mtab, ftab, negdelta = _filter_tables()
    w1p = jnp.concatenate([filt_w1[l], jnp.zeros((FORD - FEMB, FORD), F32)], axis=0)
    twice = lambda w: jnp.kron(jnp.eye(2, dtype=F32), w)
    pair = lambda v: jnp.tile(v[None, :], (1, 2))
    kern, abs_sum = _implicit_filter(ztab, ttab, mtab, ftab, twice(w1p), pair(filt_b1[l]), twice(filt_w2[l]),
                                     pair(filt_b2[l]), twice(filt_w3[l]), pair(filt_freq[l]), negdelta)
    hy = _hyena_long_conv(u_rj, x2_rj, kern, abs_sum, hyena_bias[l])

    wr2 = jnp.concatenate(_split(w_router[l].T), axis=0)
    x1, h2, logits = _out_projection(att, hy, x, w_out[l].astype(BF16), lat(g1), norm2_g[l][None, :],
                                     lat(sh2), lat(sc2), wr2)

    tri = jnp.asarray(np.triu(np.ones((TCH, TCH))), BF16)
    pos, gate, offs = _routing(logits, tri)
    offs_flat = offs.reshape(-1)
    xg = _gather(offs_flat, h2, pos.reshape(B, NE, NTCH, TCH))
    y = _expert_ffn(xg, jnp.swapaxes(w_gate[l], 1, 2), jnp.swapaxes(w_up[l], 1, 2), w_down[l])
    return _combine(offs_flat, y, pos, gate, x1, lat(g2))
```

```python
import functools
import math

import numpy as np
import jax
import jax.numpy as jnp
from jax import lax
from jax.experimental import pallas as pl
from jax.experimental.pallas import tpu as pltpu

F32 = jnp.float32
BF16 = jnp.bfloat16
I32 = jnp.int32

D = 1024
SEQ = 8192
CTX = 256
GRID_W = 64
ATT_W = 512
HY_W = 512
HD = 64
NQ = 8
NKV = 2
QPK = NQ // NKV
KV_W = NKV * HD
IN_W = ATT_W + 2 * KV_W + 3 * HY_W
FEMB = 33
FORD = 64
NE = 16
CAP = 2 * SEQ // NE
DEXP = 2752
ROPE_THETA = 10000.0
EPS = 1e-6
DECAY_TARGET = 1e-2
FAST_DECAY_PCT = 0.3
SLOW_DECAY_PCT = 1.5

LANES = 128
SUBLANES = 8
BF16_ROWS = 16
VMEM_BYTES_V7X = 64 * 1024 * 1024
VMEM_LIMIT = VMEM_BYTES_V7X - 8 * 1024 * 1024

FN = 2 * SEQ
FN1 = 128
FN2 = 128

TCH = LANES
NTCH = SEQ // TCH
GW = TCH + SUBLANES
CW = TCH + BF16_ROWS
YROWS = CAP + BF16_ROWS


def _cparams(sem, vmem=None):
    return pltpu.CompilerParams(dimension_semantics=sem, vmem_limit_bytes=vmem or VMEM_LIMIT)


def _split(a):
    hi = a.astype(BF16)
    lo = (a - hi.astype(F32)).astype(BF16)
    return hi, lo


_NN = (((1,), (0,)), ((), ()))
_NT = (((1,), (1,)), ((), ()))
_TN = (((0,), (0,)), ((), ()))


def _dot(a, b, dn=_NN):
    return lax.dot_general(a, b, dn, preferred_element_type=F32)


def _dot3(a, b, dn=_NN):
    ah, al = _split(a)
    bh, bl = _split(b)
    return _dot(ah, bh, dn) + _dot(ah, bl, dn) + _dot(al, bh, dn)


def _mod_body(c_ref, w_ref, b_ref, o_ref):
    c = c_ref[...]
    s = c * (1.0 / (1.0 + jnp.exp(-c)))
    o_ref[...] = _dot3(s, w_ref[...]) + b_ref[...]


def _modulation(cc, w_mod, b_mod):
    n = w_mod.shape[1]
    return pl.pallas_call(
        _mod_body,
        grid=(n // D,),
        in_specs=[pl.BlockSpec((SUBLANES, D), lambda j: (0, 0)),
                  pl.BlockSpec((D, D), lambda j: (0, j)),
                  pl.BlockSpec((1, D), lambda j: (0, j))],
        out_specs=pl.BlockSpec((SUBLANES, D), lambda j: (0, j)),
        out_shape=jax.ShapeDtypeStruct((SUBLANES, n), F32),
        compiler_params=_cparams(("arbitrary",)),
        name="modulation",
    )(cc, w_mod, b_mod)


Q_SCALE = HD ** -0.5 * math.log2(math.e)


def _rms_mod(x, g, sh, sc):
    ms = jnp.mean(x * x, axis=-1, keepdims=True)
    return (x * lax.rsqrt(ms + EPS) * g) * (1.0 + sc) + sh


def _head_mean_square(t, bd):
    hi, lo = _split(t * t)
    return _dot(hi, bd) + _dot(lo, bd)


def _head_norm_rope(t, ms, g, cos, sin):
    tn = t * lax.rsqrt(ms + EPS) * g
    lane = lax.broadcasted_iota(I32, tn.shape, 1)
    sw = jnp.where((lane & 31) < 16, pltpu.roll(tn, LANES - 16, 1), pltpu.roll(tn, 16, 1))
    return tn * cos + sw * sin


def _proj_body(x_ref, g_ref, sh_ref, sc_ref, w_ref, gq_ref, gk_ref, bd_ref, cos_ref, sin_ref,
               q_ref, k_ref, v_ref, p_ref):
    h = _rms_mod(x_ref[0], g_ref[...], sh_ref[0], sc_ref[0])
    proj = _dot(h.astype(BF16), w_ref[...])
    bd = bd_ref[...]
    cos = cos_ref[...]
    sin = sin_ref[...]
    wide = 2 * LANES
    for j in range(ATT_W // wide):
        ms = _head_mean_square(proj[:, j * wide:(j + 1) * wide], bd)
        for i in range(2):
            sl = slice(j * wide + i * LANES, j * wide + (i + 1) * LANES)
            qj = _head_norm_rope(proj[:, sl], ms[:, i * LANES:(i + 1) * LANES], gq_ref[...], cos, sin)
            q_ref[0, :, sl] = (qj * Q_SCALE).astype(BF16)
    ms = _head_mean_square(proj[:, ATT_W:ATT_W + 2 * KV_W], bd)
    kk = _head_norm_rope(proj[:, ATT_W:ATT_W + KV_W], ms[:, 0:KV_W], gk_ref[...], cos, sin)
    vt = proj[:, ATT_W + KV_W:ATT_W + 2 * KV_W].T
    for g in range(NKV):
        k_ref[0, g] = kk[:, g * HD:(g + 1) * HD].astype(BF16)
        v_ref[0, g] = vt[g * HD:(g + 1) * HD, :].astype(BF16)
    p_ref[0] = proj[:, ATT_W + 2 * KV_W:]


def _in_projection(x, g1, sh, sc, w_in_bf, gq2, gk2, bd, cos_t, sin_t, tm):
    b, s, _ = x.shape
    row = lambda bi, i: (bi, 0, 0)
    tok = lambda bi, i: (bi, i, 0)
    const = lambda bi, i: (0, 0)
    return pl.pallas_call(
        _proj_body,
        grid=(b, s // tm),
        in_specs=[pl.BlockSpec((1, tm, D), tok),
                  pl.BlockSpec((1, D), const),
                  pl.BlockSpec((1, 1, D), row),
                  pl.BlockSpec((1, 1, D), row),
                  pl.BlockSpec((D, IN_W), const),
                  pl.BlockSpec((1, LANES), const),
                  pl.BlockSpec((1, LANES), const),
                  pl.BlockSpec((2 * LANES, 2 * LANES), const),
                  pl.BlockSpec((tm, LANES), lambda bi, i: (i, 0)),
                  pl.BlockSpec((tm, LANES), lambda bi, i: (i, 0))],
        out_specs=[pl.BlockSpec((1, tm, ATT_W), tok),
                   pl.BlockSpec((1, NKV, tm, HD), lambda bi, i: (bi, 0, i, 0)),
                   pl.BlockSpec((1, NKV, HD, tm), lambda bi, i: (bi, 0, 0, i)),
                   pl.BlockSpec((1, tm, 3 * HY_W), tok)],
        out_shape=[jax.ShapeDtypeStruct((b, s, ATT_W), BF16),
                   jax.ShapeDtypeStruct((b, NKV, s, HD), BF16),
                   jax.ShapeDtypeStruct((b, NKV, HD, s), BF16),
                   jax.ShapeDtypeStruct((b, s, 3 * HY_W), F32)],
        compiler_params=_cparams(("parallel", "parallel")),
        name="in_projection",
    )(x, g1, sh, sc, w_in_bf, gq2, gk2, bd, cos_t, sin_t)


ATT_TQ = 512
ATT_TK = 768
SK = SEQ + CTX
ATT_NCH = SK // ATT_TK


ATT_NQ = QPK * ATT_TQ
ATT_VR = HD + BF16_ROWS
assert ATT_NCH % 2 == 1
ATT_SHIFT_MAX = 120.0


def _attn_body(bound_ref, q_ref, k_ref, vt_ref, o_ref, s_ref, mx_ref, m_ref, acc_ref):
    qall = jnp.concatenate([q_ref[0, :, r * HD:(r + 1) * HD] for r in range(QPK)], axis=0)
    acc_ref[...] = jnp.zeros_like(acc_ref)
    bound = bound_ref[0]
    fixed_shift = 2.0 * bound <= ATT_SHIFT_MAX

    def finish():
        out = acc_ref[0:HD, :] * (1.0 / acc_ref[HD:HD + 1, :])
        for r in range(QPK):
            o_ref[0, :, r * HD:(r + 1) * HD] = out[:, r * ATT_TQ:(r + 1) * ATT_TQ].T.astype(BF16)

    @pl.when(fixed_shift)
    def _():
        def chunk(c, _):
            s = _dot(k_ref[0, 0, c], qall, _NT)
            acc_ref[...] += _dot(vt_ref[0, 0, c], jnp.exp2(s - bound).astype(BF16))
            return 0

        lax.fori_loop(0, ATT_NCH, chunk, 0, unroll=True)
        finish()

    @pl.when(jnp.logical_not(fixed_shift))
    def _():
        m_ref[...] = jnp.full(m_ref.shape, -1e30, F32)

        def scores(c, slot):
            s = _dot(k_ref[0, 0, c], qall, _NT)
            s_ref[slot] = s
            mx_ref[slot] = jnp.max(s, axis=0, keepdims=True)

        def update(c, slot):
            m_old = m_ref[...]
            m_new = jnp.maximum(m_old, mx_ref[slot])
            p = jnp.exp2(s_ref[slot] - m_new).astype(BF16)
            acc_ref[...] = jnp.exp2(m_old - m_new) * acc_ref[...] + _dot(vt_ref[0, 0, c], p)
            m_ref[...] = m_new

        scores(0, 0)

        def pair(i, _):
            c = 2 * i
            scores(c + 1, 1)
            update(c, 0)
            scores(c + 2, 0)
            update(c + 1, 1)
            return 0

        lax.fori_loop(0, ATT_NCH // 2, pair, 0)
        update(ATT_NCH - 1, 0)
        finish()


def _attention(bound, q, kch, vtch):
    b = q.shape[0]
    grid_spec = pltpu.PrefetchScalarGridSpec(
        num_scalar_prefetch=1,
        grid=(b, NKV, SEQ // ATT_TQ),
        in_specs=[pl.BlockSpec((1, ATT_TQ, QPK * HD), lambda bi, g, i, bd: (bi, i, g)),
                  pl.BlockSpec((1, 1, ATT_NCH, ATT_TK, HD), lambda bi, g, i, bd: (bi, g, 0, 0, 0)),
                  pl.BlockSpec((1, 1, ATT_NCH, ATT_VR, ATT_TK), lambda bi, g, i, bd: (bi, g, 0, 0, 0))],
        out_specs=pl.BlockSpec((1, ATT_TQ, QPK * HD), lambda bi, g, i, bd: (bi, i, g)),
        scratch_shapes=[pltpu.VMEM((2, ATT_TK, ATT_NQ), F32), pltpu.VMEM((2, 1, ATT_NQ), F32),
                        pltpu.VMEM((1, ATT_NQ), F32), pltpu.VMEM((ATT_VR, ATT_NQ), F32)],
    )
    return pl.pallas_call(
        _attn_body,
        grid_spec=grid_spec,
        out_shape=jax.ShapeDtypeStruct((b, SEQ, ATT_W), BF16),
        compiler_params=_cparams(("parallel", "parallel", "parallel")),
        name="attention",
    )(bound, q, kch, vtch)


SC_TM = 2048
SC_J = SC_TM // FN2


def _sconv_body(m1, a1, n1, m2, a2, n2, m3, a3, n3, w_ref, b_ref, u_ref, x2_ref):
    i = pl.program_id(1)
    last = pl.num_programs(1) - 1
    rows = lax.broadcasted_iota(I32, (SC_TM, HY_W), 0)

    def conv(main, prev, nxt, g):
        x = main[0]
        pr = jnp.where(i > 0, prev[0, SUBLANES - 1:SUBLANES, :], 0.0)
        nx = jnp.where(i < last, nxt[0, 0:1, :], 0.0)
        xm = jnp.where(rows == 0, pr, pltpu.roll(x, 1, 0))
        xp = jnp.where(rows == SC_TM - 1, nx, pltpu.roll(x, SC_TM - 1, 0))
        return (w_ref[g:g + 1, :] * xm + w_ref[3 + g:4 + g, :] * x + w_ref[6 + g:7 + g, :] * xp
                + b_ref[g:g + 1, :])

    def to_rj(t):
        return jnp.swapaxes(t.reshape(SC_J, FN2, HY_W), 0, 1).astype(BF16)

    x1 = conv(m1, a1, n1, 0)
    x2 = conv(m2, a2, n2, 1)
    v = conv(m3, a3, n3, 2)
    u_ref[0] = to_rj(v * x1)
    x2_ref[0] = to_rj(x2)


def _short_conv(p, cw9, cb3):
    b = p.shape[0]
    nblk8 = SEQ // SUBLANES
    step8 = SC_TM // SUBLANES
    specs = []
    for g in range(3):
        specs += [pl.BlockSpec((1, SC_TM, HY_W), lambda bi, i, g=g: (bi, i, g)),
                  pl.BlockSpec((1, SUBLANES, HY_W), lambda bi, i, g=g: (bi, jnp.maximum(i * step8 - 1, 0), g)),
                  pl.BlockSpec((1, SUBLANES, HY_W), lambda bi, i, g=g: (bi, jnp.minimum((i + 1) * step8, nblk8 - 1), g))]
    specs += [pl.BlockSpec((9, HY_W), lambda bi, i: (0, 0)), pl.BlockSpec((3, HY_W), lambda bi, i: (0, 0))]
    out = pl.BlockSpec((1, FN2, SC_J, HY_W), lambda bi, i: (bi, 0, i, 0))
    return pl.pallas_call(
        _sconv_body,
        grid=(b, SEQ // SC_TM),
        in_specs=specs,
        out_specs=[out, out],
        out_shape=[jax.ShapeDtypeStruct((b, FN2, SEQ // FN2, HY_W), BF16)] * 2,
        compiler_params=_cparams(("parallel", "parallel")),
        name="short_conv",
    )(p, p, p, p, p, p, p, p, p, cw9, cb3)


FILT_TR = 1024


def _filter_body(z_ref, t_ref, msk_ref, fwd_ref, w1_ref, b1_ref, w2_ref, b2_ref, w3_ref, fr_ref, dl_ref,
                 k_ref, s_ref):
    fr = fr_ref[...]
    h = jnp.sin(fr * (_dot3(z_ref[...], w1_ref[...]) + b1_ref[...]))
    h = jnp.sin(fr * (_dot3(h, w2_ref[...]) + b2_ref[...]))
    h = _dot3(h, w3_ref[...])
    h = jnp.concatenate([h[:, :2 * HY_W], h[:, 2 * HY_W:]], axis=0)
    h = jnp.where(fwd_ref[...] > 0.5, h[:, :HY_W], h[:, HY_W:])
    kern = h * jnp.exp(t_ref[...] * dl_ref[...]) * msk_ref[...]
    k_ref[...] = kern

    @pl.when(pl.program_id(0) == 0)
    def _():
        s_ref[...] = jnp.zeros_like(s_ref)

    s_ref[...] += jnp.sum(jnp.abs(kern), axis=0, keepdims=True)


def _implicit_filter(ztab, ttab, mtab, ftab, w1p, b1, w2, b2, w3, freq, negdelta):
    rowblk = lambda i: (i, 0)
    const = lambda i: (0, 0)
    col = pl.BlockSpec((FILT_TR, 1), rowblk)
    return pl.pallas_call(
        _filter_body,
        grid=(FN // FILT_TR,),
        in_specs=[pl.BlockSpec((FILT_TR // 2, 2 * FORD), rowblk), col, col, col,
                  pl.BlockSpec((2 * FORD, 2 * FORD), const),
                  pl.BlockSpec((1, 2 * FORD), const),
                  pl.BlockSpec((2 * FORD, 2 * FORD), const),
                  pl.BlockSpec((1, 2 * FORD), const),
                  pl.BlockSpec((2 * FORD, 4 * HY_W), const),
                  pl.BlockSpec((1, 2 * FORD), const),
                  pl.BlockSpec((1, HY_W), const)],
        out_specs=[pl.BlockSpec((FILT_TR, HY_W), rowblk),
                   pl.BlockSpec((1, HY_W), const)],
        out_shape=[jax.ShapeDtypeStruct((FN, HY_W), F32), jax.ShapeDtypeStruct((1, HY_W), F32)],
        compiler_params=_cparams(("arbitrary",)),
        name="implicit_filter",
    )(ztab, ttab, mtab, ftab, w1p, b1, w2, b2, w3, freq, negdelta)


DFT_G = 4
DFT_KB = 16
DFT_NP = FN1 // DFT_KB
DFT_UNROLL = 4


def _dot2c(fh, fl, zb):
    return _dot(fh, zb) + _dot(fl, zb)


def _lead_stage(src, fh, fl, dst_ref):
    def group(rg, _):
        r0 = rg * DFT_G
        rhs = jnp.concatenate([src(r0 + g) for g in range(DFT_G)], axis=1)
        blk = _dot2c(fh, fl, rhs)
        for g in range(DFT_G):
            dst_ref[r0 + g] = blk[:, g * LANES:(g + 1) * LANES]
        return 0

    lax.fori_loop(0, FN2 // DFT_G, group, 0, unroll=DFT_UNROLL)


def _lead_phase(src, lh_ref, ll_ref, p_ref, q_ref):
    for h in range(2):
        rows = slice(h * FN1, (h + 1) * FN1)
        _lead_stage(src, lh_ref[rows, :], ll_ref[rows, :], p_ref)
        q_ref[rows] = jnp.swapaxes(p_ref[...], 0, 1)


def _lane_cat(xs):
    return jnp.concatenate(xs, axis=1)


def _mid_forward(q_ref, k0, tr_ref, ti_ref, fh, fl, half):
    tr_t, ti_t = tr_ref[...].T, ti_ref[...].T
    brs, bis, trs, tis = [], [], [], []
    for g in range(DFT_G):
        jj = half * DFT_G + g
        ar, ai = q_ref[k0 + jj], q_ref[FN1 + k0 + jj]
        tr = jnp.broadcast_to(tr_t[:, jj:jj + 1], (FN2, LANES))
        ti = jnp.broadcast_to(ti_t[:, jj:jj + 1], (FN2, LANES))
        brs.append(ar * tr - ai * ti)
        bis.append(ar * ti + ai * tr)
        trs.append(tr)
        tis.append(ti)
    b = jnp.concatenate([_lane_cat(brs), _lane_cat(bis)], axis=0).astype(BF16)
    return _dot2c(fh, fl, b), _lane_cat(trs), _lane_cat(tis)


def _spectrum_body(k_ref, tr_ref, ti_ref, lh_ref, ll_ref, fh_ref, fl_ref, sc_ref, o_ref, p_ref, q_ref):
    ph = pl.program_id(1)

    @pl.when(ph == 0)
    def _():
        _lead_phase(lambda r: k_ref[r].astype(BF16), lh_ref, ll_ref, p_ref, q_ref)

    @pl.when(ph > 0)
    def _():
        k0 = (ph - 1) * DFT_KB
        sc = _lane_cat([sc_ref[...]] * DFT_G)
        for half in range(DFT_KB // DFT_G):
            x, _, _ = _mid_forward(q_ref, k0, tr_ref, ti_ref, fh_ref[...], fl_ref[...], half)
            x = x * sc
            for g in range(DFT_G):
                lanes = slice(g * LANES, (g + 1) * LANES)
                o_ref[0, half * DFT_G + g] = x[:FN2, lanes]
                o_ref[1, half * DFT_G + g] = x[FN2:, lanes]


def _mid_index(ph):
    return jnp.clip(ph - 1, 0, DFT_NP - 1)


def _filter_spectrum(kern_rj, twr, twi, lh, ll, fh, fl, scale):
    const = lambda c, ph: (0, 0)
    tw = pl.BlockSpec((DFT_KB, FN2), lambda c, ph: (_mid_index(ph), 0))
    return pl.pallas_call(
        _spectrum_body,
        grid=(HY_W // LANES, DFT_NP + 1),
        in_specs=[pl.BlockSpec((FN2, FN1, LANES), lambda c, ph: (0, 0, c)),
                  tw, tw,
                  pl.BlockSpec(lh.shape, const), pl.BlockSpec(ll.shape, const),
                  pl.BlockSpec(fh.shape, const), pl.BlockSpec(fl.shape, const),
                  pl.BlockSpec((1, LANES), lambda c, ph: (0, c))],
        out_specs=pl.BlockSpec((2, DFT_KB, FN2, LANES), lambda c, ph: (0, _mid_index(ph), 0, c)),
        out_shape=jax.ShapeDtypeStruct((2, FN1, FN2, HY_W), F32),
        scratch_shapes=[pltpu.VMEM((FN2, FN1, LANES), F32), pltpu.VMEM((2 * FN1, FN2, LANES), F32)],
        compiler_params=_cparams(("parallel", "arbitrary")),
        name="filter_spectrum",
    )(kern_rj, twr, twi, lh, ll, fh, fl, scale)


def _hconv_body(u_ref, x2_ref, kh_ref, tr_ref, ti_ref, lh_ref, ll_ref, fh_ref, fl_ref, gh_ref, gl_ref,
                oh_ref, ol_ref, bias_ref, o_ref, p_ref, q_ref):
    ph = pl.program_id(1)

    def both(ref, r):
        return jnp.concatenate([ref[0, r], ref[1, r]], axis=0)

    @pl.when(ph == 0)
    def _():
        _lead_phase(lambda r: both(u_ref, r), lh_ref, ll_ref, p_ref, q_ref)

    @pl.when((ph > 0) & (ph <= DFT_NP))
    def _():
        k0 = (ph - 1) * DFT_KB
        for half in range(DFT_KB // DFT_G):
            x, tr, ti = _mid_forward(q_ref, k0, tr_ref, ti_ref, fh_ref[...], fl_ref[...], half)
            xr, xi = x[:FN2], x[FN2:]
            kr = _lane_cat([kh_ref[0, half * DFT_G + g] for g in range(DFT_G)])
            ki = _lane_cat([kh_ref[1, half * DFT_G + g] for g in range(DFT_G)])
            y = jnp.concatenate([xr * kr - xi * ki, xr * ki + xi * kr], axis=0).astype(BF16)
            c = _dot2c(gh_ref[...], gl_ref[...], y)
            cr, ci = c[:FN2], c[FN2:]
            dr = cr * tr + ci * ti
            di = ci * tr - cr * ti
            for g in range(DFT_G):
                lanes = slice(g * LANES, (g + 1) * LANES)
                q_ref[k0 + half * DFT_G + g] = dr[:, lanes]
                q_ref[FN1 + k0 + half * DFT_G + g] = di[:, lanes]

    @pl.when(ph == DFT_NP + 1)
    def _():
        bias = bias_ref[...]
        p_ref[...] = jnp.swapaxes(q_ref[0:FN1], 0, 1)
        _lead_stage(lambda r: p_ref[r].astype(BF16), oh_ref[:, 0:FN1], ol_ref[:, 0:FN1], q_ref)
        p_ref[...] = jnp.swapaxes(q_ref[FN1:2 * FN1], 0, 1)
        oh2, ol2 = oh_ref[:, FN1:2 * FN1], ol_ref[:, FN1:2 * FN1]

        def group(rg, _):
            r0 = rg * DFT_G
            rhs = _lane_cat([p_ref[r0 + g].astype(BF16) for g in range(DFT_G)])
            blk = _dot2c(oh2, ol2, rhs)
            for g in range(DFT_G):
                r = r0 + g
                y = q_ref[r] + blk[:, g * LANES:(g + 1) * LANES]
                q_ref[r] = (y + both(u_ref, r).astype(F32) * bias) * both(x2_ref, r).astype(F32)
            return 0

        lax.fori_loop(0, FN2 // DFT_G, group, 0, unroll=DFT_UNROLL)
        p_ref[...] = jnp.swapaxes(q_ref[0:FN2], 0, 1)
        nj = SEQ // FN2
        for b in range(2):
            o_ref[b] = p_ref[b * nj:(b + 1) * nj].reshape(SEQ, LANES).astype(BF16)


def _hyena_conv(u_rj, x2_rj, khat, twr, twi, lead, fwd, inv, out, bias):
    const = lambda c, ph: (0, 0)
    nj = SEQ // FN2
    tw = pl.BlockSpec((DFT_KB, FN2), lambda c, ph: (_mid_index(ph), 0))
    sig = pl.BlockSpec((2, FN2, nj, LANES), lambda c, ph: (0, 0, 0, c))
    mats = [m for pair in (lead, fwd, inv, out) for m in pair]
    return pl.pallas_call(
        _hconv_body,
        grid=(HY_W // LANES, DFT_NP + 2),
        in_specs=[sig, sig,
                  pl.BlockSpec((2, DFT_KB, FN2, LANES), lambda c, ph: (0, _mid_index(ph), 0, c)),
                  tw, tw] + [pl.BlockSpec(m.shape, const) for m in mats]
                 + [pl.BlockSpec((1, LANES), lambda c, ph: (0, c))],
        out_specs=pl.BlockSpec((2, SEQ, LANES), lambda c, ph: (0, 0, c)),
        out_shape=jax.ShapeDtypeStruct((2, SEQ, HY_W), BF16),
        scratch_shapes=[pltpu.VMEM((FN2, FN1, LANES), F32), pltpu.VMEM((2 * FN1, FN2, LANES), F32)],
        compiler_params=_cparams(("parallel", "arbitrary")),
        name="hyena_conv",
    )(u_rj, x2_rj, khat, twr, twi, *mats, bias)


OP_TM = 512


def _outproj_body(att_ref, hy_ref, x_ref, w_ref, g1_ref, n2_ref, sh_ref, sc_ref, wr_ref,
                  x1_ref, h2_ref, lg_ref):
    a = jnp.concatenate([att_ref[0], hy_ref[0]], axis=1)
    x1 = x_ref[0] + g1_ref[0] * _dot(a, w_ref[...])
    x1_ref[0] = x1
    h2 = _rms_mod(x1, n2_ref[...], sh_ref[0], sc_ref[0])
    hh, hl = _split(h2)
    h2_ref[0] = hh
    wr = wr_ref[...]
    both = _dot(wr, hh, _NT)
    lg_ref[0] = both[0:NE] + both[NE:2 * NE] + _dot(wr[0:NE], hl, _NT)


def _out_projection(att, hy, x, w_out_bf, g1r, n2g, sh2, sc2, wr2):
    b = x.shape[0]
    tok = lambda bi, i: (bi, i, 0)
    row = lambda bi, i: (bi, 0, 0)
    const = lambda bi, i: (0, 0)
    return pl.pallas_call(
        _outproj_body,
        grid=(b, SEQ // OP_TM),
        in_specs=[pl.BlockSpec((1, OP_TM, ATT_W), tok),
                  pl.BlockSpec((1, OP_TM, HY_W), tok),
                  pl.BlockSpec((1, OP_TM, D), tok),
                  pl.BlockSpec((ATT_W + HY_W, D), const),
                  pl.BlockSpec((1, 1, D), row),
                  pl.BlockSpec((1, D), const),
                  pl.BlockSpec((1, 1, D), row),
                  pl.BlockSpec((1, 1, D), row),
                  pl.BlockSpec((2 * NE, D), const)],
        out_specs=[pl.BlockSpec((1, OP_TM, D), tok),
                   pl.BlockSpec((1, OP_TM, D), tok),
                   pl.BlockSpec((1, NE, OP_TM), lambda bi, i: (bi, 0, i))],
        out_shape=[jax.ShapeDtypeStruct((b, SEQ, D), F32),
                   jax.ShapeDtypeStruct((b, SEQ, D), BF16),
                   jax.ShapeDtypeStruct((b, NE, SEQ), F32)],
        compiler_params=_cparams(("parallel", "parallel")),
        name="out_projection",
    )(att, hy, x, w_out_bf, g1r, n2g, sh2, sc2, wr2)


def _routing_body(lg_ref, tri_ref, pos_ref, gate_ref, off_ref, cs_ref):
    lg = lg_ref[0]
    e = jnp.exp(lg - jnp.max(lg, axis=0, keepdims=True))
    aff = e / jnp.sum(e, axis=0, keepdims=True)
    gate_ref[0] = aff
    def count_ge(t):
        return jnp.sum(jnp.where(aff >= t, 1.0, 0.0), axis=1, keepdims=True)

    def bisect(i, thr):
        cand = thr | (jnp.int32(1) << (30 - i))
        return jnp.where(count_ge(pltpu.bitcast(cand, F32)) >= float(CAP), cand, thr)

    thr = lax.fori_loop(0, 31, bisect, jnp.zeros((NE, 1), I32))
    lo = pltpu.bitcast(thr, F32)
    hi = jnp.maximum(pltpu.bitcast(thr + 1, F32), jnp.finfo(F32).tiny)

    def refine(i, c):
        lo, hi = c
        mid = lo + (hi - lo) * 0.5
        ok = count_ge(mid) >= float(CAP)
        return jnp.where(ok, mid, lo), jnp.where(ok, hi, mid)

    lo, hi = lax.fori_loop(0, 32, refine, (lo, hi))
    gt = aff >= hi
    eq = (aff >= lo) & jnp.logical_not(gt)
    need = float(CAP) - jnp.sum(jnp.where(gt, 1.0, 0.0), axis=1, keepdims=True)
    tri = tri_ref[...]

    def excl_cumsum(mask_f, record_offsets):
        carry = jnp.zeros((NE, 1), F32)
        for c in range(NTCH):
            sl = slice(c * TCH, (c + 1) * TCH)
            m = mask_f[:, sl]
            inc = _dot(m.astype(BF16), tri)
            cs_ref[:, sl] = inc - m + carry
            if record_offsets:
                off_ref[0, :, c:c + 1] = carry.astype(I32)
            carry = carry + inc[:, TCH - 1:TCH]
        return cs_ref[...]

    eq_rank = excl_cumsum(jnp.where(eq, 1.0, 0.0), False)
    sel = gt | (eq & (eq_rank < need))
    pos = excl_cumsum(jnp.where(sel, 1.0, 0.0), True)
    pos_ref[0] = jnp.where(sel, pos.astype(I32), -1)


def _routing(logits, tri):
    b = logits.shape[0]
    blk = pl.BlockSpec((1, NE, SEQ), lambda bi: (bi, 0, 0))
    return pl.pallas_call(
        _routing_body,
        grid=(b,),
        in_specs=[blk, pl.BlockSpec((TCH, TCH), lambda bi: (0, 0))],
        out_specs=[blk, blk, pl.BlockSpec((1, NE, NTCH), lambda bi: (bi, 0, 0))],
        out_shape=[jax.ShapeDtypeStruct((b, NE, SEQ), I32),
                   jax.ShapeDtypeStruct((b, NE, SEQ), F32),
                   jax.ShapeDtypeStruct((b, NE, NTCH), I32)],
        scratch_shapes=[pltpu.VMEM((NE, SEQ), F32)],
        compiler_params=_cparams(("parallel",)),
        name="routing",
    )(logits, tri)


GATHER_UNROLL = 8


GW_SMALL = 64


def _gather_body(off_ref, h_ref, pos_ref, xg_ref, acc_ref):
    b = pl.program_id(0)
    e = pl.program_id(1)
    row0 = (b * NE + e) * NTCH
    acc_ref[...] = jnp.zeros_like(acc_ref)

    def count(c2, most):
        nxt = jnp.where(c2 + 1 < NTCH // 2, off_ref[row0 + jnp.minimum(2 * c2 + 2, NTCH - 1)], CAP)
        return jnp.maximum(most, nxt - off_ref[row0 + 2 * c2])

    most = lax.fori_loop(0, NTCH // 2, count, 0)

    def sweep(window, span):
        crow = lax.broadcasted_iota(I32, (window, span * TCH), 0)

        def chunks(i, _):
            for j in range(GATHER_UNROLL):
                c = (i * GATHER_UNROLL + j) * span
                off = off_ref[row0 + c]
                base = pl.multiple_of(jnp.minimum((off >> 3) << 3, CAP + SUBLANES - window), SUBLANES)
                t0 = pl.multiple_of(c * TCH, TCH)
                rel = _lane_cat([pos_ref[0, 0, pl.ds(c + s, 1), :] for s in range(span)]) - base
                onehot = jnp.where(crow == rel, 1.0, 0.0).astype(BF16)
                acc_ref[pl.ds(base, window), :] += _dot(onehot, h_ref[0, pl.ds(t0, span * TCH), :])
            return 0

        lax.fori_loop(0, NTCH // (GATHER_UNROLL * span), chunks, 0)

    @pl.when(most <= GW_SMALL - SUBLANES)
    def _():
        sweep(GW_SMALL, 2)

    @pl.when(most > GW_SMALL - SUBLANES)
    def _():
        sweep(GW, 1)

    xg_ref[0, 0] = acc_ref[0:CAP, :].astype(BF16)


def _gather(offs_flat, h2, pos4):
    b = h2.shape[0]
    grid_spec = pltpu.PrefetchScalarGridSpec(
        num_scalar_prefetch=1,
        grid=(b, NE),
        in_specs=[pl.BlockSpec((1, SEQ, D), lambda bi, e, off: (bi, 0, 0)),
                  pl.BlockSpec((1, 1, NTCH, TCH), lambda bi, e, off: (bi, e, 0, 0))],
        out_specs=pl.BlockSpec((1, 1, CAP, D), lambda bi, e, off: (bi, e, 0, 0)),
        scratch_shapes=[pltpu.VMEM((CAP + SUBLANES, D), F32)],
    )
    return pl.pallas_call(
        _gather_body,
        grid_spec=grid_spec,
        out_shape=jax.ShapeDtypeStruct((b, NE, CAP, D), BF16),
        compiler_params=_cparams(("parallel", "arbitrary")),
        name="moe_gather",
    )(offs_flat, h2, pos4)


FFN_TM = 512
FFN_NF = 4
FFN_FC = DEXP // FFN_NF
assert FFN_FC * FFN_NF == DEXP and FFN_FC % BF16_ROWS == 0


def _ffn_body(xg_ref, wgt_ref, wut_ref, wd_ref, y_ref, acc_ref):
    j = pl.program_id(1)
    nb = xg_ref.shape[0]

    @pl.when(j == 0)
    def _():
        acc_ref[...] = jnp.zeros_like(acc_ref)

    wgt = wgt_ref[0].astype(BF16)
    wut = wut_ref[0].astype(BF16)
    wd = wd_ref[0].astype(BF16)
    for b in range(nb):
        for mb in range(CAP // FFN_TM):
            rows = slice(mb * FFN_TM, (mb + 1) * FFN_TM)
            xb = xg_ref[b, 0, rows, :]
            a = _dot(xb, wgt, _NT)
            u = _dot(xb, wut, _NT)
            h = (a * (1.0 / (1.0 + jnp.exp(-a))) * u).astype(BF16)
            acc_ref[b, rows, :] += _dot(h, wd)

    @pl.when(j == FFN_NF - 1)
    def _():
        for b in range(nb):
            y_ref[b, 0, 0:CAP, :] = acc_ref[b].astype(BF16)
            y_ref[b, 0, CAP:YROWS, :] = jnp.zeros((YROWS - CAP, D), BF16)


def _expert_ffn(xg, w_gate_t, w_up_t, w_down):
    b = xg.shape[0]
    wblk = pl.BlockSpec((1, FFN_FC, D), lambda e, j: (e, j, 0))
    return pl.pallas_call(
        _ffn_body,
        grid=(NE, FFN_NF),
        in_specs=[pl.BlockSpec((b, 1, CAP, D), lambda e, j: (0, e, 0, 0)), wblk, wblk, wblk],
        out_specs=pl.BlockSpec((b, 1, YROWS, D), lambda e, j: (0, e, 0, 0)),
        out_shape=jax.ShapeDtypeStruct((b, NE, YROWS, D), BF16),
        scratch_shapes=[pltpu.VMEM((b, CAP, D), F32)],
        compiler_params=_cparams(("parallel", "arbitrary")),
        name="expert_ffn",
    )(xg, w_gate_t, w_up_t, w_down)


CW_SMALL = 64
CW_STACK = 2 * LANES // CW_SMALL


CMB_T = 2


def _combine_body(off_ref, y_ref, pos_ref, gate_ref, x1_ref, g2_ref, o_ref):
    b = pl.program_id(0)
    i = pl.program_id(1)

    def offset(e, c):
        return off_ref[(b * NE + e) * NTCH + c]

    def window(e, t, rows):
        lanes = slice(t * TCH, (t + 1) * TCH)
        base = pl.multiple_of(jnp.minimum((offset(e, i * CMB_T + t) >> 4) << 4, YROWS - rows), BF16_ROWS)
        rel = pos_ref[0, e:e + 1, lanes] - base
        crow = lax.broadcasted_iota(I32, (rows, TCH), 0)
        w = jnp.where(crow == rel, gate_ref[0, e:e + 1, lanes], 0.0).astype(BF16)
        return w, y_ref[0, e, pl.ds(base, rows), :]

    most = jnp.int32(0)
    for t in range(CMB_T):
        c = i * CMB_T + t
        for e in range(NE):
            nxt = jnp.where(c + 1 < NTCH, offset(e, jnp.minimum(c + 1, NTCH - 1)), CAP)
            most = jnp.maximum(most, nxt - offset(e, c))

    def finish(t, acc):
        rows = slice(t * TCH, (t + 1) * TCH)
        o_ref[0, rows, :] = x1_ref[0, rows, :] + g2_ref[0] * acc

    @pl.when(most <= CW_SMALL - BF16_ROWS)
    def _():
        for t in range(CMB_T):
            acc = jnp.zeros((TCH, D), F32)
            for e0 in range(0, NE, CW_STACK):
                ws, ys = zip(*[window(e, t, CW_SMALL) for e in range(e0, e0 + CW_STACK)])
                acc = acc + _dot(jnp.concatenate(ws, axis=0), jnp.concatenate(ys, axis=0), _TN)
            finish(t, acc)

    @pl.when(most > CW_SMALL - BF16_ROWS)
    def _():
        for t in range(CMB_T):
            acc = jnp.zeros((TCH, D), F32)
            for e in range(NE):
                w, yw = window(e, t, CW)
                acc = acc + _dot(w, yw, _TN)
            finish(t, acc)


def _combine(offs_flat, y, pos, gate, x1, g2r):
    b = x1.shape[0]
    grid_spec = pltpu.PrefetchScalarGridSpec(
        num_scalar_prefetch=1,
        grid=(b, NTCH // CMB_T),
        in_specs=[pl.BlockSpec((1, NE, YROWS, D), lambda bi, i, off: (bi, 0, 0, 0),
                               pipeline_mode=pl.Buffered(1)),
                  pl.BlockSpec((1, NE, CMB_T * TCH), lambda bi, i, off: (bi, 0, i)),
                  pl.BlockSpec((1, NE, CMB_T * TCH), lambda bi, i, off: (bi, 0, i)),
                  pl.BlockSpec((1, CMB_T * TCH, D), lambda bi, i, off: (bi, i, 0)),
                  pl.BlockSpec((1, 1, D), lambda bi, i, off: (bi, 0, 0))],
        out_specs=pl.BlockSpec((1, CMB_T * TCH, D), lambda bi, i, off: (bi, i, 0)),
    )
    return pl.pallas_call(
        _combine_body,
        grid_spec=grid_spec,
        out_shape=jax.ShapeDtypeStruct((b, SEQ, D), F32),
        compiler_params=_cparams(("parallel", "arbitrary")),
        name="moe_combine",
    )(offs_flat, y, pos, gate, x1, g2r)


def _np_split(m):
    hi = np.asarray(m, np.float64).astype(BF16)
    lo = (m - hi.astype(np.float64)).astype(BF16)
    return jnp.asarray(hi), jnp.asarray(lo)


@functools.lru_cache(maxsize=None)
def _dft_tables():
    a = np.arange(FN1, dtype=np.float64)
    ang = 2.0 * np.pi * np.outer(a, a) / FN1
    fr, fi = np.cos(ang), -np.sin(ang)
    half = SEQ // FN2
    lead_u = np.block([[fr[:, :half], -fi[:, :half]], [fi[:, :half], fr[:, :half]]])
    lead_k = np.concatenate([fr, fi], axis=0)
    fwd = np.block([[fr, -fi], [fi, fr]])
    inv = np.block([[fr, fi], [-fi, fr]])
    out = np.block([[fr[:half], fi[:half]], [-fi[:half], fr[:half]]])
    n2 = np.arange(FN2, dtype=np.float64)
    tw = 2.0 * np.pi * np.outer(a, n2) / FN
    twr = np.cos(tw).astype(np.float32)
    twi = (-np.sin(tw)).astype(np.float32)
    return dict(lead_u=lead_u, lead_k=lead_k, fwd=fwd, inv=inv, out=out, twr=twr, twi=twi)


@functools.lru_cache(maxsize=None)
def _filter_tables():
    L = SEQ
    n = np.arange(FN).reshape(FN1, FN2).T.reshape(-1)
    lag = np.where(n < L, n, FN - n)
    jc = np.minimum(lag, L - 1).astype(np.float64)
    t = (jc / (L - 1))[:, None]
    bands = (FEMB - 1) // 2
    w = 2.0 * np.pi * jc / L
    f = np.linspace(1e-4, bands - 1, bands)
    fw = w[:, None] * f[None, :]
    z = np.concatenate([t, np.cos(fw), -np.sin(fw), np.zeros((FN, FORD - FEMB))], axis=-1)
    mask = np.where(n == L, 0.0, 1.0)[:, None]
    fwd = np.where(n < L, 1.0, 0.0)[:, None]
    max_decay = math.log(DECAY_TARGET) / FAST_DECAY_PCT
    min_decay = math.log(DECAY_TARGET) / SLOW_DECAY_PCT
    negdelta = -np.abs(np.linspace(min_decay, max_decay, HY_W))[None, :]
    z = z.reshape(FN // FILT_TR, 2, FILT_TR // 2, FORD).transpose(0, 2, 1, 3).reshape(FN // 2, 2 * FORD)
    return tuple(np.asarray(a, np.float32) for a in (z, t, mask, fwd, negdelta))


@functools.lru_cache(maxsize=None)
def _rope_tables(n):
    rows = n // GRID_W
    row_id, col_id = np.meshgrid(np.arange(rows, dtype=np.float64), np.arange(GRID_W, dtype=np.float64), indexing="ij")
    quarter = HD // 4
    inv_freq = ROPE_THETA ** (-np.arange(quarter, dtype=np.float64) / quarter)
    ar = row_id.reshape(-1)[:, None] * inv_freq
    ac = col_id.reshape(-1)[:, None] * inv_freq
    cos = np.concatenate([np.cos(ar), np.cos(ar), np.cos(ac), np.cos(ac)], axis=-1)
    sin = np.concatenate([-np.sin(ar), np.sin(ar), -np.sin(ac), np.sin(ac)], axis=-1)
    reps = (1, LANES // HD)
    return np.tile(cos, reps).astype(np.float32), np.tile(sin, reps).astype(np.float32)


def _hyena_long_conv(u_rj, x2_rj, kern, abs_sum, bias):
    tb = _dft_tables()
    twr, twi = tb["twr"], tb["twi"]
    fwd = _np_split(tb["fwd"])
    scale = 1.0 / (abs_sum * float(FN))
    khat = _filter_spectrum(kern.reshape(FN2, FN1, HY_W), twr, twi, *_np_split(tb["lead_k"]), *fwd, scale)
    return _hyena_conv(u_rj, x2_rj, khat, twr, twi, _np_split(tb["lead_u"]), fwd, _np_split(tb["inv"]),
                       _np_split(tb["out"]), bias.reshape(1, HY_W))


def kernel(x, c, ctx, c_ctx, w_mod, b_mod, norm1_g, norm2_g, w_in, w_out, q_norm_g, k_norm_g,
           conv_w, conv_b, filt_w1, filt_b1, filt_w2, filt_b2, filt_w3, filt_freq, hyena_bias,
           w_router, w_gate, w_up, w_down):
    B = x.shape[0]
    assert x.shape == (B, SEQ, D) and B == 2 and ctx.shape == (B, CTX, D) and w_mod.shape[0] == 1
    l = 0

    cc = jnp.concatenate([c, c_ctx[None, :], jnp.zeros((SUBLANES - B - 1, D), F32)], axis=0)
    mod = _modulation(cc, w_mod[l], b_mod[l][None, :])
    sh1, sc1, g1, sh2, sc2, g2 = [mod[:, i * D:(i + 1) * D] for i in range(6)]
    lat = lambda m: m[:B, None, :]
    ctxrow = lambda m: jnp.broadcast_to(m[B:B + 1, None, :], (B, 1, D))

    w_in_bf = w_in[l].astype(BF16)
    gq2 = jnp.tile(q_norm_g[l][None, :], (1, LANES // HD))
    gk2 = jnp.tile(k_norm_g[l][None, :], (1, LANES // HD))
    bd = jnp.asarray(np.kron(np.eye(2 * LANES // HD), np.full((HD, HD), 1.0 / HD)), BF16)
    cos_t, sin_t = _rope_tables(SEQ)
    n1g = norm1_g[l][None, :]

    q, k, vt, p = _in_projection(x, n1g, lat(sh1), lat(sc1), w_in_bf, gq2, gk2, bd, cos_t, sin_t, 512)
    _, kc, vct, _ = _in_projection(ctx, n1g, ctxrow(sh1), ctxrow(sc1), w_in_bf, gq2, gk2, bd,
                                   jnp.ones((CTX, LANES), F32), jnp.zeros((CTX, LANES), F32), CTX)

    kch = jnp.concatenate([k, kc], axis=2).reshape(B, NKV, ATT_NCH, ATT_TK, HD)
    vt_all = jnp.concatenate([vt, vct], axis=3).reshape(B, NKV, HD, ATT_NCH, ATT_TK)
    ones_pad = jnp.concatenate([jnp.ones((B, NKV, ATT_NCH, 1, ATT_TK), BF16),
                                jnp.zeros((B, NKV, ATT_NCH, BF16_ROWS - 1, ATT_TK), BF16)], axis=3)
    vtch = jnp.concatenate([vt_all.transpose(0, 1, 3, 2, 4), ones_pad], axis=3)
    bound = (1.02 * HD * Q_SCALE) * jnp.max(jnp.abs(q_norm_g[l])) * jnp.max(jnp.abs(k_norm_g[l]))
    att = _attention(bound.reshape(1).astype(F32), q, kch, vtch)

    cw9 = conv_w[l].reshape(3, 3, HY_W).reshape(9, HY_W)
    cb3 = conv_b[l].reshape(3, HY_W)
    u_rj, x2_rj = _short_conv(p, cw9, cb3)
    ztab, ttab, mtab, ftab, negdelta = _filter_tables()
    w1p = jnp.concatenate([filt_w1[l], jnp.zeros((FORD - FEMB, FORD), F32)], axis=0)
    twice = lambda w: jnp.kron(jnp.eye(2, dtype=F32), w)
    pair = lambda v: jnp.tile(v[None, :], (1, 2))
    kern, abs_sum = _implicit_filter(ztab, ttab, mtab, ftab, twice(w1p), pair(filt_b1[l]), twice(filt_w2[l]),
                                     pair(filt_b2[l]), twice(filt_w3[l]), pair(filt_freq[l]), negdelta)
    hy = _hyena_long_conv(u_rj, x2_rj, kern, abs_sum, hyena_bias[l])

    wr2 = jnp.concatenate(_split(w_router[l].T), axis=0)
    x1, h2, logits = _out_projection(att, hy, x, w_out[l].astype(BF16), lat(g1), norm2_g[l][None, :],
                                     lat(sh2), lat(sc2), wr2)

    tri = jnp.asarray(np.triu(np.ones((TCH, TCH))), BF16)
    pos, gate, offs = _routing(logits, tri)
    offs_flat = offs.reshape(-1)
    xg = _gather(offs_flat, h2, pos.reshape(B, NE, NTCH, TCH))
    y = _expert_ffn(xg, jnp.swapaxes(w_gate[l], 1, 2), jnp.swapaxes(w_up[l], 1, 2), w_down[l])
    return _combine(offs_flat, y, pos, gate, x1, lat(g2))
```

```python
import functools
import math

import numpy as np
import jax
import jax.numpy as jnp
from jax import lax
from jax.experimental import pallas as pl
from jax.experimental.pallas import tpu as pltpu

F32 = jnp.float32
BF16 = jnp.bfloat16
I32 = jnp.int32

D = 1024
SEQ = 8192
CTX = 256
GRID_W = 64
ATT_W = 512
HY_W = 512
HD = 64
NQ = 8
NKV = 2
QPK = NQ // NKV
KV_W = NKV * HD
IN_W = ATT_W + 2 * KV_W + 3 * HY_W
FEMB = 33
FORD = 64
NE = 16
CAP = 2 * SEQ // NE
DEXP = 2752
ROPE_THETA = 10000.0
EPS = 1e-6
DECAY_TARGET = 1e-2
FAST_DECAY_PCT = 0.3
SLOW_DECAY_PCT = 1.5

LANES = 128
SUBLANES = 8
BF16_ROWS = 16
VMEM_BYTES_V7X = 64 * 1024 * 1024
VMEM_LIMIT = VMEM_BYTES_V7X - 8 * 1024 * 1024

FN = 2 * SEQ
FN1 = 128
FN2 = 128

TCH = LANES
NTCH = SEQ // TCH
GW = TCH + SUBLANES
CW = TCH + BF16_ROWS
YROWS = CAP + BF16_ROWS


def _cparams(sem, vmem=None):
    return pltpu.CompilerParams(dimension_semantics=sem, vmem_limit_bytes=vmem or VMEM_LIMIT)


def _split(a):
    hi = a.astype(BF16)
    lo = (a - hi.astype(F32)).astype(BF16)
    return hi, lo


_NN = (((1,), (0,)), ((), ()))
_NT = (((1,), (1,)), ((), ()))
_TN = (((0,), (0,)), ((), ()))


def _dot(a, b, dn=_NN):
    return lax.dot_general(a, b, dn, preferred_element_type=F32)


def _dot3(a, b, dn=_NN):
    ah, al = _split(a)
    bh, bl = _split(b)
    return _dot(ah, bh, dn) + _dot(ah, bl, dn) + _dot(al, bh, dn)


def _mod_body(c_ref, w_ref, b_ref, o_ref):
    c = c_ref[...]
    s = c * (1.0 / (1.0 + jnp.exp(-c)))
    o_ref[...] = _dot3(s, w_ref[...]) + b_ref[...]


def _modulation(cc, w_mod, b_mod):
    n = w_mod.shape[1]
    return pl.pallas_call(
        _mod_body,
        grid=(n // D,),
        in_specs=[pl.BlockSpec((SUBLANES, D), lambda j: (0, 0)),
                  pl.BlockSpec((D, D), lambda j: (0, j)),
                  pl.BlockSpec((1, D), lambda j: (0, j))],
        out_specs=pl.BlockSpec((SUBLANES, D), lambda j: (0, j)),
        out_shape=jax.ShapeDtypeStruct((SUBLANES, n), F32),
        compiler_params=_cparams(("arbitrary",)),
        name="modulation",
    )(cc, w_mod, b_mod)


Q_SCALE = HD ** -0.5 * math.log2(math.e)


def _rms_mod(x, g, sh, sc):
    ms = jnp.mean(x * x, axis=-1, keepdims=True)
    return (x * lax.rsqrt(ms + EPS) * g) * (1.0 + sc) + sh


def _head_mean_square(t, bd):
    hi, lo = _split(t * t)
    return _dot(hi, bd) + _dot(lo, bd)


def _head_norm_rope(t, ms, g, cos, sin):
    tn = t * lax.rsqrt(ms + EPS) * g
    lane = lax.broadcasted_iota(I32, tn.shape, 1)
    sw = jnp.where((lane & 31) < 16, pltpu.roll(tn, LANES - 16, 1), pltpu.roll(tn, 16, 1))
    return tn * cos + sw * sin


def _proj_body(x_ref, g_ref, sh_ref, sc_ref, w_ref, gq_ref, gk_ref, bd_ref, cos_ref, sin_ref,
               q_ref, k_ref, v_ref, p_ref):
    h = _rms_mod(x_ref[0], g_ref[...], sh_ref[0], sc_ref[0])
    proj = _dot(h.astype(BF16), w_ref[...])
    bd = bd_ref[...]
    cos = cos_ref[...]
    sin = sin_ref[...]
    wide = 2 * LANES
    for j in range(ATT_W // wide):
        ms = _head_mean_square(proj[:, j * wide:(j + 1) * wide], bd)
        for i in range(2):
            sl = slice(j * wide + i * LANES, j * wide + (i + 1) * LANES)
            qj = _head_norm_rope(proj[:, sl], ms[:, i * LANES:(i + 1) * LANES], gq_ref[...], cos, sin)
            q_ref[0, :, sl] = (qj * Q_SCALE).astype(BF16)
    ms = _head_mean_square(proj[:, ATT_W:ATT_W + 2 * KV_W], bd)
    kk = _head_norm_rope(proj[:, ATT_W:ATT_W + KV_W], ms[:, 0:KV_W], gk_ref[...], cos, sin)
    vt = proj[:, ATT_W + KV_W:ATT_W + 2 * KV_W].T
    for g in range(NKV):
        k_ref[0, g] = kk[:, g * HD:(g + 1) * HD].astype(BF16)
        v_ref[0, g] = vt[g * HD:(g + 1) * HD, :].astype(BF16)
    p_ref[0] = proj[:, ATT_W + 2 * KV_W:]


def _in_projection(x, g1, sh, sc, w_in_bf, gq2, gk2, bd, cos_t, sin_t, tm):
    b, s, _ = x.shape
    row = lambda bi, i: (bi, 0, 0)
    tok = lambda bi, i: (bi, i, 0)
    const = lambda bi, i: (0, 0)
    return pl.pallas_call(
        _proj_body,
        grid=(b, s // tm),
        in_specs=[pl.BlockSpec((1, tm, D), tok),
                  pl.BlockSpec((1, D), const),
                  pl.BlockSpec((1, 1, D), row),
                  pl.BlockSpec((1, 1, D), row),
                  pl.BlockSpec((D, IN_W), const),
                  pl.BlockSpec((1, LANES), const),
                  pl.BlockSpec((1, LANES), const),
                  pl.BlockSpec((2 * LANES, 2 * LANES), const),
                  pl.BlockSpec((tm, LANES), lambda bi, i: (i, 0)),
                  pl.BlockSpec((tm, LANES), lambda bi, i: (i, 0))],
        out_specs=[pl.BlockSpec((1, tm, ATT_W), tok),
                   pl.BlockSpec((1, NKV, tm, HD), lambda bi, i: (bi, 0, i, 0)),
                   pl.BlockSpec((1, NKV, HD, tm), lambda bi, i: (bi, 0, 0, i)),
                   pl.BlockSpec((1, tm, 3 * HY_W), tok)],
        out_shape=[jax.ShapeDtypeStruct((b, s, ATT_W), BF16),
                   jax.ShapeDtypeStruct((b, NKV, s, HD), BF16),
                   jax.ShapeDtypeStruct((b, NKV, HD, s), BF16),
                   jax.ShapeDtypeStruct((b, s, 3 * HY_W), F32)],
        compiler_params=_cparams(("parallel", "parallel")),
        name="in_projection",
    )(x, g1, sh, sc, w_in_bf, gq2, gk2, bd, cos_t, sin_t)


ATT_TQ = 512
ATT_TK = 768
SK = SEQ + CTX
ATT_NCH = SK // ATT_TK


ATT_NQ = QPK * ATT_TQ
ATT_VR = HD + BF16_ROWS
assert ATT_NCH % 2 == 1
ATT_SHIFT_MAX = 120.0


def _attn_body(bound_ref, q_ref, k_ref, vt_ref, o_ref, s_ref, mx_ref, m_ref, acc_ref):
    qall = jnp.concatenate([q_ref[0, :, r * HD:(r + 1) * HD] for r in range(QPK)], axis=0)
    acc_ref[...] = jnp.zeros_like(acc_ref)
    bound = bound_ref[0]
    fixed_shift = 2.0 * bound <= ATT_SHIFT_MAX

    def finish():
        out = acc_ref[0:HD, :] * (1.0 / acc_ref[HD:HD + 1, :])
        for r in range(QPK):
            o_ref[0, :, r * HD:(r + 1) * HD] = out[:, r * ATT_TQ:(r + 1) * ATT_TQ].T.astype(BF16)

    @pl.when(fixed_shift)
    def _():
        def chunk(c, _):
            s = _dot(k_ref[0, 0, c], qall, _NT)
            acc_ref[...] += _dot(vt_ref[0, 0, c], jnp.exp2(s - bound).astype(BF16))
            return 0

        lax.fori_loop(0, ATT_NCH, chunk, 0, unroll=True)
        finish()

    @pl.when(jnp.logical_not(fixed_shift))
    def _():
        m_ref[...] = jnp.full(m_ref.shape, -1e30, F32)

        def scores(c, slot):
            s = _dot(k_ref[0, 0, c], qall, _NT)
            s_ref[slot] = s
            mx_ref[slot] = jnp.max(s, axis=0, keepdims=True)

        def update(c, slot):
            m_old = m_ref[...]
            m_new = jnp.maximum(m_old, mx_ref[slot])
            p = jnp.exp2(s_ref[slot] - m_new).astype(BF16)
            acc_ref[...] = jnp.exp2(m_old - m_new) * acc_ref[...] + _dot(vt_ref[0, 0, c], p)
            m_ref[...] = m_new

        scores(0, 0)

        def pair(i, _):
            c = 2 * i
            scores(c + 1, 1)
            update(c, 0)
            scores(c + 2, 0)
            update(c + 1, 1)
            return 0

        lax.fori_loop(0, ATT_NCH // 2, pair, 0)
        update(ATT_NCH - 1, 0)
        finish()


def _attention(bound, q, kch, vtch):
    b = q.shape[0]
    grid_spec = pltpu.PrefetchScalarGridSpec(
        num_scalar_prefetch=1,
        grid=(b, NKV, SEQ // ATT_TQ),
        in_specs=[pl.BlockSpec((1, ATT_TQ, QPK * HD), lambda bi, g, i, bd: (bi, i, g)),
                  pl.BlockSpec((1, 1, ATT_NCH, ATT_TK, HD), lambda bi, g, i, bd: (bi, g, 0, 0, 0)),
                  pl.BlockSpec((1, 1, ATT_NCH, ATT_VR, ATT_TK), lambda bi, g, i, bd: (bi, g, 0, 0, 0))],
        out_specs=pl.BlockSpec((1, ATT_TQ, QPK * HD), lambda bi, g, i, bd: (bi, i, g)),
        scratch_shapes=[pltpu.VMEM((2, ATT_TK, ATT_NQ), F32), pltpu.VMEM((2, 1, ATT_NQ), F32),
                        pltpu.VMEM((1, ATT_NQ), F32), pltpu.VMEM((ATT_VR, ATT_NQ), F32)],
    )
    return pl.pallas_call(
        _attn_body,
        grid_spec=grid_spec,
        out_shape=jax.ShapeDtypeStruct((b, SEQ, ATT_W), BF16),
        compiler_params=_cparams(("parallel", "parallel", "parallel")),
        name="attention",
    )(bound, q, kch, vtch)


SC_TM = 2048
SC_J = SC_TM // FN2


def _sconv_body(m1, a1, n1, m2, a2, n2, m3, a3, n3, w_ref, b_ref, u_ref, x2_ref):
    i = pl.program_id(1)
    last = pl.num_programs(1) - 1
    rows = lax.broadcasted_iota(I32, (SC_TM, HY_W), 0)

    def conv(main, prev, nxt, g):
        x = main[0]
        pr = jnp.where(i > 0, prev[0, SUBLANES - 1:SUBLANES, :], 0.0)
        nx = jnp.where(i < last, nxt[0, 0:1, :], 0.0)
        xm = jnp.where(rows == 0, pr, pltpu.roll(x, 1, 0))
        xp = jnp.where(rows == SC_TM - 1, nx, pltpu.roll(x, SC_TM - 1, 0))
        return (w_ref[g:g + 1, :] * xm + w_ref[3 + g:4 + g, :] * x + w_ref[6 + g:7 + g, :] * xp
                + b_ref[g:g + 1, :])

    def to_rj(t):
        return jnp.swapaxes(t.reshape(SC_J, FN2, HY_W), 0, 1).astype(BF16)

    x1 = conv(m1, a1, n1, 0)
    x2 = conv(m2, a2, n2, 1)
    v = conv(m3, a3, n3, 2)
    u_ref[0] = to_rj(v * x1)
    x2_ref[0] = to_rj(x2)


def _short_conv(p, cw9, cb3):
    b = p.shape[0]
    nblk8 = SEQ // SUBLANES
    step8 = SC_TM // SUBLANES
    specs = []
    for g in range(3):
        specs += [pl.BlockSpec((1, SC_TM, HY_W), lambda bi, i, g=g: (bi, i, g)),
                  pl.BlockSpec((1, SUBLANES, HY_W), lambda bi, i, g=g: (bi, jnp.maximum(i * step8 - 1, 0), g)),
                  pl.BlockSpec((1, SUBLANES, HY_W), lambda bi, i, g=g: (bi, jnp.minimum((i + 1) * step8, nblk8 - 1), g))]
    specs += [pl.BlockSpec((9, HY_W), lambda bi, i: (0, 0)), pl.BlockSpec((3, HY_W), lambda bi, i: (0, 0))]
    out = pl.BlockSpec((1, FN2, SC_J, HY_W), lambda bi, i: (bi, 0, i, 0))
    return pl.pallas_call(
        _sconv_body,
        grid=(b, SEQ // SC_TM),
        in_specs=specs,
        out_specs=[out, out],
        out_shape=[jax.ShapeDtypeStruct((b, FN2, SEQ // FN2, HY_W), BF16)] * 2,
        compiler_params=_cparams(("parallel", "parallel")),
        name="short_conv",
    )(p, p, p, p, p, p, p, p, p, cw9, cb3)


FILT_TR = 1024


def _filter_body(z_ref, t_ref, msk_ref, fwd_ref, w1_ref, b1_ref, w2_ref, b2_ref, w3_ref, fr_ref, dl_ref,
                 k_ref, s_ref):
    fr = fr_ref[...]
    h = jnp.sin(fr * (_dot3(z_ref[...], w1_ref[...]) + b1_ref[...]))
    h = jnp.sin(fr * (_dot3(h, w2_ref[...]) + b2_ref[...]))
    h = _dot3(h, w3_ref[...])
    h = jnp.concatenate([h[:, :2 * HY_W], h[:, 2 * HY_W:]], axis=0)
    h = jnp.where(fwd_ref[...] > 0.5, h[:, :HY_W], h[:, HY_W:])
    kern = h * jnp.exp(t_ref[...] * dl_ref[...]) * msk_ref[...]
    k_ref[...] = kern

    @pl.when(pl.program_id(0) == 0)
    def _():
        s_ref[...] = jnp.zeros_like(s_ref)

    s_ref[...] += jnp.sum(jnp.abs(kern), axis=0, keepdims=True)


def _implicit_filter(ztab, ttab, mtab, ftab, w1p, b1, w2, b2, w3, freq, negdelta):
    rowblk = lambda i: (i, 0)
    const = lambda i: (0, 0)
    col = pl.BlockSpec((FILT_TR, 1), rowblk)
    return pl.pallas_call(
        _filter_body,
        grid=(FN // FILT_TR,),
        in_specs=[pl.BlockSpec((FILT_TR // 2, 2 * FORD), rowblk), col, col, col,
                  pl.BlockSpec((2 * FORD, 2 * FORD), const),
                  pl.BlockSpec((1, 2 * FORD), const),
                  pl.BlockSpec((2 * FORD, 2 * FORD), const),
                  pl.BlockSpec((1, 2 * FORD), const),
                  pl.BlockSpec((2 * FORD, 4 * HY_W), const),
                  pl.BlockSpec((1, 2 * FORD), const),
                  pl.BlockSpec((1, HY_W), const)],
        out_specs=[pl.BlockSpec((FILT_TR, HY_W), rowblk),
                   pl.BlockSpec((1, HY_W), const)],
        out_shape=[jax.ShapeDtypeStruct((FN, HY_W), F32), jax.ShapeDtypeStruct((1, HY_W), F32)],
        compiler_params=_cparams(("arbitrary",)),
        name="implicit_filter",
    )(ztab, ttab, mtab, ftab, w1p, b1, w2, b2, w3, freq, negdelta)


DFT_G = 4
DFT_KB = 16
DFT_NP = FN1 // DFT_KB
DFT_UNROLL = 8


def _dot2c(fh, fl, zb):
    return _dot(fh, zb) + _dot(fl, zb)


def _lead_stage(src, fh, fl, dst_ref):
    def group(rg, _):
        r0 = rg * DFT_G
        rhs = jnp.concatenate([src(r0 + g) for g in range(DFT_G)], axis=1)
        blk = _dot2c(fh, fl, rhs)
        for g in range(DFT_G):
            dst_ref[r0 + g] = blk[:, g * LANES:(g + 1) * LANES]
        return 0

    lax.fori_loop(0, FN2 // DFT_G, group, 0, unroll=DFT_UNROLL)


def _lead_phase(src, lh_ref, ll_ref, p_ref, q_ref):
    for h in range(2):
        rows = slice(h * FN1, (h + 1) * FN1)
        _lead_stage(src, lh_ref[rows, :], ll_ref[rows, :], p_ref)
        q_ref[rows] = jnp.swapaxes(p_ref[...], 0, 1)


def _lane_cat(xs):
    return jnp.concatenate(xs, axis=1)


def _mid_forward(q_ref, k0, tr_ref, ti_ref, fh, fl, half):
    tr_t, ti_t = tr_ref[...].T, ti_ref[...].T
    brs, bis, trs, tis = [], [], [], []
    for g in range(DFT_G):
        jj = half * DFT_G + g
        ar, ai = q_ref[k0 + jj], q_ref[FN1 + k0 + jj]
        tr = jnp.broadcast_to(tr_t[:, jj:jj + 1], (FN2, LANES))
        ti = jnp.broadcast_to(ti_t[:, jj:jj + 1], (FN2, LANES))
        brs.append(ar * tr - ai * ti)
        bis.append(ar * ti + ai * tr)
        trs.append(tr)
        tis.append(ti)
    b = jnp.concatenate([_lane_cat(brs), _lane_cat(bis)], axis=0).astype(BF16)
    return _dot2c(fh, fl, b), _lane_cat(trs), _lane_cat(tis)


def _spectrum_body(k_ref, tr_ref, ti_ref, lh_ref, ll_ref, fh_ref, fl_ref, sc_ref, o_ref, p_ref, q_ref):
    ph = pl.program_id(1)

    @pl.when(ph == 0)
    def _():
        _lead_phase(lambda r: k_ref[r].astype(BF16), lh_ref, ll_ref, p_ref, q_ref)

    @pl.when(ph > 0)
    def _():
        k0 = (ph - 1) * DFT_KB
        sc = _lane_cat([sc_ref[...]] * DFT_G)
        for half in range(DFT_KB // DFT_G):
            x, _, _ = _mid_forward(q_ref, k0, tr_ref, ti_ref, fh_ref[...], fl_ref[...], half)
            x = x * sc
            for g in range(DFT_G):
                lanes = slice(g * LANES, (g + 1) * LANES)
                o_ref[0, half * DFT_G + g] = x[:FN2, lanes]
                o_ref[1, half * DFT_G + g] = x[FN2:, lanes]


def _mid_index(ph):
    return jnp.clip(ph - 1, 0, DFT_NP - 1)


def _filter_spectrum(kern_rj, twr, twi, lh, ll, fh, fl, scale):
    const = lambda c, ph: (0, 0)
    tw = pl.BlockSpec((DFT_KB, FN2), lambda c, ph: (_mid_index(ph), 0))
    return pl.pallas_call(
        _spectrum_body,
        grid=(HY_W // LANES, DFT_NP + 1),
        in_specs=[pl.BlockSpec((FN2, FN1, LANES), lambda c, ph: (0, 0, c)),
                  tw, tw,
                  pl.BlockSpec(lh.shape, const), pl.BlockSpec(ll.shape, const),
                  pl.BlockSpec(fh.shape, const), pl.BlockSpec(fl.shape, const),
                  pl.BlockSpec((1, LANES), lambda c, ph: (0, c))],
        out_specs=pl.BlockSpec((2, DFT_KB, FN2, LANES), lambda c, ph: (0, _mid_index(ph), 0, c)),
        out_shape=jax.ShapeDtypeStruct((2, FN1, FN2, HY_W), F32),
        scratch_shapes=[pltpu.VMEM((FN2, FN1, LANES), F32), pltpu.VMEM((2 * FN1, FN2, LANES), F32)],
        compiler_params=_cparams(("parallel", "arbitrary")),
        name="filter_spectrum",
    )(kern_rj, twr, twi, lh, ll, fh, fl, scale)


def _hconv_body(u_ref, x2_ref, kh_ref, tr_ref, ti_ref, lh_ref, ll_ref, fh_ref, fl_ref, gh_ref, gl_ref,
                oh_ref, ol_ref, bias_ref, o_ref, p_ref, q_ref):
    ph = pl.program_id(1)

    def both(ref, r):
        return jnp.concatenate([ref[0, r], ref[1, r]], axis=0)

    @pl.when(ph == 0)
    def _():
        _lead_phase(lambda r: both(u_ref, r), lh_ref, ll_ref, p_ref, q_ref)

    @pl.when((ph > 0) & (ph <= DFT_NP))
    def _():
        k0 = (ph - 1) * DFT_KB
        for half in range(DFT_KB // DFT_G):
            x, tr, ti = _mid_forward(q_ref, k0, tr_ref, ti_ref, fh_ref[...], fl_ref[...], half)
            xr, xi = x[:FN2], x[FN2:]
            kr = _lane_cat([kh_ref[0, half * DFT_G + g] for g in range(DFT_G)])
            ki = _lane_cat([kh_ref[1, half * DFT_G + g] for g in range(DFT_G)])
            y = jnp.concatenate([xr * kr - xi * ki, xr * ki + xi * kr], axis=0).astype(BF16)
            c = _dot2c(gh_ref[...], gl_ref[...], y)
            cr, ci = c[:FN2], c[FN2:]
            dr = cr * tr + ci * ti
            di = ci * tr - cr * ti
            for g in range(DFT_G):
                lanes = slice(g * LANES, (g + 1) * LANES)
                q_ref[k0 + half * DFT_G + g] = dr[:, lanes]
                q_ref[FN1 + k0 + half * DFT_G + g] = di[:, lanes]

    @pl.when(ph == DFT_NP + 1)
    def _():
        bias = bias_ref[...]
        p_ref[...] = jnp.swapaxes(q_ref[0:FN1], 0, 1)
        _lead_stage(lambda r: p_ref[r].astype(BF16), oh_ref[:, 0:FN1], ol_ref[:, 0:FN1], q_ref)
        p_ref[...] = jnp.swapaxes(q_ref[FN1:2 * FN1], 0, 1)
        oh2, ol2 = oh_ref[:, FN1:2 * FN1], ol_ref[:, FN1:2 * FN1]

        def group(rg, _):
            r0 = rg * DFT_G
            rhs = _lane_cat([p_ref[r0 + g].astype(BF16) for g in range(DFT_G)])
            blk = _dot2c(oh2, ol2, rhs)
            for g in range(DFT_G):
                r = r0 + g
                y = q_ref[r] + blk[:, g * LANES:(g + 1) * LANES]
                q_ref[r] = (y + both(u_ref, r).astype(F32) * bias) * both(x2_ref, r).astype(F32)
            return 0

        lax.fori_loop(0, FN2 // DFT_G, group, 0, unroll=DFT_UNROLL)
        p_ref[...] = jnp.swapaxes(q_ref[0:FN2], 0, 1)
        nj = SEQ // FN2
        for b in range(2):
            o_ref[b] = p_ref[b * nj:(b + 1) * nj].reshape(SEQ, LANES).astype(BF16)


def _hyena_conv(u_rj, x2_rj, khat, twr, twi, lead, fwd, inv, out, bias):
    const = lambda c, ph: (0, 0)
    nj = SEQ // FN2
    tw = pl.BlockSpec((DFT_KB, FN2), lambda c, ph: (_mid_index(ph), 0))
    sig = pl.BlockSpec((2, FN2, nj, LANES), lambda c, ph: (0, 0, 0, c))
    mats = [m for pair in (lead, fwd, inv, out) for m in pair]
    return pl.pallas_call(
        _hconv_body,
        grid=(HY_W // LANES, DFT_NP + 2),
        in_specs=[sig, sig,
                  pl.BlockSpec((2, DFT_KB, FN2, LANES), lambda c, ph: (0, _mid_index(ph), 0, c)),
                  tw, tw] + [pl.BlockSpec(m.shape, const) for m in mats]
                 + [pl.BlockSpec((1, LANES), lambda c, ph: (0, c))],
        out_specs=pl.BlockSpec((2, SEQ, LANES), lambda c, ph: (0, 0, c)),
        out_shape=jax.ShapeDtypeStruct((2, SEQ, HY_W), BF16),
        scratch_shapes=[pltpu.VMEM((FN2, FN1, LANES), F32), pltpu.VMEM((2 * FN1, FN2, LANES), F32)],
        compiler_params=_cparams(("parallel", "arbitrary")),
        name="hyena_conv",
    )(u_rj, x2_rj, khat, twr, twi, *mats, bias)


OP_TM = 1024


def _outproj_body(att_ref, hy_ref, x_ref, w_ref, g1_ref, n2_ref, sh_ref, sc_ref, wr_ref,
                  x1_ref, h2_ref, lg_ref):
    a = jnp.concatenate([att_ref[0], hy_ref[0]], axis=1)
    x1 = x_ref[0] + g1_ref[0] * _dot(a, w_ref[...])
    x1_ref[0] = x1
    h2 = _rms_mod(x1, n2_ref[...], sh_ref[0], sc_ref[0])
    hh, hl = _split(h2)
    h2_ref[0] = hh
    wr = wr_ref[...]
    both = _dot(wr, hh, _NT)
    lg_ref[0] = both[0:NE] + both[NE:2 * NE] + _dot(wr[0:NE], hl, _NT)


def _out_projection(att, hy, x, w_out_bf, g1r, n2g, sh2, sc2, wr2):
    b = x.shape[0]
    tok = lambda bi, i: (bi, i, 0)
    row = lambda bi, i: (bi, 0, 0)
    const = lambda bi, i: (0, 0)
    return pl.pallas_call(
        _outproj_body,
        grid=(b, SEQ // OP_TM),
        in_specs=[pl.BlockSpec((1, OP_TM, ATT_W), tok),
                  pl.BlockSpec((1, OP_TM, HY_W), tok),
                  pl.BlockSpec((1, OP_TM, D), tok),
                  pl.BlockSpec((ATT_W + HY_W, D), const),
                  pl.BlockSpec((1, 1, D), row),
                  pl.BlockSpec((1, D), const),
                  pl.BlockSpec((1, 1, D), row),
                  pl.BlockSpec((1, 1, D), row),
                  pl.BlockSpec((2 * NE, D), const)],
        out_specs=[pl.BlockSpec((1, OP_TM, D), tok),
                   pl.BlockSpec((1, OP_TM, D), tok),
                   pl.BlockSpec((1, NE, OP_TM), lambda bi, i: (bi, 0, i))],
        out_shape=[jax.ShapeDtypeStruct((b, SEQ, D), F32),
                   jax.ShapeDtypeStruct((b, SEQ, D), BF16),
                   jax.ShapeDtypeStruct((b, NE, SEQ), F32)],
        compiler_params=_cparams(("parallel", "parallel")),
        name="out_projection",
    )(att, hy, x, w_out_bf, g1r, n2g, sh2, sc2, wr2)


def _routing_body(lg_ref, tri_ref, pos_ref, gate_ref, off_ref, cs_ref):
    lg = lg_ref[0]
    e = jnp.exp(lg - jnp.max(lg, axis=0, keepdims=True))
    aff = e / jnp.sum(e, axis=0, keepdims=True)
    gate_ref[0] = aff
    def count_ge(t):
        return jnp.sum(jnp.where(aff >= t, 1.0, 0.0), axis=1, keepdims=True)

    def bisect(i, thr):
        cand = thr | (jnp.int32(1) << (30 - i))
        return jnp.where(count_ge(pltpu.bitcast(cand, F32)) >= float(CAP), cand, thr)

    thr = lax.fori_loop(0, 31, bisect, jnp.zeros((NE, 1), I32))
    lo = pltpu.bitcast(thr, F32)
    hi = jnp.maximum(pltpu.bitcast(thr + 1, F32), jnp.finfo(F32).tiny)

    def refine(i, c):
        lo, hi = c
        mid = lo + (hi - lo) * 0.5
        ok = count_ge(mid) >= float(CAP)
        return jnp.where(ok, mid, lo), jnp.where(ok, hi, mid)

    lo, hi = lax.fori_loop(0, 32, refine, (lo, hi))
    gt = aff >= hi
    eq = (aff >= lo) & jnp.logical_not(gt)
    need = float(CAP) - jnp.sum(jnp.where(gt, 1.0, 0.0), axis=1, keepdims=True)
    tri = tri_ref[...]

    def excl_cumsum(mask_f, record_offsets):
        carry = jnp.zeros((NE, 1), F32)
        for c in range(NTCH):
            sl = slice(c * TCH, (c + 1) * TCH)
            m = mask_f[:, sl]
            inc = _dot(m.astype(BF16), tri)
            cs_ref[:, sl] = inc - m + carry
            if record_offsets:
                off_ref[0, :, c:c + 1] = carry.astype(I32)
            carry = carry + inc[:, TCH - 1:TCH]
        return cs_ref[...]

    eq_rank = excl_cumsum(jnp.where(eq, 1.0, 0.0), False)
    sel = gt | (eq & (eq_rank < need))
    pos = excl_cumsum(jnp.where(sel, 1.0, 0.0), True)
    pos_ref[0] = jnp.where(sel, pos.astype(I32), -1)


def _routing(logits, tri):
    b = logits.shape[0]
    blk = pl.BlockSpec((1, NE, SEQ), lambda bi: (bi, 0, 0))
    return pl.pallas_call(
        _routing_body,
        grid=(b,),
        in_specs=[blk, pl.BlockSpec((TCH, TCH), lambda bi: (0, 0))],
        out_specs=[blk, blk, pl.BlockSpec((1, NE, NTCH), lambda bi: (bi, 0, 0))],
        out_shape=[jax.ShapeDtypeStruct((b, NE, SEQ), I32),
                   jax.ShapeDtypeStruct((b, NE, SEQ), F32),
                   jax.ShapeDtypeStruct((b, NE, NTCH), I32)],
        scratch_shapes=[pltpu.VMEM((NE, SEQ), F32)],
        compiler_params=_cparams(("parallel",)),
        name="routing",
    )(logits, tri)


GATHER_UNROLL = 8


GW_SMALL = 64


def _gather_body(off_ref, h_ref, pos_ref, xg_ref, acc_ref):
    b = pl.program_id(0)
    e = pl.program_id(1)
    row0 = (b * NE + e) * NTCH
    acc_ref[...] = jnp.zeros_like(acc_ref)

    def count(c2, most):
        nxt = jnp.where(c2 + 1 < NTCH // 2, off_ref[row0 + jnp.minimum(2 * c2 + 2, NTCH - 1)], CAP)
        return jnp.maximum(most, nxt - off_ref[row0 + 2 * c2])

    most = lax.fori_loop(0, NTCH // 2, count, 0)

    def sweep(window, span):
        crow = lax.broadcasted_iota(I32, (window, span * TCH), 0)

        def chunks(i, _):
            for j in range(GATHER_UNROLL):
                c = (i * GATHER_UNROLL + j) * span
                off = off_ref[row0 + c]
                base = pl.multiple_of(jnp.minimum((off >> 3) << 3, CAP + SUBLANES - window), SUBLANES)
                t0 = pl.multiple_of(c * TCH, TCH)
                rel = _lane_cat([pos_ref[0, 0, pl.ds(c + s, 1), :] for s in range(span)]) - base
                onehot = jnp.where(crow == rel, 1.0, 0.0).astype(BF16)
                acc_ref[pl.ds(base, window), :] += _dot(onehot, h_ref[0, pl.ds(t0, span * TCH), :])
            return 0

        lax.fori_loop(0, NTCH // (GATHER_UNROLL * span), chunks, 0)

    @pl.when(most <= GW_SMALL - SUBLANES)
    def _():
        sweep(GW_SMALL, 2)

    @pl.when(most > GW_SMALL - SUBLANES)
    def _():
        sweep(GW, 1)

    xg_ref[0, 0] = acc_ref[0:CAP, :].astype(BF16)


def _gather(offs_flat, h2, pos4):
    b = h2.shape[0]
    grid_spec = pltpu.PrefetchScalarGridSpec(
        num_scalar_prefetch=1,
        grid=(b, NE),
        in_specs=[pl.BlockSpec((1, SEQ, D), lambda bi, e, off: (bi, 0, 0)),
                  pl.BlockSpec((1, 1, NTCH, TCH), lambda bi, e, off: (bi, e, 0, 0))],
        out_specs=pl.BlockSpec((1, 1, CAP, D), lambda bi, e, off: (bi, e, 0, 0)),
        scratch_shapes=[pltpu.VMEM((CAP + SUBLANES, D), F32)],
    )
    return pl.pallas_call(
        _gather_body,
        grid_spec=grid_spec,
        out_shape=jax.ShapeDtypeStruct((b, NE, CAP, D), BF16),
        compiler_params=_cparams(("parallel", "arbitrary")),
        name="moe_gather",
    )(offs_flat, h2, pos4)


FFN_TM = 512
FFN_NF = 4
FFN_FC = DEXP // FFN_NF
assert FFN_FC * FFN_NF == DEXP and FFN_FC % BF16_ROWS == 0


def _ffn_body(xg_ref, wgt_ref, wut_ref, wd_ref, y_ref, acc_ref):
    j = pl.program_id(1)
    nb = xg_ref.shape[0]

    @pl.when(j == 0)
    def _():
        acc_ref[...] = jnp.zeros_like(acc_ref)

    wgt = wgt_ref[0].astype(BF16)
    wut = wut_ref[0].astype(BF16)
    wd = wd_ref[0].astype(BF16)
    for b in range(nb):
        for mb in range(CAP // FFN_TM):
            rows = slice(mb * FFN_TM, (mb + 1) * FFN_TM)
            xb = xg_ref[b, 0, rows, :]
            a = _dot(xb, wgt, _NT)
            u = _dot(xb, wut, _NT)
            h = (a * (1.0 / (1.0 + jnp.exp(-a))) * u).astype(BF16)
            acc_ref[b, rows, :] += _dot(h, wd)

    @pl.when(j == FFN_NF - 1)
    def _():
        for b in range(nb):
            y_ref[b, 0, 0:CAP, :] = acc_ref[b].astype(BF16)
            y_ref[b, 0, CAP:YROWS, :] = jnp.zeros((YROWS - CAP, D), BF16)


def _expert_ffn(xg, w_gate_t, w_up_t, w_down):
    b = xg.shape[0]
    wblk = pl.BlockSpec((1, FFN_FC, D), lambda e, j: (e, j, 0))
    return pl.pallas_call(
        _ffn_body,
        grid=(NE, FFN_NF),
        in_specs=[pl.BlockSpec((b, 1, CAP, D), lambda e, j: (0, e, 0, 0)), wblk, wblk, wblk],
        out_specs=pl.BlockSpec((b, 1, YROWS, D), lambda e, j: (0, e, 0, 0)),
        out_shape=jax.ShapeDtypeStruct((b, NE, YROWS, D), BF16),
        scratch_shapes=[pltpu.VMEM((b, CAP, D), F32)],
        compiler_params=_cparams(("parallel", "arbitrary")),
        name="expert_ffn",
    )(xg, w_gate_t, w_up_t, w_down)


CW_SMALL = 64
CW_STACK = 2 * LANES // CW_SMALL


CMB_T = 4


def _combine_body(off_ref, y_ref, pos_ref, gate_ref, x1_ref, g2_ref, o_ref):
    b = pl.program_id(0)
    i = pl.program_id(1)

    def offset(e, c):
        return off_ref[(b * NE + e) * NTCH + c]

    def window(e, t, rows):
        lanes = slice(t * TCH, (t + 1) * TCH)
        base = pl.multiple_of(jnp.minimum((offset(e, i * CMB_T + t) >> 4) << 4, YROWS - rows), BF16_ROWS)
        rel = pos_ref[0, e:e + 1, lanes] - base
        crow = lax.broadcasted_iota(I32, (rows, TCH), 0)
        w = jnp.where(crow == rel, gate_ref[0, e:e + 1, lanes], 0.0).astype(BF16)
        return w, y_ref[0, e, pl.ds(base, rows), :]

    most = jnp.int32(0)
    for t in range(CMB_T):
        c = i * CMB_T + t
        for e in range(NE):
            nxt = jnp.where(c + 1 < NTCH, offset(e, jnp.minimum(c + 1, NTCH - 1)), CAP)
            most = jnp.maximum(most, nxt - offset(e, c))

    def finish(t, acc):
        rows = slice(t * TCH, (t + 1) * TCH)
        o_ref[0, rows, :] = x1_ref[0, rows, :] + g2_ref[0] * acc

    @pl.when(most <= CW_SMALL - BF16_ROWS)
    def _():
        for t in range(CMB_T):
            acc = jnp.zeros((TCH, D), F32)
            for e0 in range(0, NE, CW_STACK):
                ws, ys = zip(*[window(e, t, CW_SMALL) for e in range(e0, e0 + CW_STACK)])
                acc = acc + _dot(jnp.concatenate(ws, axis=0), jnp.concatenate(ys, axis=0), _TN)
            finish(t, acc)

    @pl.when(most > CW_SMALL - BF16_ROWS)
    def _():
        for t in range(CMB_T):
            acc = jnp.zeros((TCH, D), F32)
            for e in range(NE):
                w, yw = window(e, t, CW)
                acc = acc + _dot(w, yw, _TN)
            finish(t, acc)


def _combine(offs_flat, y, pos, gate, x1, g2r):
    b = x1.shape[0]
    grid_spec = pltpu.PrefetchScalarGridSpec(
        num_scalar_prefetch=1,
        grid=(b, NTCH // CMB_T),
        in_specs=[pl.BlockSpec((1, NE, YROWS, D), lambda bi, i, off: (bi, 0, 0, 0),
                               pipeline_mode=pl.Buffered(1)),
                  pl.BlockSpec((1, NE, CMB_T * TCH), lambda bi, i, off: (bi, 0, i)),
                  pl.BlockSpec((1, NE, CMB_T * TCH), lambda bi, i, off: (bi, 0, i)),
                  pl.BlockSpec((1, CMB_T * TCH, D), lambda bi, i, off: (bi, i, 0)),
                  pl.BlockSpec((1, 1, D), lambda bi, i, off: (bi, 0, 0))],
        out_specs=pl.BlockSpec((1, CMB_T * TCH, D), lambda bi, i, off: (bi, i, 0)),
    )
    return pl.pallas_call(
        _combine_body,
        grid_spec=grid_spec,
        out_shape=jax.ShapeDtypeStruct((b, SEQ, D), F32),
        compiler_params=_cparams(("parallel", "arbitrary")),
        name="moe_combine",
    )(offs_flat, y, pos, gate, x1, g2r)


def _np_split(m):
    hi = np.asarray(m, np.float64).astype(BF16)
    lo = (m - hi.astype(np.float64)).astype(BF16)
    return jnp.asarray(hi), jnp.asarray(lo)


@functools.lru_cache(maxsize=None)
def _dft_tables():
    a = np.arange(FN1, dtype=np.float64)
    ang = 2.0 * np.pi * np.outer(a, a) / FN1
    fr, fi = np.cos(ang), -np.sin(ang)
    half = SEQ // FN2
    lead_u = np.block([[fr[:, :half], -fi[:, :half]], [fi[:, :half], fr[:, :half]]])
    lead_k = np.concatenate([fr, fi], axis=0)
    fwd = np.block([[fr, -fi], [fi, fr]])
    inv = np.block([[fr, fi], [-fi, fr]])
    out = np.block([[fr[:half], fi[:half]], [-fi[:half], fr[:half]]])
    n2 = np.arange(FN2, dtype=np.float64)
    tw = 2.0 * np.pi * np.outer(a, n2) / FN
    twr = np.cos(tw).astype(np.float32)
    twi = (-np.sin(tw)).astype(np.float32)
    return dict(lead_u=lead_u, lead_k=lead_k, fwd=fwd, inv=inv, out=out, twr=twr, twi=twi)


@functools.lru_cache(maxsize=None)
def _filter_tables():
    L = SEQ
    n = np.arange(FN).reshape(FN1, FN2).T.reshape(-1)
    lag = np.where(n < L, n, FN - n)
    jc = np.minimum(lag, L - 1).astype(np.float64)
    t = (jc / (L - 1))[:, None]
    bands = (FEMB - 1) // 2
    w = 2.0 * np.pi * jc / L
    f = np.linspace(1e-4, bands - 1, bands)
    fw = w[:, None] * f[None, :]
    z = np.concatenate([t, np.cos(fw), -np.sin(fw), np.zeros((FN, FORD - FEMB))], axis=-1)
    mask = np.where(n == L, 0.0, 1.0)[:, None]
    fwd = np.where(n < L, 1.0, 0.0)[:, None]
    max_decay = math.log(DECAY_TARGET) / FAST_DECAY_PCT
    min_decay = math.log(DECAY_TARGET) / SLOW_DECAY_PCT
    negdelta = -np.abs(np.linspace(min_decay, max_decay, HY_W))[None, :]
    z = z.reshape(FN // FILT_TR, 2, FILT_TR // 2, FORD).transpose(0, 2, 1, 3).reshape(FN // 2, 2 * FORD)
    return tuple(np.asarray(a, np.float32) for a in (z, t, mask, fwd, negdelta))


@functools.lru_cache(maxsize=None)
def _rope_tables(n):
    rows = n // GRID_W
    row_id, col_id = np.meshgrid(np.arange(rows, dtype=np.float64), np.arange(GRID_W, dtype=np.float64), indexing="ij")
    quarter = HD // 4
    inv_freq = ROPE_THETA ** (-np.arange(quarter, dtype=np.float64) / quarter)
    ar = row_id.reshape(-1)[:, None] * inv_freq
    ac = col_id.reshape(-1)[:, None] * inv_freq
    cos = np.concatenate([np.cos(ar), np.cos(ar), np.cos(ac), np.cos(ac)], axis=-1)
    sin = np.concatenate([-np.sin(ar), np.sin(ar), -np.sin(ac), np.sin(ac)], axis=-1)
    reps = (1, LANES // HD)
    return np.tile(cos, reps).astype(np.float32), np.tile(sin, reps).astype(np.float32)


def _hyena_long_conv(u_rj, x2_rj, kern, abs_sum, bias):
    tb = _dft_tables()
    twr, twi = tb["twr"], tb["twi"]
    fwd = _np_split(tb["fwd"])
    scale = 1.0 / (abs_sum * float(FN))
    khat = _filter_spectrum(kern.reshape(FN2, FN1, HY_W), twr, twi, *_np_split(tb["lead_k"]), *fwd, scale)
    return _hyena_conv(u_rj, x2_rj, khat, twr, twi, _np_split(tb["lead_u"]), fwd, _np_split(tb["inv"]),
                       _np_split(tb["out"]), bias.reshape(1, HY_W))


def kernel(x, c, ctx, c_ctx, w_mod, b_mod, norm1_g, norm2_g, w_in, w_out, q_norm_g, k_norm_g,
           conv_w, conv_b, filt_w1, filt_b1, filt_w2, filt_b2, filt_w3, filt_freq, hyena_bias,
           w_router, w_gate, w_up, w_down):
    B = x.shape[0]
    assert x.shape == (B, SEQ, D) and B == 2 and ctx.shape == (B, CTX, D) and w_mod.shape[0] == 1
    l = 0

    cc = jnp.concatenate([c, c_ctx[None, :], jnp.zeros((SUBLANES - B - 1, D), F32)], axis=0)
    mod = _modulation(cc, w_mod[l], b_mod[l][None, :])
    sh1, sc1, g1, sh2, sc2, g2 = [mod[:, i * D:(i + 1) * D] for i in range(6)]
    lat = lambda m: m[:B, None, :]
    ctxrow = lambda m: jnp.broadcast_to(m[B:B + 1, None, :], (B, 1, D))

    w_in_bf = w_in[l].astype(BF16)
    gq2 = jnp.tile(q_norm_g[l][None, :], (1, LANES // HD))
    gk2 = jnp.tile(k_norm_g[l][None, :], (1, LANES // HD))
    bd = jnp.asarray(np.kron(np.eye(2 * LANES // HD), np.full((HD, HD), 1.0 / HD)), BF16)
    cos_t, sin_t = _rope_tables(SEQ)
    n1g = norm1_g[l][None, :]

    q, k, vt, p = _in_projection(x, n1g, lat(sh1), lat(sc1), w_in_bf, gq2, gk2, bd, cos_t, sin_t, 512)
    _, kc, vct, _ = _in_projection(ctx, n1g, ctxrow(sh1), ctxrow(sc1), w_in_bf, gq2, gk2, bd,
                                   jnp.ones((CTX, LANES), F32), jnp.zeros((CTX, LANES), F32), CTX)

    kch = jnp.concatenate([k, kc], axis=2).reshape(B, NKV, ATT_NCH, ATT_TK, HD)
    vt_all = jnp.concatenate([vt, vct], axis=3).reshape(B, NKV, HD, ATT_NCH, ATT_TK)
    ones_pad = jnp.concatenate([jnp.ones((B, NKV, ATT_NCH, 1, ATT_TK), BF16),
                                jnp.zeros((B, NKV, ATT_NCH, BF16_ROWS - 1, ATT_TK), BF16)], axis=3)
    vtch = jnp.concatenate([vt_all.transpose(0, 1, 3, 2, 4), ones_pad], axis=3)
    bound = (1.02 * HD * Q_SCALE) * jnp.max(jnp.abs(q_norm_g[l])) * jnp.max(jnp.abs(k_norm_g[l]))
    att = _attention(bound.reshape(1).astype(F32), q, kch, vtch)

    cw9 = conv_w[l].reshape(3, 3, HY_W).reshape(9, HY_W)
    cb3 = conv_b[l].reshape(3, HY_W)
    u_rj, x2_rj = _short_conv(p, cw9, cb3)
    ztab, ttab, mtab, ftab, negdelta = _filter_tables()
    w1p = jnp.concatenate([filt_w1[l], jnp.zeros((FORD - FEMB, FORD), F32)], axis=0)
    twice = lambda w: jnp.kron(jnp.eye(2, dtype=F32), w)
    pair = lambda v: jnp.tile(v[None, :], (1, 2))
    kern, abs_sum = _implicit_filter(ztab, ttab, mtab, ftab, twice(w1p), pair(filt_b1[l]), twice(filt_w2[l]),
                                     pair(filt_b2[l]), twice(filt_w3[l]), pair(filt_freq[l]), negdelta)
    hy = _hyena_long_conv(u_rj, x2_rj, kern, abs_sum, hyena_bias[l])

    wr2 = jnp.concatenate(_split(w_router[l].T), axis=0)
    x1, h2, logits = _out_projection(att, hy, x, w_out[l].astype(BF16), lat(g1), norm2_g[l][None, :],
                                     lat(sh2), lat(sc2), wr2)

    tri = jnp.asarray(np.triu(np.ones((TCH, TCH))), BF16)
    pos, gate, offs = _routing(logits, tri)
    offs_flat = offs.reshape(-1)
    xg = _gather(offs_flat, h2, pos.reshape(B, NE, NTCH, TCH))
    y = _expert_ffn(xg, jnp.swapaxes(w_gate[l], 1, 2), jnp.swapaxes(w_up[l], 1, 2), w_down[l])
    return _combine(offs_flat, y, pos, gate, x1, lat(g2))
```

```python
import functools
import math

import numpy as np
import jax
import jax.numpy as jnp
from jax import lax
from jax.experimental import pallas as pl
from jax.experimental.pallas import tpu as pltpu

F32 = jnp.float32
BF16 = jnp.bfloat16
I32 = jnp.int32

D = 1024
SEQ = 8192
CTX = 256
GRID_W = 64
ATT_W = 512
HY_W = 512
HD = 64
NQ = 8
NKV = 2
QPK = NQ // NKV
KV_W = NKV * HD
IN_W = ATT_W + 2 * KV_W + 3 * HY_W
FEMB = 33
FORD = 64
NE = 16
CAP = 2 * SEQ // NE
DEXP = 2752
ROPE_THETA = 10000.0
EPS = 1e-6
DECAY_TARGET = 1e-2
FAST_DECAY_PCT = 0.3
SLOW_DECAY_PCT = 1.5

LANES = 128
SUBLANES = 8
BF16_ROWS = 16
VMEM_BYTES_V7X = 64 * 1024 * 1024
VMEM_LIMIT = VMEM_BYTES_V7X - 8 * 1024 * 1024

FN = 2 * SEQ
FN1 = 128
FN2 = 128

TCH = LANES
NTCH = SEQ // TCH
GW = TCH + SUBLANES
CW = TCH + BF16_ROWS
YROWS = CAP + BF16_ROWS


def _cparams(sem, vmem=None):
    return pltpu.CompilerParams(dimension_semantics=sem, vmem_limit_bytes=vmem or VMEM_LIMIT)


def _split(a):
    hi = a.astype(BF16)
    lo = (a - hi.astype(F32)).astype(BF16)
    return hi, lo


_NN = (((1,), (0,)), ((), ()))
_NT = (((1,), (1,)), ((), ()))
_TN = (((0,), (0,)), ((), ()))


def _dot(a, b, dn=_NN):
    return lax.dot_general(a, b, dn, preferred_element_type=F32)


def _dot3(a, b, dn=_NN):
    ah, al = _split(a)
    bh, bl = _split(b)
    return _dot(ah, bh, dn) + _dot(ah, bl, dn) + _dot(al, bh, dn)


def _mod_body(c_ref, w_ref, b_ref, o_ref):
    c = c_ref[...]
    s = c * (1.0 / (1.0 + jnp.exp(-c)))
    o_ref[...] = _dot3(s, w_ref[...]) + b_ref[...]


def _modulation(cc, w_mod, b_mod):
    n = w_mod.shape[1]
    return pl.pallas_call(
        _mod_body,
        grid=(n // D,),
        in_specs=[pl.BlockSpec((SUBLANES, D), lambda j: (0, 0)),
                  pl.BlockSpec((D, D), lambda j: (0, j)),
                  pl.BlockSpec((1, D), lambda j: (0, j))],
        out_specs=pl.BlockSpec((SUBLANES, D), lambda j: (0, j)),
        out_shape=jax.ShapeDtypeStruct((SUBLANES, n), F32),
        compiler_params=_cparams(("arbitrary",)),
        name="modulation",
    )(cc, w_mod, b_mod)


Q_SCALE = HD ** -0.5 * math.log2(math.e)


def _rms_mod(x, g, sh, sc):
    ms = jnp.mean(x * x, axis=-1, keepdims=True)
    return (x * lax.rsqrt(ms + EPS) * g) * (1.0 + sc) + sh


def _head_mean_square(t, bd):
    hi, lo = _split(t * t)
    return _dot(hi, bd) + _dot(lo, bd)


def _head_norm_rope(t, ms, g, cos, sin):
    tn = t * lax.rsqrt(ms + EPS) * g
    lane = lax.broadcasted_iota(I32, tn.shape, 1)
    sw = jnp.where((lane & 31) < 16, pltpu.roll(tn, LANES - 16, 1), pltpu.roll(tn, 16, 1))
    return tn * cos + sw * sin


def _proj_body(x_ref, g_ref, sh_ref, sc_ref, w_ref, gq_ref, gk_ref, bd_ref, cos_ref, sin_ref,
               q_ref, k_ref, v_ref, p_ref):
    h = _rms_mod(x_ref[0], g_ref[...], sh_ref[0], sc_ref[0])
    proj = _dot(h.astype(BF16), w_ref[...])
    bd = bd_ref[...]
    cos = cos_ref[...]
    sin = sin_ref[...]
    wide = 2 * LANES
    for j in range(ATT_W // wide):
        ms = _head_mean_square(proj[:, j * wide:(j + 1) * wide], bd)
        for i in range(2):
            sl = slice(j * wide + i * LANES, j * wide + (i + 1) * LANES)
            qj = _head_norm_rope(proj[:, sl], ms[:, i * LANES:(i + 1) * LANES], gq_ref[...], cos, sin)
            q_ref[0, :, sl] = (qj * Q_SCALE).astype(BF16)
    ms = _head_mean_square(proj[:, ATT_W:ATT_W + 2 * KV_W], bd)
    kk = _head_norm_rope(proj[:, ATT_W:ATT_W + KV_W], ms[:, 0:KV_W], gk_ref[...], cos, sin)
    vt = proj[:, ATT_W + KV_W:ATT_W + 2 * KV_W].T
    for g in range(NKV):
        k_ref[0, g] = kk[:, g * HD:(g + 1) * HD].astype(BF16)
        v_ref[0, g] = vt[g * HD:(g + 1) * HD, :].astype(BF16)
    p_ref[0] = proj[:, ATT_W + 2 * KV_W:]


def _in_projection(x, g1, sh, sc, w_in_bf, gq2, gk2, bd, cos_t, sin_t, tm):
    b, s, _ = x.shape
    row = lambda bi, i: (bi, 0, 0)
    tok = lambda bi, i: (bi, i, 0)
    const = lambda bi, i: (0, 0)
    return pl.pallas_call(
        _proj_body,
        grid=(b, s // tm),
        in_specs=[pl.BlockSpec((1, tm, D), tok),
                  pl.BlockSpec((1, D), const),
                  pl.BlockSpec((1, 1, D), row),
                  pl.BlockSpec((1, 1, D), row),
                  pl.BlockSpec((D, IN_W), const),
                  pl.BlockSpec((1, LANES), const),
                  pl.BlockSpec((1, LANES), const),
                  pl.BlockSpec((2 * LANES, 2 * LANES), const),
                  pl.BlockSpec((tm, LANES), lambda bi, i: (i, 0)),
                  pl.BlockSpec((tm, LANES), lambda bi, i: (i, 0))],
        out_specs=[pl.BlockSpec((1, tm, ATT_W), tok),
                   pl.BlockSpec((1, NKV, tm, HD), lambda bi, i: (bi, 0, i, 0)),
                   pl.BlockSpec((1, NKV, HD, tm), lambda bi, i: (bi, 0, 0, i)),
                   pl.BlockSpec((1, tm, 3 * HY_W), tok)],
        out_shape=[jax.ShapeDtypeStruct((b, s, ATT_W), BF16),
                   jax.ShapeDtypeStruct((b, NKV, s, HD), BF16),
                   jax.ShapeDtypeStruct((b, NKV, HD, s), BF16),
                   jax.ShapeDtypeStruct((b, s, 3 * HY_W), F32)],
        compiler_params=_cparams(("parallel", "parallel")),
        name="in_projection",
    )(x, g1, sh, sc, w_in_bf, gq2, gk2, bd, cos_t, sin_t)


ATT_TQ = 512
ATT_TK = 768
SK = SEQ + CTX
ATT_NCH = SK // ATT_TK


ATT_NQ = QPK * ATT_TQ
ATT_VR = HD + BF16_ROWS
assert ATT_NCH % 2 == 1
ATT_SHIFT_MAX = 120.0


def _attn_body(bound_ref, q_ref, k_ref, vt_ref, o_ref, s_ref, mx_ref, m_ref, acc_ref):
    qall = jnp.concatenate([q_ref[0, :, r * HD:(r + 1) * HD] for r in range(QPK)], axis=0)
    acc_ref[...] = jnp.zeros_like(acc_ref)
    bound = bound_ref[0]
    fixed_shift = 2.0 * bound <= ATT_SHIFT_MAX

    def finish():
        out = acc_ref[0:HD, :] * (1.0 / acc_ref[HD:HD + 1, :])
        for r in range(QPK):
            o_ref[0, :, r * HD:(r + 1) * HD] = out[:, r * ATT_TQ:(r + 1) * ATT_TQ].T.astype(BF16)

    @pl.when(fixed_shift)
    def _():
        def chunk(c, _):
            s = _dot(k_ref[0, 0, c], qall, _NT)
            acc_ref[...] += _dot(vt_ref[0, 0, c], jnp.exp2(s - bound).astype(BF16))
            return 0

        lax.fori_loop(0, ATT_NCH, chunk, 0, unroll=True)
        finish()

    @pl.when(jnp.logical_not(fixed_shift))
    def _():
        m_ref[...] = jnp.full(m_ref.shape, -1e30, F32)

        def scores(c, slot):
            s = _dot(k_ref[0, 0, c], qall, _NT)
            s_ref[slot] = s
            mx_ref[slot] = jnp.max(s, axis=0, keepdims=True)

        def update(c, slot):
            m_old = m_ref[...]
            m_new = jnp.maximum(m_old, mx_ref[slot])
            p = jnp.exp2(s_ref[slot] - m_new).astype(BF16)
            acc_ref[...] = jnp.exp2(m_old - m_new) * acc_ref[...] + _dot(vt_ref[0, 0, c], p)
            m_ref[...] = m_new

        scores(0, 0)

        def pair(i, _):
            c = 2 * i
            scores(c + 1, 1)
            update(c, 0)
            scores(c + 2, 0)
            update(c + 1, 1)
            return 0

        lax.fori_loop(0, ATT_NCH // 2, pair, 0)
        update(ATT_NCH - 1, 0)
        finish()


def _attention(bound, q, kch, vtch):
    b = q.shape[0]
    grid_spec = pltpu.PrefetchScalarGridSpec(
        num_scalar_prefetch=1,
        grid=(b, NKV, SEQ // ATT_TQ),
        in_specs=[pl.BlockSpec((1, ATT_TQ, QPK * HD), lambda bi, g, i, bd: (bi, i, g)),
                  pl.BlockSpec((1, 1, ATT_NCH, ATT_TK, HD), lambda bi, g, i, bd: (bi, g, 0, 0, 0)),
                  pl.BlockSpec((1, 1, ATT_NCH, ATT_VR, ATT_TK), lambda bi, g, i, bd: (bi, g, 0, 0, 0))],
        out_specs=pl.BlockSpec((1, ATT_TQ, QPK * HD), lambda bi, g, i, bd: (bi, i, g)),
        scratch_shapes=[pltpu.VMEM((2, ATT_TK, ATT_NQ), F32), pltpu.VMEM((2, 1, ATT_NQ), F32),
                        pltpu.VMEM((1, ATT_NQ), F32), pltpu.VMEM((ATT_VR, ATT_NQ), F32)],
    )
    return pl.pallas_call(
        _attn_body,
        grid_spec=grid_spec,
        out_shape=jax.ShapeDtypeStruct((b, SEQ, ATT_W), BF16),
        compiler_params=_cparams(("parallel", "parallel", "parallel")),
        name="attention",
    )(bound, q, kch, vtch)


SC_TM = 2048
SC_J = SC_TM // FN2


def _sconv_body(m1, a1, n1, m2, a2, n2, m3, a3, n3, w_ref, b_ref, u_ref, x2_ref):
    i = pl.program_id(1)
    last = pl.num_programs(1) - 1
    rows = lax.broadcasted_iota(I32, (SC_TM, HY_W), 0)

    def conv(main, prev, nxt, g):
        x = main[0]
        pr = jnp.where(i > 0, prev[0, SUBLANES - 1:SUBLANES, :], 0.0)
        nx = jnp.where(i < last, nxt[0, 0:1, :], 0.0)
        xm = jnp.where(rows == 0, pr, pltpu.roll(x, 1, 0))
        xp = jnp.where(rows == SC_TM - 1, nx, pltpu.roll(x, SC_TM - 1, 0))
        return (w_ref[g:g + 1, :] * xm + w_ref[3 + g:4 + g, :] * x + w_ref[6 + g:7 + g, :] * xp
                + b_ref[g:g + 1, :])

    def to_rj(t):
        return jnp.swapaxes(t.reshape(SC_J, FN2, HY_W), 0, 1).astype(BF16)

    x1 = conv(m1, a1, n1, 0)
    x2 = conv(m2, a2, n2, 1)
    v = conv(m3, a3, n3, 2)
    u_ref[0] = to_rj(v * x1)
    x2_ref[0] = to_rj(x2)


def _short_conv(p, cw9, cb3):
    b = p.shape[0]
    nblk8 = SEQ // SUBLANES
    step8 = SC_TM // SUBLANES
    specs = []
    for g in range(3):
        specs += [pl.BlockSpec((1, SC_TM, HY_W), lambda bi, i, g=g: (bi, i, g)),
                  pl.BlockSpec((1, SUBLANES, HY_W), lambda bi, i, g=g: (bi, jnp.maximum(i * step8 - 1, 0), g)),
                  pl.BlockSpec((1, SUBLANES, HY_W), lambda bi, i, g=g: (bi, jnp.minimum((i + 1) * step8, nblk8 - 1), g))]
    specs += [pl.BlockSpec((9, HY_W), lambda bi, i: (0, 0)), pl.BlockSpec((3, HY_W), lambda bi, i: (0, 0))]
    out = pl.BlockSpec((1, FN2, SC_J, HY_W), lambda bi, i: (bi, 0, i, 0))
    return pl.pallas_call(
        _sconv_body,
        grid=(b, SEQ // SC_TM),
        in_specs=specs,
        out_specs=[out, out],
        out_shape=[jax.ShapeDtypeStruct((b, FN2, SEQ // FN2, HY_W), BF16)] * 2,
        compiler_params=_cparams(("parallel", "parallel")),
        name="short_conv",
    )(p, p, p, p, p, p, p, p, p, cw9, cb3)


FILT_TR = 1024


def _filter_body(z_ref, t_ref, msk_ref, fwd_ref, w1_ref, b1_ref, w2_ref, b2_ref, w3_ref, fr_ref, dl_ref,
                 k_ref, s_ref):
    fr = fr_ref[...]
    h = jnp.sin(fr * (_dot3(z_ref[...], w1_ref[...]) + b1_ref[...]))
    h = jnp.sin(fr * (_dot3(h, w2_ref[...]) + b2_ref[...]))
    h = _dot3(h, w3_ref[...])
    h = jnp.concatenate([h[:, :2 * HY_W], h[:, 2 * HY_W:]], axis=0)
    h = jnp.where(fwd_ref[...] > 0.5, h[:, :HY_W], h[:, HY_W:])
    kern = h * jnp.exp(t_ref[...] * dl_ref[...]) * msk_ref[...]
    k_ref[...] = kern

    @pl.when(pl.program_id(0) == 0)
    def _():
        s_ref[...] = jnp.zeros_like(s_ref)

    s_ref[...] += jnp.sum(jnp.abs(kern), axis=0, keepdims=True)


def _implicit_filter(ztab, ttab, mtab, ftab, w1p, b1, w2, b2, w3, freq, negdelta):
    rowblk = lambda i: (i, 0)
    const = lambda i: (0, 0)
    col = pl.BlockSpec((FILT_TR, 1), rowblk)
    return pl.pallas_call(
        _filter_body,
        grid=(FN // FILT_TR,),
        in_specs=[pl.BlockSpec((FILT_TR // 2, 2 * FORD), rowblk), col, col, col,
                  pl.BlockSpec((2 * FORD, 2 * FORD), const),
                  pl.BlockSpec((1, 2 * FORD), const),
                  pl.BlockSpec((2 * FORD, 2 * FORD), const),
                  pl.BlockSpec((1, 2 * FORD), const),
                  pl.BlockSpec((2 * FORD, 4 * HY_W), const),
                  pl.BlockSpec((1, 2 * FORD), const),
                  pl.BlockSpec((1, HY_W), const)],
        out_specs=[pl.BlockSpec((FILT_TR, HY_W), rowblk),
                   pl.BlockSpec((1, HY_W), const)],
        out_shape=[jax.ShapeDtypeStruct((FN, HY_W), F32), jax.ShapeDtypeStruct((1, HY_W), F32)],
        compiler_params=_cparams(("arbitrary",)),
        name="implicit_filter",
    )(ztab, ttab, mtab, ftab, w1p, b1, w2, b2, w3, freq, negdelta)


DFT_G = 4
DFT_KB = 16
DFT_NP = FN1 // DFT_KB
DFT_UNROLL = 8


def _dot2c(fh, fl, zb):
    return _dot(fh, zb) + _dot(fl, zb)


def _lead_stage(src, fh, fl, dst_ref):
    def group(rg, _):
        r0 = rg * DFT_G
        rhs = jnp.concatenate([src(r0 + g) for g in range(DFT_G)], axis=1)
        blk = _dot2c(fh, fl, rhs)
        for g in range(DFT_G):
            dst_ref[r0 + g] = blk[:, g * LANES:(g + 1) * LANES]
        return 0

    lax.fori_loop(0, FN2 // DFT_G, group, 0, unroll=DFT_UNROLL)


def _lead_phase(src, lh_ref, ll_ref, p_ref, q_ref):
    for h in range(2):
        rows = slice(h * FN1, (h + 1) * FN1)
        _lead_stage(src, lh_ref[rows, :], ll_ref[rows, :], p_ref)
        q_ref[rows] = jnp.swapaxes(p_ref[...], 0, 1)


def _lane_cat(xs):
    return jnp.concatenate(xs, axis=1)


def _mid_forward(q_ref, k0, tr_ref, ti_ref, fh, fl, half):
    tr_t, ti_t = tr_ref[...].T, ti_ref[...].T
    brs, bis, trs, tis = [], [], [], []
    for g in range(DFT_G):
        jj = half * DFT_G + g
        ar, ai = q_ref[k0 + jj], q_ref[FN1 + k0 + jj]
        tr = jnp.broadcast_to(tr_t[:, jj:jj + 1], (FN2, LANES))
        ti = jnp.broadcast_to(ti_t[:, jj:jj + 1], (FN2, LANES))
        brs.append(ar * tr - ai * ti)
        bis.append(ar * ti + ai * tr)
        trs.append(tr)
        tis.append(ti)
    b = jnp.concatenate([_lane_cat(brs), _lane_cat(bis)], axis=0).astype(BF16)
    return _dot2c(fh, fl, b), _lane_cat(trs), _lane_cat(tis)


def _spectrum_body(k_ref, tr_ref, ti_ref, lh_ref, ll_ref, fh_ref, fl_ref, sc_ref, o_ref, p_ref, q_ref):
    ph = pl.program_id(1)

    @pl.when(ph == 0)
    def _():
        _lead_phase(lambda r: k_ref[r].astype(BF16), lh_ref, ll_ref, p_ref, q_ref)

    @pl.when(ph > 0)
    def _():
        k0 = (ph - 1) * DFT_KB
        sc = _lane_cat([sc_ref[...]] * DFT_G)
        for half in range(DFT_KB // DFT_G):
            x, _, _ = _mid_forward(q_ref, k0, tr_ref, ti_ref, fh_ref[...], fl_ref[...], half)
            x = x * sc
            for g in range(DFT_G):
                lanes = slice(g * LANES, (g + 1) * LANES)
                o_ref[0, half * DFT_G + g] = x[:FN2, lanes]
                o_ref[1, half * DFT_G + g] = x[FN2:, lanes]


def _mid_index(ph):
    return jnp.clip(ph - 1, 0, DFT_NP - 1)


def _filter_spectrum(kern_rj, twr, twi, lh, ll, fh, fl, scale):
    const = lambda c, ph: (0, 0)
    tw = pl.BlockSpec((DFT_KB, FN2), lambda c, ph: (_mid_index(ph), 0))
    return pl.pallas_call(
        _spectrum_body,
        grid=(HY_W // LANES, DFT_NP + 1),
        in_specs=[pl.BlockSpec((FN2, FN1, LANES), lambda c, ph: (0, 0, c)),
                  tw, tw,
                  pl.BlockSpec(lh.shape, const), pl.BlockSpec(ll.shape, const),
                  pl.BlockSpec(fh.shape, const), pl.BlockSpec(fl.shape, const),
                  pl.BlockSpec((1, LANES), lambda c, ph: (0, c))],
        out_specs=pl.BlockSpec((2, DFT_KB, FN2, LANES), lambda c, ph: (0, _mid_index(ph), 0, c)),
        out_shape=jax.ShapeDtypeStruct((2, FN1, FN2, HY_W), F32),
        scratch_shapes=[pltpu.VMEM((FN2, FN1, LANES), F32), pltpu.VMEM((2 * FN1, FN2, LANES), F32)],
        compiler_params=_cparams(("parallel", "arbitrary")),
        name="filter_spectrum",
    )(kern_rj, twr, twi, lh, ll, fh, fl, scale)


def _hconv_body(u_ref, x2_ref, kh_ref, tr_ref, ti_ref, lh_ref, ll_ref, fh_ref, fl_ref, gh_ref, gl_ref,
                oh_ref, ol_ref, bias_ref, o_ref, p_ref, q_ref):
    ph = pl.program_id(1)

    def both(ref, r):
        return jnp.concatenate([ref[0, r], ref[1, r]], axis=0)

    @pl.when(ph == 0)
    def _():
        _lead_phase(lambda r: both(u_ref, r), lh_ref, ll_ref, p_ref, q_ref)

    @pl.when((ph > 0) & (ph <= DFT_NP))
    def _():
        k0 = (ph - 1) * DFT_KB
        for half in range(DFT_KB // DFT_G):
            x, tr, ti = _mid_forward(q_ref, k0, tr_ref, ti_ref, fh_ref[...], fl_ref[...], half)
            xr, xi = x[:FN2], x[FN2:]
            kr = _lane_cat([kh_ref[0, half * DFT_G + g] for g in range(DFT_G)])
            ki = _lane_cat([kh_ref[1, half * DFT_G + g] for g in range(DFT_G)])
            y = jnp.concatenate([xr * kr - xi * ki, xr * ki + xi * kr], axis=0).astype(BF16)
            c = _dot2c(gh_ref[...], gl_ref[...], y)
            cr, ci = c[:FN2], c[FN2:]
            dr = cr * tr + ci * ti
            di = ci * tr - cr * ti
            for g in range(DFT_G):
                lanes = slice(g * LANES, (g + 1) * LANES)
                q_ref[k0 + half * DFT_G + g] = dr[:, lanes]
                q_ref[FN1 + k0 + half * DFT_G + g] = di[:, lanes]

    @pl.when(ph == DFT_NP + 1)
    def _():
        bias = bias_ref[...]
        p_ref[...] = jnp.swapaxes(q_ref[0:FN1], 0, 1)
        _lead_stage(lambda r: p_ref[r].astype(BF16), oh_ref[:, 0:FN1], ol_ref[:, 0:FN1], q_ref)
        p_ref[...] = jnp.swapaxes(q_ref[FN1:2 * FN1], 0, 1)
        oh2, ol2 = oh_ref[:, FN1:2 * FN1], ol_ref[:, FN1:2 * FN1]

        def group(rg, _):
            r0 = rg * DFT_G
            rhs = _lane_cat([p_ref[r0 + g].astype(BF16) for g in range(DFT_G)])
            blk = _dot2c(oh2, ol2, rhs)
            for g in range(DFT_G):
                r = r0 + g
                y = q_ref[r] + blk[:, g * LANES:(g + 1) * LANES]
                q_ref[r] = (y + both(u_ref, r).astype(F32) * bias) * both(x2_ref, r).astype(F32)
            return 0

        lax.fori_loop(0, FN2 // DFT_G, group, 0, unroll=DFT_UNROLL)
        p_ref[...] = jnp.swapaxes(q_ref[0:FN2], 0, 1)
        nj = SEQ // FN2
        for b in range(2):
            o_ref[b] = p_ref[b * nj:(b + 1) * nj].reshape(SEQ, LANES).astype(BF16)


def _hyena_conv(u_rj, x2_rj, khat, twr, twi, lead, fwd, inv, out, bias):
    const = lambda c, ph: (0, 0)
    nj = SEQ // FN2
    tw = pl.BlockSpec((DFT_KB, FN2), lambda c, ph: (_mid_index(ph), 0))
    sig = pl.BlockSpec((2, FN2, nj, LANES), lambda c, ph: (0, 0, 0, c))
    mats = [m for pair in (lead, fwd, inv, out) for m in pair]
    return pl.pallas_call(
        _hconv_body,
        grid=(HY_W // LANES, DFT_NP + 2),
        in_specs=[sig, sig,
                  pl.BlockSpec((2, DFT_KB, FN2, LANES), lambda c, ph: (0, _mid_index(ph), 0, c)),
                  tw, tw] + [pl.BlockSpec(m.shape, const) for m in mats]
                 + [pl.BlockSpec((1, LANES), lambda c, ph: (0, c))],
        out_specs=pl.BlockSpec((2, SEQ, LANES), lambda c, ph: (0, 0, c)),
        out_shape=jax.ShapeDtypeStruct((2, SEQ, HY_W), BF16),
        scratch_shapes=[pltpu.VMEM((FN2, FN1, LANES), F32), pltpu.VMEM((2 * FN1, FN2, LANES), F32)],
        compiler_params=_cparams(("parallel", "arbitrary")),
        name="hyena_conv",
    )(u_rj, x2_rj, khat, twr, twi, *mats, bias)


OP_TM = 1024


def _outproj_body(att_ref, hy_ref, x_ref, w_ref, g1_ref, n2_ref, sh_ref, sc_ref, wr_ref,
                  x1_ref, h2_ref, lg_ref):
    a = jnp.concatenate([att_ref[0], hy_ref[0]], axis=1)
    x1 = x_ref[0] + g1_ref[0] * _dot(a, w_ref[...])
    x1_ref[0] = x1
    h2 = _rms_mod(x1, n2_ref[...], sh_ref[0], sc_ref[0])
    hh, hl = _split(h2)
    h2_ref[0] = hh
    wr = wr_ref[...]
    both = _dot(wr, hh, _NT)
    lg_ref[0] = both[0:NE] + both[NE:2 * NE] + _dot(wr[0:NE], hl, _NT)


def _out_projection(att, hy, x, w_out_bf, g1r, n2g, sh2, sc2, wr2):
    b = x.shape[0]
    tok = lambda bi, i: (bi, i, 0)
    row = lambda bi, i: (bi, 0, 0)
    const = lambda bi, i: (0, 0)
    return pl.pallas_call(
        _outproj_body,
        grid=(b, SEQ // OP_TM),
        in_specs=[pl.BlockSpec((1, OP_TM, ATT_W), tok),
                  pl.BlockSpec((1, OP_TM, HY_W), tok),
                  pl.BlockSpec((1, OP_TM, D), tok),
                  pl.BlockSpec((ATT_W + HY_W, D), const),
                  pl.BlockSpec((1, 1, D), row),
                  pl.BlockSpec((1, D), const),
                  pl.BlockSpec((1, 1, D), row),
                  pl.BlockSpec((1, 1, D), row),
                  pl.BlockSpec((2 * NE, D), const)],
        out_specs=[pl.BlockSpec((1, OP_TM, D), tok),
                   pl.BlockSpec((1, OP_TM, D), tok),
                   pl.BlockSpec((1, NE, OP_TM), lambda bi, i: (bi, 0, i))],
        out_shape=[jax.ShapeDtypeStruct((b, SEQ, D), F32),
                   jax.ShapeDtypeStruct((b, SEQ, D), BF16),
                   jax.ShapeDtypeStruct((b, NE, SEQ), F32)],
        compiler_params=_cparams(("parallel", "parallel")),
        name="out_projection",
    )(att, hy, x, w_out_bf, g1r, n2g, sh2, sc2, wr2)


def _routing_body(lg_ref, tri_ref, pos_ref, gate_ref, off_ref, cs_ref):
    lg = lg_ref[0]
    e = jnp.exp(lg - jnp.max(lg, axis=0, keepdims=True))
    aff = e / jnp.sum(e, axis=0, keepdims=True)
    gate_ref[0] = aff
    def count_ge(t):
        return jnp.sum(jnp.where(aff >= t, 1.0, 0.0), axis=1, keepdims=True)

    def bisect(i, thr):
        cand = thr | (jnp.int32(1) << (30 - i))
        return jnp.where(count_ge(pltpu.bitcast(cand, F32)) >= float(CAP), cand, thr)

    thr = lax.fori_loop(0, 31, bisect, jnp.zeros((NE, 1), I32))
    lo = pltpu.bitcast(thr, F32)
    hi = jnp.maximum(pltpu.bitcast(thr + 1, F32), jnp.finfo(F32).tiny)

    def refine(i, c):
        lo, hi = c
        mid = lo + (hi - lo) * 0.5
        ok = count_ge(mid) >= float(CAP)
        return jnp.where(ok, mid, lo), jnp.where(ok, hi, mid)

    lo, hi = lax.fori_loop(0, 32, refine, (lo, hi))
    gt = aff >= hi
    eq = (aff >= lo) & jnp.logical_not(gt)
    need = float(CAP) - jnp.sum(jnp.where(gt, 1.0, 0.0), axis=1, keepdims=True)
    tri = tri_ref[...]

    def excl_cumsum(mask_f, record_offsets):
        carry = jnp.zeros((NE, 1), F32)
        for c in range(NTCH):
            sl = slice(c * TCH, (c + 1) * TCH)
            m = mask_f[:, sl]
            inc = _dot(m.astype(BF16), tri)
            cs_ref[:, sl] = inc - m + carry
            if record_offsets:
                off_ref[0, :, c:c + 1] = carry.astype(I32)
            carry = carry + inc[:, TCH - 1:TCH]
        return cs_ref[...]

    eq_rank = excl_cumsum(jnp.where(eq, 1.0, 0.0), False)
    sel = gt | (eq & (eq_rank < need))
    pos = excl_cumsum(jnp.where(sel, 1.0, 0.0), True)
    pos_ref[0] = jnp.where(sel, pos.astype(I32), -1)


def _routing(logits, tri):
    b = logits.shape[0]
    blk = pl.BlockSpec((1, NE, SEQ), lambda bi: (bi, 0, 0))
    return pl.pallas_call(
        _routing_body,
        grid=(b,),
        in_specs=[blk, pl.BlockSpec((TCH, TCH), lambda bi: (0, 0))],
        out_specs=[blk, blk, pl.BlockSpec((1, NE, NTCH), lambda bi: (bi, 0, 0))],
        out_shape=[jax.ShapeDtypeStruct((b, NE, SEQ), I32),
                   jax.ShapeDtypeStruct((b, NE, SEQ), F32),
                   jax.ShapeDtypeStruct((b, NE, NTCH), I32)],
        scratch_shapes=[pltpu.VMEM((NE, SEQ), F32)],
        compiler_params=_cparams(("parallel",)),
        name="routing",
    )(logits, tri)


GATHER_UNROLL = 8


GW_SMALL = 64


def _gather_body(off_ref, h_ref, pos_ref, xg_ref, acc_ref):
    b = pl.program_id(0)
    e = pl.program_id(1)
    row0 = (b * NE + e) * NTCH
    acc_ref[...] = jnp.zeros_like(acc_ref)

    def count(c2, most):
        nxt = jnp.where(c2 + 1 < NTCH // 2, off_ref[row0 + jnp.minimum(2 * c2 + 2, NTCH - 1)], CAP)
        return jnp.maximum(most, nxt - off_ref[row0 + 2 * c2])

    most = lax.fori_loop(0, NTCH // 2, count, 0)

    def sweep(window, span):
        crow = lax.broadcasted_iota(I32, (window, span * TCH), 0)

        def chunks(i, _):
            for j in range(GATHER_UNROLL):
                c = (i * GATHER_UNROLL + j) * span
                off = off_ref[row0 + c]
                base = pl.multiple_of(jnp.minimum((off >> 3) << 3, CAP + SUBLANES - window), SUBLANES)
                t0 = pl.multiple_of(c * TCH, TCH)
                rel = _lane_cat([pos_ref[0, 0, pl.ds(c + s, 1), :] for s in range(span)]) - base
                onehot = jnp.where(crow == rel, 1.0, 0.0).astype(BF16)
                acc_ref[pl.ds(base, window), :] += _dot(onehot, h_ref[0, pl.ds(t0, span * TCH), :])
            return 0

        lax.fori_loop(0, NTCH // (GATHER_UNROLL * span), chunks, 0)

    @pl.when(most <= GW_SMALL - SUBLANES)
    def _():
        sweep(GW_SMALL, 2)

    @pl.when(most > GW_SMALL - SUBLANES)
    def _():
        sweep(GW, 1)

    xg_ref[0, 0] = acc_ref[0:CAP, :].astype(BF16)


def _gather(offs_flat, h2, pos4):
    b = h2.shape[0]
    grid_spec = pltpu.PrefetchScalarGridSpec(
        num_scalar_prefetch=1,
        grid=(b, NE),
        in_specs=[pl.BlockSpec((1, SEQ, D), lambda bi, e, off: (bi, 0, 0)),
                  pl.BlockSpec((1, 1, NTCH, TCH), lambda bi, e, off: (bi, e, 0, 0))],
        out_specs=pl.BlockSpec((1, 1, CAP, D), lambda bi, e, off: (bi, e, 0, 0)),
        scratch_shapes=[pltpu.VMEM((CAP + SUBLANES, D), F32)],
    )
    return pl.pallas_call(
        _gather_body,
        grid_spec=grid_spec,
        out_shape=jax.ShapeDtypeStruct((b, NE, CAP, D), BF16),
        compiler_params=_cparams(("parallel", "arbitrary")),
        name="moe_gather",
    )(offs_flat, h2, pos4)


FFN_TM = 512
FFN_NF = 4
FFN_FC = DEXP // FFN_NF
assert FFN_FC * FFN_NF == DEXP and FFN_FC % BF16_ROWS == 0


def _ffn_body(xg_ref, wgt_ref, wut_ref, wd_ref, y_ref, acc_ref):
    j = pl.program_id(1)
    nb = xg_ref.shape[0]

    @pl.when(j == 0)
    def _():
        acc_ref[...] = jnp.zeros_like(acc_ref)

    wgt = wgt_ref[0].astype(BF16)
    wut = wut_ref[0].astype(BF16)
    wd = wd_ref[0].astype(BF16)
    for b in range(nb):
        for mb in range(CAP // FFN_TM):
            rows = slice(mb * FFN_TM, (mb + 1) * FFN_TM)
            xb = xg_ref[b, 0, rows, :]
            a = _dot(xb, wgt, _NT)
            u = _dot(xb, wut, _NT)
            h = (a * (1.0 / (1.0 + jnp.exp(-a))) * u).astype(BF16)
            acc_ref[b, rows, :] += _dot(h, wd)

    @pl.when(j == FFN_NF - 1)
    def _():
        for b in range(nb):
            y_ref[b, 0, 0:CAP, :] = acc_ref[b].astype(BF16)
            y_ref[b, 0, CAP:YROWS, :] = jnp.zeros((YROWS - CAP, D), BF16)


def _expert_ffn(xg, w_gate_t, w_up_t, w_down):
    b = xg.shape[0]
    wblk = pl.BlockSpec((1, FFN_FC, D), lambda e, j: (e, j, 0))
    return pl.pallas_call(
        _ffn_body,
        grid=(NE, FFN_NF),
        in_specs=[pl.BlockSpec((b, 1, CAP, D), lambda e, j: (0, e, 0, 0)), wblk, wblk, wblk],
        out_specs=pl.BlockSpec((b, 1, YROWS, D), lambda e, j: (0, e, 0, 0)),
        out_shape=jax.ShapeDtypeStruct((b, NE, YROWS, D), BF16),
        scratch_shapes=[pltpu.VMEM((b, CAP, D), F32)],
        compiler_params=_cparams(("parallel", "arbitrary")),
        name="expert_ffn",
    )(xg, w_gate_t, w_up_t, w_down)


CW_SMALL = 64
CW_STACK = 2 * LANES // CW_SMALL


CMB_T = 4


def _combine_body(off_ref, y_ref, pos_ref, gate_ref, x1_ref, g2_ref, o_ref):
    b = pl.program_id(0)
    i = pl.program_id(1)

    def offset(e, c):
        return off_ref[(b * NE + e) * NTCH + c]

    def window(e, t, rows):
        lanes = slice(t * TCH, (t + 1) * TCH)
        base = pl.multiple_of(jnp.minimum((offset(e, i * CMB_T + t) >> 4) << 4, YROWS - rows), BF16_ROWS)
        rel = pos_ref[0, e:e + 1, lanes] - base
        crow = lax.broadcasted_iota(I32, (rows, TCH), 0)
        w = jnp.where(crow == rel, gate_ref[0, e:e + 1, lanes], 0.0).astype(BF16)
        return w, y_ref[0, e, pl.ds(base, rows), :]

    most = jnp.int32(0)
    for t in range(CMB_T):
        c = i * CMB_T + t
        for e in range(NE):
            nxt = jnp.where(c + 1 < NTCH, offset(e, jnp.minimum(c + 1, NTCH - 1)), CAP)
            most = jnp.maximum(most, nxt - offset(e, c))

    def finish(t, acc):
        rows = slice(t * TCH, (t + 1) * TCH)
        o_ref[0, rows, :] = x1_ref[0, rows, :] + g2_ref[0] * acc

    @pl.when(most <= CW_SMALL - BF16_ROWS)
    def _():
        for t in range(CMB_T):
            acc = jnp.zeros((TCH, D), F32)
            for e0 in range(0, NE, CW_STACK):
                ws, ys = zip(*[window(e, t, CW_SMALL) for e in range(e0, e0 + CW_STACK)])
                acc = acc + _dot(jnp.concatenate(ws, axis=0), jnp.concatenate(ys, axis=0), _TN)
            finish(t, acc)

    @pl.when(most > CW_SMALL - BF16_ROWS)
    def _():
        for t in range(CMB_T):
            acc = jnp.zeros((TCH, D), F32)
            for e in range(NE):
                w, yw = window(e, t, CW)
                acc = acc + _dot(w, yw, _TN)
            finish(t, acc)


def _combine(offs_flat, y, pos, gate, x1, g2r):
    b = x1.shape[0]
    grid_spec = pltpu.PrefetchScalarGridSpec(
        num_scalar_prefetch=1,
        grid=(b, NTCH // CMB_T),
        in_specs=[pl.BlockSpec((1, NE, YROWS, D), lambda bi, i, off: (bi, 0, 0, 0),
                               pipeline_mode=pl.Buffered(1)),
                  pl.BlockSpec((1, NE, CMB_T * TCH), lambda bi, i, off: (bi, 0, i)),
                  pl.BlockSpec((1, NE, CMB_T * TCH), lambda bi, i, off: (bi, 0, i)),
                  pl.BlockSpec((1, CMB_T * TCH, D), lambda bi, i, off: (bi, i, 0)),
                  pl.BlockSpec((1, 1, D), lambda bi, i, off: (bi, 0, 0))],
        out_specs=pl.BlockSpec((1, CMB_T * TCH, D), lambda bi, i, off: (bi, i, 0)),
    )
    return pl.pallas_call(
        _combine_body,
        grid_spec=grid_spec,
        out_shape=jax.ShapeDtypeStruct((b, SEQ, D), F32),
        compiler_params=_cparams(("parallel", "arbitrary")),
        name="moe_combine",
    )(offs_flat, y, pos, gate, x1, g2r)


def _np_split(m):
    hi = np.asarray(m, np.float64).astype(BF16)
    lo = (m - hi.astype(np.float64)).astype(BF16)
    return jnp.asarray(hi), jnp.asarray(lo)


@functools.lru_cache(maxsize=None)
def _dft_tables():
    a = np.arange(FN1, dtype=np.float64)
    ang = 2.0 * np.pi * np.outer(a, a) / FN1
    fr, fi = np.cos(ang), -np.sin(ang)
    half = SEQ // FN2
    lead_u = np.block([[fr[:, :half], -fi[:, :half]], [fi[:, :half], fr[:, :half]]])
    lead_k = np.concatenate([fr, fi], axis=0)
    fwd = np.block([[fr, -fi], [fi, fr]])
    inv = np.block([[fr, fi], [-fi, fr]])
    out = np.block([[fr[:half], fi[:half]], [-fi[:half], fr[:half]]])
    n2 = np.arange(FN2, dtype=np.float64)
    tw = 2.0 * np.pi * np.outer(a, n2) / FN
    twr = np.cos(tw).astype(np.float32)
    twi = (-np.sin(tw)).astype(np.float32)
    return dict(lead_u=lead_u, lead_k=lead_k, fwd=fwd, inv=inv, out=out, twr=twr, twi=twi)


@functools.lru_cache(maxsize=None)
def _filter_tables():
    L = SEQ
    n = np.arange(FN).reshape(FN1, FN2).T.reshape(-1)
    lag = np.where(n < L, n, FN - n)
    jc = np.minimum(lag, L - 1).astype(np.float64)
    t = (jc / (L - 1))[:, None]
    bands = (FEMB - 1) // 2
    w = 2.0 * np.pi * jc / L
    f = np.linspace(1e-4, bands - 1, bands)
    fw = w[:, None] * f[None, :]
    z = np.concatenate([t, np.cos(fw), -np.sin(fw), np.zeros((FN, FORD - FEMB))], axis=-1)
    mask = np.where(n == L, 0.0, 1.0)[:, None]
    fwd = np.where(n < L, 1.0, 0.0)[:, None]
    max_decay = math.log(DECAY_TARGET) / FAST_DECAY_PCT
    min_decay = math.log(DECAY_TARGET) / SLOW_DECAY_PCT
    negdelta = -np.abs(np.linspace(min_decay, max_decay, HY_W))[None, :]
    z = z.reshape(FN // FILT_TR, 2, FILT_TR // 2, FORD).transpose(0, 2, 1, 3).reshape(FN // 2, 2 * FORD)
    return tuple(np.asarray(a, np.float32) for a in (z, t, mask, fwd, negdelta))


@functools.lru_cache(maxsize=None)
def _rope_tables(n):
    rows = n // GRID_W
    row_id, col_id = np.meshgrid(np.arange(rows, dtype=np.float64), np.arange(GRID_W, dtype=np.float64), indexing="ij")
    quarter = HD // 4
    inv_freq = ROPE_THETA ** (-np.arange(quarter, dtype=np.float64) / quarter)
    ar = row_id.reshape(-1)[:, None] * inv_freq
    ac = col_id.reshape(-1)[:, None] * inv_freq
    cos = np.concatenate([np.cos(ar), np.cos(ar), np.cos(ac), np.cos(ac)], axis=-1)
    sin = np.concatenate([-np.sin(ar), np.sin(ar), -np.sin(ac), np.sin(ac)], axis=-1)
    reps = (1, LANES // HD)
    return np.tile(cos, reps).astype(np.float32), np.tile(sin, reps).astype(np.float32)


def _hyena_long_conv(u_rj, x2_rj, kern, abs_sum, bias):
    tb = _dft_tables()
    twr, twi = tb["twr"], tb["twi"]
    fwd = _np_split(tb["fwd"])
    scale = 1.0 / (abs_sum * float(FN))
    khat = _filter_spectrum(kern.reshape(FN2, FN1, HY_W), twr, twi, *_np_split(tb["lead_k"]), *fwd, scale)
    return _hyena_conv(u_rj, x2_rj, khat, twr, twi, _np_split(tb["lead_u"]), fwd, _np_split(tb["inv"]),
                       _np_split(tb["out"]), bias.reshape(1, HY_W))


def kernel(x, c, ctx, c_ctx, w_mod, b_mod, norm1_g, norm2_g, w_in, w_out, q_norm_g, k_norm_g,
           conv_w, conv_b, filt_w1, filt_b1, filt_w2, filt_b2, filt_w3, filt_freq, hyena_bias,
           w_router, w_gate, w_up, w_down):
    B = x.shape[0]
    assert x.shape == (B, SEQ, D) and B == 2 and ctx.shape == (B, CTX, D) and w_mod.shape[0] == 1
    l = 0

    cc = jnp.concatenate([c, c_ctx[None, :], jnp.zeros((SUBLANES - B - 1, D), F32)], axis=0)
    mod = _modulation(cc, w_mod[l], b_mod[l][None, :])
    sh1, sc1, g1, sh2, sc2, g2 = [mod[:, i * D:(i + 1) * D] for i in range(6)]
    lat = lambda m: m[:B, None, :]
    ctxrow = lambda m: jnp.broadcast_to(m[B:B + 1, None, :], (B, 1, D))

    w_in_bf = w_in[l].astype(BF16)
    gq2 = jnp.tile(q_norm_g[l][None, :], (1, LANES // HD))
    gk2 = jnp.tile(k_norm_g[l][None, :], (1, LANES // HD))
    bd = jnp.asarray(np.kron(np.eye(2 * LANES // HD), np.full((HD, HD), 1.0 / HD)), BF16)
    cos_t, sin_t = _rope_tables(SEQ)
    n1g = norm1_g[l][None, :]

    q, k, vt, p = _in_projection(x, n1g, lat(sh1), lat(sc1), w_in_bf, gq2, gk2, bd, cos_t, sin_t, 1024)
    _, kc, vct, _ = _in_projection(ctx, n1g, ctxrow(sh1), ctxrow(sc1), w_in_bf, gq2, gk2, bd,
                                   jnp.ones((CTX, LANES), F32), jnp.zeros((CTX, LANES), F32), CTX)

    kch = jnp.concatenate([k, kc], axis=2).reshape(B, NKV, ATT_NCH, ATT_TK, HD)
    vt_all = jnp.concatenate([vt, vct], axis=3).reshape(B, NKV, HD, ATT_NCH, ATT_TK)
    ones_pad = jnp.concatenate([jnp.ones((B, NKV, ATT_NCH, 1, ATT_TK), BF16),
                                jnp.zeros((B, NKV, ATT_NCH, BF16_ROWS - 1, ATT_TK), BF16)], axis=3)
    vtch = jnp.concatenate([vt_all.transpose(0, 1, 3, 2, 4), ones_pad], axis=3)
    bound = (1.02 * HD * Q_SCALE) * jnp.max(jnp.abs(q_norm_g[l])) * jnp.max(jnp.abs(k_norm_g[l]))
    att = _attention(bound.reshape(1).astype(F32), q, kch, vtch)

    cw9 = conv_w[l].reshape(3, 3, HY_W).reshape(9, HY_W)
    cb3 = conv_b[l].reshape(3, HY_W)
    u_rj, x2_rj = _short_conv(p, cw9, cb3)
    ztab, ttab, mtab, ftab, negdelta = _filter_tables()
    w1p = jnp.concatenate([filt_w1[l], jnp.zeros((FORD - FEMB, FORD), F32)], axis=0)
    twice = lambda w: jnp.kron(jnp.eye(2, dtype=F32), w)
    pair = lambda v: jnp.tile(v[None, :], (1, 2))
    kern, abs_sum = _implicit_filter(ztab, ttab, mtab, ftab, twice(w1p), pair(filt_b1[l]), twice(filt_w2[l]),
                                     pair(filt_b2[l]), twice(filt_w3[l]), pair(filt_freq[l]), negdelta)
    hy = _hyena_long_conv(u_rj, x2_rj, kern, abs_sum, hyena_bias[l])

    wr2 = jnp.concatenate(_split(w_router[l].T), axis=0)
    x1, h2, logits = _out_projection(att, hy, x, w_out[l].astype(BF16), lat(g1), norm2_g[l][None, :],
                                     lat(sh2), lat(sc2), wr2)

    tri = jnp.asarray(np.triu(np.ones((TCH, TCH))), BF16)
    pos, gate, offs = _routing(logits, tri)
    offs_flat = offs.reshape(-1)
    xg = _gather(offs_flat, h2, pos.reshape(B, NE, NTCH, TCH))
    y = _expert_ffn(xg, jnp.swapaxes(w_gate[l], 1, 2), jnp.swapaxes(w_up[l], 1, 2), w_down[l])
    return _combine(offs_flat, y, pos, gate, x1, lat(g2))
```

```python
import functools
import math

import numpy as np
import jax
import jax.numpy as jnp
from jax import lax
from jax.experimental import pallas as pl
from jax.experimental.pallas import tpu as pltpu

F32 = jnp.float32
BF16 = jnp.bfloat16
I32 = jnp.int32

D = 1024
SEQ = 8192
CTX = 256
GRID_W = 64
ATT_W = 512
HY_W = 512
HD = 64
NQ = 8
NKV = 2
QPK = NQ // NKV
KV_W = NKV * HD
IN_W = ATT_W + 2 * KV_W + 3 * HY_W
FEMB = 33
FORD = 64
NE = 16
CAP = 2 * SEQ // NE
DEXP = 2752
ROPE_THETA = 10000.0
EPS = 1e-6
DECAY_TARGET = 1e-2
FAST_DECAY_PCT = 0.3
SLOW_DECAY_PCT = 1.5

LANES = 128
SUBLANES = 8
BF16_ROWS = 16
VMEM_BYTES_V7X = 64 * 1024 * 1024
VMEM_LIMIT = VMEM_BYTES_V7X - 8 * 1024 * 1024

FN = 2 * SEQ
FN1 = 128
FN2 = 128

TCH = LANES
NTCH = SEQ // TCH
GW = TCH + SUBLANES
CW = TCH + BF16_ROWS
YROWS = CAP + BF16_ROWS


def _cparams(sem, vmem=None):
    return pltpu.CompilerParams(dimension_semantics=sem, vmem_limit_bytes=vmem or VMEM_LIMIT)


def _split(a):
    hi = a.astype(BF16)
    lo = (a - hi.astype(F32)).astype(BF16)
    return hi, lo


_NN = (((1,), (0,)), ((), ()))
_NT = (((1,), (1,)), ((), ()))
_TN = (((0,), (0,)), ((), ()))


def _dot(a, b, dn=_NN):
    return lax.dot_general(a, b, dn, preferred_element_type=F32)


def _dot3(a, b, dn=_NN):
    ah, al = _split(a)
    bh, bl = _split(b)
    return _dot(ah, bh, dn) + _dot(ah, bl, dn) + _dot(al, bh, dn)


def _mod_body(c_ref, w_ref, b_ref, o_ref):
    c = c_ref[...]
    s = c * (1.0 / (1.0 + jnp.exp(-c)))
    o_ref[...] = _dot3(s, w_ref[...]) + b_ref[...]


def _modulation(cc, w_mod, b_mod):
    n = w_mod.shape[1]
    return pl.pallas_call(
        _mod_body,
        grid=(n // D,),
        in_specs=[pl.BlockSpec((SUBLANES, D), lambda j: (0, 0)),
                  pl.BlockSpec((D, D), lambda j: (0, j)),
                  pl.BlockSpec((1, D), lambda j: (0, j))],
        out_specs=pl.BlockSpec((SUBLANES, D), lambda j: (0, j)),
        out_shape=jax.ShapeDtypeStruct((SUBLANES, n), F32),
        compiler_params=_cparams(("arbitrary",)),
        name="modulation",
    )(cc, w_mod, b_mod)


Q_SCALE = HD ** -0.5 * math.log2(math.e)


def _rms_mod(x, g, sh, sc):
    ms = jnp.mean(x * x, axis=-1, keepdims=True)
    return (x * lax.rsqrt(ms + EPS) * g) * (1.0 + sc) + sh


def _head_mean_square(t, bd):
    hi, lo = _split(t * t)
    return _dot(hi, bd) + _dot(lo, bd)


def _head_norm_rope(t, ms, g, cos, sin):
    tn = t * lax.rsqrt(ms + EPS) * g
    lane = lax.broadcasted_iota(I32, tn.shape, 1)
    sw = jnp.where((lane & 31) < 16, pltpu.roll(tn, LANES - 16, 1), pltpu.roll(tn, 16, 1))
    return tn * cos + sw * sin


def _proj_body(x_ref, g_ref, sh_ref, sc_ref, w_ref, gq_ref, gk_ref, bd_ref, cos_ref, sin_ref,
               q_ref, k_ref, v_ref, p_ref):
    h = _rms_mod(x_ref[0], g_ref[...], sh_ref[0], sc_ref[0])
    proj = _dot(h.astype(BF16), w_ref[...])
    bd = bd_ref[...]
    cos = cos_ref[...]
    sin = sin_ref[...]
    wide = 2 * LANES
    for j in range(ATT_W // wide):
        ms = _head_mean_square(proj[:, j * wide:(j + 1) * wide], bd)
        for i in range(2):
            sl = slice(j * wide + i * LANES, j * wide + (i + 1) * LANES)
            qj = _head_norm_rope(proj[:, sl], ms[:, i * LANES:(i + 1) * LANES], gq_ref[...], cos, sin)
            q_ref[0, :, sl] = (qj * Q_SCALE).astype(BF16)
    ms = _head_mean_square(proj[:, ATT_W:ATT_W + 2 * KV_W], bd)
    kk = _head_norm_rope(proj[:, ATT_W:ATT_W + KV_W], ms[:, 0:KV_W], gk_ref[...], cos, sin)
    vt = proj[:, ATT_W + KV_W:ATT_W + 2 * KV_W].T
    for g in range(NKV):
        k_ref[0, g] = kk[:, g * HD:(g + 1) * HD].astype(BF16)
        v_ref[0, g] = vt[g * HD:(g + 1) * HD, :].astype(BF16)
    p_ref[0] = proj[:, ATT_W + 2 * KV_W:]


def _in_projection(x, g1, sh, sc, w_in_bf, gq2, gk2, bd, cos_t, sin_t, tm):
    b, s, _ = x.shape
    row = lambda bi, i: (bi, 0, 0)
    tok = lambda bi, i: (bi, i, 0)
    const = lambda bi, i: (0, 0)
    return pl.pallas_call(
        _proj_body,
        grid=(b, s // tm),
        in_specs=[pl.BlockSpec((1, tm, D), tok),
                  pl.BlockSpec((1, D), const),
                  pl.BlockSpec((1, 1, D), row),
                  pl.BlockSpec((1, 1, D), row),
                  pl.BlockSpec((D, IN_W), const),
                  pl.BlockSpec((1, LANES), const),
                  pl.BlockSpec((1, LANES), const),
                  pl.BlockSpec((2 * LANES, 2 * LANES), const),
                  pl.BlockSpec((tm, LANES), lambda bi, i: (i, 0)),
                  pl.BlockSpec((tm, LANES), lambda bi, i: (i, 0))],
        out_specs=[pl.BlockSpec((1, tm, ATT_W), tok),
                   pl.BlockSpec((1, NKV, tm, HD), lambda bi, i: (bi, 0, i, 0)),
                   pl.BlockSpec((1, NKV, HD, tm), lambda bi, i: (bi, 0, 0, i)),
                   pl.BlockSpec((1, tm, 3 * HY_W), tok)],
        out_shape=[jax.ShapeDtypeStruct((b, s, ATT_W), BF16),
                   jax.ShapeDtypeStruct((b, NKV, s, HD), BF16),
                   jax.ShapeDtypeStruct((b, NKV, HD, s), BF16),
                   jax.ShapeDtypeStruct((b, s, 3 * HY_W), F32)],
        compiler_params=_cparams(("parallel", "parallel")),
        name="in_projection",
    )(x, g1, sh, sc, w_in_bf, gq2, gk2, bd, cos_t, sin_t)


ATT_TQ = 1024
ATT_TK = 768
SK = SEQ + CTX
ATT_NCH = SK // ATT_TK


ATT_NQ = QPK * ATT_TQ
ATT_VR = HD + BF16_ROWS
assert ATT_NCH % 2 == 1
ATT_SHIFT_MAX = 120.0


def _attn_body(bound_ref, q_ref, k_ref, vt_ref, o_ref, s_ref, mx_ref, m_ref, acc_ref):
    qall = jnp.concatenate([q_ref[0, :, r * HD:(r + 1) * HD] for r in range(QPK)], axis=0)
    acc_ref[...] = jnp.zeros_like(acc_ref)
    bound = bound_ref[0]
    fixed_shift = 2.0 * bound <= ATT_SHIFT_MAX

    def finish():
        out = acc_ref[0:HD, :] * (1.0 / acc_ref[HD:HD + 1, :])
        for r in range(QPK):
            o_ref[0, :, r * HD:(r + 1) * HD] = out[:, r * ATT_TQ:(r + 1) * ATT_TQ].T.astype(BF16)

    @pl.when(fixed_shift)
    def _():
        def chunk(c, _):
            s = _dot(k_ref[0, 0, c], qall, _NT)
            acc_ref[...] += _dot(vt_ref[0, 0, c], jnp.exp2(s - bound).astype(BF16))
            return 0

        lax.fori_loop(0, ATT_NCH, chunk, 0, unroll=True)
        finish()

    @pl.when(jnp.logical_not(fixed_shift))
    def _():
        m_ref[...] = jnp.full(m_ref.shape, -1e30, F32)

        def scores(c, slot):
            s = _dot(k_ref[0, 0, c], qall, _NT)
            s_ref[slot] = s
            mx_ref[slot] = jnp.max(s, axis=0, keepdims=True)

        def update(c, slot):
            m_old = m_ref[...]
            m_new = jnp.maximum(m_old, mx_ref[slot])
            p = jnp.exp2(s_ref[slot] - m_new).astype(BF16)
            acc_ref[...] = jnp.exp2(m_old - m_new) * acc_ref[...] + _dot(vt_ref[0, 0, c], p)
            m_ref[...] = m_new

        scores(0, 0)

        def pair(i, _):
            c = 2 * i
            scores(c + 1, 1)
            update(c, 0)
            scores(c + 2, 0)
            update(c + 1, 1)
            return 0

        lax.fori_loop(0, ATT_NCH // 2, pair, 0)
        update(ATT_NCH - 1, 0)
        finish()


def _attention(bound, q, kch, vtch):
    b = q.shape[0]
    grid_spec = pltpu.PrefetchScalarGridSpec(
        num_scalar_prefetch=1,
        grid=(b, NKV, SEQ // ATT_TQ),
        in_specs=[pl.BlockSpec((1, ATT_TQ, QPK * HD), lambda bi, g, i, bd: (bi, i, g)),
                  pl.BlockSpec((1, 1, ATT_NCH, ATT_TK, HD), lambda bi, g, i, bd: (bi, g, 0, 0, 0)),
                  pl.BlockSpec((1, 1, ATT_NCH, ATT_VR, ATT_TK), lambda bi, g, i, bd: (bi, g, 0, 0, 0))],
        out_specs=pl.BlockSpec((1, ATT_TQ, QPK * HD), lambda bi, g, i, bd: (bi, i, g)),
        scratch_shapes=[pltpu.VMEM((2, ATT_TK, ATT_NQ), F32), pltpu.VMEM((2, 1, ATT_NQ), F32),
                        pltpu.VMEM((1, ATT_NQ), F32), pltpu.VMEM((ATT_VR, ATT_NQ), F32)],
    )
    return pl.pallas_call(
        _attn_body,
        grid_spec=grid_spec,
        out_shape=jax.ShapeDtypeStruct((b, SEQ, ATT_W), BF16),
        compiler_params=_cparams(("parallel", "parallel", "parallel")),
        name="attention",
    )(bound, q, kch, vtch)


SC_TM = 2048
SC_J = SC_TM // FN2


def _sconv_body(m1, a1, n1, m2, a2, n2, m3, a3, n3, w_ref, b_ref, u_ref, x2_ref):
    i = pl.program_id(1)
    last = pl.num_programs(1) - 1
    rows = lax.broadcasted_iota(I32, (SC_TM, HY_W), 0)

    def conv(main, prev, nxt, g):
        x = main[0]
        pr = jnp.where(i > 0, prev[0, SUBLANES - 1:SUBLANES, :], 0.0)
        nx = jnp.where(i < last, nxt[0, 0:1, :], 0.0)
        xm = jnp.where(rows == 0, pr, pltpu.roll(x, 1, 0))
        xp = jnp.where(rows == SC_TM - 1, nx, pltpu.roll(x, SC_TM - 1, 0))
        return (w_ref[g:g + 1, :] * xm + w_ref[3 + g:4 + g, :] * x + w_ref[6 + g:7 + g, :] * xp
                + b_ref[g:g + 1, :])

    def to_rj(t):
        return jnp.swapaxes(t.reshape(SC_J, FN2, HY_W), 0, 1).astype(BF16)

    x1 = conv(m1, a1, n1, 0)
    x2 = conv(m2, a2, n2, 1)
    v = conv(m3, a3, n3, 2)
    u_ref[0] = to_rj(v * x1)
    x2_ref[0] = to_rj(x2)


def _short_conv(p, cw9, cb3):
    b = p.shape[0]
    nblk8 = SEQ // SUBLANES
    step8 = SC_TM // SUBLANES
    specs = []
    for g in range(3):
        specs += [pl.BlockSpec((1, SC_TM, HY_W), lambda bi, i, g=g: (bi, i, g)),
                  pl.BlockSpec((1, SUBLANES, HY_W), lambda bi, i, g=g: (bi, jnp.maximum(i * step8 - 1, 0), g)),
                  pl.BlockSpec((1, SUBLANES, HY_W), lambda bi, i, g=g: (bi, jnp.minimum((i + 1) * step8, nblk8 - 1), g))]
    specs += [pl.BlockSpec((9, HY_W), lambda bi, i: (0, 0)), pl.BlockSpec((3, HY_W), lambda bi, i: (0, 0))]
    out = pl.BlockSpec((1, FN2, SC_J, HY_W), lambda bi, i: (bi, 0, i, 0))
    return pl.pallas_call(
        _sconv_body,
        grid=(b, SEQ // SC_TM),
        in_specs=specs,
        out_specs=[out, out],
        out_shape=[jax.ShapeDtypeStruct((b, FN2, SEQ // FN2, HY_W), BF16)] * 2,
        compiler_params=_cparams(("parallel", "parallel")),
        name="short_conv",
    )(p, p, p, p, p, p, p, p, p, cw9, cb3)


FILT_TR = 1024


def _filter_body(z_ref, t_ref, msk_ref, fwd_ref, w1_ref, b1_ref, w2_ref, b2_ref, w3_ref, fr_ref, dl_ref,
                 k_ref, s_ref):
    fr = fr_ref[...]
    h = jnp.sin(fr * (_dot3(z_ref[...], w1_ref[...]) + b1_ref[...]))
    h = jnp.sin(fr * (_dot3(h, w2_ref[...]) + b2_ref[...]))
    h = _dot3(h, w3_ref[...])
    h = jnp.concatenate([h[:, :2 * HY_W], h[:, 2 * HY_W:]], axis=0)
    h = jnp.where(fwd_ref[...] > 0.5, h[:, :HY_W], h[:, HY_W:])
    kern = h * jnp.exp(t_ref[...] * dl_ref[...]) * msk_ref[...]
    k_ref[...] = kern

    @pl.when(pl.program_id(0) == 0)
    def _():
        s_ref[...] = jnp.zeros_like(s_ref)

    s_ref[...] += jnp.sum(jnp.abs(kern), axis=0, keepdims=True)


def _implicit_filter(ztab, ttab, mtab, ftab, w1p, b1, w2, b2, w3, freq, negdelta):
    rowblk = lambda i: (i, 0)
    const = lambda i: (0, 0)
    col = pl.BlockSpec((FILT_TR, 1), rowblk)
    return pl.pallas_call(
        _filter_body,
        grid=(FN // FILT_TR,),
        in_specs=[pl.BlockSpec((FILT_TR // 2, 2 * FORD), rowblk), col, col, col,
                  pl.BlockSpec((2 * FORD, 2 * FORD), const),
                  pl.BlockSpec((1, 2 * FORD), const),
                  pl.BlockSpec((2 * FORD, 2 * FORD), const),
                  pl.BlockSpec((1, 2 * FORD), const),
                  pl.BlockSpec((2 * FORD, 4 * HY_W), const),
                  pl.BlockSpec((1, 2 * FORD), const),
                  pl.BlockSpec((1, HY_W), const)],
        out_specs=[pl.BlockSpec((FILT_TR, HY_W), rowblk),
                   pl.BlockSpec((1, HY_W), const)],
        out_shape=[jax.ShapeDtypeStruct((FN, HY_W), F32), jax.ShapeDtypeStruct((1, HY_W), F32)],
        compiler_params=_cparams(("arbitrary",)),
        name="implicit_filter",
    )(ztab, ttab, mtab, ftab, w1p, b1, w2, b2, w3, freq, negdelta)


DFT_G = 4
DFT_KB = 16
DFT_NP = FN1 // DFT_KB
DFT_UNROLL = 8


def _dot2c(fh, fl, zb):
    return _dot(fh, zb) + _dot(fl, zb)


def _lead_stage(src, fh, fl, dst_ref):
    def group(rg, _):
        r0 = rg * DFT_G
        rhs = jnp.concatenate([src(r0 + g) for g in range(DFT_G)], axis=1)
        blk = _dot2c(fh, fl, rhs)
        for g in range(DFT_G):
            dst_ref[r0 + g] = blk[:, g * LANES:(g + 1) * LANES]
        return 0

    lax.fori_loop(0, FN2 // DFT_G, group, 0, unroll=DFT_UNROLL)


def _lead_phase(src, lh_ref, ll_ref, p_ref, q_ref):
    for h in range(2):
        rows = slice(h * FN1, (h + 1) * FN1)
        _lead_stage(src, lh_ref[rows, :], ll_ref[rows, :], p_ref)
        q_ref[rows] = jnp.swapaxes(p_ref[...], 0, 1)


def _lane_cat(xs):
    return jnp.concatenate(xs, axis=1)


def _mid_forward(q_ref, k0, tr_ref, ti_ref, fh, fl, half):
    tr_t, ti_t = tr_ref[...].T, ti_ref[...].T
    brs, bis, trs, tis = [], [], [], []
    for g in range(DFT_G):
        jj = half * DFT_G + g
        ar, ai = q_ref[k0 + jj], q_ref[FN1 + k0 + jj]
        tr = jnp.broadcast_to(tr_t[:, jj:jj + 1], (FN2, LANES))
        ti = jnp.broadcast_to(ti_t[:, jj:jj + 1], (FN2, LANES))
        brs.append(ar * tr - ai * ti)
        bis.append(ar * ti + ai * tr)
        trs.append(tr)
        tis.append(ti)
    b = jnp.concatenate([_lane_cat(brs), _lane_cat(bis)], axis=0).astype(BF16)
    return _dot2c(fh, fl, b), _lane_cat(trs), _lane_cat(tis)


def _spectrum_body(k_ref, tr_ref, ti_ref, lh_ref, ll_ref, fh_ref, fl_ref, sc_ref, o_ref, p_ref, q_ref):
    ph = pl.program_id(1)

    @pl.when(ph == 0)
    def _():
        _lead_phase(lambda r: k_ref[r].astype(BF16), lh_ref, ll_ref, p_ref, q_ref)

    @pl.when(ph > 0)
    def _():
        k0 = (ph - 1) * DFT_KB
        sc = _lane_cat([sc_ref[...]] * DFT_G)
        for half in range(DFT_KB // DFT_G):
            x, _, _ = _mid_forward(q_ref, k0, tr_ref, ti_ref, fh_ref[...], fl_ref[...], half)
            x = x * sc
            for g in range(DFT_G):
                lanes = slice(g * LANES, (g + 1) * LANES)
                o_ref[0, half * DFT_G + g] = x[:FN2, lanes]
                o_ref[1, half * DFT_G + g] = x[FN2:, lanes]


def _mid_index(ph):
    return jnp.clip(ph - 1, 0, DFT_NP - 1)


def _filter_spectrum(kern_rj, twr, twi, lh, ll, fh, fl, scale):
    const = lambda c, ph: (0, 0)
    tw = pl.BlockSpec((DFT_KB, FN2), lambda c, ph: (_mid_index(ph), 0))
    return pl.pallas_call(
        _spectrum_body,
        grid=(HY_W // LANES, DFT_NP + 1),
        in_specs=[pl.BlockSpec((FN2, FN1, LANES), lambda c, ph: (0, 0, c)),
                  tw, tw,
                  pl.BlockSpec(lh.shape, const), pl.BlockSpec(ll.shape, const),
                  pl.BlockSpec(fh.shape, const), pl.BlockSpec(fl.shape, const),
                  pl.BlockSpec((1, LANES), lambda c, ph: (0, c))],
        out_specs=pl.BlockSpec((2, DFT_KB, FN2, LANES), lambda c, ph: (0, _mid_index(ph), 0, c)),
        out_shape=jax.ShapeDtypeStruct((2, FN1, FN2, HY_W), F32),
        scratch_shapes=[pltpu.VMEM((FN2, FN1, LANES), F32), pltpu.VMEM((2 * FN1, FN2, LANES), F32)],
        compiler_params=_cparams(("parallel", "arbitrary")),
        name="filter_spectrum",
    )(kern_rj, twr, twi, lh, ll, fh, fl, scale)


def _hconv_body(u_ref, x2_ref, kh_ref, tr_ref, ti_ref, lh_ref, ll_ref, fh_ref, fl_ref, gh_ref, gl_ref,
                oh_ref, ol_ref, bias_ref, o_ref, p_ref, q_ref):
    ph = pl.program_id(1)

    def both(ref, r):
        return jnp.concatenate([ref[0, r], ref[1, r]], axis=0)

    @pl.when(ph == 0)
    def _():
        _lead_phase(lambda r: both(u_ref, r), lh_ref, ll_ref, p_ref, q_ref)

    @pl.when((ph > 0) & (ph <= DFT_NP))
    def _():
        k0 = (ph - 1) * DFT_KB
        for half in range(DFT_KB // DFT_G):
            x, tr, ti = _mid_forward(q_ref, k0, tr_ref, ti_ref, fh_ref[...], fl_ref[...], half)
            xr, xi = x[:FN2], x[FN2:]
            kr = _lane_cat([kh_ref[0, half * DFT_G + g] for g in range(DFT_G)])
            ki = _lane_cat([kh_ref[1, half * DFT_G + g] for g in range(DFT_G)])
            y = jnp.concatenate([xr * kr - xi * ki, xr * ki + xi * kr], axis=0).astype(BF16)
            c = _dot2c(gh_ref[...], gl_ref[...], y)
            cr, ci = c[:FN2], c[FN2:]
            dr = cr * tr + ci * ti
            di = ci * tr - cr * ti
            for g in range(DFT_G):
                lanes = slice(g * LANES, (g + 1) * LANES)
                q_ref[k0 + half * DFT_G + g] = dr[:, lanes]
                q_ref[FN1 + k0 + half * DFT_G + g] = di[:, lanes]

    @pl.when(ph == DFT_NP + 1)
    def _():
        bias = bias_ref[...]
        p_ref[...] = jnp.swapaxes(q_ref[0:FN1], 0, 1)
        _lead_stage(lambda r: p_ref[r].astype(BF16), oh_ref[:, 0:FN1], ol_ref[:, 0:FN1], q_ref)
        p_ref[...] = jnp.swapaxes(q_ref[FN1:2 * FN1], 0, 1)
        oh2, ol2 = oh_ref[:, FN1:2 * FN1], ol_ref[:, FN1:2 * FN1]

        def group(rg, _):
            r0 = rg * DFT_G
            rhs = _lane_cat([p_ref[r0 + g].astype(BF16) for g in range(DFT_G)])
            blk = _dot2c(oh2, ol2, rhs)
            for g in range(DFT_G):
                r = r0 + g
                y = q_ref[r] + blk[:, g * LANES:(g + 1) * LANES]
                q_ref[r] = (y + both(u_ref, r).astype(F32) * bias) * both(x2_ref, r).astype(F32)
            return 0

        lax.fori_loop(0, FN2 // DFT_G, group, 0, unroll=DFT_UNROLL)
        p_ref[...] = jnp.swapaxes(q_ref[0:FN2], 0, 1)
        nj = SEQ // FN2
        for b in range(2):
            o_ref[b] = p_ref[b * nj:(b + 1) * nj].reshape(SEQ, LANES).astype(BF16)


def _hyena_conv(u_rj, x2_rj, khat, twr, twi, lead, fwd, inv, out, bias):
    const = lambda c, ph: (0, 0)
    nj = SEQ // FN2
    tw = pl.BlockSpec((DFT_KB, FN2), lambda c, ph: (_mid_index(ph), 0))
    sig = pl.BlockSpec((2, FN2, nj, LANES), lambda c, ph: (0, 0, 0, c))
    mats = [m for pair in (lead, fwd, inv, out) for m in pair]
    return pl.pallas_call(
        _hconv_body,
        grid=(HY_W // LANES, DFT_NP + 2),
        in_specs=[sig, sig,
                  pl.BlockSpec((2, DFT_KB, FN2, LANES), lambda c, ph: (0, _mid_index(ph), 0, c)),
                  tw, tw] + [pl.BlockSpec(m.shape, const) for m in mats]
                 + [pl.BlockSpec((1, LANES), lambda c, ph: (0, c))],
        out_specs=pl.BlockSpec((2, SEQ, LANES), lambda c, ph: (0, 0, c)),
        out_shape=jax.ShapeDtypeStruct((2, SEQ, HY_W), BF16),
        scratch_shapes=[pltpu.VMEM((FN2, FN1, LANES), F32), pltpu.VMEM((2 * FN1, FN2, LANES), F32)],
        compiler_params=_cparams(("parallel", "arbitrary")),
        name="hyena_conv",
    )(u_rj, x2_rj, khat, twr, twi, *mats, bias)


OP_TM = 1024


def _outproj_body(att_ref, hy_ref, x_ref, w_ref, g1_ref, n2_ref, sh_ref, sc_ref, wr_ref,
                  x1_ref, h2_ref, lg_ref):
    a = jnp.concatenate([att_ref[0], hy_ref[0]], axis=1)
    x1 = x_ref[0] + g1_ref[0] * _dot(a, w_ref[...])
    x1_ref[0] = x1
    h2 = _rms_mod(x1, n2_ref[...], sh_ref[0], sc_ref[0])
    hh, hl = _split(h2)
    h2_ref[0] = hh
    wr = wr_ref[...]
    both = _dot(wr, hh, _NT)
    lg_ref[0] = both[0:NE] + both[NE:2 * NE] + _dot(wr[0:NE], hl, _NT)


def _out_projection(att, hy, x, w_out_bf, g1r, n2g, sh2, sc2, wr2):
    b = x.shape[0]
    tok = lambda bi, i: (bi, i, 0)
    row = lambda bi, i: (bi, 0, 0)
    const = lambda bi, i: (0, 0)
    return pl.pallas_call(
        _outproj_body,
        grid=(b, SEQ // OP_TM),
        in_specs=[pl.BlockSpec((1, OP_TM, ATT_W), tok),
                  pl.BlockSpec((1, OP_TM, HY_W), tok),
                  pl.BlockSpec((1, OP_TM, D), tok),
                  pl.BlockSpec((ATT_W + HY_W, D), const),
                  pl.BlockSpec((1, 1, D), row),
                  pl.BlockSpec((1, D), const),
                  pl.BlockSpec((1, 1, D), row),
                  pl.BlockSpec((1, 1, D), row),
                  pl.BlockSpec((2 * NE, D), const)],
        out_specs=[pl.BlockSpec((1, OP_TM, D), tok),
                   pl.BlockSpec((1, OP_TM, D), tok),
                   pl.BlockSpec((1, NE, OP_TM), lambda bi, i: (bi, 0, i))],
        out_shape=[jax.ShapeDtypeStruct((b, SEQ, D), F32),
                   jax.ShapeDtypeStruct((b, SEQ, D), BF16),
                   jax.ShapeDtypeStruct((b, NE, SEQ), F32)],
        compiler_params=_cparams(("parallel", "parallel")),
        name="out_projection",
    )(att, hy, x, w_out_bf, g1r, n2g, sh2, sc2, wr2)


def _routing_body(lg_ref, tri_ref, pos_ref, gate_ref, off_ref, cs_ref):
    lg = lg_ref[0]
    e = jnp.exp(lg - jnp.max(lg, axis=0, keepdims=True))
    aff = e / jnp.sum(e, axis=0, keepdims=True)
    gate_ref[0] = aff
    def count_ge(t):
        return jnp.sum(jnp.where(aff >= t, 1.0, 0.0), axis=1, keepdims=True)

    def bisect(i, thr):
        cand = thr | (jnp.int32(1) << (30 - i))
        return jnp.where(count_ge(pltpu.bitcast(cand, F32)) >= float(CAP), cand, thr)

    thr = lax.fori_loop(0, 31, bisect, jnp.zeros((NE, 1), I32))
    lo = pltpu.bitcast(thr, F32)
    hi = jnp.maximum(pltpu.bitcast(thr + 1, F32), jnp.finfo(F32).tiny)

    def refine(i, c):
        lo, hi = c
        mid = lo + (hi - lo) * 0.5
        ok = count_ge(mid) >= float(CAP)
        return jnp.where(ok, mid, lo), jnp.where(ok, hi, mid)

    lo, hi = lax.fori_loop(0, 32, refine, (lo, hi))
    gt = aff >= hi
    eq = (aff >= lo) & jnp.logical_not(gt)
    need = float(CAP) - jnp.sum(jnp.where(gt, 1.0, 0.0), axis=1, keepdims=True)
    tri = tri_ref[...]

    def excl_cumsum(mask_f, record_offsets):
        carry = jnp.zeros((NE, 1), F32)
        for c in range(NTCH):
            sl = slice(c * TCH, (c + 1) * TCH)
            m = mask_f[:, sl]
            inc = _dot(m.astype(BF16), tri)
            cs_ref[:, sl] = inc - m + carry
            if record_offsets:
                off_ref[0, :, c:c + 1] = carry.astype(I32)
            carry = carry + inc[:, TCH - 1:TCH]
        return cs_ref[...]

    eq_rank = excl_cumsum(jnp.where(eq, 1.0, 0.0), False)
    sel = gt | (eq & (eq_rank < need))
    pos = excl_cumsum(jnp.where(sel, 1.0, 0.0), True)
    pos_ref[0] = jnp.where(sel, pos.astype(I32), -1)


def _routing(logits, tri):
    b = logits.shape[0]
    blk = pl.BlockSpec((1, NE, SEQ), lambda bi: (bi, 0, 0))
    return pl.pallas_call(
        _routing_body,
        grid=(b,),
        in_specs=[blk, pl.BlockSpec((TCH, TCH), lambda bi: (0, 0))],
        out_specs=[blk, blk, pl.BlockSpec((1, NE, NTCH), lambda bi: (bi, 0, 0))],
        out_shape=[jax.ShapeDtypeStruct((b, NE, SEQ), I32),
                   jax.ShapeDtypeStruct((b, NE, SEQ), F32),
                   jax.ShapeDtypeStruct((b, NE, NTCH), I32)],
        scratch_shapes=[pltpu.VMEM((NE, SEQ), F32)],
        compiler_params=_cparams(("parallel",)),
        name="routing",
    )(logits, tri)


GATHER_UNROLL = 8


GW_SMALL = 64


def _gather_body(off_ref, h_ref, pos_ref, xg_ref, acc_ref):
    b = pl.program_id(0)
    e = pl.program_id(1)
    row0 = (b * NE + e) * NTCH
    acc_ref[...] = jnp.zeros_like(acc_ref)

    def count(c2, most):
        nxt = jnp.where(c2 + 1 < NTCH // 2, off_ref[row0 + jnp.minimum(2 * c2 + 2, NTCH - 1)], CAP)
        return jnp.maximum(most, nxt - off_ref[row0 + 2 * c2])

    most = lax.fori_loop(0, NTCH // 2, count, 0)

    def sweep(window, span):
        crow = lax.broadcasted_iota(I32, (window, span * TCH), 0)

        def chunks(i, _):
            for j in range(GATHER_UNROLL):
                c = (i * GATHER_UNROLL + j) * span
                off = off_ref[row0 + c]
                base = pl.multiple_of(jnp.minimum((off >> 3) << 3, CAP + SUBLANES - window), SUBLANES)
                t0 = pl.multiple_of(c * TCH, TCH)
                rel = _lane_cat([pos_ref[0, 0, pl.ds(c + s, 1), :] for s in range(span)]) - base
                onehot = jnp.where(crow == rel, 1.0, 0.0).astype(BF16)
                acc_ref[pl.ds(base, window), :] += _dot(onehot, h_ref[0, pl.ds(t0, span * TCH), :])
            return 0

        lax.fori_loop(0, NTCH // (GATHER_UNROLL * span), chunks, 0)

    @pl.when(most <= GW_SMALL - SUBLANES)
    def _():
        sweep(GW_SMALL, 2)

    @pl.when(most > GW_SMALL - SUBLANES)
    def _():
        sweep(GW, 1)

    xg_ref[0, 0] = acc_ref[0:CAP, :].astype(BF16)


def _gather(offs_flat, h2, pos4):
    b = h2.shape[0]
    grid_spec = pltpu.PrefetchScalarGridSpec(
        num_scalar_prefetch=1,
        grid=(b, NE),
        in_specs=[pl.BlockSpec((1, SEQ, D), lambda bi, e, off: (bi, 0, 0)),
                  pl.BlockSpec((1, 1, NTCH, TCH), lambda bi, e, off: (bi, e, 0, 0))],
        out_specs=pl.BlockSpec((1, 1, CAP, D), lambda bi, e, off: (bi, e, 0, 0)),
        scratch_shapes=[pltpu.VMEM((CAP + SUBLANES, D), F32)],
    )
    return pl.pallas_call(
        _gather_body,
        grid_spec=grid_spec,
        out_shape=jax.ShapeDtypeStruct((b, NE, CAP, D), BF16),
        compiler_params=_cparams(("parallel", "arbitrary")),
        name="moe_gather",
    )(offs_flat, h2, pos4)


FFN_TM = 512
FFN_NF = 4
FFN_FC = DEXP // FFN_NF
assert FFN_FC * FFN_NF == DEXP and FFN_FC % BF16_ROWS == 0


def _ffn_body(xg_ref, wgt_ref, wut_ref, wd_ref, y_ref, acc_ref):
    j = pl.program_id(1)
    nb = xg_ref.shape[0]

    @pl.when(j == 0)
    def _():
        acc_ref[...] = jnp.zeros_like(acc_ref)

    wgt = wgt_ref[0].astype(BF16)
    wut = wut_ref[0].astype(BF16)
    wd = wd_ref[0].astype(BF16)
    for b in range(nb):
        for mb in range(CAP // FFN_TM):
            rows = slice(mb * FFN_TM, (mb + 1) * FFN_TM)
            xb = xg_ref[b, 0, rows, :]
            a = _dot(xb, wgt, _NT)
            u = _dot(xb, wut, _NT)
            h = (a * (1.0 / (1.0 + jnp.exp(-a))) * u).astype(BF16)
            acc_ref[b, rows, :] += _dot(h, wd)

    @pl.when(j == FFN_NF - 1)
    def _():
        for b in range(nb):
            y_ref[b, 0, 0:CAP, :] = acc_ref[b].astype(BF16)
            y_ref[b, 0, CAP:YROWS, :] = jnp.zeros((YROWS - CAP, D), BF16)


def _expert_ffn(xg, w_gate_t, w_up_t, w_down):
    b = xg.shape[0]
    wblk = pl.BlockSpec((1, FFN_FC, D), lambda e, j: (e, j, 0))
    return pl.pallas_call(
        _ffn_body,
        grid=(NE, FFN_NF),
        in_specs=[pl.BlockSpec((b, 1, CAP, D), lambda e, j: (0, e, 0, 0)), wblk, wblk, wblk],
        out_specs=pl.BlockSpec((b, 1, YROWS, D), lambda e, j: (0, e, 0, 0)),
        out_shape=jax.ShapeDtypeStruct((b, NE, YROWS, D), BF16),
        scratch_shapes=[pltpu.VMEM((b, CAP, D), F32)],
        compiler_params=_cparams(("parallel", "arbitrary")),
        name="expert_ffn",
    )(xg, w_gate_t, w_up_t, w_down)


CW_SMALL = 64
CW_STACK = 2 * LANES // CW_SMALL


CMB_T = 4


def _combine_body(off_ref, y_ref, pos_ref, gate_ref, x1_ref, g2_ref, o_ref):
    b = pl.program_id(0)
    i = pl.program_id(1)

    def offset(e, c):
        return off_ref[(b * NE + e) * NTCH + c]

    def window(e, t, rows):
        lanes = slice(t * TCH, (t + 1) * TCH)
        base = pl.multiple_of(jnp.minimum((offset(e, i * CMB_T + t) >> 4) << 4, YROWS - rows), BF16_ROWS)
        rel = pos_ref[0, e:e + 1, lanes] - base
        crow = lax.broadcasted_iota(I32, (rows, TCH), 0)
        w = jnp.where(crow == rel, gate_ref[0, e:e + 1, lanes], 0.0).astype(BF16)
        return w, y_ref[0, e, pl.ds(base, rows), :]

    most = jnp.int32(0)
    for t in range(CMB_T):
        c = i * CMB_T + t
        for e in range(NE):
            nxt = jnp.where(c + 1 < NTCH, offset(e, jnp.minimum(c + 1, NTCH - 1)), CAP)
            most = jnp.maximum(most, nxt - offset(e, c))

    def finish(t, acc):
        rows = slice(t * TCH, (t + 1) * TCH)
        o_ref[0, rows, :] = x1_ref[0, rows, :] + g2_ref[0] * acc

    @pl.when(most <= CW_SMALL - BF16_ROWS)
    def _():
        for t in range(CMB_T):
            acc = jnp.zeros((TCH, D), F32)
            for e0 in range(0, NE, CW_STACK):
                ws, ys = zip(*[window(e, t, CW_SMALL) for e in range(e0, e0 + CW_STACK)])
                acc = acc + _dot(jnp.concatenate(ws, axis=0), jnp.concatenate(ys, axis=0), _TN)
            finish(t, acc)

    @pl.when(most > CW_SMALL - BF16_ROWS)
    def _():
        for t in range(CMB_T):
            acc = jnp.zeros((TCH, D), F32)
            for e in range(NE):
                w, yw = window(e, t, CW)
                acc = acc + _dot(w, yw, _TN)
            finish(t, acc)


def _combine(offs_flat, y, pos, gate, x1, g2r):
    b = x1.shape[0]
    grid_spec = pltpu.PrefetchScalarGridSpec(
        num_scalar_prefetch=1,
        grid=(b, NTCH // CMB_T),
        in_specs=[pl.BlockSpec((1, NE, YROWS, D), lambda bi, i, off: (bi, 0, 0, 0),
                               pipeline_mode=pl.Buffered(1)),
                  pl.BlockSpec((1, NE, CMB_T * TCH), lambda bi, i, off: (bi, 0, i)),
                  pl.BlockSpec((1, NE, CMB_T * TCH), lambda bi, i, off: (bi, 0, i)),
                  pl.BlockSpec((1, CMB_T * TCH, D), lambda bi, i, off: (bi, i, 0)),
                  pl.BlockSpec((1, 1, D), lambda bi, i, off: (bi, 0, 0))],
        out_specs=pl.BlockSpec((1, CMB_T * TCH, D), lambda bi, i, off: (bi, i, 0)),
    )
    return pl.pallas_call(
        _combine_body,
        grid_spec=grid_spec,
        out_shape=jax.ShapeDtypeStruct((b, SEQ, D), F32),
        compiler_params=_cparams(("parallel", "arbitrary")),
        name="moe_combine",
    )(offs_flat, y, pos, gate, x1, g2r)


def _np_split(m):
    hi = np.asarray(m, np.float64).astype(BF16)
    lo = (m - hi.astype(np.float64)).astype(BF16)
    return jnp.asarray(hi), jnp.asarray(lo)


@functools.lru_cache(maxsize=None)
def _dft_tables():
    a = np.arange(FN1, dtype=np.float64)
    ang = 2.0 * np.pi * np.outer(a, a) / FN1
    fr, fi = np.cos(ang), -np.sin(ang)
    half = SEQ // FN2
    lead_u = np.block([[fr[:, :half], -fi[:, :half]], [fi[:, :half], fr[:, :half]]])
    lead_k = np.concatenate([fr, fi], axis=0)
    fwd = np.block([[fr, -fi], [fi, fr]])
    inv = np.block([[fr, fi], [-fi, fr]])
    out = np.block([[fr[:half], fi[:half]], [-fi[:half], fr[:half]]])
    n2 = np.arange(FN2, dtype=np.float64)
    tw = 2.0 * np.pi * np.outer(a, n2) / FN
    twr = np.cos(tw).astype(np.float32)
    twi = (-np.sin(tw)).astype(np.float32)
    return dict(lead_u=lead_u, lead_k=lead_k, fwd=fwd, inv=inv, out=out, twr=twr, twi=twi)


@functools.lru_cache(maxsize=None)
def _filter_tables():
    L = SEQ
    n = np.arange(FN).reshape(FN1, FN2).T.reshape(-1)
    lag = np.where(n < L, n, FN - n)
    jc = np.minimum(lag, L - 1).astype(np.float64)
    t = (jc / (L - 1))[:, None]
    bands = (FEMB - 1) // 2
    w = 2.0 * np.pi * jc / L
    f = np.linspace(1e-4, bands - 1, bands)
    fw = w[:, None] * f[None, :]
    z = np.concatenate([t, np.cos(fw), -np.sin(fw), np.zeros((FN, FORD - FEMB))], axis=-1)
    mask = np.where(n == L, 0.0, 1.0)[:, None]
    fwd = np.where(n < L, 1.0, 0.0)[:, None]
    max_decay = math.log(DECAY_TARGET) / FAST_DECAY_PCT
    min_decay = math.log(DECAY_TARGET) / SLOW_DECAY_PCT
    negdelta = -np.abs(np.linspace(min_decay, max_decay, HY_W))[None, :]
    z = z.reshape(FN // FILT_TR, 2, FILT_TR // 2, FORD).transpose(0, 2, 1, 3).reshape(FN // 2, 2 * FORD)
    return tuple(np.asarray(a, np.float32) for a in (z, t, mask, fwd, negdelta))


@functools.lru_cache(maxsize=None)
def _rope_tables(n):
    rows = n // GRID_W
    row_id, col_id = np.meshgrid(np.arange(rows, dtype=np.float64), np.arange(GRID_W, dtype=np.float64), indexing="ij")
    quarter = HD // 4
    inv_freq = ROPE_THETA ** (-np.arange(quarter, dtype=np.float64) / quarter)
    ar = row_id.reshape(-1)[:, None] * inv_freq
    ac = col_id.reshape(-1)[:, None] * inv_freq
    cos = np.concatenate([np.cos(ar), np.cos(ar), np.cos(ac), np.cos(ac)], axis=-1)
    sin = np.concatenate([-np.sin(ar), np.sin(ar), -np.sin(ac), np.sin(ac)], axis=-1)
    reps = (1, LANES // HD)
    return np.tile(cos, reps).astype(np.float32), np.tile(sin, reps).astype(np.float32)


def _hyena_long_conv(u_rj, x2_rj, kern, abs_sum, bias):
    tb = _dft_tables()
    twr, twi = tb["twr"], tb["twi"]
    fwd = _np_split(tb["fwd"])
    scale = 1.0 / (abs_sum * float(FN))
    khat = _filter_spectrum(kern.reshape(FN2, FN1, HY_W), twr, twi, *_np_split(tb["lead_k"]), *fwd, scale)
    return _hyena_conv(u_rj, x2_rj, khat, twr, twi, _np_split(tb["lead_u"]), fwd, _np_split(tb["inv"]),
                       _np_split(tb["out"]), bias.reshape(1, HY_W))


def kernel(x, c, ctx, c_ctx, w_mod, b_mod, norm1_g, norm2_g, w_in, w_out, q_norm_g, k_norm_g,
           conv_w, conv_b, filt_w1, filt_b1, filt_w2, filt_b2, filt_w3, filt_freq, hyena_bias,
           w_router, w_gate, w_up, w_down):
    B = x.shape[0]
    assert x.shape == (B, SEQ, D) and B == 2 and ctx.shape == (B, CTX, D) and w_mod.shape[0] == 1
    l = 0

    cc = jnp.concatenate([c, c_ctx[None, :], jnp.zeros((SUBLANES - B - 1, D), F32)], axis=0)
    mod = _modulation(cc, w_mod[l], b_mod[l][None, :])
    sh1, sc1, g1, sh2, sc2, g2 = [mod[:, i * D:(i + 1) * D] for i in range(6)]
    lat = lambda m: m[:B, None, :]
    ctxrow = lambda m: jnp.broadcast_to(m[B:B + 1, None, :], (B, 1, D))

    w_in_bf = w_in[l].astype(BF16)
    gq2 = jnp.tile(q_norm_g[l][None, :], (1, LANES // HD))
    gk2 = jnp.tile(k_norm_g[l][None, :], (1, LANES // HD))
    bd = jnp.asarray(np.kron(np.eye(2 * LANES // HD), np.full((HD, HD), 1.0 / HD)), BF16)
    cos_t, sin_t = _rope_tables(SEQ)
    n1g = norm1_g[l][None, :]

    q, k, vt, p = _in_projection(x, n1g, lat(sh1), lat(sc1), w_in_bf, gq2, gk2, bd, cos_t, sin_t, 512)
    _, kc, vct, _ = _in_projection(ctx, n1g, ctxrow(sh1), ctxrow(sc1), w_in_bf, gq2, gk2, bd,
                                   jnp.ones((CTX, LANES), F32), jnp.zeros((CTX, LANES), F32), CTX)

    kch = jnp.concatenate([k, kc], axis=2).reshape(B, NKV, ATT_NCH, ATT_TK, HD)
    vt_all = jnp.concatenate([vt, vct], axis=3).reshape(B, NKV, HD, ATT_NCH, ATT_TK)
    ones_pad = jnp.concatenate([jnp.ones((B, NKV, ATT_NCH, 1, ATT_TK), BF16),
                                jnp.zeros((B, NKV, ATT_NCH, BF16_ROWS - 1, ATT_TK), BF16)], axis=3)
    vtch = jnp.concatenate([vt_all.transpose(0, 1, 3, 2, 4), ones_pad], axis=3)
    bound = (1.02 * HD * Q_SCALE) * jnp.max(jnp.abs(q_norm_g[l])) * jnp.max(jnp.abs(k_norm_g[l]))
    att = _attention(bound.reshape(1).astype(F32), q, kch, vtch)

    cw9 = conv_w[l].reshape(3, 3, HY_W).reshape(9, HY_W)
    cb3 = conv_b[l].reshape(3, HY_W)
    u_rj, x2_rj = _short_conv(p, cw9, cb3)
    ztab, ttab, mtab, ftab, negdelta = _filter_tables()
    w1p = jnp.concatenate([filt_w1[l], jnp.zeros((FORD - FEMB, FORD), F32)], axis=0)
    twice = lambda w: jnp.kron(jnp.eye(2, dtype=F32), w)
    pair = lambda v: jnp.tile(v[None, :], (1, 2))
    kern, abs_sum = _implicit_filter(ztab, ttab, mtab, ftab, twice(w1p), pair(filt_b1[l]), twice(filt_w2[l]),
                                     pair(filt_b2[l]), twice(filt_w3[l]), pair(filt_freq[l]), negdelta)
    hy = _hyena_long_conv(u_rj, x2_rj, kern, abs_sum, hyena_bias[l])

    wr2 = jnp.concatenate(_split(w_router[l].T), axis=0)
    x1, h2, logits = _out_projection(att, hy, x, w_out[l].astype(BF16), lat(g1), norm2_g[l][None, :],
                                     lat(sh2), lat(sc2), wr2)

    tri = jnp.asarray(np.triu(np.ones((TCH, TCH))), BF16)
    pos, gate, offs = _routing(logits, tri)
    offs_flat = offs.reshape(-1)
    xg = _gather(offs_flat, h2, pos.reshape(B, NE, NTCH, TCH))
    y = _expert_ffn(xg, jnp.swapaxes(w_gate[l], 1, 2), jnp.swapaxes(w_up[l], 1, 2), w_down[l])
    return _combine(offs_flat, y, pos, gate, x1, lat(g2))
```

```python
import functools
import math

import numpy as np
import jax
import jax.numpy as jnp
from jax import lax
from jax.experimental import pallas as pl
from jax.experimental.pallas import tpu as pltpu

F32 = jnp.float32
BF16 = jnp.bfloat16
I32 = jnp.int32

D = 1024
SEQ = 8192
CTX = 256
GRID_W = 64
ATT_W = 512
HY_W = 512
HD = 64
NQ = 8
NKV = 2
QPK = NQ // NKV
KV_W = NKV * HD
IN_W = ATT_W + 2 * KV_W + 3 * HY_W
FEMB = 33
FORD = 64
NE = 16
CAP = 2 * SEQ // NE
DEXP = 2752
ROPE_THETA = 10000.0
EPS = 1e-6
DECAY_TARGET = 1e-2
FAST_DECAY_PCT = 0.3
SLOW_DECAY_PCT = 1.5

LANES = 128
SUBLANES = 8
BF16_ROWS = 16
VMEM_BYTES_V7X = 64 * 1024 * 1024
VMEM_LIMIT = VMEM_BYTES_V7X - 8 * 1024 * 1024

FN = 2 * SEQ
FN1 = 128
FN2 = 128

TCH = LANES
NTCH = SEQ // TCH
GW = TCH + SUBLANES
CW = TCH + BF16_ROWS
YROWS = CAP + BF16_ROWS


def _cparams(sem, vmem=None):
    return pltpu.CompilerParams(dimension_semantics=sem, vmem_limit_bytes=vmem or VMEM_LIMIT)


def _split(a):
    hi = a.astype(BF16)
    lo = (a - hi.astype(F32)).astype(BF16)
    return hi, lo


_NN = (((1,), (0,)), ((), ()))
_NT = (((1,), (1,)), ((), ()))
_TN = (((0,), (0,)), ((), ()))


def _dot(a, b, dn=_NN):
    return lax.dot_general(a, b, dn, preferred_element_type=F32)


def _dot3(a, b, dn=_NN):
    ah, al = _split(a)
    bh, bl = _split(b)
    return _dot(ah, bh, dn) + _dot(ah, bl, dn) + _dot(al, bh, dn)


def _mod_body(c_ref, w_ref, b_ref, o_ref):
    c = c_ref[...]
    s = c * (1.0 / (1.0 + jnp.exp(-c)))
    o_ref[...] = _dot3(s, w_ref[...]) + b_ref[...]


def _modulation(cc, w_mod, b_mod):
    n = w_mod.shape[1]
    return pl.pallas_call(
        _mod_body,
        grid=(n // D,),
        in_specs=[pl.BlockSpec((SUBLANES, D), lambda j: (0, 0)),
                  pl.BlockSpec((D, D), lambda j: (0, j)),
                  pl.BlockSpec((1, D), lambda j: (0, j))],
        out_specs=pl.BlockSpec((SUBLANES, D), lambda j: (0, j)),
        out_shape=jax.ShapeDtypeStruct((SUBLANES, n), F32),
        compiler_params=_cparams(("arbitrary",)),
        name="modulation",
    )(cc, w_mod, b_mod)


Q_SCALE = HD ** -0.5 * math.log2(math.e)


def _rms_mod(x, g, sh, sc):
    ms = jnp.mean(x * x, axis=-1, keepdims=True)
    return (x * lax.rsqrt(ms + EPS) * g) * (1.0 + sc) + sh


def _head_mean_square(t, bd):
    hi, lo = _split(t * t)
    return _dot(hi, bd) + _dot(lo, bd)


def _head_norm_rope(t, ms, g, cos, sin):
    tn = t * lax.rsqrt(ms + EPS) * g
    lane = lax.broadcasted_iota(I32, tn.shape, 1)
    sw = jnp.where((lane & 31) < 16, pltpu.roll(tn, LANES - 16, 1), pltpu.roll(tn, 16, 1))
    return tn * cos + sw * sin


def _proj_body(x_ref, g_ref, sh_ref, sc_ref, w_ref, gq_ref, gk_ref, bd_ref, cos_ref, sin_ref,
               q_ref, k_ref, v_ref, p_ref):
    h = _rms_mod(x_ref[0], g_ref[...], sh_ref[0], sc_ref[0])
    proj = _dot(h.astype(BF16), w_ref[...])
    bd = bd_ref[...]
    cos = cos_ref[...]
    sin = sin_ref[...]
    wide = 2 * LANES
    for j in range(ATT_W // wide):
        ms = _head_mean_square(proj[:, j * wide:(j + 1) * wide], bd)
        for i in range(2):
            sl = slice(j * wide + i * LANES, j * wide + (i + 1) * LANES)
            qj = _head_norm_rope(proj[:, sl], ms[:, i * LANES:(i + 1) * LANES], gq_ref[...], cos, sin)
            q_ref[0, :, sl] = (qj * Q_SCALE).astype(BF16)
    ms = _head_mean_square(proj[:, ATT_W:ATT_W + 2 * KV_W], bd)
    kk = _head_norm_rope(proj[:, ATT_W:ATT_W + KV_W], ms[:, 0:KV_W], gk_ref[...], cos, sin)
    vt = proj[:, ATT_W + KV_W:ATT_W + 2 * KV_W].T
    for g in range(NKV):
        k_ref[0, g] = kk[:, g * HD:(g + 1) * HD].astype(BF16)
        v_ref[0, g] = vt[g * HD:(g + 1) * HD, :].astype(BF16)
    p_ref[0] = proj[:, ATT_W + 2 * KV_W:]


def _in_projection(x, g1, sh, sc, w_in_bf, gq2, gk2, bd, cos_t, sin_t, tm):
    b, s, _ = x.shape
    row = lambda bi, i: (bi, 0, 0)
    tok = lambda bi, i: (bi, i, 0)
    const = lambda bi, i: (0, 0)
    return pl.pallas_call(
        _proj_body,
        grid=(b, s // tm),
        in_specs=[pl.BlockSpec((1, tm, D), tok),
                  pl.BlockSpec((1, D), const),
                  pl.BlockSpec((1, 1, D), row),
                  pl.BlockSpec((1, 1, D), row),
                  pl.BlockSpec((D, IN_W), const),
                  pl.BlockSpec((1, LANES), const),
                  pl.BlockSpec((1, LANES), const),
                  pl.BlockSpec((2 * LANES, 2 * LANES), const),
                  pl.BlockSpec((tm, LANES), lambda bi, i: (i, 0)),
                  pl.BlockSpec((tm, LANES), lambda bi, i: (i, 0))],
        out_specs=[pl.BlockSpec((1, tm, ATT_W), tok),
                   pl.BlockSpec((1, NKV, tm, HD), lambda bi, i: (bi, 0, i, 0)),
                   pl.BlockSpec((1, NKV, HD, tm), lambda bi, i: (bi, 0, 0, i)),
                   pl.BlockSpec((1, tm, 3 * HY_W), tok)],
        out_shape=[jax.ShapeDtypeStruct((b, s, ATT_W), BF16),
                   jax.ShapeDtypeStruct((b, NKV, s, HD), BF16),
                   jax.ShapeDtypeStruct((b, NKV, HD, s), BF16),
                   jax.ShapeDtypeStruct((b, s, 3 * HY_W), F32)],
        compiler_params=_cparams(("parallel", "parallel")),
        name="in_projection",
    )(x, g1, sh, sc, w_in_bf, gq2, gk2, bd, cos_t, sin_t)


ATT_TQ = 512
ATT_TK = 768
SK = SEQ + CTX
ATT_NCH = SK // ATT_TK


ATT_NQ = QPK * ATT_TQ
ATT_VR = HD + BF16_ROWS
assert ATT_NCH % 2 == 1
ATT_SHIFT_MAX = 120.0


def _attn_body(bound_ref, q_ref, k_ref, vt_ref, o_ref, s_ref, mx_ref, m_ref, acc_ref):
    qall = jnp.concatenate([q_ref[0, :, r * HD:(r + 1) * HD] for r in range(QPK)], axis=0)
    acc_ref[...] = jnp.zeros_like(acc_ref)
    bound = bound_ref[0]
    fixed_shift = 2.0 * bound <= ATT_SHIFT_MAX

    def finish():
        out = acc_ref[0:HD, :] * (1.0 / acc_ref[HD:HD + 1, :])
        for r in range(QPK):
            o_ref[0, :, r * HD:(r + 1) * HD] = out[:, r * ATT_TQ:(r + 1) * ATT_TQ].T.astype(BF16)

    @pl.when(fixed_shift)
    def _():
        def chunk(c, _):
            s = _dot(k_ref[0, 0, c], qall, _NT)
            acc_ref[...] += _dot(vt_ref[0, 0, c], jnp.exp2(s - bound).astype(BF16))
            return 0

        lax.fori_loop(0, ATT_NCH, chunk, 0, unroll=True)
        finish()

    @pl.when(jnp.logical_not(fixed_shift))
    def _():
        m_ref[...] = jnp.full(m_ref.shape, -1e30, F32)

        def scores(c, slot):
            s = _dot(k_ref[0, 0, c], qall, _NT)
            s_ref[slot] = s
            mx_ref[slot] = jnp.max(s, axis=0, keepdims=True)

        def update(c, slot):
            m_old = m_ref[...]
            m_new = jnp.maximum(m_old, mx_ref[slot])
            p = jnp.exp2(s_ref[slot] - m_new).astype(BF16)
            acc_ref[...] = jnp.exp2(m_old - m_new) * acc_ref[...] + _dot(vt_ref[0, 0, c], p)
            m_ref[...] = m_new

        scores(0, 0)

        def pair(i, _):
            c = 2 * i
            scores(c + 1, 1)
            update(c, 0)
            scores(c + 2, 0)
            update(c + 1, 1)
            return 0

        lax.fori_loop(0, ATT_NCH // 2, pair, 0)
        update(ATT_NCH - 1, 0)
        finish()


def _attention(bound, q, kch, vtch):
    b = q.shape[0]
    grid_spec = pltpu.PrefetchScalarGridSpec(
        num_scalar_prefetch=1,
        grid=(b, NKV, SEQ // ATT_TQ),
        in_specs=[pl.BlockSpec((1, ATT_TQ, QPK * HD), lambda bi, g, i, bd: (bi, i, g)),
                  pl.BlockSpec((1, 1, ATT_NCH, ATT_TK, HD), lambda bi, g, i, bd: (bi, g, 0, 0, 0)),
                  pl.BlockSpec((1, 1, ATT_NCH, ATT_VR, ATT_TK), lambda bi, g, i, bd: (bi, g, 0, 0, 0))],
        out_specs=pl.BlockSpec((1, ATT_TQ, QPK * HD), lambda bi, g, i, bd: (bi, i, g)),
        scratch_shapes=[pltpu.VMEM((2, ATT_TK, ATT_NQ), F32), pltpu.VMEM((2, 1, ATT_NQ), F32),
                        pltpu.VMEM((1, ATT_NQ), F32), pltpu.VMEM((ATT_VR, ATT_NQ), F32)],
    )
    return pl.pallas_call(
        _attn_body,
        grid_spec=grid_spec,
        out_shape=jax.ShapeDtypeStruct((b, SEQ, ATT_W), BF16),
        compiler_params=_cparams(("parallel", "parallel", "parallel")),
        name="attention",
    )(bound, q, kch, vtch)


SC_TM = 2048
SC_J = SC_TM // FN2


def _sconv_body(m1, a1, n1, m2, a2, n2, m3, a3, n3, w_ref, b_ref, u_ref, x2_ref):
    i = pl.program_id(1)
    last = pl.num_programs(1) - 1
    rows = lax.broadcasted_iota(I32, (SC_TM, HY_W), 0)

    def conv(main, prev, nxt, g):
        x = main[0]
        pr = jnp.where(i > 0, prev[0, SUBLANES - 1:SUBLANES, :], 0.0)
        nx = jnp.where(i < last, nxt[0, 0:1, :], 0.0)
        xm = jnp.where(rows == 0, pr, pltpu.roll(x, 1, 0))
        xp = jnp.where(rows == SC_TM - 1, nx, pltpu.roll(x, SC_TM - 1, 0))
        return (w_ref[g:g + 1, :] * xm + w_ref[3 + g:4 + g, :] * x + w_ref[6 + g:7 + g, :] * xp
                + b_ref[g:g + 1, :])

    def to_rj(t):
        return jnp.swapaxes(t.reshape(SC_J, FN2, HY_W), 0, 1).astype(BF16)

    x1 = conv(m1, a1, n1, 0)
    x2 = conv(m2, a2, n2, 1)
    v = conv(m3, a3, n3, 2)
    u_ref[0] = to_rj(v * x1)
    x2_ref[0] = to_rj(x2)


def _short_conv(p, cw9, cb3):
    b = p.shape[0]
    nblk8 = SEQ // SUBLANES
    step8 = SC_TM // SUBLANES
    specs = []
    for g in range(3):
        specs += [pl.BlockSpec((1, SC_TM, HY_W), lambda bi, i, g=g: (bi, i, g)),
                  pl.BlockSpec((1, SUBLANES, HY_W), lambda bi, i, g=g: (bi, jnp.maximum(i * step8 - 1, 0), g)),
                  pl.BlockSpec((1, SUBLANES, HY_W), lambda bi, i, g=g: (bi, jnp.minimum((i + 1) * step8, nblk8 - 1), g))]
    specs += [pl.BlockSpec((9, HY_W), lambda bi, i: (0, 0)), pl.BlockSpec((3, HY_W), lambda bi, i: (0, 0))]
    out = pl.BlockSpec((1, FN2, SC_J, HY_W), lambda bi, i: (bi, 0, i, 0))
    return pl.pallas_call(
        _sconv_body,
        grid=(b, SEQ // SC_TM),
        in_specs=specs,
        out_specs=[out, out],
        out_shape=[jax.ShapeDtypeStruct((b, FN2, SEQ // FN2, HY_W), BF16)] * 2,
        compiler_params=_cparams(("parallel", "parallel")),
        name="short_conv",
    )(p, p, p, p, p, p, p, p, p, cw9, cb3)


FILT_TR = 1024


def _filter_body(z_ref, t_ref, msk_ref, fwd_ref, w1_ref, b1_ref, w2_ref, b2_ref, w3_ref, fr_ref, dl_ref,
                 k_ref, s_ref):
    fr = fr_ref[...]
    h = jnp.sin(fr * (_dot3(z_ref[...], w1_ref[...]) + b1_ref[...]))
    h = jnp.sin(fr * (_dot3(h, w2_ref[...]) + b2_ref[...]))
    h = _dot3(h, w3_ref[...])
    h = jnp.concatenate([h[:, :2 * HY_W], h[:, 2 * HY_W:]], axis=0)
    h = jnp.where(fwd_ref[...] > 0.5, h[:, :HY_W], h[:, HY_W:])
    kern = h * jnp.exp(t_ref[...] * dl_ref[...]) * msk_ref[...]
    k_ref[...] = kern

    @pl.when(pl.program_id(0) == 0)
    def _():
        s_ref[...] = jnp.zeros_like(s_ref)

    s_ref[...] += jnp.sum(jnp.abs(kern), axis=0, keepdims=True)


def _implicit_filter(ztab, ttab, mtab, ftab, w1p, b1, w2, b2, w3, freq, negdelta):
    rowblk = lambda i: (i, 0)
    const = lambda i: (0, 0)
    col = pl.BlockSpec((FILT_TR, 1), rowblk)
    return pl.pallas_call(
        _filter_body,
        grid=(FN // FILT_TR,),
        in_specs=[pl.BlockSpec((FILT_TR // 2, 2 * FORD), rowblk), col, col, col,
                  pl.BlockSpec((2 * FORD, 2 * FORD), const),
                  pl.BlockSpec((1, 2 * FORD), const),
                  pl.BlockSpec((2 * FORD, 2 * FORD), const),
                  pl.BlockSpec((1, 2 * FORD), const),
                  pl.BlockSpec((2 * FORD, 4 * HY_W), const),
                  pl.BlockSpec((1, 2 * FORD), const),
                  pl.BlockSpec((1, HY_W), const)],
        out_specs=[pl.BlockSpec((FILT_TR, HY_W), rowblk),
                   pl.BlockSpec((1, HY_W), const)],
        out_shape=[jax.ShapeDtypeStruct((FN, HY_W), F32), jax.ShapeDtypeStruct((1, HY_W), F32)],
        compiler_params=_cparams(("arbitrary",)),
        name="implicit_filter",
    )(ztab, ttab, mtab, ftab, w1p, b1, w2, b2, w3, freq, negdelta)


DFT_G = 4
DFT_KB = 16
DFT_NP = FN1 // DFT_KB
DFT_UNROLL = 8


def _dot2c(fh, fl, zb):
    return _dot(fh, zb) + _dot(fl, zb)


def _lead_stage(src, fh, fl, dst_ref):
    def group(rg, _):
        r0 = rg * DFT_G
        rhs = jnp.concatenate([src(r0 + g) for g in range(DFT_G)], axis=1)
        blk = _dot2c(fh, fl, rhs)
        for g in range(DFT_G):
            dst_ref[r0 + g] = blk[:, g * LANES:(g + 1) * LANES]
        return 0

    lax.fori_loop(0, FN2 // DFT_G, group, 0, unroll=DFT_UNROLL)


def _lead_phase(src, lh_ref, ll_ref, p_ref, q_ref):
    for h in range(2):
        rows = slice(h * FN1, (h + 1) * FN1)
        _lead_stage(src, lh_ref[rows, :], ll_ref[rows, :], p_ref)
        q_ref[rows] = jnp.swapaxes(p_ref[...], 0, 1)


def _lane_cat(xs):
    return jnp.concatenate(xs, axis=1)


def _mid_forward(q_ref, k0, tr_ref, ti_ref, fh, fl, half):
    tr_t, ti_t = tr_ref[...].T, ti_ref[...].T
    brs, bis, trs, tis = [], [], [], []
    for g in range(DFT_G):
        jj = half * DFT_G + g
        ar, ai = q_ref[k0 + jj], q_ref[FN1 + k0 + jj]
        tr = jnp.broadcast_to(tr_t[:, jj:jj + 1], (FN2, LANES))
        ti = jnp.broadcast_to(ti_t[:, jj:jj + 1], (FN2, LANES))
        brs.append(ar * tr - ai * ti)
        bis.append(ar * ti + ai * tr)
        trs.append(tr)
        tis.append(ti)
    b = jnp.concatenate([_lane_cat(brs), _lane_cat(bis)], axis=0).astype(BF16)
    return _dot2c(fh, fl, b), _lane_cat(trs), _lane_cat(tis)


def _spectrum_body(k_ref, tr_ref, ti_ref, lh_ref, ll_ref, fh_ref, fl_ref, sc_ref, o_ref, p_ref, q_ref):
    ph = pl.program_id(1)

    @pl.when(ph == 0)
    def _():
        _lead_phase(lambda r: k_ref[r].astype(BF16), lh_ref, ll_ref, p_ref, q_ref)

    @pl.when(ph > 0)
    def _():
        k0 = (ph - 1) * DFT_KB
        sc = _lane_cat([sc_ref[...]] * DFT_G)
        for half in range(DFT_KB // DFT_G):
            x, _, _ = _mid_forward(q_ref, k0, tr_ref, ti_ref, fh_ref[...], fl_ref[...], half)
            x = x * sc
            for g in range(DFT_G):
                lanes = slice(g * LANES, (g + 1) * LANES)
                o_ref[0, half * DFT_G + g] = x[:FN2, lanes]
                o_ref[1, half * DFT_G + g] = x[FN2:, lanes]


def _mid_index(ph):
    return jnp.clip(ph - 1, 0, DFT_NP - 1)


def _filter_spectrum(kern_rj, twr, twi, lh, ll, fh, fl, scale):
    const = lambda c, ph: (0, 0)
    tw = pl.BlockSpec((DFT_KB, FN2), lambda c, ph: (_mid_index(ph), 0))
    return pl.pallas_call(
        _spectrum_body,
        grid=(HY_W // LANES, DFT_NP + 1),
        in_specs=[pl.BlockSpec((FN2, FN1, LANES), lambda c, ph: (0, 0, c)),
                  tw, tw,
                  pl.BlockSpec(lh.shape, const), pl.BlockSpec(ll.shape, const),
                  pl.BlockSpec(fh.shape, const), pl.BlockSpec(fl.shape, const),
                  pl.BlockSpec((1, LANES), lambda c, ph: (0, c))],
        out_specs=pl.BlockSpec((2, DFT_KB, FN2, LANES), lambda c, ph: (0, _mid_index(ph), 0, c)),
        out_shape=jax.ShapeDtypeStruct((2, FN1, FN2, HY_W), F32),
        scratch_shapes=[pltpu.VMEM((FN2, FN1, LANES), F32), pltpu.VMEM((2 * FN1, FN2, LANES), F32)],
        compiler_params=_cparams(("parallel", "arbitrary")),
        name="filter_spectrum",
    )(kern_rj, twr, twi, lh, ll, fh, fl, scale)


def _hconv_body(u_ref, x2_ref, kh_ref, tr_ref, ti_ref, lh_ref, ll_ref, fh_ref, fl_ref, gh_ref, gl_ref,
                oh_ref, ol_ref, bias_ref, o_ref, p_ref, q_ref):
    ph = pl.program_id(1)

    def both(ref, r):
        return jnp.concatenate([ref[0, r], ref[1, r]], axis=0)

    @pl.when(ph == 0)
    def _():
        _lead_phase(lambda r: both(u_ref, r), lh_ref, ll_ref, p_ref, q_ref)

    @pl.when((ph > 0) & (ph <= DFT_NP))
    def _():
        k0 = (ph - 1) * DFT_KB
        for half in range(DFT_KB // DFT_G):
            x, tr, ti = _mid_forward(q_ref, k0, tr_ref, ti_ref, fh_ref[...], fl_ref[...], half)
            xr, xi = x[:FN2], x[FN2:]
            kr = _lane_cat([kh_ref[0, half * DFT_G + g] for g in range(DFT_G)])
            ki = _lane_cat([kh_ref[1, half * DFT_G + g] for g in range(DFT_G)])
            y = jnp.concatenate([xr * kr - xi * ki, xr * ki + xi * kr], axis=0).astype(BF16)
            c = _dot2c(gh_ref[...], gl_ref[...], y)
            cr, ci = c[:FN2], c[FN2:]
            dr = cr * tr + ci * ti
            di = ci * tr - cr * ti
            for g in range(DFT_G):
                lanes = slice(g * LANES, (g + 1) * LANES)
                q_ref[k0 + half * DFT_G + g] = dr[:, lanes]
                q_ref[FN1 + k0 + half * DFT_G + g] = di[:, lanes]

    @pl.when(ph == DFT_NP + 1)
    def _():
        bias = bias_ref[...]
        p_ref[...] = jnp.swapaxes(q_ref[0:FN1], 0, 1)
        _lead_stage(lambda r: p_ref[r].astype(BF16), oh_ref[:, 0:FN1], ol_ref[:, 0:FN1], q_ref)
        p_ref[...] = jnp.swapaxes(q_ref[FN1:2 * FN1], 0, 1)
        oh2, ol2 = oh_ref[:, FN1:2 * FN1], ol_ref[:, FN1:2 * FN1]

        def group(rg, _):
            r0 = rg * DFT_G
            rhs = _lane_cat([p_ref[r0 + g].astype(BF16) for g in range(DFT_G)])
            blk = _dot2c(oh2, ol2, rhs)
            for g in range(DFT_G):
                r = r0 + g
                y = q_ref[r] + blk[:, g * LANES:(g + 1) * LANES]
                q_ref[r] = (y + both(u_ref, r).astype(F32) * bias) * both(x2_ref, r).astype(F32)
            return 0

        lax.fori_loop(0, FN2 // DFT_G, group, 0, unroll=DFT_UNROLL)
        p_ref[...] = jnp.swapaxes(q_ref[0:FN2], 0, 1)
        nj = SEQ // FN2
        for b in range(2):
            o_ref[b] = p_ref[b * nj:(b + 1) * nj].reshape(SEQ, LANES).astype(BF16)


def _hyena_conv(u_rj, x2_rj, khat, twr, twi, lead, fwd, inv, out, bias):
    const = lambda c, ph: (0, 0)
    nj = SEQ // FN2
    tw = pl.BlockSpec((DFT_KB, FN2), lambda c, ph: (_mid_index(ph), 0))
    sig = pl.BlockSpec((2, FN2, nj, LANES), lambda c, ph: (0, 0, 0, c))
    mats = [m for pair in (lead, fwd, inv, out) for m in pair]
    return pl.pallas_call(
        _hconv_body,
        grid=(HY_W // LANES, DFT_NP + 2),
        in_specs=[sig, sig,
                  pl.BlockSpec((2, DFT_KB, FN2, LANES), lambda c, ph: (0, _mid_index(ph), 0, c)),
                  tw, tw] + [pl.BlockSpec(m.shape, const) for m in mats]
                 + [pl.BlockSpec((1, LANES), lambda c, ph: (0, c))],
        out_specs=pl.BlockSpec((2, SEQ, LANES), lambda c, ph: (0, 0, c)),
        out_shape=jax.ShapeDtypeStruct((2, SEQ, HY_W), BF16),
        scratch_shapes=[pltpu.VMEM((FN2, FN1, LANES), F32), pltpu.VMEM((2 * FN1, FN2, LANES), F32)],
        compiler_params=_cparams(("parallel", "arbitrary")),
        name="hyena_conv",
    )(u_rj, x2_rj, khat, twr, twi, *mats, bias)


OP_TM = 1024


def _outproj_body(att_ref, hy_ref, x_ref, w_ref, g1_ref, n2_ref, sh_ref, sc_ref, wr_ref,
                  x1_ref, h2_ref, lg_ref):
    a = jnp.concatenate([att_ref[0], hy_ref[0]], axis=1)
    x1 = x_ref[0] + g1_ref[0] * _dot(a, w_ref[...])
    x1_ref[0] = x1
    h2 = _rms_mod(x1, n2_ref[...], sh_ref[0], sc_ref[0])
    hh, hl = _split(h2)
    h2_ref[0] = hh
    wr = wr_ref[...]
    both = _dot(wr, hh, _NT)
    lg_ref[0] = both[0:NE] + both[NE:2 * NE] + _dot(wr[0:NE], hl, _NT)


def _out_projection(att, hy, x, w_out_bf, g1r, n2g, sh2, sc2, wr2):
    b = x.shape[0]
    tok = lambda bi, i: (bi, i, 0)
    row = lambda bi, i: (bi, 0, 0)
    const = lambda bi, i: (0, 0)
    return pl.pallas_call(
        _outproj_body,
        grid=(b, SEQ // OP_TM),
        in_specs=[pl.BlockSpec((1, OP_TM, ATT_W), tok),
                  pl.BlockSpec((1, OP_TM, HY_W), tok),
                  pl.BlockSpec((1, OP_TM, D), tok),
                  pl.BlockSpec((ATT_W + HY_W, D), const),
                  pl.BlockSpec((1, 1, D), row),
                  pl.BlockSpec((1, D), const),
                  pl.BlockSpec((1, 1, D), row),
                  pl.BlockSpec((1, 1, D), row),
                  pl.BlockSpec((2 * NE, D), const)],
        out_specs=[pl.BlockSpec((1, OP_TM, D), tok),
                   pl.BlockSpec((1, OP_TM, D), tok),
                   pl.BlockSpec((1, NE, OP_TM), lambda bi, i: (bi, 0, i))],
        out_shape=[jax.ShapeDtypeStruct((b, SEQ, D), F32),
                   jax.ShapeDtypeStruct((b, SEQ, D), BF16),
                   jax.ShapeDtypeStruct((b, NE, SEQ), F32)],
        compiler_params=_cparams(("parallel", "parallel")),
        name="out_projection",
    )(att, hy, x, w_out_bf, g1r, n2g, sh2, sc2, wr2)


def _routing_body(lg_ref, tri_ref, pos_ref, gate_ref, off_ref, cs_ref):
    lg = lg_ref[0]
    e = jnp.exp(lg - jnp.max(lg, axis=0, keepdims=True))
    aff = e / jnp.sum(e, axis=0, keepdims=True)
    gate_ref[0] = aff
    def count_ge(t):
        return jnp.sum(jnp.where(aff >= t, 1.0, 0.0), axis=1, keepdims=True)

    def bisect(i, thr):
        cand = thr | (jnp.int32(1) << (30 - i))
        return jnp.where(count_ge(pltpu.bitcast(cand, F32)) >= float(CAP), cand, thr)

    thr = lax.fori_loop(0, 31, bisect, jnp.zeros((NE, 1), I32))
    lo = pltpu.bitcast(thr, F32)
    hi = jnp.maximum(pltpu.bitcast(thr + 1, F32), jnp.finfo(F32).tiny)

    def refine(i, c):
        lo, hi = c
        mid = lo + (hi - lo) * 0.5
        ok = count_ge(mid) >= float(CAP)
        return jnp.where(ok, mid, lo), jnp.where(ok, hi, mid)

    lo, hi = lax.fori_loop(0, 32, refine, (lo, hi))
    gt = aff >= hi
    eq = (aff >= lo) & jnp.logical_not(gt)
    need = float(CAP) - jnp.sum(jnp.where(gt, 1.0, 0.0), axis=1, keepdims=True)
    tri = tri_ref[...]

    def excl_cumsum(mask_f, record_offsets):
        carry = jnp.zeros((NE, 1), F32)
        for c in range(NTCH):
            sl = slice(c * TCH, (c + 1) * TCH)
            m = mask_f[:, sl]
            inc = _dot(m.astype(BF16), tri)
            cs_ref[:, sl] = inc - m + carry
            if record_offsets:
                off_ref[0, :, c:c + 1] = carry.astype(I32)
            carry = carry + inc[:, TCH - 1:TCH]
        return cs_ref[...]

    eq_rank = excl_cumsum(jnp.where(eq, 1.0, 0.0), False)
    sel = gt | (eq & (eq_rank < need))
    pos = excl_cumsum(jnp.where(sel, 1.0, 0.0), True)
    pos_ref[0] = jnp.where(sel, pos.astype(I32), -1)


def _routing(logits, tri):
    b = logits.shape[0]
    blk = pl.BlockSpec((1, NE, SEQ), lambda bi: (bi, 0, 0))
    return pl.pallas_call(
        _routing_body,
        grid=(b,),
        in_specs=[blk, pl.BlockSpec((TCH, TCH), lambda bi: (0, 0))],
        out_specs=[blk, blk, pl.BlockSpec((1, NE, NTCH), lambda bi: (bi, 0, 0))],
        out_shape=[jax.ShapeDtypeStruct((b, NE, SEQ), I32),
                   jax.ShapeDtypeStruct((b, NE, SEQ), F32),
                   jax.ShapeDtypeStruct((b, NE, NTCH), I32)],
        scratch_shapes=[pltpu.VMEM((NE, SEQ), F32)],
        compiler_params=_cparams(("parallel",)),
        name="routing",
    )(logits, tri)


GATHER_UNROLL = 8


GW_SMALL = 64


def _gather_body(off_ref, h_ref, pos_ref, xg_ref):
    b = pl.program_id(0)
    e = pl.program_id(1)
    row0 = (b * NE + e) * NTCH
    xg_ref[0, 0] = jnp.zeros((YROWS, D), BF16)

    def count(c2, most):
        nxt = jnp.where(c2 + 1 < NTCH // 2, off_ref[row0 + jnp.minimum(2 * c2 + 2, NTCH - 1)], CAP)
        return jnp.maximum(most, nxt - off_ref[row0 + 2 * c2])

    most = lax.fori_loop(0, NTCH // 2, count, 0)

    def sweep(window, span):
        crow = lax.broadcasted_iota(I32, (window, span * TCH), 0)

        def chunks(i, _):
            for j in range(GATHER_UNROLL):
                c = (i * GATHER_UNROLL + j) * span
                off = off_ref[row0 + c]
                base = pl.multiple_of(jnp.minimum((off >> 4) << 4, YROWS - window), BF16_ROWS)
                t0 = pl.multiple_of(c * TCH, TCH)
                rel = _lane_cat([pos_ref[0, 0, pl.ds(c + s, 1), :] for s in range(span)]) - base
                onehot = jnp.where(crow == rel, 1.0, 0.0).astype(BF16)
                rows = _dot(onehot, h_ref[0, pl.ds(t0, span * TCH), :]).astype(BF16)
                xg_ref[0, 0, pl.ds(base, window), :] += rows
            return 0

        lax.fori_loop(0, NTCH // (GATHER_UNROLL * span), chunks, 0)

    @pl.when(most <= GW_SMALL - BF16_ROWS)
    def _():
        sweep(GW_SMALL, 2)

    @pl.when(most > GW_SMALL - BF16_ROWS)
    def _():
        sweep(CW, 1)


def _gather(offs_flat, h2, pos4):
    b = h2.shape[0]
    grid_spec = pltpu.PrefetchScalarGridSpec(
        num_scalar_prefetch=1,
        grid=(b, NE),
        in_specs=[pl.BlockSpec((1, SEQ, D), lambda bi, e, off: (bi, 0, 0)),
                  pl.BlockSpec((1, 1, NTCH, TCH), lambda bi, e, off: (bi, e, 0, 0))],
        out_specs=pl.BlockSpec((1, 1, YROWS, D), lambda bi, e, off: (bi, e, 0, 0)),
    )
    return pl.pallas_call(
        _gather_body,
        grid_spec=grid_spec,
        out_shape=jax.ShapeDtypeStruct((b, NE, YROWS, D), BF16),
        compiler_params=_cparams(("parallel", "arbitrary")),
        name="moe_gather",
    )(offs_flat, h2, pos4)


FFN_TM = 512
FFN_NF = 4
FFN_FC = DEXP // FFN_NF
assert FFN_FC * FFN_NF == DEXP and FFN_FC % BF16_ROWS == 0


def _ffn_body(xg_ref, wgt_ref, wut_ref, wd_ref, y_ref, acc_ref):
    j = pl.program_id(1)
    nb = xg_ref.shape[0]

    @pl.when(j == 0)
    def _():
        acc_ref[...] = jnp.zeros_like(acc_ref)

    wgt = wgt_ref[0].astype(BF16)
    wut = wut_ref[0].astype(BF16)
    wd = wd_ref[0].astype(BF16)
    for b in range(nb):
        for mb in range(CAP // FFN_TM):
            rows = slice(mb * FFN_TM, (mb + 1) * FFN_TM)
            xb = xg_ref[b, 0, rows, :]
            a = _dot(xb, wgt, _NT)
            u = _dot(xb, wut, _NT)
            h = (a * (1.0 / (1.0 + jnp.exp(-a))) * u).astype(BF16)
            acc_ref[b, rows, :] += _dot(h, wd)

    @pl.when(j == FFN_NF - 1)
    def _():
        for b in range(nb):
            y_ref[b, 0, 0:CAP, :] = acc_ref[b].astype(BF16)
            y_ref[b, 0, CAP:YROWS, :] = jnp.zeros((YROWS - CAP, D), BF16)


def _expert_ffn(xg, w_gate_t, w_up_t, w_down):
    b = xg.shape[0]
    wblk = pl.BlockSpec((1, FFN_FC, D), lambda e, j: (e, j, 0))
    return pl.pallas_call(
        _ffn_body,
        grid=(NE, FFN_NF),
        in_specs=[pl.BlockSpec((b, 1, CAP, D), lambda e, j: (0, e, 0, 0)), wblk, wblk, wblk],
        out_specs=pl.BlockSpec((b, 1, YROWS, D), lambda e, j: (0, e, 0, 0)),
        out_shape=jax.ShapeDtypeStruct((b, NE, YROWS, D), BF16),
        scratch_shapes=[pltpu.VMEM((b, CAP, D), F32)],
        compiler_params=_cparams(("parallel", "arbitrary")),
        name="expert_ffn",
    )(xg, w_gate_t, w_up_t, w_down)


CW_SMALL = 64
CW_STACK = 2 * LANES // CW_SMALL


CMB_T = 4


def _combine_body(off_ref, y_ref, pos_ref, gate_ref, x1_ref, g2_ref, o_ref):
    b = pl.program_id(0)
    i = pl.program_id(1)

    def offset(e, c):
        return off_ref[(b * NE + e) * NTCH + c]

    def window(e, t, rows):
        lanes = slice(t * TCH, (t + 1) * TCH)
        base = pl.multiple_of(jnp.minimum((offset(e, i * CMB_T + t) >> 4) << 4, YROWS - rows), BF16_ROWS)
        rel = pos_ref[0, e:e + 1, lanes] - base
        crow = lax.broadcasted_iota(I32, (rows, TCH), 0)
        w = jnp.where(crow == rel, gate_ref[0, e:e + 1, lanes], 0.0).astype(BF16)
        return w, y_ref[0, e, pl.ds(base, rows), :]

    most = jnp.int32(0)
    for t in range(CMB_T):
        c = i * CMB_T + t
        for e in range(NE):
            nxt = jnp.where(c + 1 < NTCH, offset(e, jnp.minimum(c + 1, NTCH - 1)), CAP)
            most = jnp.maximum(most, nxt - offset(e, c))

    def finish(t, acc):
        rows = slice(t * TCH, (t + 1) * TCH)
        o_ref[0, rows, :] = x1_ref[0, rows, :] + g2_ref[0] * acc

    @pl.when(most <= CW_SMALL - BF16_ROWS)
    def _():
        for t in range(CMB_T):
            acc = jnp.zeros((TCH, D), F32)
            for e0 in range(0, NE, CW_STACK):
                ws, ys = zip(*[window(e, t, CW_SMALL) for e in range(e0, e0 + CW_STACK)])
                acc = acc + _dot(jnp.concatenate(ws, axis=0), jnp.concatenate(ys, axis=0), _TN)
            finish(t, acc)

    @pl.when(most > CW_SMALL - BF16_ROWS)
    def _():
        for t in range(CMB_T):
            acc = jnp.zeros((TCH, D), F32)
            for e in range(NE):
                w, yw = window(e, t, CW)
                acc = acc + _dot(w, yw, _TN)
            finish(t, acc)


def _combine(offs_flat, y, pos, gate, x1, g2r):
    b = x1.shape[0]
    grid_spec = pltpu.PrefetchScalarGridSpec(
        num_scalar_prefetch=1,
        grid=(b, NTCH // CMB_T),
        in_specs=[pl.BlockSpec((1, NE, YROWS, D), lambda bi, i, off: (bi, 0, 0, 0),
                               pipeline_mode=pl.Buffered(1)),
                  pl.BlockSpec((1, NE, CMB_T * TCH), lambda bi, i, off: (bi, 0, i)),
                  pl.BlockSpec((1, NE, CMB_T * TCH), lambda bi, i, off: (bi, 0, i)),
                  pl.BlockSpec((1, CMB_T * TCH, D), lambda bi, i, off: (bi, i, 0)),
                  pl.BlockSpec((1, 1, D), lambda bi, i, off: (bi, 0, 0))],
        out_specs=pl.BlockSpec((1, CMB_T * TCH, D), lambda bi, i, off: (bi, i, 0)),
    )
    return pl.pallas_call(
        _combine_body,
        grid_spec=grid_spec,
        out_shape=jax.ShapeDtypeStruct((b, SEQ, D), F32),
        compiler_params=_cparams(("parallel", "arbitrary")),
        name="moe_combine",
    )(offs_flat, y, pos, gate, x1, g2r)


def _np_split(m):
    hi = np.asarray(m, np.float64).astype(BF16)
    lo = (m - hi.astype(np.float64)).astype(BF16)
    return jnp.asarray(hi), jnp.asarray(lo)


@functools.lru_cache(maxsize=None)
def _dft_tables():
    a = np.arange(FN1, dtype=np.float64)
    ang = 2.0 * np.pi * np.outer(a, a) / FN1
    fr, fi = np.cos(ang), -np.sin(ang)
    half = SEQ // FN2
    lead_u = np.block([[fr[:, :half], -fi[:, :half]], [fi[:, :half], fr[:, :half]]])
    lead_k = np.concatenate([fr, fi], axis=0)
    fwd = np.block([[fr, -fi], [fi, fr]])
    inv = np.block([[fr, fi], [-fi, fr]])
    out = np.block([[fr[:half], fi[:half]], [-fi[:half], fr[:half]]])
    n2 = np.arange(FN2, dtype=np.float64)
    tw = 2.0 * np.pi * np.outer(a, n2) / FN
    twr = np.cos(tw).astype(np.float32)
    twi = (-np.sin(tw)).astype(np.float32)
    return dict(lead_u=lead_u, lead_k=lead_k, fwd=fwd, inv=inv, out=out, twr=twr, twi=twi)


@functools.lru_cache(maxsize=None)
def _filter_tables():
    L = SEQ
    n = np.arange(FN).reshape(FN1, FN2).T.reshape(-1)
    lag = np.where(n < L, n, FN - n)
    jc = np.minimum(lag, L - 1).astype(np.float64)
    t = (jc / (L - 1))[:, None]
    bands = (FEMB - 1) // 2
    w = 2.0 * np.pi * jc / L
    f = np.linspace(1e-4, bands - 1, bands)
    fw = w[:, None] * f[None, :]
    z = np.concatenate([t, np.cos(fw), -np.sin(fw), np.zeros((FN, FORD - FEMB))], axis=-1)
    mask = np.where(n == L, 0.0, 1.0)[:, None]
    fwd = np.where(n < L, 1.0, 0.0)[:, None]
    max_decay = math.log(DECAY_TARGET) / FAST_DECAY_PCT
    min_decay = math.log(DECAY_TARGET) / SLOW_DECAY_PCT
    negdelta = -np.abs(np.linspace(min_decay, max_decay, HY_W))[None, :]
    z = z.reshape(FN // FILT_TR, 2, FILT_TR // 2, FORD).transpose(0, 2, 1, 3).reshape(FN // 2, 2 * FORD)
    return tuple(np.asarray(a, np.float32) for a in (z, t, mask, fwd, negdelta))


@functools.lru_cache(maxsize=None)
def _rope_tables(n):
    rows = n // GRID_W
    row_id, col_id = np.meshgrid(np.arange(rows, dtype=np.float64), np.arange(GRID_W, dtype=np.float64), indexing="ij")
    quarter = HD // 4
    inv_freq = ROPE_THETA ** (-np.arange(quarter, dtype=np.float64) / quarter)
    ar = row_id.reshape(-1)[:, None] * inv_freq
    ac = col_id.reshape(-1)[:, None] * inv_freq
    cos = np.concatenate([np.cos(ar), np.cos(ar), np.cos(ac), np.cos(ac)], axis=-1)
    sin = np.concatenate([-np.sin(ar), np.sin(ar), -np.sin(ac), np.sin(ac)], axis=-1)
    reps = (1, LANES // HD)
    return np.tile(cos, reps).astype(np.float32), np.tile(sin, reps).astype(np.float32)


def _hyena_long_conv(u_rj, x2_rj, kern, abs_sum, bias):
    tb = _dft_tables()
    twr, twi = tb["twr"], tb["twi"]
    fwd = _np_split(tb["fwd"])
    scale = 1.0 / (abs_sum * float(FN))
    khat = _filter_spectrum(kern.reshape(FN2, FN1, HY_W), twr, twi, *_np_split(tb["lead_k"]), *fwd, scale)
    return _hyena_conv(u_rj, x2_rj, khat, twr, twi, _np_split(tb["lead_u"]), fwd, _np_split(tb["inv"]),
                       _np_split(tb["out"]), bias.reshape(1, HY_W))


def kernel(x, c, ctx, c_ctx, w_mod, b_mod, norm1_g, norm2_g, w_in, w_out, q_norm_g, k_norm_g,
           conv_w, conv_b, filt_w1, filt_b1, filt_w2, filt_b2, filt_w3, filt_freq, hyena_bias,
           w_router, w_gate, w_up, w_down):
    B = x.shape[0]
    assert x.shape == (B, SEQ, D) and B == 2 and ctx.shape == (B, CTX, D) and w_mod.shape[0] == 1
    l = 0

    cc = jnp.concatenate([c, c_ctx[None, :], jnp.zeros((SUBLANES - B - 1, D), F32)], axis=0)
    mod = _modulation(cc, w_mod[l], b_mod[l][None, :])
    sh1, sc1, g1, sh2, sc2, g2 = [mod[:, i * D:(i + 1) * D] for i in range(6)]
    lat = lambda m: m[:B, None, :]
    ctxrow = lambda m: jnp.broadcast_to(m[B:B + 1, None, :], (B, 1, D))

    w_in_bf = w_in[l].astype(BF16)
    gq2 = jnp.tile(q_norm_g[l][None, :], (1, LANES // HD))
    gk2 = jnp.tile(k_norm_g[l][None, :], (1, LANES // HD))
    bd = jnp.asarray(np.kron(np.eye(2 * LANES // HD), np.full((HD, HD), 1.0 / HD)), BF16)
    cos_t, sin_t = _rope_tables(SEQ)
    n1g = norm1_g[l][None, :]

    q, k, vt, p = _in_projection(x, n1g, lat(sh1), lat(sc1), w_in_bf, gq2, gk2, bd, cos_t, sin_t, 512)
    _, kc, vct, _ = _in_projection(ctx, n1g, ctxrow(sh1), ctxrow(sc1), w_in_bf, gq2, gk2, bd,
                                   jnp.ones((CTX, LANES), F32), jnp.zeros((CTX, LANES), F32), CTX)

    kch = jnp.concatenate([k, kc], axis=2).reshape(B, NKV, ATT_NCH, ATT_TK, HD)
    vt_all = jnp.concatenate([vt, vct], axis=3).reshape(B, NKV, HD, ATT_NCH, ATT_TK)
    ones_pad = jnp.concatenate([jnp.ones((B, NKV, ATT_NCH, 1, ATT_TK), BF16),
                                jnp.zeros((B, NKV, ATT_NCH, BF16_ROWS - 1, ATT_TK), BF16)], axis=3)
    vtch = jnp.concatenate([vt_all.transpose(0, 1, 3, 2, 4), ones_pad], axis=3)
    bound = (1.02 * HD * Q_SCALE) * jnp.max(jnp.abs(q_norm_g[l])) * jnp.max(jnp.abs(k_norm_g[l]))
    att = _attention(bound.reshape(1).astype(F32), q, kch, vtch)

    cw9 = conv_w[l].reshape(3, 3, HY_W).reshape(9, HY_W)
    cb3 = conv_b[l].reshape(3, HY_W)
    u_rj, x2_rj = _short_conv(p, cw9, cb3)
    ztab, ttab, mtab, ftab, negdelta = _filter_tables()
    w1p = jnp.concatenate([filt_w1[l], jnp.zeros((FORD - FEMB, FORD), F32)], axis=0)
    twice = lambda w: jnp.kron(jnp.eye(2, dtype=F32), w)
    pair = lambda v: jnp.tile(v[None, :], (1, 2))
    kern, abs_sum = _implicit_filter(ztab, ttab, mtab, ftab, twice(w1p), pair(filt_b1[l]), twice(filt_w2[l]),
                                     pair(filt_b2[l]), twice(filt_w3[l]), pair(filt_freq[l]), negdelta)
    hy = _hyena_long_conv(u_rj, x2_rj, kern, abs_sum, hyena_bias[l])

    wr2 = jnp.concatenate(_split(w_router[l].T), axis=0)
    x1, h2, logits = _out_projection(att, hy, x, w_out[l].astype(BF16), lat(g1), norm2_g[l][None, :],
                                     lat(sh2), lat(sc2), wr2)

    tri = jnp.asarray(np.triu(np.ones((TCH, TCH))), BF16)
    pos, gate, offs = _routing(logits, tri)
    offs_flat = offs.reshape(-1)
    xg = _gather(offs_flat, h2, pos.reshape(B, NE, NTCH, TCH))
    y = _expert_ffn(xg, jnp.swapaxes(w_gate[l], 1, 2), jnp.swapaxes(w_up[l], 1, 2), w_down[l])
    return _combine(offs_flat, y, pos, gate, x1, lat(g2))
```
